```python
import math
import jax, jax.numpy as jnp
from jax import lax
import numpy as np

D_MODEL = 1024
BATCH = 16
SEQ = 2048
DEPTH = 1

BLOCK = 128
EPS = 1e-6
HEAD_DIM = 64
SWA_Q_HEADS = 8
SWA_KV_HEADS = 2
SWA_WINDOW = 128
N_BUCKETS = 32
MAX_DISTANCE = 128
SB_HEADS = 8
MEM_LEN = 256
MEM_HEADS = 4
MEM_HEAD_DIM = 128
SWA_Q_W = SWA_Q_HEADS * HEAD_DIM
SWA_KV_W = SWA_KV_HEADS * HEAD_DIM
SB_W = SB_HEADS * HEAD_DIM
MEM_W = MEM_HEADS * MEM_HEAD_DIM
N_BRANCH = 3
IN_SPLITS = (SWA_Q_W, SWA_KV_W, SWA_KV_W, SB_W, SB_W, SB_W, MEM_W, N_BRANCH * D_MODEL)
IN_W = sum(IN_SPLITS)
D_FF = -(-8 * D_MODEL // (3 * 256)) * 256

kernel_name = "hybrid_gated_swa_stickbreak_memxattn_swiglu"


def rms_norm(x, g):
    xf = x.astype(jnp.float32)
    y = xf * lax.rsqrt(jnp.mean(xf * xf, axis=-1, keepdims=True) + EPS)
    return (y * g.astype(jnp.float32)).astype(x.dtype)


def t5_bucket(dist):
    max_exact = N_BUCKETS // 2
    d = jnp.maximum(dist, 0)
    df = jnp.maximum(d, 1).astype(jnp.float32)
    large = max_exact + (jnp.log(df / max_exact) / math.log(MAX_DISTANCE / max_exact)
                         * (N_BUCKETS - max_exact)).astype(jnp.int32)
    large = jnp.minimum(large, N_BUCKETS - 1)
    return jnp.where(d < max_exact, d, large)


def swa_sink_attention(q, k, v, sinks, rel_bias):
    B, S, Hq, d = q.shape
    Hkv = k.shape[2]
    G = Hq // Hkv
    nb = S // BLOCK
    qb = q.reshape(B, nb, BLOCK, Hkv, G, d)
    kb = k.reshape(B, nb, BLOCK, Hkv, d)
    vb = v.reshape(B, nb, BLOCK, Hkv, d)
    kband = jnp.concatenate([jnp.concatenate([jnp.zeros_like(kb[:, :1]), kb[:, :-1]], axis=1), kb], axis=2)
    vband = jnp.concatenate([jnp.concatenate([jnp.zeros_like(vb[:, :1]), vb[:, :-1]], axis=1), vb], axis=2)
    scores = jnp.einsum('bnqhgd,bnkhd->bnhgqk', qb, kband).astype(jnp.float32) * (d ** -0.5)
    dist = (jnp.arange(BLOCK)[:, None] + BLOCK) - jnp.arange(2 * BLOCK)[None, :]
    in_win = (dist >= 0) & (dist < SWA_WINDOW)
    bias = rel_bias.astype(jnp.float32)[t5_bucket(dist)]
    bias = bias.transpose(2, 0, 1).reshape(Hkv, G, BLOCK, 2 * BLOCK)
    k_abs = (jnp.arange(nb)[:, None] - 1) * BLOCK + jnp.arange(2 * BLOCK)[None, :]
    mask = in_win[None] & (k_abs >= 0)[:, None, :]
    scores = jnp.where(mask[None, :, None, None], scores + bias[None, None], -jnp.inf)
    sink = sinks.astype(jnp.float32).reshape(Hkv, G)[:, :, None, None]
    m = jnp.maximum(jnp.max(scores, axis=-1, keepdims=True), sink)
    p = jnp.exp(scores - m)
    w = p / (jnp.sum(p, axis=-1, keepdims=True) + jnp.exp(sink - m))
    out = jnp.einsum('bnhgqk,bnkhd->bnqhgd', w.astype(v.dtype), vband)
    return out.reshape(B, S, Hq * d)


def stick_breaking_attention(q, k, v):
    B, S, H, d = q.shape
    nb = S // BLOCK
    outs = []
    for i in range(nb):
        L = (i + 1) * BLOCK
        z = jnp.einsum('bqhd,bkhd->bhqk', q[:, i * BLOCK:L], k[:, :L]).astype(jnp.float32) * (d ** -0.5)
        t = i * BLOCK + jnp.arange(BLOCK)[:, None]
        causal = jnp.arange(L)[None, :] < t
        log_1m = jnp.where(causal, jax.nn.log_sigmoid(-z), 0.0)
        between = lax.cumsum(log_1m, axis=3, reverse=True) - log_1m
        a = jnp.where(causal, jnp.exp(jax.nn.log_sigmoid(z) + between), 0.0)
        outs.append(jnp.einsum('bhqk,bkhd->bqhd', a.astype(v.dtype), v[:, :L]))
    return jnp.concatenate(outs, axis=1).reshape(B, S, H * d)


def memory_cross_attention(q, mk, mv):
    B, S, H, d = q.shape
    z = jnp.einsum('bshd,bmhd->bhsm', q, mk).astype(jnp.float32) * (d ** -0.5)
    w = jax.nn.softmax(z, axis=-1)
    return jnp.einsum('bhsm,bmhd->bshd', w.astype(mv.dtype), mv).reshape(B, S, H * d)


def _fwd_setup_inputs(seed: int = 0) -> dict:
    key = jax.random.key(seed)
    ks = jax.random.split(key, 20)
    f = jnp.float32

    def w(k, shape, fan_in):
        return jax.random.normal(k, shape, f) * fan_in ** -0.5

    def gain(k, n):
        return 1.0 + 0.01 * jax.random.normal(k, (DEPTH, n), f)

    return {
        "x": jax.random.normal(ks[0], (BATCH, SEQ, D_MODEL), f),
        "mem": jax.random.normal(ks[1], (BATCH, MEM_LEN, D_MODEL), f),
        "ln_mix_pre": gain(ks[2], D_MODEL),
        "ln_mix_post": gain(ks[3], D_MODEL),
        "w_in": w(ks[4], (DEPTH, D_MODEL, IN_W), D_MODEL),
        "swa_sinks": 0.5 * jax.random.normal(ks[5], (DEPTH, SWA_Q_HEADS), f),
        "rel_bias": 0.5 * jax.random.normal(ks[6], (N_BUCKETS, SWA_Q_HEADS), f),
        "ln_mem": gain(ks[7], D_MODEL),
        "w_mem_kv": w(ks[8], (DEPTH, D_MODEL, 2 * MEM_W), D_MODEL),
        "w_branch_swa": w(ks[9], (DEPTH, SWA_Q_W, D_MODEL), SWA_Q_W),
        "w_branch_sb": w(ks[10], (DEPTH, SB_W, D_MODEL), SB_W),
        "w_branch_mem": w(ks[11], (DEPTH, MEM_W, D_MODEL), MEM_W),
        "w_out": w(ks[12], (DEPTH, D_MODEL, D_MODEL), D_MODEL),
        "ln_ffn_pre": gain(ks[13], D_MODEL),
        "ln_ffn_post": gain(ks[14], D_MODEL),
        "w_gate": w(ks[15], (DEPTH, D_MODEL, D_FF), D_MODEL),
        "w_up": w(ks[16], (DEPTH, D_MODEL, D_FF), D_MODEL),
        "w_down": w(ks[17], (DEPTH, D_FF, D_MODEL), D_FF),
    }


def _fwd_reference(x, mem, ln_mix_pre, ln_mix_post, w_in, swa_sinks, rel_bias, ln_mem, w_mem_kv,
              w_branch_swa, w_branch_sb, w_branch_mem, w_out, ln_ffn_pre, ln_ffn_post,
              w_gate, w_up, w_down):
    B, S, D = x.shape
    M = mem.shape[1]
    split_idx = list(np.cumsum(IN_SPLITS)[:-1])
    h = x
    for l in range(DEPTH):
        u = rms_norm(h, ln_mix_pre[l])
        proj = jnp.einsum('bsd,de->bse', u, w_in[l])
        qa, ka, va, qb, kb, vb, qm, gl = jnp.split(proj, split_idx, axis=-1)
        y_swa = swa_sink_attention(qa.reshape(B, S, SWA_Q_HEADS, HEAD_DIM),
                                   ka.reshape(B, S, SWA_KV_HEADS, HEAD_DIM),
                                   va.reshape(B, S, SWA_KV_HEADS, HEAD_DIM),
                                   swa_sinks[l], rel_bias)
        y_sb = stick_breaking_attention(qb.reshape(B, S, SB_HEADS, HEAD_DIM),
                                        kb.reshape(B, S, SB_HEADS, HEAD_DIM),
                                        vb.reshape(B, S, SB_HEADS, HEAD_DIM))
        mkv = jnp.einsum('bmd,de->bme', rms_norm(mem, ln_mem[l]), w_mem_kv[l])
        mk, mv = jnp.split(mkv, 2, axis=-1)
        y_mem = memory_cross_attention(qm.reshape(B, S, MEM_HEADS, MEM_HEAD_DIM),
                                       mk.reshape(B, M, MEM_HEADS, MEM_HEAD_DIM),
                                       mv.reshape(B, M, MEM_HEADS, MEM_HEAD_DIM))
        g = jax.nn.sigmoid(gl.reshape(B, S, N_BRANCH, D))
        merged = (g[:, :, 0] * jnp.einsum('bse,ed->bsd', y_swa, w_branch_swa[l])
                  + g[:, :, 1] * jnp.einsum('bse,ed->bsd', y_sb, w_branch_sb[l])
                  + g[:, :, 2] * jnp.einsum('bse,ed->bsd', y_mem, w_branch_mem[l]))
        mix = jnp.einsum('bsd,de->bse', merged, w_out[l])
        h = h + rms_norm(mix, ln_mix_post[l])
        u = rms_norm(h, ln_ffn_pre[l])
        a = jax.nn.silu(jnp.einsum('bsd,df->bsf', u, w_gate[l])) * jnp.einsum('bsd,df->bsf', u, w_up[l])
        ffn = jnp.einsum('bsf,fd->bsd', a, w_down[l])
        h = h + rms_norm(ffn, ln_ffn_post[l])
    return h


import jax as _jax
import jax.numpy as _jnp

TWIN_FORMAT = 'train_step'
FWD_PARAMS = ['x', 'mem', 'ln_mix_pre', 'ln_mix_post', 'w_in', 'swa_sinks', 'rel_bias', 'ln_mem', 'w_mem_kv', 'w_branch_swa', 'w_branch_sb', 'w_branch_mem', 'w_out', 'ln_ffn_pre', 'ln_ffn_post', 'w_gate', 'w_up', 'w_down']
TWIN_WEIGHTS = ['ln_mix_pre', 'ln_mix_post', 'w_in', 'swa_sinks', 'rel_bias', 'ln_mem', 'w_mem_kv', 'w_branch_swa', 'w_branch_sb', 'w_branch_mem', 'w_out', 'ln_ffn_pre', 'ln_ffn_post', 'w_gate', 'w_up', 'w_down']
TWIN_DIFF_INPUT = 'x'
TWIN_INPUTS = ['x', 'mem', 'ln_mix_pre', 'ln_mix_post', 'w_in', 'swa_sinks', 'rel_bias', 'ln_mem', 'w_mem_kv', 'w_branch_swa', 'w_branch_sb', 'w_branch_mem', 'w_out', 'ln_ffn_pre', 'ln_ffn_post', 'w_gate', 'w_up', 'w_down', 'loss_target', 'm_ln_mix_pre', 'm_ln_mix_post', 'm_w_in', 'm_swa_sinks', 'm_rel_bias', 'm_ln_mem', 'm_w_mem_kv', 'm_w_branch_swa', 'm_w_branch_sb', 'm_w_branch_mem', 'm_w_out', 'm_ln_ffn_pre', 'm_ln_ffn_post', 'm_w_gate', 'm_w_up', 'm_w_down', 'v_ln_mix_pre', 'v_ln_mix_post', 'v_w_in', 'v_swa_sinks', 'v_rel_bias', 'v_ln_mem', 'v_w_mem_kv', 'v_w_branch_swa', 'v_w_branch_sb', 'v_w_branch_mem', 'v_w_out', 'v_ln_ffn_pre', 'v_ln_ffn_post', 'v_w_gate', 'v_w_up', 'v_w_down']
TWIN_OUTPUTS = ['loss', 'grad_x', 'grad_ln_mix_pre', 'grad_ln_mix_post', 'grad_w_in', 'grad_swa_sinks', 'grad_rel_bias', 'grad_ln_mem', 'grad_w_mem_kv', 'grad_w_branch_swa', 'grad_w_branch_sb', 'grad_w_branch_mem', 'grad_w_out', 'grad_ln_ffn_pre', 'grad_ln_ffn_post', 'grad_w_gate', 'grad_w_up', 'grad_w_down', 'delta_ln_mix_pre', 'delta_ln_mix_post', 'delta_w_in', 'delta_swa_sinks', 'delta_rel_bias', 'delta_ln_mem', 'delta_w_mem_kv', 'delta_w_branch_swa', 'delta_w_branch_sb', 'delta_w_branch_mem', 'delta_w_out', 'delta_ln_ffn_pre', 'delta_ln_ffn_post', 'delta_w_gate', 'delta_w_up', 'delta_w_down', 'new_m_ln_mix_pre', 'new_m_ln_mix_post', 'new_m_w_in', 'new_m_swa_sinks', 'new_m_rel_bias', 'new_m_ln_mem', 'new_m_w_mem_kv', 'new_m_w_branch_swa', 'new_m_w_branch_sb', 'new_m_w_branch_mem', 'new_m_w_out', 'new_m_ln_ffn_pre', 'new_m_ln_ffn_post', 'new_m_w_gate', 'new_m_w_up', 'new_m_w_down', 'new_v_ln_mix_pre', 'new_v_ln_mix_post', 'new_v_w_in', 'new_v_swa_sinks', 'new_v_rel_bias', 'new_v_ln_mem', 'new_v_w_mem_kv', 'new_v_w_branch_swa', 'new_v_w_branch_sb', 'new_v_w_branch_mem', 'new_v_w_out', 'new_v_ln_ffn_pre', 'new_v_ln_ffn_post', 'new_v_w_gate', 'new_v_w_up', 'new_v_w_down']
TWIN_LEAF_KINDS = {'loss': 'loss', 'grad_x': 'grad_x', 'grad_ln_mix_pre': 'grad_w', 'grad_ln_mix_post': 'grad_w', 'grad_w_in': 'grad_w', 'grad_swa_sinks': 'grad_w', 'grad_rel_bias': 'grad_w', 'grad_ln_mem': 'grad_w', 'grad_w_mem_kv': 'grad_w', 'grad_w_branch_swa': 'grad_w', 'grad_w_branch_sb': 'grad_w', 'grad_w_branch_mem': 'grad_w', 'grad_w_out': 'grad_w', 'grad_ln_ffn_pre': 'grad_w', 'grad_ln_ffn_post': 'grad_w', 'grad_w_gate': 'grad_w', 'grad_w_up': 'grad_w', 'grad_w_down': 'grad_w', 'delta_ln_mix_pre': 'delta_w', 'delta_ln_mix_post': 'delta_w', 'delta_w_in': 'delta_w', 'delta_swa_sinks': 'delta_w', 'delta_rel_bias': 'delta_w', 'delta_ln_mem': 'delta_w', 'delta_w_mem_kv': 'delta_w', 'delta_w_branch_swa': 'delta_w', 'delta_w_branch_sb': 'delta_w', 'delta_w_branch_mem': 'delta_w', 'delta_w_out': 'delta_w', 'delta_ln_ffn_pre': 'delta_w', 'delta_ln_ffn_post': 'delta_w', 'delta_w_gate': 'delta_w', 'delta_w_up': 'delta_w', 'delta_w_down': 'delta_w', 'new_m_ln_mix_pre': 'new_m', 'new_m_ln_mix_post': 'new_m', 'new_m_w_in': 'new_m', 'new_m_swa_sinks': 'new_m', 'new_m_rel_bias': 'new_m', 'new_m_ln_mem': 'new_m', 'new_m_w_mem_kv': 'new_m', 'new_m_w_branch_swa': 'new_m', 'new_m_w_branch_sb': 'new_m', 'new_m_w_branch_mem': 'new_m', 'new_m_w_out': 'new_m', 'new_m_ln_ffn_pre': 'new_m', 'new_m_ln_ffn_post': 'new_m', 'new_m_w_gate': 'new_m', 'new_m_w_up': 'new_m', 'new_m_w_down': 'new_m', 'new_v_ln_mix_pre': 'new_v', 'new_v_ln_mix_post': 'new_v', 'new_v_w_in': 'new_v', 'new_v_swa_sinks': 'new_v', 'new_v_rel_bias': 'new_v', 'new_v_ln_mem': 'new_v', 'new_v_w_mem_kv': 'new_v', 'new_v_w_branch_swa': 'new_v', 'new_v_w_branch_sb': 'new_v', 'new_v_w_branch_mem': 'new_v', 'new_v_w_out': 'new_v', 'new_v_ln_ffn_pre': 'new_v', 'new_v_ln_ffn_post': 'new_v', 'new_v_w_gate': 'new_v', 'new_v_w_up': 'new_v', 'new_v_w_down': 'new_v'}


def _forward(args):
    return _fwd_reference(*[args[k] for k in FWD_PARAMS])


def _output_shape():
    out = _jax.eval_shape(lambda: _forward(_fwd_setup_inputs(0)))
    return out.shape, out.dtype

N_MICROBATCH = 1
ADAM_LR = 0.001
ADAM_B1 = 0.9
ADAM_B2 = 0.999
ADAM_EPS = 1e-08
ADAM_WD = 0.01
ADAM_STEP = 10
PER_EXAMPLE_BATCH_AXIS = {'x': 0, 'mem': 0, 'loss_target': 0}
SHARED_INPUTS = []
_WEIGHT_DTYPES = {'ln_mix_pre': _jnp.float32, 'ln_mix_post': _jnp.float32, 'w_in': _jnp.float32, 'swa_sinks': _jnp.float32, 'rel_bias': _jnp.float32, 'ln_mem': _jnp.float32, 'w_mem_kv': _jnp.float32, 'w_branch_swa': _jnp.float32, 'w_branch_sb': _jnp.float32, 'w_branch_mem': _jnp.float32, 'w_out': _jnp.float32, 'ln_ffn_pre': _jnp.float32, 'ln_ffn_post': _jnp.float32, 'w_gate': _jnp.float32, 'w_up': _jnp.float32, 'w_down': _jnp.float32}
MOMENT_SCALE = {'ln_mix_pre': 6.324921e-01, 'ln_mix_post': 3.198351e+01, 'w_in': 2.414531e-01, 'swa_sinks': 1.673360e-01, 'rel_bias': 2.457498e-01, 'ln_mem': 1.323118e-01, 'w_mem_kv': 1.220311e-01, 'w_branch_swa': 1.412675e-01, 'w_branch_sb': 4.987308e-01, 'w_branch_mem': 9.254680e-02, 'w_out': 5.443275e-01, 'ln_ffn_pre': 5.666674e-01, 'ln_ffn_post': 3.200083e+01, 'w_gate': 1.949391e-01, 'w_up': 2.897393e-01, 'w_down': 4.799860e-01}


def _to_microbatches(a, axis):
    t = _jnp.moveaxis(a, axis, 0)
    t = t.reshape((N_MICROBATCH, t.shape[0] // N_MICROBATCH) + t.shape[1:])
    return _jnp.moveaxis(t, 1, axis + 1)


def setup_inputs(seed: int = 0) -> dict:
    inp = _fwd_setup_inputs(seed)
    key = _jax.random.fold_in(_jax.random.key(seed), 7919)
    shape, _ = _output_shape()
    out = dict(inp)
    out["loss_target"] = _jax.random.normal(_jax.random.fold_in(key, 0), shape, _jnp.float32)
    for i, name in enumerate(TWIN_WEIGHTS):
        w = inp[name].astype(_jnp.float32)
        if MOMENT_SCALE is None:
            s = _jnp.sqrt(_jnp.mean(_jnp.square(w)) + 1e-30)
        else:
            s = MOMENT_SCALE[name]
        km, kv = _jax.random.split(_jax.random.fold_in(key, i + 1))
        out[name] = w
        out["m_" + name] = s * _jax.random.normal(km, w.shape, _jnp.float32)
        out["v_" + name] = (s * s) * _jax.random.uniform(kv, w.shape, _jnp.float32, 0.5, 1.5)
    if N_MICROBATCH > 1:
        for name, axis in PER_EXAMPLE_BATCH_AXIS.items():
            out[name] = _to_microbatches(out[name], axis)
    return {'x': out['x'], 'mem': out['mem'], 'ln_mix_pre': out['ln_mix_pre'], 'ln_mix_post': out['ln_mix_post'], 'w_in': out['w_in'], 'swa_sinks': out['swa_sinks'], 'rel_bias': out['rel_bias'], 'ln_mem': out['ln_mem'], 'w_mem_kv': out['w_mem_kv'], 'w_branch_swa': out['w_branch_swa'], 'w_branch_sb': out['w_branch_sb'], 'w_branch_mem': out['w_branch_mem'], 'w_out': out['w_out'], 'ln_ffn_pre': out['ln_ffn_pre'], 'ln_ffn_post': out['ln_ffn_post'], 'w_gate': out['w_gate'], 'w_up': out['w_up'], 'w_down': out['w_down'], 'loss_target': out['loss_target'], 'm_ln_mix_pre': out['m_ln_mix_pre'], 'm_ln_mix_post': out['m_ln_mix_post'], 'm_w_in': out['m_w_in'], 'm_swa_sinks': out['m_swa_sinks'], 'm_rel_bias': out['m_rel_bias'], 'm_ln_mem': out['m_ln_mem'], 'm_w_mem_kv': out['m_w_mem_kv'], 'm_w_branch_swa': out['m_w_branch_swa'], 'm_w_branch_sb': out['m_w_branch_sb'], 'm_w_branch_mem': out['m_w_branch_mem'], 'm_w_out': out['m_w_out'], 'm_ln_ffn_pre': out['m_ln_ffn_pre'], 'm_ln_ffn_post': out['m_ln_ffn_post'], 'm_w_gate': out['m_w_gate'], 'm_w_up': out['m_w_up'], 'm_w_down': out['m_w_down'], 'v_ln_mix_pre': out['v_ln_mix_pre'], 'v_ln_mix_post': out['v_ln_mix_post'], 'v_w_in': out['v_w_in'], 'v_swa_sinks': out['v_swa_sinks'], 'v_rel_bias': out['v_rel_bias'], 'v_ln_mem': out['v_ln_mem'], 'v_w_mem_kv': out['v_w_mem_kv'], 'v_w_branch_swa': out['v_w_branch_swa'], 'v_w_branch_sb': out['v_w_branch_sb'], 'v_w_branch_mem': out['v_w_branch_mem'], 'v_w_out': out['v_w_out'], 'v_ln_ffn_pre': out['v_ln_ffn_pre'], 'v_ln_ffn_post': out['v_ln_ffn_post'], 'v_w_gate': out['v_w_gate'], 'v_w_up': out['v_w_up'], 'v_w_down': out['v_w_down']}


def _loss(weights, diff, rest, loss_target):
    with _jax.named_scope("forward"):
        args = {**rest, TWIN_DIFF_INPUT: diff, **{k: w.astype(_WEIGHT_DTYPES[k]) for k, w in weights.items()}}
        y = _forward(args)
    with _jax.named_scope("loss_head"):
        err = _jnp.square(y.astype(_jnp.float32) - loss_target)
        return 0.5 * _jnp.sum(_jnp.mean(err, axis=-1)) if err.ndim else 0.5 * err


def _adamw(w, g, m, v):
    m = ADAM_B1 * m + (1.0 - ADAM_B1) * g
    v = ADAM_B2 * v + (1.0 - ADAM_B2) * _jnp.square(g)
    m_hat = m / (1.0 - ADAM_B1 ** ADAM_STEP)
    v_hat = v / (1.0 - ADAM_B2 ** ADAM_STEP)
    delta = -ADAM_LR * (m_hat / (_jnp.sqrt(v_hat) + ADAM_EPS) + ADAM_WD * w)
    return delta, m, v


def reference(x, mem, ln_mix_pre, ln_mix_post, w_in, swa_sinks, rel_bias, ln_mem, w_mem_kv, w_branch_swa, w_branch_sb, w_branch_mem, w_out, ln_ffn_pre, ln_ffn_post, w_gate, w_up, w_down, loss_target, m_ln_mix_pre, m_ln_mix_post, m_w_in, m_swa_sinks, m_rel_bias, m_ln_mem, m_w_mem_kv, m_w_branch_swa, m_w_branch_sb, m_w_branch_mem, m_w_out, m_ln_ffn_pre, m_ln_ffn_post, m_w_gate, m_w_up, m_w_down, v_ln_mix_pre, v_ln_mix_post, v_w_in, v_swa_sinks, v_rel_bias, v_ln_mem, v_w_mem_kv, v_w_branch_swa, v_w_branch_sb, v_w_branch_mem, v_w_out, v_ln_ffn_pre, v_ln_ffn_post, v_w_gate, v_w_up, v_w_down):
    given = dict(x=x, mem=mem, ln_mix_pre=ln_mix_pre, ln_mix_post=ln_mix_post, w_in=w_in, swa_sinks=swa_sinks, rel_bias=rel_bias, ln_mem=ln_mem, w_mem_kv=w_mem_kv, w_branch_swa=w_branch_swa, w_branch_sb=w_branch_sb, w_branch_mem=w_branch_mem, w_out=w_out, ln_ffn_pre=ln_ffn_pre, ln_ffn_post=ln_ffn_post, w_gate=w_gate, w_up=w_up, w_down=w_down, loss_target=loss_target, m_ln_mix_pre=m_ln_mix_pre, m_ln_mix_post=m_ln_mix_post, m_w_in=m_w_in, m_swa_sinks=m_swa_sinks, m_rel_bias=m_rel_bias, m_ln_mem=m_ln_mem, m_w_mem_kv=m_w_mem_kv, m_w_branch_swa=m_w_branch_swa, m_w_branch_sb=m_w_branch_sb, m_w_branch_mem=m_w_branch_mem, m_w_out=m_w_out, m_ln_ffn_pre=m_ln_ffn_pre, m_ln_ffn_post=m_ln_ffn_post, m_w_gate=m_w_gate, m_w_up=m_w_up, m_w_down=m_w_down, v_ln_mix_pre=v_ln_mix_pre, v_ln_mix_post=v_ln_mix_post, v_w_in=v_w_in, v_swa_sinks=v_swa_sinks, v_rel_bias=v_rel_bias, v_ln_mem=v_ln_mem, v_w_mem_kv=v_w_mem_kv, v_w_branch_swa=v_w_branch_swa, v_w_branch_sb=v_w_branch_sb, v_w_branch_mem=v_w_branch_mem, v_w_out=v_w_out, v_ln_ffn_pre=v_ln_ffn_pre, v_ln_ffn_post=v_ln_ffn_post, v_w_gate=v_w_gate, v_w_up=v_w_up, v_w_down=v_w_down)
    weights = {n: given[n] for n in TWIN_WEIGHTS}
    shared = {n: given[n] for n in SHARED_INPUTS}
    per_example = {n: given[n] for n in ['x', 'mem']}
    grad_fn = _jax.value_and_grad(_loss, argnums=(0, 1))

    def one_microbatch(ex, loss_target):
        ex = dict(ex)
        diff = ex.pop(TWIN_DIFF_INPUT)
        return grad_fn(weights, diff, {**shared, **ex}, loss_target)

    if N_MICROBATCH == 1:
        loss, (grad_w, grad_x) = one_microbatch(per_example, given["loss_target"])
    else:
        def body(carry, xs):
            loss_sum, grad_sum = carry
            l_k, (gw_k, gx_k) = one_microbatch(xs[0], xs[1])
            with _jax.named_scope("update"):
                return (loss_sum + l_k, _jax.tree.map(_jnp.add, grad_sum, gw_k)), gx_k

        init = (_jnp.zeros((), _jnp.float32), _jax.tree.map(_jnp.zeros_like, weights))
        (loss, grad_w), grad_x = _jax.lax.scan(body, init, (per_example, given["loss_target"]))
    with _jax.named_scope("update"):
        delta_w, new_m, new_v = {}, {}, {}
        for n in TWIN_WEIGHTS:
            delta_w[n], new_m[n], new_v[n] = _adamw(weights[n], grad_w[n], given["m_" + n], given["v_" + n])
    return (loss, grad_x, *[grad_w[n] for n in TWIN_WEIGHTS], *[delta_w[n] for n in TWIN_WEIGHTS],
            *[new_m[n] for n in TWIN_WEIGHTS], *[new_v[n] for n in TWIN_WEIGHTS])
```

```python
import functools
import math

import numpy as np
import jax
import jax.numpy as jnp
from jax import lax
from jax.experimental import pallas as pl
from jax.experimental.pallas import tpu as pltpu

F32 = jnp.float32
BF16 = jnp.bfloat16

N_DEV = 8
D_MODEL = 1024
BLOCK = 128
EPS = 1e-6
HEAD_DIM = 64
SWA_Q_HEADS = 8
SWA_WINDOW = 128
N_BUCKETS = 32
MAX_DISTANCE = 128
MEM_HEADS = 4
MEM_HEAD_DIM = 128
D_FF = 2816
IN_W = 5888
COL_QA, COL_KA, COL_VA, COL_QB, COL_KB, COL_VB, COL_QM, COL_GL = 0, 4, 5, 6, 10, 14, 18, 22
SCALE64 = HEAD_DIM ** -0.5
SCALE128 = MEM_HEAD_DIM ** -0.5
NEG = -1e30

ADAM_LR = 0.001
ADAM_B1 = 0.9
ADAM_B2 = 0.999
ADAM_EPS = 1e-08
ADAM_WD = 0.01
ADAM_STEP = 10

VMEM_LIMIT_BYTES = 56 * 1024 * 1024


def _params(**kw):
    return pltpu.CompilerParams(vmem_limit_bytes=VMEM_LIMIT_BYTES, **kw)


def _dot(a, b):
    return jnp.dot(a, b, preferred_element_type=F32)


def _dot_nt(a, b):
    return lax.dot_general(a, b, (((1,), (1,)), ((), ())), preferred_element_type=F32)


def _dot_tn(a, b):
    return lax.dot_general(a, b, (((0,), (0,)), ((), ())), preferred_element_type=F32)


def _dot_split(x, m):
    hi = x.astype(BF16)
    lo = (x - hi.astype(F32)).astype(BF16)
    return _dot(hi, m) + _dot(lo, m)


_DIMS = {"nn": (((1,), (0,)), ((), ())), "nt": (((1,), (1,)), ((), ())), "tn": (((0,), (0,)), ((), ()))}


def _matmul(pairs, mode, out_dtype, tm, tn, tk, name):
    a0, b0 = pairs[0]
    if mode == "nn":
        (M, K), N = a0.shape, b0.shape[1]
    elif mode == "nt":
        (M, K), N = a0.shape, b0.shape[0]
    else:
        (K, M), N = a0.shape, b0.shape[1]
    tm, tn, tk = min(tm, M), min(tn, N), min(tk, K)
    assert M % tm == 0 and N % tn == 0 and K % tk == 0, (name, M, N, K, tm, tn, tk)
    nm, nn, nk = M // tm, N // tn, K // tk
    npair = len(pairs)
    dims = _DIMS[mode]

    def body(*refs):
        ab = refs[:2 * npair]
        o_ref = refs[2 * npair]
        acc_ref = refs[2 * npair + 1]
        k = pl.program_id(2)
        part = lax.dot_general(ab[0][...], ab[1][...], dims, preferred_element_type=F32)
        for q in range(1, npair):
            part += lax.dot_general(ab[2 * q][...], ab[2 * q + 1][...], dims, preferred_element_type=F32)
        if nk == 1:
            o_ref[...] = part.astype(o_ref.dtype)
        else:
            @pl.when(k == 0)
            def _():
                acc_ref[...] = part

            @pl.when(k > 0)
            def _():
                acc_ref[...] += part

            @pl.when(k == nk - 1)
            def _():
                o_ref[...] = acc_ref[...].astype(o_ref.dtype)

    if mode == "nn":
        a_spec = pl.BlockSpec((tm, tk), lambda n, m, k: (m, k))
        b_spec = pl.BlockSpec((tk, tn), lambda n, m, k: (k, n))
    elif mode == "nt":
        a_spec = pl.BlockSpec((tm, tk), lambda n, m, k: (m, k))
        b_spec = pl.BlockSpec((tn, tk), lambda n, m, k: (n, k))
    else:
        a_spec = pl.BlockSpec((tk, tm), lambda n, m, k: (k, m))
        b_spec = pl.BlockSpec((tk, tn), lambda n, m, k: (k, n))
    args = [t for pr in pairs for t in pr]
    return pl.pallas_call(
        body,
        grid=(nn, nm, nk),
        in_specs=[a_spec, b_spec] * npair,
        out_specs=pl.BlockSpec((tm, tn), lambda n, m, k: (m, n)),
        out_shape=jax.ShapeDtypeStruct((M, N), out_dtype),
        scratch_shapes=[pltpu.VMEM((tm, tn) if nk > 1 else (8, 128), F32)],
        compiler_params=_params(dimension_semantics=("arbitrary", "arbitrary", "arbitrary")),
        name=name,
    )(*args)


def _rms_fwd(x, g, name):
    T, D = x.shape
    tr = min(512, T)

    def body(x_ref, g_ref, u_ref):
        xf = x_ref[...]
        r = lax.rsqrt(jnp.mean(xf * xf, axis=-1, keepdims=True) + EPS)
        u_ref[...] = ((xf * r) * g_ref[...]).astype(u_ref.dtype)

    return pl.pallas_call(
        body,
        grid=(T // tr,),
        in_specs=[pl.BlockSpec((tr, D), lambda i: (i, 0)), pl.BlockSpec((1, D), lambda i: (0, 0))],
        out_specs=pl.BlockSpec((tr, D), lambda i: (i, 0)),
        out_shape=jax.ShapeDtypeStruct((T, D), BF16),
        name=name,
    )(x, g)


def _rms_bwd_terms(xin, g, dy):
    r = lax.rsqrt(jnp.mean(xin * xin, axis=-1, keepdims=True) + EPS)
    xh = xin * r
    dg = jnp.sum(dy * xh, axis=0, keepdims=True)
    dxh = dy * g
    dx = r * (dxh - xh * jnp.mean(dxh * xh, axis=-1, keepdims=True))
    return dx, dg


GATE_TC = 256


def _branch_gate(proj, ys, wbs):
    T = proj.shape[0]
    D = D_MODEL
    tr, tc = 512, GATE_TC
    nc = D // tc
    gl0 = COL_GL * 128 // tc

    def body(ya, yb, yc, wa, wb, wc, g0, g1, g2, merged_ref, pa, pb, pc):
        acc = jnp.zeros((tr, tc), F32)
        for y_ref, w_ref, g_ref, p_ref in ((ya, wa, g0, pa), (yb, wb, g1, pb), (yc, wc, g2, pc)):
            p = _dot(y_ref[...], w_ref[...])
            p_ref[...] = p.astype(p_ref.dtype)
            acc += jax.nn.sigmoid(g_ref[...].astype(F32)) * p
        merged_ref[...] = acc.astype(merged_ref.dtype)

    y_spec = pl.BlockSpec((tr, 512), lambda i, n: (i, 0))
    w_spec = pl.BlockSpec((512, tc), lambda i, n: (0, n))
    o_spec = pl.BlockSpec((tr, tc), lambda i, n: (i, n))
    gl_specs = [pl.BlockSpec((tr, tc), lambda i, n, j=j: (i, gl0 + j * nc + n)) for j in range(3)]
    out = jax.ShapeDtypeStruct((T, D), BF16)
    return pl.pallas_call(
        body,
        grid=(T // tr, nc),
        in_specs=[y_spec] * 3 + [w_spec] * 3 + gl_specs,
        out_specs=[o_spec] * 4,
        out_shape=[out] * 4,
        compiler_params=_params(),
        name="branch_gate",
    )(*ys, *wbs, proj, proj, proj)


def _post_pre(x, mix, g_post, g_pre):
    T, D = x.shape
    tr = 512

    def body(x_ref, mix_ref, gp_ref, gq_ref, h1_ref, u2_ref):
        mx = mix_ref[...]
        r = lax.rsqrt(jnp.mean(mx * mx, axis=-1, keepdims=True) + EPS)
        h1 = x_ref[...] + (mx * r) * gp_ref[...]
        h1_ref[...] = h1
        r2 = lax.rsqrt(jnp.mean(h1 * h1, axis=-1, keepdims=True) + EPS)
        u2_ref[...] = ((h1 * r2) * gq_ref[...]).astype(u2_ref.dtype)

    row = pl.BlockSpec((tr, D), lambda i: (i, 0))
    vec = pl.BlockSpec((1, D), lambda i: (0, 0))
    return pl.pallas_call(
        body,
        grid=(T // tr,),
        in_specs=[row, row, vec, vec],
        out_specs=[row, row],
        out_shape=[jax.ShapeDtypeStruct((T, D), F32), jax.ShapeDtypeStruct((T, D), BF16)],
        compiler_params=_params(),
        name="post_pre",
    )(x, mix, g_post, g_pre)


def _ffn_up(u2, w_gate, w_up):
    T, D = u2.shape
    F = D_FF
    tm, tn = 512, F // 2

    def body(u_ref, wg_ref, wu_ref, a_ref, zg_ref, zu_ref):
        u = u_ref[...]
        zg = _dot(u, wg_ref[...])
        zu = _dot(u, wu_ref[...])
        a_ref[...] = (zg * jax.nn.sigmoid(zg) * zu).astype(a_ref.dtype)
        zg_ref[...] = zg.astype(zg_ref.dtype)
        zu_ref[...] = zu.astype(zu_ref.dtype)

    o_spec = pl.BlockSpec((tm, tn), lambda n, m: (m, n))
    out = jax.ShapeDtypeStruct((T, F), BF16)
    return pl.pallas_call(
        body,
        grid=(F // tn, T // tm),
        in_specs=[pl.BlockSpec((tm, D), lambda n, m: (m, 0)),
                  pl.BlockSpec((D, tn), lambda n, m: (0, n)),
                  pl.BlockSpec((D, tn), lambda n, m: (0, n))],
        out_specs=[o_spec] * 3,
        out_shape=[out] * 3,
        compiler_params=_params(),
        name="ffn_up",
    )(u2, w_gate, w_up)


def _loss_head(ffn, h1, target, g_post):
    T, D = ffn.shape
    tr = 512

    def body(f_ref, h1_ref, t_ref, g_ref, dffn_ref, dh2_ref, loss_ref, dg_ref):
        i = pl.program_id(0)
        f = f_ref[...]
        g = g_ref[...]
        r = lax.rsqrt(jnp.mean(f * f, axis=-1, keepdims=True) + EPS)
        xh = f * r
        err = (h1_ref[...] + xh * g) - t_ref[...]
        part = 0.5 * jnp.sum(jnp.mean(err * err, axis=-1, keepdims=True), axis=0, keepdims=True)
        dh2 = err * (1.0 / D)
        dh2_ref[...] = dh2
        dgp = jnp.sum(dh2 * xh, axis=0, keepdims=True)
        dxh = dh2 * g
        dffn_ref[...] = (r * (dxh - xh * jnp.mean(dxh * xh, axis=-1, keepdims=True))).astype(dffn_ref.dtype)

        @pl.when(i == 0)
        def _():
            loss_ref[...] = jnp.zeros_like(loss_ref)
            dg_ref[...] = jnp.zeros_like(dg_ref)

        loss_ref[...] += jnp.broadcast_to(part, loss_ref.shape)
        dg_ref[...] += dgp

    row = pl.BlockSpec((tr, D), lambda i: (i, 0))
    vec = pl.BlockSpec((1, D), lambda i: (0, 0))
    return pl.pallas_call(
        body,
        grid=(T // tr,),
        in_specs=[row, row, row, vec],
        out_specs=[row, row, pl.BlockSpec((8, 128), lambda i: (0, 0)), vec],
        out_shape=[jax.ShapeDtypeStruct((T, D), BF16), jax.ShapeDtypeStruct((T, D), F32),
                   jax.ShapeDtypeStruct((8, 128), F32), jax.ShapeDtypeStruct((1, D), F32)],
        compiler_params=_params(),
        name="loss_head",
    )(ffn, h1, target, g_post)


def _ffn_down_bwd(dffn, wd, zg, zu):
    T, D = dffn.shape
    F = D_FF
    tm, tn = 512, F // 2

    def body(d_ref, w_ref, zg_ref, zu_ref, dzg_ref, dzu_ref):
        da = _dot_nt(d_ref[...], w_ref[...])
        zg = zg_ref[...].astype(F32)
        zu = zu_ref[...].astype(F32)
        s = jax.nn.sigmoid(zg)
        dzu_ref[...] = (da * (zg * s)).astype(dzu_ref.dtype)
        dzg_ref[...] = (da * zu * (s * (1.0 + zg * (1.0 - s)))).astype(dzg_ref.dtype)

    z_spec = pl.BlockSpec((tm, tn), lambda n, m: (m, n))
    out = jax.ShapeDtypeStruct((T, F), BF16)
    return pl.pallas_call(
        body,
        grid=(F // tn, T // tm),
        in_specs=[pl.BlockSpec((tm, D), lambda n, m: (m, 0)), pl.BlockSpec((tn, D), lambda n, m: (n, 0)),
                  z_spec, z_spec],
        out_specs=[z_spec, z_spec],
        out_shape=[out, out],
        compiler_params=_params(),
        name="ffn_down_bwd",
    )(dffn, wd, zg, zu)


def _mid_bwd(h1, du2, dh2, mix, g_pre, g_post):
    T, D = h1.shape
    tr = 512

    def body(h1_ref, du2_ref, dh2_ref, mix_ref, gq_ref, gp_ref, dh1_ref, dmix_ref, dgq_ref, dgp_ref):
        i = pl.program_id(0)
        dx, dgq = _rms_bwd_terms(h1_ref[...], gq_ref[...], du2_ref[...])
        dh1 = dh2_ref[...] + dx
        dh1_ref[...] = dh1
        dmix, dgp = _rms_bwd_terms(mix_ref[...], gp_ref[...], dh1)
        dmix_ref[...] = dmix.astype(dmix_ref.dtype)

        @pl.when(i == 0)
        def _():
            dgq_ref[...] = jnp.zeros_like(dgq_ref)
            dgp_ref[...] = jnp.zeros_like(dgp_ref)

        dgq_ref[...] += dgq
        dgp_ref[...] += dgp

    row = pl.BlockSpec((tr, D), lambda i: (i, 0))
    vec = pl.BlockSpec((1, D), lambda i: (0, 0))
    return pl.pallas_call(
        body,
        grid=(T // tr,),
        in_specs=[row, row, row, row, vec, vec],
        out_specs=[row, row, vec, vec],
        out_shape=[jax.ShapeDtypeStruct((T, D), F32), jax.ShapeDtypeStruct((T, D), BF16),
                   jax.ShapeDtypeStruct((1, D), F32), jax.ShapeDtypeStruct((1, D), F32)],
        compiler_params=_params(),
        name="mid_bwd",
    )(h1, du2, dh2, mix, g_pre, g_post)


def _gate_bwd(dmerged, ps, proj, wbs):
    T, D = dmerged.shape
    tr, tc = 512, GATE_TC
    nc = D // tc
    gl0 = COL_GL * 128 // tc

    def body(dm_ref, pa, pb, pc, g0, g1, g2, wa, wb, wc, dpa, dpb, dpc, dga, dgb, dgc, dya, dyb, dyc,
             acc_a, acc_b, acc_c):
        n = pl.program_id(1)
        dm = dm_ref[...]
        for p_ref, g_ref, w_ref, dp_ref, dg_ref, dy_ref, acc_ref in (
                (pa, g0, wa, dpa, dga, dya, acc_a), (pb, g1, wb, dpb, dgb, dyb, acc_b),
                (pc, g2, wc, dpc, dgc, dyc, acc_c)):
            s = jax.nn.sigmoid(g_ref[...].astype(F32))
            dp = (dm * s).astype(BF16)
            dp_ref[...] = dp
            dg_ref[...] = (dm * p_ref[...].astype(F32) * (s * (1.0 - s))).astype(dg_ref.dtype)
            part = _dot_nt(dp, w_ref[...])

            @pl.when(n == 0)
            def _():
                acc_ref[...] = part

            @pl.when(n > 0)
            def _():
                acc_ref[...] += part

            @pl.when(n == nc - 1)
            def _():
                dy_ref[...] = acc_ref[...].astype(dy_ref.dtype)

    col = pl.BlockSpec((tr, tc), lambda i, n: (i, n))
    y_spec = pl.BlockSpec((tr, 512), lambda i, n: (i, 0))
    w_spec = pl.BlockSpec((512, tc), lambda i, n: (0, n))
    gl_specs = [pl.BlockSpec((tr, tc), lambda i, n, j=j: (i, gl0 + j * nc + n)) for j in range(3)]
    big = jax.ShapeDtypeStruct((T, D), BF16)
    small = jax.ShapeDtypeStruct((T, 512), BF16)
    return pl.pallas_call(
        body,
        grid=(T // tr, nc),
        in_specs=[col] * 4 + gl_specs + [w_spec] * 3,
        out_specs=[col] * 6 + [y_spec] * 3,
        out_shape=[big] * 6 + [small] * 3,
        scratch_shapes=[pltpu.VMEM((tr, 512), F32)] * 3,
        compiler_params=_params(),
        name="gate_bwd",
    )(dmerged, *ps, proj, proj, proj, *wbs)


def _pre_bwd(x, du, dh1, g):
    T, D = x.shape
    tr = 512

    def body(x_ref, du_ref, dh1_ref, g_ref, gx_ref, dg_ref):
        i = pl.program_id(0)
        dx, dg = _rms_bwd_terms(x_ref[...], g_ref[...], du_ref[...])
        gx_ref[...] = dh1_ref[...] + dx

        @pl.when(i == 0)
        def _():
            dg_ref[...] = jnp.zeros_like(dg_ref)

        dg_ref[...] += dg

    row = pl.BlockSpec((tr, D), lambda i: (i, 0))
    vec = pl.BlockSpec((1, D), lambda i: (0, 0))
    return pl.pallas_call(
        body,
        grid=(T // tr,),
        in_specs=[row, row, row, vec],
        out_specs=[row, vec],
        out_shape=[jax.ShapeDtypeStruct((T, D), F32), jax.ShapeDtypeStruct((1, D), F32)],
        compiler_params=_params(),
        name="pre_bwd",
    )(x, du, dh1, g)


def _gain_grad(xin, dy):
    T, D = xin.shape
    tr = min(512, T)

    def body(x_ref, dy_ref, dg_ref):
        i = pl.program_id(0)
        xf = x_ref[...]
        r = lax.rsqrt(jnp.mean(xf * xf, axis=-1, keepdims=True) + EPS)

        @pl.when(i == 0)
        def _():
            dg_ref[...] = jnp.zeros_like(dg_ref)

        dg_ref[...] += jnp.sum(dy_ref[...] * (xf * r), axis=0, keepdims=True)

    row = pl.BlockSpec((tr, D), lambda i: (i, 0))
    return pl.pallas_call(
        body,
        grid=(T // tr,),
        in_specs=[row, row],
        out_specs=pl.BlockSpec((1, D), lambda i: (0, 0)),
        out_shape=jax.ShapeDtypeStruct((1, D), F32),
        name="gain_grad",
    )(xin, dy)


def _swa_buckets():
    dist = (np.arange(BLOCK)[:, None] + BLOCK) - np.arange(2 * BLOCK)[None, :]
    max_exact = N_BUCKETS // 2
    d = np.maximum(dist, 0)
    df = np.maximum(d, 1).astype(np.float32)
    large = max_exact + (np.log(df / np.float32(max_exact)) / np.float32(math.log(MAX_DISTANCE / max_exact))
                         * np.float32(N_BUCKETS - max_exact)).astype(np.int32)
    large = np.minimum(large, N_BUCKETS - 1)
    bucket = np.where(d < max_exact, d, large)
    in_win = (dist >= 0) & (dist < SWA_WINDOW)
    return np.where(in_win, bucket, -1).astype(np.int32)


def _swa_bias_table(rel_bias, buckets):
    H = SWA_Q_HEADS

    def body(rb_ref, bk_ref, o_ref):
        bk = bk_ref[...]
        for h in range(H):
            acc = jnp.full(bk.shape, NEG, F32)
            for b in range(N_BUCKETS):
                acc = jnp.where(bk == b, rb_ref[b, h], acc)
            o_ref[h] = acc

    return pl.pallas_call(
        body,
        in_specs=[pl.BlockSpec(memory_space=pltpu.SMEM), pl.BlockSpec(memory_space=pltpu.VMEM)],
        out_specs=pl.BlockSpec(memory_space=pltpu.VMEM),
        out_shape=jax.ShapeDtypeStruct((H, BLOCK, 2 * BLOCK), F32),
        name="swa_bias_table",
    )(rel_bias, buckets)


def _swa_bias_grad(dbias, buckets):
    H = SWA_Q_HEADS

    def body(db_ref, bk_ref, o_ref):
        bk = bk_ref[...]
        rows = lax.broadcasted_iota(jnp.int32, (N_BUCKETS, 128), 0)
        lanes = lax.broadcasted_iota(jnp.int32, (N_BUCKETS, 128), 1)
        acc = jnp.zeros((N_BUCKETS, 128), F32)
        for h in range(H):
            d = db_ref[h]
            for b in range(N_BUCKETS):
                s = jnp.sum(jnp.sum(jnp.where(bk == b, d, 0.0), axis=1, keepdims=True), axis=0, keepdims=True)
                acc = jnp.where((rows == b) & (lanes == h), s, acc)
        o_ref[...] = acc

    return pl.pallas_call(
        body,
        in_specs=[pl.BlockSpec(memory_space=pltpu.VMEM)] * 2,
        out_specs=pl.BlockSpec(memory_space=pltpu.VMEM),
        out_shape=jax.ShapeDtypeStruct((N_BUCKETS, 128), F32),
        name="swa_bias_grad",
    )(dbias, buckets)


def _swa_scores(qs, kp, kc, bias, sink, first):
    sp = _dot_nt(qs, kp) * SCALE64 + bias[:, :BLOCK]
    sp = jnp.where(first, NEG, sp)
    sc = _dot_nt(qs, kc) * SCALE64 + bias[:, BLOCK:]
    m = jnp.maximum(jnp.maximum(jnp.max(sp, axis=1, keepdims=True), jnp.max(sc, axis=1, keepdims=True)), sink)
    pp = jnp.exp(sp - m)
    pc = jnp.exp(sc - m)
    ps = jnp.exp(sink - m)
    den = jnp.sum(pp, axis=1, keepdims=True) + jnp.sum(pc, axis=1, keepdims=True) + ps
    return pp / den, pc / den, ps / den


def _swa_fwd(proj, sinks, bias_tab, B, S):
    nb = S // BLOCK
    T = B * S

    def body(sink_ref, q_ref, kp_ref, kc_ref, vp_ref, vc_ref, bias_ref, o_ref):
        p = pl.program_id(0)
        i = pl.program_id(2)
        kvh = p // 2
        lane = lax.broadcasted_iota(jnp.int32, (BLOCK, BLOCK), 1)
        kvmask = (lane // HEAD_DIM) == kvh
        first = jnp.full((BLOCK, BLOCK), i, jnp.int32) == 0
        kp = jnp.where(kvmask, kp_ref[...], 0)
        kc = jnp.where(kvmask, kc_ref[...], 0)
        vp = jnp.where(kvmask, vp_ref[...], 0)
        vc = jnp.where(kvmask, vc_ref[...], 0)
        q = q_ref[...].astype(F32)
        q_r = pltpu.roll(q, HEAD_DIM, 1)
        acc = jnp.zeros((BLOCK, BLOCK), F32)
        for hh in range(2):
            same = jnp.full((BLOCK, BLOCK), kvh, jnp.int32) == hh
            qs = jnp.where(same, q, q_r).astype(BF16)
            wp, wc, _ = _swa_scores(qs, kp, kc, bias_ref[hh], sink_ref[0, 2 * p + hh], first)
            o = _dot(wp.astype(BF16), vp) + _dot(wc.astype(BF16), vc)
            acc += jnp.where(same, o, pltpu.roll(o, HEAD_DIM, 1))
        o_ref[...] = acc.astype(o_ref.dtype)

    blk = (BLOCK, BLOCK)
    return pl.pallas_call(
        body,
        grid=(4, B, nb),
        in_specs=[pl.BlockSpec(memory_space=pltpu.SMEM),
                  pl.BlockSpec(blk, lambda p, b, i: (b * nb + i, COL_QA + p)),
                  pl.BlockSpec(blk, lambda p, b, i: (b * nb + jnp.maximum(i - 1, 0), COL_KA)),
                  pl.BlockSpec(blk, lambda p, b, i: (b * nb + i, COL_KA)),
                  pl.BlockSpec(blk, lambda p, b, i: (b * nb + jnp.maximum(i - 1, 0), COL_VA)),
                  pl.BlockSpec(blk, lambda p, b, i: (b * nb + i, COL_VA)),
                  pl.BlockSpec((2, BLOCK, 2 * BLOCK), lambda p, b, i: (p, 0, 0))],
        out_specs=pl.BlockSpec(blk, lambda p, b, i: (b * nb + i, p)),
        out_shape=jax.ShapeDtypeStruct((T, 512), BF16),
        name="swa_fwd",
    )(sinks, proj, proj, proj, proj, proj, bias_tab)


def _swa_bwd(proj, dy, sinks, bias_tab, B, S):
    nb = S // BLOCK
    T = B * S
    H = SWA_Q_HEADS

    def body(sink_ref, q_ref, kp_ref, kc_ref, vp_ref, vc_ref, do_ref, bias_ref,
             dq_ref, dk_ref, dv_ref, dbias_ref, dsink_ref):
        b = pl.program_id(0)
        i = pl.program_id(1)
        p = pl.program_id(2)

        @pl.when((b == 0) & (i == 0) & (p == 0))
        def _():
            dbias_ref[...] = jnp.zeros_like(dbias_ref)
            dsink_ref[...] = jnp.zeros_like(dsink_ref)

        @pl.when((i == 0) & (p == 0))
        def _():
            dk_ref[...] = jnp.zeros_like(dk_ref)
            dv_ref[...] = jnp.zeros_like(dv_ref)

        kvh = p // 2
        lane = lax.broadcasted_iota(jnp.int32, (BLOCK, BLOCK), 1)
        kvmask = (lane // HEAD_DIM) == kvh
        first = jnp.full((BLOCK, BLOCK), i, jnp.int32) == 0
        kp = jnp.where(kvmask, kp_ref[...], 0)
        kc = jnp.where(kvmask, kc_ref[...], 0)
        vp = jnp.where(kvmask, vp_ref[...], 0)
        vc = jnp.where(kvmask, vc_ref[...], 0)
        q = q_ref[...].astype(F32)
        q_r = pltpu.roll(q, HEAD_DIM, 1)
        do = do_ref[...].astype(F32)
        do_r = pltpu.roll(do, HEAD_DIM, 1)
        dq = jnp.zeros((BLOCK, BLOCK), F32)
        dkp = jnp.zeros((BLOCK, BLOCK), F32)
        dkc = jnp.zeros((BLOCK, BLOCK), F32)
        dvp = jnp.zeros((BLOCK, BLOCK), F32)
        dvc = jnp.zeros((BLOCK, BLOCK), F32)
        for hh in range(2):
            h = 2 * p + hh
            same = jnp.full((BLOCK, BLOCK), kvh, jnp.int32) == hh
            qs = jnp.where(same, q, q_r).astype(BF16)
            dos = jnp.where(same, do, do_r).astype(BF16)
            bias = bias_ref[h]
            wp, wc, ws = _swa_scores(qs, kp, kc, bias, sink_ref[0, h], first)
            dwp = _dot_nt(dos, vp)
            dwc = _dot_nt(dos, vc)
            dsum = jnp.sum(wp * dwp, axis=1, keepdims=True) + jnp.sum(wc * dwc, axis=1, keepdims=True)
            dsp = wp * (dwp - dsum)
            dsc = wc * (dwc - dsum)
            dsink = jnp.sum(-ws * dsum, axis=0, keepdims=True)
            dsink_ref[h] += jnp.broadcast_to(dsink, (8, 128))
            dbias_ref[h, :, :BLOCK] += dsp
            dbias_ref[h, :, BLOCK:] += dsc
            dspb = dsp.astype(BF16)
            dscb = dsc.astype(BF16)
            dqh = (_dot(dspb, kp) + _dot(dscb, kc)) * SCALE64
            dq += jnp.where(same, dqh, pltpu.roll(dqh, HEAD_DIM, 1))
            dkp += _dot_tn(dspb, qs)
            dkc += _dot_tn(dscb, qs)
            dvp += _dot_tn(wp.astype(BF16), dos)
            dvc += _dot_tn(wc.astype(BF16), dos)
        dq_ref[...] = dq.astype(dq_ref.dtype)
        cur = pl.ds(pl.multiple_of(i * BLOCK, BLOCK), BLOCK)
        prev = pl.ds(pl.multiple_of(jnp.maximum(i - 1, 0) * BLOCK, BLOCK), BLOCK)
        dk_ref[prev, :] += jnp.where(kvmask, dkp * SCALE64, 0.0)
        dk_ref[cur, :] += jnp.where(kvmask, dkc * SCALE64, 0.0)
        dv_ref[prev, :] += jnp.where(kvmask, dvp, 0.0)
        dv_ref[cur, :] += jnp.where(kvmask, dvc, 0.0)

    blk = (BLOCK, BLOCK)
    kv_out = pl.BlockSpec((S, BLOCK), lambda b, i, p: (b, 0))
    return pl.pallas_call(
        body,
        grid=(B, nb, 4),
        in_specs=[pl.BlockSpec(memory_space=pltpu.SMEM),
                  pl.BlockSpec(blk, lambda b, i, p: (b * nb + i, COL_QA + p)),
                  pl.BlockSpec(blk, lambda b, i, p: (b * nb + jnp.maximum(i - 1, 0), COL_KA)),
                  pl.BlockSpec(blk, lambda b, i, p: (b * nb + i, COL_KA)),
                  pl.BlockSpec(blk, lambda b, i, p: (b * nb + jnp.maximum(i - 1, 0), COL_VA)),
                  pl.BlockSpec(blk, lambda b, i, p: (b * nb + i, COL_VA)),
                  pl.BlockSpec(blk, lambda b, i, p: (b * nb + i, p)),
                  pl.BlockSpec((H, BLOCK, 2 * BLOCK), lambda b, i, p: (0, 0, 0))],
        out_specs=[pl.BlockSpec(blk, lambda b, i, p: (b * nb + i, p)),
                   kv_out, kv_out,
                   pl.BlockSpec((H, BLOCK, 2 * BLOCK), lambda b, i, p: (0, 0, 0)),
                   pl.BlockSpec((H, 8, 128), lambda b, i, p: (0, 0, 0))],
        out_shape=[jax.ShapeDtypeStruct((T, 512), BF16),
                   jax.ShapeDtypeStruct((T, BLOCK), F32), jax.ShapeDtypeStruct((T, BLOCK), F32),
                   jax.ShapeDtypeStruct((H, BLOCK, 2 * BLOCK), F32), jax.ShapeDtypeStruct((H, 8, 128), F32)],
        compiler_params=_params(),
        name="swa_bwd",
    )(sinks, proj, proj, proj, proj, proj, dy, bias_tab)


def _sb_logits(qh, kj, causal):
    z = _dot_nt(qh, kj) * SCALE64
    sp = jnp.log1p(jnp.exp(-jnp.abs(z)))
    ls = jnp.minimum(z, 0.0) - sp
    l1m = jnp.where(causal, -jnp.maximum(z, 0.0) - sp, 0.0)
    return ls, l1m


def _sb_fwd(proj, B, S):
    nb = S // BLOCK
    T = B * S

    def body(q_ref, k_ref, v_ref, o_ref, r_ref):
        i = pl.program_id(2)
        lane = lax.broadcasted_iota(jnp.int32, (BLOCK, BLOCK), 1)
        row = lax.broadcasted_iota(jnp.int32, (BLOCK, BLOCK), 0)
        hm = (lane < HEAD_DIM, lane >= HEAD_DIM)
        later = (row > lane).astype(BF16)
        q = q_ref[...]
        qh = [jnp.where(hm[h], q, 0) for h in range(2)]

        def step(it, carry):
            acc, c0, c1 = carry
            j = i - it
            rows = pl.ds(pl.multiple_of(j * BLOCK, BLOCK), BLOCK)
            kj = k_ref[rows, :]
            vj = v_ref[rows, :]
            causal = (j * BLOCK + lane) < (i * BLOCK + row)
            cs = [c0, c1]
            for h in range(2):
                ls, l1m = _sb_logits(qh[h], kj, causal)
                e = ls + cs[h] + _dot_split(l1m, later)
                a = jnp.where(causal, jnp.exp(e), 0.0)
                acc += _dot(a.astype(BF16), jnp.where(hm[h], vj, 0))
                cs[h] = cs[h] + jnp.sum(l1m, axis=1, keepdims=True)
            return acc, cs[0], cs[1]

        zero = jnp.zeros((BLOCK, 1), F32)
        acc, c0, c1 = lax.fori_loop(0, i + 1, step, (jnp.zeros((BLOCK, BLOCK), F32), zero, zero))
        o_ref[...] = acc.astype(o_ref.dtype)
        r_ref[...] = jnp.where(hm[0], c0, c1)

    blk = (BLOCK, BLOCK)
    return pl.pallas_call(
        body,
        grid=(B, 4, nb),
        in_specs=[pl.BlockSpec(blk, lambda b, p, i: (b * nb + i, COL_QB + p)),
                  pl.BlockSpec((S, BLOCK), lambda b, p, i: (b, COL_KB + p)),
                  pl.BlockSpec((S, BLOCK), lambda b, p, i: (b, COL_VB + p))],
        out_specs=[pl.BlockSpec(blk, lambda b, p, i: (b * nb + i, p))] * 2,
        out_shape=[jax.ShapeDtypeStruct((T, 512), BF16), jax.ShapeDtypeStruct((T, 512), F32)],
        name="sb_fwd",
    )(proj, proj, proj)


def _sb_bwd(proj, dy, rtot, B, S):
    nb = S // BLOCK
    T = B * S

    def body(q_ref, k_ref, v_ref, do_ref, r_ref, dq_ref, dk_ref, dv_ref):
        i = pl.program_id(2)

        @pl.when(i == 0)
        def _():
            dk_ref[...] = jnp.zeros_like(dk_ref)
            dv_ref[...] = jnp.zeros_like(dv_ref)

        lane = lax.broadcasted_iota(jnp.int32, (BLOCK, BLOCK), 1)
        row = lax.broadcasted_iota(jnp.int32, (BLOCK, BLOCK), 0)
        hm = (lane < HEAD_DIM, lane >= HEAD_DIM)
        later = (row > lane).astype(BF16)
        earlier = (row < lane).astype(BF16)
        q = q_ref[...]
        do = do_ref[...]
        r = r_ref[...]
        qh = [jnp.where(hm[h], q, 0) for h in range(2)]
        doh = [jnp.where(hm[h], do, 0) for h in range(2)]
        rh = [jnp.sum(jnp.where(lane == h * HEAD_DIM, r, 0.0), axis=1, keepdims=True) for h in range(2)]

        def step(j, carry):
            dq, l0, l1, p0, p1 = carry
            rows = pl.ds(pl.multiple_of(j * BLOCK, BLOCK), BLOCK)
            kj = k_ref[rows, :]
            vj = v_ref[rows, :]
            causal = (j * BLOCK + lane) < (i * BLOCK + row)
            lsum = [l0, l1]
            psum = [p0, p1]
            dkj = jnp.zeros((BLOCK, BLOCK), F32)
            dvj = jnp.zeros((BLOCK, BLOCK), F32)
            for h in range(2):
                ls, l1m = _sb_logits(qh[h], kj, causal)
                sig = jnp.exp(ls)
                lsum[h] = lsum[h] + jnp.sum(l1m, axis=1, keepdims=True)
                e = ls + (rh[h] - lsum[h]) + _dot_split(l1m, later)
                a = jnp.where(causal, jnp.exp(e), 0.0)
                de = _dot_nt(doh[h], vj) * a
                pre = psum[h] + _dot_split(de, earlier)
                dz = jnp.where(causal, de * (1.0 - sig) - sig * pre, 0.0).astype(BF16)
                psum[h] = psum[h] + jnp.sum(de, axis=1, keepdims=True)
                dq += _dot(dz, jnp.where(hm[h], kj, 0))
                dkj += _dot_tn(dz, qh[h])
                dvj += _dot_tn(a.astype(BF16), doh[h])
            dk_ref[rows, :] += dkj * SCALE64
            dv_ref[rows, :] += dvj
            return dq, lsum[0], lsum[1], psum[0], psum[1]

        zero = jnp.zeros((BLOCK, 1), F32)
        dq = lax.fori_loop(0, i + 1, step, (jnp.zeros((BLOCK, BLOCK), F32), zero, zero, zero, zero))[0]
        dq_ref[...] = (dq * SCALE64).astype(dq_ref.dtype)

    blk = (BLOCK, BLOCK)
    q_spec = pl.BlockSpec(blk, lambda b, p, i: (b * nb + i, p))
    kv_out = pl.BlockSpec((S, BLOCK), lambda b, p, i: (b, p))
    return pl.pallas_call(
        body,
        grid=(B, 4, nb),
        in_specs=[pl.BlockSpec(blk, lambda b, p, i: (b * nb + i, COL_QB + p)),
                  pl.BlockSpec((S, BLOCK), lambda b, p, i: (b, COL_KB + p)),
                  pl.BlockSpec((S, BLOCK), lambda b, p, i: (b, COL_VB + p)),
                  q_spec, q_spec],
        out_specs=[q_spec, kv_out, kv_out],
        out_shape=[jax.ShapeDtypeStruct((T, 512), BF16),
                   jax.ShapeDtypeStruct((T, 512), F32), jax.ShapeDtypeStruct((T, 512), F32)],
        compiler_params=_params(),
        name="sb_bwd",
    )(proj, proj, proj, dy, rtot)


def _mem_weights(q, mk):
    z = _dot_nt(q, mk) * SCALE128
    e = jnp.exp(z - jnp.max(z, axis=1, keepdims=True))
    return e / jnp.sum(e, axis=1, keepdims=True)


def _mem_fwd(proj, mkv, B, S, M):
    tq = 512
    nq = S // tq
    T = B * S
    Hm = MEM_HEADS

    def body(q_ref, mk_ref, mv_ref, o_ref):
        w = _mem_weights(q_ref[...], mk_ref[...])
        o_ref[...] = _dot(w.astype(BF16), mv_ref[...]).astype(o_ref.dtype)

    return pl.pallas_call(
        body,
        grid=(B, Hm, nq),
        in_specs=[pl.BlockSpec((tq, 128), lambda b, h, i: (b * nq + i, COL_QM + h)),
                  pl.BlockSpec((M, 128), lambda b, h, i: (b, h)),
                  pl.BlockSpec((M, 128), lambda b, h, i: (b, Hm + h))],
        out_specs=pl.BlockSpec((tq, 128), lambda b, h, i: (b * nq + i, h)),
        out_shape=jax.ShapeDtypeStruct((T, 512), BF16),
        name="mem_fwd",
    )(proj, mkv, mkv)


def _mem_bwd(proj, mkv, dy, B, S, M):
    tq = 512
    nq = S // tq
    T = B * S
    Hm = MEM_HEADS

    def body(q_ref, mk_ref, mv_ref, do_ref, dq_ref, dmk_ref, dmv_ref):
        i = pl.program_id(2)
        q = q_ref[...]
        do = do_ref[...]
        mk = mk_ref[...]
        w = _mem_weights(q, mk)
        dw = _dot_nt(do, mv_ref[...])
        ds = (w * (dw - jnp.sum(w * dw, axis=1, keepdims=True))).astype(BF16)
        dq_ref[...] = (_dot(ds, mk) * SCALE128).astype(dq_ref.dtype)

        @pl.when(i == 0)
        def _():
            dmk_ref[...] = jnp.zeros_like(dmk_ref)
            dmv_ref[...] = jnp.zeros_like(dmv_ref)

        dmk_ref[...] += _dot_tn(ds, q) * SCALE128
        dmv_ref[...] += _dot_tn(w.astype(BF16), do)

    q_spec = pl.BlockSpec((tq, 128), lambda b, h, i: (b * nq + i, h))
    m_out = pl.BlockSpec((M, 128), lambda b, h, i: (b, h))
    return pl.pallas_call(
        body,
        grid=(B, Hm, nq),
        in_specs=[pl.BlockSpec((tq, 128), lambda b, h, i: (b * nq + i, COL_QM + h)),
                  pl.BlockSpec((M, 128), lambda b, h, i: (b, h)),
                  pl.BlockSpec((M, 128), lambda b, h, i: (b, Hm + h)),
                  q_spec],
        out_specs=[q_spec, m_out, m_out],
        out_shape=[jax.ShapeDtypeStruct((T, 512), BF16),
                   jax.ShapeDtypeStruct((B * M, 512), F32), jax.ShapeDtypeStruct((B * M, 512), F32)],
        name="mem_bwd",
    )(proj, mkv, mkv, dy)


def _local_step(x, mem, target, ln_mix_pre, ln_mix_post, ln_mem, ln_ffn_pre, ln_ffn_post, swa_sinks, rel_bias,
                w_in, w_mem_kv, wb_swa, wb_sb, wb_mem, w_out, w_gate, w_up, w_down):
    B, S, D = x.shape
    M = mem.shape[1]
    T = B * S
    F = D_FF
    x2 = x.reshape(T, D)
    mem2 = mem.reshape(B * M, D)
    t2 = target.reshape(T, D)
    buckets = jnp.asarray(_swa_buckets())
    wbs = (wb_swa, wb_sb, wb_mem)

    u = _rms_fwd(x2, ln_mix_pre, "rms_mix_pre")
    proj = _matmul([(u, w_in)], "nn", BF16, 512, IN_W // 2, D, "proj_in")
    mn = _rms_fwd(mem2, ln_mem, "rms_mem")
    mkv = _matmul([(mn, w_mem_kv)], "nn", BF16, 512, 1024, D, "proj_mem")
    bias_tab = _swa_bias_table(rel_bias, buckets)
    y_swa = _swa_fwd(proj, swa_sinks, bias_tab, B, S)
    y_sb, rtot = _sb_fwd(proj, B, S)
    y_mem = _mem_fwd(proj, mkv, B, S, M)
    ys = (y_swa, y_sb, y_mem)
    merged, p_swa, p_sb, p_mem = _branch_gate(proj, ys, wbs)
    mix = _matmul([(merged, w_out)], "nn", F32, 512, 1024, D, "proj_out")
    h1, u2 = _post_pre(x2, mix, ln_mix_post, ln_ffn_pre)
    a, zg, zu = _ffn_up(u2, w_gate, w_up)
    ffn = _matmul([(a, w_down)], "nn", F32, 512, 1024, F, "ffn_down")
    dffn, dh2, loss_tile, d_ln_ffn_post = _loss_head(ffn, h1, t2, ln_ffn_post)

    d_w_down = _matmul([(a, dffn)], "tn", BF16, F // 2, 1024, 512, "dw_down")
    dzg, dzu = _ffn_down_bwd(dffn, w_down, zg, zu)
    d_w_gate = _matmul([(u2, dzg)], "tn", BF16, 1024, F // 2, 512, "dw_gate")
    d_w_up = _matmul([(u2, dzu)], "tn", BF16, 1024, F // 2, 512, "dw_up")
    du2 = _matmul([(dzg, w_gate), (dzu, w_up)], "nt", F32, 512, 1024, F // 2, "d_u2")
    dh1, dmix, d_ln_ffn_pre, d_ln_mix_post = _mid_bwd(h1, du2, dh2, mix, ln_ffn_pre, ln_mix_post)
    d_w_out = _matmul([(merged, dmix)], "tn", BF16, 1024, 1024, 512, "dw_out")
    dmerged = _matmul([(dmix, w_out)], "nt", F32, 512, 1024, D, "d_merged")
    (dp_swa, dp_sb, dp_mem, dg0, dg1, dg2, dy_swa, dy_sb, dy_mem) = _gate_bwd(
        dmerged, (p_swa, p_sb, p_mem), proj, wbs)
    d_wb_swa = _matmul([(y_swa, dp_swa)], "tn", BF16, 512, 1024, 512, "dw_branch_swa")
    d_wb_sb = _matmul([(y_sb, dp_sb)], "tn", BF16, 512, 1024, 512, "dw_branch_sb")
    d_wb_mem = _matmul([(y_mem, dp_mem)], "tn", BF16, 512, 1024, 512, "dw_branch_mem")
    dqa, dka, dva, dbias, dsink = _swa_bwd(proj, dy_swa, swa_sinks, bias_tab, B, S)
    dqb, dkb, dvb = _sb_bwd(proj, dy_sb, rtot, B, S)
    dqm, dmk, dmv = _mem_bwd(proj, mkv, dy_mem, B, S, M)
    d_rel_bias = _swa_bias_grad(dbias, buckets)[:, :SWA_Q_HEADS]
    d_sinks = dsink[:, 0, 0].reshape(1, SWA_Q_HEADS)
    dproj = jnp.concatenate([dqa, dka.astype(BF16), dva.astype(BF16), dqb, dkb.astype(BF16), dvb.astype(BF16),
                             dqm, dg0, dg1, dg2], axis=1)
    d_w_in = _matmul([(u, dproj)], "tn", BF16, 512, IN_W // 2, 512, "dw_in")
    du = _matmul([(dproj, w_in)], "nt", F32, 512, 1024, IN_W // 2, "d_u")
    grad_x, d_ln_mix_pre = _pre_bwd(x2, du, dh1, ln_mix_pre)
    dmkv = jnp.concatenate([dmk, dmv], axis=1).astype(BF16)
    d_w_mem_kv = _matmul([(mn, dmkv)], "tn", BF16, 1024, 1024, 512, "dw_mem_kv")
    dmn = _matmul([(dmkv, w_mem_kv)], "nt", F32, 512, 1024, 1024, "d_mn")
    d_ln_mem = _gain_grad(mem2, dmn)

    grads = dict(ln_mix_pre=d_ln_mix_pre, ln_mix_post=d_ln_mix_post, w_in=d_w_in, swa_sinks=d_sinks,
                 rel_bias=d_rel_bias, ln_mem=d_ln_mem, w_mem_kv=d_w_mem_kv, w_branch_swa=d_wb_swa,
                 w_branch_sb=d_wb_sb, w_branch_mem=d_wb_mem, w_out=d_w_out, ln_ffn_pre=d_ln_ffn_pre,
                 ln_ffn_post=d_ln_ffn_post, w_gate=d_w_gate, w_up=d_w_up, w_down=d_w_down)
    return loss_tile, grad_x.reshape(B, S, D), grads


def _mesh_pos():
    return lax.axis_index("x"), lax.axis_index("y"), lax.axis_index("c")


def _all_gather(blk, name):
    R, C = blk.shape

    def body(x_ref, out_ref, send_sems, recv_sems, local_sem):
        x, y, c = _mesh_pos()
        me, sibling = (x, y, c), (x, y, 1 - c)
        chips = [(1 - x, y), (x, 1 - y), (1 - x, 1 - y)]

        def slot(px, py, pc):
            return out_ref.at[4 * px + 2 * py + pc]

        def copy(k, block, to, src=None):
            return pltpu.make_async_remote_copy(
                src_ref=slot(*block) if src is None else src, dst_ref=slot(*block),
                send_sem=send_sems.at[k], recv_sem=recv_sems.at[k],
                device_id=to, device_id_type=pl.DeviceIdType.MESH)

        mine = pltpu.make_async_copy(x_ref, slot(*me), local_sem)
        mine.start()
        first = [copy(0, me, sibling, src=x_ref)]
        first += [copy(1 + j, me, (*chip, c), src=x_ref) for j, chip in enumerate(chips)]
        for cp in first:
            cp.start()
        passed = [copy(4 + j, (*chip, c), sibling) for j, chip in enumerate(chips)]
        for j, chip in enumerate(chips):
            copy(1 + j, (*chip, c), me).wait_recv()
            passed[j].start()
        copy(0, sibling, me).wait_recv()
        for j, chip in enumerate(chips):
            copy(4 + j, (*chip, 1 - c), me).wait_recv()
        for cp in first + passed:
            cp.wait_send()
        mine.wait()

    return pl.pallas_call(
        body,
        in_specs=[pl.BlockSpec(memory_space=pl.ANY)],
        out_specs=pl.BlockSpec(memory_space=pl.ANY),
        out_shape=jax.ShapeDtypeStruct((N_DEV, R, C), blk.dtype),
        scratch_shapes=[pltpu.SemaphoreType.DMA((7,)), pltpu.SemaphoreType.DMA((7,)), pltpu.SemaphoreType.DMA],
        name=name,
    )(blk)


def _exchange(parts, name):
    _, R, C = parts.shape

    def body(x_ref, out_ref, send_sems, recv_sems, local_sem):
        x, y, c = _mesh_pos()
        me = 4 * x + 2 * y + c
        mine = pltpu.make_async_copy(x_ref.at[me], out_ref.at[me], local_sem)
        mine.start()
        copies = []
        for k in range(1, N_DEV):
            px, py, pc = x ^ (k >> 2), y ^ ((k >> 1) & 1), c ^ (k & 1)
            peer = 4 * px + 2 * py + pc
            cp = pltpu.make_async_remote_copy(
                src_ref=x_ref.at[peer], dst_ref=out_ref.at[me],
                send_sem=send_sems.at[k - 1], recv_sem=recv_sems.at[k - 1],
                device_id=(px, py, pc), device_id_type=pl.DeviceIdType.MESH)
            cp.start()
            copies.append(cp)
        for cp in copies:
            cp.wait()
        mine.wait()

    return pl.pallas_call(
        body,
        in_specs=[pl.BlockSpec(memory_space=pl.ANY)],
        out_specs=pl.BlockSpec(memory_space=pl.ANY),
        out_shape=jax.ShapeDtypeStruct(parts.shape, parts.dtype),
        scratch_shapes=[pltpu.SemaphoreType.DMA((7,)), pltpu.SemaphoreType.DMA((7,)), pltpu.SemaphoreType.DMA],
        name=name,
    )(parts)


def _adamw(parts, w, m, v, name):
    R, C = w.shape
    tr = R
    for cand in (256, 176, 128, 64, 32, 16, 8):
        if R % cand == 0 and cand * C * 4 <= 1024 * 1024:
            tr = cand
            break
    c1 = 1.0 - ADAM_B1 ** ADAM_STEP
    c2 = 1.0 - ADAM_B2 ** ADAM_STEP

    def body(p_ref, w_ref, m_ref, v_ref, g_ref, d_ref, nm_ref, nv_ref):
        g = p_ref[0].astype(F32)
        for d in range(1, N_DEV):
            g = g + p_ref[d].astype(F32)
        nm = ADAM_B1 * m_ref[...] + (1.0 - ADAM_B1) * g
        nv = ADAM_B2 * v_ref[...] + (1.0 - ADAM_B2) * (g * g)
        g_ref[...] = g
        nm_ref[...] = nm
        nv_ref[...] = nv
        d_ref[...] = -ADAM_LR * ((nm / c1) / (jnp.sqrt(nv / c2) + ADAM_EPS) + ADAM_WD * w_ref[...])

    row = pl.BlockSpec((tr, C), lambda i: (i, 0))
    out = jax.ShapeDtypeStruct((R, C), F32)
    return pl.pallas_call(
        body,
        grid=(R // tr,),
        in_specs=[pl.BlockSpec((N_DEV, tr, C), lambda i: (0, i, 0)), row, row, row],
        out_specs=[row] * 4,
        out_shape=[out] * 4,
        compiler_params=_params(),
        name=name,
    )(parts, w, m, v)


def _col_shards(g):
    R, C8 = g.shape
    return g.reshape(R, N_DEV, C8 // N_DEV).transpose(1, 0, 2)


def _row_shards(g):
    R8, C = g.shape
    return g.reshape(N_DEV, R8 // N_DEV, C)


def _cols_full(gathered):
    n, R, C = gathered.shape
    return gathered.transpose(1, 0, 2).reshape(R, n * C)


_BIG = ("w_in", "w_mem_kv", "w_branch_swa", "w_branch_sb", "w_branch_mem", "w_out", "w_gate", "w_up", "w_down")
_COL_SHARDED = ("w_in", "w_branch_swa", "w_branch_sb", "w_branch_mem", "w_gate", "w_up")
_SMALL = ("ln_mix_pre", "ln_mix_post", "swa_sinks", "rel_bias", "ln_mem", "ln_ffn_pre", "ln_ffn_post")
_ORDER = ("ln_mix_pre", "ln_mix_post", "w_in", "swa_sinks", "rel_bias", "ln_mem", "w_mem_kv", "w_branch_swa",
          "w_branch_sb", "w_branch_mem", "w_out", "ln_ffn_pre", "ln_ffn_post", "w_gate", "w_up", "w_down")


def _pack_small(d):
    rows = [d["ln_mix_pre"], d["ln_mix_post"], d["ln_mem"], d["ln_ffn_pre"], d["ln_ffn_post"],
            jnp.pad(d["swa_sinks"].reshape(1, -1), ((0, 0), (0, D_MODEL - SWA_Q_HEADS))),
            jnp.pad(d["rel_bias"].reshape(1, -1), ((0, 0), (0, D_MODEL - N_BUCKETS * SWA_Q_HEADS))),
            jnp.zeros((1, D_MODEL), F32)]
    return jnp.concatenate([r.astype(F32) for r in rows], axis=0)


def _unpack_small(a):
    return dict(ln_mix_pre=a[0:1], ln_mix_post=a[1:2], ln_mem=a[2:3], ln_ffn_pre=a[3:4], ln_ffn_post=a[4:5],
                swa_sinks=a[5:6, :SWA_Q_HEADS],
                rel_bias=a[6, :N_BUCKETS * SWA_Q_HEADS].reshape(N_BUCKETS, SWA_Q_HEADS))


def kernel(x, mem, ln_mix_pre, ln_mix_post, w_in, swa_sinks, rel_bias, ln_mem, w_mem_kv, w_branch_swa, w_branch_sb, w_branch_mem, w_out, ln_ffn_pre, ln_ffn_post, w_gate, w_up, w_down, loss_target, m_ln_mix_pre, m_ln_mix_post, m_w_in, m_swa_sinks, m_rel_bias, m_ln_mem, m_w_mem_kv, m_w_branch_swa, m_w_branch_sb, m_w_branch_mem, m_w_out, m_ln_ffn_pre, m_ln_ffn_post, m_w_gate, m_w_up, m_w_down, v_ln_mix_pre, v_ln_mix_post, v_w_in, v_swa_sinks, v_rel_bias, v_ln_mem, v_w_mem_kv, v_w_branch_swa, v_w_branch_sb, v_w_branch_mem, v_w_out, v_ln_ffn_pre, v_ln_ffn_post, v_w_gate, v_w_up, v_w_down):
    w = dict(ln_mix_pre=ln_mix_pre, ln_mix_post=ln_mix_post, w_in=w_in[0], swa_sinks=swa_sinks, rel_bias=rel_bias,
             ln_mem=ln_mem, w_mem_kv=w_mem_kv[0], w_branch_swa=w_branch_swa[0], w_branch_sb=w_branch_sb[0],
             w_branch_mem=w_branch_mem[0], w_out=w_out[0], ln_ffn_pre=ln_ffn_pre, ln_ffn_post=ln_ffn_post,
             w_gate=w_gate[0], w_up=w_up[0], w_down=w_down[0])
    mom = dict(ln_mix_pre=m_ln_mix_pre, ln_mix_post=m_ln_mix_post, w_in=m_w_in[0], swa_sinks=m_swa_sinks,
               rel_bias=m_rel_bias, ln_mem=m_ln_mem, w_mem_kv=m_w_mem_kv[0], w_branch_swa=m_w_branch_swa[0],
               w_branch_sb=m_w_branch_sb[0], w_branch_mem=m_w_branch_mem[0], w_out=m_w_out[0],
               ln_ffn_pre=m_ln_ffn_pre, ln_ffn_post=m_ln_ffn_post, w_gate=m_w_gate[0], w_up=m_w_up[0],
               w_down=m_w_down[0])
    var = dict(ln_mix_pre=v_ln_mix_pre, ln_mix_post=v_ln_mix_post, w_in=v_w_in[0], swa_sinks=v_swa_sinks,
               rel_bias=v_rel_bias, ln_mem=v_ln_mem, w_mem_kv=v_w_mem_kv[0], w_branch_swa=v_w_branch_swa[0],
               w_branch_sb=v_w_branch_sb[0], w_branch_mem=v_w_branch_mem[0], w_out=v_w_out[0],
               ln_ffn_pre=v_ln_ffn_pre, ln_ffn_post=v_ln_ffn_post, w_gate=v_w_gate[0], w_up=v_w_up[0],
               w_down=v_w_down[0])

    full = {}
    for n in _BIG:
        g = _all_gather(w[n].astype(BF16), "ag_" + n)
        full[n] = _cols_full(g) if n in _COL_SHARDED else g.reshape(-1, g.shape[-1])

    loss_tile, grad_x, grads = _local_step(
        x, mem, loss_target, ln_mix_pre, ln_mix_post, ln_mem, ln_ffn_pre, ln_ffn_post, swa_sinks, rel_bias,
        full["w_in"], full["w_mem_kv"], full["w_branch_swa"], full["w_branch_sb"], full["w_branch_mem"],
        full["w_out"], full["w_gate"], full["w_up"], full["w_down"])
    loss = lax.psum(loss_tile[0, 0], ("x", "y", "c"))

    out = {}
    for n in _BIG:
        parts = _col_shards(grads[n]) if n in _COL_SHARDED else _row_shards(grads[n])
        recv = _exchange(parts, "rs_" + n)
        out[n] = _adamw(recv, w[n], mom[n], var[n], "adamw_" + n)
    small_parts = _all_gather(_pack_small(grads), "ag_small")
    res = _adamw(small_parts, _pack_small(w), _pack_small(mom), _pack_small(var), "adamw_small")
    small = [_unpack_small(r) for r in res]
    for n in _SMALL:
        out[n] = tuple(s[n] for s in small)

    def shaped(a, like):
        return a.reshape(like.shape)

    refs = dict(ln_mix_pre=ln_mix_pre, ln_mix_post=ln_mix_post, w_in=w_in, swa_sinks=swa_sinks, rel_bias=rel_bias,
                ln_mem=ln_mem, w_mem_kv=w_mem_kv, w_branch_swa=w_branch_swa, w_branch_sb=w_branch_sb,
                w_branch_mem=w_branch_mem, w_out=w_out, ln_ffn_pre=ln_ffn_pre, ln_ffn_post=ln_ffn_post,
                w_gate=w_gate, w_up=w_up, w_down=w_down)
    result = [loss, grad_x]
    for k in range(4):
        result += [shaped(out[n][k], refs[n]) for n in _ORDER]
    return tuple(result)
```

```python
import functools
import math

import numpy as np
import jax
import jax.numpy as jnp
from jax import lax
from jax.experimental import pallas as pl
from jax.experimental.pallas import tpu as pltpu

F32 = jnp.float32
BF16 = jnp.bfloat16

N_DEV = 8
D_MODEL = 1024
BLOCK = 128
EPS = 1e-6
HEAD_DIM = 64
SWA_Q_HEADS = 8
SWA_WINDOW = 128
N_BUCKETS = 32
MAX_DISTANCE = 128
MEM_HEADS = 4
MEM_HEAD_DIM = 128
D_FF = 2816
IN_W = 5888
COL_QA, COL_KA, COL_VA, COL_QB, COL_KB, COL_VB, COL_QM, COL_GL = 0, 4, 5, 6, 10, 14, 18, 22
SCALE64 = HEAD_DIM ** -0.5
SCALE128 = MEM_HEAD_DIM ** -0.5
NEG = -1e30

ADAM_LR = 0.001
ADAM_B1 = 0.9
ADAM_B2 = 0.999
ADAM_EPS = 1e-08
ADAM_WD = 0.01
ADAM_STEP = 10

VMEM_LIMIT_BYTES = 56 * 1024 * 1024


def _params(**kw):
    return pltpu.CompilerParams(vmem_limit_bytes=VMEM_LIMIT_BYTES, **kw)


def _dot(a, b):
    return jnp.dot(a, b, preferred_element_type=F32)


def _dot_nt(a, b):
    return lax.dot_general(a, b, (((1,), (1,)), ((), ())), preferred_element_type=F32)


def _dot_tn(a, b):
    return lax.dot_general(a, b, (((0,), (0,)), ((), ())), preferred_element_type=F32)


def _dot_split(x, m):
    hi = x.astype(BF16)
    lo = (x - hi.astype(F32)).astype(BF16)
    return _dot(hi, m) + _dot(lo, m)


def _mesh_pos():
    return lax.axis_index("x"), lax.axis_index("y"), lax.axis_index("c")


class _Hosted:
    def __init__(self, gathers=(), scatters=()):
        self.items = [("g", a) for a in gathers] + [("s", a) for a in scatters]
        self.n = len(self.items)

    def operands(self):
        return [a for _, a in self.items]

    def specs(self):
        return [pl.BlockSpec(memory_space=pl.ANY)] * self.n

    def out_shapes(self):
        return [jax.ShapeDtypeStruct(((N_DEV,) + a.shape) if kind == "g" else a.shape, a.dtype)
                for kind, a in self.items]

    def scratch(self):
        return [pltpu.SemaphoreType.DMA((7 * self.n,)), pltpu.SemaphoreType.DMA((7 * self.n,)),
                pltpu.SemaphoreType.DMA((self.n,))]

    def copies(self, in_refs, out_refs, send_sems, recv_sems, local_sems):
        x, y, c = _mesh_pos()
        me = 4 * x + 2 * y + c
        out = []
        for t, (kind, _) in enumerate(self.items):
            own = in_refs[t] if kind == "g" else in_refs[t].at[me]
            out.append(pltpu.make_async_copy(own, out_refs[t].at[me], local_sems.at[t]))
            for k in range(1, N_DEV):
                px, py, pc = x ^ (k >> 2), y ^ ((k >> 1) & 1), c ^ (k & 1)
                src = in_refs[t] if kind == "g" else in_refs[t].at[4 * px + 2 * py + pc]
                out.append(pltpu.make_async_remote_copy(
                    src_ref=src, dst_ref=out_refs[t].at[me],
                    send_sem=send_sems.at[7 * t + k - 1], recv_sem=recv_sems.at[7 * t + k - 1],
                    device_id=(px, py, pc), device_id_type=pl.DeviceIdType.MESH))
        return out


def _host(body, n_in, n_out, hosted, grid):
    if hosted is None:
        return body
    nc = hosted.n

    def wrapped(*refs):
        ins = refs[:n_in]
        cin = refs[n_in:n_in + nc]
        outs = refs[n_in + nc:n_in + nc + n_out]
        cout = refs[n_in + nc + n_out:n_in + 2 * nc + n_out]
        scratch = refs[n_in + 2 * nc + n_out:len(refs) - 3]
        sems = refs[len(refs) - 3:]
        ids = [pl.program_id(d) for d in range(len(grid))]
        first = functools.reduce(lambda a, b: a & b, [i == 0 for i in ids])
        last = functools.reduce(lambda a, b: a & b, [i == g - 1 for i, g in zip(ids, grid)])

        @pl.when(first)
        def _():
            for cp in hosted.copies(cin, cout, *sems):
                cp.start()

        body(*ins, *outs, *scratch)

        @pl.when(last)
        def _():
            for cp in hosted.copies(cin, cout, *sems):
                cp.wait()

    return wrapped


def _hosted_call(body, grid, in_specs, out_specs, out_shape, scratch_shapes, hosted, name, args):
    n_out = len(out_specs)
    if hosted is None:
        outs = pl.pallas_call(body, grid=grid, in_specs=in_specs, out_specs=out_specs, out_shape=out_shape,
                              scratch_shapes=scratch_shapes, compiler_params=_params(), name=name)(*args)
        return list(outs), []
    outs = pl.pallas_call(
        _host(body, len(in_specs), n_out, hosted, grid),
        grid=grid,
        in_specs=list(in_specs) + hosted.specs(),
        out_specs=list(out_specs) + hosted.specs(),
        out_shape=list(out_shape) + hosted.out_shapes(),
        scratch_shapes=list(scratch_shapes) + hosted.scratch(),
        compiler_params=_params(),
        name=name,
    )(*args, *hosted.operands())
    return list(outs[:n_out]), list(outs[n_out:])


_DIMS = {"nn": (((1,), (0,)), ((), ())), "nt": (((1,), (1,)), ((), ())), "tn": (((0,), (0,)), ((), ()))}


def _matmul(pairs, mode, out_dtype, tm, tn, tk, name, hosted=None):
    a0, b0 = pairs[0]
    if mode == "nn":
        (M, K), N = a0.shape, b0.shape[1]
    elif mode == "nt":
        (M, K), N = a0.shape, b0.shape[0]
    else:
        (K, M), N = a0.shape, b0.shape[1]
    tm, tn, tk = min(tm, M), min(tn, N), min(tk, K)
    assert M % tm == 0 and N % tn == 0 and K % tk == 0, (name, M, N, K, tm, tn, tk)
    nm, nn, nk = M // tm, N // tn, K // tk
    npair = len(pairs)
    dims = _DIMS[mode]

    def body(*refs):
        ab = refs[:2 * npair]
        o_ref = refs[2 * npair]
        acc_ref = refs[2 * npair + 1]
        k = pl.program_id(2)
        part = lax.dot_general(ab[0][...], ab[1][...], dims, preferred_element_type=F32)
        for q in range(1, npair):
            part += lax.dot_general(ab[2 * q][...], ab[2 * q + 1][...], dims, preferred_element_type=F32)
        if nk == 1:
            o_ref[...] = part.astype(o_ref.dtype)
        else:
            @pl.when(k == 0)
            def _():
                acc_ref[...] = part

            @pl.when(k > 0)
            def _():
                acc_ref[...] += part

            @pl.when(k == nk - 1)
            def _():
                o_ref[...] = acc_ref[...].astype(o_ref.dtype)

    if mode == "nn":
        a_spec = pl.BlockSpec((tm, tk), lambda n, m, k: (m, k))
        b_spec = pl.BlockSpec((tk, tn), lambda n, m, k: (k, n))
    elif mode == "nt":
        a_spec = pl.BlockSpec((tm, tk), lambda n, m, k: (m, k))
        b_spec = pl.BlockSpec((tn, tk), lambda n, m, k: (n, k))
    else:
        a_spec = pl.BlockSpec((tk, tm), lambda n, m, k: (k, m))
        b_spec = pl.BlockSpec((tk, tn), lambda n, m, k: (k, n))
    args = [t for pr in pairs for t in pr]
    outs, moved = _hosted_call(
        body, (nn, nm, nk), [a_spec, b_spec] * npair, [pl.BlockSpec((tm, tn), lambda n, m, k: (m, n))],
        [jax.ShapeDtypeStruct((M, N), out_dtype)], [pltpu.VMEM((tm, tn) if nk > 1 else (8, 128), F32)],
        hosted, name, args)
    return outs[0] if hosted is None else (outs[0], moved)


def _rms_fwd(x, g, name):
    T, D = x.shape
    tr = min(512, T)

    def body(x_ref, g_ref, u_ref):
        xf = x_ref[...]
        r = lax.rsqrt(jnp.mean(xf * xf, axis=-1, keepdims=True) + EPS)
        u_ref[...] = ((xf * r) * g_ref[...]).astype(u_ref.dtype)

    return pl.pallas_call(
        body,
        grid=(T // tr,),
        in_specs=[pl.BlockSpec((tr, D), lambda i: (i, 0)), pl.BlockSpec((1, D), lambda i: (0, 0))],
        out_specs=pl.BlockSpec((tr, D), lambda i: (i, 0)),
        out_shape=jax.ShapeDtypeStruct((T, D), BF16),
        name=name,
    )(x, g)


def _rms_bwd_terms(xin, g, dy):
    r = lax.rsqrt(jnp.mean(xin * xin, axis=-1, keepdims=True) + EPS)
    xh = xin * r
    dg = jnp.sum(dy * xh, axis=0, keepdims=True)
    dxh = dy * g
    dx = r * (dxh - xh * jnp.mean(dxh * xh, axis=-1, keepdims=True))
    return dx, dg


GATE_TC = 256


def _branch_gate(proj, ys, wbs):
    T = proj.shape[0]
    D = D_MODEL
    tr, tc = 512, GATE_TC
    nc = D // tc
    gl0 = COL_GL * 128 // tc

    def body(ya, yb, yc, wa, wb, wc, g0, g1, g2, merged_ref, pa, pb, pc):
        acc = jnp.zeros((tr, tc), F32)
        for y_ref, w_ref, g_ref, p_ref in ((ya, wa, g0, pa), (yb, wb, g1, pb), (yc, wc, g2, pc)):
            p = _dot(y_ref[...], w_ref[...])
            p_ref[...] = p.astype(p_ref.dtype)
            acc += jax.nn.sigmoid(g_ref[...].astype(F32)) * p
        merged_ref[...] = acc.astype(merged_ref.dtype)

    y_spec = pl.BlockSpec((tr, 512), lambda i, n: (i, 0))
    w_spec = pl.BlockSpec((512, tc), lambda i, n: (0, n))
    o_spec = pl.BlockSpec((tr, tc), lambda i, n: (i, n))
    gl_specs = [pl.BlockSpec((tr, tc), lambda i, n, j=j: (i, gl0 + j * nc + n)) for j in range(3)]
    out = jax.ShapeDtypeStruct((T, D), BF16)
    return pl.pallas_call(
        body,
        grid=(T // tr, nc),
        in_specs=[y_spec] * 3 + [w_spec] * 3 + gl_specs,
        out_specs=[o_spec] * 4,
        out_shape=[out] * 4,
        compiler_params=_params(),
        name="branch_gate",
    )(*ys, *wbs, proj, proj, proj)


def _post_pre(x, mix, g_post, g_pre):
    T, D = x.shape
    tr = 512

    def body(x_ref, mix_ref, gp_ref, gq_ref, h1_ref, u2_ref):
        mx = mix_ref[...]
        r = lax.rsqrt(jnp.mean(mx * mx, axis=-1, keepdims=True) + EPS)
        h1 = x_ref[...] + (mx * r) * gp_ref[...]
        h1_ref[...] = h1
        r2 = lax.rsqrt(jnp.mean(h1 * h1, axis=-1, keepdims=True) + EPS)
        u2_ref[...] = ((h1 * r2) * gq_ref[...]).astype(u2_ref.dtype)

    row = pl.BlockSpec((tr, D), lambda i: (i, 0))
    vec = pl.BlockSpec((1, D), lambda i: (0, 0))
    return pl.pallas_call(
        body,
        grid=(T // tr,),
        in_specs=[row, row, vec, vec],
        out_specs=[row, row],
        out_shape=[jax.ShapeDtypeStruct((T, D), F32), jax.ShapeDtypeStruct((T, D), BF16)],
        compiler_params=_params(),
        name="post_pre",
    )(x, mix, g_post, g_pre)


def _ffn_up(u2, w_gate, w_up):
    T, D = u2.shape
    F = D_FF
    tm, tn = 512, F // 2

    def body(u_ref, wg_ref, wu_ref, a_ref, zg_ref, zu_ref):
        u = u_ref[...]
        zg = _dot(u, wg_ref[...])
        zu = _dot(u, wu_ref[...])
        a_ref[...] = (zg * jax.nn.sigmoid(zg) * zu).astype(a_ref.dtype)
        zg_ref[...] = zg.astype(zg_ref.dtype)
        zu_ref[...] = zu.astype(zu_ref.dtype)

    o_spec = pl.BlockSpec((tm, tn), lambda n, m: (m, n))
    out = jax.ShapeDtypeStruct((T, F), BF16)
    return pl.pallas_call(
        body,
        grid=(F // tn, T // tm),
        in_specs=[pl.BlockSpec((tm, D), lambda n, m: (m, 0)),
                  pl.BlockSpec((D, tn), lambda n, m: (0, n)),
                  pl.BlockSpec((D, tn), lambda n, m: (0, n))],
        out_specs=[o_spec] * 3,
        out_shape=[out] * 3,
        compiler_params=_params(),
        name="ffn_up",
    )(u2, w_gate, w_up)


def _loss_head(ffn, h1, target, g_post):
    T, D = ffn.shape
    tr = 512

    def body(f_ref, h1_ref, t_ref, g_ref, dffn_ref, dh2_ref, loss_ref, dg_ref):
        i = pl.program_id(0)
        f = f_ref[...]
        g = g_ref[...]
        r = lax.rsqrt(jnp.mean(f * f, axis=-1, keepdims=True) + EPS)
        xh = f * r
        err = (h1_ref[...] + xh * g) - t_ref[...]
        part = 0.5 * jnp.sum(jnp.mean(err * err, axis=-1, keepdims=True), axis=0, keepdims=True)
        dh2 = err * (1.0 / D)
        dh2_ref[...] = dh2
        dgp = jnp.sum(dh2 * xh, axis=0, keepdims=True)
        dxh = dh2 * g
        dffn_ref[...] = (r * (dxh - xh * jnp.mean(dxh * xh, axis=-1, keepdims=True))).astype(dffn_ref.dtype)

        @pl.when(i == 0)
        def _():
            loss_ref[...] = jnp.zeros_like(loss_ref)
            dg_ref[...] = jnp.zeros_like(dg_ref)

        loss_ref[...] += jnp.broadcast_to(part, loss_ref.shape)
        dg_ref[...] += dgp

    row = pl.BlockSpec((tr, D), lambda i: (i, 0))
    vec = pl.BlockSpec((1, D), lambda i: (0, 0))
    return pl.pallas_call(
        body,
        grid=(T // tr,),
        in_specs=[row, row, row, vec],
        out_specs=[row, row, pl.BlockSpec((8, 128), lambda i: (0, 0)), vec],
        out_shape=[jax.ShapeDtypeStruct((T, D), BF16), jax.ShapeDtypeStruct((T, D), F32),
                   jax.ShapeDtypeStruct((8, 128), F32), jax.ShapeDtypeStruct((1, D), F32)],
        compiler_params=_params(),
        name="loss_head",
    )(ffn, h1, target, g_post)


def _ffn_down_bwd(dffn, wd, zg, zu):
    T, D = dffn.shape
    F = D_FF
    tm, tn = 512, F // 2

    def body(d_ref, w_ref, zg_ref, zu_ref, dzg_ref, dzu_ref):
        da = _dot_nt(d_ref[...], w_ref[...])
        zg = zg_ref[...].astype(F32)
        zu = zu_ref[...].astype(F32)
        s = jax.nn.sigmoid(zg)
        dzu_ref[...] = (da * (zg * s)).astype(dzu_ref.dtype)
        dzg_ref[...] = (da * zu * (s * (1.0 + zg * (1.0 - s)))).astype(dzg_ref.dtype)

    z_spec = pl.BlockSpec((tm, tn), lambda n, m: (m, n))
    out = jax.ShapeDtypeStruct((T, F), BF16)
    return pl.pallas_call(
        body,
        grid=(F // tn, T // tm),
        in_specs=[pl.BlockSpec((tm, D), lambda n, m: (m, 0)), pl.BlockSpec((tn, D), lambda n, m: (n, 0)),
                  z_spec, z_spec],
        out_specs=[z_spec, z_spec],
        out_shape=[out, out],
        compiler_params=_params(),
        name="ffn_down_bwd",
    )(dffn, wd, zg, zu)


def _mid_bwd(h1, du2, dh2, mix, g_pre, g_post):
    T, D = h1.shape
    tr = 512

    def body(h1_ref, du2_ref, dh2_ref, mix_ref, gq_ref, gp_ref, dh1_ref, dmix_ref, dgq_ref, dgp_ref):
        i = pl.program_id(0)
        dx, dgq = _rms_bwd_terms(h1_ref[...], gq_ref[...], du2_ref[...])
        dh1 = dh2_ref[...] + dx
        dh1_ref[...] = dh1
        dmix, dgp = _rms_bwd_terms(mix_ref[...], gp_ref[...], dh1)
        dmix_ref[...] = dmix.astype(dmix_ref.dtype)

        @pl.when(i == 0)
        def _():
            dgq_ref[...] = jnp.zeros_like(dgq_ref)
            dgp_ref[...] = jnp.zeros_like(dgp_ref)

        dgq_ref[...] += dgq
        dgp_ref[...] += dgp

    row = pl.BlockSpec((tr, D), lambda i: (i, 0))
    vec = pl.BlockSpec((1, D), lambda i: (0, 0))
    return pl.pallas_call(
        body,
        grid=(T // tr,),
        in_specs=[row, row, row, row, vec, vec],
        out_specs=[row, row, vec, vec],
        out_shape=[jax.ShapeDtypeStruct((T, D), F32), jax.ShapeDtypeStruct((T, D), BF16),
                   jax.ShapeDtypeStruct((1, D), F32), jax.ShapeDtypeStruct((1, D), F32)],
        compiler_params=_params(),
        name="mid_bwd",
    )(h1, du2, dh2, mix, g_pre, g_post)


def _gate_bwd(dmerged, ps, proj, wbs):
    T, D = dmerged.shape
    tr, tc = 512, GATE_TC
    nc = D // tc
    gl0 = COL_GL * 128 // tc

    def body(dm_ref, pa, pb, pc, g0, g1, g2, wa, wb, wc, dpa, dpb, dpc, dga, dgb, dgc, dya, dyb, dyc,
             acc_a, acc_b, acc_c):
        n = pl.program_id(1)
        dm = dm_ref[...]
        for p_ref, g_ref, w_ref, dp_ref, dg_ref, dy_ref, acc_ref in (
                (pa, g0, wa, dpa, dga, dya, acc_a), (pb, g1, wb, dpb, dgb, dyb, acc_b),
                (pc, g2, wc, dpc, dgc, dyc, acc_c)):
            s = jax.nn.sigmoid(g_ref[...].astype(F32))
            dp = (dm * s).astype(BF16)
            dp_ref[...] = dp
            dg_ref[...] = (dm * p_ref[...].astype(F32) * (s * (1.0 - s))).astype(dg_ref.dtype)
            part = _dot_nt(dp, w_ref[...])

            @pl.when(n == 0)
            def _():
                acc_ref[...] = part

            @pl.when(n > 0)
            def _():
                acc_ref[...] += part

            @pl.when(n == nc - 1)
            def _():
                dy_ref[...] = acc_ref[...].astype(dy_ref.dtype)

    col = pl.BlockSpec((tr, tc), lambda i, n: (i, n))
    y_spec = pl.BlockSpec((tr, 512), lambda i, n: (i, 0))
    w_spec = pl.BlockSpec((512, tc), lambda i, n: (0, n))
    gl_specs = [pl.BlockSpec((tr, tc), lambda i, n, j=j: (i, gl0 + j * nc + n)) for j in range(3)]
    big = jax.ShapeDtypeStruct((T, D), BF16)
    small = jax.ShapeDtypeStruct((T, 512), BF16)
    return pl.pallas_call(
        body,
        grid=(T // tr, nc),
        in_specs=[col] * 4 + gl_specs + [w_spec] * 3,
        out_specs=[col] * 6 + [y_spec] * 3,
        out_shape=[big] * 6 + [small] * 3,
        scratch_shapes=[pltpu.VMEM((tr, 512), F32)] * 3,
        compiler_params=_params(),
        name="gate_bwd",
    )(dmerged, *ps, proj, proj, proj, *wbs)


def _pre_bwd(x, du, dh1, g):
    T, D = x.shape
    tr = 512

    def body(x_ref, du_ref, dh1_ref, g_ref, gx_ref, dg_ref):
        i = pl.program_id(0)
        dx, dg = _rms_bwd_terms(x_ref[...], g_ref[...], du_ref[...])
        gx_ref[...] = dh1_ref[...] + dx

        @pl.when(i == 0)
        def _():
            dg_ref[...] = jnp.zeros_like(dg_ref)

        dg_ref[...] += dg

    row = pl.BlockSpec((tr, D), lambda i: (i, 0))
    vec = pl.BlockSpec((1, D), lambda i: (0, 0))
    return pl.pallas_call(
        body,
        grid=(T // tr,),
        in_specs=[row, row, row, vec],
        out_specs=[row, vec],
        out_shape=[jax.ShapeDtypeStruct((T, D), F32), jax.ShapeDtypeStruct((1, D), F32)],
        compiler_params=_params(),
        name="pre_bwd",
    )(x, du, dh1, g)


def _gain_grad(xin, dy):
    T, D = xin.shape
    tr = min(512, T)

    def body(x_ref, dy_ref, dg_ref):
        i = pl.program_id(0)
        xf = x_ref[...]
        r = lax.rsqrt(jnp.mean(xf * xf, axis=-1, keepdims=True) + EPS)

        @pl.when(i == 0)
        def _():
            dg_ref[...] = jnp.zeros_like(dg_ref)

        dg_ref[...] += jnp.sum(dy_ref[...] * (xf * r), axis=0, keepdims=True)

    row = pl.BlockSpec((tr, D), lambda i: (i, 0))
    return pl.pallas_call(
        body,
        grid=(T // tr,),
        in_specs=[row, row],
        out_specs=pl.BlockSpec((1, D), lambda i: (0, 0)),
        out_shape=jax.ShapeDtypeStruct((1, D), F32),
        name="gain_grad",
    )(xin, dy)


def _swa_buckets():
    dist = (np.arange(BLOCK)[:, None] + BLOCK) - np.arange(2 * BLOCK)[None, :]
    max_exact = N_BUCKETS // 2
    d = np.maximum(dist, 0)
    df = np.maximum(d, 1).astype(np.float32)
    large = max_exact + (np.log(df / np.float32(max_exact)) / np.float32(math.log(MAX_DISTANCE / max_exact))
                         * np.float32(N_BUCKETS - max_exact)).astype(np.int32)
    large = np.minimum(large, N_BUCKETS - 1)
    bucket = np.where(d < max_exact, d, large)
    in_win = (dist >= 0) & (dist < SWA_WINDOW)
    return np.where(in_win, bucket, -1).astype(np.int32)


def _swa_bias_table(rel_bias, buckets):
    H = SWA_Q_HEADS

    def body(rb_ref, bk_ref, o_ref):
        bk = bk_ref[...]
        for h in range(H):
            acc = jnp.full(bk.shape, NEG, F32)
            for b in range(N_BUCKETS):
                acc = jnp.where(bk == b, rb_ref[b, h], acc)
            o_ref[h] = acc

    return pl.pallas_call(
        body,
        in_specs=[pl.BlockSpec(memory_space=pltpu.SMEM), pl.BlockSpec(memory_space=pltpu.VMEM)],
        out_specs=pl.BlockSpec(memory_space=pltpu.VMEM),
        out_shape=jax.ShapeDtypeStruct((H, BLOCK, 2 * BLOCK), F32),
        name="swa_bias_table",
    )(rel_bias, buckets)


def _swa_bias_grad(dbias, buckets):
    H = SWA_Q_HEADS

    def body(db_ref, bk_ref, o_ref):
        bk = bk_ref[...]
        rows = lax.broadcasted_iota(jnp.int32, (N_BUCKETS, 128), 0)
        lanes = lax.broadcasted_iota(jnp.int32, (N_BUCKETS, 128), 1)
        acc = jnp.zeros((N_BUCKETS, 128), F32)
        for h in range(H):
            d = db_ref[h]
            for b in range(N_BUCKETS):
                s = jnp.sum(jnp.sum(jnp.where(bk == b, d, 0.0), axis=1, keepdims=True), axis=0, keepdims=True)
                acc = jnp.where((rows == b) & (lanes == h), s, acc)
        o_ref[...] = acc

    return pl.pallas_call(
        body,
        in_specs=[pl.BlockSpec(memory_space=pltpu.VMEM)] * 2,
        out_specs=pl.BlockSpec(memory_space=pltpu.VMEM),
        out_shape=jax.ShapeDtypeStruct((N_BUCKETS, 128), F32),
        name="swa_bias_grad",
    )(dbias, buckets)


def _swa_scores(qs, kp, kc, bias, sink, first):
    sp = _dot_nt(qs, kp) * SCALE64 + bias[:, :BLOCK]
    sp = jnp.where(first, NEG, sp)
    sc = _dot_nt(qs, kc) * SCALE64 + bias[:, BLOCK:]
    m = jnp.maximum(jnp.maximum(jnp.max(sp, axis=1, keepdims=True), jnp.max(sc, axis=1, keepdims=True)), sink)
    pp = jnp.exp(sp - m)
    pc = jnp.exp(sc - m)
    ps = jnp.exp(sink - m)
    den = jnp.sum(pp, axis=1, keepdims=True) + jnp.sum(pc, axis=1, keepdims=True) + ps
    return pp / den, pc / den, ps / den


def _swa_head(q, do, p, hh):
    kvh = p // 2
    qp = q[:, p * BLOCK:(p + 1) * BLOCK]
    qs = qp if kvh == hh else pltpu.roll(qp, HEAD_DIM, 1)
    if do is None:
        return kvh, qs.astype(BF16), None
    dp = do[:, p * BLOCK:(p + 1) * BLOCK]
    ds = dp if kvh == hh else pltpu.roll(dp, HEAD_DIM, 1)
    return kvh, qs.astype(BF16), ds.astype(BF16)


def _swa_fwd(proj, sinks, bias_tab, B, S, hosted=None):
    nb = S // BLOCK
    T = B * S

    def body(sink_ref, q_ref, kp_ref, kc_ref, vp_ref, vc_ref, bias_ref, o_ref):
        i = pl.program_id(1)
        lane = lax.broadcasted_iota(jnp.int32, (BLOCK, BLOCK), 1)
        first = jnp.full((BLOCK, BLOCK), i, jnp.int32) == 0
        kvm = [(lane // HEAD_DIM) == g for g in range(2)]
        kp = [jnp.where(m, kp_ref[...], 0) for m in kvm]
        kc = [jnp.where(m, kc_ref[...], 0) for m in kvm]
        vp = [jnp.where(m, vp_ref[...], 0) for m in kvm]
        vc = [jnp.where(m, vc_ref[...], 0) for m in kvm]
        q = q_ref[...].astype(F32)
        for p in range(4):
            acc = jnp.zeros((BLOCK, BLOCK), F32)
            for hh in range(2):
                g, qs, _ = _swa_head(q, None, p, hh)
                wp, wc, _ = _swa_scores(qs, kp[g], kc[g], bias_ref[2 * p + hh], sink_ref[0, 2 * p + hh], first)
                o = _dot(wp.astype(BF16), vp[g]) + _dot(wc.astype(BF16), vc[g])
                acc += o if g == hh else pltpu.roll(o, HEAD_DIM, 1)
            o_ref[:, p * BLOCK:(p + 1) * BLOCK] = acc.astype(o_ref.dtype)

    blk = (BLOCK, BLOCK)
    wide = (BLOCK, 4 * BLOCK)
    outs, moved = _hosted_call(
        body, (B, nb),
        [pl.BlockSpec(memory_space=pltpu.SMEM),
         pl.BlockSpec(wide, lambda b, i: (b * nb + i, COL_QA // 4)),
         pl.BlockSpec(blk, lambda b, i: (b * nb + jnp.maximum(i - 1, 0), COL_KA)),
         pl.BlockSpec(blk, lambda b, i: (b * nb + i, COL_KA)),
         pl.BlockSpec(blk, lambda b, i: (b * nb + jnp.maximum(i - 1, 0), COL_VA)),
         pl.BlockSpec(blk, lambda b, i: (b * nb + i, COL_VA)),
         pl.BlockSpec((SWA_Q_HEADS, BLOCK, 2 * BLOCK), lambda b, i: (0, 0, 0))],
        [pl.BlockSpec(wide, lambda b, i: (b * nb + i, 0))],
        [jax.ShapeDtypeStruct((T, 512), BF16)], [], hosted, "swa_fwd",
        (sinks, proj, proj, proj, proj, proj, bias_tab))
    return outs[0], moved


def _swa_bwd(proj, dy, sinks, bias_tab, B, S):
    nb = S // BLOCK
    T = B * S
    H = SWA_Q_HEADS

    def body(sink_ref, q_ref, kp_ref, kc_ref, vp_ref, vc_ref, do_ref, bias_ref,
             dq_ref, dk_ref, dv_ref, dbias_ref, dsink_ref):
        b = pl.program_id(0)
        i = pl.program_id(1)

        @pl.when((b == 0) & (i == 0))
        def _():
            dbias_ref[...] = jnp.zeros_like(dbias_ref)
            dsink_ref[...] = jnp.zeros_like(dsink_ref)

        @pl.when(i == 0)
        def _():
            dk_ref[...] = jnp.zeros_like(dk_ref)
            dv_ref[...] = jnp.zeros_like(dv_ref)

        lane = lax.broadcasted_iota(jnp.int32, (BLOCK, BLOCK), 1)
        first = jnp.full((BLOCK, BLOCK), i, jnp.int32) == 0
        kvm = [(lane // HEAD_DIM) == g for g in range(2)]
        kp = [jnp.where(m, kp_ref[...], 0) for m in kvm]
        kc = [jnp.where(m, kc_ref[...], 0) for m in kvm]
        vp = [jnp.where(m, vp_ref[...], 0) for m in kvm]
        vc = [jnp.where(m, vc_ref[...], 0) for m in kvm]
        q = q_ref[...].astype(F32)
        do = do_ref[...].astype(F32)
        zero = jnp.zeros((BLOCK, BLOCK), F32)
        dkp, dkc, dvp, dvc = [zero, zero], [zero, zero], [zero, zero], [zero, zero]
        for p in range(4):
            dq = zero
            for hh in range(2):
                h = 2 * p + hh
                g, qs, dos = _swa_head(q, do, p, hh)
                wp, wc, ws = _swa_scores(qs, kp[g], kc[g], bias_ref[h], sink_ref[0, h], first)
                dwp = _dot_nt(dos, vp[g])
                dwc = _dot_nt(dos, vc[g])
                dsum = jnp.sum(wp * dwp, axis=1, keepdims=True) + jnp.sum(wc * dwc, axis=1, keepdims=True)
                dsp = wp * (dwp - dsum)
                dsc = wc * (dwc - dsum)
                dsink = jnp.sum(-ws * dsum, axis=0, keepdims=True)
                dsink_ref[h] += jnp.broadcast_to(dsink, (8, 128))
                dbias_ref[h, :, :BLOCK] += dsp
                dbias_ref[h, :, BLOCK:] += dsc
                dspb = dsp.astype(BF16)
                dscb = dsc.astype(BF16)
                dqh = _dot(dspb, kp[g]) + _dot(dscb, kc[g])
                dq += dqh if g == hh else pltpu.roll(dqh, HEAD_DIM, 1)
                dkp[g] += _dot_tn(dspb, qs)
                dkc[g] += _dot_tn(dscb, qs)
                dvp[g] += _dot_tn(wp.astype(BF16), dos)
                dvc[g] += _dot_tn(wc.astype(BF16), dos)
            dq_ref[:, p * BLOCK:(p + 1) * BLOCK] = (dq * SCALE64).astype(dq_ref.dtype)
        cur = pl.ds(pl.multiple_of(i * BLOCK, BLOCK), BLOCK)
        prev = pl.ds(pl.multiple_of(jnp.maximum(i - 1, 0) * BLOCK, BLOCK), BLOCK)
        dk_ref[prev, :] += (jnp.where(kvm[0], dkp[0], 0.0) + jnp.where(kvm[1], dkp[1], 0.0)) * SCALE64
        dk_ref[cur, :] += (jnp.where(kvm[0], dkc[0], 0.0) + jnp.where(kvm[1], dkc[1], 0.0)) * SCALE64
        dv_ref[prev, :] += jnp.where(kvm[0], dvp[0], 0.0) + jnp.where(kvm[1], dvp[1], 0.0)
        dv_ref[cur, :] += jnp.where(kvm[0], dvc[0], 0.0) + jnp.where(kvm[1], dvc[1], 0.0)

    blk = (BLOCK, BLOCK)
    wide = (BLOCK, 4 * BLOCK)
    kv_out = pl.BlockSpec((S, BLOCK), lambda b, i: (b, 0))
    full_bias = pl.BlockSpec((H, BLOCK, 2 * BLOCK), lambda b, i: (0, 0, 0))
    return pl.pallas_call(
        body,
        grid=(B, nb),
        in_specs=[pl.BlockSpec(memory_space=pltpu.SMEM),
                  pl.BlockSpec(wide, lambda b, i: (b * nb + i, COL_QA // 4)),
                  pl.BlockSpec(blk, lambda b, i: (b * nb + jnp.maximum(i - 1, 0), COL_KA)),
                  pl.BlockSpec(blk, lambda b, i: (b * nb + i, COL_KA)),
                  pl.BlockSpec(blk, lambda b, i: (b * nb + jnp.maximum(i - 1, 0), COL_VA)),
                  pl.BlockSpec(blk, lambda b, i: (b * nb + i, COL_VA)),
                  pl.BlockSpec(wide, lambda b, i: (b * nb + i, 0)),
                  full_bias],
        out_specs=[pl.BlockSpec(wide, lambda b, i: (b * nb + i, 0)),
                   kv_out, kv_out, full_bias,
                   pl.BlockSpec((H, 8, 128), lambda b, i: (0, 0, 0))],
        out_shape=[jax.ShapeDtypeStruct((T, 512), BF16),
                   jax.ShapeDtypeStruct((T, BLOCK), F32), jax.ShapeDtypeStruct((T, BLOCK), F32),
                   jax.ShapeDtypeStruct((H, BLOCK, 2 * BLOCK), F32), jax.ShapeDtypeStruct((H, 8, 128), F32)],
        compiler_params=_params(),
        name="swa_bwd",
    )(sinks, proj, proj, proj, proj, proj, dy, bias_tab)


SB_TILE = 256


def _sb_logits(qh, kj, tri):
    z = _dot_nt(qh, kj)
    sp = jnp.log(1.0 + jnp.exp(-jnp.abs(z)))
    ls = jnp.minimum(z, 0.0) - sp
    l1m = -jnp.maximum(z, 0.0) - sp
    if tri is not None:
        l1m = jnp.where(tri, l1m, 0.0)
    return ls, l1m


def _sb_masks():
    lane = lax.broadcasted_iota(jnp.int32, (SB_TILE, BLOCK), 1)
    hm = (lane < HEAD_DIM, lane >= HEAD_DIM)
    row = lax.broadcasted_iota(jnp.int32, (SB_TILE, SB_TILE), 0)
    col = lax.broadcasted_iota(jnp.int32, (SB_TILE, SB_TILE), 1)
    return hm, row, col


def _sb_fwd(proj, B, S, hosted=None):
    nt = S // SB_TILE
    T = B * S

    def body(q_ref, k_ref, v_ref, o_ref, r_ref):
        i = pl.program_id(2)
        hm, row, col = _sb_masks()
        tri = col < row
        later = (row > col).astype(BF16)
        q = q_ref[...] * SCALE64
        qh = [jnp.where(hm[h], q, 0) for h in range(2)]

        def tile(j, carry, mask):
            acc, c0, c1 = carry
            rows = pl.ds(pl.multiple_of(j * SB_TILE, SB_TILE), SB_TILE)
            kj = k_ref[rows, :]
            vj = v_ref[rows, :]
            cs = [c0, c1]
            for h in range(2):
                ls, l1m = _sb_logits(qh[h], kj, mask)
                a = jnp.exp(ls + cs[h] + _dot_split(l1m, later))
                if mask is not None:
                    a = jnp.where(mask, a, 0.0)
                acc += _dot(a.astype(BF16), jnp.where(hm[h], vj, 0))
                cs[h] = cs[h] + jnp.sum(l1m, axis=1, keepdims=True)
            return acc, cs[0], cs[1]

        zero = jnp.zeros((SB_TILE, 1), F32)
        carry = tile(i, (jnp.zeros((SB_TILE, BLOCK), F32), zero, zero), tri)
        acc, c0, c1 = lax.fori_loop(0, i, lambda it, c: tile(i - 1 - it, c, None), carry)
        o_ref[...] = acc.astype(o_ref.dtype)
        r_ref[...] = jnp.where(hm[0], c0, c1)

    blk = (SB_TILE, BLOCK)
    outs, moved = _hosted_call(
        body, (B, 4, nt),
        [pl.BlockSpec(blk, lambda b, p, i: (b * nt + i, COL_QB + p)),
         pl.BlockSpec((S, BLOCK), lambda b, p, i: (b, COL_KB + p)),
         pl.BlockSpec((S, BLOCK), lambda b, p, i: (b, COL_VB + p))],
        [pl.BlockSpec(blk, lambda b, p, i: (b * nt + i, p))] * 2,
        [jax.ShapeDtypeStruct((T, 512), BF16), jax.ShapeDtypeStruct((T, 512), F32)], [], hosted, "sb_fwd",
        (proj, proj, proj))
    return outs[0], outs[1], moved


def _sb_bwd(proj, dy, rtot, B, S, hosted=None):
    nt = S // SB_TILE
    T = B * S

    def body(q_ref, k_ref, v_ref, do_ref, r_ref, dq_ref, dk_ref, dv_ref):
        i = pl.program_id(2)

        @pl.when(i == 0)
        def _():
            dk_ref[...] = jnp.zeros_like(dk_ref)
            dv_ref[...] = jnp.zeros_like(dv_ref)

        hm, row, col = _sb_masks()
        tri = col < row
        later = (row > col).astype(BF16)
        earlier = (row < col).astype(BF16)
        q = q_ref[...] * SCALE64
        do = do_ref[...]
        r = r_ref[...]
        lane = lax.broadcasted_iota(jnp.int32, (SB_TILE, BLOCK), 1)
        qh = [jnp.where(hm[h], q, 0) for h in range(2)]
        doh = [jnp.where(hm[h], do, 0) for h in range(2)]
        rh = [jnp.sum(jnp.where(lane == h * HEAD_DIM, r, 0.0), axis=1, keepdims=True) for h in range(2)]

        def tile(j, carry, mask):
            dq, l0, l1, p0, p1 = carry
            rows = pl.ds(pl.multiple_of(j * SB_TILE, SB_TILE), SB_TILE)
            kj = k_ref[rows, :]
            vj = v_ref[rows, :]
            lsum = [l0, l1]
            psum = [p0, p1]
            dkj = jnp.zeros((SB_TILE, BLOCK), F32)
            dvj = jnp.zeros((SB_TILE, BLOCK), F32)
            for h in range(2):
                ls, l1m = _sb_logits(qh[h], kj, mask)
                sig = jnp.exp(ls)
                lsum[h] = lsum[h] + jnp.sum(l1m, axis=1, keepdims=True)
                a = jnp.exp(ls + (rh[h] - lsum[h]) + _dot_split(l1m, later))
                if mask is not None:
                    a = jnp.where(mask, a, 0.0)
                de = _dot_nt(doh[h], vj) * a
                pre = psum[h] + _dot_split(de, earlier)
                dz = de * (1.0 - sig) - sig * pre
                if mask is not None:
                    dz = jnp.where(mask, dz, 0.0)
                dz = dz.astype(BF16)
                psum[h] = psum[h] + jnp.sum(de, axis=1, keepdims=True)
                dq += _dot(dz, jnp.where(hm[h], kj, 0))
                dkj += _dot_tn(dz, qh[h])
                dvj += _dot_tn(a.astype(BF16), doh[h])
            dk_ref[rows, :] += dkj
            dv_ref[rows, :] += dvj
            return dq, lsum[0], lsum[1], psum[0], psum[1]

        zero = jnp.zeros((SB_TILE, 1), F32)
        carry = (jnp.zeros((SB_TILE, BLOCK), F32), zero, zero, zero, zero)
        carry = lax.fori_loop(0, i, lambda j, c: tile(j, c, None), carry)
        dq = tile(i, carry, tri)[0]
        dq_ref[...] = (dq * SCALE64).astype(dq_ref.dtype)

    blk = (SB_TILE, BLOCK)
    q_spec = pl.BlockSpec(blk, lambda b, p, i: (b * nt + i, p))
    kv_out = pl.BlockSpec((S, BLOCK), lambda b, p, i: (b, p))
    outs, moved = _hosted_call(
        body, (B, 4, nt),
        [pl.BlockSpec(blk, lambda b, p, i: (b * nt + i, COL_QB + p)),
         pl.BlockSpec((S, BLOCK), lambda b, p, i: (b, COL_KB + p)),
         pl.BlockSpec((S, BLOCK), lambda b, p, i: (b, COL_VB + p)),
         q_spec, q_spec],
        [q_spec, kv_out, kv_out],
        [jax.ShapeDtypeStruct((T, 512), BF16),
         jax.ShapeDtypeStruct((T, 512), F32), jax.ShapeDtypeStruct((T, 512), F32)], [], hosted, "sb_bwd",
        (proj, proj, proj, dy, rtot))
    return outs[0], outs[1], outs[2], moved


def _mem_weights(q, mk):
    z = _dot_nt(q, mk) * SCALE128
    e = jnp.exp(z - jnp.max(z, axis=1, keepdims=True))
    return e / jnp.sum(e, axis=1, keepdims=True)


def _mem_fwd(proj, mkv, B, S, M):
    tq = 512
    nq = S // tq
    T = B * S
    Hm = MEM_HEADS

    def body(q_ref, mk_ref, mv_ref, o_ref):
        w = _mem_weights(q_ref[...], mk_ref[...])
        o_ref[...] = _dot(w.astype(BF16), mv_ref[...]).astype(o_ref.dtype)

    return pl.pallas_call(
        body,
        grid=(B, Hm, nq),
        in_specs=[pl.BlockSpec((tq, 128), lambda b, h, i: (b * nq + i, COL_QM + h)),
                  pl.BlockSpec((M, 128), lambda b, h, i: (b, h)),
                  pl.BlockSpec((M, 128), lambda b, h, i: (b, Hm + h))],
        out_specs=pl.BlockSpec((tq, 128), lambda b, h, i: (b * nq + i, h)),
        out_shape=jax.ShapeDtypeStruct((T, 512), BF16),
        name="mem_fwd",
    )(proj, mkv, mkv)


def _mem_bwd(proj, mkv, dy, B, S, M):
    tq = 512
    nq = S // tq
    T = B * S
    Hm = MEM_HEADS

    def body(q_ref, mk_ref, mv_ref, do_ref, dq_ref, dmk_ref, dmv_ref):
        i = pl.program_id(2)
        q = q_ref[...]
        do = do_ref[...]
        mk = mk_ref[...]
        w = _mem_weights(q, mk)
        dw = _dot_nt(do, mv_ref[...])
        ds = (w * (dw - jnp.sum(w * dw, axis=1, keepdims=True))).astype(BF16)
        dq_ref[...] = (_dot(ds, mk) * SCALE128).astype(dq_ref.dtype)

        @pl.when(i == 0)
        def _():
            dmk_ref[...] = jnp.zeros_like(dmk_ref)
            dmv_ref[...] = jnp.zeros_like(dmv_ref)

        dmk_ref[...] += _dot_tn(ds, q) * SCALE128
        dmv_ref[...] += _dot_tn(w.astype(BF16), do)

    q_spec = pl.BlockSpec((tq, 128), lambda b, h, i: (b * nq + i, h))
    m_out = pl.BlockSpec((M, 128), lambda b, h, i: (b, h))
    return pl.pallas_call(
        body,
        grid=(B, Hm, nq),
        in_specs=[pl.BlockSpec((tq, 128), lambda b, h, i: (b * nq + i, COL_QM + h)),
                  pl.BlockSpec((M, 128), lambda b, h, i: (b, h)),
                  pl.BlockSpec((M, 128), lambda b, h, i: (b, Hm + h)),
                  q_spec],
        out_specs=[q_spec, m_out, m_out],
        out_shape=[jax.ShapeDtypeStruct((T, 512), BF16),
                   jax.ShapeDtypeStruct((B * M, 512), F32), jax.ShapeDtypeStruct((B * M, 512), F32)],
        name="mem_bwd",
    )(proj, mkv, mkv, dy)


def _all_gather(blk, name):
    R, C = blk.shape

    def body(x_ref, out_ref, send_sems, recv_sems, local_sem):
        x, y, c = _mesh_pos()
        me, sibling = (x, y, c), (x, y, 1 - c)
        chips = [(1 - x, y), (x, 1 - y), (1 - x, 1 - y)]

        def slot(px, py, pc):
            return out_ref.at[4 * px + 2 * py + pc]

        def copy(k, block, to, src=None):
            return pltpu.make_async_remote_copy(
                src_ref=slot(*block) if src is None else src, dst_ref=slot(*block),
                send_sem=send_sems.at[k], recv_sem=recv_sems.at[k],
                device_id=to, device_id_type=pl.DeviceIdType.MESH)

        mine = pltpu.make_async_copy(x_ref, slot(*me), local_sem)
        mine.start()
        first = [copy(0, me, sibling, src=x_ref)]
        first += [copy(1 + j, me, (*chip, c), src=x_ref) for j, chip in enumerate(chips)]
        for cp in first:
            cp.start()
        passed = [copy(4 + j, (*chip, c), sibling) for j, chip in enumerate(chips)]
        for j, chip in enumerate(chips):
            copy(1 + j, (*chip, c), me).wait_recv()
            passed[j].start()
        copy(0, sibling, me).wait_recv()
        for j, chip in enumerate(chips):
            copy(4 + j, (*chip, 1 - c), me).wait_recv()
        for cp in first + passed:
            cp.wait_send()
        mine.wait()

    return pl.pallas_call(
        body,
        in_specs=[pl.BlockSpec(memory_space=pl.ANY)],
        out_specs=pl.BlockSpec(memory_space=pl.ANY),
        out_shape=jax.ShapeDtypeStruct((N_DEV, R, C), blk.dtype),
        scratch_shapes=[pltpu.SemaphoreType.DMA((7,)), pltpu.SemaphoreType.DMA((7,)), pltpu.SemaphoreType.DMA],
        name=name,
    )(blk)


def _adamw(parts, w, m, v, name):
    R, C = w.shape
    tr = R
    for cand in (256, 176, 128, 64, 32, 16, 8):
        if R % cand == 0 and cand * C * 4 <= 1024 * 1024:
            tr = cand
            break
    c1 = 1.0 - ADAM_B1 ** ADAM_STEP
    c2 = 1.0 - ADAM_B2 ** ADAM_STEP

    def body(p_ref, w_ref, m_ref, v_ref, g_ref, d_ref, nm_ref, nv_ref):
        g = p_ref[0].astype(F32)
        for d in range(1, N_DEV):
            g = g + p_ref[d].astype(F32)
        nm = ADAM_B1 * m_ref[...] + (1.0 - ADAM_B1) * g
        nv = ADAM_B2 * v_ref[...] + (1.0 - ADAM_B2) * (g * g)
        g_ref[...] = g
        nm_ref[...] = nm
        nv_ref[...] = nv
        d_ref[...] = -ADAM_LR * ((nm / c1) / (jnp.sqrt(nv / c2) + ADAM_EPS) + ADAM_WD * w_ref[...])

    row = pl.BlockSpec((tr, C), lambda i: (i, 0))
    out = jax.ShapeDtypeStruct((R, C), F32)
    return pl.pallas_call(
        body,
        grid=(R // tr,),
        in_specs=[pl.BlockSpec((N_DEV, tr, C), lambda i: (0, i, 0)), row, row, row],
        out_specs=[row] * 4,
        out_shape=[out] * 4,
        compiler_params=_params(),
        name=name,
    )(parts, w, m, v)


def _col_shards(g):
    R, C8 = g.shape
    return g.reshape(R, N_DEV, C8 // N_DEV).transpose(1, 0, 2)


def _row_shards(g):
    R8, C = g.shape
    return g.reshape(N_DEV, R8 // N_DEV, C)


def _cols_full(gathered):
    n, R, C = gathered.shape
    return gathered.transpose(1, 0, 2).reshape(R, n * C)


_BIG = ("w_in", "w_mem_kv", "w_branch_swa", "w_branch_sb", "w_branch_mem", "w_out", "w_gate", "w_up", "w_down")
_COL_SHARDED = ("w_in", "w_branch_swa", "w_branch_sb", "w_branch_mem", "w_gate", "w_up")
_SMALL = ("ln_mix_pre", "ln_mix_post", "swa_sinks", "rel_bias", "ln_mem", "ln_ffn_pre", "ln_ffn_post")
_ORDER = ("ln_mix_pre", "ln_mix_post", "w_in", "swa_sinks", "rel_bias", "ln_mem", "w_mem_kv", "w_branch_swa",
          "w_branch_sb", "w_branch_mem", "w_out", "ln_ffn_pre", "ln_ffn_post", "w_gate", "w_up", "w_down")


def _pack_small(d):
    rows = [d["ln_mix_pre"], d["ln_mix_post"], d["ln_mem"], d["ln_ffn_pre"], d["ln_ffn_post"],
            jnp.pad(d["swa_sinks"].reshape(1, -1), ((0, 0), (0, D_MODEL - SWA_Q_HEADS))),
            jnp.pad(d["rel_bias"].reshape(1, -1), ((0, 0), (0, D_MODEL - N_BUCKETS * SWA_Q_HEADS))),
            jnp.zeros((1, D_MODEL), F32)]
    return jnp.concatenate([r.astype(F32) for r in rows], axis=0)


def _unpack_small(a):
    return dict(ln_mix_pre=a[0:1], ln_mix_post=a[1:2], ln_mem=a[2:3], ln_ffn_pre=a[3:4], ln_ffn_post=a[4:5],
                swa_sinks=a[5:6, :SWA_Q_HEADS],
                rel_bias=a[6, :N_BUCKETS * SWA_Q_HEADS].reshape(N_BUCKETS, SWA_Q_HEADS))


def kernel(x, mem, ln_mix_pre, ln_mix_post, w_in, swa_sinks, rel_bias, ln_mem, w_mem_kv, w_branch_swa, w_branch_sb, w_branch_mem, w_out, ln_ffn_pre, ln_ffn_post, w_gate, w_up, w_down, loss_target, m_ln_mix_pre, m_ln_mix_post, m_w_in, m_swa_sinks, m_rel_bias, m_ln_mem, m_w_mem_kv, m_w_branch_swa, m_w_branch_sb, m_w_branch_mem, m_w_out, m_ln_ffn_pre, m_ln_ffn_post, m_w_gate, m_w_up, m_w_down, v_ln_mix_pre, v_ln_mix_post, v_w_in, v_swa_sinks, v_rel_bias, v_ln_mem, v_w_mem_kv, v_w_branch_swa, v_w_branch_sb, v_w_branch_mem, v_w_out, v_ln_ffn_pre, v_ln_ffn_post, v_w_gate, v_w_up, v_w_down):
    w = dict(ln_mix_pre=ln_mix_pre, ln_mix_post=ln_mix_post, w_in=w_in[0], swa_sinks=swa_sinks, rel_bias=rel_bias,
             ln_mem=ln_mem, w_mem_kv=w_mem_kv[0], w_branch_swa=w_branch_swa[0], w_branch_sb=w_branch_sb[0],
             w_branch_mem=w_branch_mem[0], w_out=w_out[0], ln_ffn_pre=ln_ffn_pre, ln_ffn_post=ln_ffn_post,
             w_gate=w_gate[0], w_up=w_up[0], w_down=w_down[0])
    mom = dict(ln_mix_pre=m_ln_mix_pre, ln_mix_post=m_ln_mix_post, w_in=m_w_in[0], swa_sinks=m_swa_sinks,
               rel_bias=m_rel_bias, ln_mem=m_ln_mem, w_mem_kv=m_w_mem_kv[0], w_branch_swa=m_w_branch_swa[0],
               w_branch_sb=m_w_branch_sb[0], w_branch_mem=m_w_branch_mem[0], w_out=m_w_out[0],
               ln_ffn_pre=m_ln_ffn_pre, ln_ffn_post=m_ln_ffn_post, w_gate=m_w_gate[0], w_up=m_w_up[0],
               w_down=m_w_down[0])
    var = dict(ln_mix_pre=v_ln_mix_pre, ln_mix_post=v_ln_mix_post, w_in=v_w_in[0], swa_sinks=v_swa_sinks,
               rel_bias=v_rel_bias, ln_mem=v_ln_mem, w_mem_kv=v_w_mem_kv[0], w_branch_swa=v_w_branch_swa[0],
               w_branch_sb=v_w_branch_sb[0], w_branch_mem=v_w_branch_mem[0], w_out=v_w_out[0],
               ln_ffn_pre=v_ln_ffn_pre, ln_ffn_post=v_ln_ffn_post, w_gate=v_w_gate[0], w_up=v_w_up[0],
               w_down=v_w_down[0])
    B, S, D = x.shape
    M = mem.shape[1]
    T = B * S
    F = D_FF
    x2 = x.reshape(T, D)
    mem2 = mem.reshape(B * M, D)
    t2 = loss_target.reshape(T, D)
    buckets = jnp.asarray(_swa_buckets())
    wb = {n: w[n].astype(BF16) for n in _BIG}
    full = {}

    def landed(names, got):
        for n, g in zip(names, got):
            full[n] = _cols_full(g) if n in _COL_SHARDED else g.reshape(-1, g.shape[-1])

    def shards(n, g):
        return _col_shards(g) if n in _COL_SHARDED else _row_shards(g)

    landed(["w_in"], [_all_gather(wb["w_in"], "ag_w_in")])
    u = _rms_fwd(x2, ln_mix_pre, "rms_mix_pre")
    early = ["w_mem_kv", "w_branch_swa", "w_branch_sb", "w_branch_mem"]
    proj, got = _matmul([(u, full["w_in"])], "nn", BF16, 512, IN_W // 2, D, "proj_in",
                        hosted=_Hosted(gathers=[wb[n] for n in early]))
    landed(early, got)
    mn = _rms_fwd(mem2, ln_mem, "rms_mem")
    mkv = _matmul([(mn, full["w_mem_kv"])], "nn", BF16, 512, 1024, D, "proj_mem")
    bias_tab = _swa_bias_table(rel_bias, buckets)
    y_swa, got = _swa_fwd(proj, swa_sinks, bias_tab, B, S, hosted=_Hosted(gathers=[wb["w_out"]]))
    landed(["w_out"], got)
    late = ["w_gate", "w_up", "w_down"]
    y_sb, rtot, got = _sb_fwd(proj, B, S, hosted=_Hosted(gathers=[wb[n] for n in late]))
    landed(late, got)
    y_mem = _mem_fwd(proj, mkv, B, S, M)
    wbs = (full["w_branch_swa"], full["w_branch_sb"], full["w_branch_mem"])
    merged, p_swa, p_sb, p_mem = _branch_gate(proj, (y_swa, y_sb, y_mem), wbs)
    mix = _matmul([(merged, full["w_out"])], "nn", F32, 512, 1024, D, "proj_out")
    h1, u2 = _post_pre(x2, mix, ln_mix_post, ln_ffn_pre)
    a, zg, zu = _ffn_up(u2, full["w_gate"], full["w_up"])
    ffn = _matmul([(a, full["w_down"])], "nn", F32, 512, 1024, F, "ffn_down")
    dffn, dh2, loss_tile, d_ln_ffn_post = _loss_head(ffn, h1, t2, ln_ffn_post)
    loss = lax.psum(loss_tile[0, 0], ("x", "y", "c"))

    part = {}
    part["w_down"] = _matmul([(a, dffn)], "tn", BF16, F // 2, 1024, 512, "dw_down")
    dzg, dzu = _ffn_down_bwd(dffn, full["w_down"], zg, zu)
    part["w_gate"] = _matmul([(u2, dzg)], "tn", BF16, 1024, F // 2, 512, "dw_gate")
    part["w_up"] = _matmul([(u2, dzu)], "tn", BF16, 1024, F // 2, 512, "dw_up")
    du2 = _matmul([(dzg, full["w_gate"]), (dzu, full["w_up"])], "nt", F32, 512, 1024, F // 2, "d_u2")
    dh1, dmix, d_ln_ffn_pre, d_ln_mix_post = _mid_bwd(h1, du2, dh2, mix, ln_ffn_pre, ln_mix_post)
    part["w_out"] = _matmul([(merged, dmix)], "tn", BF16, 1024, 1024, 512, "dw_out")
    dmerged = _matmul([(dmix, full["w_out"])], "nt", F32, 512, 1024, D, "d_merged")
    (dp_swa, dp_sb, dp_mem, dg0, dg1, dg2, dy_swa, dy_sb, dy_mem) = _gate_bwd(
        dmerged, (p_swa, p_sb, p_mem), proj, wbs)
    part["w_branch_swa"] = _matmul([(y_swa, dp_swa)], "tn", BF16, 512, 1024, 512, "dw_branch_swa")
    part["w_branch_sb"] = _matmul([(y_sb, dp_sb)], "tn", BF16, 512, 1024, 512, "dw_branch_sb")
    part["w_branch_mem"] = _matmul([(y_mem, dp_mem)], "tn", BF16, 512, 1024, 512, "dw_branch_mem")
    dqa, dka, dva, dbias, dsink = _swa_bwd(proj, dy_swa, swa_sinks, bias_tab, B, S)
    behind_sb = ["w_down", "w_gate", "w_up", "w_out", "w_branch_swa", "w_branch_sb", "w_branch_mem"]
    dqb, dkb, dvb, got = _sb_bwd(proj, dy_sb, rtot, B, S,
                                 hosted=_Hosted(scatters=[shards(n, part[n]) for n in behind_sb]))
    recv = dict(zip(behind_sb, got))
    dqm, dmk, dmv = _mem_bwd(proj, mkv, dy_mem, B, S, M)
    d_rel_bias = _swa_bias_grad(dbias, buckets)[:, :SWA_Q_HEADS]
    d_sinks = dsink[:, 0, 0].reshape(1, SWA_Q_HEADS)
    dmkv = jnp.concatenate([dmk, dmv], axis=1).astype(BF16)
    part["w_mem_kv"] = _matmul([(mn, dmkv)], "tn", BF16, 1024, 1024, 512, "dw_mem_kv")
    dmn = _matmul([(dmkv, full["w_mem_kv"])], "nt", F32, 512, 1024, 1024, "d_mn")
    d_ln_mem = _gain_grad(mem2, dmn)
    dproj = jnp.concatenate([dqa, dka.astype(BF16), dva.astype(BF16), dqb, dkb.astype(BF16), dvb.astype(BF16),
                             dqm, dg0, dg1, dg2], axis=1)
    half = D // 2
    dw_in_a = _matmul([(u[:, :half], dproj)], "tn", BF16, 512, IN_W // 2, 512, "dw_in_a")
    dw_in_b, got_a = _matmul([(u[:, half:], dproj)], "tn", BF16, 512, IN_W // 2, 512, "dw_in_b",
                             hosted=_Hosted(scatters=[_col_shards(dw_in_a)]))
    du, got_b = _matmul([(dproj, full["w_in"])], "nt", F32, 512, 1024, IN_W // 2, "d_u",
                        hosted=_Hosted(scatters=[_col_shards(dw_in_b), _row_shards(part["w_mem_kv"])]))
    recv["w_in"] = jnp.concatenate([got_a[0], got_b[0]], axis=1)
    recv["w_mem_kv"] = got_b[1]
    grad_x, d_ln_mix_pre = _pre_bwd(x2, du, dh1, ln_mix_pre)

    out = {n: _adamw(recv[n], w[n], mom[n], var[n], "adamw_" + n) for n in _BIG}
    small_grads = dict(ln_mix_pre=d_ln_mix_pre, ln_mix_post=d_ln_mix_post, swa_sinks=d_sinks, rel_bias=d_rel_bias,
                       ln_mem=d_ln_mem, ln_ffn_pre=d_ln_ffn_pre, ln_ffn_post=d_ln_ffn_post)
    small_parts = _all_gather(_pack_small(small_grads), "ag_small")
    res = _adamw(small_parts, _pack_small(w), _pack_small(mom), _pack_small(var), "adamw_small")
    small = [_unpack_small(r) for r in res]
    for n in _SMALL:
        out[n] = tuple(s[n] for s in small)

    like = dict(ln_mix_pre=ln_mix_pre, ln_mix_post=ln_mix_post, w_in=w_in, swa_sinks=swa_sinks, rel_bias=rel_bias,
                ln_mem=ln_mem, w_mem_kv=w_mem_kv, w_branch_swa=w_branch_swa, w_branch_sb=w_branch_sb,
                w_branch_mem=w_branch_mem, w_out=w_out, ln_ffn_pre=ln_ffn_pre, ln_ffn_post=ln_ffn_post,
                w_gate=w_gate, w_up=w_up, w_down=w_down)
    result = [loss, grad_x.reshape(B, S, D)]
    for k in range(4):
        result += [out[n][k].reshape(like[n].shape) for n in _ORDER]
    return tuple(result)
```

```python
import functools
import math

import numpy as np
import jax
import jax.numpy as jnp
from jax import lax
from jax.experimental import pallas as pl
from jax.experimental.pallas import tpu as pltpu

F32 = jnp.float32
BF16 = jnp.bfloat16

N_DEV = 8
D_MODEL = 1024
BLOCK = 128
EPS = 1e-6
HEAD_DIM = 64
SWA_Q_HEADS = 8
SWA_WINDOW = 128
N_BUCKETS = 32
MAX_DISTANCE = 128
MEM_HEADS = 4
MEM_HEAD_DIM = 128
D_FF = 2816
IN_W = 5888
COL_QA, COL_KA, COL_VA, COL_QB, COL_KB, COL_VB, COL_QM, COL_GL = 0, 4, 5, 6, 10, 14, 18, 22
SCALE64 = HEAD_DIM ** -0.5
SCALE128 = MEM_HEAD_DIM ** -0.5
NEG = -1e30

ADAM_LR = 0.001
ADAM_B1 = 0.9
ADAM_B2 = 0.999
ADAM_EPS = 1e-08
ADAM_WD = 0.01
ADAM_STEP = 10

VMEM_LIMIT_BYTES = 56 * 1024 * 1024


def _params(**kw):
    return pltpu.CompilerParams(vmem_limit_bytes=VMEM_LIMIT_BYTES, **kw)


def _dot(a, b):
    return jnp.dot(a, b, preferred_element_type=F32)


def _dot_nt(a, b):
    return lax.dot_general(a, b, (((1,), (1,)), ((), ())), preferred_element_type=F32)


def _dot_tn(a, b):
    return lax.dot_general(a, b, (((0,), (0,)), ((), ())), preferred_element_type=F32)


def _dot_split(x, m):
    hi = x.astype(BF16)
    lo = (x - hi.astype(F32)).astype(BF16)
    return _dot(hi, m) + _dot(lo, m)


def _mesh_pos():
    return lax.axis_index("x"), lax.axis_index("y"), lax.axis_index("c")


class _Hosted:
    def __init__(self, gathers=(), scatters=()):
        self.items = [("g", a) for a in gathers] + [("s", a) for a in scatters]
        self.n = len(self.items)

    def operands(self):
        return [a for _, a in self.items]

    def specs(self):
        return [pl.BlockSpec(memory_space=pl.ANY)] * self.n

    def out_shapes(self):
        return [jax.ShapeDtypeStruct(((N_DEV,) + a.shape) if kind == "g" else a.shape, a.dtype)
                for kind, a in self.items]

    def scratch(self):
        return [pltpu.SemaphoreType.DMA((7 * self.n,)), pltpu.SemaphoreType.DMA((7 * self.n,)),
                pltpu.SemaphoreType.DMA((self.n,))]

    def copies(self, in_refs, out_refs, send_sems, recv_sems, local_sems):
        x, y, c = _mesh_pos()
        me = 4 * x + 2 * y + c
        out = []
        for t, (kind, _) in enumerate(self.items):
            own = in_refs[t] if kind == "g" else in_refs[t].at[me]
            out.append(pltpu.make_async_copy(own, out_refs[t].at[me], local_sems.at[t]))
            for k in range(1, N_DEV):
                px, py, pc = x ^ (k >> 2), y ^ ((k >> 1) & 1), c ^ (k & 1)
                src = in_refs[t] if kind == "g" else in_refs[t].at[4 * px + 2 * py + pc]
                out.append(pltpu.make_async_remote_copy(
                    src_ref=src, dst_ref=out_refs[t].at[me],
                    send_sem=send_sems.at[7 * t + k - 1], recv_sem=recv_sems.at[7 * t + k - 1],
                    device_id=(px, py, pc), device_id_type=pl.DeviceIdType.MESH))
        return out


def _host(body, n_in, n_out, hosted, grid):
    if hosted is None:
        return body
    nc = hosted.n

    def wrapped(*refs):
        ins = refs[:n_in]
        cin = refs[n_in:n_in + nc]
        outs = refs[n_in + nc:n_in + nc + n_out]
        cout = refs[n_in + nc + n_out:n_in + 2 * nc + n_out]
        scratch = refs[n_in + 2 * nc + n_out:len(refs) - 3]
        sems = refs[len(refs) - 3:]
        ids = [pl.program_id(d) for d in range(len(grid))]
        first = functools.reduce(lambda a, b: a & b, [i == 0 for i in ids])
        last = functools.reduce(lambda a, b: a & b, [i == g - 1 for i, g in zip(ids, grid)])

        @pl.when(first)
        def _():
            for cp in hosted.copies(cin, cout, *sems):
                cp.start()

        body(*ins, *outs, *scratch)

        @pl.when(last)
        def _():
            for cp in hosted.copies(cin, cout, *sems):
                cp.wait()

    return wrapped


def _hosted_call(body, grid, in_specs, out_specs, out_shape, scratch_shapes, hosted, name, args):
    n_out = len(out_specs)
    if hosted is None:
        outs = pl.pallas_call(body, grid=grid, in_specs=in_specs, out_specs=out_specs, out_shape=out_shape,
                              scratch_shapes=scratch_shapes, compiler_params=_params(), name=name)(*args)
        return list(outs), []
    outs = pl.pallas_call(
        _host(body, len(in_specs), n_out, hosted, grid),
        grid=grid,
        in_specs=list(in_specs) + hosted.specs(),
        out_specs=list(out_specs) + hosted.specs(),
        out_shape=list(out_shape) + hosted.out_shapes(),
        scratch_shapes=list(scratch_shapes) + hosted.scratch(),
        compiler_params=_params(),
        name=name,
    )(*args, *hosted.operands())
    return list(outs[:n_out]), list(outs[n_out:])


_DIMS = {"nn": (((1,), (0,)), ((), ())), "nt": (((1,), (1,)), ((), ())), "tn": (((0,), (0,)), ((), ()))}


def _matmul(pairs, mode, out_dtype, tm, tn, tk, name, hosted=None):
    a0, b0 = pairs[0]
    if mode == "nn":
        (M, K), N = a0.shape, b0.shape[1]
    elif mode == "nt":
        (M, K), N = a0.shape, b0.shape[0]
    else:
        (K, M), N = a0.shape, b0.shape[1]
    tm, tn, tk = min(tm, M), min(tn, N), min(tk, K)
    assert M % tm == 0 and N % tn == 0 and K % tk == 0, (name, M, N, K, tm, tn, tk)
    nm, nn, nk = M // tm, N // tn, K // tk
    npair = len(pairs)
    dims = _DIMS[mode]

    def body(*refs):
        ab = refs[:2 * npair]
        o_ref = refs[2 * npair]
        acc_ref = refs[2 * npair + 1]
        k = pl.program_id(2)
        part = lax.dot_general(ab[0][...], ab[1][...], dims, preferred_element_type=F32)
        for q in range(1, npair):
            part += lax.dot_general(ab[2 * q][...], ab[2 * q + 1][...], dims, preferred_element_type=F32)
        if nk == 1:
            o_ref[...] = part.astype(o_ref.dtype)
        else:
            @pl.when(k == 0)
            def _():
                acc_ref[...] = part

            @pl.when(k > 0)
            def _():
                acc_ref[...] += part

            @pl.when(k == nk - 1)
            def _():
                o_ref[...] = acc_ref[...].astype(o_ref.dtype)

    if mode == "nn":
        a_spec = pl.BlockSpec((tm, tk), lambda n, m, k: (m, k))
        b_spec = pl.BlockSpec((tk, tn), lambda n, m, k: (k, n))
    elif mode == "nt":
        a_spec = pl.BlockSpec((tm, tk), lambda n, m, k: (m, k))
        b_spec = pl.BlockSpec((tn, tk), lambda n, m, k: (n, k))
    else:
        a_spec = pl.BlockSpec((tk, tm), lambda n, m, k: (k, m))
        b_spec = pl.BlockSpec((tk, tn), lambda n, m, k: (k, n))
    args = [t for pr in pairs for t in pr]
    outs, moved = _hosted_call(
        body, (nn, nm, nk), [a_spec, b_spec] * npair, [pl.BlockSpec((tm, tn), lambda n, m, k: (m, n))],
        [jax.ShapeDtypeStruct((M, N), out_dtype)], [pltpu.VMEM((tm, tn) if nk > 1 else (8, 128), F32)],
        hosted, name, args)
    return outs[0] if hosted is None else (outs[0], moved)


def _rms_fwd(x, g, name):
    T, D = x.shape
    tr = min(512, T)

    def body(x_ref, g_ref, u_ref):
        xf = x_ref[...]
        r = lax.rsqrt(jnp.mean(xf * xf, axis=-1, keepdims=True) + EPS)
        u_ref[...] = ((xf * r) * g_ref[...]).astype(u_ref.dtype)

    return pl.pallas_call(
        body,
        grid=(T // tr,),
        in_specs=[pl.BlockSpec((tr, D), lambda i: (i, 0)), pl.BlockSpec((1, D), lambda i: (0, 0))],
        out_specs=pl.BlockSpec((tr, D), lambda i: (i, 0)),
        out_shape=jax.ShapeDtypeStruct((T, D), BF16),
        name=name,
    )(x, g)


def _rms_bwd_terms(xin, g, dy):
    r = lax.rsqrt(jnp.mean(xin * xin, axis=-1, keepdims=True) + EPS)
    xh = xin * r
    dg = jnp.sum(dy * xh, axis=0, keepdims=True)
    dxh = dy * g
    dx = r * (dxh - xh * jnp.mean(dxh * xh, axis=-1, keepdims=True))
    return dx, dg


GATE_TC = 256


def _branch_gate(proj, ys, wbs):
    T = proj.shape[0]
    D = D_MODEL
    tr, tc = 512, GATE_TC
    nc = D // tc
    gl0 = COL_GL * 128 // tc

    def body(ya, yb, yc, wa, wb, wc, g0, g1, g2, merged_ref, pa, pb, pc):
        acc = jnp.zeros((tr, tc), F32)
        for y_ref, w_ref, g_ref, p_ref in ((ya, wa, g0, pa), (yb, wb, g1, pb), (yc, wc, g2, pc)):
            p = _dot(y_ref[...], w_ref[...])
            p_ref[...] = p.astype(p_ref.dtype)
            acc += jax.nn.sigmoid(g_ref[...].astype(F32)) * p
        merged_ref[...] = acc.astype(merged_ref.dtype)

    y_spec = pl.BlockSpec((tr, 512), lambda i, n: (i, 0))
    w_spec = pl.BlockSpec((512, tc), lambda i, n: (0, n))
    o_spec = pl.BlockSpec((tr, tc), lambda i, n: (i, n))
    gl_specs = [pl.BlockSpec((tr, tc), lambda i, n, j=j: (i, gl0 + j * nc + n)) for j in range(3)]
    out = jax.ShapeDtypeStruct((T, D), BF16)
    return pl.pallas_call(
        body,
        grid=(T // tr, nc),
        in_specs=[y_spec] * 3 + [w_spec] * 3 + gl_specs,
        out_specs=[o_spec] * 4,
        out_shape=[out] * 4,
        compiler_params=_params(),
        name="branch_gate",
    )(*ys, *wbs, proj, proj, proj)


def _post_pre(x, mix, g_post, g_pre):
    T, D = x.shape
    tr = 512

    def body(x_ref, mix_ref, gp_ref, gq_ref, h1_ref, u2_ref):
        mx = mix_ref[...]
        r = lax.rsqrt(jnp.mean(mx * mx, axis=-1, keepdims=True) + EPS)
        h1 = x_ref[...] + (mx * r) * gp_ref[...]
        h1_ref[...] = h1
        r2 = lax.rsqrt(jnp.mean(h1 * h1, axis=-1, keepdims=True) + EPS)
        u2_ref[...] = ((h1 * r2) * gq_ref[...]).astype(u2_ref.dtype)

    row = pl.BlockSpec((tr, D), lambda i: (i, 0))
    vec = pl.BlockSpec((1, D), lambda i: (0, 0))
    return pl.pallas_call(
        body,
        grid=(T // tr,),
        in_specs=[row, row, vec, vec],
        out_specs=[row, row],
        out_shape=[jax.ShapeDtypeStruct((T, D), F32), jax.ShapeDtypeStruct((T, D), BF16)],
        compiler_params=_params(),
        name="post_pre",
    )(x, mix, g_post, g_pre)


def _ffn_up(u2, w_gate, w_up):
    T, D = u2.shape
    F = D_FF
    tm, tn = 512, F // 2

    def body(u_ref, wg_ref, wu_ref, a_ref, zg_ref, zu_ref):
        u = u_ref[...]
        zg = _dot(u, wg_ref[...])
        zu = _dot(u, wu_ref[...])
        a_ref[...] = (zg * jax.nn.sigmoid(zg) * zu).astype(a_ref.dtype)
        zg_ref[...] = zg.astype(zg_ref.dtype)
        zu_ref[...] = zu.astype(zu_ref.dtype)

    o_spec = pl.BlockSpec((tm, tn), lambda n, m: (m, n))
    out = jax.ShapeDtypeStruct((T, F), BF16)
    return pl.pallas_call(
        body,
        grid=(F // tn, T // tm),
        in_specs=[pl.BlockSpec((tm, D), lambda n, m: (m, 0)),
                  pl.BlockSpec((D, tn), lambda n, m: (0, n)),
                  pl.BlockSpec((D, tn), lambda n, m: (0, n))],
        out_specs=[o_spec] * 3,
        out_shape=[out] * 3,
        compiler_params=_params(),
        name="ffn_up",
    )(u2, w_gate, w_up)


def _loss_head(ffn, h1, target, g_post):
    T, D = ffn.shape
    tr = 512

    def body(f_ref, h1_ref, t_ref, g_ref, dffn_ref, dh2_ref, loss_ref, dg_ref):
        i = pl.program_id(0)
        f = f_ref[...]
        g = g_ref[...]
        r = lax.rsqrt(jnp.mean(f * f, axis=-1, keepdims=True) + EPS)
        xh = f * r
        err = (h1_ref[...] + xh * g) - t_ref[...]
        part = 0.5 * jnp.sum(jnp.mean(err * err, axis=-1, keepdims=True), axis=0, keepdims=True)
        dh2 = err * (1.0 / D)
        dh2_ref[...] = dh2
        dgp = jnp.sum(dh2 * xh, axis=0, keepdims=True)
        dxh = dh2 * g
        dffn_ref[...] = (r * (dxh - xh * jnp.mean(dxh * xh, axis=-1, keepdims=True))).astype(dffn_ref.dtype)

        @pl.when(i == 0)
        def _():
            loss_ref[...] = jnp.zeros_like(loss_ref)
            dg_ref[...] = jnp.zeros_like(dg_ref)

        loss_ref[...] += jnp.broadcast_to(part, loss_ref.shape)
        dg_ref[...] += dgp

    row = pl.BlockSpec((tr, D), lambda i: (i, 0))
    vec = pl.BlockSpec((1, D), lambda i: (0, 0))
    return pl.pallas_call(
        body,
        grid=(T // tr,),
        in_specs=[row, row, row, vec],
        out_specs=[row, row, pl.BlockSpec((8, 128), lambda i: (0, 0)), vec],
        out_shape=[jax.ShapeDtypeStruct((T, D), BF16), jax.ShapeDtypeStruct((T, D), F32),
                   jax.ShapeDtypeStruct((8, 128), F32), jax.ShapeDtypeStruct((1, D), F32)],
        compiler_params=_params(),
        name="loss_head",
    )(ffn, h1, target, g_post)


def _ffn_down_bwd(dffn, wd, zg, zu):
    T, D = dffn.shape
    F = D_FF
    tm, tn = 512, F // 2

    def body(d_ref, w_ref, zg_ref, zu_ref, dzg_ref, dzu_ref):
        da = _dot_nt(d_ref[...], w_ref[...])
        zg = zg_ref[...].astype(F32)
        zu = zu_ref[...].astype(F32)
        s = jax.nn.sigmoid(zg)
        dzu_ref[...] = (da * (zg * s)).astype(dzu_ref.dtype)
        dzg_ref[...] = (da * zu * (s * (1.0 + zg * (1.0 - s)))).astype(dzg_ref.dtype)

    z_spec = pl.BlockSpec((tm, tn), lambda n, m: (m, n))
    out = jax.ShapeDtypeStruct((T, F), BF16)
    return pl.pallas_call(
        body,
        grid=(F // tn, T // tm),
        in_specs=[pl.BlockSpec((tm, D), lambda n, m: (m, 0)), pl.BlockSpec((tn, D), lambda n, m: (n, 0)),
                  z_spec, z_spec],
        out_specs=[z_spec, z_spec],
        out_shape=[out, out],
        compiler_params=_params(),
        name="ffn_down_bwd",
    )(dffn, wd, zg, zu)


def _mid_bwd(h1, du2, dh2, mix, g_pre, g_post):
    T, D = h1.shape
    tr = 512

    def body(h1_ref, du2_ref, dh2_ref, mix_ref, gq_ref, gp_ref, dh1_ref, dmix_ref, dgq_ref, dgp_ref):
        i = pl.program_id(0)
        dx, dgq = _rms_bwd_terms(h1_ref[...], gq_ref[...], du2_ref[...])
        dh1 = dh2_ref[...] + dx
        dh1_ref[...] = dh1
        dmix, dgp = _rms_bwd_terms(mix_ref[...], gp_ref[...], dh1)
        dmix_ref[...] = dmix.astype(dmix_ref.dtype)

        @pl.when(i == 0)
        def _():
            dgq_ref[...] = jnp.zeros_like(dgq_ref)
            dgp_ref[...] = jnp.zeros_like(dgp_ref)

        dgq_ref[...] += dgq
        dgp_ref[...] += dgp

    row = pl.BlockSpec((tr, D), lambda i: (i, 0))
    vec = pl.BlockSpec((1, D), lambda i: (0, 0))
    return pl.pallas_call(
        body,
        grid=(T // tr,),
        in_specs=[row, row, row, row, vec, vec],
        out_specs=[row, row, vec, vec],
        out_shape=[jax.ShapeDtypeStruct((T, D), F32), jax.ShapeDtypeStruct((T, D), BF16),
                   jax.ShapeDtypeStruct((1, D), F32), jax.ShapeDtypeStruct((1, D), F32)],
        compiler_params=_params(),
        name="mid_bwd",
    )(h1, du2, dh2, mix, g_pre, g_post)


def _gate_bwd(dmerged, ps, proj, wbs):
    T, D = dmerged.shape
    tr, tc = 512, GATE_TC
    nc = D // tc
    gl0 = COL_GL * 128 // tc

    def body(dm_ref, pa, pb, pc, g0, g1, g2, wa, wb, wc, dpa, dpb, dpc, dga, dgb, dgc, dya, dyb, dyc,
             acc_a, acc_b, acc_c):
        n = pl.program_id(1)
        dm = dm_ref[...]
        for p_ref, g_ref, w_ref, dp_ref, dg_ref, dy_ref, acc_ref in (
                (pa, g0, wa, dpa, dga, dya, acc_a), (pb, g1, wb, dpb, dgb, dyb, acc_b),
                (pc, g2, wc, dpc, dgc, dyc, acc_c)):
            s = jax.nn.sigmoid(g_ref[...].astype(F32))
            dp = (dm * s).astype(BF16)
            dp_ref[...] = dp
            dg_ref[...] = (dm * p_ref[...].astype(F32) * (s * (1.0 - s))).astype(dg_ref.dtype)
            part = _dot_nt(dp, w_ref[...])

            @pl.when(n == 0)
            def _():
                acc_ref[...] = part

            @pl.when(n > 0)
            def _():
                acc_ref[...] += part

            @pl.when(n == nc - 1)
            def _():
                dy_ref[...] = acc_ref[...].astype(dy_ref.dtype)

    col = pl.BlockSpec((tr, tc), lambda i, n: (i, n))
    y_spec = pl.BlockSpec((tr, 512), lambda i, n: (i, 0))
    w_spec = pl.BlockSpec((512, tc), lambda i, n: (0, n))
    gl_specs = [pl.BlockSpec((tr, tc), lambda i, n, j=j: (i, gl0 + j * nc + n)) for j in range(3)]
    big = jax.ShapeDtypeStruct((T, D), BF16)
    small = jax.ShapeDtypeStruct((T, 512), BF16)
    return pl.pallas_call(
        body,
        grid=(T // tr, nc),
        in_specs=[col] * 4 + gl_specs + [w_spec] * 3,
        out_specs=[col] * 6 + [y_spec] * 3,
        out_shape=[big] * 6 + [small] * 3,
        scratch_shapes=[pltpu.VMEM((tr, 512), F32)] * 3,
        compiler_params=_params(),
        name="gate_bwd",
    )(dmerged, *ps, proj, proj, proj, *wbs)


def _pre_bwd(x, du, dh1, g):
    T, D = x.shape
    tr = 512

    def body(x_ref, du_ref, dh1_ref, g_ref, gx_ref, dg_ref):
        i = pl.program_id(0)
        dx, dg = _rms_bwd_terms(x_ref[...], g_ref[...], du_ref[...])
        gx_ref[...] = dh1_ref[...] + dx

        @pl.when(i == 0)
        def _():
            dg_ref[...] = jnp.zeros_like(dg_ref)

        dg_ref[...] += dg

    row = pl.BlockSpec((tr, D), lambda i: (i, 0))
    vec = pl.BlockSpec((1, D), lambda i: (0, 0))
    return pl.pallas_call(
        body,
        grid=(T // tr,),
        in_specs=[row, row, row, vec],
        out_specs=[row, vec],
        out_shape=[jax.ShapeDtypeStruct((T, D), F32), jax.ShapeDtypeStruct((1, D), F32)],
        compiler_params=_params(),
        name="pre_bwd",
    )(x, du, dh1, g)


def _gain_grad(xin, dy):
    T, D = xin.shape
    tr = min(512, T)

    def body(x_ref, dy_ref, dg_ref):
        i = pl.program_id(0)
        xf = x_ref[...]
        r = lax.rsqrt(jnp.mean(xf * xf, axis=-1, keepdims=True) + EPS)

        @pl.when(i == 0)
        def _():
            dg_ref[...] = jnp.zeros_like(dg_ref)

        dg_ref[...] += jnp.sum(dy_ref[...] * (xf * r), axis=0, keepdims=True)

    row = pl.BlockSpec((tr, D), lambda i: (i, 0))
    return pl.pallas_call(
        body,
        grid=(T // tr,),
        in_specs=[row, row],
        out_specs=pl.BlockSpec((1, D), lambda i: (0, 0)),
        out_shape=jax.ShapeDtypeStruct((1, D), F32),
        name="gain_grad",
    )(xin, dy)


def _swa_buckets():
    dist = (np.arange(BLOCK)[:, None] + BLOCK) - np.arange(2 * BLOCK)[None, :]
    max_exact = N_BUCKETS // 2
    d = np.maximum(dist, 0)
    df = np.maximum(d, 1).astype(np.float32)
    large = max_exact + (np.log(df / np.float32(max_exact)) / np.float32(math.log(MAX_DISTANCE / max_exact))
                         * np.float32(N_BUCKETS - max_exact)).astype(np.int32)
    large = np.minimum(large, N_BUCKETS - 1)
    bucket = np.where(d < max_exact, d, large)
    in_win = (dist >= 0) & (dist < SWA_WINDOW)
    return np.where(in_win, bucket, -1).astype(np.int32)


def _swa_bias_table(rel_bias, buckets):
    H = SWA_Q_HEADS

    def body(rb_ref, bk_ref, o_ref):
        bk = bk_ref[...]
        for h in range(H):
            acc = jnp.full(bk.shape, NEG, F32)
            for b in range(N_BUCKETS):
                acc = jnp.where(bk == b, rb_ref[b, h], acc)
            o_ref[h] = acc

    return pl.pallas_call(
        body,
        in_specs=[pl.BlockSpec(memory_space=pltpu.SMEM), pl.BlockSpec(memory_space=pltpu.VMEM)],
        out_specs=pl.BlockSpec(memory_space=pltpu.VMEM),
        out_shape=jax.ShapeDtypeStruct((H, BLOCK, 2 * BLOCK), F32),
        name="swa_bias_table",
    )(rel_bias, buckets)


def _swa_bias_grad(dbias, buckets):
    H = SWA_Q_HEADS

    def body(db_ref, bk_ref, o_ref):
        bk = bk_ref[...]
        rows = lax.broadcasted_iota(jnp.int32, (N_BUCKETS, 128), 0)
        lanes = lax.broadcasted_iota(jnp.int32, (N_BUCKETS, 128), 1)
        acc = jnp.zeros((N_BUCKETS, 128), F32)
        for h in range(H):
            d = db_ref[h]
            for b in range(N_BUCKETS):
                s = jnp.sum(jnp.sum(jnp.where(bk == b, d, 0.0), axis=1, keepdims=True), axis=0, keepdims=True)
                acc = jnp.where((rows == b) & (lanes == h), s, acc)
        o_ref[...] = acc

    return pl.pallas_call(
        body,
        in_specs=[pl.BlockSpec(memory_space=pltpu.VMEM)] * 2,
        out_specs=pl.BlockSpec(memory_space=pltpu.VMEM),
        out_shape=jax.ShapeDtypeStruct((N_BUCKETS, 128), F32),
        name="swa_bias_grad",
    )(dbias, buckets)


SWA_GROUP = 4


def _swa_stack(x, g):
    blocks = []
    for t in range(SWA_GROUP):
        h = SWA_GROUP * g + t
        xp = x[:, (h // 2) * BLOCK:(h // 2 + 1) * BLOCK]
        blocks.append(xp if h % 2 == g else pltpu.roll(xp, HEAD_DIM, 1))
    return jnp.concatenate(blocks, axis=0).astype(BF16)


def _swa_unstack(parts):
    pairs = []
    for p in range(4):
        g = p // 2
        halves = []
        for hh in range(2):
            t = 2 * p + hh - SWA_GROUP * g
            blk = parts[g][t * BLOCK:(t + 1) * BLOCK]
            halves.append(blk if hh == g else pltpu.roll(blk, HEAD_DIM, 1))
        pairs.append(halves[0] + halves[1])
    return jnp.concatenate(pairs, axis=1)


def _swa_group_inputs(g, sink_ref, bias_ref, kv_refs):
    lane = lax.broadcasted_iota(jnp.int32, (BLOCK, BLOCK), 1)
    km = (lane // HEAD_DIM) == g
    heads = range(SWA_GROUP * g, SWA_GROUP * (g + 1))
    bias = jnp.concatenate([bias_ref[h] for h in heads], axis=0)
    sink = jnp.concatenate([jnp.full((BLOCK, 1), sink_ref[0, h], F32) for h in heads], axis=0)
    return bias, sink, [jnp.where(km, r[...], 0) for r in kv_refs], km


def _swa_scores(qs, kp, kc, bias, sink, first):
    sp = _dot_nt(qs, kp) * SCALE64 + bias[:, :BLOCK]
    sp = jnp.where(first, NEG, sp)
    sc = _dot_nt(qs, kc) * SCALE64 + bias[:, BLOCK:]
    m = jnp.maximum(jnp.maximum(jnp.max(sp, axis=1, keepdims=True), jnp.max(sc, axis=1, keepdims=True)), sink)
    pp = jnp.exp(sp - m)
    pc = jnp.exp(sc - m)
    ps = jnp.exp(sink - m)
    den = jnp.sum(pp, axis=1, keepdims=True) + jnp.sum(pc, axis=1, keepdims=True) + ps
    return pp / den, pc / den, ps / den


def _swa_fwd(proj, sinks, bias_tab, B, S, hosted=None):
    nb = S // BLOCK
    T = B * S

    def body(sink_ref, q_ref, kp_ref, kc_ref, vp_ref, vc_ref, bias_ref, o_ref):
        i = pl.program_id(1)
        first = jnp.full((SWA_GROUP * BLOCK, BLOCK), i, jnp.int32) == 0
        q = q_ref[...].astype(F32)
        outs = []
        for g in range(2):
            bias, sink, (kp, kc, vp, vc), _ = _swa_group_inputs(g, sink_ref, bias_ref, (kp_ref, kc_ref, vp_ref, vc_ref))
            wp, wc, _ = _swa_scores(_swa_stack(q, g), kp, kc, bias, sink, first)
            outs.append(_dot(wp.astype(BF16), vp) + _dot(wc.astype(BF16), vc))
        o_ref[...] = _swa_unstack(outs).astype(o_ref.dtype)

    blk = (BLOCK, BLOCK)
    wide = (BLOCK, 4 * BLOCK)
    outs, moved = _hosted_call(
        body, (B, nb),
        [pl.BlockSpec(memory_space=pltpu.SMEM),
         pl.BlockSpec(wide, lambda b, i: (b * nb + i, COL_QA // 4)),
         pl.BlockSpec(blk, lambda b, i: (b * nb + jnp.maximum(i - 1, 0), COL_KA)),
         pl.BlockSpec(blk, lambda b, i: (b * nb + i, COL_KA)),
         pl.BlockSpec(blk, lambda b, i: (b * nb + jnp.maximum(i - 1, 0), COL_VA)),
         pl.BlockSpec(blk, lambda b, i: (b * nb + i, COL_VA)),
         pl.BlockSpec((SWA_Q_HEADS, BLOCK, 2 * BLOCK), lambda b, i: (0, 0, 0))],
        [pl.BlockSpec(wide, lambda b, i: (b * nb + i, 0))],
        [jax.ShapeDtypeStruct((T, 512), BF16)], [], hosted, "swa_fwd",
        (sinks, proj, proj, proj, proj, proj, bias_tab))
    return outs[0], moved


def _swa_bwd(proj, dy, sinks, bias_tab, B, S):
    nb = S // BLOCK
    T = B * S
    H = SWA_Q_HEADS

    def body(sink_ref, q_ref, kp_ref, kc_ref, vp_ref, vc_ref, do_ref, bias_ref,
             dq_ref, dk_ref, dv_ref, dbias_ref, dsink_ref):
        b = pl.program_id(0)
        i = pl.program_id(1)

        @pl.when((b == 0) & (i == 0))
        def _():
            dbias_ref[...] = jnp.zeros_like(dbias_ref)
            dsink_ref[...] = jnp.zeros_like(dsink_ref)

        @pl.when(i == 0)
        def _():
            dk_ref[...] = jnp.zeros_like(dk_ref)
            dv_ref[...] = jnp.zeros_like(dv_ref)

        first = jnp.full((SWA_GROUP * BLOCK, BLOCK), i, jnp.int32) == 0
        q = q_ref[...].astype(F32)
        do = do_ref[...].astype(F32)
        dqs, dsinks, dbs, dkp, dkc, dvp, dvc = [], [], [], [], [], [], []
        for g in range(2):
            bias, sink, (kp, kc, vp, vc), km = _swa_group_inputs(
                g, sink_ref, bias_ref, (kp_ref, kc_ref, vp_ref, vc_ref))
            qs = _swa_stack(q, g)
            dos = _swa_stack(do, g)
            wp, wc, ws = _swa_scores(qs, kp, kc, bias, sink, first)
            dwp = _dot_nt(dos, vp)
            dwc = _dot_nt(dos, vc)
            dsum = jnp.sum(wp * dwp, axis=1, keepdims=True) + jnp.sum(wc * dwc, axis=1, keepdims=True)
            dsp = wp * (dwp - dsum)
            dsc = wc * (dwc - dsum)
            dsk = -ws * dsum
            for t in range(SWA_GROUP):
                rows = slice(t * BLOCK, (t + 1) * BLOCK)
                dsinks.append(jnp.broadcast_to(jnp.sum(dsk[rows], axis=0, keepdims=True), (1, 128)))
                dbs.append(jnp.concatenate([dsp[rows], dsc[rows]], axis=1))
            dspb = dsp.astype(BF16)
            dscb = dsc.astype(BF16)
            dqs.append(_dot(dspb, kp) + _dot(dscb, kc))
            dkp.append(jnp.where(km, _dot_tn(dspb, qs), 0.0))
            dkc.append(jnp.where(km, _dot_tn(dscb, qs), 0.0))
            dvp.append(jnp.where(km, _dot_tn(wp.astype(BF16), dos), 0.0))
            dvc.append(jnp.where(km, _dot_tn(wc.astype(BF16), dos), 0.0))
        dq_ref[...] = (_swa_unstack(dqs) * SCALE64).astype(dq_ref.dtype)
        dsink_ref[...] += jnp.concatenate(dsinks, axis=0)
        for h in range(H):
            dbias_ref[h] += dbs[h]
        cur = pl.ds(pl.multiple_of(i * BLOCK, BLOCK), BLOCK)
        prev = pl.ds(pl.multiple_of(jnp.maximum(i - 1, 0) * BLOCK, BLOCK), BLOCK)
        dk_ref[prev, :] += (dkp[0] + dkp[1]) * SCALE64
        dk_ref[cur, :] += (dkc[0] + dkc[1]) * SCALE64
        dv_ref[prev, :] += dvp[0] + dvp[1]
        dv_ref[cur, :] += dvc[0] + dvc[1]

    blk = (BLOCK, BLOCK)
    wide = (BLOCK, 4 * BLOCK)
    kv_out = pl.BlockSpec((S, BLOCK), lambda b, i: (b, 0))
    full_bias = pl.BlockSpec((H, BLOCK, 2 * BLOCK), lambda b, i: (0, 0, 0))
    return pl.pallas_call(
        body,
        grid=(B, nb),
        in_specs=[pl.BlockSpec(memory_space=pltpu.SMEM),
                  pl.BlockSpec(wide, lambda b, i: (b * nb + i, COL_QA // 4)),
                  pl.BlockSpec(blk, lambda b, i: (b * nb + jnp.maximum(i - 1, 0), COL_KA)),
                  pl.BlockSpec(blk, lambda b, i: (b * nb + i, COL_KA)),
                  pl.BlockSpec(blk, lambda b, i: (b * nb + jnp.maximum(i - 1, 0), COL_VA)),
                  pl.BlockSpec(blk, lambda b, i: (b * nb + i, COL_VA)),
                  pl.BlockSpec(wide, lambda b, i: (b * nb + i, 0)),
                  full_bias],
        out_specs=[pl.BlockSpec(wide, lambda b, i: (b * nb + i, 0)),
                   kv_out, kv_out, full_bias,
                   pl.BlockSpec((H, 128), lambda b, i: (0, 0))],
        out_shape=[jax.ShapeDtypeStruct((T, 512), BF16),
                   jax.ShapeDtypeStruct((T, BLOCK), F32), jax.ShapeDtypeStruct((T, BLOCK), F32),
                   jax.ShapeDtypeStruct((H, BLOCK, 2 * BLOCK), F32), jax.ShapeDtypeStruct((H, 128), F32)],
        compiler_params=_params(),
        name="swa_bwd",
    )(sinks, proj, proj, proj, proj, proj, dy, bias_tab)


SB_TILE = 256


SB_HEADS = 4
SB_LANES = SB_HEADS * HEAD_DIM
SB_ROWS = SB_HEADS * SB_TILE


def _sb_logits(qs, kj, tri):
    z = _dot_nt(qs, kj)
    sp = jnp.log(1.0 + jnp.exp(-jnp.abs(z)))
    ls = jnp.minimum(z, 0.0) - sp
    l1m = ls - z
    if tri is not None:
        l1m = jnp.where(tri, l1m, 0.0)
    return ls, l1m


def _sb_masks():
    lane = lax.broadcasted_iota(jnp.int32, (SB_TILE, SB_LANES), 1)
    hm = [(lane // HEAD_DIM) == h for h in range(SB_HEADS)]
    row = lax.broadcasted_iota(jnp.int32, (SB_ROWS, SB_TILE), 0) % SB_TILE
    col = lax.broadcasted_iota(jnp.int32, (SB_ROWS, SB_TILE), 1)
    return lane, hm, row, col


def _sb_stack(x, hm):
    return jnp.concatenate([jnp.where(m, x, 0) for m in hm], axis=0)


def _sb_unstack(x, hm):
    return sum(jnp.where(m, x[h * SB_TILE:(h + 1) * SB_TILE], 0.0) for h, m in enumerate(hm))


def _sb_fwd(proj, B, S, hosted=None):
    nt = S // SB_TILE
    T = B * S

    def body(q_ref, k_ref, v_ref, o_ref, r_ref):
        i = pl.program_id(2)
        _, hm, row, col = _sb_masks()
        tri = col < row
        later = (row[:SB_TILE] > col[:SB_TILE]).astype(BF16)
        qs = _sb_stack(q_ref[...] * SCALE64, hm)

        def tile(j, carry, mask):
            acc, c = carry
            rows = pl.ds(pl.multiple_of(j * SB_TILE, SB_TILE), SB_TILE)
            ls, l1m = _sb_logits(qs, k_ref[rows, :], mask)
            a = jnp.exp(ls + c + _dot_split(l1m, later))
            if mask is not None:
                a = jnp.where(mask, a, 0.0)
            pv = _dot(a.astype(BF16), v_ref[rows, :])
            return acc + _sb_unstack(pv, hm), c + jnp.sum(l1m, axis=1, keepdims=True)

        carry = tile(i, (jnp.zeros((SB_TILE, SB_LANES), F32), jnp.zeros((SB_ROWS, 1), F32)), tri)
        acc, c = lax.fori_loop(0, i, lambda it, cr: tile(i - 1 - it, cr, None), carry)
        o_ref[...] = acc.astype(o_ref.dtype)
        r_ref[...] = _sb_unstack(jnp.broadcast_to(c, (SB_ROWS, SB_LANES)), hm)

    blk = (SB_TILE, SB_LANES)
    ng = 512 // SB_LANES
    cq, ck, cv = (c * BLOCK // SB_LANES for c in (COL_QB, COL_KB, COL_VB))
    outs, moved = _hosted_call(
        body, (B, ng, nt),
        [pl.BlockSpec(blk, lambda b, p, i: (b * nt + i, cq + p)),
         pl.BlockSpec((S, SB_LANES), lambda b, p, i: (b, ck + p)),
         pl.BlockSpec((S, SB_LANES), lambda b, p, i: (b, cv + p))],
        [pl.BlockSpec(blk, lambda b, p, i: (b * nt + i, p))] * 2,
        [jax.ShapeDtypeStruct((T, 512), BF16), jax.ShapeDtypeStruct((T, 512), F32)], [], hosted, "sb_fwd",
        (proj, proj, proj))
    return outs[0], outs[1], moved


def _sb_bwd(proj, dy, rtot, B, S, hosted=None):
    nt = S // SB_TILE
    T = B * S

    def body(q_ref, k_ref, v_ref, do_ref, r_ref, dq_ref, dk_ref, dv_ref):
        i = pl.program_id(2)

        @pl.when(i == 0)
        def _():
            dk_ref[...] = jnp.zeros_like(dk_ref)
            dv_ref[...] = jnp.zeros_like(dv_ref)

        lane, hm, row, col = _sb_masks()
        tri = col < row
        later = (row[:SB_TILE] > col[:SB_TILE]).astype(BF16)
        earlier = (row[:SB_TILE] < col[:SB_TILE]).astype(BF16)
        qs = _sb_stack(q_ref[...] * SCALE64, hm)
        dos = _sb_stack(do_ref[...], hm)
        r = r_ref[...]
        rs = jnp.concatenate([jnp.sum(jnp.where(lane == h * HEAD_DIM, r, 0.0), axis=1, keepdims=True)
                              for h in range(SB_HEADS)], axis=0)

        def tile(j, carry, mask):
            dq, lsum, psum = carry
            rows = pl.ds(pl.multiple_of(j * SB_TILE, SB_TILE), SB_TILE)
            kj = k_ref[rows, :]
            vj = v_ref[rows, :]
            ls, l1m = _sb_logits(qs, kj, mask)
            sig = jnp.exp(ls)
            lsum = lsum + jnp.sum(l1m, axis=1, keepdims=True)
            a = jnp.exp(ls + (rs - lsum) + _dot_split(l1m, later))
            if mask is not None:
                a = jnp.where(mask, a, 0.0)
            de = _dot_nt(dos, vj) * a
            pre = psum + _dot_split(de, earlier)
            dz = de * (1.0 - sig) - sig * pre
            if mask is not None:
                dz = jnp.where(mask, dz, 0.0)
            dz = dz.astype(BF16)
            dk_ref[rows, :] += _dot_tn(dz, qs)
            dv_ref[rows, :] += _dot_tn(a.astype(BF16), dos)
            return dq + _sb_unstack(_dot(dz, kj), hm), lsum, psum + jnp.sum(de, axis=1, keepdims=True)

        zero = jnp.zeros((SB_ROWS, 1), F32)
        carry = lax.fori_loop(0, i, lambda j, c: tile(j, c, None), (jnp.zeros((SB_TILE, SB_LANES), F32), zero, zero))
        dq = tile(i, carry, tri)[0]
        dq_ref[...] = (dq * SCALE64).astype(dq_ref.dtype)

    blk = (SB_TILE, SB_LANES)
    ng = 512 // SB_LANES
    cq, ck, cv = (c * BLOCK // SB_LANES for c in (COL_QB, COL_KB, COL_VB))
    q_spec = pl.BlockSpec(blk, lambda b, p, i: (b * nt + i, p))
    kv_out = pl.BlockSpec((S, SB_LANES), lambda b, p, i: (b, p))
    outs, moved = _hosted_call(
        body, (B, ng, nt),
        [pl.BlockSpec(blk, lambda b, p, i: (b * nt + i, cq + p)),
         pl.BlockSpec((S, SB_LANES), lambda b, p, i: (b, ck + p)),
         pl.BlockSpec((S, SB_LANES), lambda b, p, i: (b, cv + p)),
         q_spec, q_spec],
        [q_spec, kv_out, kv_out],
        [jax.ShapeDtypeStruct((T, 512), BF16),
         jax.ShapeDtypeStruct((T, 512), F32), jax.ShapeDtypeStruct((T, 512), F32)], [], hosted, "sb_bwd",
        (proj, proj, proj, dy, rtot))
    return outs[0], outs[1], outs[2], moved


def _mem_weights(q, mk):
    z = _dot_nt(q, mk) * SCALE128
    e = jnp.exp(z - jnp.max(z, axis=1, keepdims=True))
    return e / jnp.sum(e, axis=1, keepdims=True)


def _mem_fwd(proj, mkv, B, S, M):
    tq = 512
    nq = S // tq
    T = B * S
    Hm = MEM_HEADS

    def body(q_ref, mk_ref, mv_ref, o_ref):
        w = _mem_weights(q_ref[...], mk_ref[...])
        o_ref[...] = _dot(w.astype(BF16), mv_ref[...]).astype(o_ref.dtype)

    return pl.pallas_call(
        body,
        grid=(B, Hm, nq),
        in_specs=[pl.BlockSpec((tq, 128), lambda b, h, i: (b * nq + i, COL_QM + h)),
                  pl.BlockSpec((M, 128), lambda b, h, i: (b, h)),
                  pl.BlockSpec((M, 128), lambda b, h, i: (b, Hm + h))],
        out_specs=pl.BlockSpec((tq, 128), lambda b, h, i: (b * nq + i, h)),
        out_shape=jax.ShapeDtypeStruct((T, 512), BF16),
        name="mem_fwd",
    )(proj, mkv, mkv)


def _mem_bwd(proj, mkv, dy, B, S, M):
    tq = 512
    nq = S // tq
    T = B * S
    Hm = MEM_HEADS

    def body(q_ref, mk_ref, mv_ref, do_ref, dq_ref, dmk_ref, dmv_ref):
        i = pl.program_id(2)
        q = q_ref[...]
        do = do_ref[...]
        mk = mk_ref[...]
        w = _mem_weights(q, mk)
        dw = _dot_nt(do, mv_ref[...])
        ds = (w * (dw - jnp.sum(w * dw, axis=1, keepdims=True))).astype(BF16)
        dq_ref[...] = (_dot(ds, mk) * SCALE128).astype(dq_ref.dtype)

        @pl.when(i == 0)
        def _():
            dmk_ref[...] = jnp.zeros_like(dmk_ref)
            dmv_ref[...] = jnp.zeros_like(dmv_ref)

        dmk_ref[...] += _dot_tn(ds, q) * SCALE128
        dmv_ref[...] += _dot_tn(w.astype(BF16), do)

    q_spec = pl.BlockSpec((tq, 128), lambda b, h, i: (b * nq + i, h))
    m_out = pl.BlockSpec((M, 128), lambda b, h, i: (b, h))
    return pl.pallas_call(
        body,
        grid=(B, Hm, nq),
        in_specs=[pl.BlockSpec((tq, 128), lambda b, h, i: (b * nq + i, COL_QM + h)),
                  pl.BlockSpec((M, 128), lambda b, h, i: (b, h)),
                  pl.BlockSpec((M, 128), lambda b, h, i: (b, Hm + h)),
                  q_spec],
        out_specs=[q_spec, m_out, m_out],
        out_shape=[jax.ShapeDtypeStruct((T, 512), BF16),
                   jax.ShapeDtypeStruct((B * M, 512), F32), jax.ShapeDtypeStruct((B * M, 512), F32)],
        name="mem_bwd",
    )(proj, mkv, mkv, dy)


def _all_gather(blk, name):
    R, C = blk.shape

    def body(x_ref, out_ref, send_sems, recv_sems, local_sem):
        x, y, c = _mesh_pos()
        me, sibling = (x, y, c), (x, y, 1 - c)
        chips = [(1 - x, y), (x, 1 - y), (1 - x, 1 - y)]

        def slot(px, py, pc):
            return out_ref.at[4 * px + 2 * py + pc]

        def copy(k, block, to, src=None):
            return pltpu.make_async_remote_copy(
                src_ref=slot(*block) if src is None else src, dst_ref=slot(*block),
                send_sem=send_sems.at[k], recv_sem=recv_sems.at[k],
                device_id=to, device_id_type=pl.DeviceIdType.MESH)

        mine = pltpu.make_async_copy(x_ref, slot(*me), local_sem)
        mine.start()
        first = [copy(0, me, sibling, src=x_ref)]
        first += [copy(1 + j, me, (*chip, c), src=x_ref) for j, chip in enumerate(chips)]
        for cp in first:
            cp.start()
        passed = [copy(4 + j, (*chip, c), sibling) for j, chip in enumerate(chips)]
        for j, chip in enumerate(chips):
            copy(1 + j, (*chip, c), me).wait_recv()
            passed[j].start()
        copy(0, sibling, me).wait_recv()
        for j, chip in enumerate(chips):
            copy(4 + j, (*chip, 1 - c), me).wait_recv()
        for cp in first + passed:
            cp.wait_send()
        mine.wait()

    return pl.pallas_call(
        body,
        in_specs=[pl.BlockSpec(memory_space=pl.ANY)],
        out_specs=pl.BlockSpec(memory_space=pl.ANY),
        out_shape=jax.ShapeDtypeStruct((N_DEV, R, C), blk.dtype),
        scratch_shapes=[pltpu.SemaphoreType.DMA((7,)), pltpu.SemaphoreType.DMA((7,)), pltpu.SemaphoreType.DMA],
        name=name,
    )(blk)


def _adamw(parts, w, m, v, name):
    R, C = w.shape
    tr = R
    for cand in (256, 176, 128, 64, 32, 16, 8):
        if R % cand == 0 and cand * C * 4 <= 1024 * 1024:
            tr = cand
            break
    c1 = 1.0 - ADAM_B1 ** ADAM_STEP
    c2 = 1.0 - ADAM_B2 ** ADAM_STEP

    def body(p_ref, w_ref, m_ref, v_ref, g_ref, d_ref, nm_ref, nv_ref):
        g = p_ref[0].astype(F32)
        for d in range(1, N_DEV):
            g = g + p_ref[d].astype(F32)
        nm = ADAM_B1 * m_ref[...] + (1.0 - ADAM_B1) * g
        nv = ADAM_B2 * v_ref[...] + (1.0 - ADAM_B2) * (g * g)
        g_ref[...] = g
        nm_ref[...] = nm
        nv_ref[...] = nv
        d_ref[...] = -ADAM_LR * ((nm / c1) / (jnp.sqrt(nv / c2) + ADAM_EPS) + ADAM_WD * w_ref[...])

    row = pl.BlockSpec((tr, C), lambda i: (i, 0))
    out = jax.ShapeDtypeStruct((R, C), F32)
    return pl.pallas_call(
        body,
        grid=(R // tr,),
        in_specs=[pl.BlockSpec((N_DEV, tr, C), lambda i: (0, i, 0)), row, row, row],
        out_specs=[row] * 4,
        out_shape=[out] * 4,
        compiler_params=_params(),
        name=name,
    )(parts, w, m, v)


def _col_shards(g):
    R, C8 = g.shape
    return g.reshape(R, N_DEV, C8 // N_DEV).transpose(1, 0, 2)


def _row_shards(g):
    R8, C = g.shape
    return g.reshape(N_DEV, R8 // N_DEV, C)


def _cols_full(gathered):
    n, R, C = gathered.shape
    return gathered.transpose(1, 0, 2).reshape(R, n * C)


_BIG = ("w_in", "w_mem_kv", "w_branch_swa", "w_branch_sb", "w_branch_mem", "w_out", "w_gate", "w_up", "w_down")
_COL_SHARDED = ("w_in", "w_branch_swa", "w_branch_sb", "w_branch_mem", "w_gate", "w_up")
_SMALL = ("ln_mix_pre", "ln_mix_post", "swa_sinks", "rel_bias", "ln_mem", "ln_ffn_pre", "ln_ffn_post")
_ORDER = ("ln_mix_pre", "ln_mix_post", "w_in", "swa_sinks", "rel_bias", "ln_mem", "w_mem_kv", "w_branch_swa",
          "w_branch_sb", "w_branch_mem", "w_out", "ln_ffn_pre", "ln_ffn_post", "w_gate", "w_up", "w_down")


def _pack_small(d):
    rows = [d["ln_mix_pre"], d["ln_mix_post"], d["ln_mem"], d["ln_ffn_pre"], d["ln_ffn_post"],
            jnp.pad(d["swa_sinks"].reshape(1, -1), ((0, 0), (0, D_MODEL - SWA_Q_HEADS))),
            jnp.pad(d["rel_bias"].reshape(1, -1), ((0, 0), (0, D_MODEL - N_BUCKETS * SWA_Q_HEADS))),
            jnp.zeros((1, D_MODEL), F32)]
    return jnp.concatenate([r.astype(F32) for r in rows], axis=0)


def _unpack_small(a):
    return dict(ln_mix_pre=a[0:1], ln_mix_post=a[1:2], ln_mem=a[2:3], ln_ffn_pre=a[3:4], ln_ffn_post=a[4:5],
                swa_sinks=a[5:6, :SWA_Q_HEADS],
                rel_bias=a[6, :N_BUCKETS * SWA_Q_HEADS].reshape(N_BUCKETS, SWA_Q_HEADS))


def kernel(x, mem, ln_mix_pre, ln_mix_post, w_in, swa_sinks, rel_bias, ln_mem, w_mem_kv, w_branch_swa, w_branch_sb, w_branch_mem, w_out, ln_ffn_pre, ln_ffn_post, w_gate, w_up, w_down, loss_target, m_ln_mix_pre, m_ln_mix_post, m_w_in, m_swa_sinks, m_rel_bias, m_ln_mem, m_w_mem_kv, m_w_branch_swa, m_w_branch_sb, m_w_branch_mem, m_w_out, m_ln_ffn_pre, m_ln_ffn_post, m_w_gate, m_w_up, m_w_down, v_ln_mix_pre, v_ln_mix_post, v_w_in, v_swa_sinks, v_rel_bias, v_ln_mem, v_w_mem_kv, v_w_branch_swa, v_w_branch_sb, v_w_branch_mem, v_w_out, v_ln_ffn_pre, v_ln_ffn_post, v_w_gate, v_w_up, v_w_down):
    w = dict(ln_mix_pre=ln_mix_pre, ln_mix_post=ln_mix_post, w_in=w_in[0], swa_sinks=swa_sinks, rel_bias=rel_bias,
             ln_mem=ln_mem, w_mem_kv=w_mem_kv[0], w_branch_swa=w_branch_swa[0], w_branch_sb=w_branch_sb[0],
             w_branch_mem=w_branch_mem[0], w_out=w_out[0], ln_ffn_pre=ln_ffn_pre, ln_ffn_post=ln_ffn_post,
             w_gate=w_gate[0], w_up=w_up[0], w_down=w_down[0])
    mom = dict(ln_mix_pre=m_ln_mix_pre, ln_mix_post=m_ln_mix_post, w_in=m_w_in[0], swa_sinks=m_swa_sinks,
               rel_bias=m_rel_bias, ln_mem=m_ln_mem, w_mem_kv=m_w_mem_kv[0], w_branch_swa=m_w_branch_swa[0],
               w_branch_sb=m_w_branch_sb[0], w_branch_mem=m_w_branch_mem[0], w_out=m_w_out[0],
               ln_ffn_pre=m_ln_ffn_pre, ln_ffn_post=m_ln_ffn_post, w_gate=m_w_gate[0], w_up=m_w_up[0],
               w_down=m_w_down[0])
    var = dict(ln_mix_pre=v_ln_mix_pre, ln_mix_post=v_ln_mix_post, w_in=v_w_in[0], swa_sinks=v_swa_sinks,
               rel_bias=v_rel_bias, ln_mem=v_ln_mem, w_mem_kv=v_w_mem_kv[0], w_branch_swa=v_w_branch_swa[0],
               w_branch_sb=v_w_branch_sb[0], w_branch_mem=v_w_branch_mem[0], w_out=v_w_out[0],
               ln_ffn_pre=v_ln_ffn_pre, ln_ffn_post=v_ln_ffn_post, w_gate=v_w_gate[0], w_up=v_w_up[0],
               w_down=v_w_down[0])
    B, S, D = x.shape
    M = mem.shape[1]
    T = B * S
    F = D_FF
    x2 = x.reshape(T, D)
    mem2 = mem.reshape(B * M, D)
    t2 = loss_target.reshape(T, D)
    buckets = jnp.asarray(_swa_buckets())
    wb = {n: w[n].astype(BF16) for n in _BIG}
    full = {}

    def landed(names, got):
        for n, g in zip(names, got):
            full[n] = _cols_full(g) if n in _COL_SHARDED else g.reshape(-1, g.shape[-1])

    def shards(n, g):
        return _col_shards(g) if n in _COL_SHARDED else _row_shards(g)

    landed(["w_in"], [_all_gather(wb["w_in"], "ag_w_in")])
    u = _rms_fwd(x2, ln_mix_pre, "rms_mix_pre")
    early = ["w_mem_kv", "w_branch_swa", "w_branch_sb", "w_branch_mem"]
    proj, got = _matmul([(u, full["w_in"])], "nn", BF16, 512, IN_W // 2, D, "proj_in",
                        hosted=_Hosted(gathers=[wb[n] for n in early]))
    landed(early, got)
    mn = _rms_fwd(mem2, ln_mem, "rms_mem")
    mkv = _matmul([(mn, full["w_mem_kv"])], "nn", BF16, 512, 1024, D, "proj_mem")
    bias_tab = _swa_bias_table(rel_bias, buckets)
    y_swa, got = _swa_fwd(proj, swa_sinks, bias_tab, B, S, hosted=_Hosted(gathers=[wb["w_out"]]))
    landed(["w_out"], got)
    late = ["w_gate", "w_up", "w_down"]
    y_sb, rtot, got = _sb_fwd(proj, B, S, hosted=_Hosted(gathers=[wb[n] for n in late]))
    landed(late, got)
    y_mem = _mem_fwd(proj, mkv, B, S, M)
    wbs = (full["w_branch_swa"], full["w_branch_sb"], full["w_branch_mem"])
    merged, p_swa, p_sb, p_mem = _branch_gate(proj, (y_swa, y_sb, y_mem), wbs)
    mix = _matmul([(merged, full["w_out"])], "nn", F32, 512, 1024, D, "proj_out")
    h1, u2 = _post_pre(x2, mix, ln_mix_post, ln_ffn_pre)
    a, zg, zu = _ffn_up(u2, full["w_gate"], full["w_up"])
    ffn = _matmul([(a, full["w_down"])], "nn", F32, 512, 1024, F, "ffn_down")
    dffn, dh2, loss_tile, d_ln_ffn_post = _loss_head(ffn, h1, t2, ln_ffn_post)
    loss = lax.psum(loss_tile[0, 0], ("x", "y", "c"))

    part = {}
    part["w_down"] = _matmul([(a, dffn)], "tn", BF16, F // 2, 1024, 512, "dw_down")
    dzg, dzu = _ffn_down_bwd(dffn, full["w_down"], zg, zu)
    part["w_gate"] = _matmul([(u2, dzg)], "tn", BF16, 1024, F // 2, 512, "dw_gate")
    part["w_up"] = _matmul([(u2, dzu)], "tn", BF16, 1024, F // 2, 512, "dw_up")
    du2 = _matmul([(dzg, full["w_gate"]), (dzu, full["w_up"])], "nt", F32, 512, 1024, F // 2, "d_u2")
    dh1, dmix, d_ln_ffn_pre, d_ln_mix_post = _mid_bwd(h1, du2, dh2, mix, ln_ffn_pre, ln_mix_post)
    part["w_out"] = _matmul([(merged, dmix)], "tn", BF16, 1024, 1024, 512, "dw_out")
    dmerged = _matmul([(dmix, full["w_out"])], "nt", F32, 512, 1024, D, "d_merged")
    (dp_swa, dp_sb, dp_mem, dg0, dg1, dg2, dy_swa, dy_sb, dy_mem) = _gate_bwd(
        dmerged, (p_swa, p_sb, p_mem), proj, wbs)
    part["w_branch_swa"] = _matmul([(y_swa, dp_swa)], "tn", BF16, 512, 1024, 512, "dw_branch_swa")
    part["w_branch_sb"] = _matmul([(y_sb, dp_sb)], "tn", BF16, 512, 1024, 512, "dw_branch_sb")
    part["w_branch_mem"] = _matmul([(y_mem, dp_mem)], "tn", BF16, 512, 1024, 512, "dw_branch_mem")
    dqa, dka, dva, dbias, dsink = _swa_bwd(proj, dy_swa, swa_sinks, bias_tab, B, S)
    behind_sb = ["w_down", "w_gate", "w_up", "w_out", "w_branch_swa", "w_branch_sb", "w_branch_mem"]
    dqb, dkb, dvb, got = _sb_bwd(proj, dy_sb, rtot, B, S,
                                 hosted=_Hosted(scatters=[shards(n, part[n]) for n in behind_sb]))
    recv = dict(zip(behind_sb, got))
    dqm, dmk, dmv = _mem_bwd(proj, mkv, dy_mem, B, S, M)
    d_rel_bias = _swa_bias_grad(dbias, buckets)[:, :SWA_Q_HEADS]
    d_sinks = dsink[:, 0].reshape(1, SWA_Q_HEADS)
    dmkv = jnp.concatenate([dmk, dmv], axis=1).astype(BF16)
    part["w_mem_kv"] = _matmul([(mn, dmkv)], "tn", BF16, 1024, 1024, 512, "dw_mem_kv")
    dmn = _matmul([(dmkv, full["w_mem_kv"])], "nt", F32, 512, 1024, 1024, "d_mn")
    d_ln_mem = _gain_grad(mem2, dmn)
    dproj = jnp.concatenate([dqa, dka.astype(BF16), dva.astype(BF16), dqb, dkb.astype(BF16), dvb.astype(BF16),
                             dqm, dg0, dg1, dg2], axis=1)
    half = D // 2
    dw_in_a = _matmul([(u[:, :half], dproj)], "tn", BF16, 512, IN_W // 2, 512, "dw_in_a")
    dw_in_b, got_a = _matmul([(u[:, half:], dproj)], "tn", BF16, 512, IN_W // 2, 512, "dw_in_b",
                             hosted=_Hosted(scatters=[_col_shards(dw_in_a)]))
    du, got_b = _matmul([(dproj, full["w_in"])], "nt", F32, 512, 1024, IN_W // 2, "d_u",
                        hosted=_Hosted(scatters=[_col_shards(dw_in_b), _row_shards(part["w_mem_kv"])]))
    recv["w_in"] = jnp.concatenate([got_a[0], got_b[0]], axis=1)
    recv["w_mem_kv"] = got_b[1]
    grad_x, d_ln_mix_pre = _pre_bwd(x2, du, dh1, ln_mix_pre)

    out = {n: _adamw(recv[n], w[n], mom[n], var[n], "adamw_" + n) for n in _BIG}
    small_grads = dict(ln_mix_pre=d_ln_mix_pre, ln_mix_post=d_ln_mix_post, swa_sinks=d_sinks, rel_bias=d_rel_bias,
                       ln_mem=d_ln_mem, ln_ffn_pre=d_ln_ffn_pre, ln_ffn_post=d_ln_ffn_post)
    small_parts = _all_gather(_pack_small(small_grads), "ag_small")
    res = _adamw(small_parts, _pack_small(w), _pack_small(mom), _pack_small(var), "adamw_small")
    small = [_unpack_small(r) for r in res]
    for n in _SMALL:
        out[n] = tuple(s[n] for s in small)

    like = dict(ln_mix_pre=ln_mix_pre, ln_mix_post=ln_mix_post, w_in=w_in, swa_sinks=swa_sinks, rel_bias=rel_bias,
                ln_mem=ln_mem, w_mem_kv=w_mem_kv, w_branch_swa=w_branch_swa, w_branch_sb=w_branch_sb,
                w_branch_mem=w_branch_mem, w_out=w_out, ln_ffn_pre=ln_ffn_pre, ln_ffn_post=ln_ffn_post,
                w_gate=w_gate, w_up=w_up, w_down=w_down)
    result = [loss, grad_x.reshape(B, S, D)]
    for k in range(4):
        result += [out[n][k].reshape(like[n].shape) for n in _ORDER]
    return tuple(result)
```

```python
import functools
import math

import numpy as np
import jax
import jax.numpy as jnp
from jax import lax
from jax.experimental import pallas as pl
from jax.experimental.pallas import tpu as pltpu

F32 = jnp.float32
BF16 = jnp.bfloat16

N_DEV = 8
D_MODEL = 1024
BLOCK = 128
EPS = 1e-6
HEAD_DIM = 64
SWA_Q_HEADS = 8
SWA_WINDOW = 128
N_BUCKETS = 32
MAX_DISTANCE = 128
MEM_HEADS = 4
MEM_HEAD_DIM = 128
D_FF = 2816
IN_W = 5888
COL_QA, COL_KA, COL_VA, COL_QB, COL_KB, COL_VB, COL_QM, COL_GL = 0, 4, 5, 6, 10, 14, 18, 22
SCALE64 = HEAD_DIM ** -0.5
SCALE128 = MEM_HEAD_DIM ** -0.5
NEG = -1e30

ADAM_LR = 0.001
ADAM_B1 = 0.9
ADAM_B2 = 0.999
ADAM_EPS = 1e-08
ADAM_WD = 0.01
ADAM_STEP = 10

VMEM_LIMIT_BYTES = 56 * 1024 * 1024


def _params(**kw):
    return pltpu.CompilerParams(vmem_limit_bytes=VMEM_LIMIT_BYTES, **kw)


def _dot(a, b):
    return jnp.dot(a, b, preferred_element_type=F32)


def _dot_nt(a, b):
    return lax.dot_general(a, b, (((1,), (1,)), ((), ())), preferred_element_type=F32)


def _dot_tn(a, b):
    return lax.dot_general(a, b, (((0,), (0,)), ((), ())), preferred_element_type=F32)


def _dot_split(x, m):
    hi = x.astype(BF16)
    lo = (x - hi.astype(F32)).astype(BF16)
    return _dot(hi, m) + _dot(lo, m)


def _mesh_pos():
    return lax.axis_index("x"), lax.axis_index("y"), lax.axis_index("c")


class _Hosted:
    def __init__(self, gathers=(), scatters=()):
        self.items = [("g", a) for a in gathers] + [("s", a) for a in scatters]
        self.n = len(self.items)

    def operands(self):
        return [a for _, a in self.items]

    def specs(self):
        return [pl.BlockSpec(memory_space=pl.ANY)] * self.n

    def out_shapes(self):
        return [jax.ShapeDtypeStruct(((N_DEV,) + a.shape) if kind == "g" else a.shape, a.dtype)
                for kind, a in self.items]

    def scratch(self):
        return [pltpu.SemaphoreType.DMA((7 * self.n,)), pltpu.SemaphoreType.DMA((7 * self.n,)),
                pltpu.SemaphoreType.DMA((self.n,))]

    def copies(self, in_refs, out_refs, send_sems, recv_sems, local_sems):
        x, y, c = _mesh_pos()
        me = 4 * x + 2 * y + c
        out = []
        for t, (kind, _) in enumerate(self.items):
            own = in_refs[t] if kind == "g" else in_refs[t].at[me]
            out.append(pltpu.make_async_copy(own, out_refs[t].at[me], local_sems.at[t]))
            for k in range(1, N_DEV):
                px, py, pc = x ^ (k >> 2), y ^ ((k >> 1) & 1), c ^ (k & 1)
                src = in_refs[t] if kind == "g" else in_refs[t].at[4 * px + 2 * py + pc]
                out.append(pltpu.make_async_remote_copy(
                    src_ref=src, dst_ref=out_refs[t].at[me],
                    send_sem=send_sems.at[7 * t + k - 1], recv_sem=recv_sems.at[7 * t + k - 1],
                    device_id=(px, py, pc), device_id_type=pl.DeviceIdType.MESH))
        return out


def _host(body, n_in, n_out, hosted, grid):
    if hosted is None:
        return body
    nc = hosted.n

    def wrapped(*refs):
        ins = refs[:n_in]
        cin = refs[n_in:n_in + nc]
        outs = refs[n_in + nc:n_in + nc + n_out]
        cout = refs[n_in + nc + n_out:n_in + 2 * nc + n_out]
        scratch = refs[n_in + 2 * nc + n_out:len(refs) - 3]
        sems = refs[len(refs) - 3:]
        ids = [pl.program_id(d) for d in range(len(grid))]
        first = functools.reduce(lambda a, b: a & b, [i == 0 for i in ids])
        last = functools.reduce(lambda a, b: a & b, [i == g - 1 for i, g in zip(ids, grid)])

        @pl.when(first)
        def _():
            for cp in hosted.copies(cin, cout, *sems):
                cp.start()

        body(*ins, *outs, *scratch)

        @pl.when(last)
        def _():
            for cp in hosted.copies(cin, cout, *sems):
                cp.wait()

    return wrapped


def _hosted_call(body, grid, in_specs, out_specs, out_shape, scratch_shapes, hosted, name, args):
    n_out = len(out_specs)
    if hosted is None:
        outs = pl.pallas_call(body, grid=grid, in_specs=in_specs, out_specs=out_specs, out_shape=out_shape,
                              scratch_shapes=scratch_shapes, compiler_params=_params(), name=name)(*args)
        return list(outs), []
    outs = pl.pallas_call(
        _host(body, len(in_specs), n_out, hosted, grid),
        grid=grid,
        in_specs=list(in_specs) + hosted.specs(),
        out_specs=list(out_specs) + hosted.specs(),
        out_shape=list(out_shape) + hosted.out_shapes(),
        scratch_shapes=list(scratch_shapes) + hosted.scratch(),
        compiler_params=_params(),
        name=name,
    )(*args, *hosted.operands())
    return list(outs[:n_out]), list(outs[n_out:])


_DIMS = {"nn": (((1,), (0,)), ((), ())), "nt": (((1,), (1,)), ((), ())), "tn": (((0,), (0,)), ((), ()))}


def _matmul(pairs, mode, out_dtype, tm, tn, tk, name, hosted=None):
    a0, b0 = pairs[0]
    if mode == "nn":
        (M, K), N = a0.shape, b0.shape[1]
    elif mode == "nt":
        (M, K), N = a0.shape, b0.shape[0]
    else:
        (K, M), N = a0.shape, b0.shape[1]
    tm, tn, tk = min(tm, M), min(tn, N), min(tk, K)
    assert M % tm == 0 and N % tn == 0 and K % tk == 0, (name, M, N, K, tm, tn, tk)
    nm, nn, nk = M // tm, N // tn, K // tk
    npair = len(pairs)
    dims = _DIMS[mode]

    def body(*refs):
        ab = refs[:2 * npair]
        o_ref = refs[2 * npair]
        acc_ref = refs[2 * npair + 1]
        k = pl.program_id(2)
        part = lax.dot_general(ab[0][...], ab[1][...], dims, preferred_element_type=F32)
        for q in range(1, npair):
            part += lax.dot_general(ab[2 * q][...], ab[2 * q + 1][...], dims, preferred_element_type=F32)
        if nk == 1:
            o_ref[...] = part.astype(o_ref.dtype)
        else:
            @pl.when(k == 0)
            def _():
                acc_ref[...] = part

            @pl.when(k > 0)
            def _():
                acc_ref[...] += part

            @pl.when(k == nk - 1)
            def _():
                o_ref[...] = acc_ref[...].astype(o_ref.dtype)

    if mode == "nn":
        a_spec = pl.BlockSpec((tm, tk), lambda n, m, k: (m, k))
        b_spec = pl.BlockSpec((tk, tn), lambda n, m, k: (k, n))
    elif mode == "nt":
        a_spec = pl.BlockSpec((tm, tk), lambda n, m, k: (m, k))
        b_spec = pl.BlockSpec((tn, tk), lambda n, m, k: (n, k))
    else:
        a_spec = pl.BlockSpec((tk, tm), lambda n, m, k: (k, m))
        b_spec = pl.BlockSpec((tk, tn), lambda n, m, k: (k, n))
    args = [t for pr in pairs for t in pr]
    outs, moved = _hosted_call(
        body, (nn, nm, nk), [a_spec, b_spec] * npair, [pl.BlockSpec((tm, tn), lambda n, m, k: (m, n))],
        [jax.ShapeDtypeStruct((M, N), out_dtype)], [pltpu.VMEM((tm, tn) if nk > 1 else (8, 128), F32)],
        hosted, name, args)
    return outs[0] if hosted is None else (outs[0], moved)


def _rms_fwd(x, g, name):
    T, D = x.shape
    tr = min(512, T)

    def body(x_ref, g_ref, u_ref):
        xf = x_ref[...]
        r = lax.rsqrt(jnp.mean(xf * xf, axis=-1, keepdims=True) + EPS)
        u_ref[...] = ((xf * r) * g_ref[...]).astype(u_ref.dtype)

    return pl.pallas_call(
        body,
        grid=(T // tr,),
        in_specs=[pl.BlockSpec((tr, D), lambda i: (i, 0)), pl.BlockSpec((1, D), lambda i: (0, 0))],
        out_specs=pl.BlockSpec((tr, D), lambda i: (i, 0)),
        out_shape=jax.ShapeDtypeStruct((T, D), BF16),
        name=name,
    )(x, g)


def _rms_bwd_terms(xin, g, dy):
    r = lax.rsqrt(jnp.mean(xin * xin, axis=-1, keepdims=True) + EPS)
    xh = xin * r
    dg = jnp.sum(dy * xh, axis=0, keepdims=True)
    dxh = dy * g
    dx = r * (dxh - xh * jnp.mean(dxh * xh, axis=-1, keepdims=True))
    return dx, dg


GATE_TC = 256


def _branch_gate(proj, ys, wbs):
    T = proj.shape[0]
    D = D_MODEL
    tr, tc = min(1024, T), GATE_TC
    nc = D // tc
    gl0 = COL_GL * 128 // tc

    def body(ya, yb, yc, wa, wb, wc, g0, g1, g2, merged_ref, pa, pb, pc):
        acc = jnp.zeros((tr, tc), F32)
        for y_ref, w_ref, g_ref, p_ref in ((ya, wa, g0, pa), (yb, wb, g1, pb), (yc, wc, g2, pc)):
            p = _dot(y_ref[...], w_ref[...])
            p_ref[...] = p.astype(p_ref.dtype)
            acc += jax.nn.sigmoid(g_ref[...].astype(F32)) * p
        merged_ref[...] = acc.astype(merged_ref.dtype)

    y_spec = pl.BlockSpec((tr, 512), lambda i, n: (i, 0))
    w_spec = pl.BlockSpec((512, tc), lambda i, n: (0, n))
    o_spec = pl.BlockSpec((tr, tc), lambda i, n: (i, n))
    gl_specs = [pl.BlockSpec((tr, tc), lambda i, n, j=j: (i, gl0 + j * nc + n)) for j in range(3)]
    out = jax.ShapeDtypeStruct((T, D), BF16)
    return pl.pallas_call(
        body,
        grid=(T // tr, nc),
        in_specs=[y_spec] * 3 + [w_spec] * 3 + gl_specs,
        out_specs=[o_spec] * 4,
        out_shape=[out] * 4,
        compiler_params=_params(),
        name="branch_gate",
    )(*ys, *wbs, proj, proj, proj)


def _post_pre(x, mix, g_post, g_pre):
    T, D = x.shape
    tr = 512

    def body(x_ref, mix_ref, gp_ref, gq_ref, h1_ref, u2_ref):
        mx = mix_ref[...]
        r = lax.rsqrt(jnp.mean(mx * mx, axis=-1, keepdims=True) + EPS)
        h1 = x_ref[...] + (mx * r) * gp_ref[...]
        h1_ref[...] = h1
        r2 = lax.rsqrt(jnp.mean(h1 * h1, axis=-1, keepdims=True) + EPS)
        u2_ref[...] = ((h1 * r2) * gq_ref[...]).astype(u2_ref.dtype)

    row = pl.BlockSpec((tr, D), lambda i: (i, 0))
    vec = pl.BlockSpec((1, D), lambda i: (0, 0))
    return pl.pallas_call(
        body,
        grid=(T // tr,),
        in_specs=[row, row, vec, vec],
        out_specs=[row, row],
        out_shape=[jax.ShapeDtypeStruct((T, D), F32), jax.ShapeDtypeStruct((T, D), BF16)],
        compiler_params=_params(),
        name="post_pre",
    )(x, mix, g_post, g_pre)


def _ffn_up(u2, w_gate, w_up):
    T, D = u2.shape
    F = D_FF
    tm, tn = 512, F // 2

    def body(u_ref, wg_ref, wu_ref, a_ref, zg_ref, zu_ref):
        u = u_ref[...]
        zg = _dot(u, wg_ref[...])
        zu = _dot(u, wu_ref[...])
        a_ref[...] = (zg * jax.nn.sigmoid(zg) * zu).astype(a_ref.dtype)
        zg_ref[...] = zg.astype(zg_ref.dtype)
        zu_ref[...] = zu.astype(zu_ref.dtype)

    o_spec = pl.BlockSpec((tm, tn), lambda n, m: (m, n))
    out = jax.ShapeDtypeStruct((T, F), BF16)
    return pl.pallas_call(
        body,
        grid=(F // tn, T // tm),
        in_specs=[pl.BlockSpec((tm, D), lambda n, m: (m, 0)),
                  pl.BlockSpec((D, tn), lambda n, m: (0, n)),
                  pl.BlockSpec((D, tn), lambda n, m: (0, n))],
        out_specs=[o_spec] * 3,
        out_shape=[out] * 3,
        compiler_params=_params(),
        name="ffn_up",
    )(u2, w_gate, w_up)


def _loss_head(ffn, h1, target, g_post):
    T, D = ffn.shape
    tr = 512

    def body(f_ref, h1_ref, t_ref, g_ref, dffn_ref, dh2_ref, loss_ref, dg_ref):
        i = pl.program_id(0)
        f = f_ref[...]
        g = g_ref[...]
        r = lax.rsqrt(jnp.mean(f * f, axis=-1, keepdims=True) + EPS)
        xh = f * r
        err = (h1_ref[...] + xh * g) - t_ref[...]
        part = 0.5 * jnp.sum(jnp.mean(err * err, axis=-1, keepdims=True), axis=0, keepdims=True)
        dh2 = err * (1.0 / D)
        dh2_ref[...] = dh2
        dgp = jnp.sum(dh2 * xh, axis=0, keepdims=True)
        dxh = dh2 * g
        dffn_ref[...] = (r * (dxh - xh * jnp.mean(dxh * xh, axis=-1, keepdims=True))).astype(dffn_ref.dtype)

        @pl.when(i == 0)
        def _():
            loss_ref[...] = jnp.zeros_like(loss_ref)
            dg_ref[...] = jnp.zeros_like(dg_ref)

        loss_ref[...] += jnp.broadcast_to(part, loss_ref.shape)
        dg_ref[...] += dgp

    row = pl.BlockSpec((tr, D), lambda i: (i, 0))
    vec = pl.BlockSpec((1, D), lambda i: (0, 0))
    return pl.pallas_call(
        body,
        grid=(T // tr,),
        in_specs=[row, row, row, vec],
        out_specs=[row, row, pl.BlockSpec((8, 128), lambda i: (0, 0)), vec],
        out_shape=[jax.ShapeDtypeStruct((T, D), BF16), jax.ShapeDtypeStruct((T, D), F32),
                   jax.ShapeDtypeStruct((8, 128), F32), jax.ShapeDtypeStruct((1, D), F32)],
        compiler_params=_params(),
        name="loss_head",
    )(ffn, h1, target, g_post)


def _ffn_down_bwd(dffn, wd, zg, zu):
    T, D = dffn.shape
    F = D_FF
    tm, tn = 512, F // 2

    def body(d_ref, w_ref, zg_ref, zu_ref, dzg_ref, dzu_ref):
        da = _dot_nt(d_ref[...], w_ref[...])
        zg = zg_ref[...].astype(F32)
        zu = zu_ref[...].astype(F32)
        s = jax.nn.sigmoid(zg)
        dzu_ref[...] = (da * (zg * s)).astype(dzu_ref.dtype)
        dzg_ref[...] = (da * zu * (s * (1.0 + zg * (1.0 - s)))).astype(dzg_ref.dtype)

    z_spec = pl.BlockSpec((tm, tn), lambda n, m: (m, n))
    out = jax.ShapeDtypeStruct((T, F), BF16)
    return pl.pallas_call(
        body,
        grid=(F // tn, T // tm),
        in_specs=[pl.BlockSpec((tm, D), lambda n, m: (m, 0)), pl.BlockSpec((tn, D), lambda n, m: (n, 0)),
                  z_spec, z_spec],
        out_specs=[z_spec, z_spec],
        out_shape=[out, out],
        compiler_params=_params(),
        name="ffn_down_bwd",
    )(dffn, wd, zg, zu)


def _mid_bwd(h1, du2, dh2, mix, g_pre, g_post):
    T, D = h1.shape
    tr = 512

    def body(h1_ref, du2_ref, dh2_ref, mix_ref, gq_ref, gp_ref, dh1_ref, dmix_ref, dgq_ref, dgp_ref):
        i = pl.program_id(0)
        dx, dgq = _rms_bwd_terms(h1_ref[...], gq_ref[...], du2_ref[...])
        dh1 = dh2_ref[...] + dx
        dh1_ref[...] = dh1
        dmix, dgp = _rms_bwd_terms(mix_ref[...], gp_ref[...], dh1)
        dmix_ref[...] = dmix.astype(dmix_ref.dtype)

        @pl.when(i == 0)
        def _():
            dgq_ref[...] = jnp.zeros_like(dgq_ref)
            dgp_ref[...] = jnp.zeros_like(dgp_ref)

        dgq_ref[...] += dgq
        dgp_ref[...] += dgp

    row = pl.BlockSpec((tr, D), lambda i: (i, 0))
    vec = pl.BlockSpec((1, D), lambda i: (0, 0))
    return pl.pallas_call(
        body,
        grid=(T // tr,),
        in_specs=[row, row, row, row, vec, vec],
        out_specs=[row, row, vec, vec],
        out_shape=[jax.ShapeDtypeStruct((T, D), F32), jax.ShapeDtypeStruct((T, D), BF16),
                   jax.ShapeDtypeStruct((1, D), F32), jax.ShapeDtypeStruct((1, D), F32)],
        compiler_params=_params(),
        name="mid_bwd",
    )(h1, du2, dh2, mix, g_pre, g_post)


def _gate_bwd(dmerged, ps, proj, wbs):
    T, D = dmerged.shape
    tr, tc = min(1024, T), GATE_TC
    nc = D // tc
    gl0 = COL_GL * 128 // tc

    def body(dm_ref, pa, pb, pc, g0, g1, g2, wa, wb, wc, dpa, dpb, dpc, dga, dgb, dgc, dya, dyb, dyc,
             acc_a, acc_b, acc_c):
        n = pl.program_id(1)
        dm = dm_ref[...]
        for p_ref, g_ref, w_ref, dp_ref, dg_ref, dy_ref, acc_ref in (
                (pa, g0, wa, dpa, dga, dya, acc_a), (pb, g1, wb, dpb, dgb, dyb, acc_b),
                (pc, g2, wc, dpc, dgc, dyc, acc_c)):
            s = jax.nn.sigmoid(g_ref[...].astype(F32))
            dp = (dm * s).astype(BF16)
            dp_ref[...] = dp
            dg_ref[...] = (dm * p_ref[...].astype(F32) * (s * (1.0 - s))).astype(dg_ref.dtype)
            part = _dot_nt(dp, w_ref[...])

            @pl.when(n == 0)
            def _():
                acc_ref[...] = part

            @pl.when(n > 0)
            def _():
                acc_ref[...] += part

            @pl.when(n == nc - 1)
            def _():
                dy_ref[...] = acc_ref[...].astype(dy_ref.dtype)

    col = pl.BlockSpec((tr, tc), lambda i, n: (i, n))
    y_spec = pl.BlockSpec((tr, 512), lambda i, n: (i, 0))
    w_spec = pl.BlockSpec((512, tc), lambda i, n: (0, n))
    gl_specs = [pl.BlockSpec((tr, tc), lambda i, n, j=j: (i, gl0 + j * nc + n)) for j in range(3)]
    big = jax.ShapeDtypeStruct((T, D), BF16)
    small = jax.ShapeDtypeStruct((T, 512), BF16)
    return pl.pallas_call(
        body,
        grid=(T // tr, nc),
        in_specs=[col] * 4 + gl_specs + [w_spec] * 3,
        out_specs=[col] * 6 + [y_spec] * 3,
        out_shape=[big] * 6 + [small] * 3,
        scratch_shapes=[pltpu.VMEM((tr, 512), F32)] * 3,
        compiler_params=_params(),
        name="gate_bwd",
    )(dmerged, *ps, proj, proj, proj, *wbs)


def _pre_bwd(x, du, dh1, g):
    T, D = x.shape
    tr = 512

    def body(x_ref, du_ref, dh1_ref, g_ref, gx_ref, dg_ref):
        i = pl.program_id(0)
        dx, dg = _rms_bwd_terms(x_ref[...], g_ref[...], du_ref[...])
        gx_ref[...] = dh1_ref[...] + dx

        @pl.when(i == 0)
        def _():
            dg_ref[...] = jnp.zeros_like(dg_ref)

        dg_ref[...] += dg

    row = pl.BlockSpec((tr, D), lambda i: (i, 0))
    vec = pl.BlockSpec((1, D), lambda i: (0, 0))
    return pl.pallas_call(
        body,
        grid=(T // tr,),
        in_specs=[row, row, row, vec],
        out_specs=[row, vec],
        out_shape=[jax.ShapeDtypeStruct((T, D), F32), jax.ShapeDtypeStruct((1, D), F32)],
        compiler_params=_params(),
        name="pre_bwd",
    )(x, du, dh1, g)


def _gain_grad(xin, dy):
    T, D = xin.shape
    tr = min(512, T)

    def body(x_ref, dy_ref, dg_ref):
        i = pl.program_id(0)
        xf = x_ref[...]
        r = lax.rsqrt(jnp.mean(xf * xf, axis=-1, keepdims=True) + EPS)

        @pl.when(i == 0)
        def _():
            dg_ref[...] = jnp.zeros_like(dg_ref)

        dg_ref[...] += jnp.sum(dy_ref[...] * (xf * r), axis=0, keepdims=True)

    row = pl.BlockSpec((tr, D), lambda i: (i, 0))
    return pl.pallas_call(
        body,
        grid=(T // tr,),
        in_specs=[row, row],
        out_specs=pl.BlockSpec((1, D), lambda i: (0, 0)),
        out_shape=jax.ShapeDtypeStruct((1, D), F32),
        name="gain_grad",
    )(xin, dy)


def _swa_buckets():
    dist = (np.arange(BLOCK)[:, None] + BLOCK) - np.arange(2 * BLOCK)[None, :]
    max_exact = N_BUCKETS // 2
    d = np.maximum(dist, 0)
    df = np.maximum(d, 1).astype(np.float32)
    large = max_exact + (np.log(df / np.float32(max_exact)) / np.float32(math.log(MAX_DISTANCE / max_exact))
                         * np.float32(N_BUCKETS - max_exact)).astype(np.int32)
    large = np.minimum(large, N_BUCKETS - 1)
    bucket = np.where(d < max_exact, d, large)
    in_win = (dist >= 0) & (dist < SWA_WINDOW)
    return np.where(in_win, bucket, -1).astype(np.int32)


def _swa_bias_table(rel_bias, buckets):
    H = SWA_Q_HEADS

    def body(rb_ref, bk_ref, o_ref):
        bk = bk_ref[...]
        for h in range(H):
            acc = jnp.full(bk.shape, NEG, F32)
            for b in range(N_BUCKETS):
                acc = jnp.where(bk == b, rb_ref[b, h], acc)
            o_ref[h] = acc

    return pl.pallas_call(
        body,
        in_specs=[pl.BlockSpec(memory_space=pltpu.SMEM), pl.BlockSpec(memory_space=pltpu.VMEM)],
        out_specs=pl.BlockSpec(memory_space=pltpu.VMEM),
        out_shape=jax.ShapeDtypeStruct((H, BLOCK, 2 * BLOCK), F32),
        name="swa_bias_table",
    )(rel_bias, buckets)


def _swa_bias_grad(dbias, buckets):
    H = SWA_Q_HEADS

    def body(db_ref, bk_ref, o_ref):
        bk = bk_ref[...]
        rows = lax.broadcasted_iota(jnp.int32, (N_BUCKETS, 128), 0)
        lanes = lax.broadcasted_iota(jnp.int32, (N_BUCKETS, 128), 1)
        acc = jnp.zeros((N_BUCKETS, 128), F32)
        for h in range(H):
            d = db_ref[h]
            for b in range(N_BUCKETS):
                s = jnp.sum(jnp.sum(jnp.where(bk == b, d, 0.0), axis=1, keepdims=True), axis=0, keepdims=True)
                acc = jnp.where((rows == b) & (lanes == h), s, acc)
        o_ref[...] = acc

    return pl.pallas_call(
        body,
        in_specs=[pl.BlockSpec(memory_space=pltpu.VMEM)] * 2,
        out_specs=pl.BlockSpec(memory_space=pltpu.VMEM),
        out_shape=jax.ShapeDtypeStruct((N_BUCKETS, 128), F32),
        name="swa_bias_grad",
    )(dbias, buckets)


SWA_GROUP = 4


def _swa_stack(x, g):
    blocks = []
    for t in range(SWA_GROUP):
        h = SWA_GROUP * g + t
        xp = x[:, (h // 2) * BLOCK:(h // 2 + 1) * BLOCK]
        blocks.append(xp if h % 2 == g else pltpu.roll(xp, HEAD_DIM, 1))
    return jnp.concatenate(blocks, axis=0).astype(BF16)


def _swa_unstack(parts):
    pairs = []
    for p in range(4):
        g = p // 2
        halves = []
        for hh in range(2):
            t = 2 * p + hh - SWA_GROUP * g
            blk = parts[g][t * BLOCK:(t + 1) * BLOCK]
            halves.append(blk if hh == g else pltpu.roll(blk, HEAD_DIM, 1))
        pairs.append(halves[0] + halves[1])
    return jnp.concatenate(pairs, axis=1)


def _swa_group_inputs(g, sink_ref, bias_ref, kv_refs):
    lane = lax.broadcasted_iota(jnp.int32, (BLOCK, BLOCK), 1)
    km = (lane // HEAD_DIM) == g
    heads = range(SWA_GROUP * g, SWA_GROUP * (g + 1))
    bias = jnp.concatenate([bias_ref[h] for h in heads], axis=0)
    sink = jnp.concatenate([jnp.full((BLOCK, 1), sink_ref[0, h], F32) for h in heads], axis=0)
    return bias, sink, [jnp.where(km, r[...], 0) for r in kv_refs], km


def _swa_scores(qs, kp, kc, bias, sink, first):
    sp = _dot_nt(qs, kp) * SCALE64 + bias[:, :BLOCK]
    sp = jnp.where(first, NEG, sp)
    sc = _dot_nt(qs, kc) * SCALE64 + bias[:, BLOCK:]
    m = jnp.maximum(jnp.maximum(jnp.max(sp, axis=1, keepdims=True), jnp.max(sc, axis=1, keepdims=True)), sink)
    pp = jnp.exp(sp - m)
    pc = jnp.exp(sc - m)
    ps = jnp.exp(sink - m)
    den = jnp.sum(pp, axis=1, keepdims=True) + jnp.sum(pc, axis=1, keepdims=True) + ps
    return pp / den, pc / den, ps / den


def _swa_fwd(proj, sinks, bias_tab, B, S, hosted=None):
    nb = S // BLOCK
    T = B * S

    def body(sink_ref, q_ref, kp_ref, kc_ref, vp_ref, vc_ref, bias_ref, o_ref):
        i = pl.program_id(1)
        first = jnp.full((SWA_GROUP * BLOCK, BLOCK), i, jnp.int32) == 0
        q = q_ref[...].astype(F32)
        outs = []
        for g in range(2):
            bias, sink, (kp, kc, vp, vc), _ = _swa_group_inputs(g, sink_ref, bias_ref, (kp_ref, kc_ref, vp_ref, vc_ref))
            wp, wc, _ = _swa_scores(_swa_stack(q, g), kp, kc, bias, sink, first)
            outs.append(_dot(wp.astype(BF16), vp) + _dot(wc.astype(BF16), vc))
        o_ref[...] = _swa_unstack(outs).astype(o_ref.dtype)

    blk = (BLOCK, BLOCK)
    wide = (BLOCK, 4 * BLOCK)
    outs, moved = _hosted_call(
        body, (B, nb),
        [pl.BlockSpec(memory_space=pltpu.SMEM),
         pl.BlockSpec(wide, lambda b, i: (b * nb + i, COL_QA // 4)),
         pl.BlockSpec(blk, lambda b, i: (b * nb + jnp.maximum(i - 1, 0), COL_KA)),
         pl.BlockSpec(blk, lambda b, i: (b * nb + i, COL_KA)),
         pl.BlockSpec(blk, lambda b, i: (b * nb + jnp.maximum(i - 1, 0), COL_VA)),
         pl.BlockSpec(blk, lambda b, i: (b * nb + i, COL_VA)),
         pl.BlockSpec((SWA_Q_HEADS, BLOCK, 2 * BLOCK), lambda b, i: (0, 0, 0))],
        [pl.BlockSpec(wide, lambda b, i: (b * nb + i, 0))],
        [jax.ShapeDtypeStruct((T, 512), BF16)], [], hosted, "swa_fwd",
        (sinks, proj, proj, proj, proj, proj, bias_tab))
    return outs[0], moved


def _swa_bwd(proj, dy, sinks, bias_tab, B, S):
    nb = S // BLOCK
    T = B * S
    H = SWA_Q_HEADS

    def body(sink_ref, q_ref, kp_ref, kc_ref, vp_ref, vc_ref, do_ref, bias_ref,
             dq_ref, dk_ref, dv_ref, dbias_ref, dsink_ref):
        b = pl.program_id(0)
        i = pl.program_id(1)

        @pl.when((b == 0) & (i == 0))
        def _():
            dbias_ref[...] = jnp.zeros_like(dbias_ref)
            dsink_ref[...] = jnp.zeros_like(dsink_ref)

        @pl.when(i == 0)
        def _():
            dk_ref[...] = jnp.zeros_like(dk_ref)
            dv_ref[...] = jnp.zeros_like(dv_ref)

        first = jnp.full((SWA_GROUP * BLOCK, BLOCK), i, jnp.int32) == 0
        q = q_ref[...].astype(F32)
        do = do_ref[...].astype(F32)
        dqs, dsinks, dbs, dkp, dkc, dvp, dvc = [], [], [], [], [], [], []
        for g in range(2):
            bias, sink, (kp, kc, vp, vc), km = _swa_group_inputs(
                g, sink_ref, bias_ref, (kp_ref, kc_ref, vp_ref, vc_ref))
            qs = _swa_stack(q, g)
            dos = _swa_stack(do, g)
            wp, wc, ws = _swa_scores(qs, kp, kc, bias, sink, first)
            dwp = _dot_nt(dos, vp)
            dwc = _dot_nt(dos, vc)
            dsum = jnp.sum(wp * dwp, axis=1, keepdims=True) + jnp.sum(wc * dwc, axis=1, keepdims=True)
            dsp = wp * (dwp - dsum)
            dsc = wc * (dwc - dsum)
            dsk = -ws * dsum
            for t in range(SWA_GROUP):
                rows = slice(t * BLOCK, (t + 1) * BLOCK)
                dsinks.append(jnp.broadcast_to(jnp.sum(dsk[rows], axis=0, keepdims=True), (1, 128)))
                dbs.append(jnp.concatenate([dsp[rows], dsc[rows]], axis=1))
            dspb = dsp.astype(BF16)
            dscb = dsc.astype(BF16)
            dqs.append(_dot(dspb, kp) + _dot(dscb, kc))
            dkp.append(jnp.where(km, _dot_tn(dspb, qs), 0.0))
            dkc.append(jnp.where(km, _dot_tn(dscb, qs), 0.0))
            dvp.append(jnp.where(km, _dot_tn(wp.astype(BF16), dos), 0.0))
            dvc.append(jnp.where(km, _dot_tn(wc.astype(BF16), dos), 0.0))
        dq_ref[...] = (_swa_unstack(dqs) * SCALE64).astype(dq_ref.dtype)
        dsink_ref[...] += jnp.concatenate(dsinks, axis=0)
        for h in range(H):
            dbias_ref[h] += dbs[h]
        cur = pl.ds(pl.multiple_of(i * BLOCK, BLOCK), BLOCK)
        prev = pl.ds(pl.multiple_of(jnp.maximum(i - 1, 0) * BLOCK, BLOCK), BLOCK)
        dk_ref[prev, :] += (dkp[0] + dkp[1]) * SCALE64
        dk_ref[cur, :] += (dkc[0] + dkc[1]) * SCALE64
        dv_ref[prev, :] += dvp[0] + dvp[1]
        dv_ref[cur, :] += dvc[0] + dvc[1]

    blk = (BLOCK, BLOCK)
    wide = (BLOCK, 4 * BLOCK)
    kv_out = pl.BlockSpec((S, BLOCK), lambda b, i: (b, 0))
    full_bias = pl.BlockSpec((H, BLOCK, 2 * BLOCK), lambda b, i: (0, 0, 0))
    return pl.pallas_call(
        body,
        grid=(B, nb),
        in_specs=[pl.BlockSpec(memory_space=pltpu.SMEM),
                  pl.BlockSpec(wide, lambda b, i: (b * nb + i, COL_QA // 4)),
                  pl.BlockSpec(blk, lambda b, i: (b * nb + jnp.maximum(i - 1, 0), COL_KA)),
                  pl.BlockSpec(blk, lambda b, i: (b * nb + i, COL_KA)),
                  pl.BlockSpec(blk, lambda b, i: (b * nb + jnp.maximum(i - 1, 0), COL_VA)),
                  pl.BlockSpec(blk, lambda b, i: (b * nb + i, COL_VA)),
                  pl.BlockSpec(wide, lambda b, i: (b * nb + i, 0)),
                  full_bias],
        out_specs=[pl.BlockSpec(wide, lambda b, i: (b * nb + i, 0)),
                   kv_out, kv_out, full_bias,
                   pl.BlockSpec((H, 128), lambda b, i: (0, 0))],
        out_shape=[jax.ShapeDtypeStruct((T, 512), BF16),
                   jax.ShapeDtypeStruct((T, BLOCK), F32), jax.ShapeDtypeStruct((T, BLOCK), F32),
                   jax.ShapeDtypeStruct((H, BLOCK, 2 * BLOCK), F32), jax.ShapeDtypeStruct((H, 128), F32)],
        compiler_params=_params(),
        name="swa_bwd",
    )(sinks, proj, proj, proj, proj, proj, dy, bias_tab)


SB_TILE = 256


SB_HEADS = 4
SB_LANES = SB_HEADS * HEAD_DIM
SB_ROWS = SB_HEADS * SB_TILE


def _sb_logits(qs, kj, tri):
    z = _dot_nt(qs, kj)
    sp = jnp.log(1.0 + jnp.exp(-jnp.abs(z)))
    ls = jnp.minimum(z, 0.0) - sp
    l1m = ls - z
    if tri is not None:
        l1m = jnp.where(tri, l1m, 0.0)
    return ls, l1m


def _sb_masks():
    lane = lax.broadcasted_iota(jnp.int32, (SB_TILE, SB_LANES), 1)
    hm = [(lane // HEAD_DIM) == h for h in range(SB_HEADS)]
    row = lax.broadcasted_iota(jnp.int32, (SB_ROWS, SB_TILE), 0) % SB_TILE
    col = lax.broadcasted_iota(jnp.int32, (SB_ROWS, SB_TILE), 1)
    return lane, hm, row, col


def _sb_stack(x, hm):
    return jnp.concatenate([jnp.where(m, x, 0) for m in hm], axis=0)


def _sb_unstack(x, hm):
    return sum(jnp.where(m, x[h * SB_TILE:(h + 1) * SB_TILE], 0.0) for h, m in enumerate(hm))


def _sb_fwd(proj, B, S, hosted=None):
    nt = S // SB_TILE
    T = B * S

    def body(q_ref, k_ref, v_ref, o_ref, r_ref):
        i = pl.program_id(2)
        _, hm, row, col = _sb_masks()
        tri = col < row
        later = (row[:SB_TILE] > col[:SB_TILE]).astype(BF16)
        qs = _sb_stack(q_ref[...] * SCALE64, hm)

        def tile(j, carry, mask):
            acc, c = carry
            rows = pl.ds(pl.multiple_of(j * SB_TILE, SB_TILE), SB_TILE)
            ls, l1m = _sb_logits(qs, k_ref[rows, :], mask)
            a = jnp.exp(ls + c + _dot_split(l1m, later))
            if mask is not None:
                a = jnp.where(mask, a, 0.0)
            pv = _dot(a.astype(BF16), v_ref[rows, :])
            return acc + _sb_unstack(pv, hm), c + jnp.sum(l1m, axis=1, keepdims=True)

        carry = tile(i, (jnp.zeros((SB_TILE, SB_LANES), F32), jnp.zeros((SB_ROWS, 1), F32)), tri)
        acc, c = lax.fori_loop(0, i, lambda it, cr: tile(i - 1 - it, cr, None), carry)
        o_ref[...] = acc.astype(o_ref.dtype)
        r_ref[...] = _sb_unstack(jnp.broadcast_to(c, (SB_ROWS, SB_LANES)), hm)

    blk = (SB_TILE, SB_LANES)
    ng = 512 // SB_LANES
    cq, ck, cv = (c * BLOCK // SB_LANES for c in (COL_QB, COL_KB, COL_VB))
    outs, moved = _hosted_call(
        body, (B, ng, nt),
        [pl.BlockSpec(blk, lambda b, p, i: (b * nt + i, cq + p)),
         pl.BlockSpec((S, SB_LANES), lambda b, p, i: (b, ck + p)),
         pl.BlockSpec((S, SB_LANES), lambda b, p, i: (b, cv + p))],
        [pl.BlockSpec(blk, lambda b, p, i: (b * nt + i, p))] * 2,
        [jax.ShapeDtypeStruct((T, 512), BF16), jax.ShapeDtypeStruct((T, 512), F32)], [], hosted, "sb_fwd",
        (proj, proj, proj))
    return outs[0], outs[1], moved


def _sb_bwd(proj, dy, rtot, B, S, hosted=None):
    nt = S // SB_TILE
    T = B * S

    def body(q_ref, k_ref, v_ref, do_ref, r_ref, dq_ref, dk_ref, dv_ref):
        i = pl.program_id(2)

        @pl.when(i == 0)
        def _():
            dk_ref[...] = jnp.zeros_like(dk_ref)
            dv_ref[...] = jnp.zeros_like(dv_ref)

        lane, hm, row, col = _sb_masks()
        tri = col < row
        later = (row[:SB_TILE] > col[:SB_TILE]).astype(BF16)
        earlier = (row[:SB_TILE] < col[:SB_TILE]).astype(BF16)
        qs = _sb_stack(q_ref[...] * SCALE64, hm)
        dos = _sb_stack(do_ref[...], hm)
        r = r_ref[...]
        rs = jnp.concatenate([jnp.sum(jnp.where(lane == h * HEAD_DIM, r, 0.0), axis=1, keepdims=True)
                              for h in range(SB_HEADS)], axis=0)

        def tile(j, carry, mask):
            dq, lsum, psum = carry
            rows = pl.ds(pl.multiple_of(j * SB_TILE, SB_TILE), SB_TILE)
            kj = k_ref[rows, :]
            vj = v_ref[rows, :]
            ls, l1m = _sb_logits(qs, kj, mask)
            sig = jnp.exp(ls)
            lsum = lsum + jnp.sum(l1m, axis=1, keepdims=True)
            a = jnp.exp(ls + (rs - lsum) + _dot_split(l1m, later))
            if mask is not None:
                a = jnp.where(mask, a, 0.0)
            de = _dot_nt(dos, vj) * a
            pre = psum + _dot(de.astype(BF16), earlier)
            dz = de * (1.0 - sig) - sig * pre
            if mask is not None:
                dz = jnp.where(mask, dz, 0.0)
            dz = dz.astype(BF16)
            dk_ref[rows, :] += _dot_tn(dz, qs)
            dv_ref[rows, :] += _dot_tn(a.astype(BF16), dos)
            return dq + _sb_unstack(_dot(dz, kj), hm), lsum, psum + jnp.sum(de, axis=1, keepdims=True)

        zero = jnp.zeros((SB_ROWS, 1), F32)
        carry = lax.fori_loop(0, i, lambda j, c: tile(j, c, None), (jnp.zeros((SB_TILE, SB_LANES), F32), zero, zero))
        dq = tile(i, carry, tri)[0]
        dq_ref[...] = (dq * SCALE64).astype(dq_ref.dtype)

    blk = (SB_TILE, SB_LANES)
    ng = 512 // SB_LANES
    cq, ck, cv = (c * BLOCK // SB_LANES for c in (COL_QB, COL_KB, COL_VB))
    q_spec = pl.BlockSpec(blk, lambda b, p, i: (b * nt + i, p))
    kv_out = pl.BlockSpec((S, SB_LANES), lambda b, p, i: (b, p))
    outs, moved = _hosted_call(
        body, (B, ng, nt),
        [pl.BlockSpec(blk, lambda b, p, i: (b * nt + i, cq + p)),
         pl.BlockSpec((S, SB_LANES), lambda b, p, i: (b, ck + p)),
         pl.BlockSpec((S, SB_LANES), lambda b, p, i: (b, cv + p)),
         q_spec, q_spec],
        [q_spec, kv_out, kv_out],
        [jax.ShapeDtypeStruct((T, 512), BF16),
         jax.ShapeDtypeStruct((T, 512), F32), jax.ShapeDtypeStruct((T, 512), F32)], [], hosted, "sb_bwd",
        (proj, proj, proj, dy, rtot))
    return outs[0], outs[1], outs[2], moved


def _mem_weights(q, mk):
    z = _dot_nt(q, mk) * SCALE128
    e = jnp.exp(z - jnp.max(z, axis=1, keepdims=True))
    return e / jnp.sum(e, axis=1, keepdims=True)


def _mem_fwd(proj, mkv, B, S, M):
    tq = 512
    nq = S // tq
    T = B * S
    Hm = MEM_HEADS

    def body(q0, q1, q2, q3, mk_ref, mv_ref, o_ref):
        outs = []
        for h, q_ref in enumerate((q0, q1, q2, q3)):
            cols = slice(h * 128, (h + 1) * 128)
            w = _mem_weights(q_ref[...], mk_ref[:, cols])
            outs.append(_dot(w.astype(BF16), mv_ref[:, cols]))
        o_ref[...] = jnp.concatenate(outs, axis=1).astype(o_ref.dtype)

    return pl.pallas_call(
        body,
        grid=(B, nq),
        in_specs=[pl.BlockSpec((tq, 128), lambda b, i, h=h: (b * nq + i, COL_QM + h)) for h in range(Hm)]
        + [pl.BlockSpec((M, 512), lambda b, i: (b, 0)), pl.BlockSpec((M, 512), lambda b, i: (b, 1))],
        out_specs=pl.BlockSpec((tq, 512), lambda b, i: (b * nq + i, 0)),
        out_shape=jax.ShapeDtypeStruct((T, 512), BF16),
        name="mem_fwd",
    )(proj, proj, proj, proj, mkv, mkv)


def _mem_bwd(proj, mkv, dy, B, S, M):
    tq = 512
    nq = S // tq
    T = B * S
    Hm = MEM_HEADS

    def body(q0, q1, q2, q3, mk_ref, mv_ref, do_ref, dq_ref, dmk_ref, dmv_ref):
        i = pl.program_id(1)
        dqs, dmks, dmvs = [], [], []
        for h, q_ref in enumerate((q0, q1, q2, q3)):
            cols = slice(h * 128, (h + 1) * 128)
            q = q_ref[...]
            do = do_ref[:, cols]
            mk = mk_ref[:, cols]
            w = _mem_weights(q, mk)
            dw = _dot_nt(do, mv_ref[:, cols])
            ds = (w * (dw - jnp.sum(w * dw, axis=1, keepdims=True))).astype(BF16)
            dqs.append(_dot(ds, mk))
            dmks.append(_dot_tn(ds, q))
            dmvs.append(_dot_tn(w.astype(BF16), do))
        dq_ref[...] = (jnp.concatenate(dqs, axis=1) * SCALE128).astype(dq_ref.dtype)

        @pl.when(i == 0)
        def _():
            dmk_ref[...] = jnp.zeros_like(dmk_ref)
            dmv_ref[...] = jnp.zeros_like(dmv_ref)

        dmk_ref[...] += jnp.concatenate(dmks, axis=1) * SCALE128
        dmv_ref[...] += jnp.concatenate(dmvs, axis=1)

    q_spec = pl.BlockSpec((tq, 512), lambda b, i: (b * nq + i, 0))
    m_out = pl.BlockSpec((M, 512), lambda b, i: (b, 0))
    return pl.pallas_call(
        body,
        grid=(B, nq),
        in_specs=[pl.BlockSpec((tq, 128), lambda b, i, h=h: (b * nq + i, COL_QM + h)) for h in range(Hm)]
        + [pl.BlockSpec((M, 512), lambda b, i: (b, 0)), pl.BlockSpec((M, 512), lambda b, i: (b, 1)), q_spec],
        out_specs=[q_spec, m_out, m_out],
        out_shape=[jax.ShapeDtypeStruct((T, 512), BF16),
                   jax.ShapeDtypeStruct((B * M, 512), F32), jax.ShapeDtypeStruct((B * M, 512), F32)],
        name="mem_bwd",
    )(proj, proj, proj, proj, mkv, mkv, dy)


def _all_gather(blk, name):
    R, C = blk.shape

    def body(x_ref, out_ref, send_sems, recv_sems, local_sem):
        x, y, c = _mesh_pos()
        me, sibling = (x, y, c), (x, y, 1 - c)
        chips = [(1 - x, y), (x, 1 - y), (1 - x, 1 - y)]

        def slot(px, py, pc):
            return out_ref.at[4 * px + 2 * py + pc]

        def copy(k, block, to, src=None):
            return pltpu.make_async_remote_copy(
                src_ref=slot(*block) if src is None else src, dst_ref=slot(*block),
                send_sem=send_sems.at[k], recv_sem=recv_sems.at[k],
                device_id=to, device_id_type=pl.DeviceIdType.MESH)

        mine = pltpu.make_async_copy(x_ref, slot(*me), local_sem)
        mine.start()
        first = [copy(0, me, sibling, src=x_ref)]
        first += [copy(1 + j, me, (*chip, c), src=x_ref) for j, chip in enumerate(chips)]
        for cp in first:
            cp.start()
        passed = [copy(4 + j, (*chip, c), sibling) for j, chip in enumerate(chips)]
        for j, chip in enumerate(chips):
            copy(1 + j, (*chip, c), me).wait_recv()
            passed[j].start()
        copy(0, sibling, me).wait_recv()
        for j, chip in enumerate(chips):
            copy(4 + j, (*chip, 1 - c), me).wait_recv()
        for cp in first + passed:
            cp.wait_send()
        mine.wait()

    return pl.pallas_call(
        body,
        in_specs=[pl.BlockSpec(memory_space=pl.ANY)],
        out_specs=pl.BlockSpec(memory_space=pl.ANY),
        out_shape=jax.ShapeDtypeStruct((N_DEV, R, C), blk.dtype),
        scratch_shapes=[pltpu.SemaphoreType.DMA((7,)), pltpu.SemaphoreType.DMA((7,)), pltpu.SemaphoreType.DMA],
        name=name,
    )(blk)


_HBM = pl.BlockSpec(memory_space=pltpu.HBM)
_SEM = pl.BlockSpec(memory_space=pltpu.SEMAPHORE)


def _scatter_start(parts, carried, name):
    hosted = _Hosted(scatters=[parts])

    def body(p_ref, land_ref, c_ref, send_sems, recv_sems, local_sems, p_thru, land_thru, c_thru):
        for cp in hosted.copies([p_ref], [land_ref], send_sems, recv_sems, local_sems):
            cp.start()

    sems = (pltpu.SemaphoreType.DMA((7,)), pltpu.SemaphoreType.DMA((7,)), pltpu.SemaphoreType.DMA((1,)))
    hbm = lambda a: pltpu.HBM(a.shape, a.dtype)
    outs = pl.pallas_call(
        body, name=name,
        out_shape=sems + (hbm(parts), hbm(parts), hbm(carried)),
        in_specs=(_HBM, _HBM, _HBM), out_specs=(_SEM, _SEM, _SEM, _HBM, _HBM, _HBM),
        input_output_aliases={0: 3, 1: 4, 2: 5},
        compiler_params=pltpu.CompilerParams(has_side_effects=pltpu.SideEffectType.DATAFLOW_SIDE_EFFECTING),
    )(pltpu.with_memory_space_constraint(parts, pltpu.HBM),
      pltpu.with_memory_space_constraint(lax.empty(parts.shape, parts.dtype), pltpu.HBM),
      pltpu.with_memory_space_constraint(carried, pltpu.HBM))
    return outs[:5], outs[5]


def _scatter_wait(flight, after, name):
    send_sems, recv_sems, local_sems, p_thru, land_thru = flight
    hosted = _Hosted(scatters=[p_thru])

    def body(p_ref, land_ref, send, recv, local, after_ref, p_dead, got_ref):
        for cp in hosted.copies([p_ref], [land_ref], send, recv, local):
            cp.wait()

    hbm = lambda a: pltpu.HBM(a.shape, a.dtype)
    return pl.pallas_call(
        body, name=name,
        out_shape=(hbm(p_thru), hbm(land_thru)),
        in_specs=(_HBM, _HBM, _SEM, _SEM, _SEM, pl.BlockSpec(memory_space=pl.ANY)), out_specs=(_HBM, _HBM),
        input_output_aliases={0: 0, 1: 1},
        compiler_params=pltpu.CompilerParams(has_side_effects=pltpu.SideEffectType.DATAFLOW_SIDE_EFFECTING),
    )(p_thru, land_thru, send_sems, recv_sems, local_sems, after)[1]


def _adamw(parts, w, m, v, name):
    R, C = w.shape
    tr = R
    for cand in (256, 176, 128, 64, 32, 16, 8):
        if R % cand == 0 and cand * C * 4 <= 1024 * 1024:
            tr = cand
            break
    c1 = 1.0 - ADAM_B1 ** ADAM_STEP
    c2 = 1.0 - ADAM_B2 ** ADAM_STEP

    def body(p_ref, w_ref, m_ref, v_ref, g_ref, d_ref, nm_ref, nv_ref):
        g = p_ref[0].astype(F32)
        for d in range(1, N_DEV):
            g = g + p_ref[d].astype(F32)
        nm = ADAM_B1 * m_ref[...] + (1.0 - ADAM_B1) * g
        nv = ADAM_B2 * v_ref[...] + (1.0 - ADAM_B2) * (g * g)
        g_ref[...] = g
        nm_ref[...] = nm
        nv_ref[...] = nv
        d_ref[...] = -ADAM_LR * ((nm / c1) / (jnp.sqrt(nv / c2) + ADAM_EPS) + ADAM_WD * w_ref[...])

    row = pl.BlockSpec((tr, C), lambda i: (i, 0))
    out = jax.ShapeDtypeStruct((R, C), F32)
    return pl.pallas_call(
        body,
        grid=(R // tr,),
        in_specs=[pl.BlockSpec((N_DEV, tr, C), lambda i: (0, i, 0)), row, row, row],
        out_specs=[row] * 4,
        out_shape=[out] * 4,
        compiler_params=_params(),
        name=name,
    )(parts, w, m, v)


def _col_shards(g):
    R, C8 = g.shape
    return g.reshape(R, N_DEV, C8 // N_DEV).transpose(1, 0, 2)


def _row_shards(g):
    R8, C = g.shape
    return g.reshape(N_DEV, R8 // N_DEV, C)


def _cols_full(gathered):
    n, R, C = gathered.shape
    return gathered.transpose(1, 0, 2).reshape(R, n * C)


_BIG = ("w_in", "w_mem_kv", "w_branch_swa", "w_branch_sb", "w_branch_mem", "w_out", "w_gate", "w_up", "w_down")
_COL_SHARDED = ("w_in", "w_branch_swa", "w_branch_sb", "w_branch_mem", "w_gate", "w_up")
_SMALL = ("ln_mix_pre", "ln_mix_post", "swa_sinks", "rel_bias", "ln_mem", "ln_ffn_pre", "ln_ffn_post")
_ORDER = ("ln_mix_pre", "ln_mix_post", "w_in", "swa_sinks", "rel_bias", "ln_mem", "w_mem_kv", "w_branch_swa",
          "w_branch_sb", "w_branch_mem", "w_out", "ln_ffn_pre", "ln_ffn_post", "w_gate", "w_up", "w_down")


def _pack_small(d):
    rows = [d["ln_mix_pre"], d["ln_mix_post"], d["ln_mem"], d["ln_ffn_pre"], d["ln_ffn_post"],
            jnp.pad(d["swa_sinks"].reshape(1, -1), ((0, 0), (0, D_MODEL - SWA_Q_HEADS))),
            jnp.pad(d["rel_bias"].reshape(1, -1), ((0, 0), (0, D_MODEL - N_BUCKETS * SWA_Q_HEADS))),
            jnp.zeros((1, D_MODEL), F32)]
    return jnp.concatenate([r.astype(F32) for r in rows], axis=0)


def _unpack_small(a):
    return dict(ln_mix_pre=a[0:1], ln_mix_post=a[1:2], ln_mem=a[2:3], ln_ffn_pre=a[3:4], ln_ffn_post=a[4:5],
                swa_sinks=a[5:6, :SWA_Q_HEADS],
                rel_bias=a[6, :N_BUCKETS * SWA_Q_HEADS].reshape(N_BUCKETS, SWA_Q_HEADS))


def kernel(x, mem, ln_mix_pre, ln_mix_post, w_in, swa_sinks, rel_bias, ln_mem, w_mem_kv, w_branch_swa, w_branch_sb, w_branch_mem, w_out, ln_ffn_pre, ln_ffn_post, w_gate, w_up, w_down, loss_target, m_ln_mix_pre, m_ln_mix_post, m_w_in, m_swa_sinks, m_rel_bias, m_ln_mem, m_w_mem_kv, m_w_branch_swa, m_w_branch_sb, m_w_branch_mem, m_w_out, m_ln_ffn_pre, m_ln_ffn_post, m_w_gate, m_w_up, m_w_down, v_ln_mix_pre, v_ln_mix_post, v_w_in, v_swa_sinks, v_rel_bias, v_ln_mem, v_w_mem_kv, v_w_branch_swa, v_w_branch_sb, v_w_branch_mem, v_w_out, v_ln_ffn_pre, v_ln_ffn_post, v_w_gate, v_w_up, v_w_down):
    w = dict(ln_mix_pre=ln_mix_pre, ln_mix_post=ln_mix_post, w_in=w_in[0], swa_sinks=swa_sinks, rel_bias=rel_bias,
             ln_mem=ln_mem, w_mem_kv=w_mem_kv[0], w_branch_swa=w_branch_swa[0], w_branch_sb=w_branch_sb[0],
             w_branch_mem=w_branch_mem[0], w_out=w_out[0], ln_ffn_pre=ln_ffn_pre, ln_ffn_post=ln_ffn_post,
             w_gate=w_gate[0], w_up=w_up[0], w_down=w_down[0])
    mom = dict(ln_mix_pre=m_ln_mix_pre, ln_mix_post=m_ln_mix_post, w_in=m_w_in[0], swa_sinks=m_swa_sinks,
               rel_bias=m_rel_bias, ln_mem=m_ln_mem, w_mem_kv=m_w_mem_kv[0], w_branch_swa=m_w_branch_swa[0],
               w_branch_sb=m_w_branch_sb[0], w_branch_mem=m_w_branch_mem[0], w_out=m_w_out[0],
               ln_ffn_pre=m_ln_ffn_pre, ln_ffn_post=m_ln_ffn_post, w_gate=m_w_gate[0], w_up=m_w_up[0],
               w_down=m_w_down[0])
    var = dict(ln_mix_pre=v_ln_mix_pre, ln_mix_post=v_ln_mix_post, w_in=v_w_in[0], swa_sinks=v_swa_sinks,
               rel_bias=v_rel_bias, ln_mem=v_ln_mem, w_mem_kv=v_w_mem_kv[0], w_branch_swa=v_w_branch_swa[0],
               w_branch_sb=v_w_branch_sb[0], w_branch_mem=v_w_branch_mem[0], w_out=v_w_out[0],
               ln_ffn_pre=v_ln_ffn_pre, ln_ffn_post=v_ln_ffn_post, w_gate=v_w_gate[0], w_up=v_w_up[0],
               w_down=v_w_down[0])
    B, S, D = x.shape
    M = mem.shape[1]
    T = B * S
    F = D_FF
    x2 = x.reshape(T, D)
    mem2 = mem.reshape(B * M, D)
    t2 = loss_target.reshape(T, D)
    buckets = jnp.asarray(_swa_buckets())
    wb = {n: w[n].astype(BF16) for n in _BIG}
    full = {}

    def landed(names, got):
        for n, g in zip(names, got):
            full[n] = _cols_full(g) if n in _COL_SHARDED else g.reshape(-1, g.shape[-1])

    def shards(n, g):
        return _col_shards(g) if n in _COL_SHARDED else _row_shards(g)

    landed(["w_in"], [_all_gather(wb["w_in"], "ag_w_in")])
    u = _rms_fwd(x2, ln_mix_pre, "rms_mix_pre")
    early = ["w_mem_kv", "w_branch_swa", "w_branch_sb", "w_branch_mem"]
    proj, got = _matmul([(u, full["w_in"])], "nn", BF16, 512, IN_W // 2, D, "proj_in",
                        hosted=_Hosted(gathers=[wb[n] for n in early]))
    landed(early, got)
    mn = _rms_fwd(mem2, ln_mem, "rms_mem")
    mkv = _matmul([(mn, full["w_mem_kv"])], "nn", BF16, 512, 1024, D, "proj_mem")
    bias_tab = _swa_bias_table(rel_bias, buckets)
    y_swa, got = _swa_fwd(proj, swa_sinks, bias_tab, B, S, hosted=_Hosted(gathers=[wb["w_out"]]))
    landed(["w_out"], got)
    late = ["w_gate", "w_up", "w_down"]
    y_sb, rtot, got = _sb_fwd(proj, B, S, hosted=_Hosted(gathers=[wb[n] for n in late]))
    landed(late, got)
    y_mem = _mem_fwd(proj, mkv, B, S, M)
    wbs = (full["w_branch_swa"], full["w_branch_sb"], full["w_branch_mem"])
    merged, p_swa, p_sb, p_mem = _branch_gate(proj, (y_swa, y_sb, y_mem), wbs)
    mix = _matmul([(merged, full["w_out"])], "nn", F32, 512, 1024, D, "proj_out")
    h1, u2 = _post_pre(x2, mix, ln_mix_post, ln_ffn_pre)
    a, zg, zu = _ffn_up(u2, full["w_gate"], full["w_up"])
    ffn = _matmul([(a, full["w_down"])], "nn", F32, 512, 1024, F, "ffn_down")
    dffn, dh2, loss_tile, d_ln_ffn_post = _loss_head(ffn, h1, t2, ln_ffn_post)
    loss = lax.psum(loss_tile[0, 0], ("x", "y", "c"))

    part = {}
    part["w_down"] = _matmul([(a, dffn)], "tn", BF16, F // 2, 1024, 512, "dw_down")
    dzg, dzu = _ffn_down_bwd(dffn, full["w_down"], zg, zu)
    part["w_gate"] = _matmul([(u2, dzg)], "tn", BF16, 1024, F // 2, 512, "dw_gate")
    part["w_up"] = _matmul([(u2, dzu)], "tn", BF16, 1024, F // 2, 512, "dw_up")
    du2 = _matmul([(dzg, full["w_gate"]), (dzu, full["w_up"])], "nt", F32, 512, 1024, F // 2, "d_u2")
    dh1, dmix, d_ln_ffn_pre, d_ln_mix_post = _mid_bwd(h1, du2, dh2, mix, ln_ffn_pre, ln_mix_post)
    part["w_out"] = _matmul([(merged, dmix)], "tn", BF16, 1024, 1024, 512, "dw_out")
    dmerged = _matmul([(dmix, full["w_out"])], "nt", F32, 512, 1024, D, "d_merged")
    (dp_swa, dp_sb, dp_mem, dg0, dg1, dg2, dy_swa, dy_sb, dy_mem) = _gate_bwd(
        dmerged, (p_swa, p_sb, p_mem), proj, wbs)
    part["w_branch_swa"] = _matmul([(y_swa, dp_swa)], "tn", BF16, 512, 1024, 512, "dw_branch_swa")
    part["w_branch_sb"] = _matmul([(y_sb, dp_sb)], "tn", BF16, 512, 1024, 512, "dw_branch_sb")
    part["w_branch_mem"] = _matmul([(y_mem, dp_mem)], "tn", BF16, 512, 1024, 512, "dw_branch_mem")
    dqa, dka, dva, dbias, dsink = _swa_bwd(proj, dy_swa, swa_sinks, bias_tab, B, S)
    behind_sb = ["w_down", "w_gate", "w_up", "w_out", "w_branch_swa", "w_branch_sb", "w_branch_mem"]
    dqb, dkb, dvb, got = _sb_bwd(proj, dy_sb, rtot, B, S,
                                 hosted=_Hosted(scatters=[shards(n, part[n]) for n in behind_sb]))
    recv = dict(zip(behind_sb, got))
    dqm, dmk, dmv = _mem_bwd(proj, mkv, dy_mem, B, S, M)
    d_rel_bias = _swa_bias_grad(dbias, buckets)[:, :SWA_Q_HEADS]
    d_sinks = dsink[:, 0].reshape(1, SWA_Q_HEADS)
    dmkv = jnp.concatenate([dmk, dmv], axis=1).astype(BF16)
    part["w_mem_kv"] = _matmul([(mn, dmkv)], "tn", BF16, 1024, 1024, 512, "dw_mem_kv")
    dmn = _matmul([(dmkv, full["w_mem_kv"])], "nt", F32, 512, 1024, 1024, "d_mn")
    d_ln_mem = _gain_grad(mem2, dmn)
    dproj = jnp.concatenate([dqa, dka.astype(BF16), dva.astype(BF16), dqb, dkb.astype(BF16), dvb.astype(BF16),
                             dqm, dg0, dg1, dg2], axis=1)
    part["w_in"] = _matmul([(u, dproj)], "tn", BF16, 512, IN_W // 2, 512, "dw_in")
    flight, dproj = _scatter_start(_col_shards(part["w_in"]), dproj, "rs_w_in_start")
    du, got = _matmul([(dproj, full["w_in"])], "nt", F32, 512, 1024, IN_W // 2, "d_u",
                      hosted=_Hosted(scatters=[_row_shards(part["w_mem_kv"])]))
    recv["w_mem_kv"] = got[0]
    grad_x, d_ln_mix_pre = _pre_bwd(x2, du, dh1, ln_mix_pre)

    out = {n: _adamw(recv[n], w[n], mom[n], var[n], "adamw_" + n) for n in _BIG if n != "w_in"}
    small_grads = dict(ln_mix_pre=d_ln_mix_pre, ln_mix_post=d_ln_mix_post, swa_sinks=d_sinks, rel_bias=d_rel_bias,
                       ln_mem=d_ln_mem, ln_ffn_pre=d_ln_ffn_pre, ln_ffn_post=d_ln_ffn_post)
    small_parts = _all_gather(_pack_small(small_grads), "ag_small")
    res = _adamw(small_parts, _pack_small(w), _pack_small(mom), _pack_small(var), "adamw_small")
    small = [_unpack_small(r) for r in res]
    for n in _SMALL:
        out[n] = tuple(s[n] for s in small)
    recv["w_in"] = _scatter_wait(flight, res[0], "rs_w_in_wait")
    out["w_in"] = _adamw(recv["w_in"], w["w_in"], mom["w_in"], var["w_in"], "adamw_w_in")

    like = dict(ln_mix_pre=ln_mix_pre, ln_mix_post=ln_mix_post, w_in=w_in, swa_sinks=swa_sinks, rel_bias=rel_bias,
                ln_mem=ln_mem, w_mem_kv=w_mem_kv, w_branch_swa=w_branch_swa, w_branch_sb=w_branch_sb,
                w_branch_mem=w_branch_mem, w_out=w_out, ln_ffn_pre=ln_ffn_pre, ln_ffn_post=ln_ffn_post,
                w_gate=w_gate, w_up=w_up, w_down=w_down)
    result = [loss, grad_x.reshape(B, S, D)]
    for k in range(4):
        result += [out[n][k].reshape(like[n].shape) for n in _ORDER]
    return tuple(result)
```

```python
import functools
import math

import numpy as np
import jax
import jax.numpy as jnp
from jax import lax
from jax.experimental import pallas as pl
from jax.experimental.pallas import tpu as pltpu

F32 = jnp.float32
BF16 = jnp.bfloat16

N_DEV = 8
D_MODEL = 1024
BLOCK = 128
EPS = 1e-6
HEAD_DIM = 64
SWA_Q_HEADS = 8
SWA_WINDOW = 128
N_BUCKETS = 32
MAX_DISTANCE = 128
MEM_HEADS = 4
MEM_HEAD_DIM = 128
D_FF = 2816
IN_W = 5888
COL_QA, COL_KA, COL_VA, COL_QB, COL_KB, COL_VB, COL_QM, COL_GL = 0, 4, 5, 6, 10, 14, 18, 22
SCALE64 = HEAD_DIM ** -0.5
SCALE128 = MEM_HEAD_DIM ** -0.5
NEG = -1e30

ADAM_LR = 0.001
ADAM_B1 = 0.9
ADAM_B2 = 0.999
ADAM_EPS = 1e-08
ADAM_WD = 0.01
ADAM_STEP = 10

VMEM_LIMIT_BYTES = 56 * 1024 * 1024


def _params(**kw):
    return pltpu.CompilerParams(vmem_limit_bytes=VMEM_LIMIT_BYTES, **kw)


def _dot(a, b):
    return jnp.dot(a, b, preferred_element_type=F32)


def _dot_nt(a, b):
    return lax.dot_general(a, b, (((1,), (1,)), ((), ())), preferred_element_type=F32)


def _dot_tn(a, b):
    return lax.dot_general(a, b, (((0,), (0,)), ((), ())), preferred_element_type=F32)


def _dot_split(x, m):
    hi = x.astype(BF16)
    lo = (x - hi.astype(F32)).astype(BF16)
    return _dot(hi, m) + _dot(lo, m)


def _mesh_pos():
    return lax.axis_index("x"), lax.axis_index("y"), lax.axis_index("c")


class _Hosted:
    def __init__(self, gathers=(), scatters=()):
        self.items = [("g", a) for a in gathers] + [("s", a) for a in scatters]
        self.n = len(self.items)

    def operands(self):
        return [a for _, a in self.items]

    def specs(self):
        return [pl.BlockSpec(memory_space=pl.ANY)] * self.n

    def out_shapes(self):
        return [jax.ShapeDtypeStruct(((N_DEV,) + a.shape) if kind == "g" else a.shape, a.dtype)
                for kind, a in self.items]

    def scratch(self):
        return [pltpu.SemaphoreType.DMA((7 * self.n,)), pltpu.SemaphoreType.DMA((7 * self.n,)),
                pltpu.SemaphoreType.DMA((self.n,))]

    def copies(self, in_refs, out_refs, send_sems, recv_sems, local_sems):
        x, y, c = _mesh_pos()
        me = 4 * x + 2 * y + c
        out = []
        for t, (kind, _) in enumerate(self.items):
            own = in_refs[t] if kind == "g" else in_refs[t].at[me]
            out.append(pltpu.make_async_copy(own, out_refs[t].at[me], local_sems.at[t]))
            for k in range(1, N_DEV):
                px, py, pc = x ^ (k >> 2), y ^ ((k >> 1) & 1), c ^ (k & 1)
                src = in_refs[t] if kind == "g" else in_refs[t].at[4 * px + 2 * py + pc]
                out.append(pltpu.make_async_remote_copy(
                    src_ref=src, dst_ref=out_refs[t].at[me],
                    send_sem=send_sems.at[7 * t + k - 1], recv_sem=recv_sems.at[7 * t + k - 1],
                    device_id=(px, py, pc), device_id_type=pl.DeviceIdType.MESH))
        return out


def _host(body, n_in, n_out, hosted, grid):
    if hosted is None:
        return body
    nc = hosted.n

    def wrapped(*refs):
        ins = refs[:n_in]
        cin = refs[n_in:n_in + nc]
        outs = refs[n_in + nc:n_in + nc + n_out]
        cout = refs[n_in + nc + n_out:n_in + 2 * nc + n_out]
        scratch = refs[n_in + 2 * nc + n_out:len(refs) - 3]
        sems = refs[len(refs) - 3:]
        ids = [pl.program_id(d) for d in range(len(grid))]
        first = functools.reduce(lambda a, b: a & b, [i == 0 for i in ids])
        last = functools.reduce(lambda a, b: a & b, [i == g - 1 for i, g in zip(ids, grid)])

        @pl.when(first)
        def _():
            for cp in hosted.copies(cin, cout, *sems):
                cp.start()

        body(*ins, *outs, *scratch)

        @pl.when(last)
        def _():
            for cp in hosted.copies(cin, cout, *sems):
                cp.wait()

    return wrapped


def _hosted_call(body, grid, in_specs, out_specs, out_shape, scratch_shapes, hosted, name, args):
    n_out = len(out_specs)
    if hosted is None:
        outs = pl.pallas_call(body, grid=grid, in_specs=in_specs, out_specs=out_specs, out_shape=out_shape,
                              scratch_shapes=scratch_shapes, compiler_params=_params(), name=name)(*args)
        return list(outs), []
    outs = pl.pallas_call(
        _host(body, len(in_specs), n_out, hosted, grid),
        grid=grid,
        in_specs=list(in_specs) + hosted.specs(),
        out_specs=list(out_specs) + hosted.specs(),
        out_shape=list(out_shape) + hosted.out_shapes(),
        scratch_shapes=list(scratch_shapes) + hosted.scratch(),
        compiler_params=_params(),
        name=name,
    )(*args, *hosted.operands())
    return list(outs[:n_out]), list(outs[n_out:])


_DIMS = {"nn": (((1,), (0,)), ((), ())), "nt": (((1,), (1,)), ((), ())), "tn": (((0,), (0,)), ((), ()))}


def _matmul(pairs, mode, out_dtype, tm, tn, tk, name, hosted=None):
    a0, b0 = pairs[0]
    if mode == "nn":
        (M, K), N = a0.shape, b0.shape[1]
    elif mode == "nt":
        (M, K), N = a0.shape, b0.shape[0]
    else:
        (K, M), N = a0.shape, b0.shape[1]
    tm, tn, tk = min(tm, M), min(tn, N), min(tk, K)
    assert M % tm == 0 and N % tn == 0 and K % tk == 0, (name, M, N, K, tm, tn, tk)
    nm, nn, nk = M // tm, N // tn, K // tk
    npair = len(pairs)
    dims = _DIMS[mode]

    def body(*refs):
        ab = refs[:2 * npair]
        o_ref = refs[2 * npair]
        acc_ref = refs[2 * npair + 1]
        k = pl.program_id(2)
        part = lax.dot_general(ab[0][...], ab[1][...], dims, preferred_element_type=F32)
        for q in range(1, npair):
            part += lax.dot_general(ab[2 * q][...], ab[2 * q + 1][...], dims, preferred_element_type=F32)
        if nk == 1:
            o_ref[...] = part.astype(o_ref.dtype)
        else:
            @pl.when(k == 0)
            def _():
                acc_ref[...] = part

            @pl.when(k > 0)
            def _():
                acc_ref[...] += part

            @pl.when(k == nk - 1)
            def _():
                o_ref[...] = acc_ref[...].astype(o_ref.dtype)

    if mode == "nn":
        a_spec = pl.BlockSpec((tm, tk), lambda n, m, k: (m, k))
        b_spec = pl.BlockSpec((tk, tn), lambda n, m, k: (k, n))
    elif mode == "nt":
        a_spec = pl.BlockSpec((tm, tk), lambda n, m, k: (m, k))
        b_spec = pl.BlockSpec((tn, tk), lambda n, m, k: (n, k))
    else:
        a_spec = pl.BlockSpec((tk, tm), lambda n, m, k: (k, m))
        b_spec = pl.BlockSpec((tk, tn), lambda n, m, k: (k, n))
    args = [t for pr in pairs for t in pr]
    outs, moved = _hosted_call(
        body, (nn, nm, nk), [a_spec, b_spec] * npair, [pl.BlockSpec((tm, tn), lambda n, m, k: (m, n))],
        [jax.ShapeDtypeStruct((M, N), out_dtype)], [pltpu.VMEM((tm, tn) if nk > 1 else (8, 128), F32)],
        hosted, name, args)
    return outs[0] if hosted is None else (outs[0], moved)


def _rms_fwd(x, g, name):
    T, D = x.shape
    tr = min(512, T)

    def body(x_ref, g_ref, u_ref):
        xf = x_ref[...]
        r = lax.rsqrt(jnp.mean(xf * xf, axis=-1, keepdims=True) + EPS)
        u_ref[...] = ((xf * r) * g_ref[...]).astype(u_ref.dtype)

    return pl.pallas_call(
        body,
        grid=(T // tr,),
        in_specs=[pl.BlockSpec((tr, D), lambda i: (i, 0)), pl.BlockSpec((1, D), lambda i: (0, 0))],
        out_specs=pl.BlockSpec((tr, D), lambda i: (i, 0)),
        out_shape=jax.ShapeDtypeStruct((T, D), BF16),
        name=name,
    )(x, g)


def _rms_bwd_terms(xin, g, dy):
    r = lax.rsqrt(jnp.mean(xin * xin, axis=-1, keepdims=True) + EPS)
    xh = xin * r
    dg = jnp.sum(dy * xh, axis=0, keepdims=True)
    dxh = dy * g
    dx = r * (dxh - xh * jnp.mean(dxh * xh, axis=-1, keepdims=True))
    return dx, dg


GATE_TC = 256


def _branch_gate(proj, ys, wbs):
    T = proj.shape[0]
    D = D_MODEL
    tr, tc = min(1024, T), GATE_TC
    nc = D // tc
    gl0 = COL_GL * 128 // tc

    def body(ya, yb, yc, wa, wb, wc, g0, g1, g2, merged_ref, pa, pb, pc):
        acc = jnp.zeros((tr, tc), F32)
        for y_ref, w_ref, g_ref, p_ref in ((ya, wa, g0, pa), (yb, wb, g1, pb), (yc, wc, g2, pc)):
            p = _dot(y_ref[...], w_ref[...])
            p_ref[...] = p.astype(p_ref.dtype)
            acc += jax.nn.sigmoid(g_ref[...].astype(F32)) * p
        merged_ref[...] = acc.astype(merged_ref.dtype)

    y_spec = pl.BlockSpec((tr, 512), lambda i, n: (i, 0))
    w_spec = pl.BlockSpec((512, tc), lambda i, n: (0, n))
    o_spec = pl.BlockSpec((tr, tc), lambda i, n: (i, n))
    gl_specs = [pl.BlockSpec((tr, tc), lambda i, n, j=j: (i, gl0 + j * nc + n)) for j in range(3)]
    out = jax.ShapeDtypeStruct((T, D), BF16)
    return pl.pallas_call(
        body,
        grid=(T // tr, nc),
        in_specs=[y_spec] * 3 + [w_spec] * 3 + gl_specs,
        out_specs=[o_spec] * 4,
        out_shape=[out] * 4,
        compiler_params=_params(),
        name="branch_gate",
    )(*ys, *wbs, proj, proj, proj)


def _post_pre(x, mix, g_post, g_pre):
    T, D = x.shape
    tr = 512

    def body(x_ref, mix_ref, gp_ref, gq_ref, h1_ref, u2_ref):
        mx = mix_ref[...]
        r = lax.rsqrt(jnp.mean(mx * mx, axis=-1, keepdims=True) + EPS)
        h1 = x_ref[...] + (mx * r) * gp_ref[...]
        h1_ref[...] = h1
        r2 = lax.rsqrt(jnp.mean(h1 * h1, axis=-1, keepdims=True) + EPS)
        u2_ref[...] = ((h1 * r2) * gq_ref[...]).astype(u2_ref.dtype)

    row = pl.BlockSpec((tr, D), lambda i: (i, 0))
    vec = pl.BlockSpec((1, D), lambda i: (0, 0))
    return pl.pallas_call(
        body,
        grid=(T // tr,),
        in_specs=[row, row, vec, vec],
        out_specs=[row, row],
        out_shape=[jax.ShapeDtypeStruct((T, D), F32), jax.ShapeDtypeStruct((T, D), BF16)],
        compiler_params=_params(),
        name="post_pre",
    )(x, mix, g_post, g_pre)


def _ffn_up(u2, w_gate_t, w_up_t):
    T, D = u2.shape
    F = D_FF
    tm, tn = 512, F // 2

    def body(u_ref, wg_ref, wu_ref, a_ref, zg_ref, zu_ref):
        u = u_ref[...]
        zg = _dot_nt(u, wg_ref[...])
        zu = _dot_nt(u, wu_ref[...])
        a_ref[...] = (zg * jax.nn.sigmoid(zg) * zu).astype(a_ref.dtype)
        zg_ref[...] = zg.astype(zg_ref.dtype)
        zu_ref[...] = zu.astype(zu_ref.dtype)

    o_spec = pl.BlockSpec((tm, tn), lambda n, m: (m, n))
    out = jax.ShapeDtypeStruct((T, F), BF16)
    return pl.pallas_call(
        body,
        grid=(F // tn, T // tm),
        in_specs=[pl.BlockSpec((tm, D), lambda n, m: (m, 0)),
                  pl.BlockSpec((tn, D), lambda n, m: (n, 0)),
                  pl.BlockSpec((tn, D), lambda n, m: (n, 0))],
        out_specs=[o_spec] * 3,
        out_shape=[out] * 3,
        compiler_params=_params(),
        name="ffn_up",
    )(u2, w_gate_t, w_up_t)


def _loss_head(ffn, h1, target, g_post):
    T, D = ffn.shape
    tr = 512

    def body(f_ref, h1_ref, t_ref, g_ref, dffn_ref, dh2_ref, loss_ref, dg_ref):
        i = pl.program_id(0)
        f = f_ref[...]
        g = g_ref[...]
        r = lax.rsqrt(jnp.mean(f * f, axis=-1, keepdims=True) + EPS)
        xh = f * r
        err = (h1_ref[...] + xh * g) - t_ref[...]
        part = 0.5 * jnp.sum(jnp.mean(err * err, axis=-1, keepdims=True), axis=0, keepdims=True)
        dh2 = err * (1.0 / D)
        dh2_ref[...] = dh2
        dgp = jnp.sum(dh2 * xh, axis=0, keepdims=True)
        dxh = dh2 * g
        dffn_ref[...] = (r * (dxh - xh * jnp.mean(dxh * xh, axis=-1, keepdims=True))).astype(dffn_ref.dtype)

        @pl.when(i == 0)
        def _():
            loss_ref[...] = jnp.zeros_like(loss_ref)
            dg_ref[...] = jnp.zeros_like(dg_ref)

        loss_ref[...] += jnp.broadcast_to(part, loss_ref.shape)
        dg_ref[...] += dgp

    row = pl.BlockSpec((tr, D), lambda i: (i, 0))
    vec = pl.BlockSpec((1, D), lambda i: (0, 0))
    return pl.pallas_call(
        body,
        grid=(T // tr,),
        in_specs=[row, row, row, vec],
        out_specs=[row, row, pl.BlockSpec((8, 128), lambda i: (0, 0)), vec],
        out_shape=[jax.ShapeDtypeStruct((T, D), BF16), jax.ShapeDtypeStruct((T, D), F32),
                   jax.ShapeDtypeStruct((8, 128), F32), jax.ShapeDtypeStruct((1, D), F32)],
        compiler_params=_params(),
        name="loss_head",
    )(ffn, h1, target, g_post)


def _ffn_down_bwd(dffn, wd, zg, zu):
    T, D = dffn.shape
    F = D_FF
    tm, tn = 512, F // 2

    def body(d_ref, w_ref, zg_ref, zu_ref, dzg_ref, dzu_ref):
        da = _dot_nt(d_ref[...], w_ref[...])
        zg = zg_ref[...].astype(F32)
        zu = zu_ref[...].astype(F32)
        s = jax.nn.sigmoid(zg)
        dzu_ref[...] = (da * (zg * s)).astype(dzu_ref.dtype)
        dzg_ref[...] = (da * zu * (s * (1.0 + zg * (1.0 - s)))).astype(dzg_ref.dtype)

    z_spec = pl.BlockSpec((tm, tn), lambda n, m: (m, n))
    out = jax.ShapeDtypeStruct((T, F), BF16)
    return pl.pallas_call(
        body,
        grid=(F // tn, T // tm),
        in_specs=[pl.BlockSpec((tm, D), lambda n, m: (m, 0)), pl.BlockSpec((tn, D), lambda n, m: (n, 0)),
                  z_spec, z_spec],
        out_specs=[z_spec, z_spec],
        out_shape=[out, out],
        compiler_params=_params(),
        name="ffn_down_bwd",
    )(dffn, wd, zg, zu)


def _mid_bwd(h1, du2, dh2, mix, g_pre, g_post):
    T, D = h1.shape
    tr = 512

    def body(h1_ref, du2_ref, dh2_ref, mix_ref, gq_ref, gp_ref, dh1_ref, dmix_ref, dgq_ref, dgp_ref):
        i = pl.program_id(0)
        dx, dgq = _rms_bwd_terms(h1_ref[...], gq_ref[...], du2_ref[...])
        dh1 = dh2_ref[...] + dx
        dh1_ref[...] = dh1
        dmix, dgp = _rms_bwd_terms(mix_ref[...], gp_ref[...], dh1)
        dmix_ref[...] = dmix.astype(dmix_ref.dtype)

        @pl.when(i == 0)
        def _():
            dgq_ref[...] = jnp.zeros_like(dgq_ref)
            dgp_ref[...] = jnp.zeros_like(dgp_ref)

        dgq_ref[...] += dgq
        dgp_ref[...] += dgp

    row = pl.BlockSpec((tr, D), lambda i: (i, 0))
    vec = pl.BlockSpec((1, D), lambda i: (0, 0))
    return pl.pallas_call(
        body,
        grid=(T // tr,),
        in_specs=[row, row, row, row, vec, vec],
        out_specs=[row, row, vec, vec],
        out_shape=[jax.ShapeDtypeStruct((T, D), F32), jax.ShapeDtypeStruct((T, D), BF16),
                   jax.ShapeDtypeStruct((1, D), F32), jax.ShapeDtypeStruct((1, D), F32)],
        compiler_params=_params(),
        name="mid_bwd",
    )(h1, du2, dh2, mix, g_pre, g_post)


def _gate_bwd(dmerged, ps, proj, wbs):
    T, D = dmerged.shape
    tr, tc = min(1024, T), GATE_TC
    nc = D // tc
    gl0 = COL_GL * 128 // tc

    def body(dm_ref, pa, pb, pc, g0, g1, g2, wa, wb, wc, dpa, dpb, dpc, dga, dgb, dgc, dya, dyb, dyc,
             acc_a, acc_b, acc_c):
        n = pl.program_id(1)
        dm = dm_ref[...]
        for p_ref, g_ref, w_ref, dp_ref, dg_ref, dy_ref, acc_ref in (
                (pa, g0, wa, dpa, dga, dya, acc_a), (pb, g1, wb, dpb, dgb, dyb, acc_b),
                (pc, g2, wc, dpc, dgc, dyc, acc_c)):
            s = jax.nn.sigmoid(g_ref[...].astype(F32))
            dp = (dm * s).astype(BF16)
            dp_ref[...] = dp
            dg_ref[...] = (dm * p_ref[...].astype(F32) * (s * (1.0 - s))).astype(dg_ref.dtype)
            part = _dot_nt(dp, w_ref[...])

            @pl.when(n == 0)
            def _():
                acc_ref[...] = part

            @pl.when(n > 0)
            def _():
                acc_ref[...] += part

            @pl.when(n == nc - 1)
            def _():
                dy_ref[...] = acc_ref[...].astype(dy_ref.dtype)

    col = pl.BlockSpec((tr, tc), lambda i, n: (i, n))
    y_spec = pl.BlockSpec((tr, 512), lambda i, n: (i, 0))
    w_spec = pl.BlockSpec((512, tc), lambda i, n: (0, n))
    gl_specs = [pl.BlockSpec((tr, tc), lambda i, n, j=j: (i, gl0 + j * nc + n)) for j in range(3)]
    big = jax.ShapeDtypeStruct((T, D), BF16)
    small = jax.ShapeDtypeStruct((T, 512), BF16)
    return pl.pallas_call(
        body,
        grid=(T // tr, nc),
        in_specs=[col] * 4 + gl_specs + [w_spec] * 3,
        out_specs=[col] * 6 + [y_spec] * 3,
        out_shape=[big] * 6 + [small] * 3,
        scratch_shapes=[pltpu.VMEM((tr, 512), F32)] * 3,
        compiler_params=_params(),
        name="gate_bwd",
    )(dmerged, *ps, proj, proj, proj, *wbs)


def _pre_bwd(x, du, dh1, g):
    T, D = x.shape
    tr = 512

    def body(x_ref, du_ref, dh1_ref, g_ref, gx_ref, dg_ref):
        i = pl.program_id(0)
        dx, dg = _rms_bwd_terms(x_ref[...], g_ref[...], du_ref[...])
        gx_ref[...] = dh1_ref[...] + dx

        @pl.when(i == 0)
        def _():
            dg_ref[...] = jnp.zeros_like(dg_ref)

        dg_ref[...] += dg

    row = pl.BlockSpec((tr, D), lambda i: (i, 0))
    vec = pl.BlockSpec((1, D), lambda i: (0, 0))
    return pl.pallas_call(
        body,
        grid=(T // tr,),
        in_specs=[row, row, row, vec],
        out_specs=[row, vec],
        out_shape=[jax.ShapeDtypeStruct((T, D), F32), jax.ShapeDtypeStruct((1, D), F32)],
        compiler_params=_params(),
        name="pre_bwd",
    )(x, du, dh1, g)


def _gain_grad(xin, dy):
    T, D = xin.shape
    tr = min(512, T)

    def body(x_ref, dy_ref, dg_ref):
        i = pl.program_id(0)
        xf = x_ref[...]
        r = lax.rsqrt(jnp.mean(xf * xf, axis=-1, keepdims=True) + EPS)

        @pl.when(i == 0)
        def _():
            dg_ref[...] = jnp.zeros_like(dg_ref)

        dg_ref[...] += jnp.sum(dy_ref[...] * (xf * r), axis=0, keepdims=True)

    row = pl.BlockSpec((tr, D), lambda i: (i, 0))
    return pl.pallas_call(
        body,
        grid=(T // tr,),
        in_specs=[row, row],
        out_specs=pl.BlockSpec((1, D), lambda i: (0, 0)),
        out_shape=jax.ShapeDtypeStruct((1, D), F32),
        name="gain_grad",
    )(xin, dy)


def _swa_buckets():
    dist = (np.arange(BLOCK)[:, None] + BLOCK) - np.arange(2 * BLOCK)[None, :]
    max_exact = N_BUCKETS // 2
    d = np.maximum(dist, 0)
    df = np.maximum(d, 1).astype(np.float32)
    large = max_exact + (np.log(df / np.float32(max_exact)) / np.float32(math.log(MAX_DISTANCE / max_exact))
                         * np.float32(N_BUCKETS - max_exact)).astype(np.int32)
    large = np.minimum(large, N_BUCKETS - 1)
    bucket = np.where(d < max_exact, d, large)
    in_win = (dist >= 0) & (dist < SWA_WINDOW)
    return np.where(in_win, bucket, -1).astype(np.int32)


def _swa_bias_table(rel_bias, buckets):
    H = SWA_Q_HEADS

    def body(rb_ref, bk_ref, o_ref):
        bk = bk_ref[...]
        for h in range(H):
            acc = jnp.full(bk.shape, NEG, F32)
            for b in range(N_BUCKETS):
                acc = jnp.where(bk == b, rb_ref[b, h], acc)
            o_ref[h] = acc

    return pl.pallas_call(
        body,
        in_specs=[pl.BlockSpec(memory_space=pltpu.SMEM), pl.BlockSpec(memory_space=pltpu.VMEM)],
        out_specs=pl.BlockSpec(memory_space=pltpu.VMEM),
        out_shape=jax.ShapeDtypeStruct((H, BLOCK, 2 * BLOCK), F32),
        name="swa_bias_table",
    )(rel_bias, buckets)


def _swa_bias_grad(dbias, buckets):
    H = SWA_Q_HEADS

    def body(db_ref, bk_ref, o_ref):
        bk = bk_ref[...]
        rows = lax.broadcasted_iota(jnp.int32, (N_BUCKETS, 128), 0)
        lanes = lax.broadcasted_iota(jnp.int32, (N_BUCKETS, 128), 1)
        acc = jnp.zeros((N_BUCKETS, 128), F32)
        for h in range(H):
            d = db_ref[h]
            for b in range(N_BUCKETS):
                s = jnp.sum(jnp.sum(jnp.where(bk == b, d, 0.0), axis=1, keepdims=True), axis=0, keepdims=True)
                acc = jnp.where((rows == b) & (lanes == h), s, acc)
        o_ref[...] = acc

    return pl.pallas_call(
        body,
        in_specs=[pl.BlockSpec(memory_space=pltpu.VMEM)] * 2,
        out_specs=pl.BlockSpec(memory_space=pltpu.VMEM),
        out_shape=jax.ShapeDtypeStruct((N_BUCKETS, 128), F32),
        name="swa_bias_grad",
    )(dbias, buckets)


SWA_GROUP = 4


def _swa_stack(x, g):
    blocks = []
    for t in range(SWA_GROUP):
        h = SWA_GROUP * g + t
        xp = x[:, (h // 2) * BLOCK:(h // 2 + 1) * BLOCK]
        blocks.append(xp if h % 2 == g else pltpu.roll(xp, HEAD_DIM, 1))
    return jnp.concatenate(blocks, axis=0).astype(BF16)


def _swa_unstack(parts):
    pairs = []
    for p in range(4):
        g = p // 2
        halves = []
        for hh in range(2):
            t = 2 * p + hh - SWA_GROUP * g
            blk = parts[g][t * BLOCK:(t + 1) * BLOCK]
            halves.append(blk if hh == g else pltpu.roll(blk, HEAD_DIM, 1))
        pairs.append(halves[0] + halves[1])
    return jnp.concatenate(pairs, axis=1)


def _swa_group_inputs(g, sink_ref, bias_ref, kv_refs):
    lane = lax.broadcasted_iota(jnp.int32, (BLOCK, BLOCK), 1)
    km = (lane // HEAD_DIM) == g
    heads = range(SWA_GROUP * g, SWA_GROUP * (g + 1))
    bias = jnp.concatenate([bias_ref[h] for h in heads], axis=0)
    sink = jnp.concatenate([jnp.full((BLOCK, 1), sink_ref[0, h], F32) for h in heads], axis=0)
    return bias, sink, [jnp.where(km, r[...], 0) for r in kv_refs], km


def _swa_scores(qs, kp, kc, bias, sink, first):
    sp = _dot_nt(qs, kp) * SCALE64 + bias[:, :BLOCK]
    sp = jnp.where(first, NEG, sp)
    sc = _dot_nt(qs, kc) * SCALE64 + bias[:, BLOCK:]
    m = jnp.maximum(jnp.maximum(jnp.max(sp, axis=1, keepdims=True), jnp.max(sc, axis=1, keepdims=True)), sink)
    pp = jnp.exp(sp - m)
    pc = jnp.exp(sc - m)
    ps = jnp.exp(sink - m)
    den = jnp.sum(pp, axis=1, keepdims=True) + jnp.sum(pc, axis=1, keepdims=True) + ps
    return pp / den, pc / den, ps / den


def _swa_fwd(proj, sinks, bias_tab, B, S, hosted=None):
    nb = S // BLOCK
    T = B * S

    def body(sink_ref, q_ref, kp_ref, kc_ref, vp_ref, vc_ref, bias_ref, o_ref):
        i = pl.program_id(1)
        first = jnp.full((SWA_GROUP * BLOCK, BLOCK), i, jnp.int32) == 0
        q = q_ref[...].astype(F32)
        outs = []
        for g in range(2):
            bias, sink, (kp, kc, vp, vc), _ = _swa_group_inputs(g, sink_ref, bias_ref, (kp_ref, kc_ref, vp_ref, vc_ref))
            wp, wc, _ = _swa_scores(_swa_stack(q, g), kp, kc, bias, sink, first)
            outs.append(_dot(wp.astype(BF16), vp) + _dot(wc.astype(BF16), vc))
        o_ref[...] = _swa_unstack(outs).astype(o_ref.dtype)

    blk = (BLOCK, BLOCK)
    wide = (BLOCK, 4 * BLOCK)
    outs, moved = _hosted_call(
        body, (B, nb),
        [pl.BlockSpec(memory_space=pltpu.SMEM),
         pl.BlockSpec(wide, lambda b, i: (b * nb + i, COL_QA // 4)),
         pl.BlockSpec(blk, lambda b, i: (b * nb + jnp.maximum(i - 1, 0), COL_KA)),
         pl.BlockSpec(blk, lambda b, i: (b * nb + i, COL_KA)),
         pl.BlockSpec(blk, lambda b, i: (b * nb + jnp.maximum(i - 1, 0), COL_VA)),
         pl.BlockSpec(blk, lambda b, i: (b * nb + i, COL_VA)),
         pl.BlockSpec((SWA_Q_HEADS, BLOCK, 2 * BLOCK), lambda b, i: (0, 0, 0))],
        [pl.BlockSpec(wide, lambda b, i: (b * nb + i, 0))],
        [jax.ShapeDtypeStruct((T, 512), BF16)], [], hosted, "swa_fwd",
        (sinks, proj, proj, proj, proj, proj, bias_tab))
    return outs[0], moved


def _swa_bwd(proj, dy, sinks, bias_tab, B, S):
    nb = S // BLOCK
    T = B * S
    H = SWA_Q_HEADS

    def body(sink_ref, q_ref, kp_ref, kc_ref, vp_ref, vc_ref, do_ref, bias_ref,
             dq_ref, dk_ref, dv_ref, dbias_ref, dsink_ref):
        b = pl.program_id(0)
        i = pl.program_id(1)

        @pl.when((b == 0) & (i == 0))
        def _():
            dbias_ref[...] = jnp.zeros_like(dbias_ref)
            dsink_ref[...] = jnp.zeros_like(dsink_ref)

        @pl.when(i == 0)
        def _():
            dk_ref[...] = jnp.zeros_like(dk_ref)
            dv_ref[...] = jnp.zeros_like(dv_ref)

        first = jnp.full((SWA_GROUP * BLOCK, BLOCK), i, jnp.int32) == 0
        q = q_ref[...].astype(F32)
        do = do_ref[...].astype(F32)
        dqs, dsinks, dbs, dkp, dkc, dvp, dvc = [], [], [], [], [], [], []
        for g in range(2):
            bias, sink, (kp, kc, vp, vc), km = _swa_group_inputs(
                g, sink_ref, bias_ref, (kp_ref, kc_ref, vp_ref, vc_ref))
            qs = _swa_stack(q, g)
            dos = _swa_stack(do, g)
            wp, wc, ws = _swa_scores(qs, kp, kc, bias, sink, first)
            dwp = _dot_nt(dos, vp)
            dwc = _dot_nt(dos, vc)
            dsum = jnp.sum(wp * dwp, axis=1, keepdims=True) + jnp.sum(wc * dwc, axis=1, keepdims=True)
            dsp = wp * (dwp - dsum)
            dsc = wc * (dwc - dsum)
            dsk = -ws * dsum
            for t in range(SWA_GROUP):
                rows = slice(t * BLOCK, (t + 1) * BLOCK)
                dsinks.append(jnp.broadcast_to(jnp.sum(dsk[rows], axis=0, keepdims=True), (1, 128)))
                dbs.append(jnp.concatenate([dsp[rows], dsc[rows]], axis=1))
            dspb = dsp.astype(BF16)
            dscb = dsc.astype(BF16)
            dqs.append(_dot(dspb, kp) + _dot(dscb, kc))
            dkp.append(jnp.where(km, _dot_tn(dspb, qs), 0.0))
            dkc.append(jnp.where(km, _dot_tn(dscb, qs), 0.0))
            dvp.append(jnp.where(km, _dot_tn(wp.astype(BF16), dos), 0.0))
            dvc.append(jnp.where(km, _dot_tn(wc.astype(BF16), dos), 0.0))
        dq_ref[...] = (_swa_unstack(dqs) * SCALE64).astype(dq_ref.dtype)
        dsink_ref[...] += jnp.concatenate(dsinks, axis=0)
        for h in range(H):
            dbias_ref[h] += dbs[h]
        cur = pl.ds(pl.multiple_of(i * BLOCK, BLOCK), BLOCK)
        prev = pl.ds(pl.multiple_of(jnp.maximum(i - 1, 0) * BLOCK, BLOCK), BLOCK)
        dk_ref[prev, :] += (dkp[0] + dkp[1]) * SCALE64
        dk_ref[cur, :] += (dkc[0] + dkc[1]) * SCALE64
        dv_ref[prev, :] += dvp[0] + dvp[1]
        dv_ref[cur, :] += dvc[0] + dvc[1]

    blk = (BLOCK, BLOCK)
    wide = (BLOCK, 4 * BLOCK)
    kv_out = pl.BlockSpec((S, BLOCK), lambda b, i: (b, 0))
    full_bias = pl.BlockSpec((H, BLOCK, 2 * BLOCK), lambda b, i: (0, 0, 0))
    return pl.pallas_call(
        body,
        grid=(B, nb),
        in_specs=[pl.BlockSpec(memory_space=pltpu.SMEM),
                  pl.BlockSpec(wide, lambda b, i: (b * nb + i, COL_QA // 4)),
                  pl.BlockSpec(blk, lambda b, i: (b * nb + jnp.maximum(i - 1, 0), COL_KA)),
                  pl.BlockSpec(blk, lambda b, i: (b * nb + i, COL_KA)),
                  pl.BlockSpec(blk, lambda b, i: (b * nb + jnp.maximum(i - 1, 0), COL_VA)),
                  pl.BlockSpec(blk, lambda b, i: (b * nb + i, COL_VA)),
                  pl.BlockSpec(wide, lambda b, i: (b * nb + i, 0)),
                  full_bias],
        out_specs=[pl.BlockSpec(wide, lambda b, i: (b * nb + i, 0)),
                   kv_out, kv_out, full_bias,
                   pl.BlockSpec((H, 128), lambda b, i: (0, 0))],
        out_shape=[jax.ShapeDtypeStruct((T, 512), BF16),
                   jax.ShapeDtypeStruct((T, BLOCK), F32), jax.ShapeDtypeStruct((T, BLOCK), F32),
                   jax.ShapeDtypeStruct((H, BLOCK, 2 * BLOCK), F32), jax.ShapeDtypeStruct((H, 128), F32)],
        compiler_params=_params(),
        name="swa_bwd",
    )(sinks, proj, proj, proj, proj, proj, dy, bias_tab)


SB_TILE = 256


SB_HEADS = 4
SB_LANES = SB_HEADS * HEAD_DIM
SB_ROWS = SB_HEADS * SB_TILE


def _sb_logits(qs, kj, tri):
    z = _dot_nt(qs, kj)
    sp = jnp.log(1.0 + jnp.exp(-jnp.abs(z)))
    ls = jnp.minimum(z, 0.0) - sp
    l1m = ls - z
    if tri is not None:
        l1m = jnp.where(tri, l1m, 0.0)
    return ls, l1m


def _sb_masks():
    lane = lax.broadcasted_iota(jnp.int32, (SB_TILE, SB_LANES), 1)
    hm = [(lane // HEAD_DIM) == h for h in range(SB_HEADS)]
    row = lax.broadcasted_iota(jnp.int32, (SB_ROWS, SB_TILE), 0) % SB_TILE
    col = lax.broadcasted_iota(jnp.int32, (SB_ROWS, SB_TILE), 1)
    return lane, hm, row, col


def _sb_stack(x, hm):
    return jnp.concatenate([jnp.where(m, x, 0) for m in hm], axis=0)


def _sb_unstack(x, hm):
    return sum(jnp.where(m, x[h * SB_TILE:(h + 1) * SB_TILE], 0.0) for h, m in enumerate(hm))


def _sb_fwd(proj, B, S, hosted=None):
    nt = S // SB_TILE
    T = B * S

    def body(q_ref, k_ref, v_ref, o_ref, r_ref):
        i = pl.program_id(2)
        _, hm, row, col = _sb_masks()
        tri = col < row
        later = (row[:SB_TILE] > col[:SB_TILE]).astype(BF16)
        qs = _sb_stack(q_ref[...] * SCALE64, hm)

        def tile(j, carry, mask):
            acc, c = carry
            rows = pl.ds(pl.multiple_of(j * SB_TILE, SB_TILE), SB_TILE)
            ls, l1m = _sb_logits(qs, k_ref[rows, :], mask)
            a = jnp.exp(ls + c + _dot_split(l1m, later))
            if mask is not None:
                a = jnp.where(mask, a, 0.0)
            pv = _dot(a.astype(BF16), v_ref[rows, :])
            return acc + _sb_unstack(pv, hm), c + jnp.sum(l1m, axis=1, keepdims=True)

        carry = tile(i, (jnp.zeros((SB_TILE, SB_LANES), F32), jnp.zeros((SB_ROWS, 1), F32)), tri)
        acc, c = lax.fori_loop(0, i, lambda it, cr: tile(i - 1 - it, cr, None), carry)
        o_ref[...] = acc.astype(o_ref.dtype)
        r_ref[...] = _sb_unstack(jnp.broadcast_to(c, (SB_ROWS, SB_LANES)), hm)

    blk = (SB_TILE, SB_LANES)
    ng = 512 // SB_LANES
    cq, ck, cv = (c * BLOCK // SB_LANES for c in (COL_QB, COL_KB, COL_VB))
    outs, moved = _hosted_call(
        body, (B, ng, nt),
        [pl.BlockSpec(blk, lambda b, p, i: (b * nt + i, cq + p)),
         pl.BlockSpec((S, SB_LANES), lambda b, p, i: (b, ck + p)),
         pl.BlockSpec((S, SB_LANES), lambda b, p, i: (b, cv + p))],
        [pl.BlockSpec(blk, lambda b, p, i: (b * nt + i, p))] * 2,
        [jax.ShapeDtypeStruct((T, 512), BF16), jax.ShapeDtypeStruct((T, 512), F32)], [], hosted, "sb_fwd",
        (proj, proj, proj))
    return outs[0], outs[1], moved


def _sb_bwd(proj, dy, rtot, B, S, hosted=None):
    nt = S // SB_TILE
    T = B * S

    def body(q_ref, k_ref, v_ref, do_ref, r_ref, dq_ref, dk_ref, dv_ref):
        i = pl.program_id(2)

        @pl.when(i == 0)
        def _():
            dk_ref[...] = jnp.zeros_like(dk_ref)
            dv_ref[...] = jnp.zeros_like(dv_ref)

        lane, hm, row, col = _sb_masks()
        tri = col < row
        later = (row[:SB_TILE] > col[:SB_TILE]).astype(BF16)
        earlier = (row[:SB_TILE] < col[:SB_TILE]).astype(BF16)
        qs = _sb_stack(q_ref[...] * SCALE64, hm)
        dos = _sb_stack(do_ref[...], hm)
        r = r_ref[...]
        rs = jnp.concatenate([jnp.sum(jnp.where(lane == h * HEAD_DIM, r, 0.0), axis=1, keepdims=True)
                              for h in range(SB_HEADS)], axis=0)

        def tile(j, carry, mask):
            dq, lsum, psum = carry
            rows = pl.ds(pl.multiple_of(j * SB_TILE, SB_TILE), SB_TILE)
            kj = k_ref[rows, :]
            vj = v_ref[rows, :]
            ls, l1m = _sb_logits(qs, kj, mask)
            sig = jnp.exp(ls)
            lsum = lsum + jnp.sum(l1m, axis=1, keepdims=True)
            a = jnp.exp(ls + (rs - lsum) + _dot_split(l1m, later))
            if mask is not None:
                a = jnp.where(mask, a, 0.0)
            de = _dot_nt(dos, vj) * a
            pre = psum + _dot(de.astype(BF16), earlier)
            dz = de * (1.0 - sig) - sig * pre
            if mask is not None:
                dz = jnp.where(mask, dz, 0.0)
            dz = dz.astype(BF16)
            dk_ref[rows, :] += _dot_tn(dz, qs)
            dv_ref[rows, :] += _dot_tn(a.astype(BF16), dos)
            return dq + _sb_unstack(_dot(dz, kj), hm), lsum, psum + jnp.sum(de, axis=1, keepdims=True)

        zero = jnp.zeros((SB_ROWS, 1), F32)
        carry = lax.fori_loop(0, i, lambda j, c: tile(j, c, None), (jnp.zeros((SB_TILE, SB_LANES), F32), zero, zero))
        dq = tile(i, carry, tri)[0]
        dq_ref[...] = (dq * SCALE64).astype(dq_ref.dtype)

    blk = (SB_TILE, SB_LANES)
    ng = 512 // SB_LANES
    cq, ck, cv = (c * BLOCK // SB_LANES for c in (COL_QB, COL_KB, COL_VB))
    q_spec = pl.BlockSpec(blk, lambda b, p, i: (b * nt + i, p))
    kv_out = pl.BlockSpec((S, SB_LANES), lambda b, p, i: (b, p))
    outs, moved = _hosted_call(
        body, (B, ng, nt),
        [pl.BlockSpec(blk, lambda b, p, i: (b * nt + i, cq + p)),
         pl.BlockSpec((S, SB_LANES), lambda b, p, i: (b, ck + p)),
         pl.BlockSpec((S, SB_LANES), lambda b, p, i: (b, cv + p)),
         q_spec, q_spec],
        [q_spec, kv_out, kv_out],
        [jax.ShapeDtypeStruct((T, 512), BF16),
         jax.ShapeDtypeStruct((T, 512), F32), jax.ShapeDtypeStruct((T, 512), F32)], [], hosted, "sb_bwd",
        (proj, proj, proj, dy, rtot))
    return outs[0], outs[1], outs[2], moved


def _mem_weights(q, mk):
    z = _dot_nt(q, mk) * SCALE128
    e = jnp.exp(z - jnp.max(z, axis=1, keepdims=True))
    return e / jnp.sum(e, axis=1, keepdims=True)


def _mem_fwd(proj, mkv, B, S, M):
    tq = 512
    nq = S // tq
    T = B * S
    Hm = MEM_HEADS

    def body(q0, q1, q2, q3, mk_ref, mv_ref, o_ref):
        outs = []
        for h, q_ref in enumerate((q0, q1, q2, q3)):
            cols = slice(h * 128, (h + 1) * 128)
            w = _mem_weights(q_ref[...], mk_ref[:, cols])
            outs.append(_dot(w.astype(BF16), mv_ref[:, cols]))
        o_ref[...] = jnp.concatenate(outs, axis=1).astype(o_ref.dtype)

    return pl.pallas_call(
        body,
        grid=(B, nq),
        in_specs=[pl.BlockSpec((tq, 128), lambda b, i, h=h: (b * nq + i, COL_QM + h)) for h in range(Hm)]
        + [pl.BlockSpec((M, 512), lambda b, i: (b, 0)), pl.BlockSpec((M, 512), lambda b, i: (b, 1))],
        out_specs=pl.BlockSpec((tq, 512), lambda b, i: (b * nq + i, 0)),
        out_shape=jax.ShapeDtypeStruct((T, 512), BF16),
        name="mem_fwd",
    )(proj, proj, proj, proj, mkv, mkv)


def _mem_bwd(proj, mkv, dy, B, S, M):
    tq = 512
    nq = S // tq
    T = B * S
    Hm = MEM_HEADS

    def body(q0, q1, q2, q3, mk_ref, mv_ref, do_ref, dq_ref, dmk_ref, dmv_ref):
        i = pl.program_id(1)
        dqs, dmks, dmvs = [], [], []
        for h, q_ref in enumerate((q0, q1, q2, q3)):
            cols = slice(h * 128, (h + 1) * 128)
            q = q_ref[...]
            do = do_ref[:, cols]
            mk = mk_ref[:, cols]
            w = _mem_weights(q, mk)
            dw = _dot_nt(do, mv_ref[:, cols])
            ds = (w * (dw - jnp.sum(w * dw, axis=1, keepdims=True))).astype(BF16)
            dqs.append(_dot(ds, mk))
            dmks.append(_dot_tn(ds, q))
            dmvs.append(_dot_tn(w.astype(BF16), do))
        dq_ref[...] = (jnp.concatenate(dqs, axis=1) * SCALE128).astype(dq_ref.dtype)

        @pl.when(i == 0)
        def _():
            dmk_ref[...] = jnp.zeros_like(dmk_ref)
            dmv_ref[...] = jnp.zeros_like(dmv_ref)

        dmk_ref[...] += jnp.concatenate(dmks, axis=1) * SCALE128
        dmv_ref[...] += jnp.concatenate(dmvs, axis=1)

    q_spec = pl.BlockSpec((tq, 512), lambda b, i: (b * nq + i, 0))
    m_out = pl.BlockSpec((M, 512), lambda b, i: (b, 0))
    return pl.pallas_call(
        body,
        grid=(B, nq),
        in_specs=[pl.BlockSpec((tq, 128), lambda b, i, h=h: (b * nq + i, COL_QM + h)) for h in range(Hm)]
        + [pl.BlockSpec((M, 512), lambda b, i: (b, 0)), pl.BlockSpec((M, 512), lambda b, i: (b, 1)), q_spec],
        out_specs=[q_spec, m_out, m_out],
        out_shape=[jax.ShapeDtypeStruct((T, 512), BF16),
                   jax.ShapeDtypeStruct((B * M, 512), F32), jax.ShapeDtypeStruct((B * M, 512), F32)],
        name="mem_bwd",
    )(proj, proj, proj, proj, mkv, mkv, dy)


def _all_gather(blk, name):
    R, C = blk.shape

    def body(x_ref, out_ref, send_sems, recv_sems, local_sem):
        x, y, c = _mesh_pos()
        me, sibling = (x, y, c), (x, y, 1 - c)
        chips = [(1 - x, y), (x, 1 - y), (1 - x, 1 - y)]

        def slot(px, py, pc):
            return out_ref.at[4 * px + 2 * py + pc]

        def copy(k, block, to, src=None):
            return pltpu.make_async_remote_copy(
                src_ref=slot(*block) if src is None else src, dst_ref=slot(*block),
                send_sem=send_sems.at[k], recv_sem=recv_sems.at[k],
                device_id=to, device_id_type=pl.DeviceIdType.MESH)

        mine = pltpu.make_async_copy(x_ref, slot(*me), local_sem)
        mine.start()
        first = [copy(0, me, sibling, src=x_ref)]
        first += [copy(1 + j, me, (*chip, c), src=x_ref) for j, chip in enumerate(chips)]
        for cp in first:
            cp.start()
        passed = [copy(4 + j, (*chip, c), sibling) for j, chip in enumerate(chips)]
        for j, chip in enumerate(chips):
            copy(1 + j, (*chip, c), me).wait_recv()
            passed[j].start()
        copy(0, sibling, me).wait_recv()
        for j, chip in enumerate(chips):
            copy(4 + j, (*chip, 1 - c), me).wait_recv()
        for cp in first + passed:
            cp.wait_send()
        mine.wait()

    return pl.pallas_call(
        body,
        in_specs=[pl.BlockSpec(memory_space=pl.ANY)],
        out_specs=pl.BlockSpec(memory_space=pl.ANY),
        out_shape=jax.ShapeDtypeStruct((N_DEV, R, C), blk.dtype),
        scratch_shapes=[pltpu.SemaphoreType.DMA((7,)), pltpu.SemaphoreType.DMA((7,)), pltpu.SemaphoreType.DMA],
        name=name,
    )(blk)


_HBM = pl.BlockSpec(memory_space=pltpu.HBM)
_SEM = pl.BlockSpec(memory_space=pltpu.SEMAPHORE)


def _scatter_start(parts, carried, name):
    hosted = _Hosted(scatters=[parts])

    def body(p_ref, land_ref, c_ref, send_sems, recv_sems, local_sems, p_thru, land_thru, c_thru):
        for cp in hosted.copies([p_ref], [land_ref], send_sems, recv_sems, local_sems):
            cp.start()

    sems = (pltpu.SemaphoreType.DMA((7,)), pltpu.SemaphoreType.DMA((7,)), pltpu.SemaphoreType.DMA((1,)))
    hbm = lambda a: pltpu.HBM(a.shape, a.dtype)
    outs = pl.pallas_call(
        body, name=name,
        out_shape=sems + (hbm(parts), hbm(parts), hbm(carried)),
        in_specs=(_HBM, _HBM, _HBM), out_specs=(_SEM, _SEM, _SEM, _HBM, _HBM, _HBM),
        input_output_aliases={0: 3, 1: 4, 2: 5},
        compiler_params=pltpu.CompilerParams(has_side_effects=pltpu.SideEffectType.DATAFLOW_SIDE_EFFECTING),
    )(pltpu.with_memory_space_constraint(parts, pltpu.HBM),
      pltpu.with_memory_space_constraint(lax.empty(parts.shape, parts.dtype), pltpu.HBM),
      pltpu.with_memory_space_constraint(carried, pltpu.HBM))
    return outs[:5], outs[5]


def _scatter_wait(flight, after, name):
    send_sems, recv_sems, local_sems, p_thru, land_thru = flight
    hosted = _Hosted(scatters=[p_thru])

    def body(p_ref, land_ref, send, recv, local, after_ref, p_dead, got_ref):
        for cp in hosted.copies([p_ref], [land_ref], send, recv, local):
            cp.wait()

    hbm = lambda a: pltpu.HBM(a.shape, a.dtype)
    return pl.pallas_call(
        body, name=name,
        out_shape=(hbm(p_thru), hbm(land_thru)),
        in_specs=(_HBM, _HBM, _SEM, _SEM, _SEM, pl.BlockSpec(memory_space=pl.ANY)), out_specs=(_HBM, _HBM),
        input_output_aliases={0: 0, 1: 1},
        compiler_params=pltpu.CompilerParams(has_side_effects=pltpu.SideEffectType.DATAFLOW_SIDE_EFFECTING),
    )(p_thru, land_thru, send_sems, recv_sems, local_sems, after)[1]


def _adamw(parts, w, m, v, name):
    R, C = w.shape
    tr = R
    for cand in (368, 352, 256, 176, 128, 64, 32, 16, 8):
        if R % cand == 0 and cand * C * 4 <= 1536 * 1024:
            tr = cand
            break
    c1 = 1.0 - ADAM_B1 ** ADAM_STEP
    c2 = 1.0 - ADAM_B2 ** ADAM_STEP

    def body(p_ref, w_ref, m_ref, v_ref, g_ref, d_ref, nm_ref, nv_ref):
        g = p_ref[0].astype(F32)
        for d in range(1, N_DEV):
            g = g + p_ref[d].astype(F32)
        nm = ADAM_B1 * m_ref[...] + (1.0 - ADAM_B1) * g
        nv = ADAM_B2 * v_ref[...] + (1.0 - ADAM_B2) * (g * g)
        g_ref[...] = g
        nm_ref[...] = nm
        nv_ref[...] = nv
        d_ref[...] = -ADAM_LR * ((nm / c1) / (jnp.sqrt(nv / c2) + ADAM_EPS) + ADAM_WD * w_ref[...])

    row = pl.BlockSpec((tr, C), lambda i: (i, 0))
    out = jax.ShapeDtypeStruct((R, C), F32)
    return pl.pallas_call(
        body,
        grid=(R // tr,),
        in_specs=[pl.BlockSpec((N_DEV, tr, C), lambda i: (0, i, 0)), row, row, row],
        out_specs=[row] * 4,
        out_shape=[out] * 4,
        compiler_params=_params(),
        name=name,
    )(parts, w, m, v)


def _col_shards(g):
    R, C8 = g.shape
    return g.reshape(R, N_DEV, C8 // N_DEV).transpose(1, 0, 2)


def _row_shards(g):
    R8, C = g.shape
    return g.reshape(N_DEV, R8 // N_DEV, C)


def _cols_full(gathered):
    n, R, C = gathered.shape
    return gathered.transpose(1, 0, 2).reshape(R, n * C)


_BIG = ("w_in", "w_mem_kv", "w_branch_swa", "w_branch_sb", "w_branch_mem", "w_out", "w_gate", "w_up", "w_down")
_COL_SHARDED = ("w_branch_swa", "w_branch_sb", "w_branch_mem")
_TRANSPOSED = ("w_in", "w_gate", "w_up")
_SMALL = ("ln_mix_pre", "ln_mix_post", "swa_sinks", "rel_bias", "ln_mem", "ln_ffn_pre", "ln_ffn_post")
_ORDER = ("ln_mix_pre", "ln_mix_post", "w_in", "swa_sinks", "rel_bias", "ln_mem", "w_mem_kv", "w_branch_swa",
          "w_branch_sb", "w_branch_mem", "w_out", "ln_ffn_pre", "ln_ffn_post", "w_gate", "w_up", "w_down")


def _pack_small(d, last=None):
    rows = [d["ln_mix_pre"], d["ln_mix_post"], d["ln_mem"], d["ln_ffn_pre"], d["ln_ffn_post"],
            jnp.pad(d["swa_sinks"].reshape(1, -1), ((0, 0), (0, D_MODEL - SWA_Q_HEADS))),
            jnp.pad(d["rel_bias"].reshape(1, -1), ((0, 0), (0, D_MODEL - N_BUCKETS * SWA_Q_HEADS))),
            jnp.zeros((1, D_MODEL), F32) if last is None else last]
    return jnp.concatenate([r.astype(F32) for r in rows], axis=0)


def _unpack_small(a):
    return dict(ln_mix_pre=a[0:1], ln_mix_post=a[1:2], ln_mem=a[2:3], ln_ffn_pre=a[3:4], ln_ffn_post=a[4:5],
                swa_sinks=a[5:6, :SWA_Q_HEADS],
                rel_bias=a[6, :N_BUCKETS * SWA_Q_HEADS].reshape(N_BUCKETS, SWA_Q_HEADS))


def kernel(x, mem, ln_mix_pre, ln_mix_post, w_in, swa_sinks, rel_bias, ln_mem, w_mem_kv, w_branch_swa, w_branch_sb, w_branch_mem, w_out, ln_ffn_pre, ln_ffn_post, w_gate, w_up, w_down, loss_target, m_ln_mix_pre, m_ln_mix_post, m_w_in, m_swa_sinks, m_rel_bias, m_ln_mem, m_w_mem_kv, m_w_branch_swa, m_w_branch_sb, m_w_branch_mem, m_w_out, m_ln_ffn_pre, m_ln_ffn_post, m_w_gate, m_w_up, m_w_down, v_ln_mix_pre, v_ln_mix_post, v_w_in, v_swa_sinks, v_rel_bias, v_ln_mem, v_w_mem_kv, v_w_branch_swa, v_w_branch_sb, v_w_branch_mem, v_w_out, v_ln_ffn_pre, v_ln_ffn_post, v_w_gate, v_w_up, v_w_down):
    w = dict(ln_mix_pre=ln_mix_pre, ln_mix_post=ln_mix_post, w_in=w_in[0], swa_sinks=swa_sinks, rel_bias=rel_bias,
             ln_mem=ln_mem, w_mem_kv=w_mem_kv[0], w_branch_swa=w_branch_swa[0], w_branch_sb=w_branch_sb[0],
             w_branch_mem=w_branch_mem[0], w_out=w_out[0], ln_ffn_pre=ln_ffn_pre, ln_ffn_post=ln_ffn_post,
             w_gate=w_gate[0], w_up=w_up[0], w_down=w_down[0])
    mom = dict(ln_mix_pre=m_ln_mix_pre, ln_mix_post=m_ln_mix_post, w_in=m_w_in[0], swa_sinks=m_swa_sinks,
               rel_bias=m_rel_bias, ln_mem=m_ln_mem, w_mem_kv=m_w_mem_kv[0], w_branch_swa=m_w_branch_swa[0],
               w_branch_sb=m_w_branch_sb[0], w_branch_mem=m_w_branch_mem[0], w_out=m_w_out[0],
               ln_ffn_pre=m_ln_ffn_pre, ln_ffn_post=m_ln_ffn_post, w_gate=m_w_gate[0], w_up=m_w_up[0],
               w_down=m_w_down[0])
    var = dict(ln_mix_pre=v_ln_mix_pre, ln_mix_post=v_ln_mix_post, w_in=v_w_in[0], swa_sinks=v_swa_sinks,
               rel_bias=v_rel_bias, ln_mem=v_ln_mem, w_mem_kv=v_w_mem_kv[0], w_branch_swa=v_w_branch_swa[0],
               w_branch_sb=v_w_branch_sb[0], w_branch_mem=v_w_branch_mem[0], w_out=v_w_out[0],
               ln_ffn_pre=v_ln_ffn_pre, ln_ffn_post=v_ln_ffn_post, w_gate=v_w_gate[0], w_up=v_w_up[0],
               w_down=v_w_down[0])
    B, S, D = x.shape
    M = mem.shape[1]
    T = B * S
    F = D_FF
    x2 = x.reshape(T, D)
    mem2 = mem.reshape(B * M, D)
    t2 = loss_target.reshape(T, D)
    buckets = jnp.asarray(_swa_buckets())
    for d in (w, mom, var):
        for n in _TRANSPOSED:
            d[n] = d[n].T
    wb = {n: w[n].astype(BF16) for n in _BIG}
    full = {}

    def landed(names, got):
        for n, g in zip(names, got):
            full[n] = _cols_full(g) if n in _COL_SHARDED else g.reshape(-1, g.shape[-1])

    def shards(n, g):
        return _col_shards(g) if n in _COL_SHARDED else _row_shards(g)

    landed(["w_in"], [_all_gather(wb["w_in"], "ag_w_in")])
    u = _rms_fwd(x2, ln_mix_pre, "rms_mix_pre")
    early = ["w_mem_kv", "w_branch_swa", "w_branch_sb", "w_branch_mem"]
    proj, got = _matmul([(u, full["w_in"])], "nt", BF16, 512, IN_W // 2, D, "proj_in",
                        hosted=_Hosted(gathers=[wb[n] for n in early]))
    landed(early, got)
    mn = _rms_fwd(mem2, ln_mem, "rms_mem")
    mkv = _matmul([(mn, full["w_mem_kv"])], "nn", BF16, 512, 1024, D, "proj_mem")
    bias_tab = _swa_bias_table(rel_bias, buckets)
    y_swa, got = _swa_fwd(proj, swa_sinks, bias_tab, B, S, hosted=_Hosted(gathers=[wb["w_out"]]))
    landed(["w_out"], got)
    late = ["w_gate", "w_up", "w_down"]
    y_sb, rtot, got = _sb_fwd(proj, B, S, hosted=_Hosted(gathers=[wb[n] for n in late]))
    landed(late, got)
    y_mem = _mem_fwd(proj, mkv, B, S, M)
    wbs = (full["w_branch_swa"], full["w_branch_sb"], full["w_branch_mem"])
    merged, p_swa, p_sb, p_mem = _branch_gate(proj, (y_swa, y_sb, y_mem), wbs)
    mix = _matmul([(merged, full["w_out"])], "nn", F32, 512, 1024, D, "proj_out")
    h1, u2 = _post_pre(x2, mix, ln_mix_post, ln_ffn_pre)
    a, zg, zu = _ffn_up(u2, full["w_gate"], full["w_up"])
    ffn = _matmul([(a, full["w_down"])], "nn", F32, 512, 1024, F, "ffn_down")
    dffn, dh2, loss_tile, d_ln_ffn_post = _loss_head(ffn, h1, t2, ln_ffn_post)

    part = {}
    part["w_down"] = _matmul([(a, dffn)], "tn", BF16, F // 2, 1024, 1024, "dw_down")
    dzg, dzu = _ffn_down_bwd(dffn, full["w_down"], zg, zu)
    part["w_gate"] = _matmul([(dzg, u2)], "tn", BF16, F // 2, 1024, 1024, "dw_gate")
    part["w_up"] = _matmul([(dzu, u2)], "tn", BF16, F // 2, 1024, 1024, "dw_up")
    du2 = _matmul([(dzg, full["w_gate"]), (dzu, full["w_up"])], "nn", F32, 512, 1024, F // 2, "d_u2")
    dh1, dmix, d_ln_ffn_pre, d_ln_mix_post = _mid_bwd(h1, du2, dh2, mix, ln_ffn_pre, ln_mix_post)
    part["w_out"] = _matmul([(merged, dmix)], "tn", BF16, 1024, 1024, 1024, "dw_out")
    dmerged = _matmul([(dmix, full["w_out"])], "nt", F32, 512, 1024, D, "d_merged")
    (dp_swa, dp_sb, dp_mem, dg0, dg1, dg2, dy_swa, dy_sb, dy_mem) = _gate_bwd(
        dmerged, (p_swa, p_sb, p_mem), proj, wbs)
    part["w_branch_swa"] = _matmul([(y_swa, dp_swa)], "tn", BF16, 512, 1024, 1024, "dw_branch_swa")
    part["w_branch_sb"] = _matmul([(y_sb, dp_sb)], "tn", BF16, 512, 1024, 1024, "dw_branch_sb")
    part["w_branch_mem"] = _matmul([(y_mem, dp_mem)], "tn", BF16, 512, 1024, 1024, "dw_branch_mem")
    dqm, dmk, dmv = _mem_bwd(proj, mkv, dy_mem, B, S, M)
    dmkv = jnp.concatenate([dmk, dmv], axis=1).astype(BF16)
    part["w_mem_kv"] = _matmul([(mn, dmkv)], "tn", BF16, 1024, 1024, 512, "dw_mem_kv")
    dmn = _matmul([(dmkv, full["w_mem_kv"])], "nt", F32, 512, 1024, 1024, "d_mn")
    d_ln_mem = _gain_grad(mem2, dmn)
    dqa, dka, dva, dbias, dsink = _swa_bwd(proj, dy_swa, swa_sinks, bias_tab, B, S)
    behind_sb = ["w_down", "w_gate", "w_up", "w_out", "w_branch_swa", "w_branch_sb", "w_branch_mem", "w_mem_kv"]
    dqb, dkb, dvb, got = _sb_bwd(proj, dy_sb, rtot, B, S,
                                 hosted=_Hosted(scatters=[shards(n, part[n]) for n in behind_sb]))
    recv = dict(zip(behind_sb, got))
    d_rel_bias = _swa_bias_grad(dbias, buckets)[:, :SWA_Q_HEADS]
    d_sinks = dsink[:, 0].reshape(1, SWA_Q_HEADS)
    dproj = jnp.concatenate([dqa, dka.astype(BF16), dva.astype(BF16), dqb, dkb.astype(BF16), dvb.astype(BF16),
                             dqm, dg0, dg1, dg2], axis=1)
    part["w_in"] = _matmul([(dproj, u)], "tn", BF16, IN_W // 2, 512, 1024, "dw_in")
    flight, dproj = _scatter_start(_row_shards(part["w_in"]), dproj, "rs_w_in_start")
    du = _matmul([(dproj, full["w_in"])], "nn", F32, 512, 1024, IN_W // 2, "d_u")
    grad_x, d_ln_mix_pre = _pre_bwd(x2, du, dh1, ln_mix_pre)

    out = {n: _adamw(recv[n], w[n], mom[n], var[n], "adamw_" + n) for n in _BIG if n != "w_in"}
    small_grads = dict(ln_mix_pre=d_ln_mix_pre, ln_mix_post=d_ln_mix_post, swa_sinks=d_sinks, rel_bias=d_rel_bias,
                       ln_mem=d_ln_mem, ln_ffn_pre=d_ln_ffn_pre, ln_ffn_post=d_ln_ffn_post)
    small_parts = _all_gather(_pack_small(small_grads, jnp.tile(loss_tile[0:1], (1, D // 128))), "ag_small")
    res = _adamw(small_parts, _pack_small(w), _pack_small(mom), _pack_small(var), "adamw_small")
    loss = res[0][7, 0]
    small = [_unpack_small(r) for r in res]
    for n in _SMALL:
        out[n] = tuple(s[n] for s in small)
    recv["w_in"] = _scatter_wait(flight, res[0], "rs_w_in_wait")
    out["w_in"] = _adamw(recv["w_in"], w["w_in"], mom["w_in"], var["w_in"], "adamw_w_in")
    for n in _TRANSPOSED:
        out[n] = tuple(o.T for o in out[n])

    like = dict(ln_mix_pre=ln_mix_pre, ln_mix_post=ln_mix_post, w_in=w_in, swa_sinks=swa_sinks, rel_bias=rel_bias,
                ln_mem=ln_mem, w_mem_kv=w_mem_kv, w_branch_swa=w_branch_swa, w_branch_sb=w_branch_sb,
                w_branch_mem=w_branch_mem, w_out=w_out, ln_ffn_pre=ln_ffn_pre, ln_ffn_post=ln_ffn_post,
                w_gate=w_gate, w_up=w_up, w_down=w_down)
    result = [loss, grad_x.reshape(B, S, D)]
    for k in range(4):
        result += [out[n][k].reshape(like[n].shape) for n in _ORDER]
    return tuple(result)
```

```python
import functools
import math

import numpy as np
import jax
import jax.numpy as jnp
from jax import lax
from jax.experimental import pallas as pl
from jax.experimental.pallas import tpu as pltpu

F32 = jnp.float32
BF16 = jnp.bfloat16

N_DEV = 8
D_MODEL = 1024
BLOCK = 128
EPS = 1e-6
HEAD_DIM = 64
SWA_Q_HEADS = 8
SWA_WINDOW = 128
N_BUCKETS = 32
MAX_DISTANCE = 128
MEM_HEADS = 4
MEM_HEAD_DIM = 128
D_FF = 2816
IN_W = 5888
COL_QA, COL_KA, COL_VA, COL_QB, COL_KB, COL_VB, COL_QM, COL_GL = 0, 4, 5, 6, 10, 14, 18, 22
SCALE64 = HEAD_DIM ** -0.5
SCALE128 = MEM_HEAD_DIM ** -0.5
NEG = -1e30

ADAM_LR = 0.001
ADAM_B1 = 0.9
ADAM_B2 = 0.999
ADAM_EPS = 1e-08
ADAM_WD = 0.01
ADAM_STEP = 10

VMEM_LIMIT_BYTES = 56 * 1024 * 1024


def _params(**kw):
    return pltpu.CompilerParams(vmem_limit_bytes=VMEM_LIMIT_BYTES, **kw)


def _dot(a, b):
    return jnp.dot(a, b, preferred_element_type=F32)


def _dot_nt(a, b):
    return lax.dot_general(a, b, (((1,), (1,)), ((), ())), preferred_element_type=F32)


def _dot_tn(a, b):
    return lax.dot_general(a, b, (((0,), (0,)), ((), ())), preferred_element_type=F32)


def _dot_split(x, m2):
    hi = x.astype(BF16)
    lo = (x - hi.astype(F32)).astype(BF16)
    return _dot(jnp.concatenate([hi, lo], axis=1), m2)


def _mesh_pos():
    return lax.axis_index("x"), lax.axis_index("y"), lax.axis_index("c")


class _Hosted:
    def __init__(self, gathers=(), scatters=()):
        self.items = [("g", a) for a in gathers] + [("s", a) for a in scatters]
        self.n = len(self.items)

    def operands(self):
        return [a for _, a in self.items]

    def specs(self):
        return [pl.BlockSpec(memory_space=pl.ANY)] * self.n

    def out_shapes(self):
        return [jax.ShapeDtypeStruct(((N_DEV,) + a.shape) if kind == "g" else a.shape, a.dtype)
                for kind, a in self.items]

    def scratch(self):
        return [pltpu.SemaphoreType.DMA((7 * self.n,)), pltpu.SemaphoreType.DMA((7 * self.n,)),
                pltpu.SemaphoreType.DMA((self.n,))]

    def copies(self, in_refs, out_refs, send_sems, recv_sems, local_sems):
        x, y, c = _mesh_pos()
        me = 4 * x + 2 * y + c
        out = []
        for t, (kind, _) in enumerate(self.items):
            own = in_refs[t] if kind == "g" else in_refs[t].at[me]
            out.append(pltpu.make_async_copy(own, out_refs[t].at[me], local_sems.at[t]))
            for k in range(1, N_DEV):
                px, py, pc = x ^ (k >> 2), y ^ ((k >> 1) & 1), c ^ (k & 1)
                src = in_refs[t] if kind == "g" else in_refs[t].at[4 * px + 2 * py + pc]
                out.append(pltpu.make_async_remote_copy(
                    src_ref=src, dst_ref=out_refs[t].at[me],
                    send_sem=send_sems.at[7 * t + k - 1], recv_sem=recv_sems.at[7 * t + k - 1],
                    device_id=(px, py, pc), device_id_type=pl.DeviceIdType.MESH))
        return out


def _host(body, n_in, n_out, hosted, grid):
    if hosted is None:
        return body
    nc = hosted.n

    def wrapped(*refs):
        ins = refs[:n_in]
        cin = refs[n_in:n_in + nc]
        outs = refs[n_in + nc:n_in + nc + n_out]
        cout = refs[n_in + nc + n_out:n_in + 2 * nc + n_out]
        scratch = refs[n_in + 2 * nc + n_out:len(refs) - 3]
        sems = refs[len(refs) - 3:]
        ids = [pl.program_id(d) for d in range(len(grid))]
        first = functools.reduce(lambda a, b: a & b, [i == 0 for i in ids])
        last = functools.reduce(lambda a, b: a & b, [i == g - 1 for i, g in zip(ids, grid)])

        @pl.when(first)
        def _():
            for cp in hosted.copies(cin, cout, *sems):
                cp.start()

        body(*ins, *outs, *scratch)

        @pl.when(last)
        def _():
            for cp in hosted.copies(cin, cout, *sems):
                cp.wait()

    return wrapped


def _hosted_call(body, grid, in_specs, out_specs, out_shape, scratch_shapes, hosted, name, args):
    n_out = len(out_specs)
    if hosted is None:
        outs = pl.pallas_call(body, grid=grid, in_specs=in_specs, out_specs=out_specs, out_shape=out_shape,
                              scratch_shapes=scratch_shapes, compiler_params=_params(), name=name)(*args)
        return list(outs), []
    outs = pl.pallas_call(
        _host(body, len(in_specs), n_out, hosted, grid),
        grid=grid,
        in_specs=list(in_specs) + hosted.specs(),
        out_specs=list(out_specs) + hosted.specs(),
        out_shape=list(out_shape) + hosted.out_shapes(),
        scratch_shapes=list(scratch_shapes) + hosted.scratch(),
        compiler_params=_params(),
        name=name,
    )(*args, *hosted.operands())
    return list(outs[:n_out]), list(outs[n_out:])


_DIMS = {"nn": (((1,), (0,)), ((), ())), "nt": (((1,), (1,)), ((), ())), "tn": (((0,), (0,)), ((), ()))}


def _matmul(pairs, mode, out_dtype, tm, tn, tk, name, hosted=None):
    a0, b0 = pairs[0]
    if mode == "nn":
        (M, K), N = a0.shape, b0.shape[1]
    elif mode == "nt":
        (M, K), N = a0.shape, b0.shape[0]
    else:
        (K, M), N = a0.shape, b0.shape[1]
    tm, tn, tk = min(tm, M), min(tn, N), min(tk, K)
    assert M % tm == 0 and N % tn == 0 and K % tk == 0, (name, M, N, K, tm, tn, tk)
    nm, nn, nk = M // tm, N // tn, K // tk
    npair = len(pairs)
    dims = _DIMS[mode]

    def body(*refs):
        ab = refs[:2 * npair]
        o_ref = refs[2 * npair]
        acc_ref = refs[2 * npair + 1]
        k = pl.program_id(2)
        part = lax.dot_general(ab[0][...], ab[1][...], dims, preferred_element_type=F32)
        for q in range(1, npair):
            part += lax.dot_general(ab[2 * q][...], ab[2 * q + 1][...], dims, preferred_element_type=F32)
        if nk == 1:
            o_ref[...] = part.astype(o_ref.dtype)
        else:
            @pl.when(k == 0)
            def _():
                acc_ref[...] = part

            @pl.when(k > 0)
            def _():
                acc_ref[...] += part

            @pl.when(k == nk - 1)
            def _():
                o_ref[...] = acc_ref[...].astype(o_ref.dtype)

    if mode == "nn":
        a_spec = pl.BlockSpec((tm, tk), lambda n, m, k: (m, k))
        b_spec = pl.BlockSpec((tk, tn), lambda n, m, k: (k, n))
    elif mode == "nt":
        a_spec = pl.BlockSpec((tm, tk), lambda n, m, k: (m, k))
        b_spec = pl.BlockSpec((tn, tk), lambda n, m, k: (n, k))
    else:
        a_spec = pl.BlockSpec((tk, tm), lambda n, m, k: (k, m))
        b_spec = pl.BlockSpec((tk, tn), lambda n, m, k: (k, n))
    args = [t for pr in pairs for t in pr]
    outs, moved = _hosted_call(
        body, (nn, nm, nk), [a_spec, b_spec] * npair, [pl.BlockSpec((tm, tn), lambda n, m, k: (m, n))],
        [jax.ShapeDtypeStruct((M, N), out_dtype)], [pltpu.VMEM((tm, tn) if nk > 1 else (8, 128), F32)],
        hosted, name, args)
    return outs[0] if hosted is None else (outs[0], moved)


def _rms_fwd(x, g, name):
    T, D = x.shape
    tr = min(512, T)

    def body(x_ref, g_ref, u_ref):
        xf = x_ref[...]
        r = lax.rsqrt(jnp.mean(xf * xf, axis=-1, keepdims=True) + EPS)
        u_ref[...] = ((xf * r) * g_ref[...]).astype(u_ref.dtype)

    return pl.pallas_call(
        body,
        grid=(T // tr,),
        in_specs=[pl.BlockSpec((tr, D), lambda i: (i, 0)), pl.BlockSpec((1, D), lambda i: (0, 0))],
        out_specs=pl.BlockSpec((tr, D), lambda i: (i, 0)),
        out_shape=jax.ShapeDtypeStruct((T, D), BF16),
        name=name,
    )(x, g)


def _rms_bwd_terms(xin, g, dy):
    r = lax.rsqrt(jnp.mean(xin * xin, axis=-1, keepdims=True) + EPS)
    xh = xin * r
    dg = jnp.sum(dy * xh, axis=0, keepdims=True)
    dxh = dy * g
    dx = r * (dxh - xh * jnp.mean(dxh * xh, axis=-1, keepdims=True))
    return dx, dg


GATE_TC = 256


def _branch_gate(proj, ys, wbs):
    T = proj.shape[0]
    D = D_MODEL
    tr, tc = min(1024, T), GATE_TC
    nc = D // tc
    gl0 = COL_GL * 128 // tc

    def body(ya, yb, yc, wa, wb, wc, g0, g1, g2, merged_ref, pa, pb, pc):
        acc = jnp.zeros((tr, tc), F32)
        for y_ref, w_ref, g_ref, p_ref in ((ya, wa, g0, pa), (yb, wb, g1, pb), (yc, wc, g2, pc)):
            p = _dot(y_ref[...], w_ref[...])
            p_ref[...] = p.astype(p_ref.dtype)
            acc += jax.nn.sigmoid(g_ref[...].astype(F32)) * p
        merged_ref[...] = acc.astype(merged_ref.dtype)

    y_spec = pl.BlockSpec((tr, 512), lambda i, n: (i, 0))
    w_spec = pl.BlockSpec((512, tc), lambda i, n: (0, n))
    o_spec = pl.BlockSpec((tr, tc), lambda i, n: (i, n))
    gl_specs = [pl.BlockSpec((tr, tc), lambda i, n, j=j: (i, gl0 + j * nc + n)) for j in range(3)]
    out = jax.ShapeDtypeStruct((T, D), BF16)
    return pl.pallas_call(
        body,
        grid=(T // tr, nc),
        in_specs=[y_spec] * 3 + [w_spec] * 3 + gl_specs,
        out_specs=[o_spec] * 4,
        out_shape=[out] * 4,
        compiler_params=_params(),
        name="branch_gate",
    )(*ys, *wbs, proj, proj, proj)


def _post_pre(x, mix, g_post, g_pre):
    T, D = x.shape
    tr = 512

    def body(x_ref, mix_ref, gp_ref, gq_ref, h1_ref, u2_ref):
        mx = mix_ref[...]
        r = lax.rsqrt(jnp.mean(mx * mx, axis=-1, keepdims=True) + EPS)
        h1 = x_ref[...] + (mx * r) * gp_ref[...]
        h1_ref[...] = h1
        r2 = lax.rsqrt(jnp.mean(h1 * h1, axis=-1, keepdims=True) + EPS)
        u2_ref[...] = ((h1 * r2) * gq_ref[...]).astype(u2_ref.dtype)

    row = pl.BlockSpec((tr, D), lambda i: (i, 0))
    vec = pl.BlockSpec((1, D), lambda i: (0, 0))
    return pl.pallas_call(
        body,
        grid=(T // tr,),
        in_specs=[row, row, vec, vec],
        out_specs=[row, row],
        out_shape=[jax.ShapeDtypeStruct((T, D), F32), jax.ShapeDtypeStruct((T, D), BF16)],
        compiler_params=_params(),
        name="post_pre",
    )(x, mix, g_post, g_pre)


def _ffn_up(u2, w_gate_t, w_up_t):
    T, D = u2.shape
    F = D_FF
    tm, tn = 512, F // 2

    def body(u_ref, wg_ref, wu_ref, a_ref, zg_ref, zu_ref):
        u = u_ref[...]
        zg = _dot_nt(u, wg_ref[...])
        zu = _dot_nt(u, wu_ref[...])
        a_ref[...] = (zg * jax.nn.sigmoid(zg) * zu).astype(a_ref.dtype)
        zg_ref[...] = zg.astype(zg_ref.dtype)
        zu_ref[...] = zu.astype(zu_ref.dtype)

    o_spec = pl.BlockSpec((tm, tn), lambda n, m: (m, n))
    out = jax.ShapeDtypeStruct((T, F), BF16)
    return pl.pallas_call(
        body,
        grid=(F // tn, T // tm),
        in_specs=[pl.BlockSpec((tm, D), lambda n, m: (m, 0)),
                  pl.BlockSpec((tn, D), lambda n, m: (n, 0)),
                  pl.BlockSpec((tn, D), lambda n, m: (n, 0))],
        out_specs=[o_spec] * 3,
        out_shape=[out] * 3,
        compiler_params=_params(),
        name="ffn_up",
    )(u2, w_gate_t, w_up_t)


def _loss_head(ffn, h1, target, g_post):
    T, D = ffn.shape
    tr = 512

    def body(f_ref, h1_ref, t_ref, g_ref, dffn_ref, dh2_ref, loss_ref, dg_ref):
        i = pl.program_id(0)
        f = f_ref[...]
        g = g_ref[...]
        r = lax.rsqrt(jnp.mean(f * f, axis=-1, keepdims=True) + EPS)
        xh = f * r
        err = (h1_ref[...] + xh * g) - t_ref[...]
        part = 0.5 * jnp.sum(jnp.mean(err * err, axis=-1, keepdims=True), axis=0, keepdims=True)
        dh2 = err * (1.0 / D)
        dh2_ref[...] = dh2
        dgp = jnp.sum(dh2 * xh, axis=0, keepdims=True)
        dxh = dh2 * g
        dffn_ref[...] = (r * (dxh - xh * jnp.mean(dxh * xh, axis=-1, keepdims=True))).astype(dffn_ref.dtype)

        @pl.when(i == 0)
        def _():
            loss_ref[...] = jnp.zeros_like(loss_ref)
            dg_ref[...] = jnp.zeros_like(dg_ref)

        loss_ref[...] += jnp.broadcast_to(part, loss_ref.shape)
        dg_ref[...] += dgp

    row = pl.BlockSpec((tr, D), lambda i: (i, 0))
    vec = pl.BlockSpec((1, D), lambda i: (0, 0))
    return pl.pallas_call(
        body,
        grid=(T // tr,),
        in_specs=[row, row, row, vec],
        out_specs=[row, row, pl.BlockSpec((8, 128), lambda i: (0, 0)), vec],
        out_shape=[jax.ShapeDtypeStruct((T, D), BF16), jax.ShapeDtypeStruct((T, D), F32),
                   jax.ShapeDtypeStruct((8, 128), F32), jax.ShapeDtypeStruct((1, D), F32)],
        compiler_params=_params(),
        name="loss_head",
    )(ffn, h1, target, g_post)


def _ffn_down_bwd(dffn, wd, zg, zu):
    T, D = dffn.shape
    F = D_FF
    tm, tn = 512, F // 2

    def body(d_ref, w_ref, zg_ref, zu_ref, dzg_ref, dzu_ref):
        d = d_ref[...]
        for lo in range(0, tn, 512):
            cols = slice(lo, min(lo + 512, tn))
            da = _dot_nt(d, w_ref[cols, :])
            zg = zg_ref[:, cols].astype(F32)
            zu = zu_ref[:, cols].astype(F32)
            s = jax.nn.sigmoid(zg)
            dzu_ref[:, cols] = (da * (zg * s)).astype(dzu_ref.dtype)
            dzg_ref[:, cols] = (da * zu * (s * (1.0 + zg * (1.0 - s)))).astype(dzg_ref.dtype)

    z_spec = pl.BlockSpec((tm, tn), lambda n, m: (m, n))
    out = jax.ShapeDtypeStruct((T, F), BF16)
    return pl.pallas_call(
        body,
        grid=(F // tn, T // tm),
        in_specs=[pl.BlockSpec((tm, D), lambda n, m: (m, 0)), pl.BlockSpec((tn, D), lambda n, m: (n, 0)),
                  z_spec, z_spec],
        out_specs=[z_spec, z_spec],
        out_shape=[out, out],
        compiler_params=_params(),
        name="ffn_down_bwd",
    )(dffn, wd, zg, zu)


def _mid_bwd(h1, du2, dh2, mix, g_pre, g_post):
    T, D = h1.shape
    tr = 512

    def body(h1_ref, du2_ref, dh2_ref, mix_ref, gq_ref, gp_ref, dh1_ref, dmix_ref, dgq_ref, dgp_ref):
        i = pl.program_id(0)
        dx, dgq = _rms_bwd_terms(h1_ref[...], gq_ref[...], du2_ref[...])
        dh1 = dh2_ref[...] + dx
        dh1_ref[...] = dh1
        dmix, dgp = _rms_bwd_terms(mix_ref[...], gp_ref[...], dh1)
        dmix_ref[...] = dmix.astype(dmix_ref.dtype)

        @pl.when(i == 0)
        def _():
            dgq_ref[...] = jnp.zeros_like(dgq_ref)
            dgp_ref[...] = jnp.zeros_like(dgp_ref)

        dgq_ref[...] += dgq
        dgp_ref[...] += dgp

    row = pl.BlockSpec((tr, D), lambda i: (i, 0))
    vec = pl.BlockSpec((1, D), lambda i: (0, 0))
    return pl.pallas_call(
        body,
        grid=(T // tr,),
        in_specs=[row, row, row, row, vec, vec],
        out_specs=[row, row, vec, vec],
        out_shape=[jax.ShapeDtypeStruct((T, D), F32), jax.ShapeDtypeStruct((T, D), BF16),
                   jax.ShapeDtypeStruct((1, D), F32), jax.ShapeDtypeStruct((1, D), F32)],
        compiler_params=_params(),
        name="mid_bwd",
    )(h1, du2, dh2, mix, g_pre, g_post)


def _gate_bwd(dmerged, ps, proj, wbs):
    T, D = dmerged.shape
    tr, tc = min(1024, T), GATE_TC
    nc = D // tc
    gl0 = COL_GL * 128 // tc

    def body(dm_ref, pa, pb, pc, g0, g1, g2, wa, wb, wc, dpa, dpb, dpc, dga, dgb, dgc, dya, dyb, dyc,
             acc_a, acc_b, acc_c):
        n = pl.program_id(1)
        dm = dm_ref[...]
        for p_ref, g_ref, w_ref, dp_ref, dg_ref, dy_ref, acc_ref in (
                (pa, g0, wa, dpa, dga, dya, acc_a), (pb, g1, wb, dpb, dgb, dyb, acc_b),
                (pc, g2, wc, dpc, dgc, dyc, acc_c)):
            s = jax.nn.sigmoid(g_ref[...].astype(F32))
            dp = (dm * s).astype(BF16)
            dp_ref[...] = dp
            dg_ref[...] = (dm * p_ref[...].astype(F32) * (s * (1.0 - s))).astype(dg_ref.dtype)
            part = _dot_nt(dp, w_ref[...])

            @pl.when(n == 0)
            def _():
                acc_ref[...] = part

            @pl.when(n > 0)
            def _():
                acc_ref[...] += part

            @pl.when(n == nc - 1)
            def _():
                dy_ref[...] = acc_ref[...].astype(dy_ref.dtype)

    col = pl.BlockSpec((tr, tc), lambda i, n: (i, n))
    y_spec = pl.BlockSpec((tr, 512), lambda i, n: (i, 0))
    w_spec = pl.BlockSpec((512, tc), lambda i, n: (0, n))
    gl_specs = [pl.BlockSpec((tr, tc), lambda i, n, j=j: (i, gl0 + j * nc + n)) for j in range(3)]
    big = jax.ShapeDtypeStruct((T, D), BF16)
    small = jax.ShapeDtypeStruct((T, 512), BF16)
    return pl.pallas_call(
        body,
        grid=(T // tr, nc),
        in_specs=[col] * 4 + gl_specs + [w_spec] * 3,
        out_specs=[col] * 6 + [y_spec] * 3,
        out_shape=[big] * 6 + [small] * 3,
        scratch_shapes=[pltpu.VMEM((tr, 512), F32)] * 3,
        compiler_params=_params(),
        name="gate_bwd",
    )(dmerged, *ps, proj, proj, proj, *wbs)


def _pre_bwd(x, du, dh1, g):
    T, D = x.shape
    tr = 512

    def body(x_ref, du_ref, dh1_ref, g_ref, gx_ref, dg_ref):
        i = pl.program_id(0)
        dx, dg = _rms_bwd_terms(x_ref[...], g_ref[...], du_ref[...])
        gx_ref[...] = dh1_ref[...] + dx

        @pl.when(i == 0)
        def _():
            dg_ref[...] = jnp.zeros_like(dg_ref)

        dg_ref[...] += dg

    row = pl.BlockSpec((tr, D), lambda i: (i, 0))
    vec = pl.BlockSpec((1, D), lambda i: (0, 0))
    return pl.pallas_call(
        body,
        grid=(T // tr,),
        in_specs=[row, row, row, vec],
        out_specs=[row, vec],
        out_shape=[jax.ShapeDtypeStruct((T, D), F32), jax.ShapeDtypeStruct((1, D), F32)],
        compiler_params=_params(),
        name="pre_bwd",
    )(x, du, dh1, g)


def _gain_grad(xin, dy):
    T, D = xin.shape
    tr = min(512, T)

    def body(x_ref, dy_ref, dg_ref):
        i = pl.program_id(0)
        xf = x_ref[...]
        r = lax.rsqrt(jnp.mean(xf * xf, axis=-1, keepdims=True) + EPS)

        @pl.when(i == 0)
        def _():
            dg_ref[...] = jnp.zeros_like(dg_ref)

        dg_ref[...] += jnp.sum(dy_ref[...] * (xf * r), axis=0, keepdims=True)

    row = pl.BlockSpec((tr, D), lambda i: (i, 0))
    return pl.pallas_call(
        body,
        grid=(T // tr,),
        in_specs=[row, row],
        out_specs=pl.BlockSpec((1, D), lambda i: (0, 0)),
        out_shape=jax.ShapeDtypeStruct((1, D), F32),
        name="gain_grad",
    )(xin, dy)


def _swa_buckets():
    dist = (np.arange(BLOCK)[:, None] + BLOCK) - np.arange(2 * BLOCK)[None, :]
    max_exact = N_BUCKETS // 2
    d = np.maximum(dist, 0)
    df = np.maximum(d, 1).astype(np.float32)
    large = max_exact + (np.log(df / np.float32(max_exact)) / np.float32(math.log(MAX_DISTANCE / max_exact))
                         * np.float32(N_BUCKETS - max_exact)).astype(np.int32)
    large = np.minimum(large, N_BUCKETS - 1)
    bucket = np.where(d < max_exact, d, large)
    in_win = (dist >= 0) & (dist < SWA_WINDOW)
    return np.where(in_win, bucket, -1).astype(np.int32)


def _swa_bias_table(rel_bias, buckets):
    H = SWA_Q_HEADS

    def body(rb_ref, bk_ref, o_ref):
        bk = bk_ref[...]
        for h in range(H):
            acc = jnp.full(bk.shape, NEG, F32)
            for b in range(N_BUCKETS):
                acc = jnp.where(bk == b, rb_ref[b, h], acc)
            o_ref[h] = acc

    return pl.pallas_call(
        body,
        in_specs=[pl.BlockSpec(memory_space=pltpu.SMEM), pl.BlockSpec(memory_space=pltpu.VMEM)],
        out_specs=pl.BlockSpec(memory_space=pltpu.VMEM),
        out_shape=jax.ShapeDtypeStruct((H, BLOCK, 2 * BLOCK), F32),
        name="swa_bias_table",
    )(rel_bias, buckets)


def _swa_bias_grad(dbias, buckets):
    H = SWA_Q_HEADS

    def body(db_ref, bk_ref, o_ref):
        bk = bk_ref[...]
        rows = lax.broadcasted_iota(jnp.int32, (N_BUCKETS, 128), 0)
        lanes = lax.broadcasted_iota(jnp.int32, (N_BUCKETS, 128), 1)
        acc = jnp.zeros((N_BUCKETS, 128), F32)
        for h in range(H):
            d = db_ref[h]
            for b in range(N_BUCKETS):
                s = jnp.sum(jnp.sum(jnp.where(bk == b, d, 0.0), axis=1, keepdims=True), axis=0, keepdims=True)
                acc = jnp.where((rows == b) & (lanes == h), s, acc)
        o_ref[...] = acc

    return pl.pallas_call(
        body,
        in_specs=[pl.BlockSpec(memory_space=pltpu.VMEM)] * 2,
        out_specs=pl.BlockSpec(memory_space=pltpu.VMEM),
        out_shape=jax.ShapeDtypeStruct((N_BUCKETS, 128), F32),
        name="swa_bias_grad",
    )(dbias, buckets)


SWA_GROUP = 4


def _swa_stack(x, g):
    blocks = []
    for t in range(SWA_GROUP):
        h = SWA_GROUP * g + t
        xp = x[:, (h // 2) * BLOCK:(h // 2 + 1) * BLOCK]
        blocks.append(xp if h % 2 == g else pltpu.roll(xp, HEAD_DIM, 1))
    return jnp.concatenate(blocks, axis=0).astype(BF16)


def _swa_unstack(parts):
    pairs = []
    for p in range(4):
        g = p // 2
        halves = []
        for hh in range(2):
            t = 2 * p + hh - SWA_GROUP * g
            blk = parts[g][t * BLOCK:(t + 1) * BLOCK]
            halves.append(blk if hh == g else pltpu.roll(blk, HEAD_DIM, 1))
        pairs.append(halves[0] + halves[1])
    return jnp.concatenate(pairs, axis=1)


def _swa_group_inputs(g, sink_ref, bias_ref, kv_refs):
    lane = lax.broadcasted_iota(jnp.int32, (BLOCK, BLOCK), 1)
    km = (lane // HEAD_DIM) == g
    heads = range(SWA_GROUP * g, SWA_GROUP * (g + 1))
    bias = jnp.concatenate([bias_ref[h] for h in heads], axis=0)
    sink = jnp.concatenate([jnp.full((BLOCK, 1), sink_ref[0, h], F32) for h in heads], axis=0)
    return bias, sink, [jnp.where(km, r[...], 0) for r in kv_refs], km


def _swa_scores(qs, kp, kc, bias, sink, first):
    sp = _dot_nt(qs, kp) * SCALE64 + bias[:, :BLOCK]
    sp = jnp.where(first, NEG, sp)
    sc = _dot_nt(qs, kc) * SCALE64 + bias[:, BLOCK:]
    m = jnp.maximum(jnp.maximum(jnp.max(sp, axis=1, keepdims=True), jnp.max(sc, axis=1, keepdims=True)), sink)
    pp = jnp.exp(sp - m)
    pc = jnp.exp(sc - m)
    ps = jnp.exp(sink - m)
    den = jnp.sum(pp, axis=1, keepdims=True) + jnp.sum(pc, axis=1, keepdims=True) + ps
    return pp / den, pc / den, ps / den


def _swa_fwd(proj, sinks, bias_tab, B, S, hosted=None):
    nb = S // BLOCK
    T = B * S

    def body(sink_ref, q_ref, kp_ref, kc_ref, vp_ref, vc_ref, bias_ref, o_ref):
        i = pl.program_id(1)
        first = jnp.full((SWA_GROUP * BLOCK, BLOCK), i, jnp.int32) == 0
        q = q_ref[...].astype(F32)
        outs = []
        for g in range(2):
            bias, sink, (kp, kc, vp, vc), _ = _swa_group_inputs(g, sink_ref, bias_ref, (kp_ref, kc_ref, vp_ref, vc_ref))
            wp, wc, _ = _swa_scores(_swa_stack(q, g), kp, kc, bias, sink, first)
            outs.append(_dot(wp.astype(BF16), vp) + _dot(wc.astype(BF16), vc))
        o_ref[...] = _swa_unstack(outs).astype(o_ref.dtype)

    blk = (BLOCK, BLOCK)
    wide = (BLOCK, 4 * BLOCK)
    outs, moved = _hosted_call(
        body, (B, nb),
        [pl.BlockSpec(memory_space=pltpu.SMEM),
         pl.BlockSpec(wide, lambda b, i: (b * nb + i, COL_QA // 4)),
         pl.BlockSpec(blk, lambda b, i: (b * nb + jnp.maximum(i - 1, 0), COL_KA)),
         pl.BlockSpec(blk, lambda b, i: (b * nb + i, COL_KA)),
         pl.BlockSpec(blk, lambda b, i: (b * nb + jnp.maximum(i - 1, 0), COL_VA)),
         pl.BlockSpec(blk, lambda b, i: (b * nb + i, COL_VA)),
         pl.BlockSpec((SWA_Q_HEADS, BLOCK, 2 * BLOCK), lambda b, i: (0, 0, 0))],
        [pl.BlockSpec(wide, lambda b, i: (b * nb + i, 0))],
        [jax.ShapeDtypeStruct((T, 512), BF16)], [], hosted, "swa_fwd",
        (sinks, proj, proj, proj, proj, proj, bias_tab))
    return outs[0], moved


def _swa_bwd(proj, dy, sinks, bias_tab, B, S):
    nb = S // BLOCK
    T = B * S
    H = SWA_Q_HEADS

    def body(sink_ref, q_ref, kp_ref, kc_ref, vp_ref, vc_ref, do_ref, bias_ref,
             dq_ref, dk_ref, dv_ref, dbias_ref, dsink_ref):
        b = pl.program_id(0)
        i = pl.program_id(1)

        @pl.when((b == 0) & (i == 0))
        def _():
            dbias_ref[...] = jnp.zeros_like(dbias_ref)
            dsink_ref[...] = jnp.zeros_like(dsink_ref)

        @pl.when(i == 0)
        def _():
            dk_ref[...] = jnp.zeros_like(dk_ref)
            dv_ref[...] = jnp.zeros_like(dv_ref)

        first = jnp.full((SWA_GROUP * BLOCK, BLOCK), i, jnp.int32) == 0
        q = q_ref[...].astype(F32)
        do = do_ref[...].astype(F32)
        dqs, dsinks, dbs, dkp, dkc, dvp, dvc = [], [], [], [], [], [], []
        for g in range(2):
            bias, sink, (kp, kc, vp, vc), km = _swa_group_inputs(
                g, sink_ref, bias_ref, (kp_ref, kc_ref, vp_ref, vc_ref))
            qs = _swa_stack(q, g)
            dos = _swa_stack(do, g)
            wp, wc, ws = _swa_scores(qs, kp, kc, bias, sink, first)
            dwp = _dot_nt(dos, vp)
            dwc = _dot_nt(dos, vc)
            dsum = jnp.sum(wp * dwp, axis=1, keepdims=True) + jnp.sum(wc * dwc, axis=1, keepdims=True)
            dsp = wp * (dwp - dsum)
            dsc = wc * (dwc - dsum)
            dsk = -ws * dsum
            for t in range(SWA_GROUP):
                rows = slice(t * BLOCK, (t + 1) * BLOCK)
                dsinks.append(jnp.broadcast_to(jnp.sum(dsk[rows], axis=0, keepdims=True), (1, 128)))
                dbs.append(jnp.concatenate([dsp[rows], dsc[rows]], axis=1))
            dspb = dsp.astype(BF16)
            dscb = dsc.astype(BF16)
            dqs.append(_dot(dspb, kp) + _dot(dscb, kc))
            dkp.append(jnp.where(km, _dot_tn(dspb, qs), 0.0))
            dkc.append(jnp.where(km, _dot_tn(dscb, qs), 0.0))
            dvp.append(jnp.where(km, _dot_tn(wp.astype(BF16), dos), 0.0))
            dvc.append(jnp.where(km, _dot_tn(wc.astype(BF16), dos), 0.0))
        dq_ref[...] = (_swa_unstack(dqs) * SCALE64).astype(dq_ref.dtype)
        dsink_ref[...] += jnp.concatenate(dsinks, axis=0)
        for h in range(H):
            dbias_ref[h] += dbs[h]
        cur = pl.ds(pl.multiple_of(i * BLOCK, BLOCK), BLOCK)
        prev = pl.ds(pl.multiple_of(jnp.maximum(i - 1, 0) * BLOCK, BLOCK), BLOCK)
        dk_ref[prev, :] += (dkp[0] + dkp[1]) * SCALE64
        dk_ref[cur, :] += (dkc[0] + dkc[1]) * SCALE64
        dv_ref[prev, :] += dvp[0] + dvp[1]
        dv_ref[cur, :] += dvc[0] + dvc[1]

    blk = (BLOCK, BLOCK)
    wide = (BLOCK, 4 * BLOCK)
    kv_out = pl.BlockSpec((S, BLOCK), lambda b, i: (b, 0))
    full_bias = pl.BlockSpec((H, BLOCK, 2 * BLOCK), lambda b, i: (0, 0, 0))
    return pl.pallas_call(
        body,
        grid=(B, nb),
        in_specs=[pl.BlockSpec(memory_space=pltpu.SMEM),
                  pl.BlockSpec(wide, lambda b, i: (b * nb + i, COL_QA // 4)),
                  pl.BlockSpec(blk, lambda b, i: (b * nb + jnp.maximum(i - 1, 0), COL_KA)),
                  pl.BlockSpec(blk, lambda b, i: (b * nb + i, COL_KA)),
                  pl.BlockSpec(blk, lambda b, i: (b * nb + jnp.maximum(i - 1, 0), COL_VA)),
                  pl.BlockSpec(blk, lambda b, i: (b * nb + i, COL_VA)),
                  pl.BlockSpec(wide, lambda b, i: (b * nb + i, 0)),
                  full_bias],
        out_specs=[pl.BlockSpec(wide, lambda b, i: (b * nb + i, 0)),
                   kv_out, kv_out, full_bias,
                   pl.BlockSpec((H, 128), lambda b, i: (0, 0))],
        out_shape=[jax.ShapeDtypeStruct((T, 512), BF16),
                   jax.ShapeDtypeStruct((T, BLOCK), F32), jax.ShapeDtypeStruct((T, BLOCK), F32),
                   jax.ShapeDtypeStruct((H, BLOCK, 2 * BLOCK), F32), jax.ShapeDtypeStruct((H, 128), F32)],
        compiler_params=_params(),
        name="swa_bwd",
    )(sinks, proj, proj, proj, proj, proj, dy, bias_tab)


SB_TILE = 256


SB_HEADS = 4
SB_LANES = SB_HEADS * HEAD_DIM
SB_ROWS = SB_HEADS * SB_TILE


def _sb_logits(z, tri):
    sp = jnp.log(1.0 + jnp.exp(-jnp.abs(z)))
    ls = jnp.minimum(z, 0.0) - sp
    l1m = ls - z
    if tri is not None:
        l1m = jnp.where(tri, l1m, 0.0)
    return ls, l1m


def _sb_masks():
    lane = lax.broadcasted_iota(jnp.int32, (SB_TILE, SB_LANES), 1)
    hm = [(lane // HEAD_DIM) == h for h in range(SB_HEADS)]
    row = lax.broadcasted_iota(jnp.int32, (SB_ROWS, SB_TILE), 0) % SB_TILE
    col = lax.broadcasted_iota(jnp.int32, (SB_ROWS, SB_TILE), 1)
    return lane, hm, row, col


def _sb_stack(x, hm):
    return jnp.concatenate([jnp.where(m, x, 0) for m in hm], axis=0)


def _sb_unstack(x, hm):
    return sum(jnp.where(m, x[h * SB_TILE:(h + 1) * SB_TILE], 0.0) for h, m in enumerate(hm))


def _sb_fwd(proj, B, S, hosted=None):
    nt = S // SB_TILE
    T = B * S

    ng = 512 // SB_LANES

    def body(*refs):
        q_refs, k_refs, v_refs = refs[:ng], refs[ng:2 * ng], refs[2 * ng:3 * ng]
        o_ref, r_ref = refs[3 * ng:]
        i = pl.program_id(1)
        _, hm, row, col = _sb_masks()
        tri = col < row
        later = jnp.concatenate([(row[:SB_TILE] > col[:SB_TILE]).astype(BF16)] * 2, axis=0)
        qs = [_sb_stack(q_ref[...] * SCALE64, hm) for q_ref in q_refs]

        def tile(j, carry, mask):
            rows = pl.ds(pl.multiple_of(j * SB_TILE, SB_TILE), SB_TILE)
            out = []
            for g in range(ng):
                acc, c = carry[g]
                ls, l1m = _sb_logits(_dot_nt(qs[g], k_refs[g][rows, :]), mask)
                a = jnp.exp(ls + c + _dot_split(l1m, later))
                if mask is not None:
                    a = jnp.where(mask, a, 0.0)
                pv = _dot(a.astype(BF16), v_refs[g][rows, :])
                out.append((acc + _sb_unstack(pv, hm), c + jnp.sum(l1m, axis=1, keepdims=True)))
            return tuple(out)

        zero = (jnp.zeros((SB_TILE, SB_LANES), F32), jnp.zeros((SB_ROWS, 1), F32))
        carry = tile(i, (zero,) * ng, tri)
        carry = lax.fori_loop(0, i, lambda it, cr: tile(i - 1 - it, cr, None), carry)
        o_ref[...] = jnp.concatenate([acc for acc, _ in carry], axis=1).astype(o_ref.dtype)
        r_ref[...] = jnp.concatenate(
            [_sb_unstack(jnp.broadcast_to(c, (SB_ROWS, SB_LANES)), hm) for _, c in carry], axis=1)

    blk = (SB_TILE, SB_LANES)
    cq, ck, cv = (c * BLOCK // SB_LANES for c in (COL_QB, COL_KB, COL_VB))
    wide = pl.BlockSpec((SB_TILE, 512), lambda b, i: (b * nt + i, 0))
    outs, moved = _hosted_call(
        body, (B, nt),
        [pl.BlockSpec(blk, lambda b, i, g=g: (b * nt + i, cq + g)) for g in range(ng)]
        + [pl.BlockSpec((S, SB_LANES), lambda b, i, g=g: (b, ck + g)) for g in range(ng)]
        + [pl.BlockSpec((S, SB_LANES), lambda b, i, g=g: (b, cv + g)) for g in range(ng)],
        [wide, wide],
        [jax.ShapeDtypeStruct((T, 512), BF16), jax.ShapeDtypeStruct((T, 512), F32)], [], hosted, "sb_fwd",
        (proj,) * (3 * ng))
    return outs[0], outs[1], moved


def _sb_bwd(proj, dy, rtot, B, S, hosted=None):
    nt = S // SB_TILE
    T = B * S

    ng = 512 // SB_LANES

    def body(*refs):
        q_refs, k_refs, v_refs = refs[:ng], refs[ng:2 * ng], refs[2 * ng:3 * ng]
        do_ref, r_ref, dq_ref, dk_ref, dv_ref = refs[3 * ng:]
        i = pl.program_id(1)

        @pl.when(i == 0)
        def _():
            dk_ref[...] = jnp.zeros_like(dk_ref)
            dv_ref[...] = jnp.zeros_like(dv_ref)

        lane, hm, row, col = _sb_masks()
        tri = col < row
        later = jnp.concatenate([(row[:SB_TILE] > col[:SB_TILE]).astype(BF16)] * 2, axis=0)
        earlier = (row[:SB_TILE] < col[:SB_TILE]).astype(BF16)
        qs, dos, rs = [], [], []
        for g in range(ng):
            lanes = slice(g * SB_LANES, (g + 1) * SB_LANES)
            qs.append(_sb_stack(q_refs[g][...] * SCALE64, hm))
            dos.append(_sb_stack(do_ref[:, lanes], hm))
            r = r_ref[:, lanes]
            rs.append(jnp.concatenate([jnp.sum(jnp.where(lane == h * HEAD_DIM, r, 0.0), axis=1, keepdims=True)
                                       for h in range(SB_HEADS)], axis=0))

        def tile(j, carry, mask):
            rows = pl.ds(pl.multiple_of(j * SB_TILE, SB_TILE), SB_TILE)
            out = []
            for g in range(ng):
                lanes = slice(g * SB_LANES, (g + 1) * SB_LANES)
                dq, lsum, psum = carry[g]
                kj = k_refs[g][rows, :]
                vj = v_refs[g][rows, :]
                ls, l1m = _sb_logits(_dot_nt(qs[g], kj), mask)
                sig = jnp.exp(ls)
                lsum = lsum + jnp.sum(l1m, axis=1, keepdims=True)
                a = jnp.exp(ls + (rs[g] - lsum) + _dot_split(l1m, later))
                if mask is not None:
                    a = jnp.where(mask, a, 0.0)
                de = _dot_nt(dos[g], vj) * a
                pre = psum + _dot(de.astype(BF16), earlier)
                dz = de * (1.0 - sig) - sig * pre
                if mask is not None:
                    dz = jnp.where(mask, dz, 0.0)
                dz = dz.astype(BF16)
                dk_ref[rows, lanes] += _dot_tn(dz, qs[g])
                dv_ref[rows, lanes] += _dot_tn(a.astype(BF16), dos[g])
                out.append((dq + _sb_unstack(_dot(dz, kj), hm), lsum, psum + jnp.sum(de, axis=1, keepdims=True)))
            return tuple(out)

        zero = jnp.zeros((SB_ROWS, 1), F32)
        init = ((jnp.zeros((SB_TILE, SB_LANES), F32), zero, zero),) * ng
        carry = lax.fori_loop(0, i, lambda j, c: tile(j, c, None), init)
        carry = tile(i, carry, tri)
        dq_ref[...] = (jnp.concatenate([c[0] for c in carry], axis=1) * SCALE64).astype(dq_ref.dtype)

    blk = (SB_TILE, SB_LANES)
    cq, ck, cv = (c * BLOCK // SB_LANES for c in (COL_QB, COL_KB, COL_VB))
    wide = pl.BlockSpec((SB_TILE, 512), lambda b, i: (b * nt + i, 0))
    kv_out = pl.BlockSpec((S, 512), lambda b, i: (b, 0))
    outs, moved = _hosted_call(
        body, (B, nt),
        [pl.BlockSpec(blk, lambda b, i, g=g: (b * nt + i, cq + g)) for g in range(ng)]
        + [pl.BlockSpec((S, SB_LANES), lambda b, i, g=g: (b, ck + g)) for g in range(ng)]
        + [pl.BlockSpec((S, SB_LANES), lambda b, i, g=g: (b, cv + g)) for g in range(ng)]
        + [wide, wide],
        [wide, kv_out, kv_out],
        [jax.ShapeDtypeStruct((T, 512), BF16),
         jax.ShapeDtypeStruct((T, 512), F32), jax.ShapeDtypeStruct((T, 512), F32)], [], hosted, "sb_bwd",
        (proj,) * (3 * ng) + (dy, rtot))
    return outs[0], outs[1], outs[2], moved


def _mem_weights(q, mk):
    z = _dot_nt(q, mk) * SCALE128
    e = jnp.exp(z - jnp.max(z, axis=1, keepdims=True))
    return e / jnp.sum(e, axis=1, keepdims=True)


def _mem_fwd(proj, mkv, B, S, M):
    tq = 512
    nq = S // tq
    T = B * S
    Hm = MEM_HEADS

    def body(q0, q1, q2, q3, mk_ref, mv_ref, o_ref):
        outs = []
        for h, q_ref in enumerate((q0, q1, q2, q3)):
            cols = slice(h * 128, (h + 1) * 128)
            w = _mem_weights(q_ref[...], mk_ref[:, cols])
            outs.append(_dot(w.astype(BF16), mv_ref[:, cols]))
        o_ref[...] = jnp.concatenate(outs, axis=1).astype(o_ref.dtype)

    return pl.pallas_call(
        body,
        grid=(B, nq),
        in_specs=[pl.BlockSpec((tq, 128), lambda b, i, h=h: (b * nq + i, COL_QM + h)) for h in range(Hm)]
        + [pl.BlockSpec((M, 512), lambda b, i: (b, 0)), pl.BlockSpec((M, 512), lambda b, i: (b, 1))],
        out_specs=pl.BlockSpec((tq, 512), lambda b, i: (b * nq + i, 0)),
        out_shape=jax.ShapeDtypeStruct((T, 512), BF16),
        name="mem_fwd",
    )(proj, proj, proj, proj, mkv, mkv)


def _mem_bwd(proj, mkv, dy, B, S, M):
    tq = 512
    nq = S // tq
    T = B * S
    Hm = MEM_HEADS

    def body(q0, q1, q2, q3, mk_ref, mv_ref, do_ref, dq_ref, dmk_ref, dmv_ref):
        i = pl.program_id(1)
        dqs, dmks, dmvs = [], [], []
        for h, q_ref in enumerate((q0, q1, q2, q3)):
            cols = slice(h * 128, (h + 1) * 128)
            q = q_ref[...]
            do = do_ref[:, cols]
            mk = mk_ref[:, cols]
            w = _mem_weights(q, mk)
            dw = _dot_nt(do, mv_ref[:, cols])
            ds = (w * (dw - jnp.sum(w * dw, axis=1, keepdims=True))).astype(BF16)
            dqs.append(_dot(ds, mk))
            dmks.append(_dot_tn(ds, q))
            dmvs.append(_dot_tn(w.astype(BF16), do))
        dq_ref[...] = (jnp.concatenate(dqs, axis=1) * SCALE128).astype(dq_ref.dtype)

        @pl.when(i == 0)
        def _():
            dmk_ref[...] = jnp.zeros_like(dmk_ref)
            dmv_ref[...] = jnp.zeros_like(dmv_ref)

        dmk_ref[...] += jnp.concatenate(dmks, axis=1) * SCALE128
        dmv_ref[...] += jnp.concatenate(dmvs, axis=1)

    q_spec = pl.BlockSpec((tq, 512), lambda b, i: (b * nq + i, 0))
    m_out = pl.BlockSpec((M, 512), lambda b, i: (b, 0))
    return pl.pallas_call(
        body,
        grid=(B, nq),
        in_specs=[pl.BlockSpec((tq, 128), lambda b, i, h=h: (b * nq + i, COL_QM + h)) for h in range(Hm)]
        + [pl.BlockSpec((M, 512), lambda b, i: (b, 0)), pl.BlockSpec((M, 512), lambda b, i: (b, 1)), q_spec],
        out_specs=[q_spec, m_out, m_out],
        out_shape=[jax.ShapeDtypeStruct((T, 512), BF16),
                   jax.ShapeDtypeStruct((B * M, 512), F32), jax.ShapeDtypeStruct((B * M, 512), F32)],
        name="mem_bwd",
    )(proj, proj, proj, proj, mkv, mkv, dy)


def _all_gather(blk, name):
    R, C = blk.shape

    def body(x_ref, out_ref, send_sems, recv_sems, local_sem):
        x, y, c = _mesh_pos()
        me, sibling = (x, y, c), (x, y, 1 - c)
        chips = [(1 - x, y), (x, 1 - y), (1 - x, 1 - y)]

        def slot(px, py, pc):
            return out_ref.at[4 * px + 2 * py + pc]

        def copy(k, block, to, src=None):
            return pltpu.make_async_remote_copy(
                src_ref=slot(*block) if src is None else src, dst_ref=slot(*block),
                send_sem=send_sems.at[k], recv_sem=recv_sems.at[k],
                device_id=to, device_id_type=pl.DeviceIdType.MESH)

        mine = pltpu.make_async_copy(x_ref, slot(*me), local_sem)
        mine.start()
        first = [copy(0, me, sibling, src=x_ref)]
        first += [copy(1 + j, me, (*chip, c), src=x_ref) for j, chip in enumerate(chips)]
        for cp in first:
            cp.start()
        passed = [copy(4 + j, (*chip, c), sibling) for j, chip in enumerate(chips)]
        for j, chip in enumerate(chips):
            copy(1 + j, (*chip, c), me).wait_recv()
            passed[j].start()
        copy(0, sibling, me).wait_recv()
        for j, chip in enumerate(chips):
            copy(4 + j, (*chip, 1 - c), me).wait_recv()
        for cp in first + passed:
            cp.wait_send()
        mine.wait()

    return pl.pallas_call(
        body,
        in_specs=[pl.BlockSpec(memory_space=pl.ANY)],
        out_specs=pl.BlockSpec(memory_space=pl.ANY),
        out_shape=jax.ShapeDtypeStruct((N_DEV, R, C), blk.dtype),
        scratch_shapes=[pltpu.SemaphoreType.DMA((7,)), pltpu.SemaphoreType.DMA((7,)), pltpu.SemaphoreType.DMA],
        name=name,
    )(blk)


_HBM = pl.BlockSpec(memory_space=pltpu.HBM)
_SEM = pl.BlockSpec(memory_space=pltpu.SEMAPHORE)


def _scatter_start(parts, carried, name):
    hosted = _Hosted(scatters=[parts])

    def body(p_ref, land_ref, c_ref, send_sems, recv_sems, local_sems, p_thru, land_thru, c_thru):
        for cp in hosted.copies([p_ref], [land_ref], send_sems, recv_sems, local_sems):
            cp.start()

    sems = (pltpu.SemaphoreType.DMA((7,)), pltpu.SemaphoreType.DMA((7,)), pltpu.SemaphoreType.DMA((1,)))
    hbm = lambda a: pltpu.HBM(a.shape, a.dtype)
    outs = pl.pallas_call(
        body, name=name,
        out_shape=sems + (hbm(parts), hbm(parts), hbm(carried)),
        in_specs=(_HBM, _HBM, _HBM), out_specs=(_SEM, _SEM, _SEM, _HBM, _HBM, _HBM),
        input_output_aliases={0: 3, 1: 4, 2: 5},
        compiler_params=pltpu.CompilerParams(has_side_effects=pltpu.SideEffectType.DATAFLOW_SIDE_EFFECTING),
    )(pltpu.with_memory_space_constraint(parts, pltpu.HBM),
      pltpu.with_memory_space_constraint(lax.empty(parts.shape, parts.dtype), pltpu.HBM),
      pltpu.with_memory_space_constraint(carried, pltpu.HBM))
    return outs[:5], outs[5]


def _scatter_wait(flight, after, name):
    send_sems, recv_sems, local_sems, p_thru, land_thru = flight
    hosted = _Hosted(scatters=[p_thru])

    def body(p_ref, land_ref, send, recv, local, after_ref, p_dead, got_ref):
        for cp in hosted.copies([p_ref], [land_ref], send, recv, local):
            cp.wait()

    hbm = lambda a: pltpu.HBM(a.shape, a.dtype)
    return pl.pallas_call(
        body, name=name,
        out_shape=(hbm(p_thru), hbm(land_thru)),
        in_specs=(_HBM, _HBM, _SEM, _SEM, _SEM, pl.BlockSpec(memory_space=pl.ANY)), out_specs=(_HBM, _HBM),
        input_output_aliases={0: 0, 1: 1},
        compiler_params=pltpu.CompilerParams(has_side_effects=pltpu.SideEffectType.DATAFLOW_SIDE_EFFECTING),
    )(p_thru, land_thru, send_sems, recv_sems, local_sems, after)[1]


def _adamw(parts, w, m, v, name):
    R, C = w.shape
    tr = R
    for cand in (368, 352, 256, 176, 128, 64, 32, 16, 8):
        if R % cand == 0 and cand * C * 4 <= 1536 * 1024:
            tr = cand
            break
    c1 = 1.0 - ADAM_B1 ** ADAM_STEP
    c2 = 1.0 - ADAM_B2 ** ADAM_STEP

    def body(p_ref, w_ref, m_ref, v_ref, g_ref, d_ref, nm_ref, nv_ref):
        g = p_ref[0].astype(F32)
        for d in range(1, N_DEV):
            g = g + p_ref[d].astype(F32)
        nm = ADAM_B1 * m_ref[...] + (1.0 - ADAM_B1) * g
        nv = ADAM_B2 * v_ref[...] + (1.0 - ADAM_B2) * (g * g)
        g_ref[...] = g
        nm_ref[...] = nm
        nv_ref[...] = nv
        d_ref[...] = -ADAM_LR * ((nm / c1) / (jnp.sqrt(nv / c2) + ADAM_EPS) + ADAM_WD * w_ref[...])

    row = pl.BlockSpec((tr, C), lambda i: (i, 0))
    out = jax.ShapeDtypeStruct((R, C), F32)
    return pl.pallas_call(
        body,
        grid=(R // tr,),
        in_specs=[pl.BlockSpec((N_DEV, tr, C), lambda i: (0, i, 0)), row, row, row],
        out_specs=[row] * 4,
        out_shape=[out] * 4,
        compiler_params=_params(),
        name=name,
    )(parts, w, m, v)


def _col_shards(g):
    R, C8 = g.shape
    return g.reshape(R, N_DEV, C8 // N_DEV).transpose(1, 0, 2)


def _row_shards(g):
    R8, C = g.shape
    return g.reshape(N_DEV, R8 // N_DEV, C)


def _cols_full(gathered):
    n, R, C = gathered.shape
    return gathered.transpose(1, 0, 2).reshape(R, n * C)


_BIG = ("w_in", "w_mem_kv", "w_branch_swa", "w_branch_sb", "w_branch_mem", "w_out", "w_gate", "w_up", "w_down")
_COL_SHARDED = ("w_branch_swa", "w_branch_sb", "w_branch_mem")
_TRANSPOSED = ("w_in", "w_gate", "w_up")
_SMALL = ("ln_mix_pre", "ln_mix_post", "swa_sinks", "rel_bias", "ln_mem", "ln_ffn_pre", "ln_ffn_post")
_ORDER = ("ln_mix_pre", "ln_mix_post", "w_in", "swa_sinks", "rel_bias", "ln_mem", "w_mem_kv", "w_branch_swa",
          "w_branch_sb", "w_branch_mem", "w_out", "ln_ffn_pre", "ln_ffn_post", "w_gate", "w_up", "w_down")


def _pack_small(d, last=None):
    rows = [d["ln_mix_pre"], d["ln_mix_post"], d["ln_mem"], d["ln_ffn_pre"], d["ln_ffn_post"],
            jnp.pad(d["swa_sinks"].reshape(1, -1), ((0, 0), (0, D_MODEL - SWA_Q_HEADS))),
            jnp.pad(d["rel_bias"].reshape(1, -1), ((0, 0), (0, D_MODEL - N_BUCKETS * SWA_Q_HEADS))),
            jnp.zeros((1, D_MODEL), F32) if last is None else last]
    return jnp.concatenate([r.astype(F32) for r in rows], axis=0)


def _unpack_small(a):
    return dict(ln_mix_pre=a[0:1], ln_mix_post=a[1:2], ln_mem=a[2:3], ln_ffn_pre=a[3:4], ln_ffn_post=a[4:5],
                swa_sinks=a[5:6, :SWA_Q_HEADS],
                rel_bias=a[6, :N_BUCKETS * SWA_Q_HEADS].reshape(N_BUCKETS, SWA_Q_HEADS))


def kernel(x, mem, ln_mix_pre, ln_mix_post, w_in, swa_sinks, rel_bias, ln_mem, w_mem_kv, w_branch_swa, w_branch_sb, w_branch_mem, w_out, ln_ffn_pre, ln_ffn_post, w_gate, w_up, w_down, loss_target, m_ln_mix_pre, m_ln_mix_post, m_w_in, m_swa_sinks, m_rel_bias, m_ln_mem, m_w_mem_kv, m_w_branch_swa, m_w_branch_sb, m_w_branch_mem, m_w_out, m_ln_ffn_pre, m_ln_ffn_post, m_w_gate, m_w_up, m_w_down, v_ln_mix_pre, v_ln_mix_post, v_w_in, v_swa_sinks, v_rel_bias, v_ln_mem, v_w_mem_kv, v_w_branch_swa, v_w_branch_sb, v_w_branch_mem, v_w_out, v_ln_ffn_pre, v_ln_ffn_post, v_w_gate, v_w_up, v_w_down):
    w = dict(ln_mix_pre=ln_mix_pre, ln_mix_post=ln_mix_post, w_in=w_in[0], swa_sinks=swa_sinks, rel_bias=rel_bias,
             ln_mem=ln_mem, w_mem_kv=w_mem_kv[0], w_branch_swa=w_branch_swa[0], w_branch_sb=w_branch_sb[0],
             w_branch_mem=w_branch_mem[0], w_out=w_out[0], ln_ffn_pre=ln_ffn_pre, ln_ffn_post=ln_ffn_post,
             w_gate=w_gate[0], w_up=w_up[0], w_down=w_down[0])
    mom = dict(ln_mix_pre=m_ln_mix_pre, ln_mix_post=m_ln_mix_post, w_in=m_w_in[0], swa_sinks=m_swa_sinks,
               rel_bias=m_rel_bias, ln_mem=m_ln_mem, w_mem_kv=m_w_mem_kv[0], w_branch_swa=m_w_branch_swa[0],
               w_branch_sb=m_w_branch_sb[0], w_branch_mem=m_w_branch_mem[0], w_out=m_w_out[0],
               ln_ffn_pre=m_ln_ffn_pre, ln_ffn_post=m_ln_ffn_post, w_gate=m_w_gate[0], w_up=m_w_up[0],
               w_down=m_w_down[0])
    var = dict(ln_mix_pre=v_ln_mix_pre, ln_mix_post=v_ln_mix_post, w_in=v_w_in[0], swa_sinks=v_swa_sinks,
               rel_bias=v_rel_bias, ln_mem=v_ln_mem, w_mem_kv=v_w_mem_kv[0], w_branch_swa=v_w_branch_swa[0],
               w_branch_sb=v_w_branch_sb[0], w_branch_mem=v_w_branch_mem[0], w_out=v_w_out[0],
               ln_ffn_pre=v_ln_ffn_pre, ln_ffn_post=v_ln_ffn_post, w_gate=v_w_gate[0], w_up=v_w_up[0],
               w_down=v_w_down[0])
    B, S, D = x.shape
    M = mem.shape[1]
    T = B * S
    F = D_FF
    x2 = x.reshape(T, D)
    mem2 = mem.reshape(B * M, D)
    t2 = loss_target.reshape(T, D)
    buckets = jnp.asarray(_swa_buckets())
    for d in (w, mom, var):
        for n in _TRANSPOSED:
            d[n] = d[n].T
    wb = {n: w[n].astype(BF16) for n in _BIG}
    full = {}

    def landed(names, got):
        for n, g in zip(names, got):
            full[n] = _cols_full(g) if n in _COL_SHARDED else g.reshape(-1, g.shape[-1])

    def shards(n, g):
        return _col_shards(g) if n in _COL_SHARDED else _row_shards(g)

    landed(["w_in"], [_all_gather(wb["w_in"], "ag_w_in")])
    u = _rms_fwd(x2, ln_mix_pre, "rms_mix_pre")
    early = ["w_mem_kv", "w_branch_swa", "w_branch_sb", "w_branch_mem"]
    proj, got = _matmul([(u, full["w_in"])], "nt", BF16, 512, IN_W // 2, D, "proj_in",
                        hosted=_Hosted(gathers=[wb[n] for n in early]))
    landed(early, got)
    mn = _rms_fwd(mem2, ln_mem, "rms_mem")
    mkv = _matmul([(mn, full["w_mem_kv"])], "nn", BF16, 512, 1024, D, "proj_mem")
    bias_tab = _swa_bias_table(rel_bias, buckets)
    y_swa, got = _swa_fwd(proj, swa_sinks, bias_tab, B, S, hosted=_Hosted(gathers=[wb["w_out"]]))
    landed(["w_out"], got)
    late = ["w_gate", "w_up", "w_down"]
    y_sb, rtot, got = _sb_fwd(proj, B, S, hosted=_Hosted(gathers=[wb[n] for n in late]))
    landed(late, got)
    y_mem = _mem_fwd(proj, mkv, B, S, M)
    wbs = (full["w_branch_swa"], full["w_branch_sb"], full["w_branch_mem"])
    merged, p_swa, p_sb, p_mem = _branch_gate(proj, (y_swa, y_sb, y_mem), wbs)
    mix = _matmul([(merged, full["w_out"])], "nn", F32, 512, 1024, D, "proj_out")
    h1, u2 = _post_pre(x2, mix, ln_mix_post, ln_ffn_pre)
    a, zg, zu = _ffn_up(u2, full["w_gate"], full["w_up"])
    ffn = _matmul([(a, full["w_down"])], "nn", F32, 512, 1024, F, "ffn_down")
    dffn, dh2, loss_tile, d_ln_ffn_post = _loss_head(ffn, h1, t2, ln_ffn_post)

    part = {}
    part["w_down"] = _matmul([(a, dffn)], "tn", BF16, F // 2, 1024, 1024, "dw_down")
    dzg, dzu = _ffn_down_bwd(dffn, full["w_down"], zg, zu)
    part["w_gate"] = _matmul([(dzg, u2)], "tn", BF16, F // 2, 1024, 1024, "dw_gate")
    part["w_up"] = _matmul([(dzu, u2)], "tn", BF16, F // 2, 1024, 1024, "dw_up")
    du2 = _matmul([(dzg, full["w_gate"]), (dzu, full["w_up"])], "nn", F32, 512, 1024, F // 2, "d_u2")
    dh1, dmix, d_ln_ffn_pre, d_ln_mix_post = _mid_bwd(h1, du2, dh2, mix, ln_ffn_pre, ln_mix_post)
    part["w_out"] = _matmul([(merged, dmix)], "tn", BF16, 1024, 1024, 1024, "dw_out")
    dmerged = _matmul([(dmix, full["w_out"])], "nt", F32, 512, 1024, D, "d_merged")
    (dp_swa, dp_sb, dp_mem, dg0, dg1, dg2, dy_swa, dy_sb, dy_mem) = _gate_bwd(
        dmerged, (p_swa, p_sb, p_mem), proj, wbs)
    part["w_branch_swa"] = _matmul([(y_swa, dp_swa)], "tn", BF16, 512, 1024, 1024, "dw_branch_swa")
    part["w_branch_sb"] = _matmul([(y_sb, dp_sb)], "tn", BF16, 512, 1024, 1024, "dw_branch_sb")
    part["w_branch_mem"] = _matmul([(y_mem, dp_mem)], "tn", BF16, 512, 1024, 1024, "dw_branch_mem")
    dqm, dmk, dmv = _mem_bwd(proj, mkv, dy_mem, B, S, M)
    dmkv = jnp.concatenate([dmk, dmv], axis=1).astype(BF16)
    part["w_mem_kv"] = _matmul([(mn, dmkv)], "tn", BF16, 1024, 1024, 512, "dw_mem_kv")
    dmn = _matmul([(dmkv, full["w_mem_kv"])], "nt", F32, 512, 1024, 1024, "d_mn")
    d_ln_mem = _gain_grad(mem2, dmn)
    dqa, dka, dva, dbias, dsink = _swa_bwd(proj, dy_swa, swa_sinks, bias_tab, B, S)
    behind_sb = ["w_down", "w_gate", "w_up", "w_out", "w_branch_swa", "w_branch_sb", "w_branch_mem", "w_mem_kv"]
    dqb, dkb, dvb, got = _sb_bwd(proj, dy_sb, rtot, B, S,
                                 hosted=_Hosted(scatters=[shards(n, part[n]) for n in behind_sb]))
    recv = dict(zip(behind_sb, got))
    d_rel_bias = _swa_bias_grad(dbias, buckets)[:, :SWA_Q_HEADS]
    d_sinks = dsink[:, 0].reshape(1, SWA_Q_HEADS)
    dproj = jnp.concatenate([dqa, dka.astype(BF16), dva.astype(BF16), dqb, dkb.astype(BF16), dvb.astype(BF16),
                             dqm, dg0, dg1, dg2], axis=1)
    part["w_in"] = _matmul([(dproj, u)], "tn", BF16, IN_W // 2, 512, 1024, "dw_in")
    flight, dproj = _scatter_start(_row_shards(part["w_in"]), dproj, "rs_w_in_start")
    du = _matmul([(dproj, full["w_in"])], "nn", F32, 512, 1024, IN_W // 2, "d_u")
    grad_x, d_ln_mix_pre = _pre_bwd(x2, du, dh1, ln_mix_pre)

    out = {n: _adamw(recv[n], w[n], mom[n], var[n], "adamw_" + n) for n in _BIG if n != "w_in"}
    small_grads = dict(ln_mix_pre=d_ln_mix_pre, ln_mix_post=d_ln_mix_post, swa_sinks=d_sinks, rel_bias=d_rel_bias,
                       ln_mem=d_ln_mem, ln_ffn_pre=d_ln_ffn_pre, ln_ffn_post=d_ln_ffn_post)
    small_parts = _all_gather(_pack_small(small_grads, jnp.tile(loss_tile[0:1], (1, D // 128))), "ag_small")
    res = _adamw(small_parts, _pack_small(w), _pack_small(mom), _pack_small(var), "adamw_small")
    loss = res[0][7, 0]
    small = [_unpack_small(r) for r in res]
    for n in _SMALL:
        out[n] = tuple(s[n] for s in small)
    recv["w_in"] = _scatter_wait(flight, res[0], "rs_w_in_wait")
    out["w_in"] = _adamw(recv["w_in"], w["w_in"], mom["w_in"], var["w_in"], "adamw_w_in")
    for n in _TRANSPOSED:
        out[n] = tuple(o.T for o in out[n])

    like = dict(ln_mix_pre=ln_mix_pre, ln_mix_post=ln_mix_post, w_in=w_in, swa_sinks=swa_sinks, rel_bias=rel_bias,
                ln_mem=ln_mem, w_mem_kv=w_mem_kv, w_branch_swa=w_branch_swa, w_branch_sb=w_branch_sb,
                w_branch_mem=w_branch_mem, w_out=w_out, ln_ffn_pre=ln_ffn_pre, ln_ffn_post=ln_ffn_post,
                w_gate=w_gate, w_up=w_up, w_down=w_down)
    result = [loss, grad_x.reshape(B, S, D)]
    for k in range(4):
        result += [out[n][k].reshape(like[n].shape) for n in _ORDER]
    return tuple(result)
```

```python
import functools
import math

import numpy as np
import jax
import jax.numpy as jnp
from jax import lax
from jax.experimental import pallas as pl
from jax.experimental.pallas import tpu as pltpu

F32 = jnp.float32
BF16 = jnp.bfloat16

N_DEV = 8
D_MODEL = 1024
BLOCK = 128
EPS = 1e-6
HEAD_DIM = 64
SWA_Q_HEADS = 8
SWA_WINDOW = 128
N_BUCKETS = 32
MAX_DISTANCE = 128
MEM_HEADS = 4
MEM_HEAD_DIM = 128
D_FF = 2816
IN_W = 5888
COL_QA, COL_KA, COL_VA, COL_QB, COL_KB, COL_VB, COL_QM, COL_GL = 0, 4, 5, 6, 10, 14, 18, 22
SCALE64 = HEAD_DIM ** -0.5
SCALE128 = MEM_HEAD_DIM ** -0.5
NEG = -1e30

ADAM_LR = 0.001
ADAM_B1 = 0.9
ADAM_B2 = 0.999
ADAM_EPS = 1e-08
ADAM_WD = 0.01
ADAM_STEP = 10

VMEM_LIMIT_BYTES = 56 * 1024 * 1024


def _params(**kw):
    return pltpu.CompilerParams(vmem_limit_bytes=VMEM_LIMIT_BYTES, **kw)


def _dot(a, b):
    return jnp.dot(a, b, preferred_element_type=F32)


def _dot_nt(a, b):
    return lax.dot_general(a, b, (((1,), (1,)), ((), ())), preferred_element_type=F32)


def _dot_tn(a, b):
    return lax.dot_general(a, b, (((0,), (0,)), ((), ())), preferred_element_type=F32)


def _dot_split(x, m2):
    hi = x.astype(BF16)
    lo = (x - hi.astype(F32)).astype(BF16)
    return _dot(jnp.concatenate([hi, lo], axis=1), m2)


def _mesh_pos():
    return lax.axis_index("x"), lax.axis_index("y"), lax.axis_index("c")


class _Hosted:
    def __init__(self, gathers=(), scatters=(), window=None):
        self.items = [("g", a) for a in gathers] + [("s", a) for a in scatters]
        self.n = len(self.items)
        self.window = window

    def operands(self):
        return [a for _, a in self.items]

    def specs(self):
        return [pl.BlockSpec(memory_space=pl.ANY)] * self.n

    def out_shapes(self):
        return [jax.ShapeDtypeStruct(((N_DEV,) + a.shape) if kind == "g" else a.shape, a.dtype)
                for kind, a in self.items]

    def scratch(self):
        return [pltpu.SemaphoreType.DMA((7 * self.n,)), pltpu.SemaphoreType.DMA((7 * self.n,)),
                pltpu.SemaphoreType.DMA((self.n,))]

    def copies(self, in_refs, out_refs, send_sems, recv_sems, local_sems):
        x, y, c = _mesh_pos()
        me = 4 * x + 2 * y + c
        out = []
        for t, (kind, _) in enumerate(self.items):
            own = in_refs[t] if kind == "g" else in_refs[t].at[me]
            dst = out_refs[t].at[me]
            if self.window is not None:
                dst = out_refs[t].at[me, :, pl.ds(*self.window)]
            out.append(pltpu.make_async_copy(own, dst, local_sems.at[t]))
            for k in range(1, N_DEV):
                px, py, pc = x ^ (k >> 2), y ^ ((k >> 1) & 1), c ^ (k & 1)
                src = in_refs[t] if kind == "g" else in_refs[t].at[4 * px + 2 * py + pc]
                out.append(pltpu.make_async_remote_copy(
                    src_ref=src, dst_ref=dst,
                    send_sem=send_sems.at[7 * t + k - 1], recv_sem=recv_sems.at[7 * t + k - 1],
                    device_id=(px, py, pc), device_id_type=pl.DeviceIdType.MESH))
        return out


def _host(body, n_in, n_out, hosted, grid):
    if hosted is None:
        return body
    nc = hosted.n

    def wrapped(*refs):
        ins = refs[:n_in]
        cin = refs[n_in:n_in + nc]
        outs = refs[n_in + nc:n_in + nc + n_out]
        cout = refs[n_in + nc + n_out:n_in + 2 * nc + n_out]
        scratch = refs[n_in + 2 * nc + n_out:len(refs) - 3]
        sems = refs[len(refs) - 3:]
        ids = [pl.program_id(d) for d in range(len(grid))]
        first = functools.reduce(lambda a, b: a & b, [i == 0 for i in ids])
        last = functools.reduce(lambda a, b: a & b, [i == g - 1 for i, g in zip(ids, grid)])

        @pl.when(first)
        def _():
            for cp in hosted.copies(cin, cout, *sems):
                cp.start()

        body(*ins, *outs, *scratch)

        @pl.when(last)
        def _():
            for cp in hosted.copies(cin, cout, *sems):
                cp.wait()

    return wrapped


def _hosted_call(body, grid, in_specs, out_specs, out_shape, scratch_shapes, hosted, name, args):
    n_out = len(out_specs)
    if hosted is None:
        outs = pl.pallas_call(body, grid=grid, in_specs=in_specs, out_specs=out_specs, out_shape=out_shape,
                              scratch_shapes=scratch_shapes, compiler_params=_params(), name=name)(*args)
        return list(outs), []
    outs = pl.pallas_call(
        _host(body, len(in_specs), n_out, hosted, grid),
        grid=grid,
        in_specs=list(in_specs) + hosted.specs(),
        out_specs=list(out_specs) + hosted.specs(),
        out_shape=list(out_shape) + hosted.out_shapes(),
        scratch_shapes=list(scratch_shapes) + hosted.scratch(),
        compiler_params=_params(),
        name=name,
    )(*args, *hosted.operands())
    return list(outs[:n_out]), list(outs[n_out:])


_DIMS = {"nn": (((1,), (0,)), ((), ())), "nt": (((1,), (1,)), ((), ())), "tn": (((0,), (0,)), ((), ()))}


def _matmul(pairs, mode, out_dtype, tm, tn, tk, name, hosted=None, n_cols=None, n_off=0):
    a0, b0 = pairs[0]
    if mode == "nn":
        (M, K), N = a0.shape, b0.shape[1]
    elif mode == "nt":
        (M, K), N = a0.shape, b0.shape[0]
    else:
        (K, M), N = a0.shape, b0.shape[1]
    N = N if n_cols is None else n_cols
    tm, tn, tk = min(tm, M), min(tn, N), min(tk, K)
    assert M % tm == 0 and N % tn == 0 and K % tk == 0, (name, M, N, K, tm, tn, tk)
    nm, nn, nk = M // tm, N // tn, K // tk
    npair = len(pairs)
    dims = _DIMS[mode]

    def body(*refs):
        ab = refs[:2 * npair]
        o_ref = refs[2 * npair]
        acc_ref = refs[2 * npair + 1]
        k = pl.program_id(2)
        part = lax.dot_general(ab[0][...], ab[1][...], dims, preferred_element_type=F32)
        for q in range(1, npair):
            part += lax.dot_general(ab[2 * q][...], ab[2 * q + 1][...], dims, preferred_element_type=F32)
        if nk == 1:
            o_ref[...] = part.astype(o_ref.dtype)
        else:
            @pl.when(k == 0)
            def _():
                acc_ref[...] = part

            @pl.when(k > 0)
            def _():
                acc_ref[...] += part

            @pl.when(k == nk - 1)
            def _():
                o_ref[...] = acc_ref[...].astype(o_ref.dtype)

    if mode == "nn":
        a_spec = pl.BlockSpec((tm, tk), lambda n, m, k: (m, k))
        b_spec = pl.BlockSpec((tk, tn), lambda n, m, k: (k, n + n_off))
    elif mode == "nt":
        a_spec = pl.BlockSpec((tm, tk), lambda n, m, k: (m, k))
        b_spec = pl.BlockSpec((tn, tk), lambda n, m, k: (n + n_off, k))
    else:
        a_spec = pl.BlockSpec((tk, tm), lambda n, m, k: (k, m))
        b_spec = pl.BlockSpec((tk, tn), lambda n, m, k: (k, n + n_off))
    args = [t for pr in pairs for t in pr]
    outs, moved = _hosted_call(
        body, (nn, nm, nk), [a_spec, b_spec] * npair, [pl.BlockSpec((tm, tn), lambda n, m, k: (m, n))],
        [jax.ShapeDtypeStruct((M, N), out_dtype)], [pltpu.VMEM((tm, tn) if nk > 1 else (8, 128), F32)],
        hosted, name, args)
    return outs[0] if hosted is None else (outs[0], moved)


def _rms_fwd(x, g, name):
    T, D = x.shape
    tr = min(512, T)

    def body(x_ref, g_ref, u_ref):
        xf = x_ref[...]
        r = lax.rsqrt(jnp.mean(xf * xf, axis=-1, keepdims=True) + EPS)
        u_ref[...] = ((xf * r) * g_ref[...]).astype(u_ref.dtype)

    return pl.pallas_call(
        body,
        grid=(T // tr,),
        in_specs=[pl.BlockSpec((tr, D), lambda i: (i, 0)), pl.BlockSpec((1, D), lambda i: (0, 0))],
        out_specs=pl.BlockSpec((tr, D), lambda i: (i, 0)),
        out_shape=jax.ShapeDtypeStruct((T, D), BF16),
        name=name,
    )(x, g)


def _rms_bwd_terms(xin, g, dy):
    r = lax.rsqrt(jnp.mean(xin * xin, axis=-1, keepdims=True) + EPS)
    xh = xin * r
    dg = jnp.sum(dy * xh, axis=0, keepdims=True)
    dxh = dy * g
    dx = r * (dxh - xh * jnp.mean(dxh * xh, axis=-1, keepdims=True))
    return dx, dg


GATE_TC = 256


def _branch_gate(proj, ys, wbs):
    T = proj.shape[0]
    D = D_MODEL
    tr, tc = min(1024, T), GATE_TC
    nc = D // tc
    gl0 = COL_GL * 128 // tc

    def body(ya, yb, yc, wa, wb, wc, g0, g1, g2, merged_ref, pa, pb, pc):
        acc = jnp.zeros((tr, tc), F32)
        for y_ref, w_ref, g_ref, p_ref in ((ya, wa, g0, pa), (yb, wb, g1, pb), (yc, wc, g2, pc)):
            p = _dot(y_ref[...], w_ref[...])
            p_ref[...] = p.astype(p_ref.dtype)
            acc += jax.nn.sigmoid(g_ref[...].astype(F32)) * p
        merged_ref[...] = acc.astype(merged_ref.dtype)

    y_spec = pl.BlockSpec((tr, 512), lambda i, n: (i, 0))
    w_spec = pl.BlockSpec((512, tc), lambda i, n: (0, n))
    o_spec = pl.BlockSpec((tr, tc), lambda i, n: (i, n))
    gl_specs = [pl.BlockSpec((tr, tc), lambda i, n, j=j: (i, gl0 + j * nc + n)) for j in range(3)]
    out = jax.ShapeDtypeStruct((T, D), BF16)
    return pl.pallas_call(
        body,
        grid=(T // tr, nc),
        in_specs=[y_spec] * 3 + [w_spec] * 3 + gl_specs,
        out_specs=[o_spec] * 4,
        out_shape=[out] * 4,
        compiler_params=_params(),
        name="branch_gate",
    )(*ys, *wbs, proj, proj, proj)


def _post_pre(x, merged, w_out, g_post, g_pre):
    T, D = x.shape
    tr = 512

    def body(x_ref, m_ref, w_ref, gp_ref, gq_ref, mix_ref, h1_ref, u2_ref):
        mx = _dot(m_ref[...], w_ref[...])
        mix_ref[...] = mx
        r = lax.rsqrt(jnp.mean(mx * mx, axis=-1, keepdims=True) + EPS)
        h1 = x_ref[...] + (mx * r) * gp_ref[...]
        h1_ref[...] = h1
        r2 = lax.rsqrt(jnp.mean(h1 * h1, axis=-1, keepdims=True) + EPS)
        u2_ref[...] = ((h1 * r2) * gq_ref[...]).astype(u2_ref.dtype)

    row = pl.BlockSpec((tr, D), lambda i: (i, 0))
    vec = pl.BlockSpec((1, D), lambda i: (0, 0))
    f32 = jax.ShapeDtypeStruct((T, D), F32)
    return pl.pallas_call(
        body,
        grid=(T // tr,),
        in_specs=[row, row, pl.BlockSpec((D, D), lambda i: (0, 0)), vec, vec],
        out_specs=[row, row, row],
        out_shape=[f32, f32, jax.ShapeDtypeStruct((T, D), BF16)],
        compiler_params=_params(),
        name="post_pre",
    )(x, merged, w_out, g_post, g_pre)


def _ffn_up(u2, w_gate_t, w_up_t):
    T, D = u2.shape
    F = D_FF
    tm, tn = 512, F // 2

    def body(u_ref, wg_ref, wu_ref, a_ref, zg_ref, zu_ref):
        u = u_ref[...]
        zg = _dot_nt(u, wg_ref[...])
        zu = _dot_nt(u, wu_ref[...])
        a_ref[...] = (zg * jax.nn.sigmoid(zg) * zu).astype(a_ref.dtype)
        zg_ref[...] = zg.astype(zg_ref.dtype)
        zu_ref[...] = zu.astype(zu_ref.dtype)

    o_spec = pl.BlockSpec((tm, tn), lambda n, m: (m, n))
    out = jax.ShapeDtypeStruct((T, F), BF16)
    return pl.pallas_call(
        body,
        grid=(F // tn, T // tm),
        in_specs=[pl.BlockSpec((tm, D), lambda n, m: (m, 0)),
                  pl.BlockSpec((tn, D), lambda n, m: (n, 0)),
                  pl.BlockSpec((tn, D), lambda n, m: (n, 0))],
        out_specs=[o_spec] * 3,
        out_shape=[out] * 3,
        compiler_params=_params(),
        name="ffn_up",
    )(u2, w_gate_t, w_up_t)


def _loss_head(a, w_down, h1, target, g_post):
    T, D = h1.shape
    F = a.shape[1]
    tr = 512

    def body(a_ref, w_ref, h1_ref, t_ref, g_ref, dffn_ref, dh2_ref, loss_ref, dg_ref):
        i = pl.program_id(0)
        f = _dot(a_ref[...], w_ref[...])
        g = g_ref[...]
        r = lax.rsqrt(jnp.mean(f * f, axis=-1, keepdims=True) + EPS)
        xh = f * r
        err = (h1_ref[...] + xh * g) - t_ref[...]
        part = 0.5 * jnp.sum(jnp.mean(err * err, axis=-1, keepdims=True), axis=0, keepdims=True)
        dh2 = err * (1.0 / D)
        dh2_ref[...] = dh2
        dgp = jnp.sum(dh2 * xh, axis=0, keepdims=True)
        dxh = dh2 * g
        dffn_ref[...] = (r * (dxh - xh * jnp.mean(dxh * xh, axis=-1, keepdims=True))).astype(dffn_ref.dtype)

        @pl.when(i == 0)
        def _():
            loss_ref[...] = jnp.zeros_like(loss_ref)
            dg_ref[...] = jnp.zeros_like(dg_ref)

        loss_ref[...] += jnp.broadcast_to(part, loss_ref.shape)
        dg_ref[...] += dgp

    row = pl.BlockSpec((tr, D), lambda i: (i, 0))
    vec = pl.BlockSpec((1, D), lambda i: (0, 0))
    return pl.pallas_call(
        body,
        grid=(T // tr,),
        in_specs=[pl.BlockSpec((tr, F), lambda i: (i, 0)), pl.BlockSpec((F, D), lambda i: (0, 0)), row, row, vec],
        out_specs=[row, row, pl.BlockSpec((8, 128), lambda i: (0, 0)), vec],
        out_shape=[jax.ShapeDtypeStruct((T, D), BF16), jax.ShapeDtypeStruct((T, D), F32),
                   jax.ShapeDtypeStruct((8, 128), F32), jax.ShapeDtypeStruct((1, D), F32)],
        compiler_params=_params(),
        name="loss_head",
    )(a, w_down, h1, target, g_post)


def _ffn_down_bwd(dffn, wd, zg, zu):
    T, D = dffn.shape
    F = D_FF
    tm, tn = 512, F // 2

    def body(d_ref, w_ref, zg_ref, zu_ref, dzg_ref, dzu_ref):
        d = d_ref[...]
        for lo in range(0, tn, 512):
            cols = slice(lo, min(lo + 512, tn))
            da = _dot_nt(d, w_ref[cols, :])
            zg = zg_ref[:, cols].astype(F32)
            zu = zu_ref[:, cols].astype(F32)
            s = jax.nn.sigmoid(zg)
            dzu_ref[:, cols] = (da * (zg * s)).astype(dzu_ref.dtype)
            dzg_ref[:, cols] = (da * zu * (s * (1.0 + zg * (1.0 - s)))).astype(dzg_ref.dtype)

    z_spec = pl.BlockSpec((tm, tn), lambda n, m: (m, n))
    out = jax.ShapeDtypeStruct((T, F), BF16)
    return pl.pallas_call(
        body,
        grid=(F // tn, T // tm),
        in_specs=[pl.BlockSpec((tm, D), lambda n, m: (m, 0)), pl.BlockSpec((tn, D), lambda n, m: (n, 0)),
                  z_spec, z_spec],
        out_specs=[z_spec, z_spec],
        out_shape=[out, out],
        compiler_params=_params(),
        name="ffn_down_bwd",
    )(dffn, wd, zg, zu)


def _mid_bwd(h1, du2, dh2, mix, g_pre, g_post):
    T, D = h1.shape
    tr = 512

    def body(h1_ref, du2_ref, dh2_ref, mix_ref, gq_ref, gp_ref, dh1_ref, dmix_ref, dgq_ref, dgp_ref):
        i = pl.program_id(0)
        dx, dgq = _rms_bwd_terms(h1_ref[...], gq_ref[...], du2_ref[...])
        dh1 = dh2_ref[...] + dx
        dh1_ref[...] = dh1
        dmix, dgp = _rms_bwd_terms(mix_ref[...], gp_ref[...], dh1)
        dmix_ref[...] = dmix.astype(dmix_ref.dtype)

        @pl.when(i == 0)
        def _():
            dgq_ref[...] = jnp.zeros_like(dgq_ref)
            dgp_ref[...] = jnp.zeros_like(dgp_ref)

        dgq_ref[...] += dgq
        dgp_ref[...] += dgp

    row = pl.BlockSpec((tr, D), lambda i: (i, 0))
    vec = pl.BlockSpec((1, D), lambda i: (0, 0))
    return pl.pallas_call(
        body,
        grid=(T // tr,),
        in_specs=[row, row, row, row, vec, vec],
        out_specs=[row, row, vec, vec],
        out_shape=[jax.ShapeDtypeStruct((T, D), F32), jax.ShapeDtypeStruct((T, D), BF16),
                   jax.ShapeDtypeStruct((1, D), F32), jax.ShapeDtypeStruct((1, D), F32)],
        compiler_params=_params(),
        name="mid_bwd",
    )(h1, du2, dh2, mix, g_pre, g_post)


def _gate_bwd(dmix, w_out, ps, proj, wbs):
    T, D = dmix.shape
    tr, tc = min(1024, T), GATE_TC
    nc = D // tc
    gl0 = COL_GL * 128 // tc

    def body(dmix_ref, wo_ref, pa, pb, pc, g0, g1, g2, wa, wb, wc, dpa, dpb, dpc, dga, dgb, dgc, dya, dyb, dyc,
             acc_a, acc_b, acc_c):
        n = pl.program_id(1)
        dm = _dot_nt(dmix_ref[...], wo_ref[...])
        for p_ref, g_ref, w_ref, dp_ref, dg_ref, dy_ref, acc_ref in (
                (pa, g0, wa, dpa, dga, dya, acc_a), (pb, g1, wb, dpb, dgb, dyb, acc_b),
                (pc, g2, wc, dpc, dgc, dyc, acc_c)):
            s = jax.nn.sigmoid(g_ref[...].astype(F32))
            dp = (dm * s).astype(BF16)
            dp_ref[...] = dp
            dg_ref[...] = (dm * p_ref[...].astype(F32) * (s * (1.0 - s))).astype(dg_ref.dtype)
            part = _dot_nt(dp, w_ref[...])

            @pl.when(n == 0)
            def _():
                acc_ref[...] = part

            @pl.when(n > 0)
            def _():
                acc_ref[...] += part

            @pl.when(n == nc - 1)
            def _():
                dy_ref[...] = acc_ref[...].astype(dy_ref.dtype)

    col = pl.BlockSpec((tr, tc), lambda i, n: (i, n))
    y_spec = pl.BlockSpec((tr, 512), lambda i, n: (i, 0))
    w_spec = pl.BlockSpec((512, tc), lambda i, n: (0, n))
    gl_specs = [pl.BlockSpec((tr, tc), lambda i, n, j=j: (i, gl0 + j * nc + n)) for j in range(3)]
    big = jax.ShapeDtypeStruct((T, D), BF16)
    small = jax.ShapeDtypeStruct((T, 512), BF16)
    return pl.pallas_call(
        body,
        grid=(T // tr, nc),
        in_specs=[pl.BlockSpec((tr, D), lambda i, n: (i, 0)), pl.BlockSpec((tc, D), lambda i, n: (n, 0))]
        + [col] * 3 + gl_specs + [w_spec] * 3,
        out_specs=[col] * 6 + [y_spec] * 3,
        out_shape=[big] * 6 + [small] * 3,
        scratch_shapes=[pltpu.VMEM((tr, 512), F32)] * 3,
        compiler_params=_params(),
        name="gate_bwd",
    )(dmix, w_out, *ps, proj, proj, proj, *wbs)


def _pre_bwd(x, du, dh1, g):
    T, D = x.shape
    tr = 512

    def body(x_ref, du_ref, dh1_ref, g_ref, gx_ref, dg_ref):
        i = pl.program_id(0)
        dx, dg = _rms_bwd_terms(x_ref[...], g_ref[...], du_ref[...])
        gx_ref[...] = dh1_ref[...] + dx

        @pl.when(i == 0)
        def _():
            dg_ref[...] = jnp.zeros_like(dg_ref)

        dg_ref[...] += dg

    row = pl.BlockSpec((tr, D), lambda i: (i, 0))
    vec = pl.BlockSpec((1, D), lambda i: (0, 0))
    return pl.pallas_call(
        body,
        grid=(T // tr,),
        in_specs=[row, row, row, vec],
        out_specs=[row, vec],
        out_shape=[jax.ShapeDtypeStruct((T, D), F32), jax.ShapeDtypeStruct((1, D), F32)],
        compiler_params=_params(),
        name="pre_bwd",
    )(x, du, dh1, g)


def _gain_grad(xin, dy):
    T, D = xin.shape
    tr = min(512, T)

    def body(x_ref, dy_ref, dg_ref):
        i = pl.program_id(0)
        xf = x_ref[...]
        r = lax.rsqrt(jnp.mean(xf * xf, axis=-1, keepdims=True) + EPS)

        @pl.when(i == 0)
        def _():
            dg_ref[...] = jnp.zeros_like(dg_ref)

        dg_ref[...] += jnp.sum(dy_ref[...] * (xf * r), axis=0, keepdims=True)

    row = pl.BlockSpec((tr, D), lambda i: (i, 0))
    return pl.pallas_call(
        body,
        grid=(T // tr,),
        in_specs=[row, row],
        out_specs=pl.BlockSpec((1, D), lambda i: (0, 0)),
        out_shape=jax.ShapeDtypeStruct((1, D), F32),
        name="gain_grad",
    )(xin, dy)


def _swa_buckets():
    dist = (np.arange(BLOCK)[:, None] + BLOCK) - np.arange(2 * BLOCK)[None, :]
    max_exact = N_BUCKETS // 2
    d = np.maximum(dist, 0)
    df = np.maximum(d, 1).astype(np.float32)
    large = max_exact + (np.log(df / np.float32(max_exact)) / np.float32(math.log(MAX_DISTANCE / max_exact))
                         * np.float32(N_BUCKETS - max_exact)).astype(np.int32)
    large = np.minimum(large, N_BUCKETS - 1)
    bucket = np.where(d < max_exact, d, large)
    in_win = (dist >= 0) & (dist < SWA_WINDOW)
    return np.where(in_win, bucket, -1).astype(np.int32)


def _swa_bias_table(rel_bias, buckets):
    H = SWA_Q_HEADS

    def body(rb_ref, bk_ref, o_ref):
        bk = bk_ref[...]
        for h in range(H):
            acc = jnp.full(bk.shape, NEG, F32)
            for b in range(N_BUCKETS):
                acc = jnp.where(bk == b, rb_ref[b, h], acc)
            o_ref[h] = acc

    return pl.pallas_call(
        body,
        in_specs=[pl.BlockSpec(memory_space=pltpu.SMEM), pl.BlockSpec(memory_space=pltpu.VMEM)],
        out_specs=pl.BlockSpec(memory_space=pltpu.VMEM),
        out_shape=jax.ShapeDtypeStruct((H, BLOCK, 2 * BLOCK), F32),
        name="swa_bias_table",
    )(rel_bias, buckets)


def _swa_bias_grad(dbias, buckets):
    H = SWA_Q_HEADS

    def body(db_ref, bk_ref, o_ref):
        bk = bk_ref[...]
        rows = lax.broadcasted_iota(jnp.int32, (N_BUCKETS, 128), 0)
        lanes = lax.broadcasted_iota(jnp.int32, (N_BUCKETS, 128), 1)
        acc = jnp.zeros((N_BUCKETS, 128), F32)
        for h in range(H):
            d = db_ref[h]
            for b in range(N_BUCKETS):
                s = jnp.sum(jnp.sum(jnp.where(bk == b, d, 0.0), axis=1, keepdims=True), axis=0, keepdims=True)
                acc = jnp.where((rows == b) & (lanes == h), s, acc)
        o_ref[...] = acc

    return pl.pallas_call(
        body,
        in_specs=[pl.BlockSpec(memory_space=pltpu.VMEM)] * 2,
        out_specs=pl.BlockSpec(memory_space=pltpu.VMEM),
        out_shape=jax.ShapeDtypeStruct((N_BUCKETS, 128), F32),
        name="swa_bias_grad",
    )(dbias, buckets)


SWA_GROUP = 4


def _swa_stack(x, g):
    blocks = []
    for t in range(SWA_GROUP):
        h = SWA_GROUP * g + t
        xp = x[:, (h // 2) * BLOCK:(h // 2 + 1) * BLOCK]
        blocks.append(xp if h % 2 == g else pltpu.roll(xp, HEAD_DIM, 1))
    return jnp.concatenate(blocks, axis=0).astype(BF16)


def _swa_unstack(parts):
    pairs = []
    for p in range(4):
        g = p // 2
        halves = []
        for hh in range(2):
            t = 2 * p + hh - SWA_GROUP * g
            blk = parts[g][t * BLOCK:(t + 1) * BLOCK]
            halves.append(blk if hh == g else pltpu.roll(blk, HEAD_DIM, 1))
        pairs.append(halves[0] + halves[1])
    return jnp.concatenate(pairs, axis=1)


def _swa_group_inputs(g, sink_ref, bias_ref, kv_refs):
    lane = lax.broadcasted_iota(jnp.int32, (BLOCK, BLOCK), 1)
    km = (lane // HEAD_DIM) == g
    heads = range(SWA_GROUP * g, SWA_GROUP * (g + 1))
    bias = jnp.concatenate([bias_ref[h] for h in heads], axis=0)
    sink = jnp.concatenate([jnp.full((BLOCK, 1), sink_ref[0, h], F32) for h in heads], axis=0)
    return bias, sink, [jnp.where(km, r[...], 0) for r in kv_refs], km


def _swa_scores(qs, kp, kc, bias, sink, first):
    sp = _dot_nt(qs, kp) * SCALE64 + bias[:, :BLOCK]
    sp = jnp.where(first, NEG, sp)
    sc = _dot_nt(qs, kc) * SCALE64 + bias[:, BLOCK:]
    m = jnp.maximum(jnp.maximum(jnp.max(sp, axis=1, keepdims=True), jnp.max(sc, axis=1, keepdims=True)), sink)
    pp = jnp.exp(sp - m)
    pc = jnp.exp(sc - m)
    ps = jnp.exp(sink - m)
    den = jnp.sum(pp, axis=1, keepdims=True) + jnp.sum(pc, axis=1, keepdims=True) + ps
    return pp / den, pc / den, ps / den


def _swa_fwd(proj, sinks, bias_tab, B, S, hosted=None):
    nb = S // BLOCK
    T = B * S

    def body(sink_ref, q_ref, kp_ref, kc_ref, vp_ref, vc_ref, bias_ref, o_ref):
        i = pl.program_id(1)
        first = jnp.full((SWA_GROUP * BLOCK, BLOCK), i, jnp.int32) == 0
        q = q_ref[...].astype(F32)
        outs = []
        for g in range(2):
            bias, sink, (kp, kc, vp, vc), _ = _swa_group_inputs(g, sink_ref, bias_ref, (kp_ref, kc_ref, vp_ref, vc_ref))
            wp, wc, _ = _swa_scores(_swa_stack(q, g), kp, kc, bias, sink, first)
            outs.append(_dot(wp.astype(BF16), vp) + _dot(wc.astype(BF16), vc))
        o_ref[...] = _swa_unstack(outs).astype(o_ref.dtype)

    blk = (BLOCK, BLOCK)
    wide = (BLOCK, 4 * BLOCK)
    outs, moved = _hosted_call(
        body, (B, nb),
        [pl.BlockSpec(memory_space=pltpu.SMEM),
         pl.BlockSpec(wide, lambda b, i: (b * nb + i, COL_QA // 4)),
         pl.BlockSpec(blk, lambda b, i: (b * nb + jnp.maximum(i - 1, 0), COL_KA)),
         pl.BlockSpec(blk, lambda b, i: (b * nb + i, COL_KA)),
         pl.BlockSpec(blk, lambda b, i: (b * nb + jnp.maximum(i - 1, 0), COL_VA)),
         pl.BlockSpec(blk, lambda b, i: (b * nb + i, COL_VA)),
         pl.BlockSpec((SWA_Q_HEADS, BLOCK, 2 * BLOCK), lambda b, i: (0, 0, 0))],
        [pl.BlockSpec(wide, lambda b, i: (b * nb + i, 0))],
        [jax.ShapeDtypeStruct((T, 512), BF16)], [], hosted, "swa_fwd",
        (sinks, proj, proj, proj, proj, proj, bias_tab))
    return outs[0], moved


def _swa_bwd(proj, dy, sinks, bias_tab, B, S):
    nb = S // BLOCK
    T = B * S
    H = SWA_Q_HEADS

    def body(sink_ref, q_ref, kp_ref, kc_ref, vp_ref, vc_ref, do_ref, bias_ref,
             dq_ref, dk_ref, dv_ref, dbias_ref, dsink_ref):
        b = pl.program_id(0)
        i = pl.program_id(1)

        @pl.when((b == 0) & (i == 0))
        def _():
            dbias_ref[...] = jnp.zeros_like(dbias_ref)
            dsink_ref[...] = jnp.zeros_like(dsink_ref)

        @pl.when(i == 0)
        def _():
            dk_ref[...] = jnp.zeros_like(dk_ref)
            dv_ref[...] = jnp.zeros_like(dv_ref)

        first = jnp.full((SWA_GROUP * BLOCK, BLOCK), i, jnp.int32) == 0
        q = q_ref[...].astype(F32)
        do = do_ref[...].astype(F32)
        dqs, dsinks, dbs, dkp, dkc, dvp, dvc = [], [], [], [], [], [], []
        for g in range(2):
            bias, sink, (kp, kc, vp, vc), km = _swa_group_inputs(
                g, sink_ref, bias_ref, (kp_ref, kc_ref, vp_ref, vc_ref))
            qs = _swa_stack(q, g)
            dos = _swa_stack(do, g)
            wp, wc, ws = _swa_scores(qs, kp, kc, bias, sink, first)
            dwp = _dot_nt(dos, vp)
            dwc = _dot_nt(dos, vc)
            dsum = jnp.sum(wp * dwp, axis=1, keepdims=True) + jnp.sum(wc * dwc, axis=1, keepdims=True)
            dsp = wp * (dwp - dsum)
            dsc = wc * (dwc - dsum)
            dsk = -ws * dsum
            for t in range(SWA_GROUP):
                rows = slice(t * BLOCK, (t + 1) * BLOCK)
                dsinks.append(jnp.broadcast_to(jnp.sum(dsk[rows], axis=0, keepdims=True), (1, 128)))
                dbs.append(jnp.concatenate([dsp[rows], dsc[rows]], axis=1))
            dspb = dsp.astype(BF16)
            dscb = dsc.astype(BF16)
            dqs.append(_dot(dspb, kp) + _dot(dscb, kc))
            dkp.append(jnp.where(km, _dot_tn(dspb, qs), 0.0))
            dkc.append(jnp.where(km, _dot_tn(dscb, qs), 0.0))
            dvp.append(jnp.where(km, _dot_tn(wp.astype(BF16), dos), 0.0))
            dvc.append(jnp.where(km, _dot_tn(wc.astype(BF16), dos), 0.0))
        dq_ref[...] = (_swa_unstack(dqs) * SCALE64).astype(dq_ref.dtype)
        dsink_ref[...] += jnp.concatenate(dsinks, axis=0)
        for h in range(H):
            dbias_ref[h] += dbs[h]
        cur = pl.ds(pl.multiple_of(i * BLOCK, BLOCK), BLOCK)
        prev = pl.ds(pl.multiple_of(jnp.maximum(i - 1, 0) * BLOCK, BLOCK), BLOCK)
        dk_ref[prev, :] += (dkp[0] + dkp[1]) * SCALE64
        dk_ref[cur, :] += (dkc[0] + dkc[1]) * SCALE64
        dv_ref[prev, :] += dvp[0] + dvp[1]
        dv_ref[cur, :] += dvc[0] + dvc[1]

    blk = (BLOCK, BLOCK)
    wide = (BLOCK, 4 * BLOCK)
    kv_out = pl.BlockSpec((S, BLOCK), lambda b, i: (b, 0))
    full_bias = pl.BlockSpec((H, BLOCK, 2 * BLOCK), lambda b, i: (0, 0, 0))
    return pl.pallas_call(
        body,
        grid=(B, nb),
        in_specs=[pl.BlockSpec(memory_space=pltpu.SMEM),
                  pl.BlockSpec(wide, lambda b, i: (b * nb + i, COL_QA // 4)),
                  pl.BlockSpec(blk, lambda b, i: (b * nb + jnp.maximum(i - 1, 0), COL_KA)),
                  pl.BlockSpec(blk, lambda b, i: (b * nb + i, COL_KA)),
                  pl.BlockSpec(blk, lambda b, i: (b * nb + jnp.maximum(i - 1, 0), COL_VA)),
                  pl.BlockSpec(blk, lambda b, i: (b * nb + i, COL_VA)),
                  pl.BlockSpec(wide, lambda b, i: (b * nb + i, 0)),
                  full_bias],
        out_specs=[pl.BlockSpec(wide, lambda b, i: (b * nb + i, 0)),
                   kv_out, kv_out, full_bias,
                   pl.BlockSpec((H, 128), lambda b, i: (0, 0))],
        out_shape=[jax.ShapeDtypeStruct((T, 512), BF16),
                   jax.ShapeDtypeStruct((T, BLOCK), F32), jax.ShapeDtypeStruct((T, BLOCK), F32),
                   jax.ShapeDtypeStruct((H, BLOCK, 2 * BLOCK), F32), jax.ShapeDtypeStruct((H, 128), F32)],
        compiler_params=_params(),
        name="swa_bwd",
    )(sinks, proj, proj, proj, proj, proj, dy, bias_tab)


SB_TILE = 256


SB_HEADS = 4
SB_LANES = SB_HEADS * HEAD_DIM
SB_ROWS = SB_HEADS * SB_TILE


def _sb_logits(z, tri):
    sp = jnp.log(1.0 + jnp.exp(-jnp.abs(z)))
    ls = jnp.minimum(z, 0.0) - sp
    l1m = ls - z
    if tri is not None:
        l1m = jnp.where(tri, l1m, 0.0)
    return ls, l1m


def _sb_masks():
    lane = lax.broadcasted_iota(jnp.int32, (SB_TILE, SB_LANES), 1)
    hm = [(lane // HEAD_DIM) == h for h in range(SB_HEADS)]
    row = lax.broadcasted_iota(jnp.int32, (SB_ROWS, SB_TILE), 0) % SB_TILE
    col = lax.broadcasted_iota(jnp.int32, (SB_ROWS, SB_TILE), 1)
    return lane, hm, row, col


def _sb_stack(x, hm):
    return jnp.concatenate([jnp.where(m, x, 0) for m in hm], axis=0)


def _sb_unstack(x, hm):
    return sum(jnp.where(m, x[h * SB_TILE:(h + 1) * SB_TILE], 0.0) for h, m in enumerate(hm))


def _sb_fwd(proj, B, S, hosted=None):
    nt = S // SB_TILE
    T = B * S

    ng = 512 // SB_LANES

    def body(*refs):
        q_refs, k_refs, v_refs = refs[:ng], refs[ng:2 * ng], refs[2 * ng:3 * ng]
        o_ref, r_ref = refs[3 * ng:]
        i = pl.program_id(1)
        _, hm, row, col = _sb_masks()
        tri = col < row
        later = jnp.concatenate([(row[:SB_TILE] > col[:SB_TILE]).astype(BF16)] * 2, axis=0)
        qs = [_sb_stack(q_ref[...] * SCALE64, hm) for q_ref in q_refs]

        def tile(j, carry, mask):
            rows = pl.ds(pl.multiple_of(j * SB_TILE, SB_TILE), SB_TILE)
            out = []
            for g in range(ng):
                acc, c = carry[g]
                ls, l1m = _sb_logits(_dot_nt(qs[g], k_refs[g][rows, :]), mask)
                a = jnp.exp(ls + c + _dot_split(l1m, later))
                if mask is not None:
                    a = jnp.where(mask, a, 0.0)
                pv = _dot(a.astype(BF16), v_refs[g][rows, :])
                out.append((acc + _sb_unstack(pv, hm), c + jnp.sum(l1m, axis=1, keepdims=True)))
            return tuple(out)

        zero = (jnp.zeros((SB_TILE, SB_LANES), F32), jnp.zeros((SB_ROWS, 1), F32))
        carry = tile(i, (zero,) * ng, tri)
        carry = lax.fori_loop(0, i, lambda it, cr: tile(i - 1 - it, cr, None), carry)
        o_ref[...] = jnp.concatenate([acc for acc, _ in carry], axis=1).astype(o_ref.dtype)
        r_ref[...] = jnp.concatenate(
            [_sb_unstack(jnp.broadcast_to(c, (SB_ROWS, SB_LANES)), hm) for _, c in carry], axis=1)

    blk = (SB_TILE, SB_LANES)
    cq, ck, cv = (c * BLOCK // SB_LANES for c in (COL_QB, COL_KB, COL_VB))
    wide = pl.BlockSpec((SB_TILE, 512), lambda b, i: (b * nt + i, 0))
    outs, moved = _hosted_call(
        body, (B, nt),
        [pl.BlockSpec(blk, lambda b, i, g=g: (b * nt + i, cq + g)) for g in range(ng)]
        + [pl.BlockSpec((S, SB_LANES), lambda b, i, g=g: (b, ck + g)) for g in range(ng)]
        + [pl.BlockSpec((S, SB_LANES), lambda b, i, g=g: (b, cv + g)) for g in range(ng)],
        [wide, wide],
        [jax.ShapeDtypeStruct((T, 512), BF16), jax.ShapeDtypeStruct((T, 512), F32)], [], hosted, "sb_fwd",
        (proj,) * (3 * ng))
    return outs[0], outs[1], moved


def _sb_bwd(proj, dy, rtot, B, S, hosted=None):
    nt = S // SB_TILE
    T = B * S

    ng = 512 // SB_LANES

    def body(*refs):
        q_refs, k_refs, v_refs = refs[:ng], refs[ng:2 * ng], refs[2 * ng:3 * ng]
        do_ref, r_ref, dq_ref, dk_ref, dv_ref = refs[3 * ng:]
        i = pl.program_id(1)

        @pl.when(i == 0)
        def _():
            dk_ref[...] = jnp.zeros_like(dk_ref)
            dv_ref[...] = jnp.zeros_like(dv_ref)

        lane, hm, row, col = _sb_masks()
        tri = col < row
        later = jnp.concatenate([(row[:SB_TILE] > col[:SB_TILE]).astype(BF16)] * 2, axis=0)
        earlier = (row[:SB_TILE] < col[:SB_TILE]).astype(BF16)
        qs, dos, rs = [], [], []
        for g in range(ng):
            lanes = slice(g * SB_LANES, (g + 1) * SB_LANES)
            qs.append(_sb_stack(q_refs[g][...] * SCALE64, hm))
            dos.append(_sb_stack(do_ref[:, lanes], hm))
            r = r_ref[:, lanes]
            rs.append(jnp.concatenate([jnp.sum(jnp.where(lane == h * HEAD_DIM, r, 0.0), axis=1, keepdims=True)
                                       for h in range(SB_HEADS)], axis=0))

        def tile(j, carry, mask):
            rows = pl.ds(pl.multiple_of(j * SB_TILE, SB_TILE), SB_TILE)
            out = []
            for g in range(ng):
                lanes = slice(g * SB_LANES, (g + 1) * SB_LANES)
                dq, lsum, psum = carry[g]
                kj = k_refs[g][rows, :]
                vj = v_refs[g][rows, :]
                ls, l1m = _sb_logits(_dot_nt(qs[g], kj), mask)
                sig = jnp.exp(ls)
                lsum = lsum + jnp.sum(l1m, axis=1, keepdims=True)
                a = jnp.exp(ls + (rs[g] - lsum) + _dot_split(l1m, later))
                if mask is not None:
                    a = jnp.where(mask, a, 0.0)
                de = _dot_nt(dos[g], vj) * a
                pre = psum + _dot(de.astype(BF16), earlier)
                dz = de * (1.0 - sig) - sig * pre
                if mask is not None:
                    dz = jnp.where(mask, dz, 0.0)
                dz = dz.astype(BF16)
                dk_ref[rows, lanes] += _dot_tn(dz, qs[g])
                dv_ref[rows, lanes] += _dot_tn(a.astype(BF16), dos[g])
                out.append((dq + _sb_unstack(_dot(dz, kj), hm), lsum, psum + jnp.sum(de, axis=1, keepdims=True)))
            return tuple(out)

        zero = jnp.zeros((SB_ROWS, 1), F32)
        init = ((jnp.zeros((SB_TILE, SB_LANES), F32), zero, zero),) * ng
        carry = lax.fori_loop(0, i, lambda j, c: tile(j, c, None), init)
        carry = tile(i, carry, tri)
        dq_ref[...] = (jnp.concatenate([c[0] for c in carry], axis=1) * SCALE64).astype(dq_ref.dtype)

    blk = (SB_TILE, SB_LANES)
    cq, ck, cv = (c * BLOCK // SB_LANES for c in (COL_QB, COL_KB, COL_VB))
    wide = pl.BlockSpec((SB_TILE, 512), lambda b, i: (b * nt + i, 0))
    kv_out = pl.BlockSpec((S, 512), lambda b, i: (b, 0))
    outs, moved = _hosted_call(
        body, (B, nt),
        [pl.BlockSpec(blk, lambda b, i, g=g: (b * nt + i, cq + g)) for g in range(ng)]
        + [pl.BlockSpec((S, SB_LANES), lambda b, i, g=g: (b, ck + g)) for g in range(ng)]
        + [pl.BlockSpec((S, SB_LANES), lambda b, i, g=g: (b, cv + g)) for g in range(ng)]
        + [wide, wide],
        [wide, kv_out, kv_out],
        [jax.ShapeDtypeStruct((T, 512), BF16),
         jax.ShapeDtypeStruct((T, 512), F32), jax.ShapeDtypeStruct((T, 512), F32)], [], hosted, "sb_bwd",
        (proj,) * (3 * ng) + (dy, rtot))
    return outs[0], outs[1], outs[2], moved


def _mem_weights(q, mk):
    z = _dot_nt(q, mk) * SCALE128
    e = jnp.exp(z - jnp.max(z, axis=1, keepdims=True))
    return e / jnp.sum(e, axis=1, keepdims=True)


def _mem_fwd(proj, mkv, B, S, M):
    tq = 512
    nq = S // tq
    T = B * S
    Hm = MEM_HEADS

    def body(q0, q1, q2, q3, mk_ref, mv_ref, o_ref):
        outs = []
        for h, q_ref in enumerate((q0, q1, q2, q3)):
            cols = slice(h * 128, (h + 1) * 128)
            w = _mem_weights(q_ref[...], mk_ref[:, cols])
            outs.append(_dot(w.astype(BF16), mv_ref[:, cols]))
        o_ref[...] = jnp.concatenate(outs, axis=1).astype(o_ref.dtype)

    return pl.pallas_call(
        body,
        grid=(B, nq),
        in_specs=[pl.BlockSpec((tq, 128), lambda b, i, h=h: (b * nq + i, COL_QM + h)) for h in range(Hm)]
        + [pl.BlockSpec((M, 512), lambda b, i: (b, 0)), pl.BlockSpec((M, 512), lambda b, i: (b, 1))],
        out_specs=pl.BlockSpec((tq, 512), lambda b, i: (b * nq + i, 0)),
        out_shape=jax.ShapeDtypeStruct((T, 512), BF16),
        name="mem_fwd",
    )(proj, proj, proj, proj, mkv, mkv)


def _mem_bwd(proj, mkv, dy, B, S, M):
    tq = 512
    nq = S // tq
    T = B * S
    Hm = MEM_HEADS

    def body(q0, q1, q2, q3, mk_ref, mv_ref, do_ref, dq_ref, dmk_ref, dmv_ref):
        i = pl.program_id(1)
        dqs, dmks, dmvs = [], [], []
        for h, q_ref in enumerate((q0, q1, q2, q3)):
            cols = slice(h * 128, (h + 1) * 128)
            q = q_ref[...]
            do = do_ref[:, cols]
            mk = mk_ref[:, cols]
            w = _mem_weights(q, mk)
            dw = _dot_nt(do, mv_ref[:, cols])
            ds = (w * (dw - jnp.sum(w * dw, axis=1, keepdims=True))).astype(BF16)
            dqs.append(_dot(ds, mk))
            dmks.append(_dot_tn(ds, q))
            dmvs.append(_dot_tn(w.astype(BF16), do))
        dq_ref[...] = (jnp.concatenate(dqs, axis=1) * SCALE128).astype(dq_ref.dtype)

        @pl.when(i == 0)
        def _():
            dmk_ref[...] = jnp.zeros_like(dmk_ref)
            dmv_ref[...] = jnp.zeros_like(dmv_ref)

        dmk_ref[...] += jnp.concatenate(dmks, axis=1) * SCALE128
        dmv_ref[...] += jnp.concatenate(dmvs, axis=1)

    q_spec = pl.BlockSpec((tq, 512), lambda b, i: (b * nq + i, 0))
    m_out = pl.BlockSpec((M, 512), lambda b, i: (b, 0))
    return pl.pallas_call(
        body,
        grid=(B, nq),
        in_specs=[pl.BlockSpec((tq, 128), lambda b, i, h=h: (b * nq + i, COL_QM + h)) for h in range(Hm)]
        + [pl.BlockSpec((M, 512), lambda b, i: (b, 0)), pl.BlockSpec((M, 512), lambda b, i: (b, 1)), q_spec],
        out_specs=[q_spec, m_out, m_out],
        out_shape=[jax.ShapeDtypeStruct((T, 512), BF16),
                   jax.ShapeDtypeStruct((B * M, 512), F32), jax.ShapeDtypeStruct((B * M, 512), F32)],
        name="mem_bwd",
    )(proj, proj, proj, proj, mkv, mkv, dy)


def _all_gather(blk, name):
    R, C = blk.shape

    def body(x_ref, out_ref, send_sems, recv_sems, local_sem):
        x, y, c = _mesh_pos()
        me, sibling = (x, y, c), (x, y, 1 - c)
        chips = [(1 - x, y), (x, 1 - y), (1 - x, 1 - y)]

        def slot(px, py, pc):
            return out_ref.at[4 * px + 2 * py + pc]

        def copy(k, block, to, src=None):
            return pltpu.make_async_remote_copy(
                src_ref=slot(*block) if src is None else src, dst_ref=slot(*block),
                send_sem=send_sems.at[k], recv_sem=recv_sems.at[k],
                device_id=to, device_id_type=pl.DeviceIdType.MESH)

        mine = pltpu.make_async_copy(x_ref, slot(*me), local_sem)
        mine.start()
        first = [copy(0, me, sibling, src=x_ref)]
        first += [copy(1 + j, me, (*chip, c), src=x_ref) for j, chip in enumerate(chips)]
        for cp in first:
            cp.start()
        passed = [copy(4 + j, (*chip, c), sibling) for j, chip in enumerate(chips)]
        for j, chip in enumerate(chips):
            copy(1 + j, (*chip, c), me).wait_recv()
            passed[j].start()
        copy(0, sibling, me).wait_recv()
        for j, chip in enumerate(chips):
            copy(4 + j, (*chip, 1 - c), me).wait_recv()
        for cp in first + passed:
            cp.wait_send()
        mine.wait()

    return pl.pallas_call(
        body,
        in_specs=[pl.BlockSpec(memory_space=pl.ANY)],
        out_specs=pl.BlockSpec(memory_space=pl.ANY),
        out_shape=jax.ShapeDtypeStruct((N_DEV, R, C), blk.dtype),
        scratch_shapes=[pltpu.SemaphoreType.DMA((7,)), pltpu.SemaphoreType.DMA((7,)), pltpu.SemaphoreType.DMA],
        name=name,
    )(blk)


_HBM = pl.BlockSpec(memory_space=pltpu.HBM)
_SEM = pl.BlockSpec(memory_space=pltpu.SEMAPHORE)


def _scatter_start(parts, land, window, carried, name):
    hosted = _Hosted(scatters=[parts], window=window)

    def body(p_ref, land_ref, c_ref, send_sems, recv_sems, local_sems, p_thru, land_thru, c_thru):
        for cp in hosted.copies([p_ref], [land_ref], send_sems, recv_sems, local_sems):
            cp.start()

    sems = (pltpu.SemaphoreType.DMA((7,)), pltpu.SemaphoreType.DMA((7,)), pltpu.SemaphoreType.DMA((1,)))
    hbm = lambda a: pltpu.HBM(a.shape, a.dtype)
    outs = pl.pallas_call(
        body, name=name,
        out_shape=sems + (hbm(parts), hbm(land), hbm(carried)),
        in_specs=(_HBM, _HBM, _HBM), out_specs=(_SEM, _SEM, _SEM, _HBM, _HBM, _HBM),
        input_output_aliases={0: 3, 1: 4, 2: 5},
        compiler_params=pltpu.CompilerParams(has_side_effects=pltpu.SideEffectType.DATAFLOW_SIDE_EFFECTING),
    )(pltpu.with_memory_space_constraint(parts, pltpu.HBM),
      pltpu.with_memory_space_constraint(land, pltpu.HBM),
      pltpu.with_memory_space_constraint(carried, pltpu.HBM))
    return (outs[:4], window), outs[4], outs[5]


def _scatter_wait(flight, land, after, name):
    (send_sems, recv_sems, local_sems, p_thru), window = flight
    hosted = _Hosted(scatters=[p_thru], window=window)

    def body(p_ref, land_ref, send, recv, local, after_ref, p_dead, got_ref):
        for cp in hosted.copies([p_ref], [land_ref], send, recv, local):
            cp.wait()

    hbm = lambda a: pltpu.HBM(a.shape, a.dtype)
    return pl.pallas_call(
        body, name=name,
        out_shape=(hbm(p_thru), hbm(land)),
        in_specs=(_HBM, _HBM, _SEM, _SEM, _SEM, pl.BlockSpec(memory_space=pl.ANY)), out_specs=(_HBM, _HBM),
        input_output_aliases={0: 0, 1: 1},
        compiler_params=pltpu.CompilerParams(has_side_effects=pltpu.SideEffectType.DATAFLOW_SIDE_EFFECTING),
    )(p_thru, land, send_sems, recv_sems, local_sems, after)[1]


def _adamw(parts, w, m, v, name):
    R, C = w.shape
    tr = R
    for cand in (368, 352, 256, 176, 128, 64, 32, 16, 8):
        if R % cand == 0 and cand * C * 4 <= 1536 * 1024:
            tr = cand
            break
    c1 = 1.0 - ADAM_B1 ** ADAM_STEP
    c2 = 1.0 - ADAM_B2 ** ADAM_STEP

    def body(p_ref, w_ref, m_ref, v_ref, g_ref, d_ref, nm_ref, nv_ref):
        g = p_ref[0].astype(F32)
        for d in range(1, N_DEV):
            g = g + p_ref[d].astype(F32)
        nm = ADAM_B1 * m_ref[...] + (1.0 - ADAM_B1) * g
        nv = ADAM_B2 * v_ref[...] + (1.0 - ADAM_B2) * (g * g)
        g_ref[...] = g
        nm_ref[...] = nm
        nv_ref[...] = nv
        d_ref[...] = -ADAM_LR * ((nm / c1) / (jnp.sqrt(nv / c2) + ADAM_EPS) + ADAM_WD * w_ref[...])

    row = pl.BlockSpec((tr, C), lambda i: (i, 0))
    out = jax.ShapeDtypeStruct((R, C), F32)
    return pl.pallas_call(
        body,
        grid=(R // tr,),
        in_specs=[pl.BlockSpec((N_DEV, tr, C), lambda i: (0, i, 0)), row, row, row],
        out_specs=[row] * 4,
        out_shape=[out] * 4,
        compiler_params=_params(),
        name=name,
    )(parts, w, m, v)


def _col_shards(g):
    R, C8 = g.shape
    return g.reshape(R, N_DEV, C8 // N_DEV).transpose(1, 0, 2)


def _row_shards(g):
    R8, C = g.shape
    return g.reshape(N_DEV, R8 // N_DEV, C)


def _cols_full(gathered):
    n, R, C = gathered.shape
    return gathered.transpose(1, 0, 2).reshape(R, n * C)


_BIG = ("w_in", "w_mem_kv", "w_branch_swa", "w_branch_sb", "w_branch_mem", "w_out", "w_gate", "w_up", "w_down")
_COL_SHARDED = ("w_branch_swa", "w_branch_sb", "w_branch_mem")
_TRANSPOSED = ("w_in", "w_gate", "w_up")
_SMALL = ("ln_mix_pre", "ln_mix_post", "swa_sinks", "rel_bias", "ln_mem", "ln_ffn_pre", "ln_ffn_post")
_ORDER = ("ln_mix_pre", "ln_mix_post", "w_in", "swa_sinks", "rel_bias", "ln_mem", "w_mem_kv", "w_branch_swa",
          "w_branch_sb", "w_branch_mem", "w_out", "ln_ffn_pre", "ln_ffn_post", "w_gate", "w_up", "w_down")


def _pack_small(d, last=None):
    rows = [d["ln_mix_pre"], d["ln_mix_post"], d["ln_mem"], d["ln_ffn_pre"], d["ln_ffn_post"],
            jnp.pad(d["swa_sinks"].reshape(1, -1), ((0, 0), (0, D_MODEL - SWA_Q_HEADS))),
            jnp.pad(d["rel_bias"].reshape(1, -1), ((0, 0), (0, D_MODEL - N_BUCKETS * SWA_Q_HEADS))),
            jnp.zeros((1, D_MODEL), F32) if last is None else last]
    return jnp.concatenate([r.astype(F32) for r in rows], axis=0)


def _unpack_small(a):
    return dict(ln_mix_pre=a[0:1], ln_mix_post=a[1:2], ln_mem=a[2:3], ln_ffn_pre=a[3:4], ln_ffn_post=a[4:5],
                swa_sinks=a[5:6, :SWA_Q_HEADS],
                rel_bias=a[6, :N_BUCKETS * SWA_Q_HEADS].reshape(N_BUCKETS, SWA_Q_HEADS))


def kernel(x, mem, ln_mix_pre, ln_mix_post, w_in, swa_sinks, rel_bias, ln_mem, w_mem_kv, w_branch_swa, w_branch_sb, w_branch_mem, w_out, ln_ffn_pre, ln_ffn_post, w_gate, w_up, w_down, loss_target, m_ln_mix_pre, m_ln_mix_post, m_w_in, m_swa_sinks, m_rel_bias, m_ln_mem, m_w_mem_kv, m_w_branch_swa, m_w_branch_sb, m_w_branch_mem, m_w_out, m_ln_ffn_pre, m_ln_ffn_post, m_w_gate, m_w_up, m_w_down, v_ln_mix_pre, v_ln_mix_post, v_w_in, v_swa_sinks, v_rel_bias, v_ln_mem, v_w_mem_kv, v_w_branch_swa, v_w_branch_sb, v_w_branch_mem, v_w_out, v_ln_ffn_pre, v_ln_ffn_post, v_w_gate, v_w_up, v_w_down):
    w = dict(ln_mix_pre=ln_mix_pre, ln_mix_post=ln_mix_post, w_in=w_in[0], swa_sinks=swa_sinks, rel_bias=rel_bias,
             ln_mem=ln_mem, w_mem_kv=w_mem_kv[0], w_branch_swa=w_branch_swa[0], w_branch_sb=w_branch_sb[0],
             w_branch_mem=w_branch_mem[0], w_out=w_out[0], ln_ffn_pre=ln_ffn_pre, ln_ffn_post=ln_ffn_post,
             w_gate=w_gate[0], w_up=w_up[0], w_down=w_down[0])
    mom = dict(ln_mix_pre=m_ln_mix_pre, ln_mix_post=m_ln_mix_post, w_in=m_w_in[0], swa_sinks=m_swa_sinks,
               rel_bias=m_rel_bias, ln_mem=m_ln_mem, w_mem_kv=m_w_mem_kv[0], w_branch_swa=m_w_branch_swa[0],
               w_branch_sb=m_w_branch_sb[0], w_branch_mem=m_w_branch_mem[0], w_out=m_w_out[0],
               ln_ffn_pre=m_ln_ffn_pre, ln_ffn_post=m_ln_ffn_post, w_gate=m_w_gate[0], w_up=m_w_up[0],
               w_down=m_w_down[0])
    var = dict(ln_mix_pre=v_ln_mix_pre, ln_mix_post=v_ln_mix_post, w_in=v_w_in[0], swa_sinks=v_swa_sinks,
               rel_bias=v_rel_bias, ln_mem=v_ln_mem, w_mem_kv=v_w_mem_kv[0], w_branch_swa=v_w_branch_swa[0],
               w_branch_sb=v_w_branch_sb[0], w_branch_mem=v_w_branch_mem[0], w_out=v_w_out[0],
               ln_ffn_pre=v_ln_ffn_pre, ln_ffn_post=v_ln_ffn_post, w_gate=v_w_gate[0], w_up=v_w_up[0],
               w_down=v_w_down[0])
    B, S, D = x.shape
    M = mem.shape[1]
    T = B * S
    F = D_FF
    x2 = x.reshape(T, D)
    mem2 = mem.reshape(B * M, D)
    t2 = loss_target.reshape(T, D)
    buckets = jnp.asarray(_swa_buckets())
    for d in (w, mom, var):
        for n in _TRANSPOSED:
            d[n] = d[n].T
    wb = {n: w[n].astype(BF16) for n in _BIG}
    full = {}

    def landed(names, got):
        for n, g in zip(names, got):
            full[n] = _cols_full(g) if n in _COL_SHARDED else g.reshape(-1, g.shape[-1])

    def shards(n, g):
        return _col_shards(g) if n in _COL_SHARDED else _row_shards(g)

    landed(["w_in"], [_all_gather(wb["w_in"], "ag_w_in")])
    u = _rms_fwd(x2, ln_mix_pre, "rms_mix_pre")
    early = ["w_mem_kv", "w_branch_swa", "w_branch_sb", "w_branch_mem"]
    proj, got = _matmul([(u, full["w_in"])], "nt", BF16, 512, IN_W // 2, D, "proj_in",
                        hosted=_Hosted(gathers=[wb[n] for n in early]))
    landed(early, got)
    mn = _rms_fwd(mem2, ln_mem, "rms_mem")
    mkv = _matmul([(mn, full["w_mem_kv"])], "nn", BF16, 512, 1024, D, "proj_mem")
    bias_tab = _swa_bias_table(rel_bias, buckets)
    y_swa, got = _swa_fwd(proj, swa_sinks, bias_tab, B, S, hosted=_Hosted(gathers=[wb["w_out"]]))
    landed(["w_out"], got)
    late = ["w_gate", "w_up", "w_down"]
    y_sb, rtot, got = _sb_fwd(proj, B, S, hosted=_Hosted(gathers=[wb[n] for n in late]))
    landed(late, got)
    y_mem = _mem_fwd(proj, mkv, B, S, M)
    wbs = (full["w_branch_swa"], full["w_branch_sb"], full["w_branch_mem"])
    merged, p_swa, p_sb, p_mem = _branch_gate(proj, (y_swa, y_sb, y_mem), wbs)
    mix, h1, u2 = _post_pre(x2, merged, full["w_out"], ln_mix_post, ln_ffn_pre)
    a, zg, zu = _ffn_up(u2, full["w_gate"], full["w_up"])
    dffn, dh2, loss_tile, d_ln_ffn_post = _loss_head(a, full["w_down"], h1, t2, ln_ffn_post)

    part = {}
    part["w_down"] = _matmul([(a, dffn)], "tn", BF16, F // 2, 1024, 1024, "dw_down")
    dzg, dzu = _ffn_down_bwd(dffn, full["w_down"], zg, zu)
    part["w_gate"] = _matmul([(dzg, u2)], "tn", BF16, F // 2, 1024, 1024, "dw_gate")
    part["w_up"] = _matmul([(dzu, u2)], "tn", BF16, F // 2, 1024, 1024, "dw_up")
    du2 = _matmul([(dzg, full["w_gate"]), (dzu, full["w_up"])], "nn", F32, 512, 1024, F // 2, "d_u2")
    dh1, dmix, d_ln_ffn_pre, d_ln_mix_post = _mid_bwd(h1, du2, dh2, mix, ln_ffn_pre, ln_mix_post)
    part["w_out"] = _matmul([(merged, dmix)], "tn", BF16, 1024, 1024, 1024, "dw_out")
    (dp_swa, dp_sb, dp_mem, dg0, dg1, dg2, dy_swa, dy_sb, dy_mem) = _gate_bwd(
        dmix, full["w_out"], (p_swa, p_sb, p_mem), proj, wbs)
    part["w_branch_swa"] = _matmul([(y_swa, dp_swa)], "tn", BF16, 512, 1024, 1024, "dw_branch_swa")
    part["w_branch_sb"] = _matmul([(y_sb, dp_sb)], "tn", BF16, 512, 1024, 1024, "dw_branch_sb")
    part["w_branch_mem"] = _matmul([(y_mem, dp_mem)], "tn", BF16, 512, 1024, 1024, "dw_branch_mem")
    dqm, dmk, dmv = _mem_bwd(proj, mkv, dy_mem, B, S, M)
    dmkv = jnp.concatenate([dmk, dmv], axis=1).astype(BF16)
    part["w_mem_kv"] = _matmul([(mn, dmkv)], "tn", BF16, 1024, 1024, 512, "dw_mem_kv")
    dmn = _matmul([(dmkv, full["w_mem_kv"])], "nt", F32, 512, 1024, 1024, "d_mn")
    d_ln_mem = _gain_grad(mem2, dmn)
    dqa, dka, dva, dbias, dsink = _swa_bwd(proj, dy_swa, swa_sinks, bias_tab, B, S)
    behind_sb = ["w_down", "w_gate", "w_up", "w_out", "w_branch_swa", "w_branch_sb", "w_branch_mem", "w_mem_kv"]
    dqb, dkb, dvb, got = _sb_bwd(proj, dy_sb, rtot, B, S,
                                 hosted=_Hosted(scatters=[shards(n, part[n]) for n in behind_sb]))
    recv = dict(zip(behind_sb, got))
    d_rel_bias = _swa_bias_grad(dbias, buckets)[:, :SWA_Q_HEADS]
    d_sinks = dsink[:, 0].reshape(1, SWA_Q_HEADS)
    dproj = jnp.concatenate([dqa, dka.astype(BF16), dva.astype(BF16), dqb, dkb.astype(BF16), dvb.astype(BF16),
                             dqm, dg0, dg1, dg2], axis=1)
    half = D // 2
    land = lax.empty((N_DEV, IN_W // N_DEV, D), BF16)
    flights = []
    for t in range(2):
        dw_half = _matmul([(dproj, u)], "tn", BF16, IN_W // 2, half, 1024, "dw_in_%d" % t, n_cols=half, n_off=t)
        flight, land, dproj = _scatter_start(_row_shards(dw_half), land, (t * half, half), dproj,
                                             "rs_w_in_start_%d" % t)
        flights.append(flight)
    du = _matmul([(dproj, full["w_in"])], "nn", F32, 512, 1024, IN_W // 2, "d_u")
    grad_x, d_ln_mix_pre = _pre_bwd(x2, du, dh1, ln_mix_pre)

    out = {n: _adamw(recv[n], w[n], mom[n], var[n], "adamw_" + n) for n in _BIG if n != "w_in"}
    small_grads = dict(ln_mix_pre=d_ln_mix_pre, ln_mix_post=d_ln_mix_post, swa_sinks=d_sinks, rel_bias=d_rel_bias,
                       ln_mem=d_ln_mem, ln_ffn_pre=d_ln_ffn_pre, ln_ffn_post=d_ln_ffn_post)
    small_parts = _all_gather(_pack_small(small_grads, jnp.tile(loss_tile[0:1], (1, D // 128))), "ag_small")
    res = _adamw(small_parts, _pack_small(w), _pack_small(mom), _pack_small(var), "adamw_small")
    loss = res[0][7, 0]
    small = [_unpack_small(r) for r in res]
    for n in _SMALL:
        out[n] = tuple(s[n] for s in small)
    for t, flight in enumerate(flights):
        land = _scatter_wait(flight, land, res[0], "rs_w_in_wait_%d" % t)
    out["w_in"] = _adamw(land, w["w_in"], mom["w_in"], var["w_in"], "adamw_w_in")
    for n in _TRANSPOSED:
        out[n] = tuple(o.T for o in out[n])

    like = dict(ln_mix_pre=ln_mix_pre, ln_mix_post=ln_mix_post, w_in=w_in, swa_sinks=swa_sinks, rel_bias=rel_bias,
                ln_mem=ln_mem, w_mem_kv=w_mem_kv, w_branch_swa=w_branch_swa, w_branch_sb=w_branch_sb,
                w_branch_mem=w_branch_mem, w_out=w_out, ln_ffn_pre=ln_ffn_pre, ln_ffn_post=ln_ffn_post,
                w_gate=w_gate, w_up=w_up, w_down=w_down)
    result = [loss, grad_x.reshape(B, S, D)]
    for k in range(4):
        result += [out[n][k].reshape(like[n].shape) for n in _ORDER]
    return tuple(result)
```

```python
import functools
import math

import numpy as np
import jax
import jax.numpy as jnp
from jax import lax
from jax.experimental import pallas as pl
from jax.experimental.pallas import tpu as pltpu

F32 = jnp.float32
BF16 = jnp.bfloat16

N_DEV = 8
D_MODEL = 1024
BLOCK = 128
EPS = 1e-6
HEAD_DIM = 64
SWA_Q_HEADS = 8
SWA_WINDOW = 128
N_BUCKETS = 32
MAX_DISTANCE = 128
MEM_HEADS = 4
MEM_HEAD_DIM = 128
D_FF = 2816
IN_W = 5888
COL_QA, COL_KA, COL_VA, COL_QB, COL_KB, COL_VB, COL_QM, COL_GL = 0, 4, 5, 6, 10, 14, 18, 22
SCALE64 = HEAD_DIM ** -0.5
SCALE128 = MEM_HEAD_DIM ** -0.5
NEG = -1e30

ADAM_LR = 0.001
ADAM_B1 = 0.9
ADAM_B2 = 0.999
ADAM_EPS = 1e-08
ADAM_WD = 0.01
ADAM_STEP = 10

VMEM_LIMIT_BYTES = 56 * 1024 * 1024


def _params(**kw):
    return pltpu.CompilerParams(vmem_limit_bytes=VMEM_LIMIT_BYTES, **kw)


def _dot(a, b):
    return jnp.dot(a, b, preferred_element_type=F32)


def _dot_nt(a, b):
    return lax.dot_general(a, b, (((1,), (1,)), ((), ())), preferred_element_type=F32)


def _dot_tn(a, b):
    return lax.dot_general(a, b, (((0,), (0,)), ((), ())), preferred_element_type=F32)


def _dot_split(x, m2):
    hi = x.astype(BF16)
    lo = (x - hi.astype(F32)).astype(BF16)
    return _dot(jnp.concatenate([hi, lo], axis=1), m2)


def _mesh_pos():
    return lax.axis_index("x"), lax.axis_index("y"), lax.axis_index("c")


class _Hosted:
    def __init__(self, gathers=(), scatters=(), window=None):
        self.items = [("g", a) for a in gathers] + [("s", a) for a in scatters]
        self.n = len(self.items)
        self.window = window

    def operands(self):
        return [a for _, a in self.items]

    def specs(self):
        return [pl.BlockSpec(memory_space=pl.ANY)] * self.n

    def out_shapes(self):
        return [jax.ShapeDtypeStruct(((N_DEV,) + a.shape) if kind == "g" else a.shape, a.dtype)
                for kind, a in self.items]

    def scratch(self):
        return [pltpu.SemaphoreType.DMA((7 * self.n,)), pltpu.SemaphoreType.DMA((7 * self.n,)),
                pltpu.SemaphoreType.DMA((self.n,))]

    def copies(self, in_refs, out_refs, send_sems, recv_sems, local_sems):
        x, y, c = _mesh_pos()
        me = 4 * x + 2 * y + c
        out = []
        for t, (kind, _) in enumerate(self.items):
            own = in_refs[t] if kind == "g" else in_refs[t].at[me]
            dst = out_refs[t].at[me]
            if self.window is not None:
                dst = out_refs[t].at[me, :, pl.ds(*self.window)]
            out.append(pltpu.make_async_copy(own, dst, local_sems.at[t]))
            for k in range(1, N_DEV):
                px, py, pc = x ^ (k >> 2), y ^ ((k >> 1) & 1), c ^ (k & 1)
                src = in_refs[t] if kind == "g" else in_refs[t].at[4 * px + 2 * py + pc]
                out.append(pltpu.make_async_remote_copy(
                    src_ref=src, dst_ref=dst,
                    send_sem=send_sems.at[7 * t + k - 1], recv_sem=recv_sems.at[7 * t + k - 1],
                    device_id=(px, py, pc), device_id_type=pl.DeviceIdType.MESH))
        return out


def _host(body, n_in, n_out, hosted, grid):
    if hosted is None:
        return body
    nc = hosted.n

    def wrapped(*refs):
        ins = refs[:n_in]
        cin = refs[n_in:n_in + nc]
        outs = refs[n_in + nc:n_in + nc + n_out]
        cout = refs[n_in + nc + n_out:n_in + 2 * nc + n_out]
        scratch = refs[n_in + 2 * nc + n_out:len(refs) - 3]
        sems = refs[len(refs) - 3:]
        ids = [pl.program_id(d) for d in range(len(grid))]
        first = functools.reduce(lambda a, b: a & b, [i == 0 for i in ids])
        last = functools.reduce(lambda a, b: a & b, [i == g - 1 for i, g in zip(ids, grid)])

        @pl.when(first)
        def _():
            for cp in hosted.copies(cin, cout, *sems):
                cp.start()

        body(*ins, *outs, *scratch)

        @pl.when(last)
        def _():
            for cp in hosted.copies(cin, cout, *sems):
                cp.wait()

    return wrapped


def _hosted_call(body, grid, in_specs, out_specs, out_shape, scratch_shapes, hosted, name, args):
    n_out = len(out_specs)
    if hosted is None:
        outs = pl.pallas_call(body, grid=grid, in_specs=in_specs, out_specs=out_specs, out_shape=out_shape,
                              scratch_shapes=scratch_shapes, compiler_params=_params(), name=name)(*args)
        return list(outs), []
    outs = pl.pallas_call(
        _host(body, len(in_specs), n_out, hosted, grid),
        grid=grid,
        in_specs=list(in_specs) + hosted.specs(),
        out_specs=list(out_specs) + hosted.specs(),
        out_shape=list(out_shape) + hosted.out_shapes(),
        scratch_shapes=list(scratch_shapes) + hosted.scratch(),
        compiler_params=_params(),
        name=name,
    )(*args, *hosted.operands())
    return list(outs[:n_out]), list(outs[n_out:])


_DIMS = {"nn": (((1,), (0,)), ((), ())), "nt": (((1,), (1,)), ((), ())), "tn": (((0,), (0,)), ((), ()))}


def _matmul(pairs, mode, out_dtype, tm, tn, tk, name, hosted=None, n_cols=None, n_off=0):
    a0, b0 = pairs[0]
    if mode == "nn":
        (M, K), N = a0.shape, b0.shape[1]
    elif mode == "nt":
        (M, K), N = a0.shape, b0.shape[0]
    else:
        (K, M), N = a0.shape, b0.shape[1]
    N = N if n_cols is None else n_cols
    tm, tn, tk = min(tm, M), min(tn, N), min(tk, K)
    assert M % tm == 0 and N % tn == 0 and K % tk == 0, (name, M, N, K, tm, tn, tk)
    nm, nn, nk = M // tm, N // tn, K // tk
    npair = len(pairs)
    dims = _DIMS[mode]

    def body(*refs):
        ab = refs[:2 * npair]
        o_ref = refs[2 * npair]
        acc_ref = refs[2 * npair + 1]
        k = pl.program_id(2)
        part = lax.dot_general(ab[0][...], ab[1][...], dims, preferred_element_type=F32)
        for q in range(1, npair):
            part += lax.dot_general(ab[2 * q][...], ab[2 * q + 1][...], dims, preferred_element_type=F32)
        if nk == 1:
            o_ref[...] = part.astype(o_ref.dtype)
        else:
            @pl.when(k == 0)
            def _():
                acc_ref[...] = part

            @pl.when(k > 0)
            def _():
                acc_ref[...] += part

            @pl.when(k == nk - 1)
            def _():
                o_ref[...] = acc_ref[...].astype(o_ref.dtype)

    if mode == "nn":
        a_spec = pl.BlockSpec((tm, tk), lambda n, m, k: (m, k))
        b_spec = pl.BlockSpec((tk, tn), lambda n, m, k: (k, n + n_off))
    elif mode == "nt":
        a_spec = pl.BlockSpec((tm, tk), lambda n, m, k: (m, k))
        b_spec = pl.BlockSpec((tn, tk), lambda n, m, k: (n + n_off, k))
    else:
        a_spec = pl.BlockSpec((tk, tm), lambda n, m, k: (k, m))
        b_spec = pl.BlockSpec((tk, tn), lambda n, m, k: (k, n + n_off))
    args = [t for pr in pairs for t in pr]
    outs, moved = _hosted_call(
        body, (nn, nm, nk), [a_spec, b_spec] * npair, [pl.BlockSpec((tm, tn), lambda n, m, k: (m, n))],
        [jax.ShapeDtypeStruct((M, N), out_dtype)], [pltpu.VMEM((tm, tn) if nk > 1 else (8, 128), F32)],
        hosted, name, args)
    return outs[0] if hosted is None else (outs[0], moved)


def _rms_fwd(x, g, name):
    T, D = x.shape
    tr = min(512, T)

    def body(x_ref, g_ref, u_ref):
        xf = x_ref[...]
        r = lax.rsqrt(jnp.mean(xf * xf, axis=-1, keepdims=True) + EPS)
        u_ref[...] = ((xf * r) * g_ref[...]).astype(u_ref.dtype)

    return pl.pallas_call(
        body,
        grid=(T // tr,),
        in_specs=[pl.BlockSpec((tr, D), lambda i: (i, 0)), pl.BlockSpec((1, D), lambda i: (0, 0))],
        out_specs=pl.BlockSpec((tr, D), lambda i: (i, 0)),
        out_shape=jax.ShapeDtypeStruct((T, D), BF16),
        name=name,
    )(x, g)


def _rms_bwd_terms(xin, g, dy):
    r = lax.rsqrt(jnp.mean(xin * xin, axis=-1, keepdims=True) + EPS)
    xh = xin * r
    dg = jnp.sum(dy * xh, axis=0, keepdims=True)
    dxh = dy * g
    dx = r * (dxh - xh * jnp.mean(dxh * xh, axis=-1, keepdims=True))
    return dx, dg


GATE_TC = 256


def _branch_gate(proj, ys, wbs):
    T = proj.shape[0]
    D = D_MODEL
    tr, tc = min(1024, T), GATE_TC
    nc = D // tc
    gl0 = COL_GL * 128 // tc

    def body(ya, yb, yc, wa, wb, wc, g0, g1, g2, merged_ref, pa, pb, pc):
        acc = jnp.zeros((tr, tc), F32)
        for y_ref, w_ref, g_ref, p_ref in ((ya, wa, g0, pa), (yb, wb, g1, pb), (yc, wc, g2, pc)):
            p = _dot(y_ref[...], w_ref[...])
            p_ref[...] = p.astype(p_ref.dtype)
            acc += jax.nn.sigmoid(g_ref[...].astype(F32)) * p
        merged_ref[...] = acc.astype(merged_ref.dtype)

    y_spec = pl.BlockSpec((tr, 512), lambda i, n: (i, 0))
    w_spec = pl.BlockSpec((512, tc), lambda i, n: (0, n))
    o_spec = pl.BlockSpec((tr, tc), lambda i, n: (i, n))
    gl_specs = [pl.BlockSpec((tr, tc), lambda i, n, j=j: (i, gl0 + j * nc + n)) for j in range(3)]
    out = jax.ShapeDtypeStruct((T, D), BF16)
    return pl.pallas_call(
        body,
        grid=(T // tr, nc),
        in_specs=[y_spec] * 3 + [w_spec] * 3 + gl_specs,
        out_specs=[o_spec] * 4,
        out_shape=[out] * 4,
        compiler_params=_params(),
        name="branch_gate",
    )(*ys, *wbs, proj, proj, proj)


def _post_pre(x, merged, w_out, g_post, g_pre):
    T, D = x.shape
    tr = 512

    def body(x_ref, m_ref, w_ref, gp_ref, gq_ref, mix_ref, h1_ref, u2_ref):
        mx = _dot(m_ref[...], w_ref[...])
        mix_ref[...] = mx
        r = lax.rsqrt(jnp.mean(mx * mx, axis=-1, keepdims=True) + EPS)
        h1 = x_ref[...] + (mx * r) * gp_ref[...]
        h1_ref[...] = h1
        r2 = lax.rsqrt(jnp.mean(h1 * h1, axis=-1, keepdims=True) + EPS)
        u2_ref[...] = ((h1 * r2) * gq_ref[...]).astype(u2_ref.dtype)

    row = pl.BlockSpec((tr, D), lambda i: (i, 0))
    vec = pl.BlockSpec((1, D), lambda i: (0, 0))
    f32 = jax.ShapeDtypeStruct((T, D), F32)
    return pl.pallas_call(
        body,
        grid=(T // tr,),
        in_specs=[row, row, pl.BlockSpec((D, D), lambda i: (0, 0)), vec, vec],
        out_specs=[row, row, row],
        out_shape=[f32, f32, jax.ShapeDtypeStruct((T, D), BF16)],
        compiler_params=_params(),
        name="post_pre",
    )(x, merged, w_out, g_post, g_pre)


def _ffn_up(u2, w_gate_t, w_up_t):
    T, D = u2.shape
    F = D_FF
    tm, tn = 512, F // 2

    def body(u_ref, wg_ref, wu_ref, a_ref, zg_ref, zu_ref):
        u = u_ref[...]
        zg = _dot_nt(u, wg_ref[...])
        zu = _dot_nt(u, wu_ref[...])
        a_ref[...] = (zg * jax.nn.sigmoid(zg) * zu).astype(a_ref.dtype)
        zg_ref[...] = zg.astype(zg_ref.dtype)
        zu_ref[...] = zu.astype(zu_ref.dtype)

    o_spec = pl.BlockSpec((tm, tn), lambda n, m: (m, n))
    out = jax.ShapeDtypeStruct((T, F), BF16)
    return pl.pallas_call(
        body,
        grid=(F // tn, T // tm),
        in_specs=[pl.BlockSpec((tm, D), lambda n, m: (m, 0)),
                  pl.BlockSpec((tn, D), lambda n, m: (n, 0)),
                  pl.BlockSpec((tn, D), lambda n, m: (n, 0))],
        out_specs=[o_spec] * 3,
        out_shape=[out] * 3,
        compiler_params=_params(),
        name="ffn_up",
    )(u2, w_gate_t, w_up_t)


def _loss_head(a, w_down, h1, target, g_post):
    T, D = h1.shape
    F = a.shape[1]
    tr = 512

    def body(a_ref, w_ref, h1_ref, t_ref, g_ref, dffn_ref, dh2_ref, loss_ref, dg_ref):
        i = pl.program_id(0)
        f = _dot(a_ref[...], w_ref[...])
        g = g_ref[...]
        r = lax.rsqrt(jnp.mean(f * f, axis=-1, keepdims=True) + EPS)
        xh = f * r
        err = (h1_ref[...] + xh * g) - t_ref[...]
        part = 0.5 * jnp.sum(jnp.mean(err * err, axis=-1, keepdims=True), axis=0, keepdims=True)
        dh2 = err * (1.0 / D)
        dh2_ref[...] = dh2
        dgp = jnp.sum(dh2 * xh, axis=0, keepdims=True)
        dxh = dh2 * g
        dffn_ref[...] = (r * (dxh - xh * jnp.mean(dxh * xh, axis=-1, keepdims=True))).astype(dffn_ref.dtype)

        @pl.when(i == 0)
        def _():
            loss_ref[...] = jnp.zeros_like(loss_ref)
            dg_ref[...] = jnp.zeros_like(dg_ref)

        loss_ref[...] += jnp.broadcast_to(part, loss_ref.shape)
        dg_ref[...] += dgp

    row = pl.BlockSpec((tr, D), lambda i: (i, 0))
    vec = pl.BlockSpec((1, D), lambda i: (0, 0))
    return pl.pallas_call(
        body,
        grid=(T // tr,),
        in_specs=[pl.BlockSpec((tr, F), lambda i: (i, 0)), pl.BlockSpec((F, D), lambda i: (0, 0)), row, row, vec],
        out_specs=[row, row, pl.BlockSpec((8, 128), lambda i: (0, 0)), vec],
        out_shape=[jax.ShapeDtypeStruct((T, D), BF16), jax.ShapeDtypeStruct((T, D), F32),
                   jax.ShapeDtypeStruct((8, 128), F32), jax.ShapeDtypeStruct((1, D), F32)],
        compiler_params=_params(),
        name="loss_head",
    )(a, w_down, h1, target, g_post)


def _ffn_down_bwd(dffn, wd, zg, zu):
    T, D = dffn.shape
    F = D_FF
    tm, tn = 512, F // 2

    def body(d_ref, w_ref, zg_ref, zu_ref, dzg_ref, dzu_ref):
        d = d_ref[...]
        for lo in range(0, tn, 512):
            cols = slice(lo, min(lo + 512, tn))
            da = _dot_nt(d, w_ref[cols, :])
            zg = zg_ref[:, cols].astype(F32)
            zu = zu_ref[:, cols].astype(F32)
            s = jax.nn.sigmoid(zg)
            dzu_ref[:, cols] = (da * (zg * s)).astype(dzu_ref.dtype)
            dzg_ref[:, cols] = (da * zu * (s * (1.0 + zg * (1.0 - s)))).astype(dzg_ref.dtype)

    z_spec = pl.BlockSpec((tm, tn), lambda n, m: (m, n))
    out = jax.ShapeDtypeStruct((T, F), BF16)
    return pl.pallas_call(
        body,
        grid=(F // tn, T // tm),
        in_specs=[pl.BlockSpec((tm, D), lambda n, m: (m, 0)), pl.BlockSpec((tn, D), lambda n, m: (n, 0)),
                  z_spec, z_spec],
        out_specs=[z_spec, z_spec],
        out_shape=[out, out],
        compiler_params=_params(),
        name="ffn_down_bwd",
    )(dffn, wd, zg, zu)


def _mid_bwd(h1, du2, dh2, mix, g_pre, g_post):
    T, D = h1.shape
    tr = 512

    def body(h1_ref, du2_ref, dh2_ref, mix_ref, gq_ref, gp_ref, dh1_ref, dmix_ref, dgq_ref, dgp_ref):
        i = pl.program_id(0)
        dx, dgq = _rms_bwd_terms(h1_ref[...], gq_ref[...], du2_ref[...])
        dh1 = dh2_ref[...] + dx
        dh1_ref[...] = dh1
        dmix, dgp = _rms_bwd_terms(mix_ref[...], gp_ref[...], dh1)
        dmix_ref[...] = dmix.astype(dmix_ref.dtype)

        @pl.when(i == 0)
        def _():
            dgq_ref[...] = jnp.zeros_like(dgq_ref)
            dgp_ref[...] = jnp.zeros_like(dgp_ref)

        dgq_ref[...] += dgq
        dgp_ref[...] += dgp

    row = pl.BlockSpec((tr, D), lambda i: (i, 0))
    vec = pl.BlockSpec((1, D), lambda i: (0, 0))
    return pl.pallas_call(
        body,
        grid=(T // tr,),
        in_specs=[row, row, row, row, vec, vec],
        out_specs=[row, row, vec, vec],
        out_shape=[jax.ShapeDtypeStruct((T, D), F32), jax.ShapeDtypeStruct((T, D), BF16),
                   jax.ShapeDtypeStruct((1, D), F32), jax.ShapeDtypeStruct((1, D), F32)],
        compiler_params=_params(),
        name="mid_bwd",
    )(h1, du2, dh2, mix, g_pre, g_post)


def _gate_bwd(dmix, w_out, ps, proj, wbs):
    T, D = dmix.shape
    tr, tc = min(1024, T), GATE_TC
    nc = D // tc
    gl0 = COL_GL * 128 // tc

    def body(dmix_ref, wo_ref, pa, pb, pc, g0, g1, g2, wa, wb, wc, dpa, dpb, dpc, dga, dgb, dgc, dya, dyb, dyc,
             acc_a, acc_b, acc_c):
        n = pl.program_id(1)
        dm = _dot_nt(dmix_ref[...], wo_ref[...])
        for p_ref, g_ref, w_ref, dp_ref, dg_ref, dy_ref, acc_ref in (
                (pa, g0, wa, dpa, dga, dya, acc_a), (pb, g1, wb, dpb, dgb, dyb, acc_b),
                (pc, g2, wc, dpc, dgc, dyc, acc_c)):
            s = jax.nn.sigmoid(g_ref[...].astype(F32))
            dp = (dm * s).astype(BF16)
            dp_ref[...] = dp
            dg_ref[...] = (dm * p_ref[...].astype(F32) * (s * (1.0 - s))).astype(dg_ref.dtype)
            part = _dot_nt(dp, w_ref[...])

            @pl.when(n == 0)
            def _():
                acc_ref[...] = part

            @pl.when(n > 0)
            def _():
                acc_ref[...] += part

            @pl.when(n == nc - 1)
            def _():
                dy_ref[...] = acc_ref[...].astype(dy_ref.dtype)

    col = pl.BlockSpec((tr, tc), lambda i, n: (i, n))
    y_spec = pl.BlockSpec((tr, 512), lambda i, n: (i, 0))
    w_spec = pl.BlockSpec((512, tc), lambda i, n: (0, n))
    gl_specs = [pl.BlockSpec((tr, tc), lambda i, n, j=j: (i, gl0 + j * nc + n)) for j in range(3)]
    big = jax.ShapeDtypeStruct((T, D), BF16)
    small = jax.ShapeDtypeStruct((T, 512), BF16)
    return pl.pallas_call(
        body,
        grid=(T // tr, nc),
        in_specs=[pl.BlockSpec((tr, D), lambda i, n: (i, 0)), pl.BlockSpec((tc, D), lambda i, n: (n, 0))]
        + [col] * 3 + gl_specs + [w_spec] * 3,
        out_specs=[col] * 6 + [y_spec] * 3,
        out_shape=[big] * 6 + [small] * 3,
        scratch_shapes=[pltpu.VMEM((tr, 512), F32)] * 3,
        compiler_params=_params(),
        name="gate_bwd",
    )(dmix, w_out, *ps, proj, proj, proj, *wbs)


def _pre_bwd(x, du, dh1, g):
    T, D = x.shape
    tr = 512

    def body(x_ref, du_ref, dh1_ref, g_ref, gx_ref, dg_ref):
        i = pl.program_id(0)
        dx, dg = _rms_bwd_terms(x_ref[...], g_ref[...], du_ref[...])
        gx_ref[...] = dh1_ref[...] + dx

        @pl.when(i == 0)
        def _():
            dg_ref[...] = jnp.zeros_like(dg_ref)

        dg_ref[...] += dg

    row = pl.BlockSpec((tr, D), lambda i: (i, 0))
    vec = pl.BlockSpec((1, D), lambda i: (0, 0))
    return pl.pallas_call(
        body,
        grid=(T // tr,),
        in_specs=[row, row, row, vec],
        out_specs=[row, vec],
        out_shape=[jax.ShapeDtypeStruct((T, D), F32), jax.ShapeDtypeStruct((1, D), F32)],
        compiler_params=_params(),
        name="pre_bwd",
    )(x, du, dh1, g)


def _gain_grad(xin, dy):
    T, D = xin.shape
    tr = min(512, T)

    def body(x_ref, dy_ref, dg_ref):
        i = pl.program_id(0)
        xf = x_ref[...]
        r = lax.rsqrt(jnp.mean(xf * xf, axis=-1, keepdims=True) + EPS)

        @pl.when(i == 0)
        def _():
            dg_ref[...] = jnp.zeros_like(dg_ref)

        dg_ref[...] += jnp.sum(dy_ref[...] * (xf * r), axis=0, keepdims=True)

    row = pl.BlockSpec((tr, D), lambda i: (i, 0))
    return pl.pallas_call(
        body,
        grid=(T // tr,),
        in_specs=[row, row],
        out_specs=pl.BlockSpec((1, D), lambda i: (0, 0)),
        out_shape=jax.ShapeDtypeStruct((1, D), F32),
        name="gain_grad",
    )(xin, dy)


def _swa_buckets():
    dist = (np.arange(BLOCK)[:, None] + BLOCK) - np.arange(2 * BLOCK)[None, :]
    max_exact = N_BUCKETS // 2
    d = np.maximum(dist, 0)
    df = np.maximum(d, 1).astype(np.float32)
    large = max_exact + (np.log(df / np.float32(max_exact)) / np.float32(math.log(MAX_DISTANCE / max_exact))
                         * np.float32(N_BUCKETS - max_exact)).astype(np.int32)
    large = np.minimum(large, N_BUCKETS - 1)
    bucket = np.where(d < max_exact, d, large)
    in_win = (dist >= 0) & (dist < SWA_WINDOW)
    return np.where(in_win, bucket, -1).astype(np.int32)


def _swa_bias_table(rel_bias, buckets):
    H = SWA_Q_HEADS

    def body(rb_ref, bk_ref, o_ref):
        bk = bk_ref[...]
        for h in range(H):
            acc = jnp.full(bk.shape, NEG, F32)
            for b in range(N_BUCKETS):
                acc = jnp.where(bk == b, rb_ref[b, h], acc)
            o_ref[h] = acc

    return pl.pallas_call(
        body,
        in_specs=[pl.BlockSpec(memory_space=pltpu.SMEM), pl.BlockSpec(memory_space=pltpu.VMEM)],
        out_specs=pl.BlockSpec(memory_space=pltpu.VMEM),
        out_shape=jax.ShapeDtypeStruct((H, BLOCK, 2 * BLOCK), F32),
        name="swa_bias_table",
    )(rel_bias, buckets)


def _swa_bias_grad(dbias, buckets):
    H = SWA_Q_HEADS

    def body(db_ref, bk_ref, o_ref):
        bk = bk_ref[...]
        rows = lax.broadcasted_iota(jnp.int32, (N_BUCKETS, 128), 0)
        lanes = lax.broadcasted_iota(jnp.int32, (N_BUCKETS, 128), 1)
        acc = jnp.zeros((N_BUCKETS, 128), F32)
        for h in range(H):
            d = db_ref[h]
            for b in range(N_BUCKETS):
                s = jnp.sum(jnp.sum(jnp.where(bk == b, d, 0.0), axis=1, keepdims=True), axis=0, keepdims=True)
                acc = jnp.where((rows == b) & (lanes == h), s, acc)
        o_ref[...] = acc

    return pl.pallas_call(
        body,
        in_specs=[pl.BlockSpec(memory_space=pltpu.VMEM)] * 2,
        out_specs=pl.BlockSpec(memory_space=pltpu.VMEM),
        out_shape=jax.ShapeDtypeStruct((N_BUCKETS, 128), F32),
        name="swa_bias_grad",
    )(dbias, buckets)


SWA_GROUP = 4
SWA_ROWS = SWA_Q_HEADS * BLOCK


def _swa_kv_lanes(h):
    lane = lax.broadcasted_iota(jnp.int32, (BLOCK, BLOCK), 1)
    return (lane // HEAD_DIM) == h // SWA_GROUP


def _swa_stack(x, heads):
    blocks = []
    for h in heads:
        xp = x[:, (h // 2) * BLOCK:(h // 2 + 1) * BLOCK]
        xs = xp if h % 2 == h // SWA_GROUP else pltpu.roll(xp, HEAD_DIM, 1)
        blocks.append(jnp.where(_swa_kv_lanes(h), xs, 0.0))
    return jnp.concatenate(blocks, axis=0).astype(BF16)


def _swa_unstack(blocks):
    pairs = []
    for p in range(4):
        halves = []
        for hh in range(2):
            h = 2 * p + hh
            blk = jnp.where(_swa_kv_lanes(h), blocks[h], 0.0)
            halves.append(blk if hh == h // SWA_GROUP else pltpu.roll(blk, HEAD_DIM, 1))
        pairs.append(halves[0] + halves[1])
    return jnp.concatenate(pairs, axis=1)


def _swa_stacked_params(sink_ref, bias_ref, heads):
    bias = jnp.concatenate([bias_ref[h] for h in heads], axis=0)
    sink = jnp.concatenate([jnp.full((BLOCK, 1), sink_ref[0, h], F32) for h in heads], axis=0)
    return bias, sink


def _swa_scores(qs, kp, kc, bias, sink, first):
    sp = _dot_nt(qs, kp) * SCALE64 + bias[:, :BLOCK]
    sp = jnp.where(first, NEG, sp)
    sc = _dot_nt(qs, kc) * SCALE64 + bias[:, BLOCK:]
    m = jnp.maximum(jnp.maximum(jnp.max(sp, axis=1, keepdims=True), jnp.max(sc, axis=1, keepdims=True)), sink)
    pp = jnp.exp(sp - m)
    pc = jnp.exp(sc - m)
    ps = jnp.exp(sink - m)
    den = jnp.sum(pp, axis=1, keepdims=True) + jnp.sum(pc, axis=1, keepdims=True) + ps
    return pp / den, pc / den, ps / den


def _swa_fwd(proj, sinks, bias_tab, B, S, hosted=None):
    nb = S // BLOCK
    T = B * S

    def body(sink_ref, q_ref, kp_ref, kc_ref, vp_ref, vc_ref, bias_ref, o_ref):
        i = pl.program_id(1)
        first = jnp.full((SWA_GROUP * BLOCK, BLOCK), i, jnp.int32) == 0
        q = q_ref[...].astype(F32)
        blocks = []
        for g in range(SWA_Q_HEADS // SWA_GROUP):
            heads = range(SWA_GROUP * g, SWA_GROUP * (g + 1))
            bias, sink = _swa_stacked_params(sink_ref, bias_ref, heads)
            wp, wc, _ = _swa_scores(_swa_stack(q, heads), kp_ref[...], kc_ref[...], bias, sink, first)
            o = _dot(wp.astype(BF16), vp_ref[...]) + _dot(wc.astype(BF16), vc_ref[...])
            blocks += [o[t * BLOCK:(t + 1) * BLOCK] for t in range(SWA_GROUP)]
        o_ref[...] = _swa_unstack(blocks).astype(o_ref.dtype)

    blk = (BLOCK, BLOCK)
    wide = (BLOCK, 4 * BLOCK)
    outs, moved = _hosted_call(
        body, (B, nb),
        [pl.BlockSpec(memory_space=pltpu.SMEM),
         pl.BlockSpec(wide, lambda b, i: (b * nb + i, COL_QA // 4)),
         pl.BlockSpec(blk, lambda b, i: (b * nb + jnp.maximum(i - 1, 0), COL_KA)),
         pl.BlockSpec(blk, lambda b, i: (b * nb + i, COL_KA)),
         pl.BlockSpec(blk, lambda b, i: (b * nb + jnp.maximum(i - 1, 0), COL_VA)),
         pl.BlockSpec(blk, lambda b, i: (b * nb + i, COL_VA)),
         pl.BlockSpec((SWA_Q_HEADS, BLOCK, 2 * BLOCK), lambda b, i: (0, 0, 0))],
        [pl.BlockSpec(wide, lambda b, i: (b * nb + i, 0))],
        [jax.ShapeDtypeStruct((T, 512), BF16)], [], hosted, "swa_fwd",
        (sinks, proj, proj, proj, proj, proj, bias_tab))
    return outs[0], moved


def _swa_bwd(proj, dy, sinks, bias_tab, B, S):
    nb = S // BLOCK
    T = B * S
    H = SWA_Q_HEADS

    def body(sink_ref, q_ref, kp_ref, kc_ref, vp_ref, vc_ref, do_ref, bias_ref,
             dq_ref, dk_ref, dv_ref, dbias_ref, dsink_ref):
        b = pl.program_id(0)
        i = pl.program_id(1)

        @pl.when((b == 0) & (i == 0))
        def _():
            dbias_ref[...] = jnp.zeros_like(dbias_ref)
            dsink_ref[...] = jnp.zeros_like(dsink_ref)

        @pl.when(i == 0)
        def _():
            dk_ref[...] = jnp.zeros_like(dk_ref)
            dv_ref[...] = jnp.zeros_like(dv_ref)

        first = jnp.full((SWA_ROWS, BLOCK), i, jnp.int32) == 0
        heads = range(H)
        bias, sink = _swa_stacked_params(sink_ref, bias_ref, heads)
        kp, kc, vp, vc = kp_ref[...], kc_ref[...], vp_ref[...], vc_ref[...]
        qs = _swa_stack(q_ref[...].astype(F32), heads)
        dos = _swa_stack(do_ref[...].astype(F32), heads)
        wp, wc, ws = _swa_scores(qs, kp, kc, bias, sink, first)
        dwp = _dot_nt(dos, vp)
        dwc = _dot_nt(dos, vc)
        dsum = jnp.sum(wp * dwp, axis=1, keepdims=True) + jnp.sum(wc * dwc, axis=1, keepdims=True)
        dsp = wp * (dwp - dsum)
        dsc = wc * (dwc - dsum)
        dsk = -ws * dsum
        dsinks = []
        for h in range(H):
            rows = slice(h * BLOCK, (h + 1) * BLOCK)
            dsinks.append(jnp.broadcast_to(jnp.sum(dsk[rows], axis=0, keepdims=True), (1, 128)))
            dbias_ref[h] += jnp.concatenate([dsp[rows], dsc[rows]], axis=1)
        dsink_ref[...] += jnp.concatenate(dsinks, axis=0)
        dspb = dsp.astype(BF16)
        dscb = dsc.astype(BF16)
        dq = _dot(dspb, kp) + _dot(dscb, kc)
        dq_ref[...] = (_swa_unstack([dq[h * BLOCK:(h + 1) * BLOCK] for h in heads]) * SCALE64).astype(dq_ref.dtype)
        cur = pl.ds(pl.multiple_of(i * BLOCK, BLOCK), BLOCK)
        prev = pl.ds(pl.multiple_of(jnp.maximum(i - 1, 0) * BLOCK, BLOCK), BLOCK)
        dk_ref[prev, :] += _dot_tn(dspb, qs) * SCALE64
        dk_ref[cur, :] += _dot_tn(dscb, qs) * SCALE64
        dv_ref[prev, :] += _dot_tn(wp.astype(BF16), dos)
        dv_ref[cur, :] += _dot_tn(wc.astype(BF16), dos)

    blk = (BLOCK, BLOCK)
    wide = (BLOCK, 4 * BLOCK)
    kv_out = pl.BlockSpec((S, BLOCK), lambda b, i: (b, 0))
    full_bias = pl.BlockSpec((H, BLOCK, 2 * BLOCK), lambda b, i: (0, 0, 0))
    return pl.pallas_call(
        body,
        grid=(B, nb),
        in_specs=[pl.BlockSpec(memory_space=pltpu.SMEM),
                  pl.BlockSpec(wide, lambda b, i: (b * nb + i, COL_QA // 4)),
                  pl.BlockSpec(blk, lambda b, i: (b * nb + jnp.maximum(i - 1, 0), COL_KA)),
                  pl.BlockSpec(blk, lambda b, i: (b * nb + i, COL_KA)),
                  pl.BlockSpec(blk, lambda b, i: (b * nb + jnp.maximum(i - 1, 0), COL_VA)),
                  pl.BlockSpec(blk, lambda b, i: (b * nb + i, COL_VA)),
                  pl.BlockSpec(wide, lambda b, i: (b * nb + i, 0)),
                  full_bias],
        out_specs=[pl.BlockSpec(wide, lambda b, i: (b * nb + i, 0)),
                   kv_out, kv_out, full_bias,
                   pl.BlockSpec((H, 128), lambda b, i: (0, 0))],
        out_shape=[jax.ShapeDtypeStruct((T, 512), BF16),
                   jax.ShapeDtypeStruct((T, BLOCK), F32), jax.ShapeDtypeStruct((T, BLOCK), F32),
                   jax.ShapeDtypeStruct((H, BLOCK, 2 * BLOCK), F32), jax.ShapeDtypeStruct((H, 128), F32)],
        compiler_params=_params(),
        name="swa_bwd",
    )(sinks, proj, proj, proj, proj, proj, dy, bias_tab)


SB_TILE = 256


SB_HEADS = 4
SB_LANES = SB_HEADS * HEAD_DIM
SB_ROWS = SB_HEADS * SB_TILE


def _sb_logits(z, tri):
    sp = jnp.log(1.0 + jnp.exp(-jnp.abs(z)))
    ls = jnp.minimum(z, 0.0) - sp
    l1m = ls - z
    if tri is not None:
        l1m = jnp.where(tri, l1m, 0.0)
    return ls, l1m


def _sb_masks():
    lane = lax.broadcasted_iota(jnp.int32, (SB_TILE, SB_LANES), 1)
    hm = [(lane // HEAD_DIM) == h for h in range(SB_HEADS)]
    row = lax.broadcasted_iota(jnp.int32, (SB_ROWS, SB_TILE), 0) % SB_TILE
    col = lax.broadcasted_iota(jnp.int32, (SB_ROWS, SB_TILE), 1)
    return lane, hm, row, col


def _sb_stack(x, hm):
    return jnp.concatenate([jnp.where(m, x, 0) for m in hm], axis=0)


def _sb_unstack(x, hm):
    return sum(jnp.where(m, x[h * SB_TILE:(h + 1) * SB_TILE], 0.0) for h, m in enumerate(hm))


def _sb_fwd(proj, B, S, hosted=None):
    nt = S // SB_TILE
    T = B * S

    ng = 512 // SB_LANES

    def body(*refs):
        q_refs, k_refs, v_refs = refs[:ng], refs[ng:2 * ng], refs[2 * ng:3 * ng]
        o_ref, r_ref = refs[3 * ng:]
        i = pl.program_id(1)
        _, hm, row, col = _sb_masks()
        tri = col < row
        later = jnp.concatenate([(row[:SB_TILE] > col[:SB_TILE]).astype(BF16)] * 2, axis=0)
        qs = [_sb_stack(q_ref[...] * SCALE64, hm) for q_ref in q_refs]

        def tile(j, carry, mask):
            rows = pl.ds(pl.multiple_of(j * SB_TILE, SB_TILE), SB_TILE)
            out = []
            for g in range(ng):
                acc, c = carry[g]
                ls, l1m = _sb_logits(_dot_nt(qs[g], k_refs[g][rows, :]), mask)
                a = jnp.exp(ls + c + _dot_split(l1m, later))
                if mask is not None:
                    a = jnp.where(mask, a, 0.0)
                pv = _dot(a.astype(BF16), v_refs[g][rows, :])
                out.append((acc + _sb_unstack(pv, hm), c + jnp.sum(l1m, axis=1, keepdims=True)))
            return tuple(out)

        zero = (jnp.zeros((SB_TILE, SB_LANES), F32), jnp.zeros((SB_ROWS, 1), F32))
        carry = tile(i, (zero,) * ng, tri)
        carry = lax.fori_loop(0, i, lambda it, cr: tile(i - 1 - it, cr, None), carry)
        o_ref[...] = jnp.concatenate([acc for acc, _ in carry], axis=1).astype(o_ref.dtype)
        r_ref[...] = jnp.concatenate(
            [_sb_unstack(jnp.broadcast_to(c, (SB_ROWS, SB_LANES)), hm) for _, c in carry], axis=1)

    blk = (SB_TILE, SB_LANES)
    cq, ck, cv = (c * BLOCK // SB_LANES for c in (COL_QB, COL_KB, COL_VB))
    wide = pl.BlockSpec((SB_TILE, 512), lambda b, i: (b * nt + i, 0))
    outs, moved = _hosted_call(
        body, (B, nt),
        [pl.BlockSpec(blk, lambda b, i, g=g: (b * nt + i, cq + g)) for g in range(ng)]
        + [pl.BlockSpec((S, SB_LANES), lambda b, i, g=g: (b, ck + g)) for g in range(ng)]
        + [pl.BlockSpec((S, SB_LANES), lambda b, i, g=g: (b, cv + g)) for g in range(ng)],
        [wide, wide],
        [jax.ShapeDtypeStruct((T, 512), BF16), jax.ShapeDtypeStruct((T, 512), F32)], [], hosted, "sb_fwd",
        (proj,) * (3 * ng))
    return outs[0], outs[1], moved


def _sb_bwd(proj, dy, rtot, B, S, hosted=None):
    nt = S // SB_TILE
    T = B * S

    ng = 512 // SB_LANES

    def body(*refs):
        q_refs, k_refs, v_refs = refs[:ng], refs[ng:2 * ng], refs[2 * ng:3 * ng]
        do_ref, r_ref, dq_ref, dk_ref, dv_ref = refs[3 * ng:]
        i = pl.program_id(1)

        @pl.when(i == 0)
        def _():
            dk_ref[...] = jnp.zeros_like(dk_ref)
            dv_ref[...] = jnp.zeros_like(dv_ref)

        lane, hm, row, col = _sb_masks()
        tri = col < row
        later = jnp.concatenate([(row[:SB_TILE] > col[:SB_TILE]).astype(BF16)] * 2, axis=0)
        earlier = (row[:SB_TILE] < col[:SB_TILE]).astype(BF16)
        qs, dos, rs = [], [], []
        for g in range(ng):
            lanes = slice(g * SB_LANES, (g + 1) * SB_LANES)
            qs.append(_sb_stack(q_refs[g][...] * SCALE64, hm))
            dos.append(_sb_stack(do_ref[:, lanes], hm))
            r = r_ref[:, lanes]
            rs.append(jnp.concatenate([jnp.sum(jnp.where(lane == h * HEAD_DIM, r, 0.0), axis=1, keepdims=True)
                                       for h in range(SB_HEADS)], axis=0))

        def tile(j, carry, mask):
            rows = pl.ds(pl.multiple_of(j * SB_TILE, SB_TILE), SB_TILE)
            out = []
            for g in range(ng):
                lanes = slice(g * SB_LANES, (g + 1) * SB_LANES)
                dq, lsum, psum = carry[g]
                kj = k_refs[g][rows, :]
                vj = v_refs[g][rows, :]
                ls, l1m = _sb_logits(_dot_nt(qs[g], kj), mask)
                sig = jnp.exp(ls)
                lsum = lsum + jnp.sum(l1m, axis=1, keepdims=True)
                a = jnp.exp(ls + (rs[g] - lsum) + _dot_split(l1m, later))
                if mask is not None:
                    a = jnp.where(mask, a, 0.0)
                de = _dot_nt(dos[g], vj) * a
                pre = psum + _dot(de.astype(BF16), earlier)
                dz = de - sig * (de + pre)
                if mask is not None:
                    dz = jnp.where(mask, dz, 0.0)
                dz = dz.astype(BF16)
                dk_ref[rows, lanes] += _dot_tn(dz, qs[g])
                dv_ref[rows, lanes] += _dot_tn(a.astype(BF16), dos[g])
                out.append((dq + _sb_unstack(_dot(dz, kj), hm), lsum, psum + jnp.sum(de, axis=1, keepdims=True)))
            return tuple(out)

        zero = jnp.zeros((SB_ROWS, 1), F32)
        init = ((jnp.zeros((SB_TILE, SB_LANES), F32), zero, zero),) * ng
        carry = lax.fori_loop(0, i, lambda j, c: tile(j, c, None), init)
        carry = tile(i, carry, tri)
        dq_ref[...] = (jnp.concatenate([c[0] for c in carry], axis=1) * SCALE64).astype(dq_ref.dtype)

    blk = (SB_TILE, SB_LANES)
    cq, ck, cv = (c * BLOCK // SB_LANES for c in (COL_QB, COL_KB, COL_VB))
    wide = pl.BlockSpec((SB_TILE, 512), lambda b, i: (b * nt + i, 0))
    kv_out = pl.BlockSpec((S, 512), lambda b, i: (b, 0))
    outs, moved = _hosted_call(
        body, (B, nt),
        [pl.BlockSpec(blk, lambda b, i, g=g: (b * nt + i, cq + g)) for g in range(ng)]
        + [pl.BlockSpec((S, SB_LANES), lambda b, i, g=g: (b, ck + g)) for g in range(ng)]
        + [pl.BlockSpec((S, SB_LANES), lambda b, i, g=g: (b, cv + g)) for g in range(ng)]
        + [wide, wide],
        [wide, kv_out, kv_out],
        [jax.ShapeDtypeStruct((T, 512), BF16),
         jax.ShapeDtypeStruct((T, 512), F32), jax.ShapeDtypeStruct((T, 512), F32)], [], hosted, "sb_bwd",
        (proj,) * (3 * ng) + (dy, rtot))
    return outs[0], outs[1], outs[2], moved


def _mem_weights(q, mk):
    z = _dot_nt(q, mk) * SCALE128
    e = jnp.exp(z - jnp.max(z, axis=1, keepdims=True))
    return e / jnp.sum(e, axis=1, keepdims=True)


def _mem_fwd(proj, mkv, B, S, M):
    tq = 512
    nq = S // tq
    T = B * S
    Hm = MEM_HEADS

    def body(q0, q1, q2, q3, mk_ref, mv_ref, o_ref):
        outs = []
        for h, q_ref in enumerate((q0, q1, q2, q3)):
            cols = slice(h * 128, (h + 1) * 128)
            w = _mem_weights(q_ref[...], mk_ref[:, cols])
            outs.append(_dot(w.astype(BF16), mv_ref[:, cols]))
        o_ref[...] = jnp.concatenate(outs, axis=1).astype(o_ref.dtype)

    return pl.pallas_call(
        body,
        grid=(B, nq),
        in_specs=[pl.BlockSpec((tq, 128), lambda b, i, h=h: (b * nq + i, COL_QM + h)) for h in range(Hm)]
        + [pl.BlockSpec((M, 512), lambda b, i: (b, 0)), pl.BlockSpec((M, 512), lambda b, i: (b, 1))],
        out_specs=pl.BlockSpec((tq, 512), lambda b, i: (b * nq + i, 0)),
        out_shape=jax.ShapeDtypeStruct((T, 512), BF16),
        name="mem_fwd",
    )(proj, proj, proj, proj, mkv, mkv)


def _mem_bwd(proj, mkv, dy, B, S, M):
    tq = 512
    nq = S // tq
    T = B * S
    Hm = MEM_HEADS

    def body(q0, q1, q2, q3, mk_ref, mv_ref, do_ref, dq_ref, dmk_ref, dmv_ref):
        i = pl.program_id(1)
        dqs, dmks, dmvs = [], [], []
        for h, q_ref in enumerate((q0, q1, q2, q3)):
            cols = slice(h * 128, (h + 1) * 128)
            q = q_ref[...]
            do = do_ref[:, cols]
            mk = mk_ref[:, cols]
            w = _mem_weights(q, mk)
            dw = _dot_nt(do, mv_ref[:, cols])
            ds = (w * (dw - jnp.sum(w * dw, axis=1, keepdims=True))).astype(BF16)
            dqs.append(_dot(ds, mk))
            dmks.append(_dot_tn(ds, q))
            dmvs.append(_dot_tn(w.astype(BF16), do))
        dq_ref[...] = (jnp.concatenate(dqs, axis=1) * SCALE128).astype(dq_ref.dtype)

        @pl.when(i == 0)
        def _():
            dmk_ref[...] = jnp.zeros_like(dmk_ref)
            dmv_ref[...] = jnp.zeros_like(dmv_ref)

        dmk_ref[...] += jnp.concatenate(dmks, axis=1) * SCALE128
        dmv_ref[...] += jnp.concatenate(dmvs, axis=1)

    q_spec = pl.BlockSpec((tq, 512), lambda b, i: (b * nq + i, 0))
    m_out = pl.BlockSpec((M, 512), lambda b, i: (b, 0))
    return pl.pallas_call(
        body,
        grid=(B, nq),
        in_specs=[pl.BlockSpec((tq, 128), lambda b, i, h=h: (b * nq + i, COL_QM + h)) for h in range(Hm)]
        + [pl.BlockSpec((M, 512), lambda b, i: (b, 0)), pl.BlockSpec((M, 512), lambda b, i: (b, 1)), q_spec],
        out_specs=[q_spec, m_out, m_out],
        out_shape=[jax.ShapeDtypeStruct((T, 512), BF16),
                   jax.ShapeDtypeStruct((B * M, 512), F32), jax.ShapeDtypeStruct((B * M, 512), F32)],
        name="mem_bwd",
    )(proj, proj, proj, proj, mkv, mkv, dy)


def _all_gather(blk, name):
    R, C = blk.shape

    def body(x_ref, out_ref, send_sems, recv_sems, local_sem):
        x, y, c = _mesh_pos()
        me, sibling = (x, y, c), (x, y, 1 - c)
        chips = [(1 - x, y), (x, 1 - y), (1 - x, 1 - y)]

        def slot(px, py, pc):
            return out_ref.at[4 * px + 2 * py + pc]

        def copy(k, block, to, src=None):
            return pltpu.make_async_remote_copy(
                src_ref=slot(*block) if src is None else src, dst_ref=slot(*block),
                send_sem=send_sems.at[k], recv_sem=recv_sems.at[k],
                device_id=to, device_id_type=pl.DeviceIdType.MESH)

        mine = pltpu.make_async_copy(x_ref, slot(*me), local_sem)
        mine.start()
        first = [copy(0, me, sibling, src=x_ref)]
        first += [copy(1 + j, me, (*chip, c), src=x_ref) for j, chip in enumerate(chips)]
        for cp in first:
            cp.start()
        passed = [copy(4 + j, (*chip, c), sibling) for j, chip in enumerate(chips)]
        for j, chip in enumerate(chips):
            copy(1 + j, (*chip, c), me).wait_recv()
            passed[j].start()
        copy(0, sibling, me).wait_recv()
        for j, chip in enumerate(chips):
            copy(4 + j, (*chip, 1 - c), me).wait_recv()
        for cp in first + passed:
            cp.wait_send()
        mine.wait()

    return pl.pallas_call(
        body,
        in_specs=[pl.BlockSpec(memory_space=pl.ANY)],
        out_specs=pl.BlockSpec(memory_space=pl.ANY),
        out_shape=jax.ShapeDtypeStruct((N_DEV, R, C), blk.dtype),
        scratch_shapes=[pltpu.SemaphoreType.DMA((7,)), pltpu.SemaphoreType.DMA((7,)), pltpu.SemaphoreType.DMA],
        name=name,
    )(blk)


_HBM = pl.BlockSpec(memory_space=pltpu.HBM)
_SEM = pl.BlockSpec(memory_space=pltpu.SEMAPHORE)


def _scatter_start(parts, land, window, carried, name):
    hosted = _Hosted(scatters=[parts], window=window)

    def body(p_ref, land_ref, c_ref, send_sems, recv_sems, local_sems, p_thru, land_thru, c_thru):
        for cp in hosted.copies([p_ref], [land_ref], send_sems, recv_sems, local_sems):
            cp.start()

    sems = (pltpu.SemaphoreType.DMA((7,)), pltpu.SemaphoreType.DMA((7,)), pltpu.SemaphoreType.DMA((1,)))
    hbm = lambda a: pltpu.HBM(a.shape, a.dtype)
    outs = pl.pallas_call(
        body, name=name,
        out_shape=sems + (hbm(parts), hbm(land), hbm(carried)),
        in_specs=(_HBM, _HBM, _HBM), out_specs=(_SEM, _SEM, _SEM, _HBM, _HBM, _HBM),
        input_output_aliases={0: 3, 1: 4, 2: 5},
        compiler_params=pltpu.CompilerParams(has_side_effects=pltpu.SideEffectType.DATAFLOW_SIDE_EFFECTING),
    )(pltpu.with_memory_space_constraint(parts, pltpu.HBM),
      pltpu.with_memory_space_constraint(land, pltpu.HBM),
      pltpu.with_memory_space_constraint(carried, pltpu.HBM))
    return (outs[:4], window), outs[4], outs[5]


def _scatter_wait(flight, land, after, name):
    (send_sems, recv_sems, local_sems, p_thru), window = flight
    hosted = _Hosted(scatters=[p_thru], window=window)

    def body(p_ref, land_ref, send, recv, local, after_ref, p_dead, got_ref):
        for cp in hosted.copies([p_ref], [land_ref], send, recv, local):
            cp.wait()

    hbm = lambda a: pltpu.HBM(a.shape, a.dtype)
    return pl.pallas_call(
        body, name=name,
        out_shape=(hbm(p_thru), hbm(land)),
        in_specs=(_HBM, _HBM, _SEM, _SEM, _SEM, pl.BlockSpec(memory_space=pl.ANY)), out_specs=(_HBM, _HBM),
        input_output_aliases={0: 0, 1: 1},
        compiler_params=pltpu.CompilerParams(has_side_effects=pltpu.SideEffectType.DATAFLOW_SIDE_EFFECTING),
    )(p_thru, land, send_sems, recv_sems, local_sems, after)[1]


def _adamw(parts, w, m, v, name):
    R, C = w.shape
    tr = R
    for cand in (368, 352, 256, 176, 128, 64, 32, 16, 8):
        if R % cand == 0 and cand * C * 4 <= 1536 * 1024:
            tr = cand
            break
    c1 = 1.0 - ADAM_B1 ** ADAM_STEP
    c2 = 1.0 - ADAM_B2 ** ADAM_STEP

    def body(p_ref, w_ref, m_ref, v_ref, g_ref, d_ref, nm_ref, nv_ref):
        g = p_ref[0].astype(F32)
        for d in range(1, N_DEV):
            g = g + p_ref[d].astype(F32)
        nm = ADAM_B1 * m_ref[...] + (1.0 - ADAM_B1) * g
        nv = ADAM_B2 * v_ref[...] + (1.0 - ADAM_B2) * (g * g)
        g_ref[...] = g
        nm_ref[...] = nm
        nv_ref[...] = nv
        d_ref[...] = -ADAM_LR * ((nm / c1) / (jnp.sqrt(nv / c2) + ADAM_EPS) + ADAM_WD * w_ref[...])

    row = pl.BlockSpec((tr, C), lambda i: (i, 0))
    out = jax.ShapeDtypeStruct((R, C), F32)
    return pl.pallas_call(
        body,
        grid=(R // tr,),
        in_specs=[pl.BlockSpec((N_DEV, tr, C), lambda i: (0, i, 0)), row, row, row],
        out_specs=[row] * 4,
        out_shape=[out] * 4,
        compiler_params=_params(),
        name=name,
    )(parts, w, m, v)


def _col_shards(g):
    R, C8 = g.shape
    return g.reshape(R, N_DEV, C8 // N_DEV).transpose(1, 0, 2)


def _row_shards(g):
    R8, C = g.shape
    return g.reshape(N_DEV, R8 // N_DEV, C)


def _cols_full(gathered):
    n, R, C = gathered.shape
    return gathered.transpose(1, 0, 2).reshape(R, n * C)


_BIG = ("w_in", "w_mem_kv", "w_branch_swa", "w_branch_sb", "w_branch_mem", "w_out", "w_gate", "w_up", "w_down")
_COL_SHARDED = ("w_branch_swa", "w_branch_sb", "w_branch_mem")
_TRANSPOSED = ("w_in", "w_gate", "w_up")
_SMALL = ("ln_mix_pre", "ln_mix_post", "swa_sinks", "rel_bias", "ln_mem", "ln_ffn_pre", "ln_ffn_post")
_ORDER = ("ln_mix_pre", "ln_mix_post", "w_in", "swa_sinks", "rel_bias", "ln_mem", "w_mem_kv", "w_branch_swa",
          "w_branch_sb", "w_branch_mem", "w_out", "ln_ffn_pre", "ln_ffn_post", "w_gate", "w_up", "w_down")


def _pack_small(d, last=None):
    rows = [d["ln_mix_pre"], d["ln_mix_post"], d["ln_mem"], d["ln_ffn_pre"], d["ln_ffn_post"],
            jnp.pad(d["swa_sinks"].reshape(1, -1), ((0, 0), (0, D_MODEL - SWA_Q_HEADS))),
            jnp.pad(d["rel_bias"].reshape(1, -1), ((0, 0), (0, D_MODEL - N_BUCKETS * SWA_Q_HEADS))),
            jnp.zeros((1, D_MODEL), F32) if last is None else last]
    return jnp.concatenate([r.astype(F32) for r in rows], axis=0)


def _unpack_small(a):
    return dict(ln_mix_pre=a[0:1], ln_mix_post=a[1:2], ln_mem=a[2:3], ln_ffn_pre=a[3:4], ln_ffn_post=a[4:5],
                swa_sinks=a[5:6, :SWA_Q_HEADS],
                rel_bias=a[6, :N_BUCKETS * SWA_Q_HEADS].reshape(N_BUCKETS, SWA_Q_HEADS))


def kernel(x, mem, ln_mix_pre, ln_mix_post, w_in, swa_sinks, rel_bias, ln_mem, w_mem_kv, w_branch_swa, w_branch_sb, w_branch_mem, w_out, ln_ffn_pre, ln_ffn_post, w_gate, w_up, w_down, loss_target, m_ln_mix_pre, m_ln_mix_post, m_w_in, m_swa_sinks, m_rel_bias, m_ln_mem, m_w_mem_kv, m_w_branch_swa, m_w_branch_sb, m_w_branch_mem, m_w_out, m_ln_ffn_pre, m_ln_ffn_post, m_w_gate, m_w_up, m_w_down, v_ln_mix_pre, v_ln_mix_post, v_w_in, v_swa_sinks, v_rel_bias, v_ln_mem, v_w_mem_kv, v_w_branch_swa, v_w_branch_sb, v_w_branch_mem, v_w_out, v_ln_ffn_pre, v_ln_ffn_post, v_w_gate, v_w_up, v_w_down):
    w = dict(ln_mix_pre=ln_mix_pre, ln_mix_post=ln_mix_post, w_in=w_in[0], swa_sinks=swa_sinks, rel_bias=rel_bias,
             ln_mem=ln_mem, w_mem_kv=w_mem_kv[0], w_branch_swa=w_branch_swa[0], w_branch_sb=w_branch_sb[0],
             w_branch_mem=w_branch_mem[0], w_out=w_out[0], ln_ffn_pre=ln_ffn_pre, ln_ffn_post=ln_ffn_post,
             w_gate=w_gate[0], w_up=w_up[0], w_down=w_down[0])
    mom = dict(ln_mix_pre=m_ln_mix_pre, ln_mix_post=m_ln_mix_post, w_in=m_w_in[0], swa_sinks=m_swa_sinks,
               rel_bias=m_rel_bias, ln_mem=m_ln_mem, w_mem_kv=m_w_mem_kv[0], w_branch_swa=m_w_branch_swa[0],
               w_branch_sb=m_w_branch_sb[0], w_branch_mem=m_w_branch_mem[0], w_out=m_w_out[0],
               ln_ffn_pre=m_ln_ffn_pre, ln_ffn_post=m_ln_ffn_post, w_gate=m_w_gate[0], w_up=m_w_up[0],
               w_down=m_w_down[0])
    var = dict(ln_mix_pre=v_ln_mix_pre, ln_mix_post=v_ln_mix_post, w_in=v_w_in[0], swa_sinks=v_swa_sinks,
               rel_bias=v_rel_bias, ln_mem=v_ln_mem, w_mem_kv=v_w_mem_kv[0], w_branch_swa=v_w_branch_swa[0],
               w_branch_sb=v_w_branch_sb[0], w_branch_mem=v_w_branch_mem[0], w_out=v_w_out[0],
               ln_ffn_pre=v_ln_ffn_pre, ln_ffn_post=v_ln_ffn_post, w_gate=v_w_gate[0], w_up=v_w_up[0],
               w_down=v_w_down[0])
    B, S, D = x.shape
    M = mem.shape[1]
    T = B * S
    F = D_FF
    x2 = x.reshape(T, D)
    mem2 = mem.reshape(B * M, D)
    t2 = loss_target.reshape(T, D)
    buckets = jnp.asarray(_swa_buckets())
    for d in (w, mom, var):
        for n in _TRANSPOSED:
            d[n] = d[n].T
    wb = {n: w[n].astype(BF16) for n in _BIG}
    full = {}

    def landed(names, got):
        for n, g in zip(names, got):
            full[n] = _cols_full(g) if n in _COL_SHARDED else g.reshape(-1, g.shape[-1])

    def shards(n, g):
        return _col_shards(g) if n in _COL_SHARDED else _row_shards(g)

    landed(["w_in"], [_all_gather(wb["w_in"], "ag_w_in")])
    u = _rms_fwd(x2, ln_mix_pre, "rms_mix_pre")
    early = ["w_mem_kv", "w_branch_swa", "w_branch_sb", "w_branch_mem"]
    proj, got = _matmul([(u, full["w_in"])], "nt", BF16, 512, IN_W // 2, D, "proj_in",
                        hosted=_Hosted(gathers=[wb[n] for n in early]))
    landed(early, got)
    mn = _rms_fwd(mem2, ln_mem, "rms_mem")
    mkv = _matmul([(mn, full["w_mem_kv"])], "nn", BF16, 512, 1024, D, "proj_mem")
    bias_tab = _swa_bias_table(rel_bias, buckets)
    y_swa, got = _swa_fwd(proj, swa_sinks, bias_tab, B, S, hosted=_Hosted(gathers=[wb["w_out"]]))
    landed(["w_out"], got)
    late = ["w_gate", "w_up", "w_down"]
    y_sb, rtot, got = _sb_fwd(proj, B, S, hosted=_Hosted(gathers=[wb[n] for n in late]))
    landed(late, got)
    y_mem = _mem_fwd(proj, mkv, B, S, M)
    wbs = (full["w_branch_swa"], full["w_branch_sb"], full["w_branch_mem"])
    merged, p_swa, p_sb, p_mem = _branch_gate(proj, (y_swa, y_sb, y_mem), wbs)
    mix, h1, u2 = _post_pre(x2, merged, full["w_out"], ln_mix_post, ln_ffn_pre)
    a, zg, zu = _ffn_up(u2, full["w_gate"], full["w_up"])
    dffn, dh2, loss_tile, d_ln_ffn_post = _loss_head(a, full["w_down"], h1, t2, ln_ffn_post)

    part = {}
    part["w_down"] = _matmul([(a, dffn)], "tn", BF16, F // 2, 1024, 1024, "dw_down")
    dzg, dzu = _ffn_down_bwd(dffn, full["w_down"], zg, zu)
    part["w_gate"] = _matmul([(dzg, u2)], "tn", BF16, F // 2, 1024, 1024, "dw_gate")
    part["w_up"] = _matmul([(dzu, u2)], "tn", BF16, F // 2, 1024, 1024, "dw_up")
    du2 = _matmul([(dzg, full["w_gate"]), (dzu, full["w_up"])], "nn", BF16, 512, 1024, F // 2, "d_u2")
    dh1, dmix, d_ln_ffn_pre, d_ln_mix_post = _mid_bwd(h1, du2, dh2, mix, ln_ffn_pre, ln_mix_post)
    part["w_out"] = _matmul([(merged, dmix)], "tn", BF16, 1024, 1024, 1024, "dw_out")
    (dp_swa, dp_sb, dp_mem, dg0, dg1, dg2, dy_swa, dy_sb, dy_mem) = _gate_bwd(
        dmix, full["w_out"], (p_swa, p_sb, p_mem), proj, wbs)
    part["w_branch_swa"] = _matmul([(y_swa, dp_swa)], "tn", BF16, 512, 1024, 1024, "dw_branch_swa")
    part["w_branch_sb"] = _matmul([(y_sb, dp_sb)], "tn", BF16, 512, 1024, 1024, "dw_branch_sb")
    part["w_branch_mem"] = _matmul([(y_mem, dp_mem)], "tn", BF16, 512, 1024, 1024, "dw_branch_mem")
    dqm, dmk, dmv = _mem_bwd(proj, mkv, dy_mem, B, S, M)
    dmkv = jnp.concatenate([dmk, dmv], axis=1).astype(BF16)
    part["w_mem_kv"] = _matmul([(mn, dmkv)], "tn", BF16, 1024, 1024, 512, "dw_mem_kv")
    dmn = _matmul([(dmkv, full["w_mem_kv"])], "nt", F32, 512, 1024, 1024, "d_mn")
    d_ln_mem = _gain_grad(mem2, dmn)
    dqa, dka, dva, dbias, dsink = _swa_bwd(proj, dy_swa, swa_sinks, bias_tab, B, S)
    behind_sb = ["w_down", "w_gate", "w_up", "w_out", "w_branch_swa", "w_branch_sb", "w_branch_mem", "w_mem_kv"]
    dqb, dkb, dvb, got = _sb_bwd(proj, dy_sb, rtot, B, S,
                                 hosted=_Hosted(scatters=[shards(n, part[n]) for n in behind_sb]))
    recv = dict(zip(behind_sb, got))
    d_rel_bias = _swa_bias_grad(dbias, buckets)[:, :SWA_Q_HEADS]
    d_sinks = dsink[:, 0].reshape(1, SWA_Q_HEADS)
    dproj = jnp.concatenate([dqa, dka.astype(BF16), dva.astype(BF16), dqb, dkb.astype(BF16), dvb.astype(BF16),
                             dqm, dg0, dg1, dg2], axis=1)
    half = D // 2
    land = lax.empty((N_DEV, IN_W // N_DEV, D), BF16)
    flights = []
    for t in range(2):
        dw_half = _matmul([(dproj, u)], "tn", BF16, IN_W // 2, half, 1024, "dw_in_%d" % t, n_cols=half, n_off=t)
        flight, land, dproj = _scatter_start(_row_shards(dw_half), land, (t * half, half), dproj,
                                             "rs_w_in_start_%d" % t)
        flights.append(flight)
    du = _matmul([(dproj, full["w_in"])], "nn", BF16, 512, 1024, IN_W // 2, "d_u")
    grad_x, d_ln_mix_pre = _pre_bwd(x2, du, dh1, ln_mix_pre)

    out = {n: _adamw(recv[n], w[n], mom[n], var[n], "adamw_" + n) for n in _BIG if n != "w_in"}
    small_grads = dict(ln_mix_pre=d_ln_mix_pre, ln_mix_post=d_ln_mix_post, swa_sinks=d_sinks, rel_bias=d_rel_bias,
                       ln_mem=d_ln_mem, ln_ffn_pre=d_ln_ffn_pre, ln_ffn_post=d_ln_ffn_post)
    small_parts = _all_gather(_pack_small(small_grads, jnp.tile(loss_tile[0:1], (1, D // 128))), "ag_small")
    res = _adamw(small_parts, _pack_small(w), _pack_small(mom), _pack_small(var), "adamw_small")
    loss = res[0][7, 0]
    small = [_unpack_small(r) for r in res]
    for n in _SMALL:
        out[n] = tuple(s[n] for s in small)
    for t, flight in enumerate(flights):
        land = _scatter_wait(flight, land, res[0], "rs_w_in_wait_%d" % t)
    out["w_in"] = _adamw(land, w["w_in"], mom["w_in"], var["w_in"], "adamw_w_in")
    for n in _TRANSPOSED:
        out[n] = tuple(o.T for o in out[n])

    like = dict(ln_mix_pre=ln_mix_pre, ln_mix_post=ln_mix_post, w_in=w_in, swa_sinks=swa_sinks, rel_bias=rel_bias,
                ln_mem=ln_mem, w_mem_kv=w_mem_kv, w_branch_swa=w_branch_swa, w_branch_sb=w_branch_sb,
                w_branch_mem=w_branch_mem, w_out=w_out, ln_ffn_pre=ln_ffn_pre, ln_ffn_post=ln_ffn_post,
                w_gate=w_gate, w_up=w_up, w_down=w_down)
    result = [loss, grad_x.reshape(B, S, D)]
    for k in range(4):
        result += [out[n][k].reshape(like[n].shape) for n in _ORDER]
    return tuple(result)
```

```python
import functools
import math

import numpy as np
import jax
import jax.numpy as jnp
from jax import lax
from jax.experimental import pallas as pl
from jax.experimental.pallas import tpu as pltpu

F32 = jnp.float32
BF16 = jnp.bfloat16

N_DEV = 8
D_MODEL = 1024
BLOCK = 128
EPS = 1e-6
HEAD_DIM = 64
SWA_Q_HEADS = 8
SWA_WINDOW = 128
N_BUCKETS = 32
MAX_DISTANCE = 128
MEM_HEADS = 4
MEM_HEAD_DIM = 128
D_FF = 2816
IN_W = 5888
COL_QA, COL_KA, COL_VA, COL_QB, COL_KB, COL_VB, COL_QM, COL_GL = 0, 4, 5, 6, 10, 14, 18, 22
SCALE64 = HEAD_DIM ** -0.5
SCALE128 = MEM_HEAD_DIM ** -0.5
NEG = -1e30

ADAM_LR = 0.001
ADAM_B1 = 0.9
ADAM_B2 = 0.999
ADAM_EPS = 1e-08
ADAM_WD = 0.01
ADAM_STEP = 10

VMEM_LIMIT_BYTES = 56 * 1024 * 1024


def _params(**kw):
    return pltpu.CompilerParams(vmem_limit_bytes=VMEM_LIMIT_BYTES, **kw)


def _dot(a, b):
    return jnp.dot(a, b, preferred_element_type=F32)


def _dot_nt(a, b):
    return lax.dot_general(a, b, (((1,), (1,)), ((), ())), preferred_element_type=F32)


def _dot_tn(a, b):
    return lax.dot_general(a, b, (((0,), (0,)), ((), ())), preferred_element_type=F32)


def _dot_split(x, m2):
    hi = x.astype(BF16)
    lo = (x - hi.astype(F32)).astype(BF16)
    return _dot(jnp.concatenate([hi, lo], axis=1), m2)


def _mesh_pos():
    return lax.axis_index("x"), lax.axis_index("y"), lax.axis_index("c")


class _Hosted:
    def __init__(self, gathers=(), scatters=(), window=None):
        self.items = [("g", a) for a in gathers] + [("s", a) for a in scatters]
        self.n = len(self.items)
        self.window = window

    def operands(self):
        return [a for _, a in self.items]

    def specs(self):
        return [pl.BlockSpec(memory_space=pl.ANY)] * self.n

    def out_shapes(self):
        return [jax.ShapeDtypeStruct(((N_DEV,) + a.shape) if kind == "g" else a.shape, a.dtype)
                for kind, a in self.items]

    def scratch(self):
        return [pltpu.SemaphoreType.DMA((7 * self.n,)), pltpu.SemaphoreType.DMA((7 * self.n,)),
                pltpu.SemaphoreType.DMA((self.n,))]

    def copies(self, in_refs, out_refs, send_sems, recv_sems, local_sems):
        x, y, c = _mesh_pos()
        me = 4 * x + 2 * y + c
        out = []
        for t, (kind, _) in enumerate(self.items):
            own = in_refs[t] if kind == "g" else in_refs[t].at[me]
            dst = out_refs[t].at[me]
            if self.window is not None:
                dst = out_refs[t].at[me, :, pl.ds(*self.window)]
            out.append(pltpu.make_async_copy(own, dst, local_sems.at[t]))
            for k in range(1, N_DEV):
                px, py, pc = x ^ (k >> 2), y ^ ((k >> 1) & 1), c ^ (k & 1)
                src = in_refs[t] if kind == "g" else in_refs[t].at[4 * px + 2 * py + pc]
                out.append(pltpu.make_async_remote_copy(
                    src_ref=src, dst_ref=dst,
                    send_sem=send_sems.at[7 * t + k - 1], recv_sem=recv_sems.at[7 * t + k - 1],
                    device_id=(px, py, pc), device_id_type=pl.DeviceIdType.MESH))
        return out


def _host(body, n_in, n_out, hosted, grid):
    if hosted is None:
        return body
    nc = hosted.n

    def wrapped(*refs):
        ins = refs[:n_in]
        cin = refs[n_in:n_in + nc]
        outs = refs[n_in + nc:n_in + nc + n_out]
        cout = refs[n_in + nc + n_out:n_in + 2 * nc + n_out]
        scratch = refs[n_in + 2 * nc + n_out:len(refs) - 3]
        sems = refs[len(refs) - 3:]
        ids = [pl.program_id(d) for d in range(len(grid))]
        first = functools.reduce(lambda a, b: a & b, [i == 0 for i in ids])
        last = functools.reduce(lambda a, b: a & b, [i == g - 1 for i, g in zip(ids, grid)])

        @pl.when(first)
        def _():
            for cp in hosted.copies(cin, cout, *sems):
                cp.start()

        body(*ins, *outs, *scratch)

        @pl.when(last)
        def _():
            for cp in hosted.copies(cin, cout, *sems):
                cp.wait()

    return wrapped


def _hosted_call(body, grid, in_specs, out_specs, out_shape, scratch_shapes, hosted, name, args):
    n_out = len(out_specs)
    if hosted is None:
        outs = pl.pallas_call(body, grid=grid, in_specs=in_specs, out_specs=out_specs, out_shape=out_shape,
                              scratch_shapes=scratch_shapes, compiler_params=_params(), name=name)(*args)
        return list(outs), []
    outs = pl.pallas_call(
        _host(body, len(in_specs), n_out, hosted, grid),
        grid=grid,
        in_specs=list(in_specs) + hosted.specs(),
        out_specs=list(out_specs) + hosted.specs(),
        out_shape=list(out_shape) + hosted.out_shapes(),
        scratch_shapes=list(scratch_shapes) + hosted.scratch(),
        compiler_params=_params(),
        name=name,
    )(*args, *hosted.operands())
    return list(outs[:n_out]), list(outs[n_out:])


_DIMS = {"nn": (((1,), (0,)), ((), ())), "nt": (((1,), (1,)), ((), ())), "tn": (((0,), (0,)), ((), ()))}


def _matmul(pairs, mode, out_dtype, tm, tn, tk, name, hosted=None, n_cols=None, n_off=0):
    a0, b0 = pairs[0]
    if mode == "nn":
        (M, K), N = a0.shape, b0.shape[1]
    elif mode == "nt":
        (M, K), N = a0.shape, b0.shape[0]
    else:
        (K, M), N = a0.shape, b0.shape[1]
    N = N if n_cols is None else n_cols
    tm, tn, tk = min(tm, M), min(tn, N), min(tk, K)
    assert M % tm == 0 and N % tn == 0 and K % tk == 0, (name, M, N, K, tm, tn, tk)
    nm, nn, nk = M // tm, N // tn, K // tk
    npair = len(pairs)
    dims = _DIMS[mode]

    def body(*refs):
        ab = refs[:2 * npair]
        o_ref = refs[2 * npair]
        acc_ref = refs[2 * npair + 1]
        k = pl.program_id(2)
        part = lax.dot_general(ab[0][...], ab[1][...], dims, preferred_element_type=F32)
        for q in range(1, npair):
            part += lax.dot_general(ab[2 * q][...], ab[2 * q + 1][...], dims, preferred_element_type=F32)
        if nk == 1:
            o_ref[...] = part.astype(o_ref.dtype)
        else:
            @pl.when(k == 0)
            def _():
                acc_ref[...] = part

            @pl.when(k > 0)
            def _():
                acc_ref[...] += part

            @pl.when(k == nk - 1)
            def _():
                o_ref[...] = acc_ref[...].astype(o_ref.dtype)

    if mode == "nn":
        a_spec = pl.BlockSpec((tm, tk), lambda n, m, k: (m, k))
        b_spec = pl.BlockSpec((tk, tn), lambda n, m, k: (k, n + n_off))
    elif mode == "nt":
        a_spec = pl.BlockSpec((tm, tk), lambda n, m, k: (m, k))
        b_spec = pl.BlockSpec((tn, tk), lambda n, m, k: (n + n_off, k))
    else:
        a_spec = pl.BlockSpec((tk, tm), lambda n, m, k: (k, m))
        b_spec = pl.BlockSpec((tk, tn), lambda n, m, k: (k, n + n_off))
    args = [t for pr in pairs for t in pr]
    outs, moved = _hosted_call(
        body, (nn, nm, nk), [a_spec, b_spec] * npair, [pl.BlockSpec((tm, tn), lambda n, m, k: (m, n))],
        [jax.ShapeDtypeStruct((M, N), out_dtype)], [pltpu.VMEM((tm, tn) if nk > 1 else (8, 128), F32)],
        hosted, name, args)
    return outs[0] if hosted is None else (outs[0], moved)


def _rms_fwd(x, g, name):
    T, D = x.shape
    tr = min(512, T)

    def body(x_ref, g_ref, u_ref):
        xf = x_ref[...]
        r = lax.rsqrt(jnp.mean(xf * xf, axis=-1, keepdims=True) + EPS)
        u_ref[...] = ((xf * r) * g_ref[...]).astype(u_ref.dtype)

    return pl.pallas_call(
        body,
        grid=(T // tr,),
        in_specs=[pl.BlockSpec((tr, D), lambda i: (i, 0)), pl.BlockSpec((1, D), lambda i: (0, 0))],
        out_specs=pl.BlockSpec((tr, D), lambda i: (i, 0)),
        out_shape=jax.ShapeDtypeStruct((T, D), BF16),
        name=name,
    )(x, g)


def _rms_bwd_terms(xin, g, dy):
    r = lax.rsqrt(jnp.mean(xin * xin, axis=-1, keepdims=True) + EPS)
    xh = xin * r
    dg = jnp.sum(dy * xh, axis=0, keepdims=True)
    dxh = dy * g
    dx = r * (dxh - xh * jnp.mean(dxh * xh, axis=-1, keepdims=True))
    return dx, dg


GATE_TC = 256


def _branch_gate(proj, ys, wbs):
    T = proj.shape[0]
    D = D_MODEL
    tr, tc = min(1024, T), GATE_TC
    nc = D // tc
    gl0 = COL_GL * 128 // tc

    def body(ya, yb, yc, wa, wb, wc, g0, g1, g2, merged_ref, pa, pb, pc):
        acc = jnp.zeros((tr, tc), F32)
        for y_ref, w_ref, g_ref, p_ref in ((ya, wa, g0, pa), (yb, wb, g1, pb), (yc, wc, g2, pc)):
            p = _dot(y_ref[...], w_ref[...])
            p_ref[...] = p.astype(p_ref.dtype)
            acc += jax.nn.sigmoid(g_ref[...].astype(F32)) * p
        merged_ref[...] = acc.astype(merged_ref.dtype)

    y_spec = pl.BlockSpec((tr, 512), lambda i, n: (i, 0))
    w_spec = pl.BlockSpec((512, tc), lambda i, n: (0, n))
    o_spec = pl.BlockSpec((tr, tc), lambda i, n: (i, n))
    gl_specs = [pl.BlockSpec((tr, tc), lambda i, n, j=j: (i, gl0 + j * nc + n)) for j in range(3)]
    out = jax.ShapeDtypeStruct((T, D), BF16)
    return pl.pallas_call(
        body,
        grid=(T // tr, nc),
        in_specs=[y_spec] * 3 + [w_spec] * 3 + gl_specs,
        out_specs=[o_spec] * 4,
        out_shape=[out] * 4,
        compiler_params=_params(),
        name="branch_gate",
    )(*ys, *wbs, proj, proj, proj)


def _post_pre(x, merged, w_out, g_post, g_pre):
    T, D = x.shape
    tr = 512

    def body(x_ref, m_ref, w_ref, gp_ref, gq_ref, mix_ref, h1_ref, u2_ref):
        mx = _dot(m_ref[...], w_ref[...])
        mix_ref[...] = mx
        r = lax.rsqrt(jnp.mean(mx * mx, axis=-1, keepdims=True) + EPS)
        h1 = x_ref[...] + (mx * r) * gp_ref[...]
        h1_ref[...] = h1
        r2 = lax.rsqrt(jnp.mean(h1 * h1, axis=-1, keepdims=True) + EPS)
        u2_ref[...] = ((h1 * r2) * gq_ref[...]).astype(u2_ref.dtype)

    row = pl.BlockSpec((tr, D), lambda i: (i, 0))
    vec = pl.BlockSpec((1, D), lambda i: (0, 0))
    f32 = jax.ShapeDtypeStruct((T, D), F32)
    return pl.pallas_call(
        body,
        grid=(T // tr,),
        in_specs=[row, row, pl.BlockSpec((D, D), lambda i: (0, 0)), vec, vec],
        out_specs=[row, row, row],
        out_shape=[f32, f32, jax.ShapeDtypeStruct((T, D), BF16)],
        compiler_params=_params(),
        name="post_pre",
    )(x, merged, w_out, g_post, g_pre)


def _ffn_up(u2, w_gate_t, w_up_t, hosted=None):
    T, D = u2.shape
    F = D_FF
    tm, tn = 512, F // 2

    def body(u_ref, wg_ref, wu_ref, a_ref, zg_ref, zu_ref):
        u = u_ref[...]
        zg = _dot_nt(u, wg_ref[...])
        zu = _dot_nt(u, wu_ref[...])
        a_ref[...] = (zg * jax.nn.sigmoid(zg) * zu).astype(a_ref.dtype)
        zg_ref[...] = zg.astype(zg_ref.dtype)
        zu_ref[...] = zu.astype(zu_ref.dtype)

    o_spec = pl.BlockSpec((tm, tn), lambda n, m: (m, n))
    out = jax.ShapeDtypeStruct((T, F), BF16)
    outs, moved = _hosted_call(
        body, (F // tn, T // tm),
        [pl.BlockSpec((tm, D), lambda n, m: (m, 0)),
         pl.BlockSpec((tn, D), lambda n, m: (n, 0)),
         pl.BlockSpec((tn, D), lambda n, m: (n, 0))],
        [o_spec] * 3, [out] * 3, [], hosted, "ffn_up", (u2, w_gate_t, w_up_t))
    return (*outs, moved)


def _loss_head(a, w_down, h1, target, g_post):
    T, D = h1.shape
    F = a.shape[1]
    tr = 512

    def body(a_ref, w_ref, h1_ref, t_ref, g_ref, dffn_ref, dh2_ref, loss_ref, dg_ref):
        i = pl.program_id(0)
        f = _dot(a_ref[...], w_ref[...])
        g = g_ref[...]
        r = lax.rsqrt(jnp.mean(f * f, axis=-1, keepdims=True) + EPS)
        xh = f * r
        err = (h1_ref[...] + xh * g) - t_ref[...]
        part = 0.5 * jnp.sum(jnp.mean(err * err, axis=-1, keepdims=True), axis=0, keepdims=True)
        dh2 = err * (1.0 / D)
        dh2_ref[...] = dh2
        dgp = jnp.sum(dh2 * xh, axis=0, keepdims=True)
        dxh = dh2 * g
        dffn_ref[...] = (r * (dxh - xh * jnp.mean(dxh * xh, axis=-1, keepdims=True))).astype(dffn_ref.dtype)

        @pl.when(i == 0)
        def _():
            loss_ref[...] = jnp.zeros_like(loss_ref)
            dg_ref[...] = jnp.zeros_like(dg_ref)

        loss_ref[...] += jnp.broadcast_to(part, loss_ref.shape)
        dg_ref[...] += dgp

    row = pl.BlockSpec((tr, D), lambda i: (i, 0))
    vec = pl.BlockSpec((1, D), lambda i: (0, 0))
    return pl.pallas_call(
        body,
        grid=(T // tr,),
        in_specs=[pl.BlockSpec((tr, F), lambda i: (i, 0)), pl.BlockSpec((F, D), lambda i: (0, 0)), row, row, vec],
        out_specs=[row, row, pl.BlockSpec((8, 128), lambda i: (0, 0)), vec],
        out_shape=[jax.ShapeDtypeStruct((T, D), BF16), jax.ShapeDtypeStruct((T, D), F32),
                   jax.ShapeDtypeStruct((8, 128), F32), jax.ShapeDtypeStruct((1, D), F32)],
        compiler_params=_params(),
        name="loss_head",
    )(a, w_down, h1, target, g_post)


def _ffn_down_bwd(dffn, wd, zg, zu):
    T, D = dffn.shape
    F = D_FF
    tm, tn = 512, F // 2

    def body(d_ref, w_ref, zg_ref, zu_ref, dzg_ref, dzu_ref):
        d = d_ref[...]
        for lo in range(0, tn, 512):
            cols = slice(lo, min(lo + 512, tn))
            da = _dot_nt(d, w_ref[cols, :])
            zg = zg_ref[:, cols].astype(F32)
            zu = zu_ref[:, cols].astype(F32)
            s = jax.nn.sigmoid(zg)
            dzu_ref[:, cols] = (da * (zg * s)).astype(dzu_ref.dtype)
            dzg_ref[:, cols] = (da * zu * (s * (1.0 + zg * (1.0 - s)))).astype(dzg_ref.dtype)

    z_spec = pl.BlockSpec((tm, tn), lambda n, m: (m, n))
    out = jax.ShapeDtypeStruct((T, F), BF16)
    return pl.pallas_call(
        body,
        grid=(F // tn, T // tm),
        in_specs=[pl.BlockSpec((tm, D), lambda n, m: (m, 0)), pl.BlockSpec((tn, D), lambda n, m: (n, 0)),
                  z_spec, z_spec],
        out_specs=[z_spec, z_spec],
        out_shape=[out, out],
        compiler_params=_params(),
        name="ffn_down_bwd",
    )(dffn, wd, zg, zu)


def _mid_bwd(h1, du2, dh2, mix, g_pre, g_post):
    T, D = h1.shape
    tr = 512

    def body(h1_ref, du2_ref, dh2_ref, mix_ref, gq_ref, gp_ref, dh1_ref, dmix_ref, dgq_ref, dgp_ref):
        i = pl.program_id(0)
        dx, dgq = _rms_bwd_terms(h1_ref[...], gq_ref[...], du2_ref[...])
        dh1 = dh2_ref[...] + dx
        dh1_ref[...] = dh1
        dmix, dgp = _rms_bwd_terms(mix_ref[...], gp_ref[...], dh1)
        dmix_ref[...] = dmix.astype(dmix_ref.dtype)

        @pl.when(i == 0)
        def _():
            dgq_ref[...] = jnp.zeros_like(dgq_ref)
            dgp_ref[...] = jnp.zeros_like(dgp_ref)

        dgq_ref[...] += dgq
        dgp_ref[...] += dgp

    row = pl.BlockSpec((tr, D), lambda i: (i, 0))
    vec = pl.BlockSpec((1, D), lambda i: (0, 0))
    return pl.pallas_call(
        body,
        grid=(T // tr,),
        in_specs=[row, row, row, row, vec, vec],
        out_specs=[row, row, vec, vec],
        out_shape=[jax.ShapeDtypeStruct((T, D), F32), jax.ShapeDtypeStruct((T, D), BF16),
                   jax.ShapeDtypeStruct((1, D), F32), jax.ShapeDtypeStruct((1, D), F32)],
        compiler_params=_params(),
        name="mid_bwd",
    )(h1, du2, dh2, mix, g_pre, g_post)


def _gate_bwd(dmix, w_out, ps, proj, wbs):
    T, D = dmix.shape
    tr, tc = min(1024, T), GATE_TC
    nc = D // tc
    gl0 = COL_GL * 128 // tc

    def body(dmix_ref, wo_ref, pa, pb, pc, g0, g1, g2, wa, wb, wc, dpa, dpb, dpc, dga, dgb, dgc, dya, dyb, dyc,
             acc_a, acc_b, acc_c):
        n = pl.program_id(1)
        dm = _dot_nt(dmix_ref[...], wo_ref[...])
        for p_ref, g_ref, w_ref, dp_ref, dg_ref, dy_ref, acc_ref in (
                (pa, g0, wa, dpa, dga, dya, acc_a), (pb, g1, wb, dpb, dgb, dyb, acc_b),
                (pc, g2, wc, dpc, dgc, dyc, acc_c)):
            s = jax.nn.sigmoid(g_ref[...].astype(F32))
            dp = (dm * s).astype(BF16)
            dp_ref[...] = dp
            dg_ref[...] = (dm * p_ref[...].astype(F32) * (s * (1.0 - s))).astype(dg_ref.dtype)
            part = _dot_nt(dp, w_ref[...])

            @pl.when(n == 0)
            def _():
                acc_ref[...] = part

            @pl.when(n > 0)
            def _():
                acc_ref[...] += part

            @pl.when(n == nc - 1)
            def _():
                dy_ref[...] = acc_ref[...].astype(dy_ref.dtype)

    col = pl.BlockSpec((tr, tc), lambda i, n: (i, n))
    y_spec = pl.BlockSpec((tr, 512), lambda i, n: (i, 0))
    w_spec = pl.BlockSpec((512, tc), lambda i, n: (0, n))
    gl_specs = [pl.BlockSpec((tr, tc), lambda i, n, j=j: (i, gl0 + j * nc + n)) for j in range(3)]
    big = jax.ShapeDtypeStruct((T, D), BF16)
    small = jax.ShapeDtypeStruct((T, 512), BF16)
    return pl.pallas_call(
        body,
        grid=(T // tr, nc),
        in_specs=[pl.BlockSpec((tr, D), lambda i, n: (i, 0)), pl.BlockSpec((tc, D), lambda i, n: (n, 0))]
        + [col] * 3 + gl_specs + [w_spec] * 3,
        out_specs=[col] * 6 + [y_spec] * 3,
        out_shape=[big] * 6 + [small] * 3,
        scratch_shapes=[pltpu.VMEM((tr, 512), F32)] * 3,
        compiler_params=_params(),
        name="gate_bwd",
    )(dmix, w_out, *ps, proj, proj, proj, *wbs)


def _pre_bwd(x, du, dh1, g):
    T, D = x.shape
    tr = 512

    def body(x_ref, du_ref, dh1_ref, g_ref, gx_ref, dg_ref):
        i = pl.program_id(0)
        dx, dg = _rms_bwd_terms(x_ref[...], g_ref[...], du_ref[...])
        gx_ref[...] = dh1_ref[...] + dx

        @pl.when(i == 0)
        def _():
            dg_ref[...] = jnp.zeros_like(dg_ref)

        dg_ref[...] += dg

    row = pl.BlockSpec((tr, D), lambda i: (i, 0))
    vec = pl.BlockSpec((1, D), lambda i: (0, 0))
    return pl.pallas_call(
        body,
        grid=(T // tr,),
        in_specs=[row, row, row, vec],
        out_specs=[row, vec],
        out_shape=[jax.ShapeDtypeStruct((T, D), F32), jax.ShapeDtypeStruct((1, D), F32)],
        compiler_params=_params(),
        name="pre_bwd",
    )(x, du, dh1, g)


def _gain_grad(xin, dy):
    T, D = xin.shape
    tr = min(512, T)

    def body(x_ref, dy_ref, dg_ref):
        i = pl.program_id(0)
        xf = x_ref[...]
        r = lax.rsqrt(jnp.mean(xf * xf, axis=-1, keepdims=True) + EPS)

        @pl.when(i == 0)
        def _():
            dg_ref[...] = jnp.zeros_like(dg_ref)

        dg_ref[...] += jnp.sum(dy_ref[...] * (xf * r), axis=0, keepdims=True)

    row = pl.BlockSpec((tr, D), lambda i: (i, 0))
    return pl.pallas_call(
        body,
        grid=(T // tr,),
        in_specs=[row, row],
        out_specs=pl.BlockSpec((1, D), lambda i: (0, 0)),
        out_shape=jax.ShapeDtypeStruct((1, D), F32),
        name="gain_grad",
    )(xin, dy)


def _swa_buckets():
    dist = (np.arange(BLOCK)[:, None] + BLOCK) - np.arange(2 * BLOCK)[None, :]
    max_exact = N_BUCKETS // 2
    d = np.maximum(dist, 0)
    df = np.maximum(d, 1).astype(np.float32)
    large = max_exact + (np.log(df / np.float32(max_exact)) / np.float32(math.log(MAX_DISTANCE / max_exact))
                         * np.float32(N_BUCKETS - max_exact)).astype(np.int32)
    large = np.minimum(large, N_BUCKETS - 1)
    bucket = np.where(d < max_exact, d, large)
    in_win = (dist >= 0) & (dist < SWA_WINDOW)
    return np.where(in_win, bucket, -1).astype(np.int32)


def _swa_bias_table(rel_bias, buckets):
    H = SWA_Q_HEADS

    def body(rb_ref, bk_ref, o_ref):
        bk = bk_ref[...]
        for h in range(H):
            acc = jnp.full(bk.shape, NEG, F32)
            for b in range(N_BUCKETS):
                acc = jnp.where(bk == b, rb_ref[b, h], acc)
            o_ref[h] = acc

    return pl.pallas_call(
        body,
        in_specs=[pl.BlockSpec(memory_space=pltpu.SMEM), pl.BlockSpec(memory_space=pltpu.VMEM)],
        out_specs=pl.BlockSpec(memory_space=pltpu.VMEM),
        out_shape=jax.ShapeDtypeStruct((H, BLOCK, 2 * BLOCK), F32),
        name="swa_bias_table",
    )(rel_bias, buckets)


def _swa_bias_grad(dbias, buckets):
    H = SWA_Q_HEADS

    def body(db_ref, bk_ref, o_ref):
        bk = bk_ref[...]
        rows = lax.broadcasted_iota(jnp.int32, (N_BUCKETS, 128), 0)
        lanes = lax.broadcasted_iota(jnp.int32, (N_BUCKETS, 128), 1)
        acc = jnp.zeros((N_BUCKETS, 128), F32)
        for h in range(H):
            d = db_ref[h]
            for b in range(N_BUCKETS):
                s = jnp.sum(jnp.sum(jnp.where(bk == b, d, 0.0), axis=1, keepdims=True), axis=0, keepdims=True)
                acc = jnp.where((rows == b) & (lanes == h), s, acc)
        o_ref[...] = acc

    return pl.pallas_call(
        body,
        in_specs=[pl.BlockSpec(memory_space=pltpu.VMEM)] * 2,
        out_specs=pl.BlockSpec(memory_space=pltpu.VMEM),
        out_shape=jax.ShapeDtypeStruct((N_BUCKETS, 128), F32),
        name="swa_bias_grad",
    )(dbias, buckets)


SWA_GROUP = 4
SWA_ROWS = SWA_Q_HEADS * BLOCK


def _swa_kv_lanes(h):
    lane = lax.broadcasted_iota(jnp.int32, (BLOCK, BLOCK), 1)
    return (lane // HEAD_DIM) == h // SWA_GROUP


def _swa_stack(x, heads):
    blocks = []
    for h in heads:
        xp = x[:, (h // 2) * BLOCK:(h // 2 + 1) * BLOCK]
        xs = xp if h % 2 == h // SWA_GROUP else pltpu.roll(xp, HEAD_DIM, 1)
        blocks.append(jnp.where(_swa_kv_lanes(h), xs, 0.0))
    return jnp.concatenate(blocks, axis=0).astype(BF16)


def _swa_unstack(blocks):
    pairs = []
    for p in range(4):
        halves = []
        for hh in range(2):
            h = 2 * p + hh
            blk = jnp.where(_swa_kv_lanes(h), blocks[h], 0.0)
            halves.append(blk if hh == h // SWA_GROUP else pltpu.roll(blk, HEAD_DIM, 1))
        pairs.append(halves[0] + halves[1])
    return jnp.concatenate(pairs, axis=1)


def _swa_stacked_params(sink_ref, bias_ref, heads):
    bias = jnp.concatenate([bias_ref[h] for h in heads], axis=0)
    sink = jnp.concatenate([jnp.full((BLOCK, 1), sink_ref[0, h], F32) for h in heads], axis=0)
    return bias, sink


def _swa_scores(qs, kp, kc, bias, sink, first):
    sp = _dot_nt(qs, kp) * SCALE64 + bias[:, :BLOCK]
    sp = jnp.where(first, NEG, sp)
    sc = _dot_nt(qs, kc) * SCALE64 + bias[:, BLOCK:]
    m = jnp.maximum(jnp.maximum(jnp.max(sp, axis=1, keepdims=True), jnp.max(sc, axis=1, keepdims=True)), sink)
    pp = jnp.exp(sp - m)
    pc = jnp.exp(sc - m)
    ps = jnp.exp(sink - m)
    den = jnp.sum(pp, axis=1, keepdims=True) + jnp.sum(pc, axis=1, keepdims=True) + ps
    return pp / den, pc / den, ps / den


def _swa_fwd(proj, sinks, bias_tab, B, S, hosted=None):
    nb = S // BLOCK
    T = B * S

    def body(sink_ref, q_ref, kp_ref, kc_ref, vp_ref, vc_ref, bias_ref, o_ref):
        i = pl.program_id(1)
        first = jnp.full((SWA_GROUP * BLOCK, BLOCK), i, jnp.int32) == 0
        q = q_ref[...].astype(F32)
        blocks = []
        for g in range(SWA_Q_HEADS // SWA_GROUP):
            heads = range(SWA_GROUP * g, SWA_GROUP * (g + 1))
            bias, sink = _swa_stacked_params(sink_ref, bias_ref, heads)
            wp, wc, _ = _swa_scores(_swa_stack(q, heads), kp_ref[...], kc_ref[...], bias, sink, first)
            o = _dot(wp.astype(BF16), vp_ref[...]) + _dot(wc.astype(BF16), vc_ref[...])
            blocks += [o[t * BLOCK:(t + 1) * BLOCK] for t in range(SWA_GROUP)]
        o_ref[...] = _swa_unstack(blocks).astype(o_ref.dtype)

    blk = (BLOCK, BLOCK)
    wide = (BLOCK, 4 * BLOCK)
    outs, moved = _hosted_call(
        body, (B, nb),
        [pl.BlockSpec(memory_space=pltpu.SMEM),
         pl.BlockSpec(wide, lambda b, i: (b * nb + i, COL_QA // 4)),
         pl.BlockSpec(blk, lambda b, i: (b * nb + jnp.maximum(i - 1, 0), COL_KA)),
         pl.BlockSpec(blk, lambda b, i: (b * nb + i, COL_KA)),
         pl.BlockSpec(blk, lambda b, i: (b * nb + jnp.maximum(i - 1, 0), COL_VA)),
         pl.BlockSpec(blk, lambda b, i: (b * nb + i, COL_VA)),
         pl.BlockSpec((SWA_Q_HEADS, BLOCK, 2 * BLOCK), lambda b, i: (0, 0, 0))],
        [pl.BlockSpec(wide, lambda b, i: (b * nb + i, 0))],
        [jax.ShapeDtypeStruct((T, 512), BF16)], [], hosted, "swa_fwd",
        (sinks, proj, proj, proj, proj, proj, bias_tab))
    return outs[0], moved


def _swa_bwd(proj, dy, sinks, bias_tab, B, S, hosted=None):
    nb = S // BLOCK
    T = B * S
    H = SWA_Q_HEADS

    def body(sink_ref, q_ref, kp_ref, kc_ref, vp_ref, vc_ref, do_ref, bias_ref,
             dq_ref, dk_ref, dv_ref, dbias_ref, dsink_ref):
        b = pl.program_id(0)
        i = pl.program_id(1)

        @pl.when((b == 0) & (i == 0))
        def _():
            dbias_ref[...] = jnp.zeros_like(dbias_ref)
            dsink_ref[...] = jnp.zeros_like(dsink_ref)

        @pl.when(i == 0)
        def _():
            dk_ref[...] = jnp.zeros_like(dk_ref)
            dv_ref[...] = jnp.zeros_like(dv_ref)

        first = jnp.full((SWA_ROWS, BLOCK), i, jnp.int32) == 0
        heads = range(H)
        bias, sink = _swa_stacked_params(sink_ref, bias_ref, heads)
        kp, kc, vp, vc = kp_ref[...], kc_ref[...], vp_ref[...], vc_ref[...]
        qs = _swa_stack(q_ref[...].astype(F32), heads)
        dos = _swa_stack(do_ref[...].astype(F32), heads)
        wp, wc, ws = _swa_scores(qs, kp, kc, bias, sink, first)
        dwp = _dot_nt(dos, vp)
        dwc = _dot_nt(dos, vc)
        dsum = jnp.sum(wp * dwp, axis=1, keepdims=True) + jnp.sum(wc * dwc, axis=1, keepdims=True)
        dsp = wp * (dwp - dsum)
        dsc = wc * (dwc - dsum)
        dsk = -ws * dsum
        dsinks = []
        for h in range(H):
            rows = slice(h * BLOCK, (h + 1) * BLOCK)
            dsinks.append(jnp.broadcast_to(jnp.sum(dsk[rows], axis=0, keepdims=True), (1, 128)))
            dbias_ref[h] += jnp.concatenate([dsp[rows], dsc[rows]], axis=1)
        dsink_ref[...] += jnp.concatenate(dsinks, axis=0)
        dspb = dsp.astype(BF16)
        dscb = dsc.astype(BF16)
        dq = _dot(dspb, kp) + _dot(dscb, kc)
        dq_ref[...] = (_swa_unstack([dq[h * BLOCK:(h + 1) * BLOCK] for h in heads]) * SCALE64).astype(dq_ref.dtype)
        cur = pl.ds(pl.multiple_of(i * BLOCK, BLOCK), BLOCK)
        prev = pl.ds(pl.multiple_of(jnp.maximum(i - 1, 0) * BLOCK, BLOCK), BLOCK)
        dk_ref[prev, :] += _dot_tn(dspb, qs) * SCALE64
        dk_ref[cur, :] += _dot_tn(dscb, qs) * SCALE64
        dv_ref[prev, :] += _dot_tn(wp.astype(BF16), dos)
        dv_ref[cur, :] += _dot_tn(wc.astype(BF16), dos)

    blk = (BLOCK, BLOCK)
    wide = (BLOCK, 4 * BLOCK)
    kv_out = pl.BlockSpec((S, BLOCK), lambda b, i: (b, 0))
    full_bias = pl.BlockSpec((H, BLOCK, 2 * BLOCK), lambda b, i: (0, 0, 0))
    outs, moved = _hosted_call(
        body, (B, nb),
        [pl.BlockSpec(memory_space=pltpu.SMEM),
         pl.BlockSpec(wide, lambda b, i: (b * nb + i, COL_QA // 4)),
         pl.BlockSpec(blk, lambda b, i: (b * nb + jnp.maximum(i - 1, 0), COL_KA)),
         pl.BlockSpec(blk, lambda b, i: (b * nb + i, COL_KA)),
         pl.BlockSpec(blk, lambda b, i: (b * nb + jnp.maximum(i - 1, 0), COL_VA)),
         pl.BlockSpec(blk, lambda b, i: (b * nb + i, COL_VA)),
         pl.BlockSpec(wide, lambda b, i: (b * nb + i, 0)),
         full_bias],
        [pl.BlockSpec(wide, lambda b, i: (b * nb + i, 0)),
         kv_out, kv_out, full_bias,
         pl.BlockSpec((H, 128), lambda b, i: (0, 0))],
        [jax.ShapeDtypeStruct((T, 512), BF16),
         jax.ShapeDtypeStruct((T, BLOCK), F32), jax.ShapeDtypeStruct((T, BLOCK), F32),
         jax.ShapeDtypeStruct((H, BLOCK, 2 * BLOCK), F32), jax.ShapeDtypeStruct((H, 128), F32)],
        [], hosted, "swa_bwd", (sinks, proj, proj, proj, proj, proj, dy, bias_tab))
    return (*outs, moved)


SB_TILE = 256


SB_HEADS = 4
SB_LANES = SB_HEADS * HEAD_DIM
SB_ROWS = SB_HEADS * SB_TILE


def _sb_logits(z, tri):
    sp = jnp.log(1.0 + jnp.exp(-jnp.abs(z)))
    ls = jnp.minimum(z, 0.0) - sp
    l1m = ls - z
    if tri is not None:
        l1m = jnp.where(tri, l1m, 0.0)
    return ls, l1m


def _sb_masks():
    lane = lax.broadcasted_iota(jnp.int32, (SB_TILE, SB_LANES), 1)
    hm = [(lane // HEAD_DIM) == h for h in range(SB_HEADS)]
    row = lax.broadcasted_iota(jnp.int32, (SB_ROWS, SB_TILE), 0) % SB_TILE
    col = lax.broadcasted_iota(jnp.int32, (SB_ROWS, SB_TILE), 1)
    return lane, hm, row, col


def _sb_stack(x, hm):
    return jnp.concatenate([jnp.where(m, x, 0) for m in hm], axis=0)


def _sb_unstack(x, hm):
    return sum(jnp.where(m, x[h * SB_TILE:(h + 1) * SB_TILE], 0.0) for h, m in enumerate(hm))


def _sb_fwd(proj, B, S, hosted=None):
    nt = S // SB_TILE
    T = B * S

    ng = 512 // SB_LANES

    def body(*refs):
        q_refs, k_refs, v_refs = refs[:ng], refs[ng:2 * ng], refs[2 * ng:3 * ng]
        o_ref, r_ref = refs[3 * ng:]
        i = pl.program_id(1)
        _, hm, row, col = _sb_masks()
        tri = col < row
        later = jnp.concatenate([(row[:SB_TILE] > col[:SB_TILE]).astype(BF16)] * 2, axis=0)
        qs = [_sb_stack(q_ref[...] * SCALE64, hm) for q_ref in q_refs]

        def tile(j, carry, mask):
            rows = pl.ds(pl.multiple_of(j * SB_TILE, SB_TILE), SB_TILE)
            out = []
            for g in range(ng):
                acc, c = carry[g]
                ls, l1m = _sb_logits(_dot_nt(qs[g], k_refs[g][rows, :]), mask)
                a = jnp.exp(ls + c + _dot_split(l1m, later))
                if mask is not None:
                    a = jnp.where(mask, a, 0.0)
                pv = _dot(a.astype(BF16), v_refs[g][rows, :])
                out.append((acc + _sb_unstack(pv, hm), c + jnp.sum(l1m, axis=1, keepdims=True)))
            return tuple(out)

        zero = (jnp.zeros((SB_TILE, SB_LANES), F32), jnp.zeros((SB_ROWS, 1), F32))
        carry = tile(i, (zero,) * ng, tri)
        carry = lax.fori_loop(0, i, lambda it, cr: tile(i - 1 - it, cr, None), carry)
        o_ref[...] = jnp.concatenate([acc for acc, _ in carry], axis=1).astype(o_ref.dtype)
        r_ref[...] = jnp.concatenate(
            [_sb_unstack(jnp.broadcast_to(c, (SB_ROWS, SB_LANES)), hm) for _, c in carry], axis=1)

    blk = (SB_TILE, SB_LANES)
    cq, ck, cv = (c * BLOCK // SB_LANES for c in (COL_QB, COL_KB, COL_VB))
    wide = pl.BlockSpec((SB_TILE, 512), lambda b, i: (b * nt + i, 0))
    outs, moved = _hosted_call(
        body, (B, nt),
        [pl.BlockSpec(blk, lambda b, i, g=g: (b * nt + i, cq + g)) for g in range(ng)]
        + [pl.BlockSpec((S, SB_LANES), lambda b, i, g=g: (b, ck + g)) for g in range(ng)]
        + [pl.BlockSpec((S, SB_LANES), lambda b, i, g=g: (b, cv + g)) for g in range(ng)],
        [wide, wide],
        [jax.ShapeDtypeStruct((T, 512), BF16), jax.ShapeDtypeStruct((T, 512), F32)], [], hosted, "sb_fwd",
        (proj,) * (3 * ng))
    return outs[0], outs[1], moved


def _sb_bwd(proj, dy, rtot, B, S, hosted=None):
    nt = S // SB_TILE
    T = B * S

    ng = 512 // SB_LANES

    def body(*refs):
        q_refs, k_refs, v_refs = refs[:ng], refs[ng:2 * ng], refs[2 * ng:3 * ng]
        do_ref, r_ref, dq_ref, dk_ref, dv_ref = refs[3 * ng:]
        i = pl.program_id(1)

        @pl.when(i == 0)
        def _():
            dk_ref[...] = jnp.zeros_like(dk_ref)
            dv_ref[...] = jnp.zeros_like(dv_ref)

        lane, hm, row, col = _sb_masks()
        tri = col < row
        later = jnp.concatenate([(row[:SB_TILE] > col[:SB_TILE]).astype(BF16)] * 2, axis=0)
        earlier = (row[:SB_TILE] < col[:SB_TILE]).astype(BF16)
        qs, dos, rs = [], [], []
        for g in range(ng):
            lanes = slice(g * SB_LANES, (g + 1) * SB_LANES)
            qs.append(_sb_stack(q_refs[g][...] * SCALE64, hm))
            dos.append(_sb_stack(do_ref[:, lanes], hm))
            r = r_ref[:, lanes]
            rs.append(jnp.concatenate([jnp.sum(jnp.where(lane == h * HEAD_DIM, r, 0.0), axis=1, keepdims=True)
                                       for h in range(SB_HEADS)], axis=0))

        def tile(j, carry, mask):
            rows = pl.ds(pl.multiple_of(j * SB_TILE, SB_TILE), SB_TILE)
            out = []
            for g in range(ng):
                lanes = slice(g * SB_LANES, (g + 1) * SB_LANES)
                dq, lsum, psum = carry[g]
                kj = k_refs[g][rows, :]
                vj = v_refs[g][rows, :]
                ls, l1m = _sb_logits(_dot_nt(qs[g], kj), mask)
                sig = jnp.exp(ls)
                lsum = lsum + jnp.sum(l1m, axis=1, keepdims=True)
                a = jnp.exp(ls + (rs[g] - lsum) + _dot_split(l1m, later))
                if mask is not None:
                    a = jnp.where(mask, a, 0.0)
                de = _dot_nt(dos[g], vj) * a
                pre = psum + _dot(de.astype(BF16), earlier)
                dz = de - sig * (de + pre)
                if mask is not None:
                    dz = jnp.where(mask, dz, 0.0)
                dz = dz.astype(BF16)
                dk_ref[rows, lanes] += _dot_tn(dz, qs[g])
                dv_ref[rows, lanes] += _dot_tn(a.astype(BF16), dos[g])
                out.append((dq + _sb_unstack(_dot(dz, kj), hm), lsum, psum + jnp.sum(de, axis=1, keepdims=True)))
            return tuple(out)

        zero = jnp.zeros((SB_ROWS, 1), F32)
        init = ((jnp.zeros((SB_TILE, SB_LANES), F32), zero, zero),) * ng
        carry = lax.fori_loop(0, i, lambda j, c: tile(j, c, None), init)
        carry = tile(i, carry, tri)
        dq_ref[...] = (jnp.concatenate([c[0] for c in carry], axis=1) * SCALE64).astype(dq_ref.dtype)

    blk = (SB_TILE, SB_LANES)
    cq, ck, cv = (c * BLOCK // SB_LANES for c in (COL_QB, COL_KB, COL_VB))
    wide = pl.BlockSpec((SB_TILE, 512), lambda b, i: (b * nt + i, 0))
    kv_out = pl.BlockSpec((S, 512), lambda b, i: (b, 0))
    outs, moved = _hosted_call(
        body, (B, nt),
        [pl.BlockSpec(blk, lambda b, i, g=g: (b * nt + i, cq + g)) for g in range(ng)]
        + [pl.BlockSpec((S, SB_LANES), lambda b, i, g=g: (b, ck + g)) for g in range(ng)]
        + [pl.BlockSpec((S, SB_LANES), lambda b, i, g=g: (b, cv + g)) for g in range(ng)]
        + [wide, wide],
        [wide, kv_out, kv_out],
        [jax.ShapeDtypeStruct((T, 512), BF16),
         jax.ShapeDtypeStruct((T, 512), F32), jax.ShapeDtypeStruct((T, 512), F32)], [], hosted, "sb_bwd",
        (proj,) * (3 * ng) + (dy, rtot))
    return outs[0], outs[1], outs[2], moved


def _mem_weights(q, mk):
    z = _dot_nt(q, mk) * SCALE128
    e = jnp.exp(z - jnp.max(z, axis=1, keepdims=True))
    return e / jnp.sum(e, axis=1, keepdims=True)


def _mem_fwd(proj, mkv, B, S, M):
    tq = 512
    nq = S // tq
    T = B * S
    Hm = MEM_HEADS

    def body(q0, q1, q2, q3, mk_ref, mv_ref, o_ref):
        outs = []
        for h, q_ref in enumerate((q0, q1, q2, q3)):
            cols = slice(h * 128, (h + 1) * 128)
            w = _mem_weights(q_ref[...], mk_ref[:, cols])
            outs.append(_dot(w.astype(BF16), mv_ref[:, cols]))
        o_ref[...] = jnp.concatenate(outs, axis=1).astype(o_ref.dtype)

    return pl.pallas_call(
        body,
        grid=(B, nq),
        in_specs=[pl.BlockSpec((tq, 128), lambda b, i, h=h: (b * nq + i, COL_QM + h)) for h in range(Hm)]
        + [pl.BlockSpec((M, 512), lambda b, i: (b, 0)), pl.BlockSpec((M, 512), lambda b, i: (b, 1))],
        out_specs=pl.BlockSpec((tq, 512), lambda b, i: (b * nq + i, 0)),
        out_shape=jax.ShapeDtypeStruct((T, 512), BF16),
        name="mem_fwd",
    )(proj, proj, proj, proj, mkv, mkv)


def _mem_bwd(proj, mkv, dy, B, S, M):
    tq = 512
    nq = S // tq
    T = B * S
    Hm = MEM_HEADS

    def body(q0, q1, q2, q3, mk_ref, mv_ref, do_ref, dq_ref, dmk_ref, dmv_ref):
        i = pl.program_id(1)
        dqs, dmks, dmvs = [], [], []
        for h, q_ref in enumerate((q0, q1, q2, q3)):
            cols = slice(h * 128, (h + 1) * 128)
            q = q_ref[...]
            do = do_ref[:, cols]
            mk = mk_ref[:, cols]
            w = _mem_weights(q, mk)
            dw = _dot_nt(do, mv_ref[:, cols])
            ds = (w * (dw - jnp.sum(w * dw, axis=1, keepdims=True))).astype(BF16)
            dqs.append(_dot(ds, mk))
            dmks.append(_dot_tn(ds, q))
            dmvs.append(_dot_tn(w.astype(BF16), do))
        dq_ref[...] = (jnp.concatenate(dqs, axis=1) * SCALE128).astype(dq_ref.dtype)

        @pl.when(i == 0)
        def _():
            dmk_ref[...] = jnp.zeros_like(dmk_ref)
            dmv_ref[...] = jnp.zeros_like(dmv_ref)

        dmk_ref[...] += jnp.concatenate(dmks, axis=1) * SCALE128
        dmv_ref[...] += jnp.concatenate(dmvs, axis=1)

    q_spec = pl.BlockSpec((tq, 512), lambda b, i: (b * nq + i, 0))
    m_out = pl.BlockSpec((M, 512), lambda b, i: (b, 0))
    return pl.pallas_call(
        body,
        grid=(B, nq),
        in_specs=[pl.BlockSpec((tq, 128), lambda b, i, h=h: (b * nq + i, COL_QM + h)) for h in range(Hm)]
        + [pl.BlockSpec((M, 512), lambda b, i: (b, 0)), pl.BlockSpec((M, 512), lambda b, i: (b, 1)), q_spec],
        out_specs=[q_spec, m_out, m_out],
        out_shape=[jax.ShapeDtypeStruct((T, 512), BF16),
                   jax.ShapeDtypeStruct((B * M, 512), F32), jax.ShapeDtypeStruct((B * M, 512), F32)],
        name="mem_bwd",
    )(proj, proj, proj, proj, mkv, mkv, dy)


def _all_gather(blk, name):
    R, C = blk.shape

    def body(x_ref, out_ref, send_sems, recv_sems, local_sem):
        x, y, c = _mesh_pos()
        me, sibling = (x, y, c), (x, y, 1 - c)
        chips = [(1 - x, y), (x, 1 - y), (1 - x, 1 - y)]

        def slot(px, py, pc):
            return out_ref.at[4 * px + 2 * py + pc]

        def copy(k, block, to, src=None):
            return pltpu.make_async_remote_copy(
                src_ref=slot(*block) if src is None else src, dst_ref=slot(*block),
                send_sem=send_sems.at[k], recv_sem=recv_sems.at[k],
                device_id=to, device_id_type=pl.DeviceIdType.MESH)

        mine = pltpu.make_async_copy(x_ref, slot(*me), local_sem)
        mine.start()
        first = [copy(0, me, sibling, src=x_ref)]
        first += [copy(1 + j, me, (*chip, c), src=x_ref) for j, chip in enumerate(chips)]
        for cp in first:
            cp.start()
        passed = [copy(4 + j, (*chip, c), sibling) for j, chip in enumerate(chips)]
        for j, chip in enumerate(chips):
            copy(1 + j, (*chip, c), me).wait_recv()
            passed[j].start()
        copy(0, sibling, me).wait_recv()
        for j, chip in enumerate(chips):
            copy(4 + j, (*chip, 1 - c), me).wait_recv()
        for cp in first + passed:
            cp.wait_send()
        mine.wait()

    return pl.pallas_call(
        body,
        in_specs=[pl.BlockSpec(memory_space=pl.ANY)],
        out_specs=pl.BlockSpec(memory_space=pl.ANY),
        out_shape=jax.ShapeDtypeStruct((N_DEV, R, C), blk.dtype),
        scratch_shapes=[pltpu.SemaphoreType.DMA((7,)), pltpu.SemaphoreType.DMA((7,)), pltpu.SemaphoreType.DMA],
        name=name,
    )(blk)


_HBM = pl.BlockSpec(memory_space=pltpu.HBM)
_SEM = pl.BlockSpec(memory_space=pltpu.SEMAPHORE)


def _scatter_start(parts, land, window, carried, name):
    hosted = _Hosted(scatters=[parts], window=window)

    def body(p_ref, land_ref, c_ref, send_sems, recv_sems, local_sems, p_thru, land_thru, c_thru):
        for cp in hosted.copies([p_ref], [land_ref], send_sems, recv_sems, local_sems):
            cp.start()

    sems = (pltpu.SemaphoreType.DMA((7,)), pltpu.SemaphoreType.DMA((7,)), pltpu.SemaphoreType.DMA((1,)))
    hbm = lambda a: pltpu.HBM(a.shape, a.dtype)
    outs = pl.pallas_call(
        body, name=name,
        out_shape=sems + (hbm(parts), hbm(land), hbm(carried)),
        in_specs=(_HBM, _HBM, _HBM), out_specs=(_SEM, _SEM, _SEM, _HBM, _HBM, _HBM),
        input_output_aliases={0: 3, 1: 4, 2: 5},
        compiler_params=pltpu.CompilerParams(has_side_effects=pltpu.SideEffectType.DATAFLOW_SIDE_EFFECTING),
    )(pltpu.with_memory_space_constraint(parts, pltpu.HBM),
      pltpu.with_memory_space_constraint(land, pltpu.HBM),
      pltpu.with_memory_space_constraint(carried, pltpu.HBM))
    return (outs[:4], window), outs[4], outs[5]


def _scatter_wait(flight, land, after, name):
    (send_sems, recv_sems, local_sems, p_thru), window = flight
    hosted = _Hosted(scatters=[p_thru], window=window)

    def body(p_ref, land_ref, send, recv, local, after_ref, p_dead, got_ref):
        for cp in hosted.copies([p_ref], [land_ref], send, recv, local):
            cp.wait()

    hbm = lambda a: pltpu.HBM(a.shape, a.dtype)
    return pl.pallas_call(
        body, name=name,
        out_shape=(hbm(p_thru), hbm(land)),
        in_specs=(_HBM, _HBM, _SEM, _SEM, _SEM, pl.BlockSpec(memory_space=pl.ANY)), out_specs=(_HBM, _HBM),
        input_output_aliases={0: 0, 1: 1},
        compiler_params=pltpu.CompilerParams(has_side_effects=pltpu.SideEffectType.DATAFLOW_SIDE_EFFECTING),
    )(p_thru, land, send_sems, recv_sems, local_sems, after)[1]


def _adamw(parts, w, m, v, name):
    R, C = w.shape
    tr = R
    for cand in (368, 352, 256, 176, 128, 64, 32, 16, 8):
        if R % cand == 0 and cand * C * 4 <= 1536 * 1024:
            tr = cand
            break
    c1 = 1.0 - ADAM_B1 ** ADAM_STEP
    c2 = 1.0 - ADAM_B2 ** ADAM_STEP

    def body(p_ref, w_ref, m_ref, v_ref, g_ref, d_ref, nm_ref, nv_ref):
        g = p_ref[0].astype(F32)
        for d in range(1, N_DEV):
            g = g + p_ref[d].astype(F32)
        nm = ADAM_B1 * m_ref[...] + (1.0 - ADAM_B1) * g
        nv = ADAM_B2 * v_ref[...] + (1.0 - ADAM_B2) * (g * g)
        g_ref[...] = g
        nm_ref[...] = nm
        nv_ref[...] = nv
        d_ref[...] = -ADAM_LR * ((nm / c1) / (jnp.sqrt(nv / c2) + ADAM_EPS) + ADAM_WD * w_ref[...])

    row = pl.BlockSpec((tr, C), lambda i: (i, 0))
    out = jax.ShapeDtypeStruct((R, C), F32)
    return pl.pallas_call(
        body,
        grid=(R // tr,),
        in_specs=[pl.BlockSpec((N_DEV, tr, C), lambda i: (0, i, 0)), row, row, row],
        out_specs=[row] * 4,
        out_shape=[out] * 4,
        compiler_params=_params(),
        name=name,
    )(parts, w, m, v)


def _col_shards(g):
    R, C8 = g.shape
    return g.reshape(R, N_DEV, C8 // N_DEV).transpose(1, 0, 2)


def _row_shards(g):
    R8, C = g.shape
    return g.reshape(N_DEV, R8 // N_DEV, C)


def _cols_full(gathered):
    n, R, C = gathered.shape
    return gathered.transpose(1, 0, 2).reshape(R, n * C)


_BIG = ("w_in", "w_mem_kv", "w_branch_swa", "w_branch_sb", "w_branch_mem", "w_out", "w_gate", "w_up", "w_down")
_COL_SHARDED = ("w_branch_swa", "w_branch_sb", "w_branch_mem")
_TRANSPOSED = ("w_in", "w_gate", "w_up")
_SMALL = ("ln_mix_pre", "ln_mix_post", "swa_sinks", "rel_bias", "ln_mem", "ln_ffn_pre", "ln_ffn_post")
_ORDER = ("ln_mix_pre", "ln_mix_post", "w_in", "swa_sinks", "rel_bias", "ln_mem", "w_mem_kv", "w_branch_swa",
          "w_branch_sb", "w_branch_mem", "w_out", "ln_ffn_pre", "ln_ffn_post", "w_gate", "w_up", "w_down")


def _pack_small(d, last=None):
    rows = [d["ln_mix_pre"], d["ln_mix_post"], d["ln_mem"], d["ln_ffn_pre"], d["ln_ffn_post"],
            jnp.pad(d["swa_sinks"].reshape(1, -1), ((0, 0), (0, D_MODEL - SWA_Q_HEADS))),
            jnp.pad(d["rel_bias"].reshape(1, -1), ((0, 0), (0, D_MODEL - N_BUCKETS * SWA_Q_HEADS))),
            jnp.zeros((1, D_MODEL), F32) if last is None else last]
    return jnp.concatenate([r.astype(F32) for r in rows], axis=0)


def _unpack_small(a):
    return dict(ln_mix_pre=a[0:1], ln_mix_post=a[1:2], ln_mem=a[2:3], ln_ffn_pre=a[3:4], ln_ffn_post=a[4:5],
                swa_sinks=a[5:6, :SWA_Q_HEADS],
                rel_bias=a[6, :N_BUCKETS * SWA_Q_HEADS].reshape(N_BUCKETS, SWA_Q_HEADS))


def kernel(x, mem, ln_mix_pre, ln_mix_post, w_in, swa_sinks, rel_bias, ln_mem, w_mem_kv, w_branch_swa, w_branch_sb, w_branch_mem, w_out, ln_ffn_pre, ln_ffn_post, w_gate, w_up, w_down, loss_target, m_ln_mix_pre, m_ln_mix_post, m_w_in, m_swa_sinks, m_rel_bias, m_ln_mem, m_w_mem_kv, m_w_branch_swa, m_w_branch_sb, m_w_branch_mem, m_w_out, m_ln_ffn_pre, m_ln_ffn_post, m_w_gate, m_w_up, m_w_down, v_ln_mix_pre, v_ln_mix_post, v_w_in, v_swa_sinks, v_rel_bias, v_ln_mem, v_w_mem_kv, v_w_branch_swa, v_w_branch_sb, v_w_branch_mem, v_w_out, v_ln_ffn_pre, v_ln_ffn_post, v_w_gate, v_w_up, v_w_down):
    w = dict(ln_mix_pre=ln_mix_pre, ln_mix_post=ln_mix_post, w_in=w_in[0], swa_sinks=swa_sinks, rel_bias=rel_bias,
             ln_mem=ln_mem, w_mem_kv=w_mem_kv[0], w_branch_swa=w_branch_swa[0], w_branch_sb=w_branch_sb[0],
             w_branch_mem=w_branch_mem[0], w_out=w_out[0], ln_ffn_pre=ln_ffn_pre, ln_ffn_post=ln_ffn_post,
             w_gate=w_gate[0], w_up=w_up[0], w_down=w_down[0])
    mom = dict(ln_mix_pre=m_ln_mix_pre, ln_mix_post=m_ln_mix_post, w_in=m_w_in[0], swa_sinks=m_swa_sinks,
               rel_bias=m_rel_bias, ln_mem=m_ln_mem, w_mem_kv=m_w_mem_kv[0], w_branch_swa=m_w_branch_swa[0],
               w_branch_sb=m_w_branch_sb[0], w_branch_mem=m_w_branch_mem[0], w_out=m_w_out[0],
               ln_ffn_pre=m_ln_ffn_pre, ln_ffn_post=m_ln_ffn_post, w_gate=m_w_gate[0], w_up=m_w_up[0],
               w_down=m_w_down[0])
    var = dict(ln_mix_pre=v_ln_mix_pre, ln_mix_post=v_ln_mix_post, w_in=v_w_in[0], swa_sinks=v_swa_sinks,
               rel_bias=v_rel_bias, ln_mem=v_ln_mem, w_mem_kv=v_w_mem_kv[0], w_branch_swa=v_w_branch_swa[0],
               w_branch_sb=v_w_branch_sb[0], w_branch_mem=v_w_branch_mem[0], w_out=v_w_out[0],
               ln_ffn_pre=v_ln_ffn_pre, ln_ffn_post=v_ln_ffn_post, w_gate=v_w_gate[0], w_up=v_w_up[0],
               w_down=v_w_down[0])
    B, S, D = x.shape
    M = mem.shape[1]
    T = B * S
    F = D_FF
    x2 = x.reshape(T, D)
    mem2 = mem.reshape(B * M, D)
    t2 = loss_target.reshape(T, D)
    buckets = jnp.asarray(_swa_buckets())
    for d in (w, mom, var):
        for n in _TRANSPOSED:
            d[n] = d[n].T
    wb = {n: w[n].astype(BF16) for n in _BIG}
    full = {}

    def landed(names, got):
        for n, g in zip(names, got):
            full[n] = _cols_full(g) if n in _COL_SHARDED else g.reshape(-1, g.shape[-1])

    def shards(n, g):
        return _col_shards(g) if n in _COL_SHARDED else _row_shards(g)

    landed(["w_in"], [_all_gather(wb["w_in"], "ag_w_in")])
    u = _rms_fwd(x2, ln_mix_pre, "rms_mix_pre")
    early = ["w_mem_kv", "w_branch_swa", "w_branch_sb", "w_branch_mem"]
    proj, got = _matmul([(u, full["w_in"])], "nt", BF16, 512, IN_W // 2, D, "proj_in",
                        hosted=_Hosted(gathers=[wb[n] for n in early]))
    landed(early, got)
    mn = _rms_fwd(mem2, ln_mem, "rms_mem")
    mkv = _matmul([(mn, full["w_mem_kv"])], "nn", BF16, 512, 1024, D, "proj_mem")
    bias_tab = _swa_bias_table(rel_bias, buckets)
    y_swa, got = _swa_fwd(proj, swa_sinks, bias_tab, B, S, hosted=_Hosted(gathers=[wb["w_out"]]))
    landed(["w_out"], got)
    late = ["w_gate", "w_up"]
    y_sb, rtot, got = _sb_fwd(proj, B, S, hosted=_Hosted(gathers=[wb[n] for n in late]))
    landed(late, got)
    y_mem = _mem_fwd(proj, mkv, B, S, M)
    wbs = (full["w_branch_swa"], full["w_branch_sb"], full["w_branch_mem"])
    merged, p_swa, p_sb, p_mem = _branch_gate(proj, (y_swa, y_sb, y_mem), wbs)
    mix, h1, u2 = _post_pre(x2, merged, full["w_out"], ln_mix_post, ln_ffn_pre)
    a, zg, zu, got = _ffn_up(u2, full["w_gate"], full["w_up"], hosted=_Hosted(gathers=[wb["w_down"]]))
    landed(["w_down"], got)
    dffn, dh2, loss_tile, d_ln_ffn_post = _loss_head(a, full["w_down"], h1, t2, ln_ffn_post)

    part = {}
    part["w_down"] = _matmul([(a, dffn)], "tn", BF16, F // 2, 1024, 1024, "dw_down")
    dzg, dzu = _ffn_down_bwd(dffn, full["w_down"], zg, zu)
    part["w_gate"] = _matmul([(dzg, u2)], "tn", BF16, F // 2, 1024, 1024, "dw_gate")
    part["w_up"] = _matmul([(dzu, u2)], "tn", BF16, F // 2, 1024, 1024, "dw_up")
    du2 = _matmul([(dzg, full["w_gate"]), (dzu, full["w_up"])], "nn", BF16, 512, 1024, F // 2, "d_u2")
    dh1, dmix, d_ln_ffn_pre, d_ln_mix_post = _mid_bwd(h1, du2, dh2, mix, ln_ffn_pre, ln_mix_post)
    part["w_out"] = _matmul([(merged, dmix)], "tn", BF16, 1024, 1024, 1024, "dw_out")
    (dp_swa, dp_sb, dp_mem, dg0, dg1, dg2, dy_swa, dy_sb, dy_mem) = _gate_bwd(
        dmix, full["w_out"], (p_swa, p_sb, p_mem), proj, wbs)
    part["w_branch_swa"] = _matmul([(y_swa, dp_swa)], "tn", BF16, 512, 1024, 1024, "dw_branch_swa")
    part["w_branch_sb"] = _matmul([(y_sb, dp_sb)], "tn", BF16, 512, 1024, 1024, "dw_branch_sb")
    part["w_branch_mem"] = _matmul([(y_mem, dp_mem)], "tn", BF16, 512, 1024, 1024, "dw_branch_mem")
    dqm, dmk, dmv = _mem_bwd(proj, mkv, dy_mem, B, S, M)
    dmkv = jnp.concatenate([dmk, dmv], axis=1).astype(BF16)
    part["w_mem_kv"] = _matmul([(mn, dmkv)], "tn", BF16, 1024, 1024, 512, "dw_mem_kv")
    dmn = _matmul([(dmkv, full["w_mem_kv"])], "nt", F32, 512, 1024, 1024, "d_mn")
    d_ln_mem = _gain_grad(mem2, dmn)
    behind_swa = ["w_out", "w_branch_swa", "w_branch_sb"]
    dqa, dka, dva, dbias, dsink, got = _swa_bwd(
        proj, dy_swa, swa_sinks, bias_tab, B, S, hosted=_Hosted(scatters=[shards(n, part[n]) for n in behind_swa]))
    recv = dict(zip(behind_swa, got))
    behind_sb = ["w_down", "w_gate", "w_up", "w_branch_mem", "w_mem_kv"]
    dqb, dkb, dvb, got = _sb_bwd(proj, dy_sb, rtot, B, S,
                                 hosted=_Hosted(scatters=[shards(n, part[n]) for n in behind_sb]))
    recv.update(zip(behind_sb, got))
    d_rel_bias = _swa_bias_grad(dbias, buckets)[:, :SWA_Q_HEADS]
    d_sinks = dsink[:, 0].reshape(1, SWA_Q_HEADS)
    dproj = jnp.concatenate([dqa, dka.astype(BF16), dva.astype(BF16), dqb, dkb.astype(BF16), dvb.astype(BF16),
                             dqm, dg0, dg1, dg2], axis=1)
    half = D // 2
    land = lax.empty((N_DEV, IN_W // N_DEV, D), BF16)
    flights = []
    for t in range(2):
        dw_half = _matmul([(dproj, u)], "tn", BF16, IN_W // 2, half, 1024, "dw_in_%d" % t, n_cols=half, n_off=t)
        flight, land, dproj = _scatter_start(_row_shards(dw_half), land, (t * half, half), dproj,
                                             "rs_w_in_start_%d" % t)
        flights.append(flight)
    du = _matmul([(dproj, full["w_in"])], "nn", BF16, 512, 1024, IN_W // 2, "d_u")
    grad_x, d_ln_mix_pre = _pre_bwd(x2, du, dh1, ln_mix_pre)

    out = {n: _adamw(recv[n], w[n], mom[n], var[n], "adamw_" + n) for n in _BIG if n != "w_in"}
    small_grads = dict(ln_mix_pre=d_ln_mix_pre, ln_mix_post=d_ln_mix_post, swa_sinks=d_sinks, rel_bias=d_rel_bias,
                       ln_mem=d_ln_mem, ln_ffn_pre=d_ln_ffn_pre, ln_ffn_post=d_ln_ffn_post)
    small_parts = _all_gather(_pack_small(small_grads, jnp.tile(loss_tile[0:1], (1, D // 128))), "ag_small")
    res = _adamw(small_parts, _pack_small(w), _pack_small(mom), _pack_small(var), "adamw_small")
    loss = res[0][7, 0]
    small = [_unpack_small(r) for r in res]
    for n in _SMALL:
        out[n] = tuple(s[n] for s in small)
    for t, flight in enumerate(flights):
        land = _scatter_wait(flight, land, res[0], "rs_w_in_wait_%d" % t)
    out["w_in"] = _adamw(land, w["w_in"], mom["w_in"], var["w_in"], "adamw_w_in")
    for n in _TRANSPOSED:
        out[n] = tuple(o.T for o in out[n])

    like = dict(ln_mix_pre=ln_mix_pre, ln_mix_post=ln_mix_post, w_in=w_in, swa_sinks=swa_sinks, rel_bias=rel_bias,
                ln_mem=ln_mem, w_mem_kv=w_mem_kv, w_branch_swa=w_branch_swa, w_branch_sb=w_branch_sb,
                w_branch_mem=w_branch_mem, w_out=w_out, ln_ffn_pre=ln_ffn_pre, ln_ffn_post=ln_ffn_post,
                w_gate=w_gate, w_up=w_up, w_down=w_down)
    result = [loss, grad_x.reshape(B, S, D)]
    for k in range(4):
        result += [out[n][k].reshape(like[n].shape) for n in _ORDER]
    return tuple(result)
```

```python
import functools
import math

import numpy as np
import jax
import jax.numpy as jnp
from jax import lax
from jax.experimental import pallas as pl
from jax.experimental.pallas import tpu as pltpu

F32 = jnp.float32
BF16 = jnp.bfloat16

N_DEV = 8
D_MODEL = 1024
BLOCK = 128
EPS = 1e-6
HEAD_DIM = 64
SWA_Q_HEADS = 8
SWA_WINDOW = 128
N_BUCKETS = 32
MAX_DISTANCE = 128
MEM_HEADS = 4
MEM_HEAD_DIM = 128
D_FF = 2816
IN_W = 5888
COL_QA, COL_KA, COL_VA, COL_QB, COL_KB, COL_VB, COL_QM, COL_GL = 0, 4, 5, 6, 10, 14, 18, 22
SCALE64 = HEAD_DIM ** -0.5
SCALE128 = MEM_HEAD_DIM ** -0.5
NEG = -1e30

ADAM_LR = 0.001
ADAM_B1 = 0.9
ADAM_B2 = 0.999
ADAM_EPS = 1e-08
ADAM_WD = 0.01
ADAM_STEP = 10

VMEM_LIMIT_BYTES = 56 * 1024 * 1024


def _params(**kw):
    return pltpu.CompilerParams(vmem_limit_bytes=VMEM_LIMIT_BYTES, **kw)


def _dot(a, b):
    return jnp.dot(a, b, preferred_element_type=F32)


def _dot_nt(a, b):
    return lax.dot_general(a, b, (((1,), (1,)), ((), ())), preferred_element_type=F32)


def _dot_tn(a, b):
    return lax.dot_general(a, b, (((0,), (0,)), ((), ())), preferred_element_type=F32)


def _dot_split(x, m2):
    hi = x.astype(BF16)
    lo = (x - hi.astype(F32)).astype(BF16)
    return _dot(jnp.concatenate([hi, lo], axis=1), m2)


def _mesh_pos():
    return lax.axis_index("x"), lax.axis_index("y"), lax.axis_index("c")


class _Hosted:
    def __init__(self, gathers=(), scatters=(), window=None):
        self.items = [("g", a) for a in gathers] + [("s", a) for a in scatters]
        self.n = len(self.items)
        self.window = window

    def operands(self):
        return [a for _, a in self.items]

    def specs(self):
        return [pl.BlockSpec(memory_space=pl.ANY)] * self.n

    def out_shapes(self):
        return [jax.ShapeDtypeStruct(((N_DEV,) + a.shape) if kind == "g" else a.shape, a.dtype)
                for kind, a in self.items]

    def scratch(self):
        return [pltpu.SemaphoreType.DMA((7 * self.n,)), pltpu.SemaphoreType.DMA((7 * self.n,)),
                pltpu.SemaphoreType.DMA((self.n,))]

    def copies(self, in_refs, out_refs, send_sems, recv_sems, local_sems):
        x, y, c = _mesh_pos()
        me = 4 * x + 2 * y + c
        out = []
        for t, (kind, _) in enumerate(self.items):
            own = in_refs[t] if kind == "g" else in_refs[t].at[me]
            dst = out_refs[t].at[me]
            if self.window is not None:
                dst = out_refs[t].at[me, :, pl.ds(*self.window)]
            out.append(pltpu.make_async_copy(own, dst, local_sems.at[t]))
            for k in range(1, N_DEV):
                px, py, pc = x ^ (k >> 2), y ^ ((k >> 1) & 1), c ^ (k & 1)
                src = in_refs[t] if kind == "g" else in_refs[t].at[4 * px + 2 * py + pc]
                out.append(pltpu.make_async_remote_copy(
                    src_ref=src, dst_ref=dst,
                    send_sem=send_sems.at[7 * t + k - 1], recv_sem=recv_sems.at[7 * t + k - 1],
                    device_id=(px, py, pc), device_id_type=pl.DeviceIdType.MESH))
        return out


def _host(body, n_in, n_out, hosted, grid):
    if hosted is None:
        return body
    nc = hosted.n

    def wrapped(*refs):
        ins = refs[:n_in]
        cin = refs[n_in:n_in + nc]
        outs = refs[n_in + nc:n_in + nc + n_out]
        cout = refs[n_in + nc + n_out:n_in + 2 * nc + n_out]
        scratch = refs[n_in + 2 * nc + n_out:len(refs) - 3]
        sems = refs[len(refs) - 3:]
        ids = [pl.program_id(d) for d in range(len(grid))]
        first = functools.reduce(lambda a, b: a & b, [i == 0 for i in ids])
        last = functools.reduce(lambda a, b: a & b, [i == g - 1 for i, g in zip(ids, grid)])

        @pl.when(first)
        def _():
            for cp in hosted.copies(cin, cout, *sems):
                cp.start()

        body(*ins, *outs, *scratch)

        @pl.when(last)
        def _():
            for cp in hosted.copies(cin, cout, *sems):
                cp.wait()

    return wrapped


def _hosted_call(body, grid, in_specs, out_specs, out_shape, scratch_shapes, hosted, name, args):
    n_out = len(out_specs)
    if hosted is None:
        outs = pl.pallas_call(body, grid=grid, in_specs=in_specs, out_specs=out_specs, out_shape=out_shape,
                              scratch_shapes=scratch_shapes, compiler_params=_params(), name=name)(*args)
        return list(outs), []
    outs = pl.pallas_call(
        _host(body, len(in_specs), n_out, hosted, grid),
        grid=grid,
        in_specs=list(in_specs) + hosted.specs(),
        out_specs=list(out_specs) + hosted.specs(),
        out_shape=list(out_shape) + hosted.out_shapes(),
        scratch_shapes=list(scratch_shapes) + hosted.scratch(),
        compiler_params=_params(),
        name=name,
    )(*args, *hosted.operands())
    return list(outs[:n_out]), list(outs[n_out:])


_DIMS = {"nn": (((1,), (0,)), ((), ())), "nt": (((1,), (1,)), ((), ())), "tn": (((0,), (0,)), ((), ()))}


def _matmul(pairs, mode, out_dtype, tm, tn, tk, name, hosted=None, n_cols=None, n_off=0):
    a0, b0 = pairs[0]
    if mode == "nn":
        (M, K), N = a0.shape, b0.shape[1]
    elif mode == "nt":
        (M, K), N = a0.shape, b0.shape[0]
    else:
        (K, M), N = a0.shape, b0.shape[1]
    N = N if n_cols is None else n_cols
    tm, tn, tk = min(tm, M), min(tn, N), min(tk, K)
    assert M % tm == 0 and N % tn == 0 and K % tk == 0, (name, M, N, K, tm, tn, tk)
    nm, nn, nk = M // tm, N // tn, K // tk
    npair = len(pairs)
    dims = _DIMS[mode]

    def body(*refs):
        ab = refs[:2 * npair]
        o_ref = refs[2 * npair]
        acc_ref = refs[2 * npair + 1]
        k = pl.program_id(2)
        part = lax.dot_general(ab[0][...], ab[1][...], dims, preferred_element_type=F32)
        for q in range(1, npair):
            part += lax.dot_general(ab[2 * q][...], ab[2 * q + 1][...], dims, preferred_element_type=F32)
        if nk == 1:
            o_ref[...] = part.astype(o_ref.dtype)
        else:
            @pl.when(k == 0)
            def _():
                acc_ref[...] = part

            @pl.when(k > 0)
            def _():
                acc_ref[...] += part

            @pl.when(k == nk - 1)
            def _():
                o_ref[...] = acc_ref[...].astype(o_ref.dtype)

    if mode == "nn":
        a_spec = pl.BlockSpec((tm, tk), lambda n, m, k: (m, k))
        b_spec = pl.BlockSpec((tk, tn), lambda n, m, k: (k, n + n_off))
    elif mode == "nt":
        a_spec = pl.BlockSpec((tm, tk), lambda n, m, k: (m, k))
        b_spec = pl.BlockSpec((tn, tk), lambda n, m, k: (n + n_off, k))
    else:
        a_spec = pl.BlockSpec((tk, tm), lambda n, m, k: (k, m))
        b_spec = pl.BlockSpec((tk, tn), lambda n, m, k: (k, n + n_off))
    args = [t for pr in pairs for t in pr]
    outs, moved = _hosted_call(
        body, (nn, nm, nk), [a_spec, b_spec] * npair, [pl.BlockSpec((tm, tn), lambda n, m, k: (m, n))],
        [jax.ShapeDtypeStruct((M, N), out_dtype)], [pltpu.VMEM((tm, tn) if nk > 1 else (8, 128), F32)],
        hosted, name, args)
    return outs[0] if hosted is None else (outs[0], moved)


def _rms_fwd(x, g, name):
    T, D = x.shape
    tr = min(512, T)

    def body(x_ref, g_ref, u_ref):
        xf = x_ref[...]
        r = lax.rsqrt(jnp.mean(xf * xf, axis=-1, keepdims=True) + EPS)
        u_ref[...] = ((xf * r) * g_ref[...]).astype(u_ref.dtype)

    return pl.pallas_call(
        body,
        grid=(T // tr,),
        in_specs=[pl.BlockSpec((tr, D), lambda i: (i, 0)), pl.BlockSpec((1, D), lambda i: (0, 0))],
        out_specs=pl.BlockSpec((tr, D), lambda i: (i, 0)),
        out_shape=jax.ShapeDtypeStruct((T, D), BF16),
        name=name,
    )(x, g)


def _rms_bwd_terms(xin, g, dy):
    r = lax.rsqrt(jnp.mean(xin * xin, axis=-1, keepdims=True) + EPS)
    xh = xin * r
    dg = jnp.sum(dy * xh, axis=0, keepdims=True)
    dxh = dy * g
    dx = r * (dxh - xh * jnp.mean(dxh * xh, axis=-1, keepdims=True))
    return dx, dg


GATE_TC = 256


def _branch_gate(proj, ys, wbs):
    T = proj.shape[0]
    D = D_MODEL
    tr, tc = min(1024, T), GATE_TC
    nc = D // tc
    gl0 = COL_GL * 128 // tc

    def body(ya, yb, yc, wa, wb, wc, g0, g1, g2, merged_ref, pa, pb, pc):
        acc = jnp.zeros((tr, tc), F32)
        for y_ref, w_ref, g_ref, p_ref in ((ya, wa, g0, pa), (yb, wb, g1, pb), (yc, wc, g2, pc)):
            p = _dot(y_ref[...], w_ref[...])
            p_ref[...] = p.astype(p_ref.dtype)
            acc += jax.nn.sigmoid(g_ref[...].astype(F32)) * p
        merged_ref[...] = acc.astype(merged_ref.dtype)

    y_spec = pl.BlockSpec((tr, 512), lambda i, n: (i, 0))
    w_spec = pl.BlockSpec((512, tc), lambda i, n: (0, n))
    o_spec = pl.BlockSpec((tr, tc), lambda i, n: (i, n))
    gl_specs = [pl.BlockSpec((tr, tc), lambda i, n, j=j: (i, gl0 + j * nc + n)) for j in range(3)]
    out = jax.ShapeDtypeStruct((T, D), BF16)
    return pl.pallas_call(
        body,
        grid=(T // tr, nc),
        in_specs=[y_spec] * 3 + [w_spec] * 3 + gl_specs,
        out_specs=[o_spec] * 4,
        out_shape=[out] * 4,
        compiler_params=_params(),
        name="branch_gate",
    )(*ys, *wbs, proj, proj, proj)


def _post_pre(x, merged, w_out, g_post, g_pre):
    T, D = x.shape
    tr = 512

    def body(x_ref, m_ref, w_ref, gp_ref, gq_ref, mix_ref, h1_ref, u2_ref):
        mx = _dot(m_ref[...], w_ref[...])
        mix_ref[...] = mx
        r = lax.rsqrt(jnp.mean(mx * mx, axis=-1, keepdims=True) + EPS)
        h1 = x_ref[...] + (mx * r) * gp_ref[...]
        h1_ref[...] = h1
        r2 = lax.rsqrt(jnp.mean(h1 * h1, axis=-1, keepdims=True) + EPS)
        u2_ref[...] = ((h1 * r2) * gq_ref[...]).astype(u2_ref.dtype)

    row = pl.BlockSpec((tr, D), lambda i: (i, 0))
    vec = pl.BlockSpec((1, D), lambda i: (0, 0))
    f32 = jax.ShapeDtypeStruct((T, D), F32)
    return pl.pallas_call(
        body,
        grid=(T // tr,),
        in_specs=[row, row, pl.BlockSpec((D, D), lambda i: (0, 0)), vec, vec],
        out_specs=[row, row, row],
        out_shape=[f32, f32, jax.ShapeDtypeStruct((T, D), BF16)],
        compiler_params=_params(),
        name="post_pre",
    )(x, merged, w_out, g_post, g_pre)


def _ffn_up(u2, w_gate_t, w_up_t, hosted=None):
    T, D = u2.shape
    F = D_FF
    tm, tn = 512, F // 2

    def body(u_ref, wg_ref, wu_ref, a_ref, zg_ref, zu_ref):
        u = u_ref[...]
        zg = _dot_nt(u, wg_ref[...])
        zu = _dot_nt(u, wu_ref[...])
        a_ref[...] = (zg * jax.nn.sigmoid(zg) * zu).astype(a_ref.dtype)
        zg_ref[...] = zg.astype(zg_ref.dtype)
        zu_ref[...] = zu.astype(zu_ref.dtype)

    o_spec = pl.BlockSpec((tm, tn), lambda n, m: (m, n))
    out = jax.ShapeDtypeStruct((T, F), BF16)
    outs, moved = _hosted_call(
        body, (F // tn, T // tm),
        [pl.BlockSpec((tm, D), lambda n, m: (m, 0)),
         pl.BlockSpec((tn, D), lambda n, m: (n, 0)),
         pl.BlockSpec((tn, D), lambda n, m: (n, 0))],
        [o_spec] * 3, [out] * 3, [], hosted, "ffn_up", (u2, w_gate_t, w_up_t))
    return (*outs, moved)


def _loss_head(a, w_down, h1, target, g_post):
    T, D = h1.shape
    F = a.shape[1]
    tr = 512

    def body(a_ref, w_ref, h1_ref, t_ref, g_ref, dffn_ref, dh2_ref, loss_ref, dg_ref):
        i = pl.program_id(0)
        f = _dot(a_ref[...], w_ref[...])
        g = g_ref[...]
        r = lax.rsqrt(jnp.mean(f * f, axis=-1, keepdims=True) + EPS)
        xh = f * r
        err = (h1_ref[...] + xh * g) - t_ref[...]
        part = 0.5 * jnp.sum(jnp.mean(err * err, axis=-1, keepdims=True), axis=0, keepdims=True)
        dh2 = err * (1.0 / D)
        dh2_ref[...] = dh2
        dgp = jnp.sum(dh2 * xh, axis=0, keepdims=True)
        dxh = dh2 * g
        dffn_ref[...] = (r * (dxh - xh * jnp.mean(dxh * xh, axis=-1, keepdims=True))).astype(dffn_ref.dtype)

        @pl.when(i == 0)
        def _():
            loss_ref[...] = jnp.zeros_like(loss_ref)
            dg_ref[...] = jnp.zeros_like(dg_ref)

        loss_ref[...] += jnp.broadcast_to(part, loss_ref.shape)
        dg_ref[...] += dgp

    row = pl.BlockSpec((tr, D), lambda i: (i, 0))
    vec = pl.BlockSpec((1, D), lambda i: (0, 0))
    return pl.pallas_call(
        body,
        grid=(T // tr,),
        in_specs=[pl.BlockSpec((tr, F), lambda i: (i, 0)), pl.BlockSpec((F, D), lambda i: (0, 0)), row, row, vec],
        out_specs=[row, row, pl.BlockSpec((8, 128), lambda i: (0, 0)), vec],
        out_shape=[jax.ShapeDtypeStruct((T, D), BF16), jax.ShapeDtypeStruct((T, D), F32),
                   jax.ShapeDtypeStruct((8, 128), F32), jax.ShapeDtypeStruct((1, D), F32)],
        compiler_params=_params(),
        name="loss_head",
    )(a, w_down, h1, target, g_post)


def _ffn_down_bwd(dffn, wd, zg, zu):
    T, D = dffn.shape
    F = D_FF
    tm, tn = 512, F // 2

    def body(d_ref, w_ref, zg_ref, zu_ref, dzg_ref, dzu_ref):
        d = d_ref[...]
        for lo in range(0, tn, 512):
            cols = slice(lo, min(lo + 512, tn))
            da = _dot_nt(d, w_ref[cols, :])
            zg = zg_ref[:, cols].astype(F32)
            zu = zu_ref[:, cols].astype(F32)
            s = jax.nn.sigmoid(zg)
            dzu_ref[:, cols] = (da * (zg * s)).astype(dzu_ref.dtype)
            dzg_ref[:, cols] = (da * zu * (s * (1.0 + zg * (1.0 - s)))).astype(dzg_ref.dtype)

    z_spec = pl.BlockSpec((tm, tn), lambda n, m: (m, n))
    out = jax.ShapeDtypeStruct((T, F), BF16)
    return pl.pallas_call(
        body,
        grid=(F // tn, T // tm),
        in_specs=[pl.BlockSpec((tm, D), lambda n, m: (m, 0)), pl.BlockSpec((tn, D), lambda n, m: (n, 0)),
                  z_spec, z_spec],
        out_specs=[z_spec, z_spec],
        out_shape=[out, out],
        compiler_params=_params(),
        name="ffn_down_bwd",
    )(dffn, wd, zg, zu)


def _matmul_rows(pairs, tm, tk, rows_in, vecs_in, rows_out, n_vec_out, epilogue, name):
    a0, b0 = pairs[0]
    (M, K), N = a0.shape, b0.shape[1]
    tm, tk = min(tm, M), min(tk, K)
    assert M % tm == 0 and K % tk == 0, (name, M, K, tm, tk)
    nk = K // tk
    npair = len(pairs)
    n_in = 2 * npair + len(rows_in) + len(vecs_in)

    def body(*refs):
        ab = refs[:2 * npair]
        r_in = refs[2 * npair:2 * npair + len(rows_in)]
        v_in = refs[2 * npair + len(rows_in):n_in]
        r_out = refs[n_in:n_in + len(rows_out)]
        v_out = refs[n_in + len(rows_out):n_in + len(rows_out) + n_vec_out]
        acc_ref = refs[-1]
        m = pl.program_id(0)
        k = pl.program_id(1)
        part = _dot(ab[0][...], ab[1][...])
        for q in range(1, npair):
            part += _dot(ab[2 * q][...], ab[2 * q + 1][...])

        @pl.when(k == 0)
        def _():
            acc_ref[...] = part

        @pl.when(k > 0)
        def _():
            acc_ref[...] += part

        @pl.when(k == nk - 1)
        def _():
            @pl.when(m == 0)
            def _():
                for v in v_out:
                    v[...] = jnp.zeros_like(v)

            epilogue(acc_ref[...], r_in, v_in, r_out, v_out)

    row = pl.BlockSpec((tm, N), lambda m, k: (m, 0))
    vec = pl.BlockSpec((1, N), lambda m, k: (0, 0))
    return pl.pallas_call(
        body,
        grid=(M // tm, nk),
        in_specs=[pl.BlockSpec((tm, tk), lambda m, k: (m, k)), pl.BlockSpec((tk, N), lambda m, k: (k, 0))] * npair
        + [row] * len(rows_in) + [vec] * len(vecs_in),
        out_specs=[row] * len(rows_out) + [vec] * n_vec_out,
        out_shape=[jax.ShapeDtypeStruct((M, N), dt) for dt in rows_out]
        + [jax.ShapeDtypeStruct((1, N), F32)] * n_vec_out,
        scratch_shapes=[pltpu.VMEM((tm, N), F32)],
        compiler_params=_params(),
        name=name,
    )(*[t for pr in pairs for t in pr], *rows_in, *vecs_in)


def _mid_bwd(dzg, dzu, w_gate_t, w_up_t, h1, dh2, mix, g_pre, g_post):
    def epilogue(du2, rows, vecs, outs, accs):
        h1_ref, dh2_ref, mix_ref = rows
        gq_ref, gp_ref = vecs
        dh1_ref, dmix_ref = outs
        dx, dgq = _rms_bwd_terms(h1_ref[...], gq_ref[...], du2)
        dh1 = dh2_ref[...] + dx
        dh1_ref[...] = dh1
        dmix, dgp = _rms_bwd_terms(mix_ref[...], gp_ref[...], dh1)
        dmix_ref[...] = dmix.astype(dmix_ref.dtype)
        accs[0][...] += dgq
        accs[1][...] += dgp

    return _matmul_rows([(dzg, w_gate_t), (dzu, w_up_t)], 512, D_FF // 2, [h1, dh2, mix], [g_pre, g_post],
                        [F32, BF16], 2, epilogue, "mid_bwd")


def _gate_bwd(dmix, w_out, ps, proj, wbs):
    T, D = dmix.shape
    tr, tc = min(1024, T), GATE_TC
    nc = D // tc
    gl0 = COL_GL * 128 // tc

    def body(dmix_ref, wo_ref, pa, pb, pc, g0, g1, g2, wa, wb, wc, dpa, dpb, dpc, dga, dgb, dgc, dya, dyb, dyc,
             acc_a, acc_b, acc_c):
        n = pl.program_id(1)
        dm = _dot_nt(dmix_ref[...], wo_ref[...])
        for p_ref, g_ref, w_ref, dp_ref, dg_ref, dy_ref, acc_ref in (
                (pa, g0, wa, dpa, dga, dya, acc_a), (pb, g1, wb, dpb, dgb, dyb, acc_b),
                (pc, g2, wc, dpc, dgc, dyc, acc_c)):
            s = jax.nn.sigmoid(g_ref[...].astype(F32))
            dp = (dm * s).astype(BF16)
            dp_ref[...] = dp
            dg_ref[...] = (dm * p_ref[...].astype(F32) * (s * (1.0 - s))).astype(dg_ref.dtype)
            part = _dot_nt(dp, w_ref[...])

            @pl.when(n == 0)
            def _():
                acc_ref[...] = part

            @pl.when(n > 0)
            def _():
                acc_ref[...] += part

            @pl.when(n == nc - 1)
            def _():
                dy_ref[...] = acc_ref[...].astype(dy_ref.dtype)

    col = pl.BlockSpec((tr, tc), lambda i, n: (i, n))
    y_spec = pl.BlockSpec((tr, 512), lambda i, n: (i, 0))
    w_spec = pl.BlockSpec((512, tc), lambda i, n: (0, n))
    gl_specs = [pl.BlockSpec((tr, tc), lambda i, n, j=j: (i, gl0 + j * nc + n)) for j in range(3)]
    big = jax.ShapeDtypeStruct((T, D), BF16)
    small = jax.ShapeDtypeStruct((T, 512), BF16)
    return pl.pallas_call(
        body,
        grid=(T // tr, nc),
        in_specs=[pl.BlockSpec((tr, D), lambda i, n: (i, 0)), pl.BlockSpec((tc, D), lambda i, n: (n, 0))]
        + [col] * 3 + gl_specs + [w_spec] * 3,
        out_specs=[col] * 6 + [y_spec] * 3,
        out_shape=[big] * 6 + [small] * 3,
        scratch_shapes=[pltpu.VMEM((tr, 512), F32)] * 3,
        compiler_params=_params(),
        name="gate_bwd",
    )(dmix, w_out, *ps, proj, proj, proj, *wbs)


def _pre_bwd(dproj, w_in_t, x, dh1, g):
    def epilogue(du, rows, vecs, outs, accs):
        x_ref, dh1_ref = rows
        dx, dg = _rms_bwd_terms(x_ref[...], vecs[0][...], du)
        outs[0][...] = dh1_ref[...] + dx
        accs[0][...] += dg

    return _matmul_rows([(dproj, w_in_t)], 512, IN_W // 2, [x, dh1], [g], [F32], 1, epilogue, "pre_bwd")


def _gain_grad(xin, dy):
    T, D = xin.shape
    tr = min(512, T)

    def body(x_ref, dy_ref, dg_ref):
        i = pl.program_id(0)
        xf = x_ref[...]
        r = lax.rsqrt(jnp.mean(xf * xf, axis=-1, keepdims=True) + EPS)

        @pl.when(i == 0)
        def _():
            dg_ref[...] = jnp.zeros_like(dg_ref)

        dg_ref[...] += jnp.sum(dy_ref[...] * (xf * r), axis=0, keepdims=True)

    row = pl.BlockSpec((tr, D), lambda i: (i, 0))
    return pl.pallas_call(
        body,
        grid=(T // tr,),
        in_specs=[row, row],
        out_specs=pl.BlockSpec((1, D), lambda i: (0, 0)),
        out_shape=jax.ShapeDtypeStruct((1, D), F32),
        name="gain_grad",
    )(xin, dy)


def _swa_buckets():
    dist = (np.arange(BLOCK)[:, None] + BLOCK) - np.arange(2 * BLOCK)[None, :]
    max_exact = N_BUCKETS // 2
    d = np.maximum(dist, 0)
    df = np.maximum(d, 1).astype(np.float32)
    large = max_exact + (np.log(df / np.float32(max_exact)) / np.float32(math.log(MAX_DISTANCE / max_exact))
                         * np.float32(N_BUCKETS - max_exact)).astype(np.int32)
    large = np.minimum(large, N_BUCKETS - 1)
    bucket = np.where(d < max_exact, d, large)
    in_win = (dist >= 0) & (dist < SWA_WINDOW)
    return np.where(in_win, bucket, -1).astype(np.int32)


def _swa_bias_table(rel_bias, buckets):
    H = SWA_Q_HEADS

    def body(rb_ref, bk_ref, o_ref):
        bk = bk_ref[...]
        for h in range(H):
            acc = jnp.full(bk.shape, NEG, F32)
            for b in range(N_BUCKETS):
                acc = jnp.where(bk == b, rb_ref[b, h], acc)
            o_ref[h] = acc

    return pl.pallas_call(
        body,
        in_specs=[pl.BlockSpec(memory_space=pltpu.SMEM), pl.BlockSpec(memory_space=pltpu.VMEM)],
        out_specs=pl.BlockSpec(memory_space=pltpu.VMEM),
        out_shape=jax.ShapeDtypeStruct((H, BLOCK, 2 * BLOCK), F32),
        name="swa_bias_table",
    )(rel_bias, buckets)


def _swa_bias_grad(dbias, buckets):
    H = SWA_Q_HEADS

    def body(db_ref, bk_ref, o_ref):
        bk = bk_ref[...]
        rows = lax.broadcasted_iota(jnp.int32, (N_BUCKETS, 128), 0)
        lanes = lax.broadcasted_iota(jnp.int32, (N_BUCKETS, 128), 1)
        acc = jnp.zeros((N_BUCKETS, 128), F32)
        for h in range(H):
            d = db_ref[h]
            for b in range(N_BUCKETS):
                s = jnp.sum(jnp.sum(jnp.where(bk == b, d, 0.0), axis=1, keepdims=True), axis=0, keepdims=True)
                acc = jnp.where((rows == b) & (lanes == h), s, acc)
        o_ref[...] = acc

    return pl.pallas_call(
        body,
        in_specs=[pl.BlockSpec(memory_space=pltpu.VMEM)] * 2,
        out_specs=pl.BlockSpec(memory_space=pltpu.VMEM),
        out_shape=jax.ShapeDtypeStruct((N_BUCKETS, 128), F32),
        name="swa_bias_grad",
    )(dbias, buckets)


SWA_GROUP = 4
SWA_ROWS = SWA_Q_HEADS * BLOCK


def _swa_kv_lanes(h):
    lane = lax.broadcasted_iota(jnp.int32, (BLOCK, BLOCK), 1)
    return (lane // HEAD_DIM) == h // SWA_GROUP


def _swa_stack(x, heads):
    blocks = []
    for h in heads:
        xp = x[:, (h // 2) * BLOCK:(h // 2 + 1) * BLOCK]
        xs = xp if h % 2 == h // SWA_GROUP else pltpu.roll(xp, HEAD_DIM, 1)
        blocks.append(jnp.where(_swa_kv_lanes(h), xs, 0.0))
    return jnp.concatenate(blocks, axis=0).astype(BF16)


def _swa_unstack(blocks):
    pairs = []
    for p in range(4):
        halves = []
        for hh in range(2):
            h = 2 * p + hh
            blk = jnp.where(_swa_kv_lanes(h), blocks[h], 0.0)
            halves.append(blk if hh == h // SWA_GROUP else pltpu.roll(blk, HEAD_DIM, 1))
        pairs.append(halves[0] + halves[1])
    return jnp.concatenate(pairs, axis=1)


def _swa_stacked_params(sink_ref, bias_ref, heads):
    bias = jnp.concatenate([bias_ref[h] for h in heads], axis=0)
    sink = jnp.concatenate([jnp.full((BLOCK, 1), sink_ref[0, h], F32) for h in heads], axis=0)
    return bias, sink


def _swa_scores(qs, kp, kc, bias, sink, first):
    sp = _dot_nt(qs, kp) * SCALE64 + bias[:, :BLOCK]
    sp = jnp.where(first, NEG, sp)
    sc = _dot_nt(qs, kc) * SCALE64 + bias[:, BLOCK:]
    m = jnp.maximum(jnp.maximum(jnp.max(sp, axis=1, keepdims=True), jnp.max(sc, axis=1, keepdims=True)), sink)
    pp = jnp.exp(sp - m)
    pc = jnp.exp(sc - m)
    ps = jnp.exp(sink - m)
    den = jnp.sum(pp, axis=1, keepdims=True) + jnp.sum(pc, axis=1, keepdims=True) + ps
    return pp / den, pc / den, ps / den


def _swa_fwd(proj, sinks, bias_tab, B, S, hosted=None):
    nb = S // BLOCK
    T = B * S

    def body(sink_ref, q_ref, kp_ref, kc_ref, vp_ref, vc_ref, bias_ref, o_ref):
        i = pl.program_id(1)
        first = jnp.full((SWA_GROUP * BLOCK, BLOCK), i, jnp.int32) == 0
        q = q_ref[...].astype(F32)
        blocks = []
        for g in range(SWA_Q_HEADS // SWA_GROUP):
            heads = range(SWA_GROUP * g, SWA_GROUP * (g + 1))
            bias, sink = _swa_stacked_params(sink_ref, bias_ref, heads)
            wp, wc, _ = _swa_scores(_swa_stack(q, heads), kp_ref[...], kc_ref[...], bias, sink, first)
            o = _dot(wp.astype(BF16), vp_ref[...]) + _dot(wc.astype(BF16), vc_ref[...])
            blocks += [o[t * BLOCK:(t + 1) * BLOCK] for t in range(SWA_GROUP)]
        o_ref[...] = _swa_unstack(blocks).astype(o_ref.dtype)

    blk = (BLOCK, BLOCK)
    wide = (BLOCK, 4 * BLOCK)
    outs, moved = _hosted_call(
        body, (B, nb),
        [pl.BlockSpec(memory_space=pltpu.SMEM),
         pl.BlockSpec(wide, lambda b, i: (b * nb + i, COL_QA // 4)),
         pl.BlockSpec(blk, lambda b, i: (b * nb + jnp.maximum(i - 1, 0), COL_KA)),
         pl.BlockSpec(blk, lambda b, i: (b * nb + i, COL_KA)),
         pl.BlockSpec(blk, lambda b, i: (b * nb + jnp.maximum(i - 1, 0), COL_VA)),
         pl.BlockSpec(blk, lambda b, i: (b * nb + i, COL_VA)),
         pl.BlockSpec((SWA_Q_HEADS, BLOCK, 2 * BLOCK), lambda b, i: (0, 0, 0))],
        [pl.BlockSpec(wide, lambda b, i: (b * nb + i, 0))],
        [jax.ShapeDtypeStruct((T, 512), BF16)], [], hosted, "swa_fwd",
        (sinks, proj, proj, proj, proj, proj, bias_tab))
    return outs[0], moved


def _swa_bwd(proj, dy, sinks, bias_tab, B, S, hosted=None):
    nb = S // BLOCK
    T = B * S
    H = SWA_Q_HEADS

    def body(sink_ref, q_ref, kp_ref, kc_ref, vp_ref, vc_ref, do_ref, bias_ref,
             dq_ref, dk_ref, dv_ref, dbias_ref, dsink_ref):
        b = pl.program_id(0)
        i = pl.program_id(1)

        @pl.when((b == 0) & (i == 0))
        def _():
            dbias_ref[...] = jnp.zeros_like(dbias_ref)
            dsink_ref[...] = jnp.zeros_like(dsink_ref)

        @pl.when(i == 0)
        def _():
            dk_ref[...] = jnp.zeros_like(dk_ref)
            dv_ref[...] = jnp.zeros_like(dv_ref)

        first = jnp.full((SWA_ROWS, BLOCK), i, jnp.int32) == 0
        heads = range(H)
        bias, sink = _swa_stacked_params(sink_ref, bias_ref, heads)
        kp, kc, vp, vc = kp_ref[...], kc_ref[...], vp_ref[...], vc_ref[...]
        qs = _swa_stack(q_ref[...].astype(F32), heads)
        dos = _swa_stack(do_ref[...].astype(F32), heads)
        wp, wc, ws = _swa_scores(qs, kp, kc, bias, sink, first)
        dwp = _dot_nt(dos, vp)
        dwc = _dot_nt(dos, vc)
        dsum = jnp.sum(wp * dwp, axis=1, keepdims=True) + jnp.sum(wc * dwc, axis=1, keepdims=True)
        dsp = wp * (dwp - dsum)
        dsc = wc * (dwc - dsum)
        dsk = -ws * dsum
        dsinks = []
        for h in range(H):
            rows = slice(h * BLOCK, (h + 1) * BLOCK)
            dsinks.append(jnp.broadcast_to(jnp.sum(dsk[rows], axis=0, keepdims=True), (1, 128)))
            dbias_ref[h] += jnp.concatenate([dsp[rows], dsc[rows]], axis=1)
        dsink_ref[...] += jnp.concatenate(dsinks, axis=0)
        dspb = dsp.astype(BF16)
        dscb = dsc.astype(BF16)
        dq = _dot(dspb, kp) + _dot(dscb, kc)
        dq_ref[...] = (_swa_unstack([dq[h * BLOCK:(h + 1) * BLOCK] for h in heads]) * SCALE64).astype(dq_ref.dtype)
        cur = pl.ds(pl.multiple_of(i * BLOCK, BLOCK), BLOCK)
        prev = pl.ds(pl.multiple_of(jnp.maximum(i - 1, 0) * BLOCK, BLOCK), BLOCK)
        dk_ref[prev, :] += _dot_tn(dspb, qs) * SCALE64
        dk_ref[cur, :] += _dot_tn(dscb, qs) * SCALE64
        dv_ref[prev, :] += _dot_tn(wp.astype(BF16), dos)
        dv_ref[cur, :] += _dot_tn(wc.astype(BF16), dos)

    blk = (BLOCK, BLOCK)
    wide = (BLOCK, 4 * BLOCK)
    kv_out = pl.BlockSpec((S, BLOCK), lambda b, i: (b, 0))
    full_bias = pl.BlockSpec((H, BLOCK, 2 * BLOCK), lambda b, i: (0, 0, 0))
    outs, moved = _hosted_call(
        body, (B, nb),
        [pl.BlockSpec(memory_space=pltpu.SMEM),
         pl.BlockSpec(wide, lambda b, i: (b * nb + i, COL_QA // 4)),
         pl.BlockSpec(blk, lambda b, i: (b * nb + jnp.maximum(i - 1, 0), COL_KA)),
         pl.BlockSpec(blk, lambda b, i: (b * nb + i, COL_KA)),
         pl.BlockSpec(blk, lambda b, i: (b * nb + jnp.maximum(i - 1, 0), COL_VA)),
         pl.BlockSpec(blk, lambda b, i: (b * nb + i, COL_VA)),
         pl.BlockSpec(wide, lambda b, i: (b * nb + i, 0)),
         full_bias],
        [pl.BlockSpec(wide, lambda b, i: (b * nb + i, 0)),
         kv_out, kv_out, full_bias,
         pl.BlockSpec((H, 128), lambda b, i: (0, 0))],
        [jax.ShapeDtypeStruct((T, 512), BF16),
         jax.ShapeDtypeStruct((T, BLOCK), F32), jax.ShapeDtypeStruct((T, BLOCK), F32),
         jax.ShapeDtypeStruct((H, BLOCK, 2 * BLOCK), F32), jax.ShapeDtypeStruct((H, 128), F32)],
        [], hosted, "swa_bwd", (sinks, proj, proj, proj, proj, proj, dy, bias_tab))
    return (*outs, moved)


SB_TILE = 256


SB_HEADS = 4
SB_LANES = SB_HEADS * HEAD_DIM
SB_ROWS = SB_HEADS * SB_TILE


def _sb_logits(z, tri):
    sp = jnp.log(1.0 + jnp.exp(-jnp.abs(z)))
    ls = jnp.minimum(z, 0.0) - sp
    l1m = ls - z
    if tri is not None:
        l1m = jnp.where(tri, l1m, 0.0)
    return ls, l1m


def _sb_masks():
    lane = lax.broadcasted_iota(jnp.int32, (SB_TILE, SB_LANES), 1)
    hm = [(lane // HEAD_DIM) == h for h in range(SB_HEADS)]
    row = lax.broadcasted_iota(jnp.int32, (SB_ROWS, SB_TILE), 0) % SB_TILE
    col = lax.broadcasted_iota(jnp.int32, (SB_ROWS, SB_TILE), 1)
    return lane, hm, row, col


def _sb_stack(x, hm):
    return jnp.concatenate([jnp.where(m, x, 0) for m in hm], axis=0)


def _sb_unstack(x, hm):
    return sum(jnp.where(m, x[h * SB_TILE:(h + 1) * SB_TILE], 0.0) for h, m in enumerate(hm))


def _sb_fwd(proj, B, S, hosted=None):
    nt = S // SB_TILE
    T = B * S

    ng = 512 // SB_LANES

    def body(*refs):
        q_refs, k_refs, v_refs = refs[:ng], refs[ng:2 * ng], refs[2 * ng:3 * ng]
        o_ref, r_ref = refs[3 * ng:]
        i = pl.program_id(1)
        _, hm, row, col = _sb_masks()
        tri = col < row
        later = jnp.concatenate([(row[:SB_TILE] > col[:SB_TILE]).astype(BF16)] * 2, axis=0)
        qs = [_sb_stack(q_ref[...] * SCALE64, hm) for q_ref in q_refs]

        def tile(j, carry, mask):
            rows = pl.ds(pl.multiple_of(j * SB_TILE, SB_TILE), SB_TILE)
            out = []
            for g in range(ng):
                acc, c = carry[g]
                ls, l1m = _sb_logits(_dot_nt(qs[g], k_refs[g][rows, :]), mask)
                a = jnp.exp(ls + c + _dot_split(l1m, later))
                if mask is not None:
                    a = jnp.where(mask, a, 0.0)
                pv = _dot(a.astype(BF16), v_refs[g][rows, :])
                out.append((acc + _sb_unstack(pv, hm), c + jnp.sum(l1m, axis=1, keepdims=True)))
            return tuple(out)

        zero = (jnp.zeros((SB_TILE, SB_LANES), F32), jnp.zeros((SB_ROWS, 1), F32))
        carry = tile(i, (zero,) * ng, tri)
        carry = lax.fori_loop(0, i, lambda it, cr: tile(i - 1 - it, cr, None), carry)
        o_ref[...] = jnp.concatenate([acc for acc, _ in carry], axis=1).astype(o_ref.dtype)
        r_ref[...] = jnp.concatenate(
            [_sb_unstack(jnp.broadcast_to(c, (SB_ROWS, SB_LANES)), hm) for _, c in carry], axis=1)

    blk = (SB_TILE, SB_LANES)
    cq, ck, cv = (c * BLOCK // SB_LANES for c in (COL_QB, COL_KB, COL_VB))
    wide = pl.BlockSpec((SB_TILE, 512), lambda b, i: (b * nt + i, 0))
    outs, moved = _hosted_call(
        body, (B, nt),
        [pl.BlockSpec(blk, lambda b, i, g=g: (b * nt + i, cq + g)) for g in range(ng)]
        + [pl.BlockSpec((S, SB_LANES), lambda b, i, g=g: (b, ck + g)) for g in range(ng)]
        + [pl.BlockSpec((S, SB_LANES), lambda b, i, g=g: (b, cv + g)) for g in range(ng)],
        [wide, wide],
        [jax.ShapeDtypeStruct((T, 512), BF16), jax.ShapeDtypeStruct((T, 512), F32)], [], hosted, "sb_fwd",
        (proj,) * (3 * ng))
    return outs[0], outs[1], moved


def _sb_bwd(proj, dy, rtot, B, S, hosted=None):
    nt = S // SB_TILE
    T = B * S

    ng = 512 // SB_LANES

    def body(*refs):
        q_refs, k_refs, v_refs = refs[:ng], refs[ng:2 * ng], refs[2 * ng:3 * ng]
        do_ref, r_ref, dq_ref, dk_ref, dv_ref = refs[3 * ng:]
        i = pl.program_id(1)

        @pl.when(i == 0)
        def _():
            dk_ref[...] = jnp.zeros_like(dk_ref)
            dv_ref[...] = jnp.zeros_like(dv_ref)

        lane, hm, row, col = _sb_masks()
        tri = col < row
        later = jnp.concatenate([(row[:SB_TILE] > col[:SB_TILE]).astype(BF16)] * 2, axis=0)
        earlier = (row[:SB_TILE] < col[:SB_TILE]).astype(BF16)
        qs, dos, rs = [], [], []
        for g in range(ng):
            lanes = slice(g * SB_LANES, (g + 1) * SB_LANES)
            qs.append(_sb_stack(q_refs[g][...] * SCALE64, hm))
            dos.append(_sb_stack(do_ref[:, lanes], hm))
            r = r_ref[:, lanes]
            rs.append(jnp.concatenate([jnp.sum(jnp.where(lane == h * HEAD_DIM, r, 0.0), axis=1, keepdims=True)
                                       for h in range(SB_HEADS)], axis=0))

        def tile(j, carry, mask):
            rows = pl.ds(pl.multiple_of(j * SB_TILE, SB_TILE), SB_TILE)
            out = []
            for g in range(ng):
                lanes = slice(g * SB_LANES, (g + 1) * SB_LANES)
                dq, lsum, psum = carry[g]
                kj = k_refs[g][rows, :]
                vj = v_refs[g][rows, :]
                ls, l1m = _sb_logits(_dot_nt(qs[g], kj), mask)
                sig = jnp.exp(ls)
                lsum = lsum + jnp.sum(l1m, axis=1, keepdims=True)
                a = jnp.exp(ls + (rs[g] - lsum) + _dot_split(l1m, later))
                if mask is not None:
                    a = jnp.where(mask, a, 0.0)
                de = _dot_nt(dos[g], vj) * a
                pre = psum + _dot(de.astype(BF16), earlier)
                dz = de - sig * (de + pre)
                if mask is not None:
                    dz = jnp.where(mask, dz, 0.0)
                dz = dz.astype(BF16)
                dk_ref[rows, lanes] += _dot_tn(dz, qs[g])
                dv_ref[rows, lanes] += _dot_tn(a.astype(BF16), dos[g])
                out.append((dq + _sb_unstack(_dot(dz, kj), hm), lsum, psum + jnp.sum(de, axis=1, keepdims=True)))
            return tuple(out)

        zero = jnp.zeros((SB_ROWS, 1), F32)
        init = ((jnp.zeros((SB_TILE, SB_LANES), F32), zero, zero),) * ng
        carry = lax.fori_loop(0, i, lambda j, c: tile(j, c, None), init)
        carry = tile(i, carry, tri)
        dq_ref[...] = (jnp.concatenate([c[0] for c in carry], axis=1) * SCALE64).astype(dq_ref.dtype)

    blk = (SB_TILE, SB_LANES)
    cq, ck, cv = (c * BLOCK // SB_LANES for c in (COL_QB, COL_KB, COL_VB))
    wide = pl.BlockSpec((SB_TILE, 512), lambda b, i: (b * nt + i, 0))
    kv_out = pl.BlockSpec((S, 512), lambda b, i: (b, 0))
    outs, moved = _hosted_call(
        body, (B, nt),
        [pl.BlockSpec(blk, lambda b, i, g=g: (b * nt + i, cq + g)) for g in range(ng)]
        + [pl.BlockSpec((S, SB_LANES), lambda b, i, g=g: (b, ck + g)) for g in range(ng)]
        + [pl.BlockSpec((S, SB_LANES), lambda b, i, g=g: (b, cv + g)) for g in range(ng)]
        + [wide, wide],
        [wide, kv_out, kv_out],
        [jax.ShapeDtypeStruct((T, 512), BF16),
         jax.ShapeDtypeStruct((T, 512), F32), jax.ShapeDtypeStruct((T, 512), F32)], [], hosted, "sb_bwd",
        (proj,) * (3 * ng) + (dy, rtot))
    return outs[0], outs[1], outs[2], moved


def _mem_weights(q, mk):
    z = _dot_nt(q, mk) * SCALE128
    e = jnp.exp(z - jnp.max(z, axis=1, keepdims=True))
    return e / jnp.sum(e, axis=1, keepdims=True)


def _mem_fwd(proj, mkv, B, S, M):
    tq = 512
    nq = S // tq
    T = B * S
    Hm = MEM_HEADS

    def body(q0, q1, q2, q3, mk_ref, mv_ref, o_ref):
        outs = []
        for h, q_ref in enumerate((q0, q1, q2, q3)):
            cols = slice(h * 128, (h + 1) * 128)
            w = _mem_weights(q_ref[...], mk_ref[:, cols])
            outs.append(_dot(w.astype(BF16), mv_ref[:, cols]))
        o_ref[...] = jnp.concatenate(outs, axis=1).astype(o_ref.dtype)

    return pl.pallas_call(
        body,
        grid=(B, nq),
        in_specs=[pl.BlockSpec((tq, 128), lambda b, i, h=h: (b * nq + i, COL_QM + h)) for h in range(Hm)]
        + [pl.BlockSpec((M, 512), lambda b, i: (b, 0)), pl.BlockSpec((M, 512), lambda b, i: (b, 1))],
        out_specs=pl.BlockSpec((tq, 512), lambda b, i: (b * nq + i, 0)),
        out_shape=jax.ShapeDtypeStruct((T, 512), BF16),
        name="mem_fwd",
    )(proj, proj, proj, proj, mkv, mkv)


def _mem_bwd(proj, mkv, dy, B, S, M):
    tq = 512
    nq = S // tq
    T = B * S
    Hm = MEM_HEADS

    def body(q0, q1, q2, q3, mk_ref, mv_ref, do_ref, dq_ref, dmk_ref, dmv_ref):
        i = pl.program_id(1)
        dqs, dmks, dmvs = [], [], []
        for h, q_ref in enumerate((q0, q1, q2, q3)):
            cols = slice(h * 128, (h + 1) * 128)
            q = q_ref[...]
            do = do_ref[:, cols]
            mk = mk_ref[:, cols]
            w = _mem_weights(q, mk)
            dw = _dot_nt(do, mv_ref[:, cols])
            ds = (w * (dw - jnp.sum(w * dw, axis=1, keepdims=True))).astype(BF16)
            dqs.append(_dot(ds, mk))
            dmks.append(_dot_tn(ds, q))
            dmvs.append(_dot_tn(w.astype(BF16), do))
        dq_ref[...] = (jnp.concatenate(dqs, axis=1) * SCALE128).astype(dq_ref.dtype)

        @pl.when(i == 0)
        def _():
            dmk_ref[...] = jnp.zeros_like(dmk_ref)
            dmv_ref[...] = jnp.zeros_like(dmv_ref)

        dmk_ref[...] += jnp.concatenate(dmks, axis=1) * SCALE128
        dmv_ref[...] += jnp.concatenate(dmvs, axis=1)

    q_spec = pl.BlockSpec((tq, 512), lambda b, i: (b * nq + i, 0))
    m_out = pl.BlockSpec((M, 512), lambda b, i: (b, 0))
    return pl.pallas_call(
        body,
        grid=(B, nq),
        in_specs=[pl.BlockSpec((tq, 128), lambda b, i, h=h: (b * nq + i, COL_QM + h)) for h in range(Hm)]
        + [pl.BlockSpec((M, 512), lambda b, i: (b, 0)), pl.BlockSpec((M, 512), lambda b, i: (b, 1)), q_spec],
        out_specs=[q_spec, m_out, m_out],
        out_shape=[jax.ShapeDtypeStruct((T, 512), BF16),
                   jax.ShapeDtypeStruct((B * M, 512), F32), jax.ShapeDtypeStruct((B * M, 512), F32)],
        name="mem_bwd",
    )(proj, proj, proj, proj, mkv, mkv, dy)


def _all_gather(blk, name):
    R, C = blk.shape

    def body(x_ref, out_ref, send_sems, recv_sems, local_sem):
        x, y, c = _mesh_pos()
        me, sibling = (x, y, c), (x, y, 1 - c)
        chips = [(1 - x, y), (x, 1 - y), (1 - x, 1 - y)]

        def slot(px, py, pc):
            return out_ref.at[4 * px + 2 * py + pc]

        def copy(k, block, to, src=None):
            return pltpu.make_async_remote_copy(
                src_ref=slot(*block) if src is None else src, dst_ref=slot(*block),
                send_sem=send_sems.at[k], recv_sem=recv_sems.at[k],
                device_id=to, device_id_type=pl.DeviceIdType.MESH)

        mine = pltpu.make_async_copy(x_ref, slot(*me), local_sem)
        mine.start()
        first = [copy(0, me, sibling, src=x_ref)]
        first += [copy(1 + j, me, (*chip, c), src=x_ref) for j, chip in enumerate(chips)]
        for cp in first:
            cp.start()
        passed = [copy(4 + j, (*chip, c), sibling) for j, chip in enumerate(chips)]
        for j, chip in enumerate(chips):
            copy(1 + j, (*chip, c), me).wait_recv()
            passed[j].start()
        copy(0, sibling, me).wait_recv()
        for j, chip in enumerate(chips):
            copy(4 + j, (*chip, 1 - c), me).wait_recv()
        for cp in first + passed:
            cp.wait_send()
        mine.wait()

    return pl.pallas_call(
        body,
        in_specs=[pl.BlockSpec(memory_space=pl.ANY)],
        out_specs=pl.BlockSpec(memory_space=pl.ANY),
        out_shape=jax.ShapeDtypeStruct((N_DEV, R, C), blk.dtype),
        scratch_shapes=[pltpu.SemaphoreType.DMA((7,)), pltpu.SemaphoreType.DMA((7,)), pltpu.SemaphoreType.DMA],
        name=name,
    )(blk)


_HBM = pl.BlockSpec(memory_space=pltpu.HBM)
_SEM = pl.BlockSpec(memory_space=pltpu.SEMAPHORE)


def _scatter_start(parts, land, window, carried, name):
    hosted = _Hosted(scatters=[parts], window=window)

    def body(p_ref, land_ref, c_ref, send_sems, recv_sems, local_sems, p_thru, land_thru, c_thru):
        for cp in hosted.copies([p_ref], [land_ref], send_sems, recv_sems, local_sems):
            cp.start()

    sems = (pltpu.SemaphoreType.DMA((7,)), pltpu.SemaphoreType.DMA((7,)), pltpu.SemaphoreType.DMA((1,)))
    hbm = lambda a: pltpu.HBM(a.shape, a.dtype)
    outs = pl.pallas_call(
        body, name=name,
        out_shape=sems + (hbm(parts), hbm(land), hbm(carried)),
        in_specs=(_HBM, _HBM, _HBM), out_specs=(_SEM, _SEM, _SEM, _HBM, _HBM, _HBM),
        input_output_aliases={0: 3, 1: 4, 2: 5},
        compiler_params=pltpu.CompilerParams(has_side_effects=pltpu.SideEffectType.DATAFLOW_SIDE_EFFECTING),
    )(pltpu.with_memory_space_constraint(parts, pltpu.HBM),
      pltpu.with_memory_space_constraint(land, pltpu.HBM),
      pltpu.with_memory_space_constraint(carried, pltpu.HBM))
    return (outs[:4], window), outs[4], outs[5]


def _scatter_wait(flight, land, after, name):
    (send_sems, recv_sems, local_sems, p_thru), window = flight
    hosted = _Hosted(scatters=[p_thru], window=window)

    def body(p_ref, land_ref, send, recv, local, after_ref, p_dead, got_ref):
        for cp in hosted.copies([p_ref], [land_ref], send, recv, local):
            cp.wait()

    hbm = lambda a: pltpu.HBM(a.shape, a.dtype)
    return pl.pallas_call(
        body, name=name,
        out_shape=(hbm(p_thru), hbm(land)),
        in_specs=(_HBM, _HBM, _SEM, _SEM, _SEM, pl.BlockSpec(memory_space=pl.ANY)), out_specs=(_HBM, _HBM),
        input_output_aliases={0: 0, 1: 1},
        compiler_params=pltpu.CompilerParams(has_side_effects=pltpu.SideEffectType.DATAFLOW_SIDE_EFFECTING),
    )(p_thru, land, send_sems, recv_sems, local_sems, after)[1]


def _adamw(parts, w, m, v, name):
    R, C = w.shape
    tr = R
    for cand in (368, 352, 256, 176, 128, 64, 32, 16, 8):
        if R % cand == 0 and cand * C * 4 <= 1536 * 1024:
            tr = cand
            break
    c1 = 1.0 - ADAM_B1 ** ADAM_STEP
    c2 = 1.0 - ADAM_B2 ** ADAM_STEP

    def body(p_ref, w_ref, m_ref, v_ref, g_ref, d_ref, nm_ref, nv_ref):
        g = p_ref[0].astype(F32)
        for d in range(1, N_DEV):
            g = g + p_ref[d].astype(F32)
        nm = ADAM_B1 * m_ref[...] + (1.0 - ADAM_B1) * g
        nv = ADAM_B2 * v_ref[...] + (1.0 - ADAM_B2) * (g * g)
        g_ref[...] = g
        nm_ref[...] = nm
        nv_ref[...] = nv
        d_ref[...] = -ADAM_LR * ((nm / c1) / (jnp.sqrt(nv / c2) + ADAM_EPS) + ADAM_WD * w_ref[...])

    row = pl.BlockSpec((tr, C), lambda i: (i, 0))
    out = jax.ShapeDtypeStruct((R, C), F32)
    return pl.pallas_call(
        body,
        grid=(R // tr,),
        in_specs=[pl.BlockSpec((N_DEV, tr, C), lambda i: (0, i, 0)), row, row, row],
        out_specs=[row] * 4,
        out_shape=[out] * 4,
        compiler_params=_params(),
        name=name,
    )(parts, w, m, v)


def _col_shards(g):
    R, C8 = g.shape
    return g.reshape(R, N_DEV, C8 // N_DEV).transpose(1, 0, 2)


def _row_shards(g):
    R8, C = g.shape
    return g.reshape(N_DEV, R8 // N_DEV, C)


def _cols_full(gathered):
    n, R, C = gathered.shape
    return gathered.transpose(1, 0, 2).reshape(R, n * C)


_BIG = ("w_in", "w_mem_kv", "w_branch_swa", "w_branch_sb", "w_branch_mem", "w_out", "w_gate", "w_up", "w_down")
_COL_SHARDED = ("w_branch_swa", "w_branch_sb", "w_branch_mem")
_TRANSPOSED = ("w_in", "w_gate", "w_up")
_SMALL = ("ln_mix_pre", "ln_mix_post", "swa_sinks", "rel_bias", "ln_mem", "ln_ffn_pre", "ln_ffn_post")
_ORDER = ("ln_mix_pre", "ln_mix_post", "w_in", "swa_sinks", "rel_bias", "ln_mem", "w_mem_kv", "w_branch_swa",
          "w_branch_sb", "w_branch_mem", "w_out", "ln_ffn_pre", "ln_ffn_post", "w_gate", "w_up", "w_down")


def _pack_small(d, last=None):
    rows = [d["ln_mix_pre"], d["ln_mix_post"], d["ln_mem"], d["ln_ffn_pre"], d["ln_ffn_post"],
            jnp.pad(d["swa_sinks"].reshape(1, -1), ((0, 0), (0, D_MODEL - SWA_Q_HEADS))),
            jnp.pad(d["rel_bias"].reshape(1, -1), ((0, 0), (0, D_MODEL - N_BUCKETS * SWA_Q_HEADS))),
            jnp.zeros((1, D_MODEL), F32) if last is None else last]
    return jnp.concatenate([r.astype(F32) for r in rows], axis=0)


def _unpack_small(a):
    return dict(ln_mix_pre=a[0:1], ln_mix_post=a[1:2], ln_mem=a[2:3], ln_ffn_pre=a[3:4], ln_ffn_post=a[4:5],
                swa_sinks=a[5:6, :SWA_Q_HEADS],
                rel_bias=a[6, :N_BUCKETS * SWA_Q_HEADS].reshape(N_BUCKETS, SWA_Q_HEADS))


def kernel(x, mem, ln_mix_pre, ln_mix_post, w_in, swa_sinks, rel_bias, ln_mem, w_mem_kv, w_branch_swa, w_branch_sb, w_branch_mem, w_out, ln_ffn_pre, ln_ffn_post, w_gate, w_up, w_down, loss_target, m_ln_mix_pre, m_ln_mix_post, m_w_in, m_swa_sinks, m_rel_bias, m_ln_mem, m_w_mem_kv, m_w_branch_swa, m_w_branch_sb, m_w_branch_mem, m_w_out, m_ln_ffn_pre, m_ln_ffn_post, m_w_gate, m_w_up, m_w_down, v_ln_mix_pre, v_ln_mix_post, v_w_in, v_swa_sinks, v_rel_bias, v_ln_mem, v_w_mem_kv, v_w_branch_swa, v_w_branch_sb, v_w_branch_mem, v_w_out, v_ln_ffn_pre, v_ln_ffn_post, v_w_gate, v_w_up, v_w_down):
    w = dict(ln_mix_pre=ln_mix_pre, ln_mix_post=ln_mix_post, w_in=w_in[0], swa_sinks=swa_sinks, rel_bias=rel_bias,
             ln_mem=ln_mem, w_mem_kv=w_mem_kv[0], w_branch_swa=w_branch_swa[0], w_branch_sb=w_branch_sb[0],
             w_branch_mem=w_branch_mem[0], w_out=w_out[0], ln_ffn_pre=ln_ffn_pre, ln_ffn_post=ln_ffn_post,
             w_gate=w_gate[0], w_up=w_up[0], w_down=w_down[0])
    mom = dict(ln_mix_pre=m_ln_mix_pre, ln_mix_post=m_ln_mix_post, w_in=m_w_in[0], swa_sinks=m_swa_sinks,
               rel_bias=m_rel_bias, ln_mem=m_ln_mem, w_mem_kv=m_w_mem_kv[0], w_branch_swa=m_w_branch_swa[0],
               w_branch_sb=m_w_branch_sb[0], w_branch_mem=m_w_branch_mem[0], w_out=m_w_out[0],
               ln_ffn_pre=m_ln_ffn_pre, ln_ffn_post=m_ln_ffn_post, w_gate=m_w_gate[0], w_up=m_w_up[0],
               w_down=m_w_down[0])
    var = dict(ln_mix_pre=v_ln_mix_pre, ln_mix_post=v_ln_mix_post, w_in=v_w_in[0], swa_sinks=v_swa_sinks,
               rel_bias=v_rel_bias, ln_mem=v_ln_mem, w_mem_kv=v_w_mem_kv[0], w_branch_swa=v_w_branch_swa[0],
               w_branch_sb=v_w_branch_sb[0], w_branch_mem=v_w_branch_mem[0], w_out=v_w_out[0],
               ln_ffn_pre=v_ln_ffn_pre, ln_ffn_post=v_ln_ffn_post, w_gate=v_w_gate[0], w_up=v_w_up[0],
               w_down=v_w_down[0])
    B, S, D = x.shape
    M = mem.shape[1]
    T = B * S
    F = D_FF
    x2 = x.reshape(T, D)
    mem2 = mem.reshape(B * M, D)
    t2 = loss_target.reshape(T, D)
    buckets = jnp.asarray(_swa_buckets())
    for d in (w, mom, var):
        for n in _TRANSPOSED:
            d[n] = d[n].T
    wb = {n: w[n].astype(BF16) for n in _BIG}
    full = {}

    def landed(names, got):
        for n, g in zip(names, got):
            full[n] = _cols_full(g) if n in _COL_SHARDED else g.reshape(-1, g.shape[-1])

    def shards(n, g):
        return _col_shards(g) if n in _COL_SHARDED else _row_shards(g)

    landed(["w_in"], [_all_gather(wb["w_in"], "ag_w_in")])
    u = _rms_fwd(x2, ln_mix_pre, "rms_mix_pre")
    early = ["w_mem_kv", "w_branch_swa", "w_branch_sb", "w_branch_mem"]
    proj, got = _matmul([(u, full["w_in"])], "nt", BF16, 512, IN_W // 2, D, "proj_in",
                        hosted=_Hosted(gathers=[wb[n] for n in early]))
    landed(early, got)
    mn = _rms_fwd(mem2, ln_mem, "rms_mem")
    mkv = _matmul([(mn, full["w_mem_kv"])], "nn", BF16, 512, 1024, D, "proj_mem")
    bias_tab = _swa_bias_table(rel_bias, buckets)
    y_swa, got = _swa_fwd(proj, swa_sinks, bias_tab, B, S, hosted=_Hosted(gathers=[wb["w_out"]]))
    landed(["w_out"], got)
    late = ["w_gate", "w_up"]
    y_sb, rtot, got = _sb_fwd(proj, B, S, hosted=_Hosted(gathers=[wb[n] for n in late]))
    landed(late, got)
    y_mem = _mem_fwd(proj, mkv, B, S, M)
    wbs = (full["w_branch_swa"], full["w_branch_sb"], full["w_branch_mem"])
    merged, p_swa, p_sb, p_mem = _branch_gate(proj, (y_swa, y_sb, y_mem), wbs)
    mix, h1, u2 = _post_pre(x2, merged, full["w_out"], ln_mix_post, ln_ffn_pre)
    a, zg, zu, got = _ffn_up(u2, full["w_gate"], full["w_up"], hosted=_Hosted(gathers=[wb["w_down"]]))
    landed(["w_down"], got)
    dffn, dh2, loss_tile, d_ln_ffn_post = _loss_head(a, full["w_down"], h1, t2, ln_ffn_post)

    part = {}
    part["w_down"] = _matmul([(a, dffn)], "tn", BF16, F // 2, 1024, 1024, "dw_down")
    dzg, dzu = _ffn_down_bwd(dffn, full["w_down"], zg, zu)
    part["w_gate"] = _matmul([(dzg, u2)], "tn", BF16, F // 2, 1024, 1024, "dw_gate")
    part["w_up"] = _matmul([(dzu, u2)], "tn", BF16, F // 2, 1024, 1024, "dw_up")
    dh1, dmix, d_ln_ffn_pre, d_ln_mix_post = _mid_bwd(dzg, dzu, full["w_gate"], full["w_up"], h1, dh2, mix,
                                                      ln_ffn_pre, ln_mix_post)
    part["w_out"] = _matmul([(merged, dmix)], "tn", BF16, 1024, 1024, 1024, "dw_out")
    (dp_swa, dp_sb, dp_mem, dg0, dg1, dg2, dy_swa, dy_sb, dy_mem) = _gate_bwd(
        dmix, full["w_out"], (p_swa, p_sb, p_mem), proj, wbs)
    part["w_branch_swa"] = _matmul([(y_swa, dp_swa)], "tn", BF16, 512, 1024, 1024, "dw_branch_swa")
    part["w_branch_sb"] = _matmul([(y_sb, dp_sb)], "tn", BF16, 512, 1024, 1024, "dw_branch_sb")
    part["w_branch_mem"] = _matmul([(y_mem, dp_mem)], "tn", BF16, 512, 1024, 1024, "dw_branch_mem")
    dqm, dmk, dmv = _mem_bwd(proj, mkv, dy_mem, B, S, M)
    dmkv = jnp.concatenate([dmk, dmv], axis=1).astype(BF16)
    part["w_mem_kv"] = _matmul([(mn, dmkv)], "tn", BF16, 1024, 1024, 512, "dw_mem_kv")
    dmn = _matmul([(dmkv, full["w_mem_kv"])], "nt", F32, 512, 1024, 1024, "d_mn")
    d_ln_mem = _gain_grad(mem2, dmn)
    behind_swa = ["w_out", "w_branch_swa", "w_branch_sb"]
    dqa, dka, dva, dbias, dsink, got = _swa_bwd(
        proj, dy_swa, swa_sinks, bias_tab, B, S, hosted=_Hosted(scatters=[shards(n, part[n]) for n in behind_swa]))
    recv = dict(zip(behind_swa, got))
    behind_sb = ["w_down", "w_gate", "w_up", "w_branch_mem", "w_mem_kv"]
    dqb, dkb, dvb, got = _sb_bwd(proj, dy_sb, rtot, B, S,
                                 hosted=_Hosted(scatters=[shards(n, part[n]) for n in behind_sb]))
    recv.update(zip(behind_sb, got))
    d_rel_bias = _swa_bias_grad(dbias, buckets)[:, :SWA_Q_HEADS]
    d_sinks = dsink[:, 0].reshape(1, SWA_Q_HEADS)
    dproj = jnp.concatenate([dqa, dka.astype(BF16), dva.astype(BF16), dqb, dkb.astype(BF16), dvb.astype(BF16),
                             dqm, dg0, dg1, dg2], axis=1)
    half = D // 2
    land = lax.empty((N_DEV, IN_W // N_DEV, D), BF16)
    flights = []
    for t in range(2):
        dw_half = _matmul([(dproj, u)], "tn", BF16, IN_W // 2, half, 1024, "dw_in_%d" % t, n_cols=half, n_off=t)
        flight, land, dproj = _scatter_start(_row_shards(dw_half), land, (t * half, half), dproj,
                                             "rs_w_in_start_%d" % t)
        flights.append(flight)
    grad_x, d_ln_mix_pre = _pre_bwd(dproj, full["w_in"], x2, dh1, ln_mix_pre)

    out = {n: _adamw(recv[n], w[n], mom[n], var[n], "adamw_" + n) for n in _BIG if n != "w_in"}
    small_grads = dict(ln_mix_pre=d_ln_mix_pre, ln_mix_post=d_ln_mix_post, swa_sinks=d_sinks, rel_bias=d_rel_bias,
                       ln_mem=d_ln_mem, ln_ffn_pre=d_ln_ffn_pre, ln_ffn_post=d_ln_ffn_post)
    small_parts = _all_gather(_pack_small(small_grads, jnp.tile(loss_tile[0:1], (1, D // 128))), "ag_small")
    res = _adamw(small_parts, _pack_small(w), _pack_small(mom), _pack_small(var), "adamw_small")
    loss = res[0][7, 0]
    small = [_unpack_small(r) for r in res]
    for n in _SMALL:
        out[n] = tuple(s[n] for s in small)
    for t, flight in enumerate(flights):
        land = _scatter_wait(flight, land, res[0], "rs_w_in_wait_%d" % t)
    out["w_in"] = _adamw(land, w["w_in"], mom["w_in"], var["w_in"], "adamw_w_in")
    for n in _TRANSPOSED:
        out[n] = tuple(o.T for o in out[n])

    like = dict(ln_mix_pre=ln_mix_pre, ln_mix_post=ln_mix_post, w_in=w_in, swa_sinks=swa_sinks, rel_bias=rel_bias,
                ln_mem=ln_mem, w_mem_kv=w_mem_kv, w_branch_swa=w_branch_swa, w_branch_sb=w_branch_sb,
                w_branch_mem=w_branch_mem, w_out=w_out, ln_ffn_pre=ln_ffn_pre, ln_ffn_post=ln_ffn_post,
                w_gate=w_gate, w_up=w_up, w_down=w_down)
    result = [loss, grad_x.reshape(B, S, D)]
    for k in range(4):
        result += [out[n][k].reshape(like[n].shape) for n in _ORDER]
    return tuple(result)
```

```python
import functools
import math

import numpy as np
import jax
import jax.numpy as jnp
from jax import lax
from jax.experimental import pallas as pl
from jax.experimental.pallas import tpu as pltpu

F32 = jnp.float32
BF16 = jnp.bfloat16

N_DEV = 8
D_MODEL = 1024
BLOCK = 128
EPS = 1e-6
HEAD_DIM = 64
SWA_Q_HEADS = 8
SWA_WINDOW = 128
N_BUCKETS = 32
MAX_DISTANCE = 128
MEM_HEADS = 4
MEM_HEAD_DIM = 128
D_FF = 2816
IN_W = 5888
COL_QA, COL_KA, COL_VA, COL_QB, COL_KB, COL_VB, COL_QM, COL_GL = 0, 4, 5, 6, 10, 14, 18, 22
SCALE64 = HEAD_DIM ** -0.5
SCALE128 = MEM_HEAD_DIM ** -0.5
NEG = -1e30

ADAM_LR = 0.001
ADAM_B1 = 0.9
ADAM_B2 = 0.999
ADAM_EPS = 1e-08
ADAM_WD = 0.01
ADAM_STEP = 10

VMEM_LIMIT_BYTES = 56 * 1024 * 1024


def _params(**kw):
    return pltpu.CompilerParams(vmem_limit_bytes=VMEM_LIMIT_BYTES, **kw)


def _dot(a, b):
    return jnp.dot(a, b, preferred_element_type=F32)


def _dot_nt(a, b):
    return lax.dot_general(a, b, (((1,), (1,)), ((), ())), preferred_element_type=F32)


def _dot_tn(a, b):
    return lax.dot_general(a, b, (((0,), (0,)), ((), ())), preferred_element_type=F32)


def _dot_split(x, m2):
    hi = x.astype(BF16)
    lo = (x - hi.astype(F32)).astype(BF16)
    return _dot(jnp.concatenate([hi, lo], axis=1), m2)


def _mesh_pos():
    return lax.axis_index("x"), lax.axis_index("y"), lax.axis_index("c")


class _Hosted:
    def __init__(self, gathers=(), scatters=(), window=None):
        self.items = [("g", a) for a in gathers] + [("s", a) for a in scatters]
        self.n = len(self.items)
        self.window = window

    def operands(self):
        return [a for _, a in self.items]

    def specs(self):
        return [pl.BlockSpec(memory_space=pl.ANY)] * self.n

    def out_shapes(self):
        return [jax.ShapeDtypeStruct(((N_DEV,) + a.shape) if kind == "g" else a.shape, a.dtype)
                for kind, a in self.items]

    def scratch(self):
        return [pltpu.SemaphoreType.DMA((7 * self.n,)), pltpu.SemaphoreType.DMA((7 * self.n,)),
                pltpu.SemaphoreType.DMA((self.n,))]

    def copies(self, in_refs, out_refs, send_sems, recv_sems, local_sems):
        x, y, c = _mesh_pos()
        me = 4 * x + 2 * y + c
        out = []
        for t, (kind, _) in enumerate(self.items):
            own = in_refs[t] if kind == "g" else in_refs[t].at[me]
            dst = out_refs[t].at[me]
            if self.window is not None:
                dst = out_refs[t].at[me, :, pl.ds(*self.window)]
            out.append(pltpu.make_async_copy(own, dst, local_sems.at[t]))
            for k in range(1, N_DEV):
                px, py, pc = x ^ (k >> 2), y ^ ((k >> 1) & 1), c ^ (k & 1)
                src = in_refs[t] if kind == "g" else in_refs[t].at[4 * px + 2 * py + pc]
                out.append(pltpu.make_async_remote_copy(
                    src_ref=src, dst_ref=dst,
                    send_sem=send_sems.at[7 * t + k - 1], recv_sem=recv_sems.at[7 * t + k - 1],
                    device_id=(px, py, pc), device_id_type=pl.DeviceIdType.MESH))
        return out


def _host(body, n_in, n_out, hosted, grid):
    if hosted is None:
        return body
    nc = hosted.n

    def wrapped(*refs):
        ins = refs[:n_in]
        cin = refs[n_in:n_in + nc]
        outs = refs[n_in + nc:n_in + nc + n_out]
        cout = refs[n_in + nc + n_out:n_in + 2 * nc + n_out]
        scratch = refs[n_in + 2 * nc + n_out:len(refs) - 3]
        sems = refs[len(refs) - 3:]
        ids = [pl.program_id(d) for d in range(len(grid))]
        first = functools.reduce(lambda a, b: a & b, [i == 0 for i in ids])
        last = functools.reduce(lambda a, b: a & b, [i == g - 1 for i, g in zip(ids, grid)])

        @pl.when(first)
        def _():
            for cp in hosted.copies(cin, cout, *sems):
                cp.start()

        body(*ins, *outs, *scratch)

        @pl.when(last)
        def _():
            for cp in hosted.copies(cin, cout, *sems):
                cp.wait()

    return wrapped


def _hosted_call(body, grid, in_specs, out_specs, out_shape, scratch_shapes, hosted, name, args):
    n_out = len(out_specs)
    if hosted is None:
        outs = pl.pallas_call(body, grid=grid, in_specs=in_specs, out_specs=out_specs, out_shape=out_shape,
                              scratch_shapes=scratch_shapes, compiler_params=_params(), name=name)(*args)
        return list(outs), []
    outs = pl.pallas_call(
        _host(body, len(in_specs), n_out, hosted, grid),
        grid=grid,
        in_specs=list(in_specs) + hosted.specs(),
        out_specs=list(out_specs) + hosted.specs(),
        out_shape=list(out_shape) + hosted.out_shapes(),
        scratch_shapes=list(scratch_shapes) + hosted.scratch(),
        compiler_params=_params(),
        name=name,
    )(*args, *hosted.operands())
    return list(outs[:n_out]), list(outs[n_out:])


_DIMS = {"nn": (((1,), (0,)), ((), ())), "nt": (((1,), (1,)), ((), ())), "tn": (((0,), (0,)), ((), ()))}


def _matmul(pairs, mode, out_dtype, tm, tn, tk, name, hosted=None, n_cols=None, n_off=0):
    a0, b0 = pairs[0]
    if mode == "nn":
        (M, K), N = a0.shape, b0.shape[1]
    elif mode == "nt":
        (M, K), N = a0.shape, b0.shape[0]
    else:
        (K, M), N = a0.shape, b0.shape[1]
    N = N if n_cols is None else n_cols
    tm, tn, tk = min(tm, M), min(tn, N), min(tk, K)
    assert M % tm == 0 and N % tn == 0 and K % tk == 0, (name, M, N, K, tm, tn, tk)
    nm, nn, nk = M // tm, N // tn, K // tk
    npair = len(pairs)
    dims = _DIMS[mode]

    def body(*refs):
        ab = refs[:2 * npair]
        o_ref = refs[2 * npair]
        acc_ref = refs[2 * npair + 1]
        k = pl.program_id(2)
        part = lax.dot_general(ab[0][...], ab[1][...], dims, preferred_element_type=F32)
        for q in range(1, npair):
            part += lax.dot_general(ab[2 * q][...], ab[2 * q + 1][...], dims, preferred_element_type=F32)
        if nk == 1:
            o_ref[...] = part.astype(o_ref.dtype)
        else:
            @pl.when(k == 0)
            def _():
                acc_ref[...] = part

            @pl.when(k > 0)
            def _():
                acc_ref[...] += part

            @pl.when(k == nk - 1)
            def _():
                o_ref[...] = acc_ref[...].astype(o_ref.dtype)

    if mode == "nn":
        a_spec = pl.BlockSpec((tm, tk), lambda n, m, k: (m, k))
        b_spec = pl.BlockSpec((tk, tn), lambda n, m, k: (k, n + n_off))
    elif mode == "nt":
        a_spec = pl.BlockSpec((tm, tk), lambda n, m, k: (m, k))
        b_spec = pl.BlockSpec((tn, tk), lambda n, m, k: (n + n_off, k))
    else:
        a_spec = pl.BlockSpec((tk, tm), lambda n, m, k: (k, m))
        b_spec = pl.BlockSpec((tk, tn), lambda n, m, k: (k, n + n_off))
    args = [t for pr in pairs for t in pr]
    outs, moved = _hosted_call(
        body, (nn, nm, nk), [a_spec, b_spec] * npair, [pl.BlockSpec((tm, tn), lambda n, m, k: (m, n))],
        [jax.ShapeDtypeStruct((M, N), out_dtype)], [pltpu.VMEM((tm, tn) if nk > 1 else (8, 128), F32)],
        hosted, name, args)
    return outs[0] if hosted is None else (outs[0], moved)


def _rms_fwd(x, g, name):
    T, D = x.shape
    tr = min(512, T)

    def body(x_ref, g_ref, u_ref):
        xf = x_ref[...]
        r = lax.rsqrt(jnp.mean(xf * xf, axis=-1, keepdims=True) + EPS)
        u_ref[...] = ((xf * r) * g_ref[...]).astype(u_ref.dtype)

    return pl.pallas_call(
        body,
        grid=(T // tr,),
        in_specs=[pl.BlockSpec((tr, D), lambda i: (i, 0)), pl.BlockSpec((1, D), lambda i: (0, 0))],
        out_specs=pl.BlockSpec((tr, D), lambda i: (i, 0)),
        out_shape=jax.ShapeDtypeStruct((T, D), BF16),
        name=name,
    )(x, g)


def _rms_bwd_terms(xin, g, dy):
    r = lax.rsqrt(jnp.mean(xin * xin, axis=-1, keepdims=True) + EPS)
    xh = xin * r
    dg = jnp.sum(dy * xh, axis=0, keepdims=True)
    dxh = dy * g
    dx = r * (dxh - xh * jnp.mean(dxh * xh, axis=-1, keepdims=True))
    return dx, dg


GATE_TC = 256


def _branch_gate(proj, ys, wbs):
    T = proj.shape[0]
    D = D_MODEL
    tr, tc = min(1024, T), GATE_TC
    nc = D // tc
    gl0 = COL_GL * 128 // tc

    def body(ya, yb, yc, wa, wb, wc, g0, g1, g2, merged_ref, pa, pb, pc):
        acc = jnp.zeros((tr, tc), F32)
        for y_ref, w_ref, g_ref, p_ref in ((ya, wa, g0, pa), (yb, wb, g1, pb), (yc, wc, g2, pc)):
            p = _dot(y_ref[...], w_ref[...])
            p_ref[...] = p.astype(p_ref.dtype)
            acc += jax.nn.sigmoid(g_ref[...].astype(F32)) * p
        merged_ref[...] = acc.astype(merged_ref.dtype)

    y_spec = pl.BlockSpec((tr, 512), lambda i, n: (i, 0))
    w_spec = pl.BlockSpec((512, tc), lambda i, n: (0, n))
    o_spec = pl.BlockSpec((tr, tc), lambda i, n: (i, n))
    gl_specs = [pl.BlockSpec((tr, tc), lambda i, n, j=j: (i, gl0 + j * nc + n)) for j in range(3)]
    out = jax.ShapeDtypeStruct((T, D), BF16)
    return pl.pallas_call(
        body,
        grid=(T // tr, nc),
        in_specs=[y_spec] * 3 + [w_spec] * 3 + gl_specs,
        out_specs=[o_spec] * 4,
        out_shape=[out] * 4,
        compiler_params=_params(),
        name="branch_gate",
    )(*ys, *wbs, proj, proj, proj)


def _post_pre(x, merged, w_out, g_post, g_pre):
    T, D = x.shape
    tr = 512

    def body(x_ref, m_ref, w_ref, gp_ref, gq_ref, mix_ref, h1_ref, u2_ref):
        mx = _dot(m_ref[...], w_ref[...])
        mix_ref[...] = mx
        r = lax.rsqrt(jnp.mean(mx * mx, axis=-1, keepdims=True) + EPS)
        h1 = x_ref[...] + (mx * r) * gp_ref[...]
        h1_ref[...] = h1
        r2 = lax.rsqrt(jnp.mean(h1 * h1, axis=-1, keepdims=True) + EPS)
        u2_ref[...] = ((h1 * r2) * gq_ref[...]).astype(u2_ref.dtype)

    row = pl.BlockSpec((tr, D), lambda i: (i, 0))
    vec = pl.BlockSpec((1, D), lambda i: (0, 0))
    f32 = jax.ShapeDtypeStruct((T, D), F32)
    return pl.pallas_call(
        body,
        grid=(T // tr,),
        in_specs=[row, row, pl.BlockSpec((D, D), lambda i: (0, 0)), vec, vec],
        out_specs=[row, row, row],
        out_shape=[f32, f32, jax.ShapeDtypeStruct((T, D), BF16)],
        compiler_params=_params(),
        name="post_pre",
    )(x, merged, w_out, g_post, g_pre)


def _ffn_up(u2, w_gate_t, w_up_t, hosted=None):
    T, D = u2.shape
    F = D_FF
    tm, tn = 512, F // 2

    def body(u_ref, wg_ref, wu_ref, a_ref, zg_ref, zu_ref):
        u = u_ref[...]
        zg = _dot_nt(u, wg_ref[...])
        zu = _dot_nt(u, wu_ref[...])
        a_ref[...] = (zg * jax.nn.sigmoid(zg) * zu).astype(a_ref.dtype)
        zg_ref[...] = zg.astype(zg_ref.dtype)
        zu_ref[...] = zu.astype(zu_ref.dtype)

    o_spec = pl.BlockSpec((tm, tn), lambda n, m: (m, n))
    out = jax.ShapeDtypeStruct((T, F), BF16)
    outs, moved = _hosted_call(
        body, (F // tn, T // tm),
        [pl.BlockSpec((tm, D), lambda n, m: (m, 0)),
         pl.BlockSpec((tn, D), lambda n, m: (n, 0)),
         pl.BlockSpec((tn, D), lambda n, m: (n, 0))],
        [o_spec] * 3, [out] * 3, [], hosted, "ffn_up", (u2, w_gate_t, w_up_t))
    return (*outs, moved)


def _loss_head(a, w_down, h1, target, g_post):
    T, D = h1.shape
    F = a.shape[1]
    tr = 512

    def body(a_ref, w_ref, h1_ref, t_ref, g_ref, dffn_ref, dh2_ref, loss_ref, dg_ref):
        i = pl.program_id(0)
        f = _dot(a_ref[...], w_ref[...])
        g = g_ref[...]
        r = lax.rsqrt(jnp.mean(f * f, axis=-1, keepdims=True) + EPS)
        xh = f * r
        err = (h1_ref[...] + xh * g) - t_ref[...]
        part = 0.5 * jnp.sum(jnp.mean(err * err, axis=-1, keepdims=True), axis=0, keepdims=True)
        dh2 = err * (1.0 / D)
        dh2_ref[...] = dh2
        dgp = jnp.sum(dh2 * xh, axis=0, keepdims=True)
        dxh = dh2 * g
        dffn_ref[...] = (r * (dxh - xh * jnp.mean(dxh * xh, axis=-1, keepdims=True))).astype(dffn_ref.dtype)

        @pl.when(i == 0)
        def _():
            loss_ref[...] = jnp.zeros_like(loss_ref)
            dg_ref[...] = jnp.zeros_like(dg_ref)

        loss_ref[...] += jnp.broadcast_to(part, loss_ref.shape)
        dg_ref[...] += dgp

    row = pl.BlockSpec((tr, D), lambda i: (i, 0))
    vec = pl.BlockSpec((1, D), lambda i: (0, 0))
    return pl.pallas_call(
        body,
        grid=(T // tr,),
        in_specs=[pl.BlockSpec((tr, F), lambda i: (i, 0)), pl.BlockSpec((F, D), lambda i: (0, 0)), row, row, vec],
        out_specs=[row, row, pl.BlockSpec((8, 128), lambda i: (0, 0)), vec],
        out_shape=[jax.ShapeDtypeStruct((T, D), BF16), jax.ShapeDtypeStruct((T, D), F32),
                   jax.ShapeDtypeStruct((8, 128), F32), jax.ShapeDtypeStruct((1, D), F32)],
        compiler_params=_params(),
        name="loss_head",
    )(a, w_down, h1, target, g_post)


def _ffn_down_bwd(dffn, wd, zg, zu):
    T, D = dffn.shape
    F = D_FF
    tm, tn = 512, F // 2

    def body(d_ref, w_ref, zg_ref, zu_ref, dzg_ref, dzu_ref):
        d = d_ref[...]
        for lo in range(0, tn, 512):
            cols = slice(lo, min(lo + 512, tn))
            da = _dot_nt(d, w_ref[cols, :])
            zg = zg_ref[:, cols].astype(F32)
            zu = zu_ref[:, cols].astype(F32)
            s = jax.nn.sigmoid(zg)
            dzu_ref[:, cols] = (da * (zg * s)).astype(dzu_ref.dtype)
            dzg_ref[:, cols] = (da * zu * (s * (1.0 + zg * (1.0 - s)))).astype(dzg_ref.dtype)

    z_spec = pl.BlockSpec((tm, tn), lambda n, m: (m, n))
    out = jax.ShapeDtypeStruct((T, F), BF16)
    return pl.pallas_call(
        body,
        grid=(F // tn, T // tm),
        in_specs=[pl.BlockSpec((tm, D), lambda n, m: (m, 0)), pl.BlockSpec((tn, D), lambda n, m: (n, 0)),
                  z_spec, z_spec],
        out_specs=[z_spec, z_spec],
        out_shape=[out, out],
        compiler_params=_params(),
        name="ffn_down_bwd",
    )(dffn, wd, zg, zu)


def _matmul_rows(pairs, tm, tk, rows_in, vecs_in, rows_out, n_vec_out, epilogue, name):
    a0, b0 = pairs[0]
    (M, K), N = a0.shape, b0.shape[1]
    tm, tk = min(tm, M), min(tk, K)
    assert M % tm == 0 and K % tk == 0, (name, M, K, tm, tk)
    nk = K // tk
    npair = len(pairs)
    n_in = 2 * npair + len(rows_in) + len(vecs_in)

    def body(*refs):
        ab = refs[:2 * npair]
        r_in = refs[2 * npair:2 * npair + len(rows_in)]
        v_in = refs[2 * npair + len(rows_in):n_in]
        r_out = refs[n_in:n_in + len(rows_out)]
        v_out = refs[n_in + len(rows_out):n_in + len(rows_out) + n_vec_out]
        acc_ref = refs[-1]
        m = pl.program_id(0)
        k = pl.program_id(1)
        part = _dot(ab[0][...], ab[1][...])
        for q in range(1, npair):
            part += _dot(ab[2 * q][...], ab[2 * q + 1][...])

        @pl.when(k == 0)
        def _():
            acc_ref[...] = part

        @pl.when(k > 0)
        def _():
            acc_ref[...] += part

        @pl.when(k == nk - 1)
        def _():
            @pl.when(m == 0)
            def _():
                for v in v_out:
                    v[...] = jnp.zeros_like(v)

            epilogue(acc_ref[...], r_in, v_in, r_out, v_out)

    row = pl.BlockSpec((tm, N), lambda m, k: (m, 0))
    vec = pl.BlockSpec((1, N), lambda m, k: (0, 0))
    return pl.pallas_call(
        body,
        grid=(M // tm, nk),
        in_specs=[pl.BlockSpec((tm, tk), lambda m, k: (m, k)), pl.BlockSpec((tk, N), lambda m, k: (k, 0))] * npair
        + [row] * len(rows_in) + [vec] * len(vecs_in),
        out_specs=[row] * len(rows_out) + [vec] * n_vec_out,
        out_shape=[jax.ShapeDtypeStruct((M, N), dt) for dt in rows_out]
        + [jax.ShapeDtypeStruct((1, N), F32)] * n_vec_out,
        scratch_shapes=[pltpu.VMEM((tm, N), F32)],
        compiler_params=_params(),
        name=name,
    )(*[t for pr in pairs for t in pr], *rows_in, *vecs_in)


def _mid_bwd(dzg, dzu, w_gate_t, w_up_t, h1, dh2, mix, g_pre, g_post):
    def epilogue(du2, rows, vecs, outs, accs):
        h1_ref, dh2_ref, mix_ref = rows
        gq_ref, gp_ref = vecs
        dh1_ref, dmix_ref = outs
        dx, dgq = _rms_bwd_terms(h1_ref[...], gq_ref[...], du2)
        dh1 = dh2_ref[...] + dx
        dh1_ref[...] = dh1
        dmix, dgp = _rms_bwd_terms(mix_ref[...], gp_ref[...], dh1)
        dmix_ref[...] = dmix.astype(dmix_ref.dtype)
        accs[0][...] += dgq
        accs[1][...] += dgp

    return _matmul_rows([(dzg, w_gate_t), (dzu, w_up_t)], 512, D_FF // 2, [h1, dh2, mix], [g_pre, g_post],
                        [F32, BF16], 2, epilogue, "mid_bwd")


def _gate_bwd(dmix, w_out, ps, proj, wbs):
    T, D = dmix.shape
    tr, tc = min(1024, T), GATE_TC
    nc = D // tc
    gl0 = COL_GL * 128 // tc

    def body(dmix_ref, wo_ref, pa, pb, pc, g0, g1, g2, wa, wb, wc, dpa, dpb, dpc, dga, dgb, dgc, dya, dyb, dyc,
             acc_a, acc_b, acc_c):
        n = pl.program_id(1)
        dm = _dot_nt(dmix_ref[...], wo_ref[...])
        for p_ref, g_ref, w_ref, dp_ref, dg_ref, dy_ref, acc_ref in (
                (pa, g0, wa, dpa, dga, dya, acc_a), (pb, g1, wb, dpb, dgb, dyb, acc_b),
                (pc, g2, wc, dpc, dgc, dyc, acc_c)):
            s = jax.nn.sigmoid(g_ref[...].astype(F32))
            dp = (dm * s).astype(BF16)
            dp_ref[...] = dp
            dg_ref[...] = (dm * p_ref[...].astype(F32) * (s * (1.0 - s))).astype(dg_ref.dtype)
            part = _dot_nt(dp, w_ref[...])

            @pl.when(n == 0)
            def _():
                acc_ref[...] = part

            @pl.when(n > 0)
            def _():
                acc_ref[...] += part

            @pl.when(n == nc - 1)
            def _():
                dy_ref[...] = acc_ref[...].astype(dy_ref.dtype)

    col = pl.BlockSpec((tr, tc), lambda i, n: (i, n))
    y_spec = pl.BlockSpec((tr, 512), lambda i, n: (i, 0))
    w_spec = pl.BlockSpec((512, tc), lambda i, n: (0, n))
    gl_specs = [pl.BlockSpec((tr, tc), lambda i, n, j=j: (i, gl0 + j * nc + n)) for j in range(3)]
    big = jax.ShapeDtypeStruct((T, D), BF16)
    small = jax.ShapeDtypeStruct((T, 512), BF16)
    return pl.pallas_call(
        body,
        grid=(T // tr, nc),
        in_specs=[pl.BlockSpec((tr, D), lambda i, n: (i, 0)), pl.BlockSpec((tc, D), lambda i, n: (n, 0))]
        + [col] * 3 + gl_specs + [w_spec] * 3,
        out_specs=[col] * 6 + [y_spec] * 3,
        out_shape=[big] * 6 + [small] * 3,
        scratch_shapes=[pltpu.VMEM((tr, 512), F32)] * 3,
        compiler_params=_params(),
        name="gate_bwd",
    )(dmix, w_out, *ps, proj, proj, proj, *wbs)


def _pre_bwd(dproj, w_in_t, x, dh1, g):
    def epilogue(du, rows, vecs, outs, accs):
        x_ref, dh1_ref = rows
        dx, dg = _rms_bwd_terms(x_ref[...], vecs[0][...], du)
        outs[0][...] = dh1_ref[...] + dx
        accs[0][...] += dg

    return _matmul_rows([(dproj, w_in_t)], 512, IN_W // 2, [x, dh1], [g], [F32], 1, epilogue, "pre_bwd")


def _gain_grad(xin, dy):
    T, D = xin.shape
    tr = min(512, T)

    def body(x_ref, dy_ref, dg_ref):
        i = pl.program_id(0)
        xf = x_ref[...]
        r = lax.rsqrt(jnp.mean(xf * xf, axis=-1, keepdims=True) + EPS)

        @pl.when(i == 0)
        def _():
            dg_ref[...] = jnp.zeros_like(dg_ref)

        dg_ref[...] += jnp.sum(dy_ref[...] * (xf * r), axis=0, keepdims=True)

    row = pl.BlockSpec((tr, D), lambda i: (i, 0))
    return pl.pallas_call(
        body,
        grid=(T // tr,),
        in_specs=[row, row],
        out_specs=pl.BlockSpec((1, D), lambda i: (0, 0)),
        out_shape=jax.ShapeDtypeStruct((1, D), F32),
        name="gain_grad",
    )(xin, dy)


def _swa_buckets():
    dist = (np.arange(BLOCK)[:, None] + BLOCK) - np.arange(2 * BLOCK)[None, :]
    max_exact = N_BUCKETS // 2
    d = np.maximum(dist, 0)
    df = np.maximum(d, 1).astype(np.float32)
    large = max_exact + (np.log(df / np.float32(max_exact)) / np.float32(math.log(MAX_DISTANCE / max_exact))
                         * np.float32(N_BUCKETS - max_exact)).astype(np.int32)
    large = np.minimum(large, N_BUCKETS - 1)
    bucket = np.where(d < max_exact, d, large)
    in_win = (dist >= 0) & (dist < SWA_WINDOW)
    return np.where(in_win, bucket, -1).astype(np.int32)


def _swa_bias_table(rel_bias, buckets):
    H = SWA_Q_HEADS

    def body(rb_ref, bk_ref, o_ref):
        bk = bk_ref[...]
        for h in range(H):
            acc = jnp.full(bk.shape, NEG, F32)
            for b in range(N_BUCKETS):
                acc = jnp.where(bk == b, rb_ref[b, h], acc)
            o_ref[h] = acc

    return pl.pallas_call(
        body,
        in_specs=[pl.BlockSpec(memory_space=pltpu.SMEM), pl.BlockSpec(memory_space=pltpu.VMEM)],
        out_specs=pl.BlockSpec(memory_space=pltpu.VMEM),
        out_shape=jax.ShapeDtypeStruct((H, BLOCK, 2 * BLOCK), F32),
        name="swa_bias_table",
    )(rel_bias, buckets)


def _swa_bias_grad(dbias, buckets):
    H = SWA_Q_HEADS

    def body(db_ref, bk_ref, o_ref):
        bk = bk_ref[...]
        rows = lax.broadcasted_iota(jnp.int32, (N_BUCKETS, 128), 0)
        lanes = lax.broadcasted_iota(jnp.int32, (N_BUCKETS, 128), 1)
        acc = jnp.zeros((N_BUCKETS, 128), F32)
        for h in range(H):
            d = db_ref[h]
            for b in range(N_BUCKETS):
                s = jnp.sum(jnp.sum(jnp.where(bk == b, d, 0.0), axis=1, keepdims=True), axis=0, keepdims=True)
                acc = jnp.where((rows == b) & (lanes == h), s, acc)
        o_ref[...] = acc

    return pl.pallas_call(
        body,
        in_specs=[pl.BlockSpec(memory_space=pltpu.VMEM)] * 2,
        out_specs=pl.BlockSpec(memory_space=pltpu.VMEM),
        out_shape=jax.ShapeDtypeStruct((N_BUCKETS, 128), F32),
        name="swa_bias_grad",
    )(dbias, buckets)


SWA_GROUP = 4
SWA_ROWS = SWA_Q_HEADS * BLOCK


def _swa_kv_lanes(h):
    lane = lax.broadcasted_iota(jnp.int32, (BLOCK, BLOCK), 1)
    return (lane // HEAD_DIM) == h // SWA_GROUP


def _swa_stack(x, heads):
    blocks = []
    for h in heads:
        xp = x[:, (h // 2) * BLOCK:(h // 2 + 1) * BLOCK]
        xs = xp if h % 2 == h // SWA_GROUP else pltpu.roll(xp, HEAD_DIM, 1)
        blocks.append(jnp.where(_swa_kv_lanes(h), xs, 0.0))
    return jnp.concatenate(blocks, axis=0).astype(BF16)


def _swa_unstack(blocks):
    pairs = []
    for p in range(4):
        halves = []
        for hh in range(2):
            h = 2 * p + hh
            blk = jnp.where(_swa_kv_lanes(h), blocks[h], 0.0)
            halves.append(blk if hh == h // SWA_GROUP else pltpu.roll(blk, HEAD_DIM, 1))
        pairs.append(halves[0] + halves[1])
    return jnp.concatenate(pairs, axis=1)


def _swa_stacked_params(sink_ref, bias_ref, heads):
    bias = jnp.concatenate([bias_ref[h] for h in heads], axis=0)
    sink = jnp.concatenate([jnp.full((BLOCK, 1), sink_ref[0, h], F32) for h in heads], axis=0)
    return bias, sink


def _swa_scores(qs, kp, kc, bias, sink, first):
    sp = _dot_nt(qs, kp) * SCALE64 + bias[:, :BLOCK]
    sp = jnp.where(first, NEG, sp)
    sc = _dot_nt(qs, kc) * SCALE64 + bias[:, BLOCK:]
    m = jnp.maximum(jnp.maximum(jnp.max(sp, axis=1, keepdims=True), jnp.max(sc, axis=1, keepdims=True)), sink)
    pp = jnp.exp(sp - m)
    pc = jnp.exp(sc - m)
    ps = jnp.exp(sink - m)
    den = jnp.sum(pp, axis=1, keepdims=True) + jnp.sum(pc, axis=1, keepdims=True) + ps
    return pp / den, pc / den, ps / den


def _swa_fwd(proj, sinks, bias_tab, B, S, hosted=None):
    nb = S // BLOCK
    T = B * S

    def body(sink_ref, q_ref, kp_ref, kc_ref, vp_ref, vc_ref, bias_ref, o_ref):
        i = pl.program_id(1)
        first = jnp.full((SWA_GROUP * BLOCK, BLOCK), i, jnp.int32) == 0
        q = q_ref[...].astype(F32)
        blocks = []
        for g in range(SWA_Q_HEADS // SWA_GROUP):
            heads = range(SWA_GROUP * g, SWA_GROUP * (g + 1))
            bias, sink = _swa_stacked_params(sink_ref, bias_ref, heads)
            wp, wc, _ = _swa_scores(_swa_stack(q, heads), kp_ref[...], kc_ref[...], bias, sink, first)
            o = _dot(wp.astype(BF16), vp_ref[...]) + _dot(wc.astype(BF16), vc_ref[...])
            blocks += [o[t * BLOCK:(t + 1) * BLOCK] for t in range(SWA_GROUP)]
        o_ref[...] = _swa_unstack(blocks).astype(o_ref.dtype)

    blk = (BLOCK, BLOCK)
    wide = (BLOCK, 4 * BLOCK)
    outs, moved = _hosted_call(
        body, (B, nb),
        [pl.BlockSpec(memory_space=pltpu.SMEM),
         pl.BlockSpec(wide, lambda b, i: (b * nb + i, COL_QA // 4)),
         pl.BlockSpec(blk, lambda b, i: (b * nb + jnp.maximum(i - 1, 0), COL_KA)),
         pl.BlockSpec(blk, lambda b, i: (b * nb + i, COL_KA)),
         pl.BlockSpec(blk, lambda b, i: (b * nb + jnp.maximum(i - 1, 0), COL_VA)),
         pl.BlockSpec(blk, lambda b, i: (b * nb + i, COL_VA)),
         pl.BlockSpec((SWA_Q_HEADS, BLOCK, 2 * BLOCK), lambda b, i: (0, 0, 0))],
        [pl.BlockSpec(wide, lambda b, i: (b * nb + i, 0))],
        [jax.ShapeDtypeStruct((T, 512), BF16)], [], hosted, "swa_fwd",
        (sinks, proj, proj, proj, proj, proj, bias_tab))
    return outs[0], moved


def _swa_bwd(proj, dy, sinks, bias_tab, B, S, hosted=None):
    nb = S // BLOCK
    T = B * S
    H = SWA_Q_HEADS

    def body(sink_ref, q_ref, kp_ref, kc_ref, vp_ref, vc_ref, do_ref, bias_ref,
             dq_ref, dk_ref, dv_ref, dbias_ref, dsink_ref):
        b = pl.program_id(0)
        i = pl.program_id(1)

        @pl.when((b == 0) & (i == 0))
        def _():
            dbias_ref[...] = jnp.zeros_like(dbias_ref)
            dsink_ref[...] = jnp.zeros_like(dsink_ref)

        @pl.when(i == 0)
        def _():
            dk_ref[...] = jnp.zeros_like(dk_ref)
            dv_ref[...] = jnp.zeros_like(dv_ref)

        first = jnp.full((SWA_ROWS, BLOCK), i, jnp.int32) == 0
        heads = range(H)
        bias, sink = _swa_stacked_params(sink_ref, bias_ref, heads)
        kp, kc, vp, vc = kp_ref[...], kc_ref[...], vp_ref[...], vc_ref[...]
        qs = _swa_stack(q_ref[...].astype(F32), heads)
        dos = _swa_stack(do_ref[...].astype(F32), heads)
        wp, wc, ws = _swa_scores(qs, kp, kc, bias, sink, first)
        dwp = _dot_nt(dos, vp)
        dwc = _dot_nt(dos, vc)
        dsum = jnp.sum(wp * dwp, axis=1, keepdims=True) + jnp.sum(wc * dwc, axis=1, keepdims=True)
        dsp = wp * (dwp - dsum)
        dsc = wc * (dwc - dsum)
        dsk = -ws * dsum
        dsinks = []
        for h in range(H):
            rows = slice(h * BLOCK, (h + 1) * BLOCK)
            dsinks.append(jnp.broadcast_to(jnp.sum(dsk[rows], axis=0, keepdims=True), (1, 128)))
            dbias_ref[h] += jnp.concatenate([dsp[rows], dsc[rows]], axis=1)
        dsink_ref[...] += jnp.concatenate(dsinks, axis=0)
        dspb = dsp.astype(BF16)
        dscb = dsc.astype(BF16)
        dq = _dot(dspb, kp) + _dot(dscb, kc)
        dq_ref[...] = (_swa_unstack([dq[h * BLOCK:(h + 1) * BLOCK] for h in heads]) * SCALE64).astype(dq_ref.dtype)
        cur = pl.ds(pl.multiple_of(i * BLOCK, BLOCK), BLOCK)
        prev = pl.ds(pl.multiple_of(jnp.maximum(i - 1, 0) * BLOCK, BLOCK), BLOCK)
        dk_ref[prev, :] += _dot_tn(dspb, qs) * SCALE64
        dk_ref[cur, :] += _dot_tn(dscb, qs) * SCALE64
        dv_ref[prev, :] += _dot_tn(wp.astype(BF16), dos)
        dv_ref[cur, :] += _dot_tn(wc.astype(BF16), dos)

    blk = (BLOCK, BLOCK)
    wide = (BLOCK, 4 * BLOCK)
    kv_out = pl.BlockSpec((S, BLOCK), lambda b, i: (b, 0))
    full_bias = pl.BlockSpec((H, BLOCK, 2 * BLOCK), lambda b, i: (0, 0, 0))
    outs, moved = _hosted_call(
        body, (B, nb),
        [pl.BlockSpec(memory_space=pltpu.SMEM),
         pl.BlockSpec(wide, lambda b, i: (b * nb + i, COL_QA // 4)),
         pl.BlockSpec(blk, lambda b, i: (b * nb + jnp.maximum(i - 1, 0), COL_KA)),
         pl.BlockSpec(blk, lambda b, i: (b * nb + i, COL_KA)),
         pl.BlockSpec(blk, lambda b, i: (b * nb + jnp.maximum(i - 1, 0), COL_VA)),
         pl.BlockSpec(blk, lambda b, i: (b * nb + i, COL_VA)),
         pl.BlockSpec(wide, lambda b, i: (b * nb + i, 0)),
         full_bias],
        [pl.BlockSpec(wide, lambda b, i: (b * nb + i, 0)),
         kv_out, kv_out, full_bias,
         pl.BlockSpec((H, 128), lambda b, i: (0, 0))],
        [jax.ShapeDtypeStruct((T, 512), BF16),
         jax.ShapeDtypeStruct((T, BLOCK), F32), jax.ShapeDtypeStruct((T, BLOCK), F32),
         jax.ShapeDtypeStruct((H, BLOCK, 2 * BLOCK), F32), jax.ShapeDtypeStruct((H, 128), F32)],
        [], hosted, "swa_bwd", (sinks, proj, proj, proj, proj, proj, dy, bias_tab))
    return (*outs, moved)


SB_TILE = 256


SB_HEADS = 4
SB_LANES = SB_HEADS * HEAD_DIM
SB_ROWS = SB_HEADS * SB_TILE


def _sb_logits(z, tri):
    sp = jnp.log(1.0 + jnp.exp(-jnp.abs(z)))
    ls = jnp.minimum(z, 0.0) - sp
    l1m = ls - z
    if tri is not None:
        l1m = jnp.where(tri, l1m, 0.0)
    return ls, l1m


def _sb_masks():
    lane = lax.broadcasted_iota(jnp.int32, (SB_TILE, SB_LANES), 1)
    hm = [(lane // HEAD_DIM) == h for h in range(SB_HEADS)]
    row = lax.broadcasted_iota(jnp.int32, (SB_ROWS, SB_TILE), 0) % SB_TILE
    col = lax.broadcasted_iota(jnp.int32, (SB_ROWS, SB_TILE), 1)
    return lane, hm, row, col


def _sb_stack(x, hm):
    return jnp.concatenate([jnp.where(m, x, 0) for m in hm], axis=0)


def _sb_unstack(x, hm):
    return sum(jnp.where(m, x[h * SB_TILE:(h + 1) * SB_TILE], 0.0) for h, m in enumerate(hm))


def _sb_fwd(proj, B, S, hosted=None):
    nt = S // SB_TILE
    T = B * S

    ng = 512 // SB_LANES

    def body(*refs):
        q_refs, k_refs, v_refs = refs[:ng], refs[ng:2 * ng], refs[2 * ng:3 * ng]
        o_ref, r_ref = refs[3 * ng:]
        i = pl.program_id(1)
        _, hm, row, col = _sb_masks()
        tri = col < row
        later = jnp.concatenate([(row[:SB_TILE] > col[:SB_TILE]).astype(BF16)] * 2, axis=0)
        qs = [_sb_stack(q_ref[...] * SCALE64, hm) for q_ref in q_refs]

        def tile(j, carry, mask):
            rows = pl.ds(pl.multiple_of(j * SB_TILE, SB_TILE), SB_TILE)
            out = []
            for g in range(ng):
                acc, c = carry[g]
                ls, l1m = _sb_logits(_dot_nt(qs[g], k_refs[g][rows, :]), mask)
                a = jnp.exp(ls + c + _dot_split(l1m, later))
                if mask is not None:
                    a = jnp.where(mask, a, 0.0)
                pv = _dot(a.astype(BF16), v_refs[g][rows, :])
                out.append((acc + _sb_unstack(pv, hm), c + jnp.sum(l1m, axis=1, keepdims=True)))
            return tuple(out)

        zero = (jnp.zeros((SB_TILE, SB_LANES), F32), jnp.zeros((SB_ROWS, 1), F32))
        carry = tile(i, (zero,) * ng, tri)
        carry = lax.fori_loop(0, i, lambda it, cr: tile(i - 1 - it, cr, None), carry)
        o_ref[...] = jnp.concatenate([acc for acc, _ in carry], axis=1).astype(o_ref.dtype)
        r_ref[...] = jnp.concatenate(
            [_sb_unstack(jnp.broadcast_to(c, (SB_ROWS, SB_LANES)), hm) for _, c in carry], axis=1)

    blk = (SB_TILE, SB_LANES)
    cq, ck, cv = (c * BLOCK // SB_LANES for c in (COL_QB, COL_KB, COL_VB))
    wide = pl.BlockSpec((SB_TILE, 512), lambda b, i: (b * nt + i, 0))
    outs, moved = _hosted_call(
        body, (B, nt),
        [pl.BlockSpec(blk, lambda b, i, g=g: (b * nt + i, cq + g)) for g in range(ng)]
        + [pl.BlockSpec((S, SB_LANES), lambda b, i, g=g: (b, ck + g)) for g in range(ng)]
        + [pl.BlockSpec((S, SB_LANES), lambda b, i, g=g: (b, cv + g)) for g in range(ng)],
        [wide, wide],
        [jax.ShapeDtypeStruct((T, 512), BF16), jax.ShapeDtypeStruct((T, 512), F32)], [], hosted, "sb_fwd",
        (proj,) * (3 * ng))
    return outs[0], outs[1], moved


def _sb_bwd(proj, dy, rtot, B, S, hosted=None):
    nt = S // SB_TILE
    T = B * S

    ng = 512 // SB_LANES

    def body(*refs):
        q_refs, k_refs, v_refs = refs[:ng], refs[ng:2 * ng], refs[2 * ng:3 * ng]
        do_ref, r_ref, dq_ref, dk_ref, dv_ref = refs[3 * ng:]
        i = pl.program_id(1)

        @pl.when(i == 0)
        def _():
            dk_ref[...] = jnp.zeros_like(dk_ref)
            dv_ref[...] = jnp.zeros_like(dv_ref)

        lane, hm, row, col = _sb_masks()
        tri = col < row
        later = jnp.concatenate([(row[:SB_TILE] > col[:SB_TILE]).astype(BF16)] * 2, axis=0)
        earlier = (row[:SB_TILE] < col[:SB_TILE]).astype(BF16)
        qs, dos, rs = [], [], []
        for g in range(ng):
            lanes = slice(g * SB_LANES, (g + 1) * SB_LANES)
            qs.append(_sb_stack(q_refs[g][...] * SCALE64, hm))
            dos.append(_sb_stack(do_ref[:, lanes], hm))
            r = r_ref[:, lanes]
            rs.append(jnp.concatenate([jnp.sum(jnp.where(lane == h * HEAD_DIM, r, 0.0), axis=1, keepdims=True)
                                       for h in range(SB_HEADS)], axis=0))

        def tile(j, carry, mask):
            rows = pl.ds(pl.multiple_of(j * SB_TILE, SB_TILE), SB_TILE)
            out = []
            for g in range(ng):
                lanes = slice(g * SB_LANES, (g + 1) * SB_LANES)
                dq, lsum, psum = carry[g]
                kj = k_refs[g][rows, :]
                vj = v_refs[g][rows, :]
                ls, l1m = _sb_logits(_dot_nt(qs[g], kj), mask)
                sig = jnp.exp(ls)
                lsum = lsum + jnp.sum(l1m, axis=1, keepdims=True)
                a = jnp.exp(ls + (rs[g] - lsum) + _dot_split(l1m, later))
                if mask is not None:
                    a = jnp.where(mask, a, 0.0)
                de = _dot_nt(dos[g], vj) * a
                pre = psum + _dot(de.astype(BF16), earlier)
                dz = de - sig * (de + pre)
                if mask is not None:
                    dz = jnp.where(mask, dz, 0.0)
                dz = dz.astype(BF16)
                dk_ref[rows, lanes] += _dot_tn(dz, qs[g])
                dv_ref[rows, lanes] += _dot_tn(a.astype(BF16), dos[g])
                out.append((dq + _sb_unstack(_dot(dz, kj), hm), lsum, psum + jnp.sum(de, axis=1, keepdims=True)))
            return tuple(out)

        zero = jnp.zeros((SB_ROWS, 1), F32)
        init = ((jnp.zeros((SB_TILE, SB_LANES), F32), zero, zero),) * ng
        carry = lax.fori_loop(0, i, lambda j, c: tile(j, c, None), init)
        carry = tile(i, carry, tri)
        dq_ref[...] = (jnp.concatenate([c[0] for c in carry], axis=1) * SCALE64).astype(dq_ref.dtype)

    blk = (SB_TILE, SB_LANES)
    cq, ck, cv = (c * BLOCK // SB_LANES for c in (COL_QB, COL_KB, COL_VB))
    wide = pl.BlockSpec((SB_TILE, 512), lambda b, i: (b * nt + i, 0))
    kv_out = pl.BlockSpec((S, 512), lambda b, i: (b, 0))
    outs, moved = _hosted_call(
        body, (B, nt),
        [pl.BlockSpec(blk, lambda b, i, g=g: (b * nt + i, cq + g)) for g in range(ng)]
        + [pl.BlockSpec((S, SB_LANES), lambda b, i, g=g: (b, ck + g)) for g in range(ng)]
        + [pl.BlockSpec((S, SB_LANES), lambda b, i, g=g: (b, cv + g)) for g in range(ng)]
        + [wide, wide],
        [wide, kv_out, kv_out],
        [jax.ShapeDtypeStruct((T, 512), BF16),
         jax.ShapeDtypeStruct((T, 512), F32), jax.ShapeDtypeStruct((T, 512), F32)], [], hosted, "sb_bwd",
        (proj,) * (3 * ng) + (dy, rtot))
    return outs[0], outs[1], outs[2], moved


def _mem_weights(q, mk):
    z = _dot_nt(q, mk) * SCALE128
    e = jnp.exp(z - jnp.max(z, axis=1, keepdims=True))
    return e / jnp.sum(e, axis=1, keepdims=True)


def _mem_fwd(proj, mkv, B, S, M):
    tq = 512
    nq = S // tq
    T = B * S
    Hm = MEM_HEADS

    def body(q0, q1, q2, q3, mk_ref, mv_ref, o_ref):
        outs = []
        for h, q_ref in enumerate((q0, q1, q2, q3)):
            cols = slice(h * 128, (h + 1) * 128)
            w = _mem_weights(q_ref[...], mk_ref[:, cols])
            outs.append(_dot(w.astype(BF16), mv_ref[:, cols]))
        o_ref[...] = jnp.concatenate(outs, axis=1).astype(o_ref.dtype)

    return pl.pallas_call(
        body,
        grid=(B, nq),
        in_specs=[pl.BlockSpec((tq, 128), lambda b, i, h=h: (b * nq + i, COL_QM + h)) for h in range(Hm)]
        + [pl.BlockSpec((M, 512), lambda b, i: (b, 0)), pl.BlockSpec((M, 512), lambda b, i: (b, 1))],
        out_specs=pl.BlockSpec((tq, 512), lambda b, i: (b * nq + i, 0)),
        out_shape=jax.ShapeDtypeStruct((T, 512), BF16),
        name="mem_fwd",
    )(proj, proj, proj, proj, mkv, mkv)


def _mem_bwd(proj, mkv, dy, B, S, M):
    tq = 512
    nq = S // tq
    T = B * S
    Hm = MEM_HEADS

    def body(q0, q1, q2, q3, mk_ref, mv_ref, do_ref, dq_ref, dmk_ref, dmv_ref):
        i = pl.program_id(1)
        dqs, dmks, dmvs = [], [], []
        for h, q_ref in enumerate((q0, q1, q2, q3)):
            cols = slice(h * 128, (h + 1) * 128)
            q = q_ref[...]
            do = do_ref[:, cols]
            mk = mk_ref[:, cols]
            w = _mem_weights(q, mk)
            dw = _dot_nt(do, mv_ref[:, cols])
            ds = (w * (dw - jnp.sum(w * dw, axis=1, keepdims=True))).astype(BF16)
            dqs.append(_dot(ds, mk))
            dmks.append(_dot_tn(ds, q))
            dmvs.append(_dot_tn(w.astype(BF16), do))
        dq_ref[...] = (jnp.concatenate(dqs, axis=1) * SCALE128).astype(dq_ref.dtype)

        @pl.when(i == 0)
        def _():
            dmk_ref[...] = jnp.zeros_like(dmk_ref)
            dmv_ref[...] = jnp.zeros_like(dmv_ref)

        dmk_ref[...] += jnp.concatenate(dmks, axis=1) * SCALE128
        dmv_ref[...] += jnp.concatenate(dmvs, axis=1)

    q_spec = pl.BlockSpec((tq, 512), lambda b, i: (b * nq + i, 0))
    m_out = pl.BlockSpec((M, 512), lambda b, i: (b, 0))
    return pl.pallas_call(
        body,
        grid=(B, nq),
        in_specs=[pl.BlockSpec((tq, 128), lambda b, i, h=h: (b * nq + i, COL_QM + h)) for h in range(Hm)]
        + [pl.BlockSpec((M, 512), lambda b, i: (b, 0)), pl.BlockSpec((M, 512), lambda b, i: (b, 1)), q_spec],
        out_specs=[q_spec, m_out, m_out],
        out_shape=[jax.ShapeDtypeStruct((T, 512), BF16),
                   jax.ShapeDtypeStruct((B * M, 512), F32), jax.ShapeDtypeStruct((B * M, 512), F32)],
        name="mem_bwd",
    )(proj, proj, proj, proj, mkv, mkv, dy)


def _all_gather(blk, name):
    R, C = blk.shape

    def body(x_ref, out_ref, send_sems, recv_sems, local_sem):
        x, y, c = _mesh_pos()
        me, sibling = (x, y, c), (x, y, 1 - c)
        chips = [(1 - x, y), (x, 1 - y), (1 - x, 1 - y)]

        def slot(px, py, pc):
            return out_ref.at[4 * px + 2 * py + pc]

        def copy(k, block, to, src=None):
            return pltpu.make_async_remote_copy(
                src_ref=slot(*block) if src is None else src, dst_ref=slot(*block),
                send_sem=send_sems.at[k], recv_sem=recv_sems.at[k],
                device_id=to, device_id_type=pl.DeviceIdType.MESH)

        mine = pltpu.make_async_copy(x_ref, slot(*me), local_sem)
        mine.start()
        first = [copy(0, me, sibling, src=x_ref)]
        first += [copy(1 + j, me, (*chip, c), src=x_ref) for j, chip in enumerate(chips)]
        for cp in first:
            cp.start()
        passed = [copy(4 + j, (*chip, c), sibling) for j, chip in enumerate(chips)]
        for j, chip in enumerate(chips):
            copy(1 + j, (*chip, c), me).wait_recv()
            passed[j].start()
        copy(0, sibling, me).wait_recv()
        for j, chip in enumerate(chips):
            copy(4 + j, (*chip, 1 - c), me).wait_recv()
        for cp in first + passed:
            cp.wait_send()
        mine.wait()

    return pl.pallas_call(
        body,
        in_specs=[pl.BlockSpec(memory_space=pl.ANY)],
        out_specs=pl.BlockSpec(memory_space=pl.ANY),
        out_shape=jax.ShapeDtypeStruct((N_DEV, R, C), blk.dtype),
        scratch_shapes=[pltpu.SemaphoreType.DMA((7,)), pltpu.SemaphoreType.DMA((7,)), pltpu.SemaphoreType.DMA],
        name=name,
    )(blk)


def _all_gather_relayed(blk, name):
    R, C = blk.shape
    R2 = R // 2

    def body(x_ref, out_ref, send_sems, recv_sems, local_sem):
        x, y, c = _mesh_pos()
        me, sib = (x, y, c), (x, y, 1 - c)
        nx, ny, dg = (1 - x, y, c), (x, 1 - y, c), (1 - x, 1 - y, c)

        def slot(p, half=None):
            ref = out_ref.at[4 * p[0] + 2 * p[1] + p[2]]
            return ref if half is None else ref.at[pl.ds(half * R2, R2), :]

        def copy(k, block, to, half=None, own=False):
            src = slot(block, half)
            if own:
                src = x_ref if half is None else x_ref.at[pl.ds(half * R2, R2), :]
            return pltpu.make_async_remote_copy(
                src_ref=src, dst_ref=slot(block, half), send_sem=send_sems.at[k], recv_sem=recv_sems.at[k],
                device_id=to, device_id_type=pl.DeviceIdType.MESH)

        def other(p):
            return (p[0], p[1], 1 - c)

        mine = pltpu.make_async_copy(x_ref, slot(me), local_sem)
        mine.start()
        sends = [copy(1, me, nx, 0, own=True), copy(3, me, ny, 1, own=True), copy(0, me, sib, own=True),
                 copy(2, me, nx, 1, own=True), copy(4, me, ny, 0, own=True)]
        for cp in sends:
            cp.start()
        copy(1, nx, me, 0).wait_recv()
        sends.append(copy(5, nx, ny, 0))
        sends[-1].start()
        copy(3, ny, me, 1).wait_recv()
        sends.append(copy(6, ny, nx, 1))
        sends[-1].start()
        copy(2, nx, me, 1).wait_recv()
        sends.append(copy(7, nx, sib))
        sends[-1].start()
        copy(4, ny, me, 0).wait_recv()
        sends.append(copy(8, ny, sib))
        sends[-1].start()
        copy(5, dg, me, 0).wait_recv()
        copy(6, dg, me, 1).wait_recv()
        sends.append(copy(9, dg, sib))
        sends[-1].start()
        copy(0, sib, me).wait_recv()
        for k, p in ((7, nx), (8, ny), (9, dg)):
            copy(k, other(p), me).wait_recv()
        for cp in sends:
            cp.wait_send()
        mine.wait()

    return pl.pallas_call(
        body,
        in_specs=[pl.BlockSpec(memory_space=pl.ANY)],
        out_specs=pl.BlockSpec(memory_space=pl.ANY),
        out_shape=jax.ShapeDtypeStruct((N_DEV, R, C), blk.dtype),
        scratch_shapes=[pltpu.SemaphoreType.DMA((10,)), pltpu.SemaphoreType.DMA((10,)), pltpu.SemaphoreType.DMA],
        name=name,
    )(blk)


_HBM = pl.BlockSpec(memory_space=pltpu.HBM)
_SEM = pl.BlockSpec(memory_space=pltpu.SEMAPHORE)


def _scatter_start(parts, land, window, carried, name):
    hosted = _Hosted(scatters=[parts], window=window)

    def body(p_ref, land_ref, c_ref, send_sems, recv_sems, local_sems, p_thru, land_thru, c_thru):
        for cp in hosted.copies([p_ref], [land_ref], send_sems, recv_sems, local_sems):
            cp.start()

    sems = (pltpu.SemaphoreType.DMA((7,)), pltpu.SemaphoreType.DMA((7,)), pltpu.SemaphoreType.DMA((1,)))
    hbm = lambda a: pltpu.HBM(a.shape, a.dtype)
    outs = pl.pallas_call(
        body, name=name,
        out_shape=sems + (hbm(parts), hbm(land), hbm(carried)),
        in_specs=(_HBM, _HBM, _HBM), out_specs=(_SEM, _SEM, _SEM, _HBM, _HBM, _HBM),
        input_output_aliases={0: 3, 1: 4, 2: 5},
        compiler_params=pltpu.CompilerParams(has_side_effects=pltpu.SideEffectType.DATAFLOW_SIDE_EFFECTING),
    )(pltpu.with_memory_space_constraint(parts, pltpu.HBM),
      pltpu.with_memory_space_constraint(land, pltpu.HBM),
      pltpu.with_memory_space_constraint(carried, pltpu.HBM))
    return (outs[:4], window), outs[4], outs[5]


def _scatter_wait(flight, land, after, name):
    (send_sems, recv_sems, local_sems, p_thru), window = flight
    hosted = _Hosted(scatters=[p_thru], window=window)

    def body(p_ref, land_ref, send, recv, local, after_ref, p_dead, got_ref):
        for cp in hosted.copies([p_ref], [land_ref], send, recv, local):
            cp.wait()

    hbm = lambda a: pltpu.HBM(a.shape, a.dtype)
    return pl.pallas_call(
        body, name=name,
        out_shape=(hbm(p_thru), hbm(land)),
        in_specs=(_HBM, _HBM, _SEM, _SEM, _SEM, pl.BlockSpec(memory_space=pl.ANY)), out_specs=(_HBM, _HBM),
        input_output_aliases={0: 0, 1: 1},
        compiler_params=pltpu.CompilerParams(has_side_effects=pltpu.SideEffectType.DATAFLOW_SIDE_EFFECTING),
    )(p_thru, land, send_sems, recv_sems, local_sems, after)[1]


def _adamw(parts, w, m, v, name):
    R, C = w.shape
    tr = R
    for cand in (368, 352, 256, 176, 128, 64, 32, 16, 8):
        if R % cand == 0 and cand * C * 4 <= 1536 * 1024:
            tr = cand
            break
    c1 = 1.0 - ADAM_B1 ** ADAM_STEP
    c2 = 1.0 - ADAM_B2 ** ADAM_STEP

    def body(p_ref, w_ref, m_ref, v_ref, g_ref, d_ref, nm_ref, nv_ref):
        g = p_ref[0].astype(F32)
        for d in range(1, N_DEV):
            g = g + p_ref[d].astype(F32)
        nm = ADAM_B1 * m_ref[...] + (1.0 - ADAM_B1) * g
        nv = ADAM_B2 * v_ref[...] + (1.0 - ADAM_B2) * (g * g)
        g_ref[...] = g
        nm_ref[...] = nm
        nv_ref[...] = nv
        d_ref[...] = -ADAM_LR * ((nm / c1) / (jnp.sqrt(nv / c2) + ADAM_EPS) + ADAM_WD * w_ref[...])

    row = pl.BlockSpec((tr, C), lambda i: (i, 0))
    out = jax.ShapeDtypeStruct((R, C), F32)
    return pl.pallas_call(
        body,
        grid=(R // tr,),
        in_specs=[pl.BlockSpec((N_DEV, tr, C), lambda i: (0, i, 0)), row, row, row],
        out_specs=[row] * 4,
        out_shape=[out] * 4,
        compiler_params=_params(),
        name=name,
    )(parts, w, m, v)


def _col_shards(g):
    R, C8 = g.shape
    return g.reshape(R, N_DEV, C8 // N_DEV).transpose(1, 0, 2)


def _row_shards(g):
    R8, C = g.shape
    return g.reshape(N_DEV, R8 // N_DEV, C)


def _cols_full(gathered):
    n, R, C = gathered.shape
    return gathered.transpose(1, 0, 2).reshape(R, n * C)


_BIG = ("w_in", "w_mem_kv", "w_branch_swa", "w_branch_sb", "w_branch_mem", "w_out", "w_gate", "w_up", "w_down")
_COL_SHARDED = ("w_branch_swa", "w_branch_sb", "w_branch_mem")
_TRANSPOSED = ("w_in", "w_gate", "w_up")
_SMALL = ("ln_mix_pre", "ln_mix_post", "swa_sinks", "rel_bias", "ln_mem", "ln_ffn_pre", "ln_ffn_post")
_ORDER = ("ln_mix_pre", "ln_mix_post", "w_in", "swa_sinks", "rel_bias", "ln_mem", "w_mem_kv", "w_branch_swa",
          "w_branch_sb", "w_branch_mem", "w_out", "ln_ffn_pre", "ln_ffn_post", "w_gate", "w_up", "w_down")


def _pack_small(d, last=None):
    rows = [d["ln_mix_pre"], d["ln_mix_post"], d["ln_mem"], d["ln_ffn_pre"], d["ln_ffn_post"],
            jnp.pad(d["swa_sinks"].reshape(1, -1), ((0, 0), (0, D_MODEL - SWA_Q_HEADS))),
            jnp.pad(d["rel_bias"].reshape(1, -1), ((0, 0), (0, D_MODEL - N_BUCKETS * SWA_Q_HEADS))),
            jnp.zeros((1, D_MODEL), F32) if last is None else last]
    return jnp.concatenate([r.astype(F32) for r in rows], axis=0)


def _unpack_small(a):
    return dict(ln_mix_pre=a[0:1], ln_mix_post=a[1:2], ln_mem=a[2:3], ln_ffn_pre=a[3:4], ln_ffn_post=a[4:5],
                swa_sinks=a[5:6, :SWA_Q_HEADS],
                rel_bias=a[6, :N_BUCKETS * SWA_Q_HEADS].reshape(N_BUCKETS, SWA_Q_HEADS))


def kernel(x, mem, ln_mix_pre, ln_mix_post, w_in, swa_sinks, rel_bias, ln_mem, w_mem_kv, w_branch_swa, w_branch_sb, w_branch_mem, w_out, ln_ffn_pre, ln_ffn_post, w_gate, w_up, w_down, loss_target, m_ln_mix_pre, m_ln_mix_post, m_w_in, m_swa_sinks, m_rel_bias, m_ln_mem, m_w_mem_kv, m_w_branch_swa, m_w_branch_sb, m_w_branch_mem, m_w_out, m_ln_ffn_pre, m_ln_ffn_post, m_w_gate, m_w_up, m_w_down, v_ln_mix_pre, v_ln_mix_post, v_w_in, v_swa_sinks, v_rel_bias, v_ln_mem, v_w_mem_kv, v_w_branch_swa, v_w_branch_sb, v_w_branch_mem, v_w_out, v_ln_ffn_pre, v_ln_ffn_post, v_w_gate, v_w_up, v_w_down):
    w = dict(ln_mix_pre=ln_mix_pre, ln_mix_post=ln_mix_post, w_in=w_in[0], swa_sinks=swa_sinks, rel_bias=rel_bias,
             ln_mem=ln_mem, w_mem_kv=w_mem_kv[0], w_branch_swa=w_branch_swa[0], w_branch_sb=w_branch_sb[0],
             w_branch_mem=w_branch_mem[0], w_out=w_out[0], ln_ffn_pre=ln_ffn_pre, ln_ffn_post=ln_ffn_post,
             w_gate=w_gate[0], w_up=w_up[0], w_down=w_down[0])
    mom = dict(ln_mix_pre=m_ln_mix_pre, ln_mix_post=m_ln_mix_post, w_in=m_w_in[0], swa_sinks=m_swa_sinks,
               rel_bias=m_rel_bias, ln_mem=m_ln_mem, w_mem_kv=m_w_mem_kv[0], w_branch_swa=m_w_branch_swa[0],
               w_branch_sb=m_w_branch_sb[0], w_branch_mem=m_w_branch_mem[0], w_out=m_w_out[0],
               ln_ffn_pre=m_ln_ffn_pre, ln_ffn_post=m_ln_ffn_post, w_gate=m_w_gate[0], w_up=m_w_up[0],
               w_down=m_w_down[0])
    var = dict(ln_mix_pre=v_ln_mix_pre, ln_mix_post=v_ln_mix_post, w_in=v_w_in[0], swa_sinks=v_swa_sinks,
               rel_bias=v_rel_bias, ln_mem=v_ln_mem, w_mem_kv=v_w_mem_kv[0], w_branch_swa=v_w_branch_swa[0],
               w_branch_sb=v_w_branch_sb[0], w_branch_mem=v_w_branch_mem[0], w_out=v_w_out[0],
               ln_ffn_pre=v_ln_ffn_pre, ln_ffn_post=v_ln_ffn_post, w_gate=v_w_gate[0], w_up=v_w_up[0],
               w_down=v_w_down[0])
    B, S, D = x.shape
    M = mem.shape[1]
    T = B * S
    F = D_FF
    x2 = x.reshape(T, D)
    mem2 = mem.reshape(B * M, D)
    t2 = loss_target.reshape(T, D)
    buckets = jnp.asarray(_swa_buckets())
    for d in (w, mom, var):
        for n in _TRANSPOSED:
            d[n] = d[n].T
    wb = {n: w[n].astype(BF16) for n in _BIG}
    full = {}

    def landed(names, got):
        for n, g in zip(names, got):
            full[n] = _cols_full(g) if n in _COL_SHARDED else g.reshape(-1, g.shape[-1])

    def shards(n, g):
        return _col_shards(g) if n in _COL_SHARDED else _row_shards(g)

    landed(["w_in"], [_all_gather_relayed(wb["w_in"], "ag_w_in")])
    u = _rms_fwd(x2, ln_mix_pre, "rms_mix_pre")
    early = ["w_mem_kv", "w_branch_swa", "w_branch_sb", "w_branch_mem"]
    proj, got = _matmul([(u, full["w_in"])], "nt", BF16, 512, IN_W // 2, D, "proj_in",
                        hosted=_Hosted(gathers=[wb[n] for n in early]))
    landed(early, got)
    mn = _rms_fwd(mem2, ln_mem, "rms_mem")
    mkv = _matmul([(mn, full["w_mem_kv"])], "nn", BF16, 512, 1024, D, "proj_mem")
    bias_tab = _swa_bias_table(rel_bias, buckets)
    y_swa, got = _swa_fwd(proj, swa_sinks, bias_tab, B, S, hosted=_Hosted(gathers=[wb["w_out"]]))
    landed(["w_out"], got)
    late = ["w_gate", "w_up"]
    y_sb, rtot, got = _sb_fwd(proj, B, S, hosted=_Hosted(gathers=[wb[n] for n in late]))
    landed(late, got)
    y_mem = _mem_fwd(proj, mkv, B, S, M)
    wbs = (full["w_branch_swa"], full["w_branch_sb"], full["w_branch_mem"])
    merged, p_swa, p_sb, p_mem = _branch_gate(proj, (y_swa, y_sb, y_mem), wbs)
    mix, h1, u2 = _post_pre(x2, merged, full["w_out"], ln_mix_post, ln_ffn_pre)
    a, zg, zu, got = _ffn_up(u2, full["w_gate"], full["w_up"], hosted=_Hosted(gathers=[wb["w_down"]]))
    landed(["w_down"], got)
    dffn, dh2, loss_tile, d_ln_ffn_post = _loss_head(a, full["w_down"], h1, t2, ln_ffn_post)

    part = {}
    part["w_down"] = _matmul([(a, dffn)], "tn", BF16, F // 2, 1024, 1024, "dw_down")
    dzg, dzu = _ffn_down_bwd(dffn, full["w_down"], zg, zu)
    part["w_gate"] = _matmul([(dzg, u2)], "tn", BF16, F // 2, 1024, 1024, "dw_gate")
    part["w_up"] = _matmul([(dzu, u2)], "tn", BF16, F // 2, 1024, 1024, "dw_up")
    dh1, dmix, d_ln_ffn_pre, d_ln_mix_post = _mid_bwd(dzg, dzu, full["w_gate"], full["w_up"], h1, dh2, mix,
                                                      ln_ffn_pre, ln_mix_post)
    part["w_out"] = _matmul([(merged, dmix)], "tn", BF16, 1024, 1024, 1024, "dw_out")
    (dp_swa, dp_sb, dp_mem, dg0, dg1, dg2, dy_swa, dy_sb, dy_mem) = _gate_bwd(
        dmix, full["w_out"], (p_swa, p_sb, p_mem), proj, wbs)
    part["w_branch_swa"] = _matmul([(y_swa, dp_swa)], "tn", BF16, 512, 1024, 1024, "dw_branch_swa")
    part["w_branch_sb"] = _matmul([(y_sb, dp_sb)], "tn", BF16, 512, 1024, 1024, "dw_branch_sb")
    part["w_branch_mem"] = _matmul([(y_mem, dp_mem)], "tn", BF16, 512, 1024, 1024, "dw_branch_mem")
    dqm, dmk, dmv = _mem_bwd(proj, mkv, dy_mem, B, S, M)
    dmkv = jnp.concatenate([dmk, dmv], axis=1).astype(BF16)
    part["w_mem_kv"] = _matmul([(mn, dmkv)], "tn", BF16, 1024, 1024, 512, "dw_mem_kv")
    dmn = _matmul([(dmkv, full["w_mem_kv"])], "nt", F32, 512, 1024, 1024, "d_mn")
    d_ln_mem = _gain_grad(mem2, dmn)
    behind_swa = ["w_out", "w_branch_swa", "w_branch_sb"]
    dqa, dka, dva, dbias, dsink, got = _swa_bwd(
        proj, dy_swa, swa_sinks, bias_tab, B, S, hosted=_Hosted(scatters=[shards(n, part[n]) for n in behind_swa]))
    recv = dict(zip(behind_swa, got))
    behind_sb = ["w_down", "w_gate", "w_up", "w_branch_mem", "w_mem_kv"]
    dqb, dkb, dvb, got = _sb_bwd(proj, dy_sb, rtot, B, S,
                                 hosted=_Hosted(scatters=[shards(n, part[n]) for n in behind_sb]))
    recv.update(zip(behind_sb, got))
    d_rel_bias = _swa_bias_grad(dbias, buckets)[:, :SWA_Q_HEADS]
    d_sinks = dsink[:, 0].reshape(1, SWA_Q_HEADS)
    dproj = jnp.concatenate([dqa, dka.astype(BF16), dva.astype(BF16), dqb, dkb.astype(BF16), dvb.astype(BF16),
                             dqm, dg0, dg1, dg2], axis=1)
    half = D // 2
    land = lax.empty((N_DEV, IN_W // N_DEV, D), BF16)
    flights = []
    for t in range(2):
        dw_half = _matmul([(dproj, u)], "tn", BF16, IN_W // 2, half, 1024, "dw_in_%d" % t, n_cols=half, n_off=t)
        flight, land, dproj = _scatter_start(_row_shards(dw_half), land, (t * half, half), dproj,
                                             "rs_w_in_start_%d" % t)
        flights.append(flight)
    grad_x, d_ln_mix_pre = _pre_bwd(dproj, full["w_in"], x2, dh1, ln_mix_pre)

    out = {n: _adamw(recv[n], w[n], mom[n], var[n], "adamw_" + n) for n in _BIG if n != "w_in"}
    small_grads = dict(ln_mix_pre=d_ln_mix_pre, ln_mix_post=d_ln_mix_post, swa_sinks=d_sinks, rel_bias=d_rel_bias,
                       ln_mem=d_ln_mem, ln_ffn_pre=d_ln_ffn_pre, ln_ffn_post=d_ln_ffn_post)
    small_parts = _all_gather(_pack_small(small_grads, jnp.tile(loss_tile[0:1], (1, D // 128))), "ag_small")
    res = _adamw(small_parts, _pack_small(w), _pack_small(mom), _pack_small(var), "adamw_small")
    loss = res[0][7, 0]
    small = [_unpack_small(r) for r in res]
    for n in _SMALL:
        out[n] = tuple(s[n] for s in small)
    for t, flight in enumerate(flights):
        land = _scatter_wait(flight, land, res[0], "rs_w_in_wait_%d" % t)
    out["w_in"] = _adamw(land, w["w_in"], mom["w_in"], var["w_in"], "adamw_w_in")
    for n in _TRANSPOSED:
        out[n] = tuple(o.T for o in out[n])

    like = dict(ln_mix_pre=ln_mix_pre, ln_mix_post=ln_mix_post, w_in=w_in, swa_sinks=swa_sinks, rel_bias=rel_bias,
                ln_mem=ln_mem, w_mem_kv=w_mem_kv, w_branch_swa=w_branch_swa, w_branch_sb=w_branch_sb,
                w_branch_mem=w_branch_mem, w_out=w_out, ln_ffn_pre=ln_ffn_pre, ln_ffn_post=ln_ffn_post,
                w_gate=w_gate, w_up=w_up, w_down=w_down)
    result = [loss, grad_x.reshape(B, S, D)]
    for k in range(4):
        result += [out[n][k].reshape(like[n].shape) for n in _ORDER]
    return tuple(result)
```

```python
import functools
import math

import numpy as np
import jax
import jax.numpy as jnp
from jax import lax
from jax.experimental import pallas as pl
from jax.experimental.pallas import tpu as pltpu

F32 = jnp.float32
BF16 = jnp.bfloat16

N_DEV = 8
D_MODEL = 1024
BLOCK = 128
EPS = 1e-6
HEAD_DIM = 64
SWA_Q_HEADS = 8
SWA_WINDOW = 128
N_BUCKETS = 32
MAX_DISTANCE = 128
MEM_HEADS = 4
MEM_HEAD_DIM = 128
D_FF = 2816
IN_W = 5888
COL_QA, COL_KA, COL_VA, COL_QB, COL_KB, COL_VB, COL_QM, COL_GL = 0, 4, 5, 6, 10, 14, 18, 22
SCALE64 = HEAD_DIM ** -0.5
SCALE128 = MEM_HEAD_DIM ** -0.5
NEG = -1e30

ADAM_LR = 0.001
ADAM_B1 = 0.9
ADAM_B2 = 0.999
ADAM_EPS = 1e-08
ADAM_WD = 0.01
ADAM_STEP = 10

VMEM_LIMIT_BYTES = 56 * 1024 * 1024


def _params(**kw):
    return pltpu.CompilerParams(vmem_limit_bytes=VMEM_LIMIT_BYTES, **kw)


def _dot(a, b):
    return jnp.dot(a, b, preferred_element_type=F32)


def _dot_nt(a, b):
    return lax.dot_general(a, b, (((1,), (1,)), ((), ())), preferred_element_type=F32)


def _dot_tn(a, b):
    return lax.dot_general(a, b, (((0,), (0,)), ((), ())), preferred_element_type=F32)


def _dot_split(x, m2):
    hi = x.astype(BF16)
    lo = (x - hi.astype(F32)).astype(BF16)
    return _dot(jnp.concatenate([hi, lo], axis=1), m2)


def _mesh_pos():
    return lax.axis_index("x"), lax.axis_index("y"), lax.axis_index("c")


class _Hosted:
    def __init__(self, gathers=(), scatters=(), window=None):
        self.items = [("g", a) for a in gathers] + [("s", a) for a in scatters]
        self.n = len(self.items)
        self.window = window

    def operands(self):
        return [a for _, a in self.items]

    def specs(self):
        return [pl.BlockSpec(memory_space=pl.ANY)] * self.n

    def out_shapes(self):
        return [jax.ShapeDtypeStruct(((N_DEV,) + a.shape) if kind == "g" else a.shape, a.dtype)
                for kind, a in self.items]

    def scratch(self):
        return [pltpu.SemaphoreType.DMA((7 * self.n,)), pltpu.SemaphoreType.DMA((7 * self.n,)),
                pltpu.SemaphoreType.DMA((self.n,))]

    def copies(self, in_refs, out_refs, send_sems, recv_sems, local_sems):
        x, y, c = _mesh_pos()
        me = 4 * x + 2 * y + c
        out = []
        for t, (kind, _) in enumerate(self.items):
            own = in_refs[t] if kind == "g" else in_refs[t].at[me]
            dst = out_refs[t].at[me]
            if self.window is not None:
                dst = out_refs[t].at[me, :, pl.ds(*self.window)]
            out.append(pltpu.make_async_copy(own, dst, local_sems.at[t]))
            for k in range(1, N_DEV):
                px, py, pc = x ^ (k >> 2), y ^ ((k >> 1) & 1), c ^ (k & 1)
                src = in_refs[t] if kind == "g" else in_refs[t].at[4 * px + 2 * py + pc]
                out.append(pltpu.make_async_remote_copy(
                    src_ref=src, dst_ref=dst,
                    send_sem=send_sems.at[7 * t + k - 1], recv_sem=recv_sems.at[7 * t + k - 1],
                    device_id=(px, py, pc), device_id_type=pl.DeviceIdType.MESH))
        return out


def _host(body, n_in, n_out, hosted, grid):
    if hosted is None:
        return body
    nc = hosted.n

    def wrapped(*refs):
        ins = refs[:n_in]
        cin = refs[n_in:n_in + nc]
        outs = refs[n_in + nc:n_in + nc + n_out]
        cout = refs[n_in + nc + n_out:n_in + 2 * nc + n_out]
        scratch = refs[n_in + 2 * nc + n_out:len(refs) - 3]
        sems = refs[len(refs) - 3:]
        ids = [pl.program_id(d) for d in range(len(grid))]
        first = functools.reduce(lambda a, b: a & b, [i == 0 for i in ids])
        last = functools.reduce(lambda a, b: a & b, [i == g - 1 for i, g in zip(ids, grid)])

        @pl.when(first)
        def _():
            for cp in hosted.copies(cin, cout, *sems):
                cp.start()

        body(*ins, *outs, *scratch)

        @pl.when(last)
        def _():
            for cp in hosted.copies(cin, cout, *sems):
                cp.wait()

    return wrapped


def _hosted_call(body, grid, in_specs, out_specs, out_shape, scratch_shapes, hosted, name, args):
    n_out = len(out_specs)
    if hosted is None:
        outs = pl.pallas_call(body, grid=grid, in_specs=in_specs, out_specs=out_specs, out_shape=out_shape,
                              scratch_shapes=scratch_shapes, compiler_params=_params(), name=name)(*args)
        return list(outs), []
    outs = pl.pallas_call(
        _host(body, len(in_specs), n_out, hosted, grid),
        grid=grid,
        in_specs=list(in_specs) + hosted.specs(),
        out_specs=list(out_specs) + hosted.specs(),
        out_shape=list(out_shape) + hosted.out_shapes(),
        scratch_shapes=list(scratch_shapes) + hosted.scratch(),
        compiler_params=_params(),
        name=name,
    )(*args, *hosted.operands())
    return list(outs[:n_out]), list(outs[n_out:])


_DIMS = {"nn": (((1,), (0,)), ((), ())), "nt": (((1,), (1,)), ((), ())), "tn": (((0,), (0,)), ((), ()))}


def _matmul(pairs, mode, out_dtype, tm, tn, tk, name, hosted=None, n_cols=None, n_off=0):
    a0, b0 = pairs[0]
    if mode == "nn":
        (M, K), N = a0.shape, b0.shape[1]
    elif mode == "nt":
        (M, K), N = a0.shape, b0.shape[0]
    else:
        (K, M), N = a0.shape, b0.shape[1]
    N = N if n_cols is None else n_cols
    tm, tn, tk = min(tm, M), min(tn, N), min(tk, K)
    assert M % tm == 0 and N % tn == 0 and K % tk == 0, (name, M, N, K, tm, tn, tk)
    nm, nn, nk = M // tm, N // tn, K // tk
    npair = len(pairs)
    dims = _DIMS[mode]

    def body(*refs):
        ab = refs[:2 * npair]
        o_ref = refs[2 * npair]
        acc_ref = refs[2 * npair + 1]
        k = pl.program_id(2)
        part = lax.dot_general(ab[0][...], ab[1][...], dims, preferred_element_type=F32)
        for q in range(1, npair):
            part += lax.dot_general(ab[2 * q][...], ab[2 * q + 1][...], dims, preferred_element_type=F32)
        if nk == 1:
            o_ref[...] = part.astype(o_ref.dtype)
        else:
            @pl.when(k == 0)
            def _():
                acc_ref[...] = part

            @pl.when(k > 0)
            def _():
                acc_ref[...] += part

            @pl.when(k == nk - 1)
            def _():
                o_ref[...] = acc_ref[...].astype(o_ref.dtype)

    if mode == "nn":
        a_spec = pl.BlockSpec((tm, tk), lambda n, m, k: (m, k))
        b_spec = pl.BlockSpec((tk, tn), lambda n, m, k: (k, n + n_off))
    elif mode == "nt":
        a_spec = pl.BlockSpec((tm, tk), lambda n, m, k: (m, k))
        b_spec = pl.BlockSpec((tn, tk), lambda n, m, k: (n + n_off, k))
    else:
        a_spec = pl.BlockSpec((tk, tm), lambda n, m, k: (k, m))
        b_spec = pl.BlockSpec((tk, tn), lambda n, m, k: (k, n + n_off))
    args = [t for pr in pairs for t in pr]
    outs, moved = _hosted_call(
        body, (nn, nm, nk), [a_spec, b_spec] * npair, [pl.BlockSpec((tm, tn), lambda n, m, k: (m, n))],
        [jax.ShapeDtypeStruct((M, N), out_dtype)], [pltpu.VMEM((tm, tn) if nk > 1 else (8, 128), F32)],
        hosted, name, args)
    return outs[0] if hosted is None else (outs[0], moved)


def _rms_fwd(x, g, name):
    T, D = x.shape
    tr = min(512, T)

    def body(x_ref, g_ref, u_ref):
        xf = x_ref[...]
        r = lax.rsqrt(jnp.mean(xf * xf, axis=-1, keepdims=True) + EPS)
        u_ref[...] = ((xf * r) * g_ref[...]).astype(u_ref.dtype)

    return pl.pallas_call(
        body,
        grid=(T // tr,),
        in_specs=[pl.BlockSpec((tr, D), lambda i: (i, 0)), pl.BlockSpec((1, D), lambda i: (0, 0))],
        out_specs=pl.BlockSpec((tr, D), lambda i: (i, 0)),
        out_shape=jax.ShapeDtypeStruct((T, D), BF16),
        name=name,
    )(x, g)


def _rms_bwd_terms(xin, g, dy):
    r = lax.rsqrt(jnp.mean(xin * xin, axis=-1, keepdims=True) + EPS)
    xh = xin * r
    dg = jnp.sum(dy * xh, axis=0, keepdims=True)
    dxh = dy * g
    dx = r * (dxh - xh * jnp.mean(dxh * xh, axis=-1, keepdims=True))
    return dx, dg


GATE_TC = 256


def _branch_gate(proj, ys, wbs):
    T = proj.shape[0]
    D = D_MODEL
    tr, tc = min(1024, T), GATE_TC
    nc = D // tc
    gl0 = COL_GL * 128 // tc

    def body(ya, yb, yc, wa, wb, wc, g0, g1, g2, merged_ref, pa, pb, pc):
        acc = jnp.zeros((tr, tc), F32)
        for y_ref, w_ref, g_ref, p_ref in ((ya, wa, g0, pa), (yb, wb, g1, pb), (yc, wc, g2, pc)):
            p = _dot(y_ref[...], w_ref[...])
            p_ref[...] = p.astype(p_ref.dtype)
            acc += jax.nn.sigmoid(g_ref[...].astype(F32)) * p
        merged_ref[...] = acc.astype(merged_ref.dtype)

    y_spec = pl.BlockSpec((tr, 512), lambda i, n: (i, 0))
    w_spec = pl.BlockSpec((512, tc), lambda i, n: (0, n))
    o_spec = pl.BlockSpec((tr, tc), lambda i, n: (i, n))
    gl_specs = [pl.BlockSpec((tr, tc), lambda i, n, j=j: (i, gl0 + j * nc + n)) for j in range(3)]
    out = jax.ShapeDtypeStruct((T, D), BF16)
    return pl.pallas_call(
        body,
        grid=(T // tr, nc),
        in_specs=[y_spec] * 3 + [w_spec] * 3 + gl_specs,
        out_specs=[o_spec] * 4,
        out_shape=[out] * 4,
        compiler_params=_params(),
        name="branch_gate",
    )(*ys, *wbs, proj, proj, proj)


def _post_pre(x, merged, w_out, g_post, g_pre):
    T, D = x.shape
    tr = 512

    def body(x_ref, m_ref, w_ref, gp_ref, gq_ref, mix_ref, h1_ref, u2_ref):
        mx = _dot(m_ref[...], w_ref[...])
        mix_ref[...] = mx
        r = lax.rsqrt(jnp.mean(mx * mx, axis=-1, keepdims=True) + EPS)
        h1 = x_ref[...] + (mx * r) * gp_ref[...]
        h1_ref[...] = h1
        r2 = lax.rsqrt(jnp.mean(h1 * h1, axis=-1, keepdims=True) + EPS)
        u2_ref[...] = ((h1 * r2) * gq_ref[...]).astype(u2_ref.dtype)

    row = pl.BlockSpec((tr, D), lambda i: (i, 0))
    vec = pl.BlockSpec((1, D), lambda i: (0, 0))
    f32 = jax.ShapeDtypeStruct((T, D), F32)
    return pl.pallas_call(
        body,
        grid=(T // tr,),
        in_specs=[row, row, pl.BlockSpec((D, D), lambda i: (0, 0)), vec, vec],
        out_specs=[row, row, row],
        out_shape=[f32, f32, jax.ShapeDtypeStruct((T, D), BF16)],
        compiler_params=_params(),
        name="post_pre",
    )(x, merged, w_out, g_post, g_pre)


def _ffn_up(u2, w_gate_t, w_up_t, hosted=None):
    T, D = u2.shape
    F = D_FF
    tm, tn = 512, F // 2

    def body(u_ref, wg_ref, wu_ref, a_ref, zg_ref, zu_ref):
        u = u_ref[...]
        zg = _dot_nt(u, wg_ref[...])
        zu = _dot_nt(u, wu_ref[...])
        a_ref[...] = (zg * jax.nn.sigmoid(zg) * zu).astype(a_ref.dtype)
        zg_ref[...] = zg.astype(zg_ref.dtype)
        zu_ref[...] = zu.astype(zu_ref.dtype)

    o_spec = pl.BlockSpec((tm, tn), lambda n, m: (m, n))
    out = jax.ShapeDtypeStruct((T, F), BF16)
    outs, moved = _hosted_call(
        body, (F // tn, T // tm),
        [pl.BlockSpec((tm, D), lambda n, m: (m, 0)),
         pl.BlockSpec((tn, D), lambda n, m: (n, 0)),
         pl.BlockSpec((tn, D), lambda n, m: (n, 0))],
        [o_spec] * 3, [out] * 3, [], hosted, "ffn_up", (u2, w_gate_t, w_up_t))
    return (*outs, moved)


def _loss_head(a, w_down, h1, target, g_post):
    T, D = h1.shape
    F = a.shape[1]
    tr = 512

    def body(a_ref, w_ref, h1_ref, t_ref, g_ref, dffn_ref, dh2_ref, loss_ref, dg_ref):
        i = pl.program_id(0)
        f = _dot(a_ref[...], w_ref[...])
        g = g_ref[...]
        r = lax.rsqrt(jnp.mean(f * f, axis=-1, keepdims=True) + EPS)
        xh = f * r
        err = (h1_ref[...] + xh * g) - t_ref[...]
        part = 0.5 * jnp.sum(jnp.mean(err * err, axis=-1, keepdims=True), axis=0, keepdims=True)
        dh2 = err * (1.0 / D)
        dh2_ref[...] = dh2
        dgp = jnp.sum(dh2 * xh, axis=0, keepdims=True)
        dxh = dh2 * g
        dffn_ref[...] = (r * (dxh - xh * jnp.mean(dxh * xh, axis=-1, keepdims=True))).astype(dffn_ref.dtype)

        @pl.when(i == 0)
        def _():
            loss_ref[...] = jnp.zeros_like(loss_ref)
            dg_ref[...] = jnp.zeros_like(dg_ref)

        loss_ref[...] += jnp.broadcast_to(part, loss_ref.shape)
        dg_ref[...] += dgp

    row = pl.BlockSpec((tr, D), lambda i: (i, 0))
    vec = pl.BlockSpec((1, D), lambda i: (0, 0))
    return pl.pallas_call(
        body,
        grid=(T // tr,),
        in_specs=[pl.BlockSpec((tr, F), lambda i: (i, 0)), pl.BlockSpec((F, D), lambda i: (0, 0)), row, row, vec],
        out_specs=[row, row, pl.BlockSpec((8, 128), lambda i: (0, 0)), vec],
        out_shape=[jax.ShapeDtypeStruct((T, D), BF16), jax.ShapeDtypeStruct((T, D), F32),
                   jax.ShapeDtypeStruct((8, 128), F32), jax.ShapeDtypeStruct((1, D), F32)],
        compiler_params=_params(),
        name="loss_head",
    )(a, w_down, h1, target, g_post)


def _ffn_down_bwd(dffn, wd, zg, zu):
    T, D = dffn.shape
    F = D_FF
    tm, tn = 512, F // 2

    def body(d_ref, w_ref, zg_ref, zu_ref, dzg_ref, dzu_ref):
        da = _dot_nt(d_ref[...], w_ref[...])
        zg = zg_ref[...].astype(F32)
        zu = zu_ref[...].astype(F32)
        s = jax.nn.sigmoid(zg)
        dzu_ref[...] = (da * (zg * s)).astype(dzu_ref.dtype)
        dzg_ref[...] = (da * zu * (s * (1.0 + zg * (1.0 - s)))).astype(dzg_ref.dtype)

    z_spec = pl.BlockSpec((tm, tn), lambda n, m: (m, n))
    out = jax.ShapeDtypeStruct((T, F), BF16)
    return pl.pallas_call(
        body,
        grid=(F // tn, T // tm),
        in_specs=[pl.BlockSpec((tm, D), lambda n, m: (m, 0)), pl.BlockSpec((tn, D), lambda n, m: (n, 0)),
                  z_spec, z_spec],
        out_specs=[z_spec, z_spec],
        out_shape=[out, out],
        compiler_params=_params(),
        name="ffn_down_bwd",
    )(dffn, wd, zg, zu)


def _matmul_rows(pairs, tm, tk, rows_in, vecs_in, rows_out, n_vec_out, epilogue, name):
    a0, b0 = pairs[0]
    (M, K), N = a0.shape, b0.shape[1]
    tm, tk = min(tm, M), min(tk, K)
    assert M % tm == 0 and K % tk == 0, (name, M, K, tm, tk)
    nk = K // tk
    npair = len(pairs)
    n_in = 2 * npair + len(rows_in) + len(vecs_in)

    def body(*refs):
        ab = refs[:2 * npair]
        r_in = refs[2 * npair:2 * npair + len(rows_in)]
        v_in = refs[2 * npair + len(rows_in):n_in]
        r_out = refs[n_in:n_in + len(rows_out)]
        v_out = refs[n_in + len(rows_out):n_in + len(rows_out) + n_vec_out]
        acc_ref = refs[-1]
        m = pl.program_id(0)
        k = pl.program_id(1)
        part = _dot(ab[0][...], ab[1][...])
        for q in range(1, npair):
            part += _dot(ab[2 * q][...], ab[2 * q + 1][...])

        @pl.when(k == 0)
        def _():
            acc_ref[...] = part

        @pl.when(k > 0)
        def _():
            acc_ref[...] += part

        @pl.when(k == nk - 1)
        def _():
            @pl.when(m == 0)
            def _():
                for v in v_out:
                    v[...] = jnp.zeros_like(v)

            epilogue(acc_ref[...], r_in, v_in, r_out, v_out)

    row = pl.BlockSpec((tm, N), lambda m, k: (m, 0))
    vec = pl.BlockSpec((1, N), lambda m, k: (0, 0))
    return pl.pallas_call(
        body,
        grid=(M // tm, nk),
        in_specs=[pl.BlockSpec((tm, tk), lambda m, k: (m, k)), pl.BlockSpec((tk, N), lambda m, k: (k, 0))] * npair
        + [row] * len(rows_in) + [vec] * len(vecs_in),
        out_specs=[row] * len(rows_out) + [vec] * n_vec_out,
        out_shape=[jax.ShapeDtypeStruct((M, N), dt) for dt in rows_out]
        + [jax.ShapeDtypeStruct((1, N), F32)] * n_vec_out,
        scratch_shapes=[pltpu.VMEM((tm, N), F32)],
        compiler_params=_params(),
        name=name,
    )(*[t for pr in pairs for t in pr], *rows_in, *vecs_in)


def _mid_bwd(dzg, dzu, w_gate_t, w_up_t, h1, dh2, mix, g_pre, g_post):
    def epilogue(du2, rows, vecs, outs, accs):
        h1_ref, dh2_ref, mix_ref = rows
        gq_ref, gp_ref = vecs
        dh1_ref, dmix_ref = outs
        dx, dgq = _rms_bwd_terms(h1_ref[...], gq_ref[...], du2)
        dh1 = dh2_ref[...] + dx
        dh1_ref[...] = dh1
        dmix, dgp = _rms_bwd_terms(mix_ref[...], gp_ref[...], dh1)
        dmix_ref[...] = dmix.astype(dmix_ref.dtype)
        accs[0][...] += dgq
        accs[1][...] += dgp

    return _matmul_rows([(dzg, w_gate_t), (dzu, w_up_t)], 512, D_FF // 2, [h1, dh2, mix], [g_pre, g_post],
                        [F32, BF16], 2, epilogue, "mid_bwd")


def _gate_bwd(dmix, w_out, ps, proj, wbs):
    T, D = dmix.shape
    tr, tc = min(512, T), GATE_TC
    nc = D // tc
    gl0 = COL_GL * 128 // tc

    def body(*refs):
        dmix_ref, wo_ref = refs[:2]
        p_refs = refs[2:5]
        g_refs = refs[5:5 + 3 * nc]
        w_refs = refs[5 + 3 * nc:8 + 3 * nc]
        outs = refs[8 + 3 * nc:]
        dp_refs, dg_refs, dy_refs = outs[:3], outs[3:6], outs[6:9]
        dm = _dot_nt(dmix_ref[...], wo_ref[...])
        for j in range(3):
            dps = []
            for c in range(nc):
                cols = slice(c * tc, (c + 1) * tc)
                s = jax.nn.sigmoid(g_refs[j * nc + c][...].astype(F32))
                dmc = dm[:, cols]
                dps.append((dmc * s).astype(BF16))
                dg_refs[j][:, cols] = (dmc * p_refs[j][:, cols].astype(F32) * (s * (1.0 - s))).astype(BF16)
            dp = jnp.concatenate(dps, axis=1)
            dp_refs[j][...] = dp
            dy_refs[j][...] = _dot_nt(dp, w_refs[j][...]).astype(BF16)

    row = pl.BlockSpec((tr, D), lambda i: (i, 0))
    y_spec = pl.BlockSpec((tr, 512), lambda i: (i, 0))
    w_spec = pl.BlockSpec((512, D), lambda i: (0, 0))
    gl_specs = [pl.BlockSpec((tr, tc), lambda i, b=gl0 + j * nc + c: (i, b)) for j in range(3) for c in range(nc)]
    big = jax.ShapeDtypeStruct((T, D), BF16)
    small = jax.ShapeDtypeStruct((T, 512), BF16)
    return pl.pallas_call(
        body,
        grid=(T // tr,),
        in_specs=[row, pl.BlockSpec((D, D), lambda i: (0, 0))] + [row] * 3 + gl_specs + [w_spec] * 3,
        out_specs=[row] * 6 + [y_spec] * 3,
        out_shape=[big] * 6 + [small] * 3,
        compiler_params=_params(),
        name="gate_bwd",
    )(dmix, w_out, *ps, *([proj] * (3 * nc)), *wbs)


def _pre_bwd(dproj, w_in_t, x, dh1, g):
    def epilogue(du, rows, vecs, outs, accs):
        x_ref, dh1_ref = rows
        dx, dg = _rms_bwd_terms(x_ref[...], vecs[0][...], du)
        outs[0][...] = dh1_ref[...] + dx
        accs[0][...] += dg

    return _matmul_rows([(dproj, w_in_t)], 512, IN_W // 2, [x, dh1], [g], [F32], 1, epilogue, "pre_bwd")


def _gain_grad(xin, dy):
    T, D = xin.shape
    tr = min(512, T)

    def body(x_ref, dy_ref, dg_ref):
        i = pl.program_id(0)
        xf = x_ref[...]
        r = lax.rsqrt(jnp.mean(xf * xf, axis=-1, keepdims=True) + EPS)

        @pl.when(i == 0)
        def _():
            dg_ref[...] = jnp.zeros_like(dg_ref)

        dg_ref[...] += jnp.sum(dy_ref[...] * (xf * r), axis=0, keepdims=True)

    row = pl.BlockSpec((tr, D), lambda i: (i, 0))
    return pl.pallas_call(
        body,
        grid=(T // tr,),
        in_specs=[row, row],
        out_specs=pl.BlockSpec((1, D), lambda i: (0, 0)),
        out_shape=jax.ShapeDtypeStruct((1, D), F32),
        name="gain_grad",
    )(xin, dy)


def _swa_buckets():
    dist = (np.arange(BLOCK)[:, None] + BLOCK) - np.arange(2 * BLOCK)[None, :]
    max_exact = N_BUCKETS // 2
    d = np.maximum(dist, 0)
    df = np.maximum(d, 1).astype(np.float32)
    large = max_exact + (np.log(df / np.float32(max_exact)) / np.float32(math.log(MAX_DISTANCE / max_exact))
                         * np.float32(N_BUCKETS - max_exact)).astype(np.int32)
    large = np.minimum(large, N_BUCKETS - 1)
    bucket = np.where(d < max_exact, d, large)
    in_win = (dist >= 0) & (dist < SWA_WINDOW)
    return np.where(in_win, bucket, -1).astype(np.int32)


def _swa_bias_table(rel_bias, buckets):
    H = SWA_Q_HEADS

    def body(rb_ref, bk_ref, o_ref):
        bk = bk_ref[...]
        for h in range(H):
            acc = jnp.full(bk.shape, NEG, F32)
            for b in range(N_BUCKETS):
                acc = jnp.where(bk == b, rb_ref[b, h], acc)
            o_ref[h] = acc

    return pl.pallas_call(
        body,
        in_specs=[pl.BlockSpec(memory_space=pltpu.SMEM), pl.BlockSpec(memory_space=pltpu.VMEM)],
        out_specs=pl.BlockSpec(memory_space=pltpu.VMEM),
        out_shape=jax.ShapeDtypeStruct((H, BLOCK, 2 * BLOCK), F32),
        name="swa_bias_table",
    )(rel_bias, buckets)


def _swa_bias_grad(dbias, buckets):
    H = SWA_Q_HEADS

    def body(db_ref, bk_ref, o_ref):
        bk = bk_ref[...]
        rows = lax.broadcasted_iota(jnp.int32, (N_BUCKETS, 128), 0)
        lanes = lax.broadcasted_iota(jnp.int32, (N_BUCKETS, 128), 1)
        acc = jnp.zeros((N_BUCKETS, 128), F32)
        for h in range(H):
            d = db_ref[h]
            for b in range(N_BUCKETS):
                s = jnp.sum(jnp.sum(jnp.where(bk == b, d, 0.0), axis=1, keepdims=True), axis=0, keepdims=True)
                acc = jnp.where((rows == b) & (lanes == h), s, acc)
        o_ref[...] = acc

    return pl.pallas_call(
        body,
        in_specs=[pl.BlockSpec(memory_space=pltpu.VMEM)] * 2,
        out_specs=pl.BlockSpec(memory_space=pltpu.VMEM),
        out_shape=jax.ShapeDtypeStruct((N_BUCKETS, 128), F32),
        name="swa_bias_grad",
    )(dbias, buckets)


SWA_GROUP = 4
SWA_ROWS = SWA_Q_HEADS * BLOCK


def _swa_kv_lanes(h):
    lane = lax.broadcasted_iota(jnp.int32, (BLOCK, BLOCK), 1)
    return (lane // HEAD_DIM) == h // SWA_GROUP


def _swa_stack(x, heads):
    blocks = []
    for h in heads:
        xp = x[:, (h // 2) * BLOCK:(h // 2 + 1) * BLOCK]
        xs = xp if h % 2 == h // SWA_GROUP else pltpu.roll(xp, HEAD_DIM, 1)
        blocks.append(jnp.where(_swa_kv_lanes(h), xs, 0.0))
    return jnp.concatenate(blocks, axis=0).astype(BF16)


def _swa_unstack(blocks):
    pairs = []
    for p in range(4):
        halves = []
        for hh in range(2):
            h = 2 * p + hh
            blk = jnp.where(_swa_kv_lanes(h), blocks[h], 0.0)
            halves.append(blk if hh == h // SWA_GROUP else pltpu.roll(blk, HEAD_DIM, 1))
        pairs.append(halves[0] + halves[1])
    return jnp.concatenate(pairs, axis=1)


def _swa_stacked_params(sink_ref, bias_ref, heads):
    bias = jnp.concatenate([bias_ref[h] for h in heads], axis=0)
    sink = jnp.concatenate([jnp.full((BLOCK, 1), sink_ref[0, h], F32) for h in heads], axis=0)
    return bias, sink


def _swa_scores(qs, kp, kc, bias, sink, first):
    sp = _dot_nt(qs, kp) * SCALE64 + bias[:, :BLOCK]
    sp = jnp.where(first, NEG, sp)
    sc = _dot_nt(qs, kc) * SCALE64 + bias[:, BLOCK:]
    m = jnp.maximum(jnp.maximum(jnp.max(sp, axis=1, keepdims=True), jnp.max(sc, axis=1, keepdims=True)), sink)
    pp = jnp.exp(sp - m)
    pc = jnp.exp(sc - m)
    ps = jnp.exp(sink - m)
    den = jnp.sum(pp, axis=1, keepdims=True) + jnp.sum(pc, axis=1, keepdims=True) + ps
    return pp / den, pc / den, ps / den


def _swa_fwd(proj, sinks, bias_tab, B, S, hosted=None):
    nb = S // BLOCK
    T = B * S

    def body(sink_ref, q_ref, kp_ref, kc_ref, vp_ref, vc_ref, bias_ref, o_ref):
        i = pl.program_id(1)
        first = jnp.full((SWA_GROUP * BLOCK, BLOCK), i, jnp.int32) == 0
        q = q_ref[...].astype(F32)
        blocks = []
        for g in range(SWA_Q_HEADS // SWA_GROUP):
            heads = range(SWA_GROUP * g, SWA_GROUP * (g + 1))
            bias, sink = _swa_stacked_params(sink_ref, bias_ref, heads)
            wp, wc, _ = _swa_scores(_swa_stack(q, heads), kp_ref[...], kc_ref[...], bias, sink, first)
            o = _dot(wp.astype(BF16), vp_ref[...]) + _dot(wc.astype(BF16), vc_ref[...])
            blocks += [o[t * BLOCK:(t + 1) * BLOCK] for t in range(SWA_GROUP)]
        o_ref[...] = _swa_unstack(blocks).astype(o_ref.dtype)

    blk = (BLOCK, BLOCK)
    wide = (BLOCK, 4 * BLOCK)
    outs, moved = _hosted_call(
        body, (B, nb),
        [pl.BlockSpec(memory_space=pltpu.SMEM),
         pl.BlockSpec(wide, lambda b, i: (b * nb + i, COL_QA // 4)),
         pl.BlockSpec(blk, lambda b, i: (b * nb + jnp.maximum(i - 1, 0), COL_KA)),
         pl.BlockSpec(blk, lambda b, i: (b * nb + i, COL_KA)),
         pl.BlockSpec(blk, lambda b, i: (b * nb + jnp.maximum(i - 1, 0), COL_VA)),
         pl.BlockSpec(blk, lambda b, i: (b * nb + i, COL_VA)),
         pl.BlockSpec((SWA_Q_HEADS, BLOCK, 2 * BLOCK), lambda b, i: (0, 0, 0))],
        [pl.BlockSpec(wide, lambda b, i: (b * nb + i, 0))],
        [jax.ShapeDtypeStruct((T, 512), BF16)], [], hosted, "swa_fwd",
        (sinks, proj, proj, proj, proj, proj, bias_tab))
    return outs[0], moved


def _swa_bwd(proj, dy, sinks, bias_tab, B, S, hosted=None):
    nb = S // BLOCK
    T = B * S
    H = SWA_Q_HEADS

    def body(sink_ref, q_ref, kp_ref, kc_ref, vp_ref, vc_ref, do_ref, bias_ref,
             dq_ref, dk_ref, dv_ref, dbias_ref, dsink_ref):
        b = pl.program_id(0)
        i = pl.program_id(1)

        @pl.when((b == 0) & (i == 0))
        def _():
            dbias_ref[...] = jnp.zeros_like(dbias_ref)
            dsink_ref[...] = jnp.zeros_like(dsink_ref)

        @pl.when(i == 0)
        def _():
            dk_ref[...] = jnp.zeros_like(dk_ref)
            dv_ref[...] = jnp.zeros_like(dv_ref)

        first = jnp.full((SWA_ROWS, BLOCK), i, jnp.int32) == 0
        heads = range(H)
        bias, sink = _swa_stacked_params(sink_ref, bias_ref, heads)
        kp, kc, vp, vc = kp_ref[...], kc_ref[...], vp_ref[...], vc_ref[...]
        qs = _swa_stack(q_ref[...].astype(F32), heads)
        dos = _swa_stack(do_ref[...].astype(F32), heads)
        wp, wc, ws = _swa_scores(qs, kp, kc, bias, sink, first)
        dwp = _dot_nt(dos, vp)
        dwc = _dot_nt(dos, vc)
        dsum = jnp.sum(wp * dwp, axis=1, keepdims=True) + jnp.sum(wc * dwc, axis=1, keepdims=True)
        dsp = wp * (dwp - dsum)
        dsc = wc * (dwc - dsum)
        dsk = -ws * dsum
        dsinks = []
        for h in range(H):
            rows = slice(h * BLOCK, (h + 1) * BLOCK)
            dsinks.append(jnp.broadcast_to(jnp.sum(dsk[rows], axis=0, keepdims=True), (1, 128)))
            dbias_ref[h] += jnp.concatenate([dsp[rows], dsc[rows]], axis=1)
        dsink_ref[...] += jnp.concatenate(dsinks, axis=0)
        dspb = dsp.astype(BF16)
        dscb = dsc.astype(BF16)
        dq = _dot(dspb, kp) + _dot(dscb, kc)
        dq_ref[...] = (_swa_unstack([dq[h * BLOCK:(h + 1) * BLOCK] for h in heads]) * SCALE64).astype(dq_ref.dtype)
        cur = pl.ds(pl.multiple_of(i * BLOCK, BLOCK), BLOCK)
        prev = pl.ds(pl.multiple_of(jnp.maximum(i - 1, 0) * BLOCK, BLOCK), BLOCK)
        dk_ref[prev, :] += _dot_tn(dspb, qs) * SCALE64
        dk_ref[cur, :] += _dot_tn(dscb, qs) * SCALE64
        dv_ref[prev, :] += _dot_tn(wp.astype(BF16), dos)
        dv_ref[cur, :] += _dot_tn(wc.astype(BF16), dos)

    blk = (BLOCK, BLOCK)
    wide = (BLOCK, 4 * BLOCK)
    kv_out = pl.BlockSpec((S, BLOCK), lambda b, i: (b, 0))
    full_bias = pl.BlockSpec((H, BLOCK, 2 * BLOCK), lambda b, i: (0, 0, 0))
    outs, moved = _hosted_call(
        body, (B, nb),
        [pl.BlockSpec(memory_space=pltpu.SMEM),
         pl.BlockSpec(wide, lambda b, i: (b * nb + i, COL_QA // 4)),
         pl.BlockSpec(blk, lambda b, i: (b * nb + jnp.maximum(i - 1, 0), COL_KA)),
         pl.BlockSpec(blk, lambda b, i: (b * nb + i, COL_KA)),
         pl.BlockSpec(blk, lambda b, i: (b * nb + jnp.maximum(i - 1, 0), COL_VA)),
         pl.BlockSpec(blk, lambda b, i: (b * nb + i, COL_VA)),
         pl.BlockSpec(wide, lambda b, i: (b * nb + i, 0)),
         full_bias],
        [pl.BlockSpec(wide, lambda b, i: (b * nb + i, 0)),
         kv_out, kv_out, full_bias,
         pl.BlockSpec((H, 128), lambda b, i: (0, 0))],
        [jax.ShapeDtypeStruct((T, 512), BF16),
         jax.ShapeDtypeStruct((T, BLOCK), F32), jax.ShapeDtypeStruct((T, BLOCK), F32),
         jax.ShapeDtypeStruct((H, BLOCK, 2 * BLOCK), F32), jax.ShapeDtypeStruct((H, 128), F32)],
        [], hosted, "swa_bwd", (sinks, proj, proj, proj, proj, proj, dy, bias_tab))
    return (*outs, moved)


SB_TILE = 256


SB_HEADS = 4
SB_LANES = SB_HEADS * HEAD_DIM
SB_ROWS = SB_HEADS * SB_TILE


def _sb_logits(z, tri):
    sp = jnp.log(1.0 + jnp.exp(-jnp.abs(z)))
    ls = jnp.minimum(z, 0.0) - sp
    l1m = ls - z
    if tri is not None:
        l1m = jnp.where(tri, l1m, 0.0)
    return ls, l1m


def _sb_masks():
    lane = lax.broadcasted_iota(jnp.int32, (SB_TILE, SB_LANES), 1)
    hm = [(lane // HEAD_DIM) == h for h in range(SB_HEADS)]
    row = lax.broadcasted_iota(jnp.int32, (SB_ROWS, SB_TILE), 0) % SB_TILE
    col = lax.broadcasted_iota(jnp.int32, (SB_ROWS, SB_TILE), 1)
    return lane, hm, row, col


def _sb_stack(x, hm):
    return jnp.concatenate([jnp.where(m, x, 0) for m in hm], axis=0)


def _sb_unstack(x, hm):
    return sum(jnp.where(m, x[h * SB_TILE:(h + 1) * SB_TILE], 0.0) for h, m in enumerate(hm))


def _sb_fwd(proj, B, S, hosted=None):
    nt = S // SB_TILE
    T = B * S

    ng = 512 // SB_LANES

    def body(*refs):
        q_refs, k_refs, v_refs = refs[:ng], refs[ng:2 * ng], refs[2 * ng:3 * ng]
        o_ref, r_ref = refs[3 * ng:]
        i = pl.program_id(1)
        _, hm, row, col = _sb_masks()
        tri = col < row
        later = jnp.concatenate([(row[:SB_TILE] > col[:SB_TILE]).astype(BF16)] * 2, axis=0)
        qs = [_sb_stack(q_ref[...] * SCALE64, hm) for q_ref in q_refs]

        def tile(j, carry, mask):
            rows = pl.ds(pl.multiple_of(j * SB_TILE, SB_TILE), SB_TILE)
            out = []
            for g in range(ng):
                acc, c = carry[g]
                ls, l1m = _sb_logits(_dot_nt(qs[g], k_refs[g][rows, :]), mask)
                a = jnp.exp(ls + c + _dot_split(l1m, later))
                if mask is not None:
                    a = jnp.where(mask, a, 0.0)
                pv = _dot(a.astype(BF16), v_refs[g][rows, :])
                out.append((acc + _sb_unstack(pv, hm), c + jnp.sum(l1m, axis=1, keepdims=True)))
            return tuple(out)

        zero = (jnp.zeros((SB_TILE, SB_LANES), F32), jnp.zeros((SB_ROWS, 1), F32))
        carry = tile(i, (zero,) * ng, tri)
        carry = lax.fori_loop(0, i, lambda it, cr: tile(i - 1 - it, cr, None), carry)
        o_ref[...] = jnp.concatenate([acc for acc, _ in carry], axis=1).astype(o_ref.dtype)
        r_ref[...] = jnp.concatenate(
            [_sb_unstack(jnp.broadcast_to(c, (SB_ROWS, SB_LANES)), hm) for _, c in carry], axis=1)

    blk = (SB_TILE, SB_LANES)
    cq, ck, cv = (c * BLOCK // SB_LANES for c in (COL_QB, COL_KB, COL_VB))
    wide = pl.BlockSpec((SB_TILE, 512), lambda b, i: (b * nt + i, 0))
    outs, moved = _hosted_call(
        body, (B, nt),
        [pl.BlockSpec(blk, lambda b, i, g=g: (b * nt + i, cq + g)) for g in range(ng)]
        + [pl.BlockSpec((S, SB_LANES), lambda b, i, g=g: (b, ck + g)) for g in range(ng)]
        + [pl.BlockSpec((S, SB_LANES), lambda b, i, g=g: (b, cv + g)) for g in range(ng)],
        [wide, wide],
        [jax.ShapeDtypeStruct((T, 512), BF16), jax.ShapeDtypeStruct((T, 512), F32)], [], hosted, "sb_fwd",
        (proj,) * (3 * ng))
    return outs[0], outs[1], moved


def _sb_bwd(proj, dy, rtot, B, S, hosted=None):
    nt = S // SB_TILE
    T = B * S

    ng = 512 // SB_LANES

    def body(*refs):
        q_refs, k_refs, v_refs = refs[:ng], refs[ng:2 * ng], refs[2 * ng:3 * ng]
        do_ref, r_ref, dq_ref, dk_ref, dv_ref = refs[3 * ng:]
        i = pl.program_id(1)

        @pl.when(i == 0)
        def _():
            dk_ref[...] = jnp.zeros_like(dk_ref)
            dv_ref[...] = jnp.zeros_like(dv_ref)

        lane, hm, row, col = _sb_masks()
        tri = col < row
        later = jnp.concatenate([(row[:SB_TILE] > col[:SB_TILE]).astype(BF16)] * 2, axis=0)
        earlier = (row[:SB_TILE] < col[:SB_TILE]).astype(BF16)
        qs, dos, rs = [], [], []
        for g in range(ng):
            lanes = slice(g * SB_LANES, (g + 1) * SB_LANES)
            qs.append(_sb_stack(q_refs[g][...] * SCALE64, hm))
            dos.append(_sb_stack(do_ref[:, lanes], hm))
            r = r_ref[:, lanes]
            rs.append(jnp.concatenate([jnp.sum(jnp.where(lane == h * HEAD_DIM, r, 0.0), axis=1, keepdims=True)
                                       for h in range(SB_HEADS)], axis=0))

        def tile(j, carry, mask):
            rows = pl.ds(pl.multiple_of(j * SB_TILE, SB_TILE), SB_TILE)
            out = []
            for g in range(ng):
                lanes = slice(g * SB_LANES, (g + 1) * SB_LANES)
                dq, lsum, psum = carry[g]
                kj = k_refs[g][rows, :]
                vj = v_refs[g][rows, :]
                ls, l1m = _sb_logits(_dot_nt(qs[g], kj), mask)
                sig = jnp.exp(ls)
                lsum = lsum + jnp.sum(l1m, axis=1, keepdims=True)
                a = jnp.exp(ls + (rs[g] - lsum) + _dot_split(l1m, later))
                if mask is not None:
                    a = jnp.where(mask, a, 0.0)
                de = _dot_nt(dos[g], vj) * a
                pre = psum + _dot(de.astype(BF16), earlier)
                dz = de - sig * (de + pre)
                if mask is not None:
                    dz = jnp.where(mask, dz, 0.0)
                dz = dz.astype(BF16)
                dk_ref[rows, lanes] += _dot_tn(dz, qs[g])
                dv_ref[rows, lanes] += _dot_tn(a.astype(BF16), dos[g])
                out.append((dq + _sb_unstack(_dot(dz, kj), hm), lsum, psum + jnp.sum(de, axis=1, keepdims=True)))
            return tuple(out)

        zero = jnp.zeros((SB_ROWS, 1), F32)
        init = ((jnp.zeros((SB_TILE, SB_LANES), F32), zero, zero),) * ng
        carry = lax.fori_loop(0, i, lambda j, c: tile(j, c, None), init)
        carry = tile(i, carry, tri)
        dq_ref[...] = (jnp.concatenate([c[0] for c in carry], axis=1) * SCALE64).astype(dq_ref.dtype)

    blk = (SB_TILE, SB_LANES)
    cq, ck, cv = (c * BLOCK // SB_LANES for c in (COL_QB, COL_KB, COL_VB))
    wide = pl.BlockSpec((SB_TILE, 512), lambda b, i: (b * nt + i, 0))
    kv_out = pl.BlockSpec((S, 512), lambda b, i: (b, 0))
    outs, moved = _hosted_call(
        body, (B, nt),
        [pl.BlockSpec(blk, lambda b, i, g=g: (b * nt + i, cq + g)) for g in range(ng)]
        + [pl.BlockSpec((S, SB_LANES), lambda b, i, g=g: (b, ck + g)) for g in range(ng)]
        + [pl.BlockSpec((S, SB_LANES), lambda b, i, g=g: (b, cv + g)) for g in range(ng)]
        + [wide, wide],
        [wide, kv_out, kv_out],
        [jax.ShapeDtypeStruct((T, 512), BF16),
         jax.ShapeDtypeStruct((T, 512), F32), jax.ShapeDtypeStruct((T, 512), F32)], [], hosted, "sb_bwd",
        (proj,) * (3 * ng) + (dy, rtot))
    return outs[0], outs[1], outs[2], moved


def _mem_weights(q, mk):
    z = _dot_nt(q, mk) * SCALE128
    e = jnp.exp(z - jnp.max(z, axis=1, keepdims=True))
    return e / jnp.sum(e, axis=1, keepdims=True)


def _mem_fwd(proj, mkv, B, S, M):
    tq = 512
    nq = S // tq
    T = B * S
    Hm = MEM_HEADS

    def body(q0, q1, q2, q3, mk_ref, mv_ref, o_ref):
        outs = []
        for h, q_ref in enumerate((q0, q1, q2, q3)):
            cols = slice(h * 128, (h + 1) * 128)
            w = _mem_weights(q_ref[...], mk_ref[:, cols])
            outs.append(_dot(w.astype(BF16), mv_ref[:, cols]))
        o_ref[...] = jnp.concatenate(outs, axis=1).astype(o_ref.dtype)

    return pl.pallas_call(
        body,
        grid=(B, nq),
        in_specs=[pl.BlockSpec((tq, 128), lambda b, i, h=h: (b * nq + i, COL_QM + h)) for h in range(Hm)]
        + [pl.BlockSpec((M, 512), lambda b, i: (b, 0)), pl.BlockSpec((M, 512), lambda b, i: (b, 1))],
        out_specs=pl.BlockSpec((tq, 512), lambda b, i: (b * nq + i, 0)),
        out_shape=jax.ShapeDtypeStruct((T, 512), BF16),
        name="mem_fwd",
    )(proj, proj, proj, proj, mkv, mkv)


def _mem_bwd(proj, mkv, dy, B, S, M):
    tq = 512
    nq = S // tq
    T = B * S
    Hm = MEM_HEADS

    def body(q0, q1, q2, q3, mk_ref, mv_ref, do_ref, dq_ref, dmk_ref, dmv_ref):
        i = pl.program_id(1)
        dqs, dmks, dmvs = [], [], []
        for h, q_ref in enumerate((q0, q1, q2, q3)):
            cols = slice(h * 128, (h + 1) * 128)
            q = q_ref[...]
            do = do_ref[:, cols]
            mk = mk_ref[:, cols]
            w = _mem_weights(q, mk)
            dw = _dot_nt(do, mv_ref[:, cols])
            ds = (w * (dw - jnp.sum(w * dw, axis=1, keepdims=True))).astype(BF16)
            dqs.append(_dot(ds, mk))
            dmks.append(_dot_tn(ds, q))
            dmvs.append(_dot_tn(w.astype(BF16), do))
        dq_ref[...] = (jnp.concatenate(dqs, axis=1) * SCALE128).astype(dq_ref.dtype)

        @pl.when(i == 0)
        def _():
            dmk_ref[...] = jnp.zeros_like(dmk_ref)
            dmv_ref[...] = jnp.zeros_like(dmv_ref)

        dmk_ref[...] += jnp.concatenate(dmks, axis=1) * SCALE128
        dmv_ref[...] += jnp.concatenate(dmvs, axis=1)

    q_spec = pl.BlockSpec((tq, 512), lambda b, i: (b * nq + i, 0))
    m_out = pl.BlockSpec((M, 512), lambda b, i: (b, 0))
    return pl.pallas_call(
        body,
        grid=(B, nq),
        in_specs=[pl.BlockSpec((tq, 128), lambda b, i, h=h: (b * nq + i, COL_QM + h)) for h in range(Hm)]
        + [pl.BlockSpec((M, 512), lambda b, i: (b, 0)), pl.BlockSpec((M, 512), lambda b, i: (b, 1)), q_spec],
        out_specs=[q_spec, m_out, m_out],
        out_shape=[jax.ShapeDtypeStruct((T, 512), BF16),
                   jax.ShapeDtypeStruct((B * M, 512), F32), jax.ShapeDtypeStruct((B * M, 512), F32)],
        name="mem_bwd",
    )(proj, proj, proj, proj, mkv, mkv, dy)


def _all_gather(blk, name):
    R, C = blk.shape

    def body(x_ref, out_ref, send_sems, recv_sems, local_sem):
        x, y, c = _mesh_pos()
        me, sibling = (x, y, c), (x, y, 1 - c)
        chips = [(1 - x, y), (x, 1 - y), (1 - x, 1 - y)]

        def slot(px, py, pc):
            return out_ref.at[4 * px + 2 * py + pc]

        def copy(k, block, to, src=None):
            return pltpu.make_async_remote_copy(
                src_ref=slot(*block) if src is None else src, dst_ref=slot(*block),
                send_sem=send_sems.at[k], recv_sem=recv_sems.at[k],
                device_id=to, device_id_type=pl.DeviceIdType.MESH)

        mine = pltpu.make_async_copy(x_ref, slot(*me), local_sem)
        mine.start()
        first = [copy(0, me, sibling, src=x_ref)]
        first += [copy(1 + j, me, (*chip, c), src=x_ref) for j, chip in enumerate(chips)]
        for cp in first:
            cp.start()
        passed = [copy(4 + j, (*chip, c), sibling) for j, chip in enumerate(chips)]
        for j, chip in enumerate(chips):
            copy(1 + j, (*chip, c), me).wait_recv()
            passed[j].start()
        copy(0, sibling, me).wait_recv()
        for j, chip in enumerate(chips):
            copy(4 + j, (*chip, 1 - c), me).wait_recv()
        for cp in first + passed:
            cp.wait_send()
        mine.wait()

    return pl.pallas_call(
        body,
        in_specs=[pl.BlockSpec(memory_space=pl.ANY)],
        out_specs=pl.BlockSpec(memory_space=pl.ANY),
        out_shape=jax.ShapeDtypeStruct((N_DEV, R, C), blk.dtype),
        scratch_shapes=[pltpu.SemaphoreType.DMA((7,)), pltpu.SemaphoreType.DMA((7,)), pltpu.SemaphoreType.DMA],
        name=name,
    )(blk)


def _all_gather_relayed(blk, name):
    R, C = blk.shape
    R2 = R // 2

    def body(x_ref, out_ref, send_sems, recv_sems, local_sem):
        x, y, c = _mesh_pos()
        me, sib = (x, y, c), (x, y, 1 - c)
        nx, ny, dg = (1 - x, y, c), (x, 1 - y, c), (1 - x, 1 - y, c)

        def slot(p, half=None):
            ref = out_ref.at[4 * p[0] + 2 * p[1] + p[2]]
            return ref if half is None else ref.at[pl.ds(half * R2, R2), :]

        def copy(k, block, to, half=None, own=False):
            src = slot(block, half)
            if own:
                src = x_ref if half is None else x_ref.at[pl.ds(half * R2, R2), :]
            return pltpu.make_async_remote_copy(
                src_ref=src, dst_ref=slot(block, half), send_sem=send_sems.at[k], recv_sem=recv_sems.at[k],
                device_id=to, device_id_type=pl.DeviceIdType.MESH)

        def other(p):
            return (p[0], p[1], 1 - c)

        mine = pltpu.make_async_copy(x_ref, slot(me), local_sem)
        mine.start()
        sends = [copy(1, me, nx, 0, own=True), copy(3, me, ny, 1, own=True), copy(0, me, sib, own=True),
                 copy(2, me, nx, 1, own=True), copy(4, me, ny, 0, own=True)]
        for cp in sends:
            cp.start()
        copy(1, nx, me, 0).wait_recv()
        sends.append(copy(5, nx, ny, 0))
        sends[-1].start()
        copy(3, ny, me, 1).wait_recv()
        sends.append(copy(6, ny, nx, 1))
        sends[-1].start()
        copy(2, nx, me, 1).wait_recv()
        sends.append(copy(7, nx, sib))
        sends[-1].start()
        copy(4, ny, me, 0).wait_recv()
        sends.append(copy(8, ny, sib))
        sends[-1].start()
        copy(5, dg, me, 0).wait_recv()
        copy(6, dg, me, 1).wait_recv()
        sends.append(copy(9, dg, sib))
        sends[-1].start()
        copy(0, sib, me).wait_recv()
        for k, p in ((7, nx), (8, ny), (9, dg)):
            copy(k, other(p), me).wait_recv()
        for cp in sends:
            cp.wait_send()
        mine.wait()

    return pl.pallas_call(
        body,
        in_specs=[pl.BlockSpec(memory_space=pl.ANY)],
        out_specs=pl.BlockSpec(memory_space=pl.ANY),
        out_shape=jax.ShapeDtypeStruct((N_DEV, R, C), blk.dtype),
        scratch_shapes=[pltpu.SemaphoreType.DMA((10,)), pltpu.SemaphoreType.DMA((10,)), pltpu.SemaphoreType.DMA],
        name=name,
    )(blk)


_HBM = pl.BlockSpec(memory_space=pltpu.HBM)
_SEM = pl.BlockSpec(memory_space=pltpu.SEMAPHORE)


def _scatter_start(parts, land, window, carried, name):
    hosted = _Hosted(scatters=[parts], window=window)

    def body(p_ref, land_ref, c_ref, send_sems, recv_sems, local_sems, p_thru, land_thru, c_thru):
        for cp in hosted.copies([p_ref], [land_ref], send_sems, recv_sems, local_sems):
            cp.start()

    sems = (pltpu.SemaphoreType.DMA((7,)), pltpu.SemaphoreType.DMA((7,)), pltpu.SemaphoreType.DMA((1,)))
    hbm = lambda a: pltpu.HBM(a.shape, a.dtype)
    outs = pl.pallas_call(
        body, name=name,
        out_shape=sems + (hbm(parts), hbm(land), hbm(carried)),
        in_specs=(_HBM, _HBM, _HBM), out_specs=(_SEM, _SEM, _SEM, _HBM, _HBM, _HBM),
        input_output_aliases={0: 3, 1: 4, 2: 5},
        compiler_params=pltpu.CompilerParams(has_side_effects=pltpu.SideEffectType.DATAFLOW_SIDE_EFFECTING),
    )(pltpu.with_memory_space_constraint(parts, pltpu.HBM),
      pltpu.with_memory_space_constraint(land, pltpu.HBM),
      pltpu.with_memory_space_constraint(carried, pltpu.HBM))
    return (outs[:4], window), outs[4], outs[5]


def _scatter_wait(flight, land, after, name):
    (send_sems, recv_sems, local_sems, p_thru), window = flight
    hosted = _Hosted(scatters=[p_thru], window=window)

    def body(p_ref, land_ref, send, recv, local, after_ref, p_dead, got_ref):
        for cp in hosted.copies([p_ref], [land_ref], send, recv, local):
            cp.wait()

    hbm = lambda a: pltpu.HBM(a.shape, a.dtype)
    return pl.pallas_call(
        body, name=name,
        out_shape=(hbm(p_thru), hbm(land)),
        in_specs=(_HBM, _HBM, _SEM, _SEM, _SEM, pl.BlockSpec(memory_space=pl.ANY)), out_specs=(_HBM, _HBM),
        input_output_aliases={0: 0, 1: 1},
        compiler_params=pltpu.CompilerParams(has_side_effects=pltpu.SideEffectType.DATAFLOW_SIDE_EFFECTING),
    )(p_thru, land, send_sems, recv_sems, local_sems, after)[1]


def _adamw(parts, w, m, v, name):
    R, C = w.shape
    tr = R
    for cand in (368, 352, 256, 176, 128, 64, 32, 16, 8):
        if R % cand == 0 and cand * C * 4 <= 1536 * 1024:
            tr = cand
            break
    c1 = 1.0 - ADAM_B1 ** ADAM_STEP
    c2 = 1.0 - ADAM_B2 ** ADAM_STEP

    def body(p_ref, w_ref, m_ref, v_ref, g_ref, d_ref, nm_ref, nv_ref):
        g = p_ref[0].astype(F32)
        for d in range(1, N_DEV):
            g = g + p_ref[d].astype(F32)
        nm = ADAM_B1 * m_ref[...] + (1.0 - ADAM_B1) * g
        nv = ADAM_B2 * v_ref[...] + (1.0 - ADAM_B2) * (g * g)
        g_ref[...] = g
        nm_ref[...] = nm
        nv_ref[...] = nv
        d_ref[...] = -ADAM_LR * ((nm / c1) / (jnp.sqrt(nv / c2) + ADAM_EPS) + ADAM_WD * w_ref[...])

    row = pl.BlockSpec((tr, C), lambda i: (i, 0))
    out = jax.ShapeDtypeStruct((R, C), F32)
    return pl.pallas_call(
        body,
        grid=(R // tr,),
        in_specs=[pl.BlockSpec((N_DEV, tr, C), lambda i: (0, i, 0)), row, row, row],
        out_specs=[row] * 4,
        out_shape=[out] * 4,
        compiler_params=_params(),
        name=name,
    )(parts, w, m, v)


def _col_shards(g):
    R, C8 = g.shape
    return g.reshape(R, N_DEV, C8 // N_DEV).transpose(1, 0, 2)


def _row_shards(g):
    R8, C = g.shape
    return g.reshape(N_DEV, R8 // N_DEV, C)


def _cols_full(gathered):
    n, R, C = gathered.shape
    return gathered.transpose(1, 0, 2).reshape(R, n * C)


_BIG = ("w_in", "w_mem_kv", "w_branch_swa", "w_branch_sb", "w_branch_mem", "w_out", "w_gate", "w_up", "w_down")
_COL_SHARDED = ("w_branch_swa", "w_branch_sb", "w_branch_mem")
_TRANSPOSED = ("w_in", "w_gate", "w_up")
_SMALL = ("ln_mix_pre", "ln_mix_post", "swa_sinks", "rel_bias", "ln_mem", "ln_ffn_pre", "ln_ffn_post")
_ORDER = ("ln_mix_pre", "ln_mix_post", "w_in", "swa_sinks", "rel_bias", "ln_mem", "w_mem_kv", "w_branch_swa",
          "w_branch_sb", "w_branch_mem", "w_out", "ln_ffn_pre", "ln_ffn_post", "w_gate", "w_up", "w_down")


def _pack_small(d, last=None):
    rows = [d["ln_mix_pre"], d["ln_mix_post"], d["ln_mem"], d["ln_ffn_pre"], d["ln_ffn_post"],
            jnp.pad(d["swa_sinks"].reshape(1, -1), ((0, 0), (0, D_MODEL - SWA_Q_HEADS))),
            jnp.pad(d["rel_bias"].reshape(1, -1), ((0, 0), (0, D_MODEL - N_BUCKETS * SWA_Q_HEADS))),
            jnp.zeros((1, D_MODEL), F32) if last is None else last]
    return jnp.concatenate([r.astype(F32) for r in rows], axis=0)


def _unpack_small(a):
    return dict(ln_mix_pre=a[0:1], ln_mix_post=a[1:2], ln_mem=a[2:3], ln_ffn_pre=a[3:4], ln_ffn_post=a[4:5],
                swa_sinks=a[5:6, :SWA_Q_HEADS],
                rel_bias=a[6, :N_BUCKETS * SWA_Q_HEADS].reshape(N_BUCKETS, SWA_Q_HEADS))


def kernel(x, mem, ln_mix_pre, ln_mix_post, w_in, swa_sinks, rel_bias, ln_mem, w_mem_kv, w_branch_swa, w_branch_sb, w_branch_mem, w_out, ln_ffn_pre, ln_ffn_post, w_gate, w_up, w_down, loss_target, m_ln_mix_pre, m_ln_mix_post, m_w_in, m_swa_sinks, m_rel_bias, m_ln_mem, m_w_mem_kv, m_w_branch_swa, m_w_branch_sb, m_w_branch_mem, m_w_out, m_ln_ffn_pre, m_ln_ffn_post, m_w_gate, m_w_up, m_w_down, v_ln_mix_pre, v_ln_mix_post, v_w_in, v_swa_sinks, v_rel_bias, v_ln_mem, v_w_mem_kv, v_w_branch_swa, v_w_branch_sb, v_w_branch_mem, v_w_out, v_ln_ffn_pre, v_ln_ffn_post, v_w_gate, v_w_up, v_w_down):
    w = dict(ln_mix_pre=ln_mix_pre, ln_mix_post=ln_mix_post, w_in=w_in[0], swa_sinks=swa_sinks, rel_bias=rel_bias,
             ln_mem=ln_mem, w_mem_kv=w_mem_kv[0], w_branch_swa=w_branch_swa[0], w_branch_sb=w_branch_sb[0],
             w_branch_mem=w_branch_mem[0], w_out=w_out[0], ln_ffn_pre=ln_ffn_pre, ln_ffn_post=ln_ffn_post,
             w_gate=w_gate[0], w_up=w_up[0], w_down=w_down[0])
    mom = dict(ln_mix_pre=m_ln_mix_pre, ln_mix_post=m_ln_mix_post, w_in=m_w_in[0], swa_sinks=m_swa_sinks,
               rel_bias=m_rel_bias, ln_mem=m_ln_mem, w_mem_kv=m_w_mem_kv[0], w_branch_swa=m_w_branch_swa[0],
               w_branch_sb=m_w_branch_sb[0], w_branch_mem=m_w_branch_mem[0], w_out=m_w_out[0],
               ln_ffn_pre=m_ln_ffn_pre, ln_ffn_post=m_ln_ffn_post, w_gate=m_w_gate[0], w_up=m_w_up[0],
               w_down=m_w_down[0])
    var = dict(ln_mix_pre=v_ln_mix_pre, ln_mix_post=v_ln_mix_post, w_in=v_w_in[0], swa_sinks=v_swa_sinks,
               rel_bias=v_rel_bias, ln_mem=v_ln_mem, w_mem_kv=v_w_mem_kv[0], w_branch_swa=v_w_branch_swa[0],
               w_branch_sb=v_w_branch_sb[0], w_branch_mem=v_w_branch_mem[0], w_out=v_w_out[0],
               ln_ffn_pre=v_ln_ffn_pre, ln_ffn_post=v_ln_ffn_post, w_gate=v_w_gate[0], w_up=v_w_up[0],
               w_down=v_w_down[0])
    B, S, D = x.shape
    M = mem.shape[1]
    T = B * S
    F = D_FF
    x2 = x.reshape(T, D)
    mem2 = mem.reshape(B * M, D)
    t2 = loss_target.reshape(T, D)
    buckets = jnp.asarray(_swa_buckets())
    for d in (w, mom, var):
        for n in _TRANSPOSED:
            d[n] = d[n].T
    wb = {n: w[n].astype(BF16) for n in _BIG}
    full = {}

    def landed(names, got):
        for n, g in zip(names, got):
            full[n] = _cols_full(g) if n in _COL_SHARDED else g.reshape(-1, g.shape[-1])

    def shards(n, g):
        return _col_shards(g) if n in _COL_SHARDED else _row_shards(g)

    landed(["w_in"], [_all_gather_relayed(wb["w_in"], "ag_w_in")])
    u = _rms_fwd(x2, ln_mix_pre, "rms_mix_pre")
    early = ["w_mem_kv", "w_branch_swa", "w_branch_sb", "w_branch_mem"]
    proj, got = _matmul([(u, full["w_in"])], "nt", BF16, 512, IN_W // 2, D, "proj_in",
                        hosted=_Hosted(gathers=[wb[n] for n in early]))
    landed(early, got)
    mn = _rms_fwd(mem2, ln_mem, "rms_mem")
    mkv = _matmul([(mn, full["w_mem_kv"])], "nn", BF16, 512, 1024, D, "proj_mem")
    bias_tab = _swa_bias_table(rel_bias, buckets)
    y_swa, got = _swa_fwd(proj, swa_sinks, bias_tab, B, S, hosted=_Hosted(gathers=[wb["w_out"]]))
    landed(["w_out"], got)
    late = ["w_gate", "w_up"]
    y_sb, rtot, got = _sb_fwd(proj, B, S, hosted=_Hosted(gathers=[wb[n] for n in late]))
    landed(late, got)
    y_mem = _mem_fwd(proj, mkv, B, S, M)
    wbs = (full["w_branch_swa"], full["w_branch_sb"], full["w_branch_mem"])
    merged, p_swa, p_sb, p_mem = _branch_gate(proj, (y_swa, y_sb, y_mem), wbs)
    mix, h1, u2 = _post_pre(x2, merged, full["w_out"], ln_mix_post, ln_ffn_pre)
    a, zg, zu, got = _ffn_up(u2, full["w_gate"], full["w_up"], hosted=_Hosted(gathers=[wb["w_down"]]))
    landed(["w_down"], got)
    dffn, dh2, loss_tile, d_ln_ffn_post = _loss_head(a, full["w_down"], h1, t2, ln_ffn_post)

    part = {}
    part["w_down"] = _matmul([(a, dffn)], "tn", BF16, F // 2, 1024, 1024, "dw_down")
    dzg, dzu = _ffn_down_bwd(dffn, full["w_down"], zg, zu)
    part["w_gate"] = _matmul([(dzg, u2)], "tn", BF16, F // 2, 1024, 1024, "dw_gate")
    part["w_up"] = _matmul([(dzu, u2)], "tn", BF16, F // 2, 1024, 1024, "dw_up")
    dh1, dmix, d_ln_ffn_pre, d_ln_mix_post = _mid_bwd(dzg, dzu, full["w_gate"], full["w_up"], h1, dh2, mix,
                                                      ln_ffn_pre, ln_mix_post)
    part["w_out"] = _matmul([(merged, dmix)], "tn", BF16, 1024, 1024, 1024, "dw_out")
    (dp_swa, dp_sb, dp_mem, dg0, dg1, dg2, dy_swa, dy_sb, dy_mem) = _gate_bwd(
        dmix, full["w_out"], (p_swa, p_sb, p_mem), proj, wbs)
    part["w_branch_swa"] = _matmul([(y_swa, dp_swa)], "tn", BF16, 512, 1024, 1024, "dw_branch_swa")
    part["w_branch_sb"] = _matmul([(y_sb, dp_sb)], "tn", BF16, 512, 1024, 1024, "dw_branch_sb")
    part["w_branch_mem"] = _matmul([(y_mem, dp_mem)], "tn", BF16, 512, 1024, 1024, "dw_branch_mem")
    dqm, dmk, dmv = _mem_bwd(proj, mkv, dy_mem, B, S, M)
    dmkv = jnp.concatenate([dmk, dmv], axis=1).astype(BF16)
    part["w_mem_kv"] = _matmul([(mn, dmkv)], "tn", BF16, 1024, 1024, 512, "dw_mem_kv")
    dmn = _matmul([(dmkv, full["w_mem_kv"])], "nt", F32, 512, 1024, 1024, "d_mn")
    d_ln_mem = _gain_grad(mem2, dmn)
    behind_swa = ["w_out", "w_branch_swa", "w_branch_sb"]
    dqa, dka, dva, dbias, dsink, got = _swa_bwd(
        proj, dy_swa, swa_sinks, bias_tab, B, S, hosted=_Hosted(scatters=[shards(n, part[n]) for n in behind_swa]))
    recv = dict(zip(behind_swa, got))
    behind_sb = ["w_down", "w_gate", "w_up", "w_branch_mem", "w_mem_kv"]
    dqb, dkb, dvb, got = _sb_bwd(proj, dy_sb, rtot, B, S,
                                 hosted=_Hosted(scatters=[shards(n, part[n]) for n in behind_sb]))
    recv.update(zip(behind_sb, got))
    d_rel_bias = _swa_bias_grad(dbias, buckets)[:, :SWA_Q_HEADS]
    d_sinks = dsink[:, 0].reshape(1, SWA_Q_HEADS)
    dproj = jnp.concatenate([dqa, dka.astype(BF16), dva.astype(BF16), dqb, dkb.astype(BF16), dvb.astype(BF16),
                             dqm, dg0, dg1, dg2], axis=1)
    half = D // 2
    land = lax.empty((N_DEV, IN_W // N_DEV, D), BF16)
    flights = []
    for t in range(2):
        dw_half = _matmul([(dproj, u)], "tn", BF16, IN_W // 2, half, 1024, "dw_in_%d" % t, n_cols=half, n_off=t)
        flight, land, dproj = _scatter_start(_row_shards(dw_half), land, (t * half, half), dproj,
                                             "rs_w_in_start_%d" % t)
        flights.append(flight)
    grad_x, d_ln_mix_pre = _pre_bwd(dproj, full["w_in"], x2, dh1, ln_mix_pre)

    out = {n: _adamw(recv[n], w[n], mom[n], var[n], "adamw_" + n) for n in _BIG if n != "w_in"}
    small_grads = dict(ln_mix_pre=d_ln_mix_pre, ln_mix_post=d_ln_mix_post, swa_sinks=d_sinks, rel_bias=d_rel_bias,
                       ln_mem=d_ln_mem, ln_ffn_pre=d_ln_ffn_pre, ln_ffn_post=d_ln_ffn_post)
    small_parts = _all_gather(_pack_small(small_grads, jnp.tile(loss_tile[0:1], (1, D // 128))), "ag_small")
    res = _adamw(small_parts, _pack_small(w), _pack_small(mom), _pack_small(var), "adamw_small")
    loss = res[0][7, 0]
    small = [_unpack_small(r) for r in res]
    for n in _SMALL:
        out[n] = tuple(s[n] for s in small)
    for t, flight in enumerate(flights):
        land = _scatter_wait(flight, land, res[0], "rs_w_in_wait_%d" % t)
    out["w_in"] = _adamw(land, w["w_in"], mom["w_in"], var["w_in"], "adamw_w_in")
    for n in _TRANSPOSED:
        out[n] = tuple(o.T for o in out[n])

    like = dict(ln_mix_pre=ln_mix_pre, ln_mix_post=ln_mix_post, w_in=w_in, swa_sinks=swa_sinks, rel_bias=rel_bias,
                ln_mem=ln_mem, w_mem_kv=w_mem_kv, w_branch_swa=w_branch_swa, w_branch_sb=w_branch_sb,
                w_branch_mem=w_branch_mem, w_out=w_out, ln_ffn_pre=ln_ffn_pre, ln_ffn_post=ln_ffn_post,
                w_gate=w_gate, w_up=w_up, w_down=w_down)
    result = [loss, grad_x.reshape(B, S, D)]
    for k in range(4):
        result += [out[n][k].reshape(like[n].shape) for n in _ORDER]
    return tuple(result)
```

```python
import functools
import math

import numpy as np
import jax
import jax.numpy as jnp
from jax import lax
from jax.experimental import pallas as pl
from jax.experimental.pallas import tpu as pltpu

F32 = jnp.float32
BF16 = jnp.bfloat16

N_DEV = 8
D_MODEL = 1024
BLOCK = 128
EPS = 1e-6
HEAD_DIM = 64
SWA_Q_HEADS = 8
SWA_WINDOW = 128
N_BUCKETS = 32
MAX_DISTANCE = 128
MEM_HEADS = 4
MEM_HEAD_DIM = 128
D_FF = 2816
IN_W = 5888
COL_QA, COL_KA, COL_VA, COL_QB, COL_KB, COL_VB, COL_QM, COL_GL = 0, 4, 5, 6, 10, 14, 18, 22
SCALE64 = HEAD_DIM ** -0.5
SCALE128 = MEM_HEAD_DIM ** -0.5
NEG = -1e30

ADAM_LR = 0.001
ADAM_B1 = 0.9
ADAM_B2 = 0.999
ADAM_EPS = 1e-08
ADAM_WD = 0.01
ADAM_STEP = 10

VMEM_LIMIT_BYTES = 56 * 1024 * 1024


def _params(**kw):
    return pltpu.CompilerParams(vmem_limit_bytes=VMEM_LIMIT_BYTES, **kw)


def _dot(a, b):
    return jnp.dot(a, b, preferred_element_type=F32)


def _dot_nt(a, b):
    return lax.dot_general(a, b, (((1,), (1,)), ((), ())), preferred_element_type=F32)


def _dot_tn(a, b):
    return lax.dot_general(a, b, (((0,), (0,)), ((), ())), preferred_element_type=F32)


def _dot_split(x, m2):
    hi = x.astype(BF16)
    lo = (x - hi.astype(F32)).astype(BF16)
    return _dot(jnp.concatenate([hi, lo], axis=1), m2)


def _mesh_pos():
    return lax.axis_index("x"), lax.axis_index("y"), lax.axis_index("c")


class _Hosted:
    def __init__(self, gathers=(), scatters=(), window=None):
        self.items = [("g", a) for a in gathers] + [("s", a) for a in scatters]
        self.n = len(self.items)
        self.window = window

    def operands(self):
        return [a for _, a in self.items]

    def specs(self):
        return [pl.BlockSpec(memory_space=pl.ANY)] * self.n

    def out_shapes(self):
        return [jax.ShapeDtypeStruct(((N_DEV,) + a.shape) if kind == "g" else a.shape, a.dtype)
                for kind, a in self.items]

    def scratch(self):
        return [pltpu.SemaphoreType.DMA((7 * self.n,)), pltpu.SemaphoreType.DMA((7 * self.n,)),
                pltpu.SemaphoreType.DMA((self.n,))]

    def copies(self, in_refs, out_refs, send_sems, recv_sems, local_sems):
        x, y, c = _mesh_pos()
        me = 4 * x + 2 * y + c
        out = []
        for t, (kind, _) in enumerate(self.items):
            own = in_refs[t] if kind == "g" else in_refs[t].at[me]
            dst = out_refs[t].at[me]
            if self.window is not None:
                dst = out_refs[t].at[me, :, pl.ds(*self.window)]
            out.append(pltpu.make_async_copy(own, dst, local_sems.at[t]))
            for k in range(1, N_DEV):
                px, py, pc = x ^ (k >> 2), y ^ ((k >> 1) & 1), c ^ (k & 1)
                src = in_refs[t] if kind == "g" else in_refs[t].at[4 * px + 2 * py + pc]
                out.append(pltpu.make_async_remote_copy(
                    src_ref=src, dst_ref=dst,
                    send_sem=send_sems.at[7 * t + k - 1], recv_sem=recv_sems.at[7 * t + k - 1],
                    device_id=(px, py, pc), device_id_type=pl.DeviceIdType.MESH))
        return out


def _host(body, n_in, n_out, hosted, grid):
    if hosted is None:
        return body
    nc = hosted.n

    def wrapped(*refs):
        ins = refs[:n_in]
        cin = refs[n_in:n_in + nc]
        outs = refs[n_in + nc:n_in + nc + n_out]
        cout = refs[n_in + nc + n_out:n_in + 2 * nc + n_out]
        scratch = refs[n_in + 2 * nc + n_out:len(refs) - 3]
        sems = refs[len(refs) - 3:]
        ids = [pl.program_id(d) for d in range(len(grid))]
        first = functools.reduce(lambda a, b: a & b, [i == 0 for i in ids])
        last = functools.reduce(lambda a, b: a & b, [i == g - 1 for i, g in zip(ids, grid)])

        @pl.when(first)
        def _():
            for cp in hosted.copies(cin, cout, *sems):
                cp.start()

        body(*ins, *outs, *scratch)

        @pl.when(last)
        def _():
            for cp in hosted.copies(cin, cout, *sems):
                cp.wait()

    return wrapped


def _hosted_call(body, grid, in_specs, out_specs, out_shape, scratch_shapes, hosted, name, args):
    n_out = len(out_specs)
    if hosted is None:
        outs = pl.pallas_call(body, grid=grid, in_specs=in_specs, out_specs=out_specs, out_shape=out_shape,
                              scratch_shapes=scratch_shapes, compiler_params=_params(), name=name)(*args)
        return list(outs), []
    outs = pl.pallas_call(
        _host(body, len(in_specs), n_out, hosted, grid),
        grid=grid,
        in_specs=list(in_specs) + hosted.specs(),
        out_specs=list(out_specs) + hosted.specs(),
        out_shape=list(out_shape) + hosted.out_shapes(),
        scratch_shapes=list(scratch_shapes) + hosted.scratch(),
        compiler_params=_params(),
        name=name,
    )(*args, *hosted.operands())
    return list(outs[:n_out]), list(outs[n_out:])


_DIMS = {"nn": (((1,), (0,)), ((), ())), "nt": (((1,), (1,)), ((), ())), "tn": (((0,), (0,)), ((), ()))}


def _matmul(pairs, mode, out_dtype, tm, tn, tk, name, hosted=None, n_cols=None, n_off=0):
    a0, b0 = pairs[0]
    if mode == "nn":
        (M, K), N = a0.shape, b0.shape[1]
    elif mode == "nt":
        (M, K), N = a0.shape, b0.shape[0]
    else:
        (K, M), N = a0.shape, b0.shape[1]
    N = N if n_cols is None else n_cols
    tm, tn, tk = min(tm, M), min(tn, N), min(tk, K)
    assert M % tm == 0 and N % tn == 0 and K % tk == 0, (name, M, N, K, tm, tn, tk)
    nm, nn, nk = M // tm, N // tn, K // tk
    npair = len(pairs)
    dims = _DIMS[mode]

    def body(*refs):
        ab = refs[:2 * npair]
        o_ref = refs[2 * npair]
        acc_ref = refs[2 * npair + 1]
        k = pl.program_id(2)
        part = lax.dot_general(ab[0][...], ab[1][...], dims, preferred_element_type=F32)
        for q in range(1, npair):
            part += lax.dot_general(ab[2 * q][...], ab[2 * q + 1][...], dims, preferred_element_type=F32)
        if nk == 1:
            o_ref[...] = part.astype(o_ref.dtype)
        else:
            @pl.when(k == 0)
            def _():
                acc_ref[...] = part

            @pl.when(k > 0)
            def _():
                acc_ref[...] += part

            @pl.when(k == nk - 1)
            def _():
                o_ref[...] = acc_ref[...].astype(o_ref.dtype)

    if mode == "nn":
        a_spec = pl.BlockSpec((tm, tk), lambda n, m, k: (m, k))
        b_spec = pl.BlockSpec((tk, tn), lambda n, m, k: (k, n + n_off))
    elif mode == "nt":
        a_spec = pl.BlockSpec((tm, tk), lambda n, m, k: (m, k))
        b_spec = pl.BlockSpec((tn, tk), lambda n, m, k: (n + n_off, k))
    else:
        a_spec = pl.BlockSpec((tk, tm), lambda n, m, k: (k, m))
        b_spec = pl.BlockSpec((tk, tn), lambda n, m, k: (k, n + n_off))
    args = [t for pr in pairs for t in pr]
    outs, moved = _hosted_call(
        body, (nn, nm, nk), [a_spec, b_spec] * npair, [pl.BlockSpec((tm, tn), lambda n, m, k: (m, n))],
        [jax.ShapeDtypeStruct((M, N), out_dtype)], [pltpu.VMEM((tm, tn) if nk > 1 else (8, 128), F32)],
        hosted, name, args)
    return outs[0] if hosted is None else (outs[0], moved)


def _rms_fwd(x, g, name):
    T, D = x.shape
    tr = min(512, T)

    def body(x_ref, g_ref, u_ref):
        xf = x_ref[...]
        r = lax.rsqrt(jnp.mean(xf * xf, axis=-1, keepdims=True) + EPS)
        u_ref[...] = ((xf * r) * g_ref[...]).astype(u_ref.dtype)

    return pl.pallas_call(
        body,
        grid=(T // tr,),
        in_specs=[pl.BlockSpec((tr, D), lambda i: (i, 0)), pl.BlockSpec((1, D), lambda i: (0, 0))],
        out_specs=pl.BlockSpec((tr, D), lambda i: (i, 0)),
        out_shape=jax.ShapeDtypeStruct((T, D), BF16),
        name=name,
    )(x, g)


def _rms_bwd_terms(xin, g, dy):
    r = lax.rsqrt(jnp.mean(xin * xin, axis=-1, keepdims=True) + EPS)
    xh = xin * r
    dg = jnp.sum(dy * xh, axis=0, keepdims=True)
    dxh = dy * g
    dx = r * (dxh - xh * jnp.mean(dxh * xh, axis=-1, keepdims=True))
    return dx, dg


GATE_TC = 256


def _branch_gate(proj, ys, wbs):
    T = proj.shape[0]
    D = D_MODEL
    tr, tc = min(512, T), GATE_TC
    nc = D // tc
    gl0 = COL_GL * 128 // tc

    def body(*refs):
        y_refs, w_refs = refs[:3], refs[3:6]
        g_refs = refs[6:6 + 3 * nc]
        merged_ref = refs[6 + 3 * nc]
        p_refs = refs[7 + 3 * nc:]
        acc = [jnp.zeros((tr, tc), F32)] * nc
        for j in range(3):
            p = _dot(y_refs[j][...], w_refs[j][...])
            p_refs[j][...] = p.astype(BF16)
            for c in range(nc):
                acc[c] = acc[c] + jax.nn.sigmoid(g_refs[j * nc + c][...].astype(F32)) * p[:, c * tc:(c + 1) * tc]
        merged_ref[...] = jnp.concatenate(acc, axis=1).astype(BF16)

    y_spec = pl.BlockSpec((tr, 512), lambda i: (i, 0))
    w_spec = pl.BlockSpec((512, D), lambda i: (0, 0))
    o_spec = pl.BlockSpec((tr, D), lambda i: (i, 0))
    gl_specs = [pl.BlockSpec((tr, tc), lambda i, b=gl0 + j * nc + c: (i, b)) for j in range(3) for c in range(nc)]
    out = jax.ShapeDtypeStruct((T, D), BF16)
    return pl.pallas_call(
        body,
        grid=(T // tr,),
        in_specs=[y_spec] * 3 + [w_spec] * 3 + gl_specs,
        out_specs=[o_spec] * 4,
        out_shape=[out] * 4,
        compiler_params=_params(),
        name="branch_gate",
    )(*ys, *wbs, *([proj] * (3 * nc)))


def _post_pre(x, merged, w_out, g_post, g_pre):
    T, D = x.shape
    tr = 512

    def body(x_ref, m_ref, w_ref, gp_ref, gq_ref, mix_ref, h1_ref, u2_ref):
        mx = _dot(m_ref[...], w_ref[...])
        mix_ref[...] = mx
        r = lax.rsqrt(jnp.mean(mx * mx, axis=-1, keepdims=True) + EPS)
        h1 = x_ref[...] + (mx * r) * gp_ref[...]
        h1_ref[...] = h1
        r2 = lax.rsqrt(jnp.mean(h1 * h1, axis=-1, keepdims=True) + EPS)
        u2_ref[...] = ((h1 * r2) * gq_ref[...]).astype(u2_ref.dtype)

    row = pl.BlockSpec((tr, D), lambda i: (i, 0))
    vec = pl.BlockSpec((1, D), lambda i: (0, 0))
    f32 = jax.ShapeDtypeStruct((T, D), F32)
    return pl.pallas_call(
        body,
        grid=(T // tr,),
        in_specs=[row, row, pl.BlockSpec((D, D), lambda i: (0, 0)), vec, vec],
        out_specs=[row, row, row],
        out_shape=[f32, f32, jax.ShapeDtypeStruct((T, D), BF16)],
        compiler_params=_params(),
        name="post_pre",
    )(x, merged, w_out, g_post, g_pre)


def _ffn_up(u2, w_gate_t, w_up_t, hosted=None):
    T, D = u2.shape
    F = D_FF
    tm, tn = 512, F // 2

    def body(u_ref, wg_ref, wu_ref, a_ref, zg_ref, zu_ref):
        u = u_ref[...]
        zg = _dot_nt(u, wg_ref[...])
        zu = _dot_nt(u, wu_ref[...])
        a_ref[...] = (zg * jax.nn.sigmoid(zg) * zu).astype(a_ref.dtype)
        zg_ref[...] = zg.astype(zg_ref.dtype)
        zu_ref[...] = zu.astype(zu_ref.dtype)

    o_spec = pl.BlockSpec((tm, tn), lambda n, m: (m, n))
    out = jax.ShapeDtypeStruct((T, F), BF16)
    outs, moved = _hosted_call(
        body, (F // tn, T // tm),
        [pl.BlockSpec((tm, D), lambda n, m: (m, 0)),
         pl.BlockSpec((tn, D), lambda n, m: (n, 0)),
         pl.BlockSpec((tn, D), lambda n, m: (n, 0))],
        [o_spec] * 3, [out] * 3, [], hosted, "ffn_up", (u2, w_gate_t, w_up_t))
    return (*outs, moved)


def _loss_head(a, w_down, h1, target, g_post):
    T, D = h1.shape
    F = a.shape[1]
    tr = 512

    def body(a_ref, w_ref, h1_ref, t_ref, g_ref, dffn_ref, dh2_ref, loss_ref, dg_ref):
        i = pl.program_id(0)
        f = _dot(a_ref[...], w_ref[...])
        g = g_ref[...]
        r = lax.rsqrt(jnp.mean(f * f, axis=-1, keepdims=True) + EPS)
        xh = f * r
        err = (h1_ref[...] + xh * g) - t_ref[...]
        part = 0.5 * jnp.sum(jnp.mean(err * err, axis=-1, keepdims=True), axis=0, keepdims=True)
        dh2 = err * (1.0 / D)
        dh2_ref[...] = dh2
        dgp = jnp.sum(dh2 * xh, axis=0, keepdims=True)
        dxh = dh2 * g
        dffn_ref[...] = (r * (dxh - xh * jnp.mean(dxh * xh, axis=-1, keepdims=True))).astype(dffn_ref.dtype)

        @pl.when(i == 0)
        def _():
            loss_ref[...] = jnp.zeros_like(loss_ref)
            dg_ref[...] = jnp.zeros_like(dg_ref)

        loss_ref[...] += jnp.broadcast_to(part, loss_ref.shape)
        dg_ref[...] += dgp

    row = pl.BlockSpec((tr, D), lambda i: (i, 0))
    vec = pl.BlockSpec((1, D), lambda i: (0, 0))
    return pl.pallas_call(
        body,
        grid=(T // tr,),
        in_specs=[pl.BlockSpec((tr, F), lambda i: (i, 0)), pl.BlockSpec((F, D), lambda i: (0, 0)), row, row, vec],
        out_specs=[row, row, pl.BlockSpec((8, 128), lambda i: (0, 0)), vec],
        out_shape=[jax.ShapeDtypeStruct((T, D), BF16), jax.ShapeDtypeStruct((T, D), F32),
                   jax.ShapeDtypeStruct((8, 128), F32), jax.ShapeDtypeStruct((1, D), F32)],
        compiler_params=_params(),
        name="loss_head",
    )(a, w_down, h1, target, g_post)


def _ffn_down_bwd(dffn, wd, zg, zu):
    T, D = dffn.shape
    F = D_FF
    tm, tn = 512, F // 2

    def body(d_ref, w_ref, zg_ref, zu_ref, dzg_ref, dzu_ref):
        da = _dot_nt(d_ref[...], w_ref[...])
        zg = zg_ref[...].astype(F32)
        zu = zu_ref[...].astype(F32)
        s = jax.nn.sigmoid(zg)
        dzu_ref[...] = (da * (zg * s)).astype(dzu_ref.dtype)
        dzg_ref[...] = (da * zu * (s * (1.0 + zg * (1.0 - s)))).astype(dzg_ref.dtype)

    z_spec = pl.BlockSpec((tm, tn), lambda n, m: (m, n))
    out = jax.ShapeDtypeStruct((T, F), BF16)
    return pl.pallas_call(
        body,
        grid=(F // tn, T // tm),
        in_specs=[pl.BlockSpec((tm, D), lambda n, m: (m, 0)), pl.BlockSpec((tn, D), lambda n, m: (n, 0)),
                  z_spec, z_spec],
        out_specs=[z_spec, z_spec],
        out_shape=[out, out],
        compiler_params=_params(),
        name="ffn_down_bwd",
    )(dffn, wd, zg, zu)


def _matmul_rows(pairs, tm, rows_in, vecs_in, rows_out, n_vec_out, epilogue, name):
    a0, b0 = pairs[0]
    (M, K), N = a0.shape, b0.shape[1]
    tm = min(tm, M)
    assert M % tm == 0, (name, M, tm)
    nm = M // tm
    npair = len(pairs)
    n_in = 2 * npair + len(rows_in) + len(vecs_in)
    n_out = len(rows_out) + n_vec_out

    def body(*refs):
        ab = refs[:2 * npair]
        r_in = refs[2 * npair:2 * npair + len(rows_in)]
        v_in = refs[2 * npair + len(rows_in):n_in]
        r_out = refs[n_in:n_in + len(rows_out)]
        v_out = refs[n_in + len(rows_out):n_in + n_out]
        last_ref, acc_ref = refs[n_in + n_out], refs[n_in + n_out + 1]
        m = pl.program_id(0)

        @pl.when(m == 0)
        def _():
            acc_ref[...] = jnp.zeros_like(acc_ref)
            for v in v_out:
                v[...] = jnp.zeros_like(v)

        prev = acc_ref[...]
        part = _dot(ab[0][...], ab[1][...])
        for q in range(1, npair):
            part += _dot(ab[2 * q][...], ab[2 * q + 1][...])
        acc_ref[...] = part
        last_ref[...] = part
        epilogue(prev, r_in, v_in, r_out, v_out, m > 0)

    def tail(*refs):
        last_ref = refs[0]
        r_in = refs[1:1 + len(rows_in)]
        v_in = refs[1 + len(rows_in):1 + len(rows_in) + len(vecs_in)]
        k0 = 1 + len(rows_in) + len(vecs_in)
        v_old = refs[k0 + len(rows_out):k0 + n_out]
        r_out = refs[k0 + n_out:k0 + n_out + len(rows_out)]
        v_out = refs[k0 + n_out + len(rows_out):]
        for old, new in zip(v_old, v_out):
            new[...] = old[...]
        epilogue(last_ref[...], r_in, v_in, r_out, v_out, True)

    prev_row = pl.BlockSpec((tm, N), lambda m: (jnp.maximum(m - 1, 0), 0))
    vec = pl.BlockSpec((1, N), lambda m: (0, 0))
    once = pl.BlockSpec((tm, N), lambda m: (0, 0))
    outs = pl.pallas_call(
        body,
        grid=(nm,),
        in_specs=[pl.BlockSpec((tm, K), lambda m: (m, 0)),
                  pl.BlockSpec((K, N), lambda m: (0, 0), pipeline_mode=pl.Buffered(1))] * npair
        + [prev_row] * len(rows_in) + [vec] * len(vecs_in),
        out_specs=[prev_row] * len(rows_out) + [vec] * n_vec_out + [once],
        out_shape=[jax.ShapeDtypeStruct((M, N), dt) for dt in rows_out]
        + [jax.ShapeDtypeStruct((1, N), F32)] * n_vec_out + [jax.ShapeDtypeStruct((tm, N), F32)],
        scratch_shapes=[pltpu.VMEM((tm, N), F32)],
        compiler_params=_params(),
        name=name,
    )(*[t for pr in pairs for t in pr], *rows_in, *vecs_in)
    last_row = pl.BlockSpec((tm, N), lambda i: (nm - 1, 0))
    vec1 = pl.BlockSpec((1, N), lambda i: (0, 0))
    anyspec = pl.BlockSpec(memory_space=pl.ANY)
    return pl.pallas_call(
        tail,
        grid=(1,),
        in_specs=[pl.BlockSpec((tm, N), lambda i: (0, 0))] + [last_row] * len(rows_in) + [vec1] * len(vecs_in)
        + [anyspec] * len(rows_out) + [vec1] * n_vec_out,
        out_specs=[last_row] * len(rows_out) + [vec1] * n_vec_out,
        out_shape=[jax.ShapeDtypeStruct((M, N), dt) for dt in rows_out]
        + [jax.ShapeDtypeStruct((1, N), F32)] * n_vec_out,
        input_output_aliases={1 + len(rows_in) + len(vecs_in) + t: t for t in range(len(rows_out))},
        compiler_params=_params(),
        name=name + "_tail",
    )(outs[n_out], *rows_in, *vecs_in, *outs[:n_out])


def _mid_bwd(dzg, dzu, w_gate_t, w_up_t, h1, dh2, mix, g_pre, g_post):
    def epilogue(du2, rows, vecs, outs, accs, valid):
        h1_ref, dh2_ref, mix_ref = rows
        gq_ref, gp_ref = vecs
        dh1_ref, dmix_ref = outs
        keep = jnp.where(valid, 1.0, 0.0)
        dx, dgq = _rms_bwd_terms(h1_ref[...], gq_ref[...], du2)
        dh1 = dh2_ref[...] + dx
        dh1_ref[...] = dh1
        dmix, dgp = _rms_bwd_terms(mix_ref[...], gp_ref[...], dh1)
        dmix_ref[...] = dmix.astype(dmix_ref.dtype)
        accs[0][...] += dgq * keep
        accs[1][...] += dgp * keep

    return _matmul_rows([(dzg, w_gate_t), (dzu, w_up_t)], 512, [h1, dh2, mix], [g_pre, g_post],
                        [F32, BF16], 2, epilogue, "mid_bwd")


def _gate_bwd(dmix, w_out, ps, proj, wbs):
    T, D = dmix.shape
    tr, tc = min(512, T), GATE_TC
    nc = D // tc
    gl0 = COL_GL * 128 // tc

    def body(*refs):
        dmix_ref, wo_ref = refs[:2]
        p_refs = refs[2:5]
        g_refs = refs[5:5 + 3 * nc]
        w_refs = refs[5 + 3 * nc:8 + 3 * nc]
        outs = refs[8 + 3 * nc:]
        dp_refs, dg_refs, dy_refs = outs[:3], outs[3:6], outs[6:9]
        dm = _dot_nt(dmix_ref[...], wo_ref[...])
        for j in range(3):
            dps = []
            for c in range(nc):
                cols = slice(c * tc, (c + 1) * tc)
                s = jax.nn.sigmoid(g_refs[j * nc + c][...].astype(F32))
                dmc = dm[:, cols]
                dps.append((dmc * s).astype(BF16))
                dg_refs[j][:, cols] = (dmc * p_refs[j][:, cols].astype(F32) * (s * (1.0 - s))).astype(BF16)
            dp = jnp.concatenate(dps, axis=1)
            dp_refs[j][...] = dp
            dy_refs[j][...] = _dot_nt(dp, w_refs[j][...]).astype(BF16)

    row = pl.BlockSpec((tr, D), lambda i: (i, 0))
    y_spec = pl.BlockSpec((tr, 512), lambda i: (i, 0))
    w_spec = pl.BlockSpec((512, D), lambda i: (0, 0))
    gl_specs = [pl.BlockSpec((tr, tc), lambda i, b=gl0 + j * nc + c: (i, b)) for j in range(3) for c in range(nc)]
    big = jax.ShapeDtypeStruct((T, D), BF16)
    small = jax.ShapeDtypeStruct((T, 512), BF16)
    return pl.pallas_call(
        body,
        grid=(T // tr,),
        in_specs=[row, pl.BlockSpec((D, D), lambda i: (0, 0))] + [row] * 3 + gl_specs + [w_spec] * 3,
        out_specs=[row] * 6 + [y_spec] * 3,
        out_shape=[big] * 6 + [small] * 3,
        compiler_params=_params(),
        name="gate_bwd",
    )(dmix, w_out, *ps, *([proj] * (3 * nc)), *wbs)


def _pre_bwd(dproj, w_in_t, x, dh1, g):
    def epilogue(du, rows, vecs, outs, accs, valid):
        x_ref, dh1_ref = rows
        dx, dg = _rms_bwd_terms(x_ref[...], vecs[0][...], du)
        outs[0][...] = dh1_ref[...] + dx
        accs[0][...] += dg * jnp.where(valid, 1.0, 0.0)

    return _matmul_rows([(dproj, w_in_t)], 512, [x, dh1], [g], [F32], 1, epilogue, "pre_bwd")


def _gain_grad(xin, dy):
    T, D = xin.shape
    tr = min(512, T)

    def body(x_ref, dy_ref, dg_ref):
        i = pl.program_id(0)
        xf = x_ref[...]
        r = lax.rsqrt(jnp.mean(xf * xf, axis=-1, keepdims=True) + EPS)

        @pl.when(i == 0)
        def _():
            dg_ref[...] = jnp.zeros_like(dg_ref)

        dg_ref[...] += jnp.sum(dy_ref[...] * (xf * r), axis=0, keepdims=True)

    row = pl.BlockSpec((tr, D), lambda i: (i, 0))
    return pl.pallas_call(
        body,
        grid=(T // tr,),
        in_specs=[row, row],
        out_specs=pl.BlockSpec((1, D), lambda i: (0, 0)),
        out_shape=jax.ShapeDtypeStruct((1, D), F32),
        name="gain_grad",
    )(xin, dy)


def _swa_buckets():
    dist = (np.arange(BLOCK)[:, None] + BLOCK) - np.arange(2 * BLOCK)[None, :]
    max_exact = N_BUCKETS // 2
    d = np.maximum(dist, 0)
    df = np.maximum(d, 1).astype(np.float32)
    large = max_exact + (np.log(df / np.float32(max_exact)) / np.float32(math.log(MAX_DISTANCE / max_exact))
                         * np.float32(N_BUCKETS - max_exact)).astype(np.int32)
    large = np.minimum(large, N_BUCKETS - 1)
    bucket = np.where(d < max_exact, d, large)
    in_win = (dist >= 0) & (dist < SWA_WINDOW)
    return np.where(in_win, bucket, -1).astype(np.int32)


def _swa_bias_table(rel_bias, buckets):
    H = SWA_Q_HEADS

    def body(rb_ref, bk_ref, o_ref):
        bk = bk_ref[...]
        for h in range(H):
            acc = jnp.full(bk.shape, NEG, F32)
            for b in range(N_BUCKETS):
                acc = jnp.where(bk == b, rb_ref[b, h], acc)
            o_ref[h] = acc

    return pl.pallas_call(
        body,
        in_specs=[pl.BlockSpec(memory_space=pltpu.SMEM), pl.BlockSpec(memory_space=pltpu.VMEM)],
        out_specs=pl.BlockSpec(memory_space=pltpu.VMEM),
        out_shape=jax.ShapeDtypeStruct((H, BLOCK, 2 * BLOCK), F32),
        name="swa_bias_table",
    )(rel_bias, buckets)


def _swa_bias_grad(dbias, buckets):
    H = SWA_Q_HEADS

    def body(db_ref, bk_ref, o_ref):
        bk = bk_ref[...]
        rows = lax.broadcasted_iota(jnp.int32, (N_BUCKETS, 128), 0)
        lanes = lax.broadcasted_iota(jnp.int32, (N_BUCKETS, 128), 1)
        acc = jnp.zeros((N_BUCKETS, 128), F32)
        for h in range(H):
            d = db_ref[h]
            for b in range(N_BUCKETS):
                s = jnp.sum(jnp.sum(jnp.where(bk == b, d, 0.0), axis=1, keepdims=True), axis=0, keepdims=True)
                acc = jnp.where((rows == b) & (lanes == h), s, acc)
        o_ref[...] = acc

    return pl.pallas_call(
        body,
        in_specs=[pl.BlockSpec(memory_space=pltpu.VMEM)] * 2,
        out_specs=pl.BlockSpec(memory_space=pltpu.VMEM),
        out_shape=jax.ShapeDtypeStruct((N_BUCKETS, 128), F32),
        name="swa_bias_grad",
    )(dbias, buckets)


SWA_GROUP = 4
SWA_ROWS = SWA_Q_HEADS * BLOCK


def _swa_kv_lanes(h):
    lane = lax.broadcasted_iota(jnp.int32, (BLOCK, BLOCK), 1)
    return (lane // HEAD_DIM) == h // SWA_GROUP


def _swa_stack(x, heads):
    blocks = []
    for h in heads:
        xp = x[:, (h // 2) * BLOCK:(h // 2 + 1) * BLOCK]
        xs = xp if h % 2 == h // SWA_GROUP else pltpu.roll(xp, HEAD_DIM, 1)
        blocks.append(jnp.where(_swa_kv_lanes(h), xs, 0.0))
    return jnp.concatenate(blocks, axis=0).astype(BF16)


def _swa_unstack(blocks):
    pairs = []
    for p in range(4):
        halves = []
        for hh in range(2):
            h = 2 * p + hh
            blk = jnp.where(_swa_kv_lanes(h), blocks[h], 0.0)
            halves.append(blk if hh == h // SWA_GROUP else pltpu.roll(blk, HEAD_DIM, 1))
        pairs.append(halves[0] + halves[1])
    return jnp.concatenate(pairs, axis=1)


def _swa_stacked_params(sink_ref, bias_ref, heads):
    bias = jnp.concatenate([bias_ref[h] for h in heads], axis=0)
    sink = jnp.concatenate([jnp.full((BLOCK, 1), sink_ref[0, h], F32) for h in heads], axis=0)
    return bias, sink


def _swa_scores(qs, kp, kc, bias, sink, first):
    sp = _dot_nt(qs, kp) * SCALE64 + bias[:, :BLOCK]
    sp = jnp.where(first, NEG, sp)
    sc = _dot_nt(qs, kc) * SCALE64 + bias[:, BLOCK:]
    m = jnp.maximum(jnp.maximum(jnp.max(sp, axis=1, keepdims=True), jnp.max(sc, axis=1, keepdims=True)), sink)
    pp = jnp.exp(sp - m)
    pc = jnp.exp(sc - m)
    ps = jnp.exp(sink - m)
    den = jnp.sum(pp, axis=1, keepdims=True) + jnp.sum(pc, axis=1, keepdims=True) + ps
    return pp / den, pc / den, ps / den


def _swa_fwd(proj, sinks, bias_tab, B, S, hosted=None):
    nb = S // BLOCK
    T = B * S

    def body(sink_ref, q_ref, kp_ref, kc_ref, vp_ref, vc_ref, bias_ref, o_ref):
        i = pl.program_id(1)
        first = jnp.full((SWA_GROUP * BLOCK, BLOCK), i, jnp.int32) == 0
        q = q_ref[...].astype(F32)
        blocks = []
        for g in range(SWA_Q_HEADS // SWA_GROUP):
            heads = range(SWA_GROUP * g, SWA_GROUP * (g + 1))
            bias, sink = _swa_stacked_params(sink_ref, bias_ref, heads)
            wp, wc, _ = _swa_scores(_swa_stack(q, heads), kp_ref[...], kc_ref[...], bias, sink, first)
            o = _dot(wp.astype(BF16), vp_ref[...]) + _dot(wc.astype(BF16), vc_ref[...])
            blocks += [o[t * BLOCK:(t + 1) * BLOCK] for t in range(SWA_GROUP)]
        o_ref[...] = _swa_unstack(blocks).astype(o_ref.dtype)

    blk = (BLOCK, BLOCK)
    wide = (BLOCK, 4 * BLOCK)
    outs, moved = _hosted_call(
        body, (B, nb),
        [pl.BlockSpec(memory_space=pltpu.SMEM),
         pl.BlockSpec(wide, lambda b, i: (b * nb + i, COL_QA // 4)),
         pl.BlockSpec(blk, lambda b, i: (b * nb + jnp.maximum(i - 1, 0), COL_KA)),
         pl.BlockSpec(blk, lambda b, i: (b * nb + i, COL_KA)),
         pl.BlockSpec(blk, lambda b, i: (b * nb + jnp.maximum(i - 1, 0), COL_VA)),
         pl.BlockSpec(blk, lambda b, i: (b * nb + i, COL_VA)),
         pl.BlockSpec((SWA_Q_HEADS, BLOCK, 2 * BLOCK), lambda b, i: (0, 0, 0))],
        [pl.BlockSpec(wide, lambda b, i: (b * nb + i, 0))],
        [jax.ShapeDtypeStruct((T, 512), BF16)], [], hosted, "swa_fwd",
        (sinks, proj, proj, proj, proj, proj, bias_tab))
    return outs[0], moved


def _swa_bwd(proj, dy, sinks, bias_tab, B, S, hosted=None):
    nb = S // BLOCK
    T = B * S
    H = SWA_Q_HEADS

    def body(sink_ref, q_ref, kp_ref, kc_ref, vp_ref, vc_ref, do_ref, bias_ref,
             dq_ref, dk_ref, dv_ref, dbias_ref, dsink_ref):
        b = pl.program_id(0)
        i = pl.program_id(1)

        @pl.when((b == 0) & (i == 0))
        def _():
            dbias_ref[...] = jnp.zeros_like(dbias_ref)
            dsink_ref[...] = jnp.zeros_like(dsink_ref)

        @pl.when(i == 0)
        def _():
            dk_ref[...] = jnp.zeros_like(dk_ref)
            dv_ref[...] = jnp.zeros_like(dv_ref)

        first = jnp.full((SWA_ROWS, BLOCK), i, jnp.int32) == 0
        heads = range(H)
        bias, sink = _swa_stacked_params(sink_ref, bias_ref, heads)
        kp, kc, vp, vc = kp_ref[...], kc_ref[...], vp_ref[...], vc_ref[...]
        qs = _swa_stack(q_ref[...].astype(F32), heads)
        dos = _swa_stack(do_ref[...].astype(F32), heads)
        wp, wc, ws = _swa_scores(qs, kp, kc, bias, sink, first)
        dwp = _dot_nt(dos, vp)
        dwc = _dot_nt(dos, vc)
        dsum = jnp.sum(wp * dwp, axis=1, keepdims=True) + jnp.sum(wc * dwc, axis=1, keepdims=True)
        dsp = wp * (dwp - dsum)
        dsc = wc * (dwc - dsum)
        dsk = -ws * dsum
        dsinks = []
        for h in range(H):
            rows = slice(h * BLOCK, (h + 1) * BLOCK)
            dsinks.append(jnp.broadcast_to(jnp.sum(dsk[rows], axis=0, keepdims=True), (1, 128)))
            dbias_ref[h] += jnp.concatenate([dsp[rows], dsc[rows]], axis=1)
        dsink_ref[...] += jnp.concatenate(dsinks, axis=0)
        dspb = dsp.astype(BF16)
        dscb = dsc.astype(BF16)
        dq = _dot(dspb, kp) + _dot(dscb, kc)
        dq_ref[...] = (_swa_unstack([dq[h * BLOCK:(h + 1) * BLOCK] for h in heads]) * SCALE64).astype(dq_ref.dtype)
        cur = pl.ds(pl.multiple_of(i * BLOCK, BLOCK), BLOCK)
        prev = pl.ds(pl.multiple_of(jnp.maximum(i - 1, 0) * BLOCK, BLOCK), BLOCK)
        dk_ref[prev, :] += _dot_tn(dspb, qs) * SCALE64
        dk_ref[cur, :] += _dot_tn(dscb, qs) * SCALE64
        dv_ref[prev, :] += _dot_tn(wp.astype(BF16), dos)
        dv_ref[cur, :] += _dot_tn(wc.astype(BF16), dos)

    blk = (BLOCK, BLOCK)
    wide = (BLOCK, 4 * BLOCK)
    kv_out = pl.BlockSpec((S, BLOCK), lambda b, i: (b, 0))
    full_bias = pl.BlockSpec((H, BLOCK, 2 * BLOCK), lambda b, i: (0, 0, 0))
    outs, moved = _hosted_call(
        body, (B, nb),
        [pl.BlockSpec(memory_space=pltpu.SMEM),
         pl.BlockSpec(wide, lambda b, i: (b * nb + i, COL_QA // 4)),
         pl.BlockSpec(blk, lambda b, i: (b * nb + jnp.maximum(i - 1, 0), COL_KA)),
         pl.BlockSpec(blk, lambda b, i: (b * nb + i, COL_KA)),
         pl.BlockSpec(blk, lambda b, i: (b * nb + jnp.maximum(i - 1, 0), COL_VA)),
         pl.BlockSpec(blk, lambda b, i: (b * nb + i, COL_VA)),
         pl.BlockSpec(wide, lambda b, i: (b * nb + i, 0)),
         full_bias],
        [pl.BlockSpec(wide, lambda b, i: (b * nb + i, 0)),
         kv_out, kv_out, full_bias,
         pl.BlockSpec((H, 128), lambda b, i: (0, 0))],
        [jax.ShapeDtypeStruct((T, 512), BF16),
         jax.ShapeDtypeStruct((T, BLOCK), F32), jax.ShapeDtypeStruct((T, BLOCK), F32),
         jax.ShapeDtypeStruct((H, BLOCK, 2 * BLOCK), F32), jax.ShapeDtypeStruct((H, 128), F32)],
        [], hosted, "swa_bwd", (sinks, proj, proj, proj, proj, proj, dy, bias_tab))
    return (*outs, moved)


SB_TILE = 256


SB_HEADS = 4
SB_LANES = SB_HEADS * HEAD_DIM
SB_ROWS = SB_HEADS * SB_TILE


def _sb_logits(z, tri):
    sp = jnp.log(1.0 + jnp.exp(-jnp.abs(z)))
    ls = jnp.minimum(z, 0.0) - sp
    l1m = ls - z
    if tri is not None:
        l1m = jnp.where(tri, l1m, 0.0)
    return ls, l1m


def _sb_masks():
    lane = lax.broadcasted_iota(jnp.int32, (SB_TILE, SB_LANES), 1)
    hm = [(lane // HEAD_DIM) == h for h in range(SB_HEADS)]
    row = lax.broadcasted_iota(jnp.int32, (SB_ROWS, SB_TILE), 0) % SB_TILE
    col = lax.broadcasted_iota(jnp.int32, (SB_ROWS, SB_TILE), 1)
    return lane, hm, row, col


def _sb_stack(x, hm):
    return jnp.concatenate([jnp.where(m, x, 0) for m in hm], axis=0)


def _sb_unstack(x, hm):
    return sum(jnp.where(m, x[h * SB_TILE:(h + 1) * SB_TILE], 0.0) for h, m in enumerate(hm))


def _sb_fwd(proj, B, S, hosted=None):
    nt = S // SB_TILE
    T = B * S

    ng = 512 // SB_LANES

    def body(*refs):
        q_refs, k_refs, v_refs = refs[:ng], refs[ng:2 * ng], refs[2 * ng:3 * ng]
        o_ref, r_ref = refs[3 * ng:]
        i = pl.program_id(1)
        _, hm, row, col = _sb_masks()
        tri = col < row
        later = jnp.concatenate([(row[:SB_TILE] > col[:SB_TILE]).astype(BF16)] * 2, axis=0)
        qs = [_sb_stack(q_ref[...] * SCALE64, hm) for q_ref in q_refs]

        def tile(j, carry, mask):
            rows = pl.ds(pl.multiple_of(j * SB_TILE, SB_TILE), SB_TILE)
            out = []
            for g in range(ng):
                acc, c = carry[g]
                ls, l1m = _sb_logits(_dot_nt(qs[g], k_refs[g][rows, :]), mask)
                a = jnp.exp(ls + c + _dot_split(l1m, later))
                if mask is not None:
                    a = jnp.where(mask, a, 0.0)
                pv = _dot(a.astype(BF16), v_refs[g][rows, :])
                out.append((acc + _sb_unstack(pv, hm), c + jnp.sum(l1m, axis=1, keepdims=True)))
            return tuple(out)

        zero = (jnp.zeros((SB_TILE, SB_LANES), F32), jnp.zeros((SB_ROWS, 1), F32))
        carry = tile(i, (zero,) * ng, tri)
        carry = lax.fori_loop(0, i, lambda it, cr: tile(i - 1 - it, cr, None), carry)
        o_ref[...] = jnp.concatenate([acc for acc, _ in carry], axis=1).astype(o_ref.dtype)
        r_ref[...] = jnp.concatenate(
            [_sb_unstack(jnp.broadcast_to(c, (SB_ROWS, SB_LANES)), hm) for _, c in carry], axis=1)

    blk = (SB_TILE, SB_LANES)
    cq, ck, cv = (c * BLOCK // SB_LANES for c in (COL_QB, COL_KB, COL_VB))
    wide = pl.BlockSpec((SB_TILE, 512), lambda b, i: (b * nt + i, 0))
    outs, moved = _hosted_call(
        body, (B, nt),
        [pl.BlockSpec(blk, lambda b, i, g=g: (b * nt + i, cq + g)) for g in range(ng)]
        + [pl.BlockSpec((S, SB_LANES), lambda b, i, g=g: (b, ck + g)) for g in range(ng)]
        + [pl.BlockSpec((S, SB_LANES), lambda b, i, g=g: (b, cv + g)) for g in range(ng)],
        [wide, wide],
        [jax.ShapeDtypeStruct((T, 512), BF16), jax.ShapeDtypeStruct((T, 512), F32)], [], hosted, "sb_fwd",
        (proj,) * (3 * ng))
    return outs[0], outs[1], moved


def _sb_bwd(proj, dy, rtot, B, S, hosted=None):
    nt = S // SB_TILE
    T = B * S

    ng = 512 // SB_LANES

    def body(*refs):
        q_refs, k_refs, v_refs = refs[:ng], refs[ng:2 * ng], refs[2 * ng:3 * ng]
        do_ref, r_ref, dq_ref, dk_ref, dv_ref = refs[3 * ng:]
        i = pl.program_id(1)

        @pl.when(i == 0)
        def _():
            dk_ref[...] = jnp.zeros_like(dk_ref)
            dv_ref[...] = jnp.zeros_like(dv_ref)

        lane, hm, row, col = _sb_masks()
        tri = col < row
        later = jnp.concatenate([(row[:SB_TILE] > col[:SB_TILE]).astype(BF16)] * 2, axis=0)
        earlier = (row[:SB_TILE] < col[:SB_TILE]).astype(BF16)
        qs, dos, rs = [], [], []
        for g in range(ng):
            lanes = slice(g * SB_LANES, (g + 1) * SB_LANES)
            qs.append(_sb_stack(q_refs[g][...] * SCALE64, hm))
            dos.append(_sb_stack(do_ref[:, lanes], hm))
            r = r_ref[:, lanes]
            rs.append(jnp.concatenate([jnp.sum(jnp.where(lane == h * HEAD_DIM, r, 0.0), axis=1, keepdims=True)
                                       for h in range(SB_HEADS)], axis=0))

        def tile(j, carry, mask):
            rows = pl.ds(pl.multiple_of(j * SB_TILE, SB_TILE), SB_TILE)
            out = []
            for g in range(ng):
                lanes = slice(g * SB_LANES, (g + 1) * SB_LANES)
                dq, lsum, psum = carry[g]
                kj = k_refs[g][rows, :]
                vj = v_refs[g][rows, :]
                ls, l1m = _sb_logits(_dot_nt(qs[g], kj), mask)
                sig = jnp.exp(ls)
                lsum = lsum + jnp.sum(l1m, axis=1, keepdims=True)
                a = jnp.exp(ls + (rs[g] - lsum) + _dot_split(l1m, later))
                if mask is not None:
                    a = jnp.where(mask, a, 0.0)
                de = _dot_nt(dos[g], vj) * a
                pre = psum + _dot(de.astype(BF16), earlier)
                dz = de - sig * (de + pre)
                if mask is not None:
                    dz = jnp.where(mask, dz, 0.0)
                dz = dz.astype(BF16)
                dk_ref[rows, lanes] += _dot_tn(dz, qs[g])
                dv_ref[rows, lanes] += _dot_tn(a.astype(BF16), dos[g])
                out.append((dq + _sb_unstack(_dot(dz, kj), hm), lsum, psum + jnp.sum(de, axis=1, keepdims=True)))
            return tuple(out)

        zero = jnp.zeros((SB_ROWS, 1), F32)
        init = ((jnp.zeros((SB_TILE, SB_LANES), F32), zero, zero),) * ng
        carry = lax.fori_loop(0, i, lambda j, c: tile(j, c, None), init)
        carry = tile(i, carry, tri)
        dq_ref[...] = (jnp.concatenate([c[0] for c in carry], axis=1) * SCALE64).astype(dq_ref.dtype)

    blk = (SB_TILE, SB_LANES)
    cq, ck, cv = (c * BLOCK // SB_LANES for c in (COL_QB, COL_KB, COL_VB))
    wide = pl.BlockSpec((SB_TILE, 512), lambda b, i: (b * nt + i, 0))
    kv_out = pl.BlockSpec((S, 512), lambda b, i: (b, 0))
    outs, moved = _hosted_call(
        body, (B, nt),
        [pl.BlockSpec(blk, lambda b, i, g=g: (b * nt + i, cq + g)) for g in range(ng)]
        + [pl.BlockSpec((S, SB_LANES), lambda b, i, g=g: (b, ck + g)) for g in range(ng)]
        + [pl.BlockSpec((S, SB_LANES), lambda b, i, g=g: (b, cv + g)) for g in range(ng)]
        + [wide, wide],
        [wide, kv_out, kv_out],
        [jax.ShapeDtypeStruct((T, 512), BF16),
         jax.ShapeDtypeStruct((T, 512), F32), jax.ShapeDtypeStruct((T, 512), F32)], [], hosted, "sb_bwd",
        (proj,) * (3 * ng) + (dy, rtot))
    return outs[0], outs[1], outs[2], moved


def _mem_weights(q, mk):
    z = _dot_nt(q, mk) * SCALE128
    e = jnp.exp(z - jnp.max(z, axis=1, keepdims=True))
    return e / jnp.sum(e, axis=1, keepdims=True)


def _mem_fwd(proj, mkv, B, S, M):
    tq = 512
    nq = S // tq
    T = B * S
    Hm = MEM_HEADS

    def body(q0, q1, q2, q3, mk_ref, mv_ref, o_ref):
        outs = []
        for h, q_ref in enumerate((q0, q1, q2, q3)):
            cols = slice(h * 128, (h + 1) * 128)
            w = _mem_weights(q_ref[...], mk_ref[:, cols])
            outs.append(_dot(w.astype(BF16), mv_ref[:, cols]))
        o_ref[...] = jnp.concatenate(outs, axis=1).astype(o_ref.dtype)

    return pl.pallas_call(
        body,
        grid=(B, nq),
        in_specs=[pl.BlockSpec((tq, 128), lambda b, i, h=h: (b * nq + i, COL_QM + h)) for h in range(Hm)]
        + [pl.BlockSpec((M, 512), lambda b, i: (b, 0)), pl.BlockSpec((M, 512), lambda b, i: (b, 1))],
        out_specs=pl.BlockSpec((tq, 512), lambda b, i: (b * nq + i, 0)),
        out_shape=jax.ShapeDtypeStruct((T, 512), BF16),
        name="mem_fwd",
    )(proj, proj, proj, proj, mkv, mkv)


def _mem_bwd(proj, mkv, dy, B, S, M):
    tq = 512
    nq = S // tq
    T = B * S
    Hm = MEM_HEADS

    def body(q0, q1, q2, q3, mk_ref, mv_ref, do_ref, dq_ref, dmk_ref, dmv_ref):
        i = pl.program_id(1)
        dqs, dmks, dmvs = [], [], []
        for h, q_ref in enumerate((q0, q1, q2, q3)):
            cols = slice(h * 128, (h + 1) * 128)
            q = q_ref[...]
            do = do_ref[:, cols]
            mk = mk_ref[:, cols]
            w = _mem_weights(q, mk)
            dw = _dot_nt(do, mv_ref[:, cols])
            ds = (w * (dw - jnp.sum(w * dw, axis=1, keepdims=True))).astype(BF16)
            dqs.append(_dot(ds, mk))
            dmks.append(_dot_tn(ds, q))
            dmvs.append(_dot_tn(w.astype(BF16), do))
        dq_ref[...] = (jnp.concatenate(dqs, axis=1) * SCALE128).astype(dq_ref.dtype)

        @pl.when(i == 0)
        def _():
            dmk_ref[...] = jnp.zeros_like(dmk_ref)
            dmv_ref[...] = jnp.zeros_like(dmv_ref)

        dmk_ref[...] += jnp.concatenate(dmks, axis=1) * SCALE128
        dmv_ref[...] += jnp.concatenate(dmvs, axis=1)

    q_spec = pl.BlockSpec((tq, 512), lambda b, i: (b * nq + i, 0))
    m_out = pl.BlockSpec((M, 512), lambda b, i: (b, 0))
    return pl.pallas_call(
        body,
        grid=(B, nq),
        in_specs=[pl.BlockSpec((tq, 128), lambda b, i, h=h: (b * nq + i, COL_QM + h)) for h in range(Hm)]
        + [pl.BlockSpec((M, 512), lambda b, i: (b, 0)), pl.BlockSpec((M, 512), lambda b, i: (b, 1)), q_spec],
        out_specs=[q_spec, m_out, m_out],
        out_shape=[jax.ShapeDtypeStruct((T, 512), BF16),
                   jax.ShapeDtypeStruct((B * M, 512), F32), jax.ShapeDtypeStruct((B * M, 512), F32)],
        name="mem_bwd",
    )(proj, proj, proj, proj, mkv, mkv, dy)


def _all_gather(blk, name):
    R, C = blk.shape

    def body(x_ref, out_ref, send_sems, recv_sems, local_sem):
        x, y, c = _mesh_pos()
        me, sibling = (x, y, c), (x, y, 1 - c)
        chips = [(1 - x, y), (x, 1 - y), (1 - x, 1 - y)]

        def slot(px, py, pc):
            return out_ref.at[4 * px + 2 * py + pc]

        def copy(k, block, to, src=None):
            return pltpu.make_async_remote_copy(
                src_ref=slot(*block) if src is None else src, dst_ref=slot(*block),
                send_sem=send_sems.at[k], recv_sem=recv_sems.at[k],
                device_id=to, device_id_type=pl.DeviceIdType.MESH)

        mine = pltpu.make_async_copy(x_ref, slot(*me), local_sem)
        mine.start()
        first = [copy(0, me, sibling, src=x_ref)]
        first += [copy(1 + j, me, (*chip, c), src=x_ref) for j, chip in enumerate(chips)]
        for cp in first:
            cp.start()
        passed = [copy(4 + j, (*chip, c), sibling) for j, chip in enumerate(chips)]
        for j, chip in enumerate(chips):
            copy(1 + j, (*chip, c), me).wait_recv()
            passed[j].start()
        copy(0, sibling, me).wait_recv()
        for j, chip in enumerate(chips):
            copy(4 + j, (*chip, 1 - c), me).wait_recv()
        for cp in first + passed:
            cp.wait_send()
        mine.wait()

    return pl.pallas_call(
        body,
        in_specs=[pl.BlockSpec(memory_space=pl.ANY)],
        out_specs=pl.BlockSpec(memory_space=pl.ANY),
        out_shape=jax.ShapeDtypeStruct((N_DEV, R, C), blk.dtype),
        scratch_shapes=[pltpu.SemaphoreType.DMA((7,)), pltpu.SemaphoreType.DMA((7,)), pltpu.SemaphoreType.DMA],
        name=name,
    )(blk)


def _all_gather_relayed(blk, name):
    R, C = blk.shape
    R2 = R // 2

    def body(x_ref, out_ref, send_sems, recv_sems, local_sem):
        x, y, c = _mesh_pos()
        me, sib = (x, y, c), (x, y, 1 - c)
        nx, ny, dg = (1 - x, y, c), (x, 1 - y, c), (1 - x, 1 - y, c)

        def slot(p, half=None):
            ref = out_ref.at[4 * p[0] + 2 * p[1] + p[2]]
            return ref if half is None else ref.at[pl.ds(half * R2, R2), :]

        def copy(k, block, to, half=None, own=False):
            src = slot(block, half)
            if own:
                src = x_ref if half is None else x_ref.at[pl.ds(half * R2, R2), :]
            return pltpu.make_async_remote_copy(
                src_ref=src, dst_ref=slot(block, half), send_sem=send_sems.at[k], recv_sem=recv_sems.at[k],
                device_id=to, device_id_type=pl.DeviceIdType.MESH)

        def other(p):
            return (p[0], p[1], 1 - c)

        mine = pltpu.make_async_copy(x_ref, slot(me), local_sem)
        mine.start()
        sends = [copy(1, me, nx, 0, own=True), copy(3, me, ny, 1, own=True), copy(0, me, sib, own=True),
                 copy(2, me, nx, 1, own=True), copy(4, me, ny, 0, own=True)]
        for cp in sends:
            cp.start()
        copy(1, nx, me, 0).wait_recv()
        sends.append(copy(5, nx, ny, 0))
        sends[-1].start()
        copy(3, ny, me, 1).wait_recv()
        sends.append(copy(6, ny, nx, 1))
        sends[-1].start()
        copy(2, nx, me, 1).wait_recv()
        sends.append(copy(7, nx, sib))
        sends[-1].start()
        copy(4, ny, me, 0).wait_recv()
        sends.append(copy(8, ny, sib))
        sends[-1].start()
        copy(5, dg, me, 0).wait_recv()
        copy(6, dg, me, 1).wait_recv()
        sends.append(copy(9, dg, sib))
        sends[-1].start()
        copy(0, sib, me).wait_recv()
        for k, p in ((7, nx), (8, ny), (9, dg)):
            copy(k, other(p), me).wait_recv()
        for cp in sends:
            cp.wait_send()
        mine.wait()

    return pl.pallas_call(
        body,
        in_specs=[pl.BlockSpec(memory_space=pl.ANY)],
        out_specs=pl.BlockSpec(memory_space=pl.ANY),
        out_shape=jax.ShapeDtypeStruct((N_DEV, R, C), blk.dtype),
        scratch_shapes=[pltpu.SemaphoreType.DMA((10,)), pltpu.SemaphoreType.DMA((10,)), pltpu.SemaphoreType.DMA],
        name=name,
    )(blk)


_HBM = pl.BlockSpec(memory_space=pltpu.HBM)
_SEM = pl.BlockSpec(memory_space=pltpu.SEMAPHORE)


def _scatter_start(parts, land, window, carried, name):
    hosted = _Hosted(scatters=[parts], window=window)

    def body(p_ref, land_ref, c_ref, send_sems, recv_sems, local_sems, p_thru, land_thru, c_thru):
        for cp in hosted.copies([p_ref], [land_ref], send_sems, recv_sems, local_sems):
            cp.start()

    sems = (pltpu.SemaphoreType.DMA((7,)), pltpu.SemaphoreType.DMA((7,)), pltpu.SemaphoreType.DMA((1,)))
    hbm = lambda a: pltpu.HBM(a.shape, a.dtype)
    outs = pl.pallas_call(
        body, name=name,
        out_shape=sems + (hbm(parts), hbm(land), hbm(carried)),
        in_specs=(_HBM, _HBM, _HBM), out_specs=(_SEM, _SEM, _SEM, _HBM, _HBM, _HBM),
        input_output_aliases={0: 3, 1: 4, 2: 5},
        compiler_params=pltpu.CompilerParams(has_side_effects=pltpu.SideEffectType.DATAFLOW_SIDE_EFFECTING),
    )(pltpu.with_memory_space_constraint(parts, pltpu.HBM),
      pltpu.with_memory_space_constraint(land, pltpu.HBM),
      pltpu.with_memory_space_constraint(carried, pltpu.HBM))
    return (outs[:4], window), outs[4], outs[5]


def _scatter_wait(flight, land, after, name):
    (send_sems, recv_sems, local_sems, p_thru), window = flight
    hosted = _Hosted(scatters=[p_thru], window=window)

    def body(p_ref, land_ref, send, recv, local, after_ref, p_dead, got_ref):
        for cp in hosted.copies([p_ref], [land_ref], send, recv, local):
            cp.wait()

    hbm = lambda a: pltpu.HBM(a.shape, a.dtype)
    return pl.pallas_call(
        body, name=name,
        out_shape=(hbm(p_thru), hbm(land)),
        in_specs=(_HBM, _HBM, _SEM, _SEM, _SEM, pl.BlockSpec(memory_space=pl.ANY)), out_specs=(_HBM, _HBM),
        input_output_aliases={0: 0, 1: 1},
        compiler_params=pltpu.CompilerParams(has_side_effects=pltpu.SideEffectType.DATAFLOW_SIDE_EFFECTING),
    )(p_thru, land, send_sems, recv_sems, local_sems, after)[1]


def _adamw(parts, w, m, v, name):
    R, C = w.shape
    tr = R
    for cand in (368, 352, 256, 176, 128, 64, 32, 16, 8):
        if R % cand == 0 and cand * C * 4 <= 1536 * 1024:
            tr = cand
            break
    c1 = 1.0 - ADAM_B1 ** ADAM_STEP
    c2 = 1.0 - ADAM_B2 ** ADAM_STEP

    def body(p_ref, w_ref, m_ref, v_ref, g_ref, d_ref, nm_ref, nv_ref):
        g = p_ref[0].astype(F32)
        for d in range(1, N_DEV):
            g = g + p_ref[d].astype(F32)
        nm = ADAM_B1 * m_ref[...] + (1.0 - ADAM_B1) * g
        nv = ADAM_B2 * v_ref[...] + (1.0 - ADAM_B2) * (g * g)
        g_ref[...] = g
        nm_ref[...] = nm
        nv_ref[...] = nv
        d_ref[...] = -ADAM_LR * ((nm / c1) / (jnp.sqrt(nv / c2) + ADAM_EPS) + ADAM_WD * w_ref[...])

    row = pl.BlockSpec((tr, C), lambda i: (i, 0))
    out = jax.ShapeDtypeStruct((R, C), F32)
    return pl.pallas_call(
        body,
        grid=(R // tr,),
        in_specs=[pl.BlockSpec((N_DEV, tr, C), lambda i: (0, i, 0)), row, row, row],
        out_specs=[row] * 4,
        out_shape=[out] * 4,
        compiler_params=_params(),
        name=name,
    )(parts, w, m, v)


def _col_shards(g):
    R, C8 = g.shape
    return g.reshape(R, N_DEV, C8 // N_DEV).transpose(1, 0, 2)


def _row_shards(g):
    R8, C = g.shape
    return g.reshape(N_DEV, R8 // N_DEV, C)


def _cols_full(gathered):
    n, R, C = gathered.shape
    return gathered.transpose(1, 0, 2).reshape(R, n * C)


_BIG = ("w_in", "w_mem_kv", "w_branch_swa", "w_branch_sb", "w_branch_mem", "w_out", "w_gate", "w_up", "w_down")
_COL_SHARDED = ("w_branch_swa", "w_branch_sb", "w_branch_mem")
_TRANSPOSED = ("w_in", "w_gate", "w_up")
_SMALL = ("ln_mix_pre", "ln_mix_post", "swa_sinks", "rel_bias", "ln_mem", "ln_ffn_pre", "ln_ffn_post")
_ORDER = ("ln_mix_pre", "ln_mix_post", "w_in", "swa_sinks", "rel_bias", "ln_mem", "w_mem_kv", "w_branch_swa",
          "w_branch_sb", "w_branch_mem", "w_out", "ln_ffn_pre", "ln_ffn_post", "w_gate", "w_up", "w_down")


def _pack_small(d, last=None):
    rows = [d["ln_mix_pre"], d["ln_mix_post"], d["ln_mem"], d["ln_ffn_pre"], d["ln_ffn_post"],
            jnp.pad(d["swa_sinks"].reshape(1, -1), ((0, 0), (0, D_MODEL - SWA_Q_HEADS))),
            jnp.pad(d["rel_bias"].reshape(1, -1), ((0, 0), (0, D_MODEL - N_BUCKETS * SWA_Q_HEADS))),
            jnp.zeros((1, D_MODEL), F32) if last is None else last]
    return jnp.concatenate([r.astype(F32) for r in rows], axis=0)


def _unpack_small(a):
    return dict(ln_mix_pre=a[0:1], ln_mix_post=a[1:2], ln_mem=a[2:3], ln_ffn_pre=a[3:4], ln_ffn_post=a[4:5],
                swa_sinks=a[5:6, :SWA_Q_HEADS],
                rel_bias=a[6, :N_BUCKETS * SWA_Q_HEADS].reshape(N_BUCKETS, SWA_Q_HEADS))


def kernel(x, mem, ln_mix_pre, ln_mix_post, w_in, swa_sinks, rel_bias, ln_mem, w_mem_kv, w_branch_swa, w_branch_sb, w_branch_mem, w_out, ln_ffn_pre, ln_ffn_post, w_gate, w_up, w_down, loss_target, m_ln_mix_pre, m_ln_mix_post, m_w_in, m_swa_sinks, m_rel_bias, m_ln_mem, m_w_mem_kv, m_w_branch_swa, m_w_branch_sb, m_w_branch_mem, m_w_out, m_ln_ffn_pre, m_ln_ffn_post, m_w_gate, m_w_up, m_w_down, v_ln_mix_pre, v_ln_mix_post, v_w_in, v_swa_sinks, v_rel_bias, v_ln_mem, v_w_mem_kv, v_w_branch_swa, v_w_branch_sb, v_w_branch_mem, v_w_out, v_ln_ffn_pre, v_ln_ffn_post, v_w_gate, v_w_up, v_w_down):
    w = dict(ln_mix_pre=ln_mix_pre, ln_mix_post=ln_mix_post, w_in=w_in[0], swa_sinks=swa_sinks, rel_bias=rel_bias,
             ln_mem=ln_mem, w_mem_kv=w_mem_kv[0], w_branch_swa=w_branch_swa[0], w_branch_sb=w_branch_sb[0],
             w_branch_mem=w_branch_mem[0], w_out=w_out[0], ln_ffn_pre=ln_ffn_pre, ln_ffn_post=ln_ffn_post,
             w_gate=w_gate[0], w_up=w_up[0], w_down=w_down[0])
    mom = dict(ln_mix_pre=m_ln_mix_pre, ln_mix_post=m_ln_mix_post, w_in=m_w_in[0], swa_sinks=m_swa_sinks,
               rel_bias=m_rel_bias, ln_mem=m_ln_mem, w_mem_kv=m_w_mem_kv[0], w_branch_swa=m_w_branch_swa[0],
               w_branch_sb=m_w_branch_sb[0], w_branch_mem=m_w_branch_mem[0], w_out=m_w_out[0],
               ln_ffn_pre=m_ln_ffn_pre, ln_ffn_post=m_ln_ffn_post, w_gate=m_w_gate[0], w_up=m_w_up[0],
               w_down=m_w_down[0])
    var = dict(ln_mix_pre=v_ln_mix_pre, ln_mix_post=v_ln_mix_post, w_in=v_w_in[0], swa_sinks=v_swa_sinks,
               rel_bias=v_rel_bias, ln_mem=v_ln_mem, w_mem_kv=v_w_mem_kv[0], w_branch_swa=v_w_branch_swa[0],
               w_branch_sb=v_w_branch_sb[0], w_branch_mem=v_w_branch_mem[0], w_out=v_w_out[0],
               ln_ffn_pre=v_ln_ffn_pre, ln_ffn_post=v_ln_ffn_post, w_gate=v_w_gate[0], w_up=v_w_up[0],
               w_down=v_w_down[0])
    B, S, D = x.shape
    M = mem.shape[1]
    T = B * S
    F = D_FF
    x2 = x.reshape(T, D)
    mem2 = mem.reshape(B * M, D)
    t2 = loss_target.reshape(T, D)
    buckets = jnp.asarray(_swa_buckets())
    for d in (w, mom, var):
        for n in _TRANSPOSED:
            d[n] = d[n].T
    wb = {n: w[n].astype(BF16) for n in _BIG}
    full = {}

    def landed(names, got):
        for n, g in zip(names, got):
            full[n] = _cols_full(g) if n in _COL_SHARDED else g.reshape(-1, g.shape[-1])

    def shards(n, g):
        return _col_shards(g) if n in _COL_SHARDED else _row_shards(g)

    landed(["w_in"], [_all_gather_relayed(wb["w_in"], "ag_w_in")])
    u = _rms_fwd(x2, ln_mix_pre, "rms_mix_pre")
    early = ["w_mem_kv", "w_branch_swa", "w_branch_sb", "w_branch_mem"]
    proj, got = _matmul([(u, full["w_in"])], "nt", BF16, 512, IN_W // 2, D, "proj_in",
                        hosted=_Hosted(gathers=[wb[n] for n in early]))
    landed(early, got)
    mn = _rms_fwd(mem2, ln_mem, "rms_mem")
    mkv = _matmul([(mn, full["w_mem_kv"])], "nn", BF16, 512, 1024, D, "proj_mem")
    bias_tab = _swa_bias_table(rel_bias, buckets)
    y_swa, got = _swa_fwd(proj, swa_sinks, bias_tab, B, S, hosted=_Hosted(gathers=[wb["w_out"]]))
    landed(["w_out"], got)
    late = ["w_gate", "w_up"]
    y_sb, rtot, got = _sb_fwd(proj, B, S, hosted=_Hosted(gathers=[wb[n] for n in late]))
    landed(late, got)
    y_mem = _mem_fwd(proj, mkv, B, S, M)
    wbs = (full["w_branch_swa"], full["w_branch_sb"], full["w_branch_mem"])
    merged, p_swa, p_sb, p_mem = _branch_gate(proj, (y_swa, y_sb, y_mem), wbs)
    mix, h1, u2 = _post_pre(x2, merged, full["w_out"], ln_mix_post, ln_ffn_pre)
    a, zg, zu, got = _ffn_up(u2, full["w_gate"], full["w_up"], hosted=_Hosted(gathers=[wb["w_down"]]))
    landed(["w_down"], got)
    dffn, dh2, loss_tile, d_ln_ffn_post = _loss_head(a, full["w_down"], h1, t2, ln_ffn_post)

    part = {}
    part["w_down"] = _matmul([(a, dffn)], "tn", BF16, F // 2, 1024, 1024, "dw_down")
    dzg, dzu = _ffn_down_bwd(dffn, full["w_down"], zg, zu)
    part["w_gate"] = _matmul([(dzg, u2)], "tn", BF16, F // 2, 1024, 1024, "dw_gate")
    part["w_up"] = _matmul([(dzu, u2)], "tn", BF16, F // 2, 1024, 1024, "dw_up")
    dh1, dmix, d_ln_ffn_pre, d_ln_mix_post = _mid_bwd(dzg, dzu, full["w_gate"], full["w_up"], h1, dh2, mix,
                                                      ln_ffn_pre, ln_mix_post)
    part["w_out"] = _matmul([(merged, dmix)], "tn", BF16, 1024, 1024, 1024, "dw_out")
    (dp_swa, dp_sb, dp_mem, dg0, dg1, dg2, dy_swa, dy_sb, dy_mem) = _gate_bwd(
        dmix, full["w_out"], (p_swa, p_sb, p_mem), proj, wbs)
    part["w_branch_swa"] = _matmul([(y_swa, dp_swa)], "tn", BF16, 512, 1024, 1024, "dw_branch_swa")
    part["w_branch_sb"] = _matmul([(y_sb, dp_sb)], "tn", BF16, 512, 1024, 1024, "dw_branch_sb")
    part["w_branch_mem"] = _matmul([(y_mem, dp_mem)], "tn", BF16, 512, 1024, 1024, "dw_branch_mem")
    dqm, dmk, dmv = _mem_bwd(proj, mkv, dy_mem, B, S, M)
    dmkv = jnp.concatenate([dmk, dmv], axis=1).astype(BF16)
    part["w_mem_kv"] = _matmul([(mn, dmkv)], "tn", BF16, 1024, 1024, 512, "dw_mem_kv")
    dmn = _matmul([(dmkv, full["w_mem_kv"])], "nt", F32, 512, 1024, 1024, "d_mn")
    d_ln_mem = _gain_grad(mem2, dmn)
    behind_swa = ["w_out", "w_branch_swa", "w_branch_sb"]
    dqa, dka, dva, dbias, dsink, got = _swa_bwd(
        proj, dy_swa, swa_sinks, bias_tab, B, S, hosted=_Hosted(scatters=[shards(n, part[n]) for n in behind_swa]))
    recv = dict(zip(behind_swa, got))
    behind_sb = ["w_down", "w_gate", "w_up", "w_branch_mem", "w_mem_kv"]
    dqb, dkb, dvb, got = _sb_bwd(proj, dy_sb, rtot, B, S,
                                 hosted=_Hosted(scatters=[shards(n, part[n]) for n in behind_sb]))
    recv.update(zip(behind_sb, got))
    d_rel_bias = _swa_bias_grad(dbias, buckets)[:, :SWA_Q_HEADS]
    d_sinks = dsink[:, 0].reshape(1, SWA_Q_HEADS)
    dproj = jnp.concatenate([dqa, dka.astype(BF16), dva.astype(BF16), dqb, dkb.astype(BF16), dvb.astype(BF16),
                             dqm, dg0, dg1, dg2], axis=1)
    half = D // 2
    land = lax.empty((N_DEV, IN_W // N_DEV, D), BF16)
    flights = []
    for t in range(2):
        dw_half = _matmul([(dproj, u)], "tn", BF16, IN_W // 2, half, 1024, "dw_in_%d" % t, n_cols=half, n_off=t)
        flight, land, dproj = _scatter_start(_row_shards(dw_half), land, (t * half, half), dproj,
                                             "rs_w_in_start_%d" % t)
        flights.append(flight)
    grad_x, d_ln_mix_pre = _pre_bwd(dproj, full["w_in"], x2, dh1, ln_mix_pre)

    out = {n: _adamw(recv[n], w[n], mom[n], var[n], "adamw_" + n) for n in _BIG if n != "w_in"}
    small_grads = dict(ln_mix_pre=d_ln_mix_pre, ln_mix_post=d_ln_mix_post, swa_sinks=d_sinks, rel_bias=d_rel_bias,
                       ln_mem=d_ln_mem, ln_ffn_pre=d_ln_ffn_pre, ln_ffn_post=d_ln_ffn_post)
    small_parts = _all_gather(_pack_small(small_grads, jnp.tile(loss_tile[0:1], (1, D // 128))), "ag_small")
    res = _adamw(small_parts, _pack_small(w), _pack_small(mom), _pack_small(var), "adamw_small")
    loss = res[0][7, 0]
    small = [_unpack_small(r) for r in res]
    for n in _SMALL:
        out[n] = tuple(s[n] for s in small)
    for t, flight in enumerate(flights):
        land = _scatter_wait(flight, land, res[0], "rs_w_in_wait_%d" % t)
    out["w_in"] = _adamw(land, w["w_in"], mom["w_in"], var["w_in"], "adamw_w_in")
    for n in _TRANSPOSED:
        out[n] = tuple(o.T for o in out[n])

    like = dict(ln_mix_pre=ln_mix_pre, ln_mix_post=ln_mix_post, w_in=w_in, swa_sinks=swa_sinks, rel_bias=rel_bias,
                ln_mem=ln_mem, w_mem_kv=w_mem_kv, w_branch_swa=w_branch_swa, w_branch_sb=w_branch_sb,
                w_branch_mem=w_branch_mem, w_out=w_out, ln_ffn_pre=ln_ffn_pre, ln_ffn_post=ln_ffn_post,
                w_gate=w_gate, w_up=w_up, w_down=w_down)
    result = [loss, grad_x.reshape(B, S, D)]
    for k in range(4):
        result += [out[n][k].reshape(like[n].shape) for n in _ORDER]
    return tuple(result)
```

```python
import functools
import math

import numpy as np
import jax
import jax.numpy as jnp
from jax import lax
from jax.experimental import pallas as pl
from jax.experimental.pallas import tpu as pltpu

F32 = jnp.float32
BF16 = jnp.bfloat16

N_DEV = 8
D_MODEL = 1024
BLOCK = 128
EPS = 1e-6
HEAD_DIM = 64
SWA_Q_HEADS = 8
SWA_WINDOW = 128
N_BUCKETS = 32
MAX_DISTANCE = 128
MEM_HEADS = 4
MEM_HEAD_DIM = 128
D_FF = 2816
IN_W = 5888
COL_QA, COL_KA, COL_VA, COL_QB, COL_KB, COL_VB, COL_QM, COL_GL = 0, 4, 5, 6, 10, 14, 18, 22
SCALE64 = HEAD_DIM ** -0.5
SCALE128 = MEM_HEAD_DIM ** -0.5
NEG = -1e30

ADAM_LR = 0.001
ADAM_B1 = 0.9
ADAM_B2 = 0.999
ADAM_EPS = 1e-08
ADAM_WD = 0.01
ADAM_STEP = 10

VMEM_LIMIT_BYTES = 56 * 1024 * 1024


def _params(**kw):
    return pltpu.CompilerParams(vmem_limit_bytes=VMEM_LIMIT_BYTES, **kw)


def _dot(a, b):
    return jnp.dot(a, b, preferred_element_type=F32)


def _dot_nt(a, b):
    return lax.dot_general(a, b, (((1,), (1,)), ((), ())), preferred_element_type=F32)


def _dot_tn(a, b):
    return lax.dot_general(a, b, (((0,), (0,)), ((), ())), preferred_element_type=F32)


def _dot_split(x, m2):
    hi = x.astype(BF16)
    lo = (x - hi.astype(F32)).astype(BF16)
    return _dot(jnp.concatenate([hi, lo], axis=1), m2)


def _mesh_pos():
    return lax.axis_index("x"), lax.axis_index("y"), lax.axis_index("c")


class _Hosted:
    def __init__(self, gathers=(), scatters=(), window=None):
        self.items = [("g", a) for a in gathers] + [("s", a) for a in scatters]
        self.n = len(self.items)
        self.window = window

    def operands(self):
        return [a for _, a in self.items]

    def specs(self):
        return [pl.BlockSpec(memory_space=pl.ANY)] * self.n

    def out_shapes(self):
        return [jax.ShapeDtypeStruct(((N_DEV,) + a.shape) if kind == "g" else a.shape, a.dtype)
                for kind, a in self.items]

    def scratch(self):
        return [pltpu.SemaphoreType.DMA((7 * self.n,)), pltpu.SemaphoreType.DMA((7 * self.n,)),
                pltpu.SemaphoreType.DMA((self.n,))]

    def copies(self, in_refs, out_refs, send_sems, recv_sems, local_sems):
        x, y, c = _mesh_pos()
        me = 4 * x + 2 * y + c
        out = []
        for t, (kind, _) in enumerate(self.items):
            own = in_refs[t] if kind == "g" else in_refs[t].at[me]
            dst = out_refs[t].at[me]
            if self.window is not None:
                dst = out_refs[t].at[me, :, pl.ds(*self.window)]
            out.append(pltpu.make_async_copy(own, dst, local_sems.at[t]))
            for k in range(1, N_DEV):
                px, py, pc = x ^ (k >> 2), y ^ ((k >> 1) & 1), c ^ (k & 1)
                src = in_refs[t] if kind == "g" else in_refs[t].at[4 * px + 2 * py + pc]
                out.append(pltpu.make_async_remote_copy(
                    src_ref=src, dst_ref=dst,
                    send_sem=send_sems.at[7 * t + k - 1], recv_sem=recv_sems.at[7 * t + k - 1],
                    device_id=(px, py, pc), device_id_type=pl.DeviceIdType.MESH))
        return out


def _host(body, n_in, n_out, hosted, grid):
    if hosted is None:
        return body
    nc = hosted.n

    def wrapped(*refs):
        ins = refs[:n_in]
        cin = refs[n_in:n_in + nc]
        outs = refs[n_in + nc:n_in + nc + n_out]
        cout = refs[n_in + nc + n_out:n_in + 2 * nc + n_out]
        scratch = refs[n_in + 2 * nc + n_out:len(refs) - 3]
        sems = refs[len(refs) - 3:]
        ids = [pl.program_id(d) for d in range(len(grid))]
        first = functools.reduce(lambda a, b: a & b, [i == 0 for i in ids])
        last = functools.reduce(lambda a, b: a & b, [i == g - 1 for i, g in zip(ids, grid)])

        @pl.when(first)
        def _():
            for cp in hosted.copies(cin, cout, *sems):
                cp.start()

        body(*ins, *outs, *scratch)

        @pl.when(last)
        def _():
            for cp in hosted.copies(cin, cout, *sems):
                cp.wait()

    return wrapped


def _hosted_call(body, grid, in_specs, out_specs, out_shape, scratch_shapes, hosted, name, args):
    n_out = len(out_specs)
    if hosted is None:
        outs = pl.pallas_call(body, grid=grid, in_specs=in_specs, out_specs=out_specs, out_shape=out_shape,
                              scratch_shapes=scratch_shapes, compiler_params=_params(), name=name)(*args)
        return list(outs), []
    outs = pl.pallas_call(
        _host(body, len(in_specs), n_out, hosted, grid),
        grid=grid,
        in_specs=list(in_specs) + hosted.specs(),
        out_specs=list(out_specs) + hosted.specs(),
        out_shape=list(out_shape) + hosted.out_shapes(),
        scratch_shapes=list(scratch_shapes) + hosted.scratch(),
        compiler_params=_params(),
        name=name,
    )(*args, *hosted.operands())
    return list(outs[:n_out]), list(outs[n_out:])


_DIMS = {"nn": (((1,), (0,)), ((), ())), "nt": (((1,), (1,)), ((), ())), "tn": (((0,), (0,)), ((), ()))}


def _matmul(pairs, mode, out_dtype, tm, tn, tk, name, hosted=None, n_cols=None, n_off=0):
    a0, b0 = pairs[0]
    if mode == "nn":
        (M, K), N = a0.shape, b0.shape[1]
    elif mode == "nt":
        (M, K), N = a0.shape, b0.shape[0]
    else:
        (K, M), N = a0.shape, b0.shape[1]
    N = N if n_cols is None else n_cols
    tm, tn, tk = min(tm, M), min(tn, N), min(tk, K)
    assert M % tm == 0 and N % tn == 0 and K % tk == 0, (name, M, N, K, tm, tn, tk)
    nm, nn, nk = M // tm, N // tn, K // tk
    npair = len(pairs)
    dims = _DIMS[mode]

    def body(*refs):
        ab = refs[:2 * npair]
        o_ref = refs[2 * npair]
        acc_ref = refs[2 * npair + 1]
        k = pl.program_id(2)
        part = lax.dot_general(ab[0][...], ab[1][...], dims, preferred_element_type=F32)
        for q in range(1, npair):
            part += lax.dot_general(ab[2 * q][...], ab[2 * q + 1][...], dims, preferred_element_type=F32)
        if nk == 1:
            o_ref[...] = part.astype(o_ref.dtype)
        else:
            @pl.when(k == 0)
            def _():
                acc_ref[...] = part

            @pl.when(k > 0)
            def _():
                acc_ref[...] += part

            @pl.when(k == nk - 1)
            def _():
                o_ref[...] = acc_ref[...].astype(o_ref.dtype)

    if mode == "nn":
        a_spec = pl.BlockSpec((tm, tk), lambda n, m, k: (m, k))
        b_spec = pl.BlockSpec((tk, tn), lambda n, m, k: (k, n + n_off))
    elif mode == "nt":
        a_spec = pl.BlockSpec((tm, tk), lambda n, m, k: (m, k))
        b_spec = pl.BlockSpec((tn, tk), lambda n, m, k: (n + n_off, k))
    else:
        a_spec = pl.BlockSpec((tk, tm), lambda n, m, k: (k, m))
        b_spec = pl.BlockSpec((tk, tn), lambda n, m, k: (k, n + n_off))
    args = [t for pr in pairs for t in pr]
    outs, moved = _hosted_call(
        body, (nn, nm, nk), [a_spec, b_spec] * npair, [pl.BlockSpec((tm, tn), lambda n, m, k: (m, n))],
        [jax.ShapeDtypeStruct((M, N), out_dtype)], [pltpu.VMEM((tm, tn) if nk > 1 else (8, 128), F32)],
        hosted, name, args)
    return outs[0] if hosted is None else (outs[0], moved)


def _rms_fwd(x, g, name):
    T, D = x.shape
    tr = min(512, T)

    def body(x_ref, g_ref, u_ref):
        xf = x_ref[...]
        r = lax.rsqrt(jnp.mean(xf * xf, axis=-1, keepdims=True) + EPS)
        u_ref[...] = ((xf * r) * g_ref[...]).astype(u_ref.dtype)

    return pl.pallas_call(
        body,
        grid=(T // tr,),
        in_specs=[pl.BlockSpec((tr, D), lambda i: (i, 0)), pl.BlockSpec((1, D), lambda i: (0, 0))],
        out_specs=pl.BlockSpec((tr, D), lambda i: (i, 0)),
        out_shape=jax.ShapeDtypeStruct((T, D), BF16),
        name=name,
    )(x, g)


def _rms_bwd_terms(xin, g, dy):
    r = lax.rsqrt(jnp.mean(xin * xin, axis=-1, keepdims=True) + EPS)
    xh = xin * r
    dg = jnp.sum(dy * xh, axis=0, keepdims=True)
    dxh = dy * g
    dx = r * (dxh - xh * jnp.mean(dxh * xh, axis=-1, keepdims=True))
    return dx, dg


GATE_TC = 256


def _branch_gate(proj, ys, wbs):
    T = proj.shape[0]
    D = D_MODEL
    tr, tc = min(512, T), GATE_TC
    nc = D // tc
    gl0 = COL_GL * 128 // tc

    def body(*refs):
        y_refs, w_refs = refs[:3], refs[3:6]
        g_refs = refs[6:6 + 3 * nc]
        merged_ref = refs[6 + 3 * nc]
        p_refs = refs[7 + 3 * nc:]
        acc = [jnp.zeros((tr, tc), F32)] * nc
        for j in range(3):
            p = _dot(y_refs[j][...], w_refs[j][...])
            p_refs[j][...] = p.astype(BF16)
            for c in range(nc):
                acc[c] = acc[c] + jax.nn.sigmoid(g_refs[j * nc + c][...].astype(F32)) * p[:, c * tc:(c + 1) * tc]
        merged_ref[...] = jnp.concatenate(acc, axis=1).astype(BF16)

    y_spec = pl.BlockSpec((tr, 512), lambda i: (i, 0))
    w_spec = pl.BlockSpec((512, D), lambda i: (0, 0))
    o_spec = pl.BlockSpec((tr, D), lambda i: (i, 0))
    gl_specs = [pl.BlockSpec((tr, tc), lambda i, b=gl0 + j * nc + c: (i, b)) for j in range(3) for c in range(nc)]
    out = jax.ShapeDtypeStruct((T, D), BF16)
    return pl.pallas_call(
        body,
        grid=(T // tr,),
        in_specs=[y_spec] * 3 + [w_spec] * 3 + gl_specs,
        out_specs=[o_spec] * 4,
        out_shape=[out] * 4,
        compiler_params=_params(),
        name="branch_gate",
    )(*ys, *wbs, *([proj] * (3 * nc)))


def _post_pre(x, merged, w_out, g_post, g_pre):
    def epilogue(mx, rows, vecs, outs, accs, valid):
        mix_ref, h1_ref, u2_ref = outs
        mix_ref[...] = mx
        r = lax.rsqrt(jnp.mean(mx * mx, axis=-1, keepdims=True) + EPS)
        h1 = rows[0][...] + (mx * r) * vecs[0][...]
        h1_ref[...] = h1
        r2 = lax.rsqrt(jnp.mean(h1 * h1, axis=-1, keepdims=True) + EPS)
        u2_ref[...] = ((h1 * r2) * vecs[1][...]).astype(u2_ref.dtype)

    return _matmul_rows([(merged, w_out)], 512, [x], [g_post, g_pre], [F32, F32, BF16], 0, epilogue, "post_pre")


def _ffn_up(u2, w_gate_t, w_up_t, hosted=None):
    T, D = u2.shape
    F = D_FF
    tm, tn = 512, F // 2

    def body(u_ref, wg_ref, wu_ref, a_ref, zg_ref, zu_ref):
        u = u_ref[...]
        zg = _dot_nt(u, wg_ref[...])
        zu = _dot_nt(u, wu_ref[...])
        a_ref[...] = (zg * jax.nn.sigmoid(zg) * zu).astype(a_ref.dtype)
        zg_ref[...] = zg.astype(zg_ref.dtype)
        zu_ref[...] = zu.astype(zu_ref.dtype)

    o_spec = pl.BlockSpec((tm, tn), lambda n, m: (m, n))
    out = jax.ShapeDtypeStruct((T, F), BF16)
    outs, moved = _hosted_call(
        body, (F // tn, T // tm),
        [pl.BlockSpec((tm, D), lambda n, m: (m, 0)),
         pl.BlockSpec((tn, D), lambda n, m: (n, 0)),
         pl.BlockSpec((tn, D), lambda n, m: (n, 0))],
        [o_spec] * 3, [out] * 3, [], hosted, "ffn_up", (u2, w_gate_t, w_up_t))
    return (*outs, moved)


def _loss_head(a, w_down, h1, target, g_post):
    D = h1.shape[1]

    def epilogue(f, rows, vecs, outs, accs, valid):
        h1_ref, t_ref = rows
        dffn_ref, dh2_ref = outs
        g = vecs[0][...]
        keep = jnp.where(valid, 1.0, 0.0)
        r = lax.rsqrt(jnp.mean(f * f, axis=-1, keepdims=True) + EPS)
        xh = f * r
        err = (h1_ref[...] + xh * g) - t_ref[...]
        part = 0.5 * jnp.sum(jnp.mean(err * err, axis=-1, keepdims=True), axis=0, keepdims=True)
        dh2 = err * (1.0 / D)
        dh2_ref[...] = dh2
        dgp = jnp.sum(dh2 * xh, axis=0, keepdims=True)
        dxh = dh2 * g
        dffn_ref[...] = (r * (dxh - xh * jnp.mean(dxh * xh, axis=-1, keepdims=True))).astype(dffn_ref.dtype)
        accs[0][...] += dgp * keep
        accs[1][...] += jnp.broadcast_to(part, accs[1].shape) * keep

    return _matmul_rows([(a, w_down)], 512, [h1, target], [g_post], [BF16, F32], 2, epilogue, "loss_head")


def _ffn_down_bwd(dffn, wd, zg, zu):
    T, D = dffn.shape
    F = D_FF
    tm, tn = 512, F // 2

    def body(d_ref, w_ref, zg_ref, zu_ref, dzg_ref, dzu_ref):
        da = _dot_nt(d_ref[...], w_ref[...])
        zg = zg_ref[...].astype(F32)
        zu = zu_ref[...].astype(F32)
        s = jax.nn.sigmoid(zg)
        dzu_ref[...] = (da * (zg * s)).astype(dzu_ref.dtype)
        dzg_ref[...] = (da * zu * (s * (1.0 + zg * (1.0 - s)))).astype(dzg_ref.dtype)

    z_spec = pl.BlockSpec((tm, tn), lambda n, m: (m, n))
    out = jax.ShapeDtypeStruct((T, F), BF16)
    return pl.pallas_call(
        body,
        grid=(F // tn, T // tm),
        in_specs=[pl.BlockSpec((tm, D), lambda n, m: (m, 0)), pl.BlockSpec((tn, D), lambda n, m: (n, 0)),
                  z_spec, z_spec],
        out_specs=[z_spec, z_spec],
        out_shape=[out, out],
        compiler_params=_params(),
        name="ffn_down_bwd",
    )(dffn, wd, zg, zu)


def _matmul_rows(pairs, tm, rows_in, vecs_in, rows_out, n_vec_out, epilogue, name):
    a0, b0 = pairs[0]
    (M, K), N = a0.shape, b0.shape[1]
    tm = min(tm, M)
    assert M % tm == 0, (name, M, tm)
    nm = M // tm
    npair = len(pairs)
    n_in = 2 * npair + len(rows_in) + len(vecs_in)
    n_out = len(rows_out) + n_vec_out

    def body(*refs):
        ab = refs[:2 * npair]
        r_in = refs[2 * npair:2 * npair + len(rows_in)]
        v_in = refs[2 * npair + len(rows_in):n_in]
        r_out = refs[n_in:n_in + len(rows_out)]
        v_out = refs[n_in + len(rows_out):n_in + n_out]
        last_ref, acc_ref = refs[n_in + n_out], refs[n_in + n_out + 1]
        m = pl.program_id(0)

        @pl.when(m == 0)
        def _():
            acc_ref[...] = jnp.zeros_like(acc_ref)
            for v in v_out:
                v[...] = jnp.zeros_like(v)

        prev = acc_ref[...]
        part = _dot(ab[0][...], ab[1][...])
        for q in range(1, npair):
            part += _dot(ab[2 * q][...], ab[2 * q + 1][...])
        acc_ref[...] = part
        last_ref[...] = part
        epilogue(prev, r_in, v_in, r_out, v_out, m > 0)

    def tail(*refs):
        last_ref = refs[0]
        r_in = refs[1:1 + len(rows_in)]
        v_in = refs[1 + len(rows_in):1 + len(rows_in) + len(vecs_in)]
        k0 = 1 + len(rows_in) + len(vecs_in)
        v_old = refs[k0 + len(rows_out):k0 + n_out]
        r_out = refs[k0 + n_out:k0 + n_out + len(rows_out)]
        v_out = refs[k0 + n_out + len(rows_out):]
        for old, new in zip(v_old, v_out):
            new[...] = old[...]
        epilogue(last_ref[...], r_in, v_in, r_out, v_out, True)

    prev_row = pl.BlockSpec((tm, N), lambda m: (jnp.maximum(m - 1, 0), 0))
    vec = pl.BlockSpec((1, N), lambda m: (0, 0))
    once = pl.BlockSpec((tm, N), lambda m: (0, 0))
    outs = pl.pallas_call(
        body,
        grid=(nm,),
        in_specs=[pl.BlockSpec((tm, K), lambda m: (m, 0)),
                  pl.BlockSpec((K, N), lambda m: (0, 0), pipeline_mode=pl.Buffered(1))] * npair
        + [prev_row] * len(rows_in) + [vec] * len(vecs_in),
        out_specs=[prev_row] * len(rows_out) + [vec] * n_vec_out + [once],
        out_shape=[jax.ShapeDtypeStruct((M, N), dt) for dt in rows_out]
        + [jax.ShapeDtypeStruct((1, N), F32)] * n_vec_out + [jax.ShapeDtypeStruct((tm, N), F32)],
        scratch_shapes=[pltpu.VMEM((tm, N), F32)],
        compiler_params=_params(),
        name=name,
    )(*[t for pr in pairs for t in pr], *rows_in, *vecs_in)
    last_row = pl.BlockSpec((tm, N), lambda i: (nm - 1, 0))
    vec1 = pl.BlockSpec((1, N), lambda i: (0, 0))
    anyspec = pl.BlockSpec(memory_space=pl.ANY)
    return pl.pallas_call(
        tail,
        grid=(1,),
        in_specs=[pl.BlockSpec((tm, N), lambda i: (0, 0))] + [last_row] * len(rows_in) + [vec1] * len(vecs_in)
        + [anyspec] * len(rows_out) + [vec1] * n_vec_out,
        out_specs=[last_row] * len(rows_out) + [vec1] * n_vec_out,
        out_shape=[jax.ShapeDtypeStruct((M, N), dt) for dt in rows_out]
        + [jax.ShapeDtypeStruct((1, N), F32)] * n_vec_out,
        input_output_aliases={1 + len(rows_in) + len(vecs_in) + t: t for t in range(len(rows_out))},
        compiler_params=_params(),
        name=name + "_tail",
    )(outs[n_out], *rows_in, *vecs_in, *outs[:n_out])


def _mid_bwd(dzg, dzu, w_gate_t, w_up_t, h1, dh2, mix, g_pre, g_post):
    def epilogue(du2, rows, vecs, outs, accs, valid):
        h1_ref, dh2_ref, mix_ref = rows
        gq_ref, gp_ref = vecs
        dh1_ref, dmix_ref = outs
        keep = jnp.where(valid, 1.0, 0.0)
        dx, dgq = _rms_bwd_terms(h1_ref[...], gq_ref[...], du2)
        dh1 = dh2_ref[...] + dx
        dh1_ref[...] = dh1
        dmix, dgp = _rms_bwd_terms(mix_ref[...], gp_ref[...], dh1)
        dmix_ref[...] = dmix.astype(dmix_ref.dtype)
        accs[0][...] += dgq * keep
        accs[1][...] += dgp * keep

    return _matmul_rows([(dzg, w_gate_t), (dzu, w_up_t)], 512, [h1, dh2, mix], [g_pre, g_post],
                        [F32, BF16], 2, epilogue, "mid_bwd")


def _gate_bwd(dmix, w_out, ps, proj, wbs):
    T, D = dmix.shape
    tr, tc = min(512, T), GATE_TC
    nc = D // tc
    gl0 = COL_GL * 128 // tc

    def body(*refs):
        dmix_ref, wo_ref = refs[:2]
        p_refs = refs[2:5]
        g_refs = refs[5:5 + 3 * nc]
        w_refs = refs[5 + 3 * nc:8 + 3 * nc]
        outs = refs[8 + 3 * nc:]
        dp_refs, dg_refs, dy_refs = outs[:3], outs[3:6], outs[6:9]
        dm = _dot_nt(dmix_ref[...], wo_ref[...])
        for j in range(3):
            dps = []
            for c in range(nc):
                cols = slice(c * tc, (c + 1) * tc)
                s = jax.nn.sigmoid(g_refs[j * nc + c][...].astype(F32))
                dmc = dm[:, cols]
                dps.append((dmc * s).astype(BF16))
                dg_refs[j][:, cols] = (dmc * p_refs[j][:, cols].astype(F32) * (s * (1.0 - s))).astype(BF16)
            dp = jnp.concatenate(dps, axis=1)
            dp_refs[j][...] = dp
            dy_refs[j][...] = _dot_nt(dp, w_refs[j][...]).astype(BF16)

    row = pl.BlockSpec((tr, D), lambda i: (i, 0))
    y_spec = pl.BlockSpec((tr, 512), lambda i: (i, 0))
    w_spec = pl.BlockSpec((512, D), lambda i: (0, 0))
    gl_specs = [pl.BlockSpec((tr, tc), lambda i, b=gl0 + j * nc + c: (i, b)) for j in range(3) for c in range(nc)]
    big = jax.ShapeDtypeStruct((T, D), BF16)
    small = jax.ShapeDtypeStruct((T, 512), BF16)
    return pl.pallas_call(
        body,
        grid=(T // tr,),
        in_specs=[row, pl.BlockSpec((D, D), lambda i: (0, 0))] + [row] * 3 + gl_specs + [w_spec] * 3,
        out_specs=[row] * 6 + [y_spec] * 3,
        out_shape=[big] * 6 + [small] * 3,
        compiler_params=_params(),
        name="gate_bwd",
    )(dmix, w_out, *ps, *([proj] * (3 * nc)), *wbs)


def _pre_bwd(dproj, w_in_t, x, dh1, g):
    def epilogue(du, rows, vecs, outs, accs, valid):
        x_ref, dh1_ref = rows
        dx, dg = _rms_bwd_terms(x_ref[...], vecs[0][...], du)
        outs[0][...] = dh1_ref[...] + dx
        accs[0][...] += dg * jnp.where(valid, 1.0, 0.0)

    return _matmul_rows([(dproj, w_in_t)], 512, [x, dh1], [g], [F32], 1, epilogue, "pre_bwd")


def _gain_grad(xin, dy):
    T, D = xin.shape
    tr = min(512, T)

    def body(x_ref, dy_ref, dg_ref):
        i = pl.program_id(0)
        xf = x_ref[...]
        r = lax.rsqrt(jnp.mean(xf * xf, axis=-1, keepdims=True) + EPS)

        @pl.when(i == 0)
        def _():
            dg_ref[...] = jnp.zeros_like(dg_ref)

        dg_ref[...] += jnp.sum(dy_ref[...] * (xf * r), axis=0, keepdims=True)

    row = pl.BlockSpec((tr, D), lambda i: (i, 0))
    return pl.pallas_call(
        body,
        grid=(T // tr,),
        in_specs=[row, row],
        out_specs=pl.BlockSpec((1, D), lambda i: (0, 0)),
        out_shape=jax.ShapeDtypeStruct((1, D), F32),
        name="gain_grad",
    )(xin, dy)


def _swa_buckets():
    dist = (np.arange(BLOCK)[:, None] + BLOCK) - np.arange(2 * BLOCK)[None, :]
    max_exact = N_BUCKETS // 2
    d = np.maximum(dist, 0)
    df = np.maximum(d, 1).astype(np.float32)
    large = max_exact + (np.log(df / np.float32(max_exact)) / np.float32(math.log(MAX_DISTANCE / max_exact))
                         * np.float32(N_BUCKETS - max_exact)).astype(np.int32)
    large = np.minimum(large, N_BUCKETS - 1)
    bucket = np.where(d < max_exact, d, large)
    in_win = (dist >= 0) & (dist < SWA_WINDOW)
    return np.where(in_win, bucket, -1).astype(np.int32)


def _swa_bias_table(rel_bias, buckets):
    H = SWA_Q_HEADS

    def body(rb_ref, bk_ref, o_ref):
        bk = bk_ref[...]
        for h in range(H):
            acc = jnp.full(bk.shape, NEG, F32)
            for b in range(N_BUCKETS):
                acc = jnp.where(bk == b, rb_ref[b, h], acc)
            o_ref[h] = acc

    return pl.pallas_call(
        body,
        in_specs=[pl.BlockSpec(memory_space=pltpu.SMEM), pl.BlockSpec(memory_space=pltpu.VMEM)],
        out_specs=pl.BlockSpec(memory_space=pltpu.VMEM),
        out_shape=jax.ShapeDtypeStruct((H, BLOCK, 2 * BLOCK), F32),
        name="swa_bias_table",
    )(rel_bias, buckets)


def _swa_bias_grad(dbias, buckets):
    H = SWA_Q_HEADS

    def body(db_ref, bk_ref, o_ref):
        bk = bk_ref[...]
        rows = lax.broadcasted_iota(jnp.int32, (N_BUCKETS, 128), 0)
        lanes = lax.broadcasted_iota(jnp.int32, (N_BUCKETS, 128), 1)
        acc = jnp.zeros((N_BUCKETS, 128), F32)
        for h in range(H):
            d = db_ref[h]
            for b in range(N_BUCKETS):
                s = jnp.sum(jnp.sum(jnp.where(bk == b, d, 0.0), axis=1, keepdims=True), axis=0, keepdims=True)
                acc = jnp.where((rows == b) & (lanes == h), s, acc)
        o_ref[...] = acc

    return pl.pallas_call(
        body,
        in_specs=[pl.BlockSpec(memory_space=pltpu.VMEM)] * 2,
        out_specs=pl.BlockSpec(memory_space=pltpu.VMEM),
        out_shape=jax.ShapeDtypeStruct((N_BUCKETS, 128), F32),
        name="swa_bias_grad",
    )(dbias, buckets)


SWA_GROUP = 4
SWA_ROWS = SWA_Q_HEADS * BLOCK


def _swa_kv_lanes(h):
    lane = lax.broadcasted_iota(jnp.int32, (BLOCK, BLOCK), 1)
    return (lane // HEAD_DIM) == h // SWA_GROUP


def _swa_stack(x, heads):
    blocks = []
    for h in heads:
        xp = x[:, (h // 2) * BLOCK:(h // 2 + 1) * BLOCK]
        xs = xp if h % 2 == h // SWA_GROUP else pltpu.roll(xp, HEAD_DIM, 1)
        blocks.append(jnp.where(_swa_kv_lanes(h), xs, 0.0))
    return jnp.concatenate(blocks, axis=0).astype(BF16)


def _swa_unstack(blocks):
    pairs = []
    for p in range(4):
        halves = []
        for hh in range(2):
            h = 2 * p + hh
            blk = jnp.where(_swa_kv_lanes(h), blocks[h], 0.0)
            halves.append(blk if hh == h // SWA_GROUP else pltpu.roll(blk, HEAD_DIM, 1))
        pairs.append(halves[0] + halves[1])
    return jnp.concatenate(pairs, axis=1)


def _swa_stacked_params(sink_ref, bias_ref, heads):
    bias = jnp.concatenate([bias_ref[h] for h in heads], axis=0)
    sink = jnp.concatenate([jnp.full((BLOCK, 1), sink_ref[0, h], F32) for h in heads], axis=0)
    return bias, sink


def _swa_scores(qs, kp, kc, bias, sink, first):
    sp = _dot_nt(qs, kp) * SCALE64 + bias[:, :BLOCK]
    sp = jnp.where(first, NEG, sp)
    sc = _dot_nt(qs, kc) * SCALE64 + bias[:, BLOCK:]
    m = jnp.maximum(jnp.maximum(jnp.max(sp, axis=1, keepdims=True), jnp.max(sc, axis=1, keepdims=True)), sink)
    pp = jnp.exp(sp - m)
    pc = jnp.exp(sc - m)
    ps = jnp.exp(sink - m)
    den = jnp.sum(pp, axis=1, keepdims=True) + jnp.sum(pc, axis=1, keepdims=True) + ps
    return pp / den, pc / den, ps / den


def _swa_fwd(proj, sinks, bias_tab, B, S, hosted=None):
    nb = S // BLOCK
    T = B * S

    def body(sink_ref, q_ref, kp_ref, kc_ref, vp_ref, vc_ref, bias_ref, o_ref):
        i = pl.program_id(1)
        first = jnp.full((SWA_GROUP * BLOCK, BLOCK), i, jnp.int32) == 0
        q = q_ref[...].astype(F32)
        blocks = []
        for g in range(SWA_Q_HEADS // SWA_GROUP):
            heads = range(SWA_GROUP * g, SWA_GROUP * (g + 1))
            bias, sink = _swa_stacked_params(sink_ref, bias_ref, heads)
            wp, wc, _ = _swa_scores(_swa_stack(q, heads), kp_ref[...], kc_ref[...], bias, sink, first)
            o = _dot(wp.astype(BF16), vp_ref[...]) + _dot(wc.astype(BF16), vc_ref[...])
            blocks += [o[t * BLOCK:(t + 1) * BLOCK] for t in range(SWA_GROUP)]
        o_ref[...] = _swa_unstack(blocks).astype(o_ref.dtype)

    blk = (BLOCK, BLOCK)
    wide = (BLOCK, 4 * BLOCK)
    outs, moved = _hosted_call(
        body, (B, nb),
        [pl.BlockSpec(memory_space=pltpu.SMEM),
         pl.BlockSpec(wide, lambda b, i: (b * nb + i, COL_QA // 4)),
         pl.BlockSpec(blk, lambda b, i: (b * nb + jnp.maximum(i - 1, 0), COL_KA)),
         pl.BlockSpec(blk, lambda b, i: (b * nb + i, COL_KA)),
         pl.BlockSpec(blk, lambda b, i: (b * nb + jnp.maximum(i - 1, 0), COL_VA)),
         pl.BlockSpec(blk, lambda b, i: (b * nb + i, COL_VA)),
         pl.BlockSpec((SWA_Q_HEADS, BLOCK, 2 * BLOCK), lambda b, i: (0, 0, 0))],
        [pl.BlockSpec(wide, lambda b, i: (b * nb + i, 0))],
        [jax.ShapeDtypeStruct((T, 512), BF16)], [], hosted, "swa_fwd",
        (sinks, proj, proj, proj, proj, proj, bias_tab))
    return outs[0], moved


def _swa_bwd(proj, dy, sinks, bias_tab, B, S, hosted=None):
    nb = S // BLOCK
    T = B * S
    H = SWA_Q_HEADS

    def body(sink_ref, q_ref, kp_ref, kc_ref, vp_ref, vc_ref, do_ref, bias_ref,
             dq_ref, dk_ref, dv_ref, dbias_ref, dsink_ref):
        b = pl.program_id(0)
        i = pl.program_id(1)

        @pl.when((b == 0) & (i == 0))
        def _():
            dbias_ref[...] = jnp.zeros_like(dbias_ref)
            dsink_ref[...] = jnp.zeros_like(dsink_ref)

        @pl.when(i == 0)
        def _():
            dk_ref[...] = jnp.zeros_like(dk_ref)
            dv_ref[...] = jnp.zeros_like(dv_ref)

        first = jnp.full((SWA_ROWS, BLOCK), i, jnp.int32) == 0
        heads = range(H)
        bias, sink = _swa_stacked_params(sink_ref, bias_ref, heads)
        kp, kc, vp, vc = kp_ref[...], kc_ref[...], vp_ref[...], vc_ref[...]
        qs = _swa_stack(q_ref[...].astype(F32), heads)
        dos = _swa_stack(do_ref[...].astype(F32), heads)
        wp, wc, ws = _swa_scores(qs, kp, kc, bias, sink, first)
        dwp = _dot_nt(dos, vp)
        dwc = _dot_nt(dos, vc)
        dsum = jnp.sum(wp * dwp, axis=1, keepdims=True) + jnp.sum(wc * dwc, axis=1, keepdims=True)
        dsp = wp * (dwp - dsum)
        dsc = wc * (dwc - dsum)
        dsk = -ws * dsum
        dsinks = []
        for h in range(H):
            rows = slice(h * BLOCK, (h + 1) * BLOCK)
            dsinks.append(jnp.broadcast_to(jnp.sum(dsk[rows], axis=0, keepdims=True), (1, 128)))
            dbias_ref[h] += jnp.concatenate([dsp[rows], dsc[rows]], axis=1)
        dsink_ref[...] += jnp.concatenate(dsinks, axis=0)
        dspb = dsp.astype(BF16)
        dscb = dsc.astype(BF16)
        dq = _dot(dspb, kp) + _dot(dscb, kc)
        dq_ref[...] = (_swa_unstack([dq[h * BLOCK:(h + 1) * BLOCK] for h in heads]) * SCALE64).astype(dq_ref.dtype)
        cur = pl.ds(pl.multiple_of(i * BLOCK, BLOCK), BLOCK)
        prev = pl.ds(pl.multiple_of(jnp.maximum(i - 1, 0) * BLOCK, BLOCK), BLOCK)
        dk_ref[prev, :] += _dot_tn(dspb, qs) * SCALE64
        dk_ref[cur, :] += _dot_tn(dscb, qs) * SCALE64
        dv_ref[prev, :] += _dot_tn(wp.astype(BF16), dos)
        dv_ref[cur, :] += _dot_tn(wc.astype(BF16), dos)

    blk = (BLOCK, BLOCK)
    wide = (BLOCK, 4 * BLOCK)
    kv_out = pl.BlockSpec((S, BLOCK), lambda b, i: (b, 0))
    full_bias = pl.BlockSpec((H, BLOCK, 2 * BLOCK), lambda b, i: (0, 0, 0))
    outs, moved = _hosted_call(
        body, (B, nb),
        [pl.BlockSpec(memory_space=pltpu.SMEM),
         pl.BlockSpec(wide, lambda b, i: (b * nb + i, COL_QA // 4)),
         pl.BlockSpec(blk, lambda b, i: (b * nb + jnp.maximum(i - 1, 0), COL_KA)),
         pl.BlockSpec(blk, lambda b, i: (b * nb + i, COL_KA)),
         pl.BlockSpec(blk, lambda b, i: (b * nb + jnp.maximum(i - 1, 0), COL_VA)),
         pl.BlockSpec(blk, lambda b, i: (b * nb + i, COL_VA)),
         pl.BlockSpec(wide, lambda b, i: (b * nb + i, 0)),
         full_bias],
        [pl.BlockSpec(wide, lambda b, i: (b * nb + i, 0)),
         kv_out, kv_out, full_bias,
         pl.BlockSpec((H, 128), lambda b, i: (0, 0))],
        [jax.ShapeDtypeStruct((T, 512), BF16),
         jax.ShapeDtypeStruct((T, BLOCK), F32), jax.ShapeDtypeStruct((T, BLOCK), F32),
         jax.ShapeDtypeStruct((H, BLOCK, 2 * BLOCK), F32), jax.ShapeDtypeStruct((H, 128), F32)],
        [], hosted, "swa_bwd", (sinks, proj, proj, proj, proj, proj, dy, bias_tab))
    return (*outs, moved)


SB_TILE = 256


SB_HEADS = 4
SB_LANES = SB_HEADS * HEAD_DIM
SB_ROWS = SB_HEADS * SB_TILE


def _sb_logits(z, tri):
    sp = jnp.log(1.0 + jnp.exp(-jnp.abs(z)))
    ls = jnp.minimum(z, 0.0) - sp
    l1m = ls - z
    if tri is not None:
        l1m = jnp.where(tri, l1m, 0.0)
    return ls, l1m


def _sb_masks():
    lane = lax.broadcasted_iota(jnp.int32, (SB_TILE, SB_LANES), 1)
    hm = [(lane // HEAD_DIM) == h for h in range(SB_HEADS)]
    row = lax.broadcasted_iota(jnp.int32, (SB_ROWS, SB_TILE), 0) % SB_TILE
    col = lax.broadcasted_iota(jnp.int32, (SB_ROWS, SB_TILE), 1)
    return lane, hm, row, col


def _sb_stack(x, hm):
    return jnp.concatenate([jnp.where(m, x, 0) for m in hm], axis=0)


def _sb_unstack(x, hm):
    return sum(jnp.where(m, x[h * SB_TILE:(h + 1) * SB_TILE], 0.0) for h, m in enumerate(hm))


def _sb_fwd(proj, B, S, hosted=None):
    nt = S // SB_TILE
    T = B * S

    ng = 512 // SB_LANES

    def body(*refs):
        q_refs, k_refs, v_refs = refs[:ng], refs[ng:2 * ng], refs[2 * ng:3 * ng]
        o_ref, r_ref = refs[3 * ng:]
        i = pl.program_id(1)
        _, hm, row, col = _sb_masks()
        tri = col < row
        later = jnp.concatenate([(row[:SB_TILE] > col[:SB_TILE]).astype(BF16)] * 2, axis=0)
        qs = [_sb_stack(q_ref[...] * SCALE64, hm) for q_ref in q_refs]

        def tile(j, carry, mask):
            rows = pl.ds(pl.multiple_of(j * SB_TILE, SB_TILE), SB_TILE)
            out = []
            for g in range(ng):
                acc, c = carry[g]
                ls, l1m = _sb_logits(_dot_nt(qs[g], k_refs[g][rows, :]), mask)
                a = jnp.exp(ls + c + _dot_split(l1m, later))
                if mask is not None:
                    a = jnp.where(mask, a, 0.0)
                pv = _dot(a.astype(BF16), v_refs[g][rows, :])
                out.append((acc + _sb_unstack(pv, hm), c + jnp.sum(l1m, axis=1, keepdims=True)))
            return tuple(out)

        zero = (jnp.zeros((SB_TILE, SB_LANES), F32), jnp.zeros((SB_ROWS, 1), F32))
        carry = tile(i, (zero,) * ng, tri)
        carry = lax.fori_loop(0, i, lambda it, cr: tile(i - 1 - it, cr, None), carry)
        o_ref[...] = jnp.concatenate([acc for acc, _ in carry], axis=1).astype(o_ref.dtype)
        r_ref[...] = jnp.concatenate(
            [_sb_unstack(jnp.broadcast_to(c, (SB_ROWS, SB_LANES)), hm) for _, c in carry], axis=1)

    blk = (SB_TILE, SB_LANES)
    cq, ck, cv = (c * BLOCK // SB_LANES for c in (COL_QB, COL_KB, COL_VB))
    wide = pl.BlockSpec((SB_TILE, 512), lambda b, i: (b * nt + i, 0))
    outs, moved = _hosted_call(
        body, (B, nt),
        [pl.BlockSpec(blk, lambda b, i, g=g: (b * nt + i, cq + g)) for g in range(ng)]
        + [pl.BlockSpec((S, SB_LANES), lambda b, i, g=g: (b, ck + g)) for g in range(ng)]
        + [pl.BlockSpec((S, SB_LANES), lambda b, i, g=g: (b, cv + g)) for g in range(ng)],
        [wide, wide],
        [jax.ShapeDtypeStruct((T, 512), BF16), jax.ShapeDtypeStruct((T, 512), F32)], [], hosted, "sb_fwd",
        (proj,) * (3 * ng))
    return outs[0], outs[1], moved


def _sb_bwd(proj, dy, rtot, B, S, hosted=None):
    nt = S // SB_TILE
    T = B * S

    ng = 512 // SB_LANES

    def body(*refs):
        q_refs, k_refs, v_refs = refs[:ng], refs[ng:2 * ng], refs[2 * ng:3 * ng]
        do_ref, r_ref, dq_ref, dk_ref, dv_ref = refs[3 * ng:]
        i = pl.program_id(1)

        @pl.when(i == 0)
        def _():
            dk_ref[...] = jnp.zeros_like(dk_ref)
            dv_ref[...] = jnp.zeros_like(dv_ref)

        lane, hm, row, col = _sb_masks()
        tri = col < row
        later = jnp.concatenate([(row[:SB_TILE] > col[:SB_TILE]).astype(BF16)] * 2, axis=0)
        earlier = (row[:SB_TILE] < col[:SB_TILE]).astype(BF16)
        qs, dos, rs = [], [], []
        for g in range(ng):
            lanes = slice(g * SB_LANES, (g + 1) * SB_LANES)
            qs.append(_sb_stack(q_refs[g][...] * SCALE64, hm))
            dos.append(_sb_stack(do_ref[:, lanes], hm))
            r = r_ref[:, lanes]
            rs.append(jnp.concatenate([jnp.sum(jnp.where(lane == h * HEAD_DIM, r, 0.0), axis=1, keepdims=True)
                                       for h in range(SB_HEADS)], axis=0))

        def tile(j, carry, mask):
            rows = pl.ds(pl.multiple_of(j * SB_TILE, SB_TILE), SB_TILE)
            out = []
            for g in range(ng):
                lanes = slice(g * SB_LANES, (g + 1) * SB_LANES)
                dq, lsum, psum = carry[g]
                kj = k_refs[g][rows, :]
                vj = v_refs[g][rows, :]
                ls, l1m = _sb_logits(_dot_nt(qs[g], kj), mask)
                sig = jnp.exp(ls)
                lsum = lsum + jnp.sum(l1m, axis=1, keepdims=True)
                a = jnp.exp(ls + (rs[g] - lsum) + _dot_split(l1m, later))
                if mask is not None:
                    a = jnp.where(mask, a, 0.0)
                de = _dot_nt(dos[g], vj) * a
                pre = psum + _dot(de.astype(BF16), earlier)
                dz = de - sig * (de + pre)
                if mask is not None:
                    dz = jnp.where(mask, dz, 0.0)
                dz = dz.astype(BF16)
                dk_ref[rows, lanes] += _dot_tn(dz, qs[g])
                dv_ref[rows, lanes] += _dot_tn(a.astype(BF16), dos[g])
                out.append((dq + _sb_unstack(_dot(dz, kj), hm), lsum, psum + jnp.sum(de, axis=1, keepdims=True)))
            return tuple(out)

        zero = jnp.zeros((SB_ROWS, 1), F32)
        init = ((jnp.zeros((SB_TILE, SB_LANES), F32), zero, zero),) * ng
        carry = lax.fori_loop(0, i, lambda j, c: tile(j, c, None), init)
        carry = tile(i, carry, tri)
        dq_ref[...] = (jnp.concatenate([c[0] for c in carry], axis=1) * SCALE64).astype(dq_ref.dtype)

    blk = (SB_TILE, SB_LANES)
    cq, ck, cv = (c * BLOCK // SB_LANES for c in (COL_QB, COL_KB, COL_VB))
    wide = pl.BlockSpec((SB_TILE, 512), lambda b, i: (b * nt + i, 0))
    kv_out = pl.BlockSpec((S, 512), lambda b, i: (b, 0))
    outs, moved = _hosted_call(
        body, (B, nt),
        [pl.BlockSpec(blk, lambda b, i, g=g: (b * nt + i, cq + g)) for g in range(ng)]
        + [pl.BlockSpec((S, SB_LANES), lambda b, i, g=g: (b, ck + g)) for g in range(ng)]
        + [pl.BlockSpec((S, SB_LANES), lambda b, i, g=g: (b, cv + g)) for g in range(ng)]
        + [wide, wide],
        [wide, kv_out, kv_out],
        [jax.ShapeDtypeStruct((T, 512), BF16),
         jax.ShapeDtypeStruct((T, 512), F32), jax.ShapeDtypeStruct((T, 512), F32)], [], hosted, "sb_bwd",
        (proj,) * (3 * ng) + (dy, rtot))
    return outs[0], outs[1], outs[2], moved


def _mem_weights(q, mk):
    z = _dot_nt(q, mk) * SCALE128
    e = jnp.exp(z - jnp.max(z, axis=1, keepdims=True))
    return e / jnp.sum(e, axis=1, keepdims=True)


def _mem_fwd(proj, mkv, B, S, M):
    tq = 512
    nq = S // tq
    T = B * S
    Hm = MEM_HEADS

    def body(q0, q1, q2, q3, mk_ref, mv_ref, o_ref):
        outs = []
        for h, q_ref in enumerate((q0, q1, q2, q3)):
            cols = slice(h * 128, (h + 1) * 128)
            w = _mem_weights(q_ref[...], mk_ref[:, cols])
            outs.append(_dot(w.astype(BF16), mv_ref[:, cols]))
        o_ref[...] = jnp.concatenate(outs, axis=1).astype(o_ref.dtype)

    return pl.pallas_call(
        body,
        grid=(B, nq),
        in_specs=[pl.BlockSpec((tq, 128), lambda b, i, h=h: (b * nq + i, COL_QM + h)) for h in range(Hm)]
        + [pl.BlockSpec((M, 512), lambda b, i: (b, 0)), pl.BlockSpec((M, 512), lambda b, i: (b, 1))],
        out_specs=pl.BlockSpec((tq, 512), lambda b, i: (b * nq + i, 0)),
        out_shape=jax.ShapeDtypeStruct((T, 512), BF16),
        name="mem_fwd",
    )(proj, proj, proj, proj, mkv, mkv)


def _mem_bwd(proj, mkv, dy, B, S, M):
    tq = 512
    nq = S // tq
    T = B * S
    Hm = MEM_HEADS

    def body(q0, q1, q2, q3, mk_ref, mv_ref, do_ref, dq_ref, dmk_ref, dmv_ref):
        i = pl.program_id(1)
        dqs, dmks, dmvs = [], [], []
        for h, q_ref in enumerate((q0, q1, q2, q3)):
            cols = slice(h * 128, (h + 1) * 128)
            q = q_ref[...]
            do = do_ref[:, cols]
            mk = mk_ref[:, cols]
            w = _mem_weights(q, mk)
            dw = _dot_nt(do, mv_ref[:, cols])
            ds = (w * (dw - jnp.sum(w * dw, axis=1, keepdims=True))).astype(BF16)
            dqs.append(_dot(ds, mk))
            dmks.append(_dot_tn(ds, q))
            dmvs.append(_dot_tn(w.astype(BF16), do))
        dq_ref[...] = (jnp.concatenate(dqs, axis=1) * SCALE128).astype(dq_ref.dtype)

        @pl.when(i == 0)
        def _():
            dmk_ref[...] = jnp.zeros_like(dmk_ref)
            dmv_ref[...] = jnp.zeros_like(dmv_ref)

        dmk_ref[...] += jnp.concatenate(dmks, axis=1) * SCALE128
        dmv_ref[...] += jnp.concatenate(dmvs, axis=1)

    q_spec = pl.BlockSpec((tq, 512), lambda b, i: (b * nq + i, 0))
    m_out = pl.BlockSpec((M, 512), lambda b, i: (b, 0))
    return pl.pallas_call(
        body,
        grid=(B, nq),
        in_specs=[pl.BlockSpec((tq, 128), lambda b, i, h=h: (b * nq + i, COL_QM + h)) for h in range(Hm)]
        + [pl.BlockSpec((M, 512), lambda b, i: (b, 0)), pl.BlockSpec((M, 512), lambda b, i: (b, 1)), q_spec],
        out_specs=[q_spec, m_out, m_out],
        out_shape=[jax.ShapeDtypeStruct((T, 512), BF16),
                   jax.ShapeDtypeStruct((B * M, 512), F32), jax.ShapeDtypeStruct((B * M, 512), F32)],
        name="mem_bwd",
    )(proj, proj, proj, proj, mkv, mkv, dy)


def _all_gather(blk, name):
    R, C = blk.shape

    def body(x_ref, out_ref, send_sems, recv_sems, local_sem):
        x, y, c = _mesh_pos()
        me, sibling = (x, y, c), (x, y, 1 - c)
        chips = [(1 - x, y), (x, 1 - y), (1 - x, 1 - y)]

        def slot(px, py, pc):
            return out_ref.at[4 * px + 2 * py + pc]

        def copy(k, block, to, src=None):
            return pltpu.make_async_remote_copy(
                src_ref=slot(*block) if src is None else src, dst_ref=slot(*block),
                send_sem=send_sems.at[k], recv_sem=recv_sems.at[k],
                device_id=to, device_id_type=pl.DeviceIdType.MESH)

        mine = pltpu.make_async_copy(x_ref, slot(*me), local_sem)
        mine.start()
        first = [copy(0, me, sibling, src=x_ref)]
        first += [copy(1 + j, me, (*chip, c), src=x_ref) for j, chip in enumerate(chips)]
        for cp in first:
            cp.start()
        passed = [copy(4 + j, (*chip, c), sibling) for j, chip in enumerate(chips)]
        for j, chip in enumerate(chips):
            copy(1 + j, (*chip, c), me).wait_recv()
            passed[j].start()
        copy(0, sibling, me).wait_recv()
        for j, chip in enumerate(chips):
            copy(4 + j, (*chip, 1 - c), me).wait_recv()
        for cp in first + passed:
            cp.wait_send()
        mine.wait()

    return pl.pallas_call(
        body,
        in_specs=[pl.BlockSpec(memory_space=pl.ANY)],
        out_specs=pl.BlockSpec(memory_space=pl.ANY),
        out_shape=jax.ShapeDtypeStruct((N_DEV, R, C), blk.dtype),
        scratch_shapes=[pltpu.SemaphoreType.DMA((7,)), pltpu.SemaphoreType.DMA((7,)), pltpu.SemaphoreType.DMA],
        name=name,
    )(blk)


def _all_gather_relayed(blk, name):
    R, C = blk.shape
    R2 = R // 2

    def body(x_ref, out_ref, send_sems, recv_sems, local_sem):
        x, y, c = _mesh_pos()
        me, sib = (x, y, c), (x, y, 1 - c)
        nx, ny, dg = (1 - x, y, c), (x, 1 - y, c), (1 - x, 1 - y, c)

        def slot(p, half=None):
            ref = out_ref.at[4 * p[0] + 2 * p[1] + p[2]]
            return ref if half is None else ref.at[pl.ds(half * R2, R2), :]

        def copy(k, block, to, half=None, own=False):
            src = slot(block, half)
            if own:
                src = x_ref if half is None else x_ref.at[pl.ds(half * R2, R2), :]
            return pltpu.make_async_remote_copy(
                src_ref=src, dst_ref=slot(block, half), send_sem=send_sems.at[k], recv_sem=recv_sems.at[k],
                device_id=to, device_id_type=pl.DeviceIdType.MESH)

        def other(p):
            return (p[0], p[1], 1 - c)

        mine = pltpu.make_async_copy(x_ref, slot(me), local_sem)
        mine.start()
        sends = [copy(1, me, nx, 0, own=True), copy(3, me, ny, 1, own=True), copy(0, me, sib, own=True),
                 copy(2, me, nx, 1, own=True), copy(4, me, ny, 0, own=True)]
        for cp in sends:
            cp.start()
        copy(1, nx, me, 0).wait_recv()
        sends.append(copy(5, nx, ny, 0))
        sends[-1].start()
        copy(3, ny, me, 1).wait_recv()
        sends.append(copy(6, ny, nx, 1))
        sends[-1].start()
        copy(2, nx, me, 1).wait_recv()
        sends.append(copy(7, nx, sib))
        sends[-1].start()
        copy(4, ny, me, 0).wait_recv()
        sends.append(copy(8, ny, sib))
        sends[-1].start()
        copy(5, dg, me, 0).wait_recv()
        copy(6, dg, me, 1).wait_recv()
        sends.append(copy(9, dg, sib))
        sends[-1].start()
        copy(0, sib, me).wait_recv()
        for k, p in ((7, nx), (8, ny), (9, dg)):
            copy(k, other(p), me).wait_recv()
        for cp in sends:
            cp.wait_send()
        mine.wait()

    return pl.pallas_call(
        body,
        in_specs=[pl.BlockSpec(memory_space=pl.ANY)],
        out_specs=pl.BlockSpec(memory_space=pl.ANY),
        out_shape=jax.ShapeDtypeStruct((N_DEV, R, C), blk.dtype),
        scratch_shapes=[pltpu.SemaphoreType.DMA((10,)), pltpu.SemaphoreType.DMA((10,)), pltpu.SemaphoreType.DMA],
        name=name,
    )(blk)


_HBM = pl.BlockSpec(memory_space=pltpu.HBM)
_SEM = pl.BlockSpec(memory_space=pltpu.SEMAPHORE)


def _scatter_start(parts, land, window, carried, name):
    hosted = _Hosted(scatters=[parts], window=window)

    def body(p_ref, land_ref, c_ref, send_sems, recv_sems, local_sems, p_thru, land_thru, c_thru):
        for cp in hosted.copies([p_ref], [land_ref], send_sems, recv_sems, local_sems):
            cp.start()

    sems = (pltpu.SemaphoreType.DMA((7,)), pltpu.SemaphoreType.DMA((7,)), pltpu.SemaphoreType.DMA((1,)))
    hbm = lambda a: pltpu.HBM(a.shape, a.dtype)
    outs = pl.pallas_call(
        body, name=name,
        out_shape=sems + (hbm(parts), hbm(land), hbm(carried)),
        in_specs=(_HBM, _HBM, _HBM), out_specs=(_SEM, _SEM, _SEM, _HBM, _HBM, _HBM),
        input_output_aliases={0: 3, 1: 4, 2: 5},
        compiler_params=pltpu.CompilerParams(has_side_effects=pltpu.SideEffectType.DATAFLOW_SIDE_EFFECTING),
    )(pltpu.with_memory_space_constraint(parts, pltpu.HBM),
      pltpu.with_memory_space_constraint(land, pltpu.HBM),
      pltpu.with_memory_space_constraint(carried, pltpu.HBM))
    return (outs[:4], window), outs[4], outs[5]


def _scatter_wait(flight, land, after, name):
    (send_sems, recv_sems, local_sems, p_thru), window = flight
    hosted = _Hosted(scatters=[p_thru], window=window)

    def body(p_ref, land_ref, send, recv, local, after_ref, p_dead, got_ref):
        for cp in hosted.copies([p_ref], [land_ref], send, recv, local):
            cp.wait()

    hbm = lambda a: pltpu.HBM(a.shape, a.dtype)
    return pl.pallas_call(
        body, name=name,
        out_shape=(hbm(p_thru), hbm(land)),
        in_specs=(_HBM, _HBM, _SEM, _SEM, _SEM, pl.BlockSpec(memory_space=pl.ANY)), out_specs=(_HBM, _HBM),
        input_output_aliases={0: 0, 1: 1},
        compiler_params=pltpu.CompilerParams(has_side_effects=pltpu.SideEffectType.DATAFLOW_SIDE_EFFECTING),
    )(p_thru, land, send_sems, recv_sems, local_sems, after)[1]


def _adamw(parts, w, m, v, name):
    R, C = w.shape
    tr = R
    for cand in (368, 352, 256, 176, 128, 64, 32, 16, 8):
        if R % cand == 0 and cand * C * 4 <= 1536 * 1024:
            tr = cand
            break
    c1 = 1.0 - ADAM_B1 ** ADAM_STEP
    c2 = 1.0 - ADAM_B2 ** ADAM_STEP

    def body(p_ref, w_ref, m_ref, v_ref, g_ref, d_ref, nm_ref, nv_ref):
        g = p_ref[0].astype(F32)
        for d in range(1, N_DEV):
            g = g + p_ref[d].astype(F32)
        nm = ADAM_B1 * m_ref[...] + (1.0 - ADAM_B1) * g
        nv = ADAM_B2 * v_ref[...] + (1.0 - ADAM_B2) * (g * g)
        g_ref[...] = g
        nm_ref[...] = nm
        nv_ref[...] = nv
        d_ref[...] = -ADAM_LR * ((nm / c1) / (jnp.sqrt(nv / c2) + ADAM_EPS) + ADAM_WD * w_ref[...])

    row = pl.BlockSpec((tr, C), lambda i: (i, 0))
    out = jax.ShapeDtypeStruct((R, C), F32)
    return pl.pallas_call(
        body,
        grid=(R // tr,),
        in_specs=[pl.BlockSpec((N_DEV, tr, C), lambda i: (0, i, 0)), row, row, row],
        out_specs=[row] * 4,
        out_shape=[out] * 4,
        compiler_params=_params(),
        name=name,
    )(parts, w, m, v)


def _col_shards(g):
    R, C8 = g.shape
    return g.reshape(R, N_DEV, C8 // N_DEV).transpose(1, 0, 2)


def _row_shards(g):
    R8, C = g.shape
    return g.reshape(N_DEV, R8 // N_DEV, C)


def _cols_full(gathered):
    n, R, C = gathered.shape
    return gathered.transpose(1, 0, 2).reshape(R, n * C)


_BIG = ("w_in", "w_mem_kv", "w_branch_swa", "w_branch_sb", "w_branch_mem", "w_out", "w_gate", "w_up", "w_down")
_COL_SHARDED = ("w_branch_swa", "w_branch_sb", "w_branch_mem")
_TRANSPOSED = ("w_in", "w_gate", "w_up")
_SMALL = ("ln_mix_pre", "ln_mix_post", "swa_sinks", "rel_bias", "ln_mem", "ln_ffn_pre", "ln_ffn_post")
_ORDER = ("ln_mix_pre", "ln_mix_post", "w_in", "swa_sinks", "rel_bias", "ln_mem", "w_mem_kv", "w_branch_swa",
          "w_branch_sb", "w_branch_mem", "w_out", "ln_ffn_pre", "ln_ffn_post", "w_gate", "w_up", "w_down")


def _pack_small(d, last=None):
    rows = [d["ln_mix_pre"], d["ln_mix_post"], d["ln_mem"], d["ln_ffn_pre"], d["ln_ffn_post"],
            jnp.pad(d["swa_sinks"].reshape(1, -1), ((0, 0), (0, D_MODEL - SWA_Q_HEADS))),
            jnp.pad(d["rel_bias"].reshape(1, -1), ((0, 0), (0, D_MODEL - N_BUCKETS * SWA_Q_HEADS))),
            jnp.zeros((1, D_MODEL), F32) if last is None else last]
    return jnp.concatenate([r.astype(F32) for r in rows], axis=0)


def _unpack_small(a):
    return dict(ln_mix_pre=a[0:1], ln_mix_post=a[1:2], ln_mem=a[2:3], ln_ffn_pre=a[3:4], ln_ffn_post=a[4:5],
                swa_sinks=a[5:6, :SWA_Q_HEADS],
                rel_bias=a[6, :N_BUCKETS * SWA_Q_HEADS].reshape(N_BUCKETS, SWA_Q_HEADS))


def kernel(x, mem, ln_mix_pre, ln_mix_post, w_in, swa_sinks, rel_bias, ln_mem, w_mem_kv, w_branch_swa, w_branch_sb, w_branch_mem, w_out, ln_ffn_pre, ln_ffn_post, w_gate, w_up, w_down, loss_target, m_ln_mix_pre, m_ln_mix_post, m_w_in, m_swa_sinks, m_rel_bias, m_ln_mem, m_w_mem_kv, m_w_branch_swa, m_w_branch_sb, m_w_branch_mem, m_w_out, m_ln_ffn_pre, m_ln_ffn_post, m_w_gate, m_w_up, m_w_down, v_ln_mix_pre, v_ln_mix_post, v_w_in, v_swa_sinks, v_rel_bias, v_ln_mem, v_w_mem_kv, v_w_branch_swa, v_w_branch_sb, v_w_branch_mem, v_w_out, v_ln_ffn_pre, v_ln_ffn_post, v_w_gate, v_w_up, v_w_down):
    w = dict(ln_mix_pre=ln_mix_pre, ln_mix_post=ln_mix_post, w_in=w_in[0], swa_sinks=swa_sinks, rel_bias=rel_bias,
             ln_mem=ln_mem, w_mem_kv=w_mem_kv[0], w_branch_swa=w_branch_swa[0], w_branch_sb=w_branch_sb[0],
             w_branch_mem=w_branch_mem[0], w_out=w_out[0], ln_ffn_pre=ln_ffn_pre, ln_ffn_post=ln_ffn_post,
             w_gate=w_gate[0], w_up=w_up[0], w_down=w_down[0])
    mom = dict(ln_mix_pre=m_ln_mix_pre, ln_mix_post=m_ln_mix_post, w_in=m_w_in[0], swa_sinks=m_swa_sinks,
               rel_bias=m_rel_bias, ln_mem=m_ln_mem, w_mem_kv=m_w_mem_kv[0], w_branch_swa=m_w_branch_swa[0],
               w_branch_sb=m_w_branch_sb[0], w_branch_mem=m_w_branch_mem[0], w_out=m_w_out[0],
               ln_ffn_pre=m_ln_ffn_pre, ln_ffn_post=m_ln_ffn_post, w_gate=m_w_gate[0], w_up=m_w_up[0],
               w_down=m_w_down[0])
    var = dict(ln_mix_pre=v_ln_mix_pre, ln_mix_post=v_ln_mix_post, w_in=v_w_in[0], swa_sinks=v_swa_sinks,
               rel_bias=v_rel_bias, ln_mem=v_ln_mem, w_mem_kv=v_w_mem_kv[0], w_branch_swa=v_w_branch_swa[0],
               w_branch_sb=v_w_branch_sb[0], w_branch_mem=v_w_branch_mem[0], w_out=v_w_out[0],
               ln_ffn_pre=v_ln_ffn_pre, ln_ffn_post=v_ln_ffn_post, w_gate=v_w_gate[0], w_up=v_w_up[0],
               w_down=v_w_down[0])
    B, S, D = x.shape
    M = mem.shape[1]
    T = B * S
    F = D_FF
    x2 = x.reshape(T, D)
    mem2 = mem.reshape(B * M, D)
    t2 = loss_target.reshape(T, D)
    buckets = jnp.asarray(_swa_buckets())
    for d in (w, mom, var):
        for n in _TRANSPOSED:
            d[n] = d[n].T
    wb = {n: w[n].astype(BF16) for n in _BIG}
    full = {}

    def landed(names, got):
        for n, g in zip(names, got):
            full[n] = _cols_full(g) if n in _COL_SHARDED else g.reshape(-1, g.shape[-1])

    def shards(n, g):
        return _col_shards(g) if n in _COL_SHARDED else _row_shards(g)

    landed(["w_in"], [_all_gather_relayed(wb["w_in"], "ag_w_in")])
    u = _rms_fwd(x2, ln_mix_pre, "rms_mix_pre")
    early = ["w_mem_kv", "w_branch_swa", "w_branch_sb", "w_branch_mem"]
    proj, got = _matmul([(u, full["w_in"])], "nt", BF16, 512, IN_W // 2, D, "proj_in",
                        hosted=_Hosted(gathers=[wb[n] for n in early]))
    landed(early, got)
    mn = _rms_fwd(mem2, ln_mem, "rms_mem")
    mkv = _matmul([(mn, full["w_mem_kv"])], "nn", BF16, 512, 1024, D, "proj_mem")
    bias_tab = _swa_bias_table(rel_bias, buckets)
    y_swa, got = _swa_fwd(proj, swa_sinks, bias_tab, B, S, hosted=_Hosted(gathers=[wb["w_out"]]))
    landed(["w_out"], got)
    late = ["w_gate", "w_up"]
    y_sb, rtot, got = _sb_fwd(proj, B, S, hosted=_Hosted(gathers=[wb[n] for n in late]))
    landed(late, got)
    y_mem = _mem_fwd(proj, mkv, B, S, M)
    wbs = (full["w_branch_swa"], full["w_branch_sb"], full["w_branch_mem"])
    merged, p_swa, p_sb, p_mem = _branch_gate(proj, (y_swa, y_sb, y_mem), wbs)
    mix, h1, u2 = _post_pre(x2, merged, full["w_out"], ln_mix_post, ln_ffn_pre)
    a, zg, zu, got = _ffn_up(u2, full["w_gate"], full["w_up"], hosted=_Hosted(gathers=[wb["w_down"]]))
    landed(["w_down"], got)
    dffn, dh2, d_ln_ffn_post, loss_row = _loss_head(a, full["w_down"], h1, t2, ln_ffn_post)

    part = {}
    part["w_down"] = _matmul([(a, dffn)], "tn", BF16, F // 2, 1024, 1024, "dw_down")
    dzg, dzu = _ffn_down_bwd(dffn, full["w_down"], zg, zu)
    part["w_gate"] = _matmul([(dzg, u2)], "tn", BF16, F // 2, 1024, 1024, "dw_gate")
    part["w_up"] = _matmul([(dzu, u2)], "tn", BF16, F // 2, 1024, 1024, "dw_up")
    dh1, dmix, d_ln_ffn_pre, d_ln_mix_post = _mid_bwd(dzg, dzu, full["w_gate"], full["w_up"], h1, dh2, mix,
                                                      ln_ffn_pre, ln_mix_post)
    part["w_out"] = _matmul([(merged, dmix)], "tn", BF16, 1024, 1024, 1024, "dw_out")
    (dp_swa, dp_sb, dp_mem, dg0, dg1, dg2, dy_swa, dy_sb, dy_mem) = _gate_bwd(
        dmix, full["w_out"], (p_swa, p_sb, p_mem), proj, wbs)
    part["w_branch_swa"] = _matmul([(y_swa, dp_swa)], "tn", BF16, 512, 1024, 1024, "dw_branch_swa")
    part["w_branch_sb"] = _matmul([(y_sb, dp_sb)], "tn", BF16, 512, 1024, 1024, "dw_branch_sb")
    part["w_branch_mem"] = _matmul([(y_mem, dp_mem)], "tn", BF16, 512, 1024, 1024, "dw_branch_mem")
    dqm, dmk, dmv = _mem_bwd(proj, mkv, dy_mem, B, S, M)
    dmkv = jnp.concatenate([dmk, dmv], axis=1).astype(BF16)
    part["w_mem_kv"] = _matmul([(mn, dmkv)], "tn", BF16, 1024, 1024, 512, "dw_mem_kv")
    dmn = _matmul([(dmkv, full["w_mem_kv"])], "nt", F32, 512, 1024, 1024, "d_mn")
    d_ln_mem = _gain_grad(mem2, dmn)
    behind_swa = ["w_out", "w_branch_swa", "w_branch_sb"]
    dqa, dka, dva, dbias, dsink, got = _swa_bwd(
        proj, dy_swa, swa_sinks, bias_tab, B, S, hosted=_Hosted(scatters=[shards(n, part[n]) for n in behind_swa]))
    recv = dict(zip(behind_swa, got))
    behind_sb = ["w_down", "w_gate", "w_up", "w_branch_mem", "w_mem_kv"]
    dqb, dkb, dvb, got = _sb_bwd(proj, dy_sb, rtot, B, S,
                                 hosted=_Hosted(scatters=[shards(n, part[n]) for n in behind_sb]))
    recv.update(zip(behind_sb, got))
    d_rel_bias = _swa_bias_grad(dbias, buckets)[:, :SWA_Q_HEADS]
    d_sinks = dsink[:, 0].reshape(1, SWA_Q_HEADS)
    dproj = jnp.concatenate([dqa, dka.astype(BF16), dva.astype(BF16), dqb, dkb.astype(BF16), dvb.astype(BF16),
                             dqm, dg0, dg1, dg2], axis=1)
    half = D // 2
    land = lax.empty((N_DEV, IN_W // N_DEV, D), BF16)
    flights = []
    for t in range(2):
        dw_half = _matmul([(dproj, u)], "tn", BF16, IN_W // 2, half, 1024, "dw_in_%d" % t, n_cols=half, n_off=t)
        flight, land, dproj = _scatter_start(_row_shards(dw_half), land, (t * half, half), dproj,
                                             "rs_w_in_start_%d" % t)
        flights.append(flight)
    grad_x, d_ln_mix_pre = _pre_bwd(dproj, full["w_in"], x2, dh1, ln_mix_pre)

    out = {n: _adamw(recv[n], w[n], mom[n], var[n], "adamw_" + n) for n in _BIG if n != "w_in"}
    small_grads = dict(ln_mix_pre=d_ln_mix_pre, ln_mix_post=d_ln_mix_post, swa_sinks=d_sinks, rel_bias=d_rel_bias,
                       ln_mem=d_ln_mem, ln_ffn_pre=d_ln_ffn_pre, ln_ffn_post=d_ln_ffn_post)
    small_parts = _all_gather(_pack_small(small_grads, loss_row), "ag_small")
    res = _adamw(small_parts, _pack_small(w), _pack_small(mom), _pack_small(var), "adamw_small")
    loss = res[0][7, 0]
    small = [_unpack_small(r) for r in res]
    for n in _SMALL:
        out[n] = tuple(s[n] for s in small)
    for t, flight in enumerate(flights):
        land = _scatter_wait(flight, land, res[0], "rs_w_in_wait_%d" % t)
    out["w_in"] = _adamw(land, w["w_in"], mom["w_in"], var["w_in"], "adamw_w_in")
    for n in _TRANSPOSED:
        out[n] = tuple(o.T for o in out[n])

    like = dict(ln_mix_pre=ln_mix_pre, ln_mix_post=ln_mix_post, w_in=w_in, swa_sinks=swa_sinks, rel_bias=rel_bias,
                ln_mem=ln_mem, w_mem_kv=w_mem_kv, w_branch_swa=w_branch_swa, w_branch_sb=w_branch_sb,
                w_branch_mem=w_branch_mem, w_out=w_out, ln_ffn_pre=ln_ffn_pre, ln_ffn_post=ln_ffn_post,
                w_gate=w_gate, w_up=w_up, w_down=w_down)
    result = [loss, grad_x.reshape(B, S, D)]
    for k in range(4):
        result += [out[n][k].reshape(like[n].shape) for n in _ORDER]
    return tuple(result)
```

```python
import functools
import math

import numpy as np
import jax
import jax.numpy as jnp
from jax import lax
from jax.experimental import pallas as pl
from jax.experimental.pallas import tpu as pltpu

F32 = jnp.float32
BF16 = jnp.bfloat16

N_DEV = 8
D_MODEL = 1024
BLOCK = 128
EPS = 1e-6
HEAD_DIM = 64
SWA_Q_HEADS = 8
SWA_WINDOW = 128
N_BUCKETS = 32
MAX_DISTANCE = 128
MEM_HEADS = 4
MEM_HEAD_DIM = 128
D_FF = 2816
IN_W = 5888
COL_QA, COL_KA, COL_VA, COL_QB, COL_KB, COL_VB, COL_QM, COL_GL = 0, 4, 5, 6, 10, 14, 18, 22
SCALE64 = HEAD_DIM ** -0.5
SCALE128 = MEM_HEAD_DIM ** -0.5
NEG = -1e30

ADAM_LR = 0.001
ADAM_B1 = 0.9
ADAM_B2 = 0.999
ADAM_EPS = 1e-08
ADAM_WD = 0.01
ADAM_STEP = 10

VMEM_LIMIT_BYTES = 56 * 1024 * 1024


def _params(**kw):
    return pltpu.CompilerParams(vmem_limit_bytes=VMEM_LIMIT_BYTES, **kw)


def _dot(a, b):
    return jnp.dot(a, b, preferred_element_type=F32)


def _dot_nt(a, b):
    return lax.dot_general(a, b, (((1,), (1,)), ((), ())), preferred_element_type=F32)


def _dot_tn(a, b):
    return lax.dot_general(a, b, (((0,), (0,)), ((), ())), preferred_element_type=F32)


def _dot_split(x, m2):
    hi = x.astype(BF16)
    lo = (x - hi.astype(F32)).astype(BF16)
    return _dot(jnp.concatenate([hi, lo], axis=1), m2)


def _mesh_pos():
    return lax.axis_index("x"), lax.axis_index("y"), lax.axis_index("c")


class _Hosted:
    def __init__(self, gathers=(), scatters=(), window=None):
        self.items = [("g", a) for a in gathers] + [("s", a) for a in scatters]
        self.n = len(self.items)
        self.window = window

    def operands(self):
        return [a for _, a in self.items]

    def specs(self):
        return [pl.BlockSpec(memory_space=pl.ANY)] * self.n

    def out_shapes(self):
        return [jax.ShapeDtypeStruct(((N_DEV,) + a.shape) if kind == "g" else a.shape, a.dtype)
                for kind, a in self.items]

    def scratch(self):
        return [pltpu.SemaphoreType.DMA((7 * self.n,)), pltpu.SemaphoreType.DMA((7 * self.n,)),
                pltpu.SemaphoreType.DMA((self.n,))]

    def copies(self, in_refs, out_refs, send_sems, recv_sems, local_sems):
        x, y, c = _mesh_pos()
        me = 4 * x + 2 * y + c
        out = []
        for t, (kind, _) in enumerate(self.items):
            own = in_refs[t] if kind == "g" else in_refs[t].at[me]
            dst = out_refs[t].at[me]
            if self.window is not None:
                dst = out_refs[t].at[me, :, pl.ds(*self.window)]
            out.append(pltpu.make_async_copy(own, dst, local_sems.at[t]))
            for k in range(1, N_DEV):
                px, py, pc = x ^ (k >> 2), y ^ ((k >> 1) & 1), c ^ (k & 1)
                src = in_refs[t] if kind == "g" else in_refs[t].at[4 * px + 2 * py + pc]
                out.append(pltpu.make_async_remote_copy(
                    src_ref=src, dst_ref=dst,
                    send_sem=send_sems.at[7 * t + k - 1], recv_sem=recv_sems.at[7 * t + k - 1],
                    device_id=(px, py, pc), device_id_type=pl.DeviceIdType.MESH))
        return out


def _host(body, n_in, n_out, hosted, grid):
    if hosted is None:
        return body
    nc = hosted.n

    def wrapped(*refs):
        ins = refs[:n_in]
        cin = refs[n_in:n_in + nc]
        outs = refs[n_in + nc:n_in + nc + n_out]
        cout = refs[n_in + nc + n_out:n_in + 2 * nc + n_out]
        scratch = refs[n_in + 2 * nc + n_out:len(refs) - 3]
        sems = refs[len(refs) - 3:]
        ids = [pl.program_id(d) for d in range(len(grid))]
        first = functools.reduce(lambda a, b: a & b, [i == 0 for i in ids])
        last = functools.reduce(lambda a, b: a & b, [i == g - 1 for i, g in zip(ids, grid)])

        @pl.when(first)
        def _():
            for cp in hosted.copies(cin, cout, *sems):
                cp.start()

        body(*ins, *outs, *scratch)

        @pl.when(last)
        def _():
            for cp in hosted.copies(cin, cout, *sems):
                cp.wait()

    return wrapped


def _hosted_call(body, grid, in_specs, out_specs, out_shape, scratch_shapes, hosted, name, args):
    n_out = len(out_specs)
    if hosted is None:
        outs = pl.pallas_call(body, grid=grid, in_specs=in_specs, out_specs=out_specs, out_shape=out_shape,
                              scratch_shapes=scratch_shapes, compiler_params=_params(), name=name)(*args)
        return list(outs), []
    outs = pl.pallas_call(
        _host(body, len(in_specs), n_out, hosted, grid),
        grid=grid,
        in_specs=list(in_specs) + hosted.specs(),
        out_specs=list(out_specs) + hosted.specs(),
        out_shape=list(out_shape) + hosted.out_shapes(),
        scratch_shapes=list(scratch_shapes) + hosted.scratch(),
        compiler_params=_params(),
        name=name,
    )(*args, *hosted.operands())
    return list(outs[:n_out]), list(outs[n_out:])


_DIMS = {"nn": (((1,), (0,)), ((), ())), "nt": (((1,), (1,)), ((), ())), "tn": (((0,), (0,)), ((), ()))}


def _matmul(pairs, mode, out_dtype, tm, tn, tk, name, hosted=None, n_cols=None, n_off=0):
    a0, b0 = pairs[0]
    if mode == "nn":
        (M, K), N = a0.shape, b0.shape[1]
    elif mode == "nt":
        (M, K), N = a0.shape, b0.shape[0]
    else:
        (K, M), N = a0.shape, b0.shape[1]
    N = N if n_cols is None else n_cols
    tm, tn, tk = min(tm, M), min(tn, N), min(tk, K)
    assert M % tm == 0 and N % tn == 0 and K % tk == 0, (name, M, N, K, tm, tn, tk)
    nm, nn, nk = M // tm, N // tn, K // tk
    npair = len(pairs)
    dims = _DIMS[mode]

    def body(*refs):
        ab = refs[:2 * npair]
        o_ref = refs[2 * npair]
        acc_ref = refs[2 * npair + 1]
        k = pl.program_id(2)
        part = lax.dot_general(ab[0][...], ab[1][...], dims, preferred_element_type=F32)
        for q in range(1, npair):
            part += lax.dot_general(ab[2 * q][...], ab[2 * q + 1][...], dims, preferred_element_type=F32)
        if nk == 1:
            o_ref[...] = part.astype(o_ref.dtype)
        else:
            @pl.when(k == 0)
            def _():
                acc_ref[...] = part

            @pl.when(k > 0)
            def _():
                acc_ref[...] += part

            @pl.when(k == nk - 1)
            def _():
                o_ref[...] = acc_ref[...].astype(o_ref.dtype)

    if mode == "nn":
        a_spec = pl.BlockSpec((tm, tk), lambda n, m, k: (m, k))
        b_spec = pl.BlockSpec((tk, tn), lambda n, m, k: (k, n + n_off))
    elif mode == "nt":
        a_spec = pl.BlockSpec((tm, tk), lambda n, m, k: (m, k))
        b_spec = pl.BlockSpec((tn, tk), lambda n, m, k: (n + n_off, k))
    else:
        a_spec = pl.BlockSpec((tk, tm), lambda n, m, k: (k, m))
        b_spec = pl.BlockSpec((tk, tn), lambda n, m, k: (k, n + n_off))
    args = [t for pr in pairs for t in pr]
    outs, moved = _hosted_call(
        body, (nn, nm, nk), [a_spec, b_spec] * npair, [pl.BlockSpec((tm, tn), lambda n, m, k: (m, n))],
        [jax.ShapeDtypeStruct((M, N), out_dtype)], [pltpu.VMEM((tm, tn) if nk > 1 else (8, 128), F32)],
        hosted, name, args)
    return outs[0] if hosted is None else (outs[0], moved)


def _rms_fwd(x, g, name):
    T, D = x.shape
    tr = min(512, T)

    def body(x_ref, g_ref, u_ref):
        xf = x_ref[...]
        r = lax.rsqrt(jnp.mean(xf * xf, axis=-1, keepdims=True) + EPS)
        u_ref[...] = ((xf * r) * g_ref[...]).astype(u_ref.dtype)

    return pl.pallas_call(
        body,
        grid=(T // tr,),
        in_specs=[pl.BlockSpec((tr, D), lambda i: (i, 0)), pl.BlockSpec((1, D), lambda i: (0, 0))],
        out_specs=pl.BlockSpec((tr, D), lambda i: (i, 0)),
        out_shape=jax.ShapeDtypeStruct((T, D), BF16),
        name=name,
    )(x, g)


def _rms_bwd_terms(xin, g, dy):
    r = lax.rsqrt(jnp.mean(xin * xin, axis=-1, keepdims=True) + EPS)
    xh = xin * r
    dg = jnp.sum(dy * xh, axis=0, keepdims=True)
    dxh = dy * g
    dx = r * (dxh - xh * jnp.mean(dxh * xh, axis=-1, keepdims=True))
    return dx, dg


GATE_TC = 256


def _branch_gate(proj, ys, wbs):
    T = proj.shape[0]
    D = D_MODEL
    tr, tc = min(512, T), GATE_TC
    nc = D // tc
    gl0 = COL_GL * 128 // tc

    def body(*refs):
        y_refs, w_refs = refs[:3], refs[3:6]
        g_refs = refs[6:6 + 3 * nc]
        merged_ref = refs[6 + 3 * nc]
        p_refs = refs[7 + 3 * nc:]
        acc = [jnp.zeros((tr, tc), F32)] * nc
        for j in range(3):
            p = _dot(y_refs[j][...], w_refs[j][...])
            p_refs[j][...] = p.astype(BF16)
            for c in range(nc):
                acc[c] = acc[c] + jax.nn.sigmoid(g_refs[j * nc + c][...].astype(F32)) * p[:, c * tc:(c + 1) * tc]
        merged_ref[...] = jnp.concatenate(acc, axis=1).astype(BF16)

    y_spec = pl.BlockSpec((tr, 512), lambda i: (i, 0))
    w_spec = pl.BlockSpec((512, D), lambda i: (0, 0))
    o_spec = pl.BlockSpec((tr, D), lambda i: (i, 0))
    gl_specs = [pl.BlockSpec((tr, tc), lambda i, b=gl0 + j * nc + c: (i, b)) for j in range(3) for c in range(nc)]
    out = jax.ShapeDtypeStruct((T, D), BF16)
    return pl.pallas_call(
        body,
        grid=(T // tr,),
        in_specs=[y_spec] * 3 + [w_spec] * 3 + gl_specs,
        out_specs=[o_spec] * 4,
        out_shape=[out] * 4,
        compiler_params=_params(),
        name="branch_gate",
    )(*ys, *wbs, *([proj] * (3 * nc)))


def _post_pre(x, merged, w_out, g_post, g_pre):
    T, D = x.shape
    tr = 512

    def body(x_ref, m_ref, w_ref, gp_ref, gq_ref, mix_ref, h1_ref, u2_ref):
        mx = _dot(m_ref[...], w_ref[...])
        mix_ref[...] = mx
        r = lax.rsqrt(jnp.mean(mx * mx, axis=-1, keepdims=True) + EPS)
        h1 = x_ref[...] + (mx * r) * gp_ref[...]
        h1_ref[...] = h1
        r2 = lax.rsqrt(jnp.mean(h1 * h1, axis=-1, keepdims=True) + EPS)
        u2_ref[...] = ((h1 * r2) * gq_ref[...]).astype(u2_ref.dtype)

    row = pl.BlockSpec((tr, D), lambda i: (i, 0))
    vec = pl.BlockSpec((1, D), lambda i: (0, 0))
    f32 = jax.ShapeDtypeStruct((T, D), F32)
    return pl.pallas_call(
        body,
        grid=(T // tr,),
        in_specs=[row, row, pl.BlockSpec((D, D), lambda i: (0, 0)), vec, vec],
        out_specs=[row, row, row],
        out_shape=[f32, f32, jax.ShapeDtypeStruct((T, D), BF16)],
        compiler_params=_params(),
        name="post_pre",
    )(x, merged, w_out, g_post, g_pre)


def _ffn_up(u2, w_gate_t, w_up_t, hosted=None):
    T, D = u2.shape
    F = D_FF
    tm, tn = 512, F // 2

    def body(u_ref, wg_ref, wu_ref, a_ref, zg_ref, zu_ref):
        u = u_ref[...]
        zg = _dot_nt(u, wg_ref[...])
        zu = _dot_nt(u, wu_ref[...])
        a_ref[...] = (zg * jax.nn.sigmoid(zg) * zu).astype(a_ref.dtype)
        zg_ref[...] = zg.astype(zg_ref.dtype)
        zu_ref[...] = zu.astype(zu_ref.dtype)

    o_spec = pl.BlockSpec((tm, tn), lambda n, m: (m, n))
    out = jax.ShapeDtypeStruct((T, F), BF16)
    outs, moved = _hosted_call(
        body, (F // tn, T // tm),
        [pl.BlockSpec((tm, D), lambda n, m: (m, 0)),
         pl.BlockSpec((tn, D), lambda n, m: (n, 0)),
         pl.BlockSpec((tn, D), lambda n, m: (n, 0))],
        [o_spec] * 3, [out] * 3, [], hosted, "ffn_up", (u2, w_gate_t, w_up_t))
    return (*outs, moved)


def _loss_head(a, w_down, h1, target, g_post):
    T, D = h1.shape
    F = a.shape[1]
    tr = 512

    def body(a_ref, w_ref, h1_ref, t_ref, g_ref, dffn_ref, dh2_ref, loss_ref, dg_ref):
        i = pl.program_id(0)
        f = _dot(a_ref[...], w_ref[...])
        g = g_ref[...]
        r = lax.rsqrt(jnp.mean(f * f, axis=-1, keepdims=True) + EPS)
        xh = f * r
        err = (h1_ref[...] + xh * g) - t_ref[...]
        part = 0.5 * jnp.sum(jnp.mean(err * err, axis=-1, keepdims=True), axis=0, keepdims=True)
        dh2 = err * (1.0 / D)
        dh2_ref[...] = dh2
        dgp = jnp.sum(dh2 * xh, axis=0, keepdims=True)
        dxh = dh2 * g
        dffn_ref[...] = (r * (dxh - xh * jnp.mean(dxh * xh, axis=-1, keepdims=True))).astype(dffn_ref.dtype)

        @pl.when(i == 0)
        def _():
            loss_ref[...] = jnp.zeros_like(loss_ref)
            dg_ref[...] = jnp.zeros_like(dg_ref)

        loss_ref[...] += jnp.broadcast_to(part, loss_ref.shape)
        dg_ref[...] += dgp

    row = pl.BlockSpec((tr, D), lambda i: (i, 0))
    vec = pl.BlockSpec((1, D), lambda i: (0, 0))
    return pl.pallas_call(
        body,
        grid=(T // tr,),
        in_specs=[pl.BlockSpec((tr, F), lambda i: (i, 0)), pl.BlockSpec((F, D), lambda i: (0, 0)), row, row, vec],
        out_specs=[row, row, pl.BlockSpec((8, 128), lambda i: (0, 0)), vec],
        out_shape=[jax.ShapeDtypeStruct((T, D), BF16), jax.ShapeDtypeStruct((T, D), F32),
                   jax.ShapeDtypeStruct((8, 128), F32), jax.ShapeDtypeStruct((1, D), F32)],
        compiler_params=_params(),
        name="loss_head",
    )(a, w_down, h1, target, g_post)


def _ffn_down_bwd(dffn, wd, zg, zu):
    def epilogue(da, rows, vecs, outs, accs, valid):
        zg = rows[0][...].astype(F32)
        zu = rows[1][...].astype(F32)
        s = jax.nn.sigmoid(zg)
        outs[1][...] = (da * (zg * s)).astype(BF16)
        outs[0][...] = (da * zu * (s * (1.0 + zg * (1.0 - s)))).astype(BF16)

    return _matmul_rows([(dffn, wd)], 512, [zg, zu], [], [BF16, BF16], 0, epilogue, "ffn_down_bwd", nt=True)


def _matmul_rows(pairs, tm, rows_in, vecs_in, rows_out, n_vec_out, epilogue, name, nt=False):
    a0, b0 = pairs[0]
    (M, K), N = a0.shape, b0.shape[0 if nt else 1]
    tm = min(tm, M)
    assert M % tm == 0, (name, M, tm)
    nm = M // tm
    npair = len(pairs)
    n_in = 2 * npair + len(rows_in) + len(vecs_in)
    n_out = len(rows_out) + n_vec_out

    def body(*refs):
        ab = refs[:2 * npair]
        r_in = refs[2 * npair:2 * npair + len(rows_in)]
        v_in = refs[2 * npair + len(rows_in):n_in]
        r_out = refs[n_in:n_in + len(rows_out)]
        v_out = refs[n_in + len(rows_out):n_in + n_out]
        last_ref, acc_ref = refs[n_in + n_out], refs[n_in + n_out + 1]
        m = pl.program_id(0)

        @pl.when(m == 0)
        def _():
            acc_ref[...] = jnp.zeros_like(acc_ref)
            for v in v_out:
                v[...] = jnp.zeros_like(v)

        prev = acc_ref[...]
        mm = _dot_nt if nt else _dot
        part = mm(ab[0][...], ab[1][...])
        for q in range(1, npair):
            part += mm(ab[2 * q][...], ab[2 * q + 1][...])
        acc_ref[...] = part
        last_ref[...] = part
        epilogue(prev, r_in, v_in, r_out, v_out, m > 0)

    def tail(*refs):
        last_ref = refs[0]
        r_in = refs[1:1 + len(rows_in)]
        v_in = refs[1 + len(rows_in):1 + len(rows_in) + len(vecs_in)]
        k0 = 1 + len(rows_in) + len(vecs_in)
        v_old = refs[k0 + len(rows_out):k0 + n_out]
        r_out = refs[k0 + n_out:k0 + n_out + len(rows_out)]
        v_out = refs[k0 + n_out + len(rows_out):]
        for old, new in zip(v_old, v_out):
            new[...] = old[...]
        epilogue(last_ref[...], r_in, v_in, r_out, v_out, True)

    prev_row = pl.BlockSpec((tm, N), lambda m: (jnp.maximum(m - 1, 0), 0))
    vec = pl.BlockSpec((1, N), lambda m: (0, 0))
    once = pl.BlockSpec((tm, N), lambda m: (0, 0))
    outs = pl.pallas_call(
        body,
        grid=(nm,),
        in_specs=[pl.BlockSpec((tm, K), lambda m: (m, 0)),
                  pl.BlockSpec((N, K) if nt else (K, N), lambda m: (0, 0), pipeline_mode=pl.Buffered(1))] * npair
        + [prev_row] * len(rows_in) + [vec] * len(vecs_in),
        out_specs=[prev_row] * len(rows_out) + [vec] * n_vec_out + [once],
        out_shape=[jax.ShapeDtypeStruct((M, N), dt) for dt in rows_out]
        + [jax.ShapeDtypeStruct((1, N), F32)] * n_vec_out + [jax.ShapeDtypeStruct((tm, N), F32)],
        scratch_shapes=[pltpu.VMEM((tm, N), F32)],
        compiler_params=_params(),
        name=name,
    )(*[t for pr in pairs for t in pr], *rows_in, *vecs_in)
    last_row = pl.BlockSpec((tm, N), lambda i: (nm - 1, 0))
    vec1 = pl.BlockSpec((1, N), lambda i: (0, 0))
    anyspec = pl.BlockSpec(memory_space=pl.ANY)
    return pl.pallas_call(
        tail,
        grid=(1,),
        in_specs=[pl.BlockSpec((tm, N), lambda i: (0, 0))] + [last_row] * len(rows_in) + [vec1] * len(vecs_in)
        + [anyspec] * len(rows_out) + [vec1] * n_vec_out,
        out_specs=[last_row] * len(rows_out) + [vec1] * n_vec_out,
        out_shape=[jax.ShapeDtypeStruct((M, N), dt) for dt in rows_out]
        + [jax.ShapeDtypeStruct((1, N), F32)] * n_vec_out,
        input_output_aliases={1 + len(rows_in) + len(vecs_in) + t: t for t in range(len(rows_out))},
        compiler_params=_params(),
        name=name + "_tail",
    )(outs[n_out], *rows_in, *vecs_in, *outs[:n_out])


def _mid_bwd(dzg, dzu, w_gate_t, w_up_t, h1, dh2, mix, g_pre, g_post):
    def epilogue(du2, rows, vecs, outs, accs, valid):
        h1_ref, dh2_ref, mix_ref = rows
        gq_ref, gp_ref = vecs
        dh1_ref, dmix_ref = outs
        keep = jnp.where(valid, 1.0, 0.0)
        dx, dgq = _rms_bwd_terms(h1_ref[...], gq_ref[...], du2)
        dh1 = dh2_ref[...] + dx
        dh1_ref[...] = dh1
        dmix, dgp = _rms_bwd_terms(mix_ref[...], gp_ref[...], dh1)
        dmix_ref[...] = dmix.astype(dmix_ref.dtype)
        accs[0][...] += dgq * keep
        accs[1][...] += dgp * keep

    return _matmul_rows([(dzg, w_gate_t), (dzu, w_up_t)], 512, [h1, dh2, mix], [g_pre, g_post],
                        [F32, BF16], 2, epilogue, "mid_bwd")


def _gate_bwd(dmix, w_out, ps, proj, wbs):
    T, D = dmix.shape
    tr, tc = min(512, T), GATE_TC
    nc = D // tc
    gl0 = COL_GL * 128 // tc

    def body(*refs):
        dmix_ref, wo_ref = refs[:2]
        p_refs = refs[2:5]
        g_refs = refs[5:5 + 3 * nc]
        w_refs = refs[5 + 3 * nc:8 + 3 * nc]
        outs = refs[8 + 3 * nc:]
        dp_refs, dg_refs, dy_refs = outs[:3], outs[3:6], outs[6:9]
        dm = _dot_nt(dmix_ref[...], wo_ref[...])
        for j in range(3):
            dps = []
            for c in range(nc):
                cols = slice(c * tc, (c + 1) * tc)
                s = jax.nn.sigmoid(g_refs[j * nc + c][...].astype(F32))
                dmc = dm[:, cols]
                dps.append((dmc * s).astype(BF16))
                dg_refs[j][:, cols] = (dmc * p_refs[j][:, cols].astype(F32) * (s * (1.0 - s))).astype(BF16)
            dp = jnp.concatenate(dps, axis=1)
            dp_refs[j][...] = dp
            dy_refs[j][...] = _dot_nt(dp, w_refs[j][...]).astype(BF16)

    row = pl.BlockSpec((tr, D), lambda i: (i, 0))
    y_spec = pl.BlockSpec((tr, 512), lambda i: (i, 0))
    w_spec = pl.BlockSpec((512, D), lambda i: (0, 0))
    gl_specs = [pl.BlockSpec((tr, tc), lambda i, b=gl0 + j * nc + c: (i, b)) for j in range(3) for c in range(nc)]
    big = jax.ShapeDtypeStruct((T, D), BF16)
    small = jax.ShapeDtypeStruct((T, 512), BF16)
    return pl.pallas_call(
        body,
        grid=(T // tr,),
        in_specs=[row, pl.BlockSpec((D, D), lambda i: (0, 0))] + [row] * 3 + gl_specs + [w_spec] * 3,
        out_specs=[row] * 6 + [y_spec] * 3,
        out_shape=[big] * 6 + [small] * 3,
        compiler_params=_params(),
        name="gate_bwd",
    )(dmix, w_out, *ps, *([proj] * (3 * nc)), *wbs)


def _pre_bwd(dproj, w_in_t, x, dh1, g):
    def epilogue(du, rows, vecs, outs, accs, valid):
        x_ref, dh1_ref = rows
        dx, dg = _rms_bwd_terms(x_ref[...], vecs[0][...], du)
        outs[0][...] = dh1_ref[...] + dx
        accs[0][...] += dg * jnp.where(valid, 1.0, 0.0)

    return _matmul_rows([(dproj, w_in_t)], 512, [x, dh1], [g], [F32], 1, epilogue, "pre_bwd")


def _gain_grad(xin, dy):
    T, D = xin.shape
    tr = min(512, T)

    def body(x_ref, dy_ref, dg_ref):
        i = pl.program_id(0)
        xf = x_ref[...]
        r = lax.rsqrt(jnp.mean(xf * xf, axis=-1, keepdims=True) + EPS)

        @pl.when(i == 0)
        def _():
            dg_ref[...] = jnp.zeros_like(dg_ref)

        dg_ref[...] += jnp.sum(dy_ref[...] * (xf * r), axis=0, keepdims=True)

    row = pl.BlockSpec((tr, D), lambda i: (i, 0))
    return pl.pallas_call(
        body,
        grid=(T // tr,),
        in_specs=[row, row],
        out_specs=pl.BlockSpec((1, D), lambda i: (0, 0)),
        out_shape=jax.ShapeDtypeStruct((1, D), F32),
        name="gain_grad",
    )(xin, dy)


def _swa_buckets():
    dist = (np.arange(BLOCK)[:, None] + BLOCK) - np.arange(2 * BLOCK)[None, :]
    max_exact = N_BUCKETS // 2
    d = np.maximum(dist, 0)
    df = np.maximum(d, 1).astype(np.float32)
    large = max_exact + (np.log(df / np.float32(max_exact)) / np.float32(math.log(MAX_DISTANCE / max_exact))
                         * np.float32(N_BUCKETS - max_exact)).astype(np.int32)
    large = np.minimum(large, N_BUCKETS - 1)
    bucket = np.where(d < max_exact, d, large)
    in_win = (dist >= 0) & (dist < SWA_WINDOW)
    return np.where(in_win, bucket, -1).astype(np.int32)


def _swa_bias_table(rel_bias, buckets):
    H = SWA_Q_HEADS

    def body(rb_ref, bk_ref, o_ref):
        bk = bk_ref[...]
        for h in range(H):
            acc = jnp.full(bk.shape, NEG, F32)
            for b in range(N_BUCKETS):
                acc = jnp.where(bk == b, rb_ref[b, h], acc)
            o_ref[h] = acc

    return pl.pallas_call(
        body,
        in_specs=[pl.BlockSpec(memory_space=pltpu.SMEM), pl.BlockSpec(memory_space=pltpu.VMEM)],
        out_specs=pl.BlockSpec(memory_space=pltpu.VMEM),
        out_shape=jax.ShapeDtypeStruct((H, BLOCK, 2 * BLOCK), F32),
        name="swa_bias_table",
    )(rel_bias, buckets)


def _swa_bias_grad(dbias, buckets):
    H = SWA_Q_HEADS

    def body(db_ref, bk_ref, o_ref):
        bk = bk_ref[...]
        rows = lax.broadcasted_iota(jnp.int32, (N_BUCKETS, 128), 0)
        lanes = lax.broadcasted_iota(jnp.int32, (N_BUCKETS, 128), 1)
        acc = jnp.zeros((N_BUCKETS, 128), F32)
        for h in range(H):
            d = db_ref[h]
            for b in range(N_BUCKETS):
                s = jnp.sum(jnp.sum(jnp.where(bk == b, d, 0.0), axis=1, keepdims=True), axis=0, keepdims=True)
                acc = jnp.where((rows == b) & (lanes == h), s, acc)
        o_ref[...] = acc

    return pl.pallas_call(
        body,
        in_specs=[pl.BlockSpec(memory_space=pltpu.VMEM)] * 2,
        out_specs=pl.BlockSpec(memory_space=pltpu.VMEM),
        out_shape=jax.ShapeDtypeStruct((N_BUCKETS, 128), F32),
        name="swa_bias_grad",
    )(dbias, buckets)


SWA_GROUP = 4
SWA_ROWS = SWA_Q_HEADS * BLOCK


def _swa_kv_lanes(h):
    lane = lax.broadcasted_iota(jnp.int32, (BLOCK, BLOCK), 1)
    return (lane // HEAD_DIM) == h // SWA_GROUP


def _swa_stack(x, heads):
    blocks = []
    for h in heads:
        xp = x[:, (h // 2) * BLOCK:(h // 2 + 1) * BLOCK]
        xs = xp if h % 2 == h // SWA_GROUP else pltpu.roll(xp, HEAD_DIM, 1)
        blocks.append(jnp.where(_swa_kv_lanes(h), xs, 0.0))
    return jnp.concatenate(blocks, axis=0).astype(BF16)


def _swa_unstack(blocks):
    pairs = []
    for p in range(4):
        halves = []
        for hh in range(2):
            h = 2 * p + hh
            blk = jnp.where(_swa_kv_lanes(h), blocks[h], 0.0)
            halves.append(blk if hh == h // SWA_GROUP else pltpu.roll(blk, HEAD_DIM, 1))
        pairs.append(halves[0] + halves[1])
    return jnp.concatenate(pairs, axis=1)


def _swa_stacked_params(sink_ref, bias_ref, heads):
    bias = jnp.concatenate([bias_ref[h] for h in heads], axis=0)
    sink = jnp.concatenate([jnp.full((BLOCK, 1), sink_ref[0, h], F32) for h in heads], axis=0)
    return bias, sink


def _swa_scores(qs, kp, kc, bias, sink, first):
    sp = _dot_nt(qs, kp) * SCALE64 + bias[:, :BLOCK]
    sp = jnp.where(first, NEG, sp)
    sc = _dot_nt(qs, kc) * SCALE64 + bias[:, BLOCK:]
    m = jnp.maximum(jnp.maximum(jnp.max(sp, axis=1, keepdims=True), jnp.max(sc, axis=1, keepdims=True)), sink)
    pp = jnp.exp(sp - m)
    pc = jnp.exp(sc - m)
    ps = jnp.exp(sink - m)
    den = jnp.sum(pp, axis=1, keepdims=True) + jnp.sum(pc, axis=1, keepdims=True) + ps
    return pp / den, pc / den, ps / den


def _swa_fwd(proj, sinks, bias_tab, B, S, hosted=None):
    nb = S // BLOCK
    T = B * S

    def body(sink_ref, q_ref, kp_ref, kc_ref, vp_ref, vc_ref, bias_ref, o_ref):
        i = pl.program_id(1)
        first = jnp.full((SWA_GROUP * BLOCK, BLOCK), i, jnp.int32) == 0
        q = q_ref[...].astype(F32)
        blocks = []
        for g in range(SWA_Q_HEADS // SWA_GROUP):
            heads = range(SWA_GROUP * g, SWA_GROUP * (g + 1))
            bias, sink = _swa_stacked_params(sink_ref, bias_ref, heads)
            wp, wc, _ = _swa_scores(_swa_stack(q, heads), kp_ref[...], kc_ref[...], bias, sink, first)
            o = _dot(wp.astype(BF16), vp_ref[...]) + _dot(wc.astype(BF16), vc_ref[...])
            blocks += [o[t * BLOCK:(t + 1) * BLOCK] for t in range(SWA_GROUP)]
        o_ref[...] = _swa_unstack(blocks).astype(o_ref.dtype)

    blk = (BLOCK, BLOCK)
    wide = (BLOCK, 4 * BLOCK)
    outs, moved = _hosted_call(
        body, (B, nb),
        [pl.BlockSpec(memory_space=pltpu.SMEM),
         pl.BlockSpec(wide, lambda b, i: (b * nb + i, COL_QA // 4)),
         pl.BlockSpec(blk, lambda b, i: (b * nb + jnp.maximum(i - 1, 0), COL_KA)),
         pl.BlockSpec(blk, lambda b, i: (b * nb + i, COL_KA)),
         pl.BlockSpec(blk, lambda b, i: (b * nb + jnp.maximum(i - 1, 0), COL_VA)),
         pl.BlockSpec(blk, lambda b, i: (b * nb + i, COL_VA)),
         pl.BlockSpec((SWA_Q_HEADS, BLOCK, 2 * BLOCK), lambda b, i: (0, 0, 0))],
        [pl.BlockSpec(wide, lambda b, i: (b * nb + i, 0))],
        [jax.ShapeDtypeStruct((T, 512), BF16)], [], hosted, "swa_fwd",
        (sinks, proj, proj, proj, proj, proj, bias_tab))
    return outs[0], moved


def _swa_bwd(proj, dy, sinks, bias_tab, B, S, hosted=None):
    nb = S // BLOCK
    T = B * S
    H = SWA_Q_HEADS

    def body(sink_ref, q_ref, kp_ref, kc_ref, vp_ref, vc_ref, do_ref, bias_ref,
             dq_ref, dk_ref, dv_ref, dbias_ref, dsink_ref):
        b = pl.program_id(0)
        i = pl.program_id(1)

        @pl.when((b == 0) & (i == 0))
        def _():
            dbias_ref[...] = jnp.zeros_like(dbias_ref)
            dsink_ref[...] = jnp.zeros_like(dsink_ref)

        @pl.when(i == 0)
        def _():
            dk_ref[...] = jnp.zeros_like(dk_ref)
            dv_ref[...] = jnp.zeros_like(dv_ref)

        first = jnp.full((SWA_ROWS, BLOCK), i, jnp.int32) == 0
        heads = range(H)
        bias, sink = _swa_stacked_params(sink_ref, bias_ref, heads)
        kp, kc, vp, vc = kp_ref[...], kc_ref[...], vp_ref[...], vc_ref[...]
        qs = _swa_stack(q_ref[...].astype(F32), heads)
        dos = _swa_stack(do_ref[...].astype(F32), heads)
        wp, wc, ws = _swa_scores(qs, kp, kc, bias, sink, first)
        dwp = _dot_nt(dos, vp)
        dwc = _dot_nt(dos, vc)
        dsum = jnp.sum(wp * dwp, axis=1, keepdims=True) + jnp.sum(wc * dwc, axis=1, keepdims=True)
        dsp = wp * (dwp - dsum)
        dsc = wc * (dwc - dsum)
        dsk = -ws * dsum
        dsinks = []
        for h in range(H):
            rows = slice(h * BLOCK, (h + 1) * BLOCK)
            dsinks.append(jnp.broadcast_to(jnp.sum(dsk[rows], axis=0, keepdims=True), (1, 128)))
            dbias_ref[h] += jnp.concatenate([dsp[rows], dsc[rows]], axis=1)
        dsink_ref[...] += jnp.concatenate(dsinks, axis=0)
        dspb = dsp.astype(BF16)
        dscb = dsc.astype(BF16)
        dq = _dot(dspb, kp) + _dot(dscb, kc)
        dq_ref[...] = (_swa_unstack([dq[h * BLOCK:(h + 1) * BLOCK] for h in heads]) * SCALE64).astype(dq_ref.dtype)
        cur = pl.ds(pl.multiple_of(i * BLOCK, BLOCK), BLOCK)
        prev = pl.ds(pl.multiple_of(jnp.maximum(i - 1, 0) * BLOCK, BLOCK), BLOCK)
        dk_ref[prev, :] += _dot_tn(dspb, qs) * SCALE64
        dk_ref[cur, :] += _dot_tn(dscb, qs) * SCALE64
        dv_ref[prev, :] += _dot_tn(wp.astype(BF16), dos)
        dv_ref[cur, :] += _dot_tn(wc.astype(BF16), dos)

    blk = (BLOCK, BLOCK)
    wide = (BLOCK, 4 * BLOCK)
    kv_out = pl.BlockSpec((S, BLOCK), lambda b, i: (b, 0))
    full_bias = pl.BlockSpec((H, BLOCK, 2 * BLOCK), lambda b, i: (0, 0, 0))
    outs, moved = _hosted_call(
        body, (B, nb),
        [pl.BlockSpec(memory_space=pltpu.SMEM),
         pl.BlockSpec(wide, lambda b, i: (b * nb + i, COL_QA // 4)),
         pl.BlockSpec(blk, lambda b, i: (b * nb + jnp.maximum(i - 1, 0), COL_KA)),
         pl.BlockSpec(blk, lambda b, i: (b * nb + i, COL_KA)),
         pl.BlockSpec(blk, lambda b, i: (b * nb + jnp.maximum(i - 1, 0), COL_VA)),
         pl.BlockSpec(blk, lambda b, i: (b * nb + i, COL_VA)),
         pl.BlockSpec(wide, lambda b, i: (b * nb + i, 0)),
         full_bias],
        [pl.BlockSpec(wide, lambda b, i: (b * nb + i, 0)),
         kv_out, kv_out, full_bias,
         pl.BlockSpec((H, 128), lambda b, i: (0, 0))],
        [jax.ShapeDtypeStruct((T, 512), BF16),
         jax.ShapeDtypeStruct((T, BLOCK), F32), jax.ShapeDtypeStruct((T, BLOCK), F32),
         jax.ShapeDtypeStruct((H, BLOCK, 2 * BLOCK), F32), jax.ShapeDtypeStruct((H, 128), F32)],
        [], hosted, "swa_bwd", (sinks, proj, proj, proj, proj, proj, dy, bias_tab))
    return (*outs, moved)


SB_TILE = 256


SB_HEADS = 4
SB_LANES = SB_HEADS * HEAD_DIM
SB_ROWS = SB_HEADS * SB_TILE


def _sb_logits(z, tri):
    sp = jnp.log(1.0 + jnp.exp(-jnp.abs(z)))
    ls = jnp.minimum(z, 0.0) - sp
    l1m = ls - z
    if tri is not None:
        l1m = jnp.where(tri, l1m, 0.0)
    return ls, l1m


def _sb_masks():
    lane = lax.broadcasted_iota(jnp.int32, (SB_TILE, SB_LANES), 1)
    hm = [(lane // HEAD_DIM) == h for h in range(SB_HEADS)]
    row = lax.broadcasted_iota(jnp.int32, (SB_ROWS, SB_TILE), 0) % SB_TILE
    col = lax.broadcasted_iota(jnp.int32, (SB_ROWS, SB_TILE), 1)
    return lane, hm, row, col


def _sb_stack(x, hm):
    return jnp.concatenate([jnp.where(m, x, 0) for m in hm], axis=0)


def _sb_unstack(x, hm):
    return sum(jnp.where(m, x[h * SB_TILE:(h + 1) * SB_TILE], 0.0) for h, m in enumerate(hm))


def _sb_fwd(proj, B, S, hosted=None):
    nt = S // SB_TILE
    T = B * S

    ng = 512 // SB_LANES

    def body(*refs):
        q_refs, k_refs, v_refs = refs[:ng], refs[ng:2 * ng], refs[2 * ng:3 * ng]
        o_ref, r_ref = refs[3 * ng:]
        i = pl.program_id(1)
        _, hm, row, col = _sb_masks()
        tri = col < row
        later = jnp.concatenate([(row[:SB_TILE] > col[:SB_TILE]).astype(BF16)] * 2, axis=0)
        qs = [_sb_stack(q_ref[...] * SCALE64, hm) for q_ref in q_refs]

        def tile(j, carry, mask):
            rows = pl.ds(pl.multiple_of(j * SB_TILE, SB_TILE), SB_TILE)
            out = []
            for g in range(ng):
                acc, c = carry[g]
                ls, l1m = _sb_logits(_dot_nt(qs[g], k_refs[g][rows, :]), mask)
                a = jnp.exp(ls + c + _dot_split(l1m, later))
                if mask is not None:
                    a = jnp.where(mask, a, 0.0)
                pv = _dot(a.astype(BF16), v_refs[g][rows, :])
                out.append((acc + _sb_unstack(pv, hm), c + jnp.sum(l1m, axis=1, keepdims=True)))
            return tuple(out)

        zero = (jnp.zeros((SB_TILE, SB_LANES), F32), jnp.zeros((SB_ROWS, 1), F32))
        carry = tile(i, (zero,) * ng, tri)
        carry = lax.fori_loop(0, i, lambda it, cr: tile(i - 1 - it, cr, None), carry)
        o_ref[...] = jnp.concatenate([acc for acc, _ in carry], axis=1).astype(o_ref.dtype)
        r_ref[...] = jnp.concatenate(
            [_sb_unstack(jnp.broadcast_to(c, (SB_ROWS, SB_LANES)), hm) for _, c in carry], axis=1)

    blk = (SB_TILE, SB_LANES)
    cq, ck, cv = (c * BLOCK // SB_LANES for c in (COL_QB, COL_KB, COL_VB))
    wide = pl.BlockSpec((SB_TILE, 512), lambda b, i: (b * nt + i, 0))
    outs, moved = _hosted_call(
        body, (B, nt),
        [pl.BlockSpec(blk, lambda b, i, g=g: (b * nt + i, cq + g)) for g in range(ng)]
        + [pl.BlockSpec((S, SB_LANES), lambda b, i, g=g: (b, ck + g)) for g in range(ng)]
        + [pl.BlockSpec((S, SB_LANES), lambda b, i, g=g: (b, cv + g)) for g in range(ng)],
        [wide, wide],
        [jax.ShapeDtypeStruct((T, 512), BF16), jax.ShapeDtypeStruct((T, 512), F32)], [], hosted, "sb_fwd",
        (proj,) * (3 * ng))
    return outs[0], outs[1], moved


def _sb_bwd(proj, dy, rtot, B, S, hosted=None):
    nt = S // SB_TILE
    T = B * S

    ng = 512 // SB_LANES

    def body(*refs):
        q_refs, k_refs, v_refs = refs[:ng], refs[ng:2 * ng], refs[2 * ng:3 * ng]
        do_ref, r_ref, dq_ref, dk_ref, dv_ref = refs[3 * ng:]
        i = pl.program_id(1)

        @pl.when(i == 0)
        def _():
            dk_ref[...] = jnp.zeros_like(dk_ref)
            dv_ref[...] = jnp.zeros_like(dv_ref)

        lane, hm, row, col = _sb_masks()
        tri = col < row
        later = jnp.concatenate([(row[:SB_TILE] > col[:SB_TILE]).astype(BF16)] * 2, axis=0)
        earlier = (row[:SB_TILE] < col[:SB_TILE]).astype(BF16)
        qs, dos, rs = [], [], []
        for g in range(ng):
            lanes = slice(g * SB_LANES, (g + 1) * SB_LANES)
            qs.append(_sb_stack(q_refs[g][...] * SCALE64, hm))
            dos.append(_sb_stack(do_ref[:, lanes], hm))
            r = r_ref[:, lanes]
            rs.append(jnp.concatenate([jnp.sum(jnp.where(lane == h * HEAD_DIM, r, 0.0), axis=1, keepdims=True)
                                       for h in range(SB_HEADS)], axis=0))

        def tile(j, carry, mask):
            rows = pl.ds(pl.multiple_of(j * SB_TILE, SB_TILE), SB_TILE)
            out = []
            for g in range(ng):
                lanes = slice(g * SB_LANES, (g + 1) * SB_LANES)
                dq, lsum, psum = carry[g]
                kj = k_refs[g][rows, :]
                vj = v_refs[g][rows, :]
                ls, l1m = _sb_logits(_dot_nt(qs[g], kj), mask)
                sig = jnp.exp(ls)
                lsum = lsum + jnp.sum(l1m, axis=1, keepdims=True)
                a = jnp.exp(ls + (rs[g] - lsum) + _dot_split(l1m, later))
                if mask is not None:
                    a = jnp.where(mask, a, 0.0)
                de = _dot_nt(dos[g], vj) * a
                pre = psum + _dot(de.astype(BF16), earlier)
                dz = de - sig * (de + pre)
                if mask is not None:
                    dz = jnp.where(mask, dz, 0.0)
                dz = dz.astype(BF16)
                dk_ref[rows, lanes] += _dot_tn(dz, qs[g])
                dv_ref[rows, lanes] += _dot_tn(a.astype(BF16), dos[g])
                out.append((dq + _sb_unstack(_dot(dz, kj), hm), lsum, psum + jnp.sum(de, axis=1, keepdims=True)))
            return tuple(out)

        zero = jnp.zeros((SB_ROWS, 1), F32)
        init = ((jnp.zeros((SB_TILE, SB_LANES), F32), zero, zero),) * ng
        carry = lax.fori_loop(0, i, lambda j, c: tile(j, c, None), init)
        carry = tile(i, carry, tri)
        dq_ref[...] = (jnp.concatenate([c[0] for c in carry], axis=1) * SCALE64).astype(dq_ref.dtype)

    blk = (SB_TILE, SB_LANES)
    cq, ck, cv = (c * BLOCK // SB_LANES for c in (COL_QB, COL_KB, COL_VB))
    wide = pl.BlockSpec((SB_TILE, 512), lambda b, i: (b * nt + i, 0))
    kv_out = pl.BlockSpec((S, 512), lambda b, i: (b, 0))
    outs, moved = _hosted_call(
        body, (B, nt),
        [pl.BlockSpec(blk, lambda b, i, g=g: (b * nt + i, cq + g)) for g in range(ng)]
        + [pl.BlockSpec((S, SB_LANES), lambda b, i, g=g: (b, ck + g)) for g in range(ng)]
        + [pl.BlockSpec((S, SB_LANES), lambda b, i, g=g: (b, cv + g)) for g in range(ng)]
        + [wide, wide],
        [wide, kv_out, kv_out],
        [jax.ShapeDtypeStruct((T, 512), BF16),
         jax.ShapeDtypeStruct((T, 512), F32), jax.ShapeDtypeStruct((T, 512), F32)], [], hosted, "sb_bwd",
        (proj,) * (3 * ng) + (dy, rtot))
    return outs[0], outs[1], outs[2], moved


def _mem_weights(q, mk):
    z = _dot_nt(q, mk) * SCALE128
    e = jnp.exp(z - jnp.max(z, axis=1, keepdims=True))
    return e / jnp.sum(e, axis=1, keepdims=True)


def _mem_fwd(proj, mkv, B, S, M):
    tq = 512
    nq = S // tq
    T = B * S
    Hm = MEM_HEADS

    def body(q0, q1, q2, q3, mk_ref, mv_ref, o_ref):
        outs = []
        for h, q_ref in enumerate((q0, q1, q2, q3)):
            cols = slice(h * 128, (h + 1) * 128)
            w = _mem_weights(q_ref[...], mk_ref[:, cols])
            outs.append(_dot(w.astype(BF16), mv_ref[:, cols]))
        o_ref[...] = jnp.concatenate(outs, axis=1).astype(o_ref.dtype)

    return pl.pallas_call(
        body,
        grid=(B, nq),
        in_specs=[pl.BlockSpec((tq, 128), lambda b, i, h=h: (b * nq + i, COL_QM + h)) for h in range(Hm)]
        + [pl.BlockSpec((M, 512), lambda b, i: (b, 0)), pl.BlockSpec((M, 512), lambda b, i: (b, 1))],
        out_specs=pl.BlockSpec((tq, 512), lambda b, i: (b * nq + i, 0)),
        out_shape=jax.ShapeDtypeStruct((T, 512), BF16),
        name="mem_fwd",
    )(proj, proj, proj, proj, mkv, mkv)


def _mem_bwd(proj, mkv, dy, B, S, M):
    tq = 512
    nq = S // tq
    T = B * S
    Hm = MEM_HEADS

    def body(q0, q1, q2, q3, mk_ref, mv_ref, do_ref, dq_ref, dmk_ref, dmv_ref):
        i = pl.program_id(1)
        dqs, dmks, dmvs = [], [], []
        for h, q_ref in enumerate((q0, q1, q2, q3)):
            cols = slice(h * 128, (h + 1) * 128)
            q = q_ref[...]
            do = do_ref[:, cols]
            mk = mk_ref[:, cols]
            w = _mem_weights(q, mk)
            dw = _dot_nt(do, mv_ref[:, cols])
            ds = (w * (dw - jnp.sum(w * dw, axis=1, keepdims=True))).astype(BF16)
            dqs.append(_dot(ds, mk))
            dmks.append(_dot_tn(ds, q))
            dmvs.append(_dot_tn(w.astype(BF16), do))
        dq_ref[...] = (jnp.concatenate(dqs, axis=1) * SCALE128).astype(dq_ref.dtype)

        @pl.when(i == 0)
        def _():
            dmk_ref[...] = jnp.zeros_like(dmk_ref)
            dmv_ref[...] = jnp.zeros_like(dmv_ref)

        dmk_ref[...] += jnp.concatenate(dmks, axis=1) * SCALE128
        dmv_ref[...] += jnp.concatenate(dmvs, axis=1)

    q_spec = pl.BlockSpec((tq, 512), lambda b, i: (b * nq + i, 0))
    m_out = pl.BlockSpec((M, 512), lambda b, i: (b, 0))
    return pl.pallas_call(
        body,
        grid=(B, nq),
        in_specs=[pl.BlockSpec((tq, 128), lambda b, i, h=h: (b * nq + i, COL_QM + h)) for h in range(Hm)]
        + [pl.BlockSpec((M, 512), lambda b, i: (b, 0)), pl.BlockSpec((M, 512), lambda b, i: (b, 1)), q_spec],
        out_specs=[q_spec, m_out, m_out],
        out_shape=[jax.ShapeDtypeStruct((T, 512), BF16),
                   jax.ShapeDtypeStruct((B * M, 512), F32), jax.ShapeDtypeStruct((B * M, 512), F32)],
        name="mem_bwd",
    )(proj, proj, proj, proj, mkv, mkv, dy)


def _all_gather(blk, name):
    R, C = blk.shape

    def body(x_ref, out_ref, send_sems, recv_sems, local_sem):
        x, y, c = _mesh_pos()
        me, sibling = (x, y, c), (x, y, 1 - c)
        chips = [(1 - x, y), (x, 1 - y), (1 - x, 1 - y)]

        def slot(px, py, pc):
            return out_ref.at[4 * px + 2 * py + pc]

        def copy(k, block, to, src=None):
            return pltpu.make_async_remote_copy(
                src_ref=slot(*block) if src is None else src, dst_ref=slot(*block),
                send_sem=send_sems.at[k], recv_sem=recv_sems.at[k],
                device_id=to, device_id_type=pl.DeviceIdType.MESH)

        mine = pltpu.make_async_copy(x_ref, slot(*me), local_sem)
        mine.start()
        first = [copy(0, me, sibling, src=x_ref)]
        first += [copy(1 + j, me, (*chip, c), src=x_ref) for j, chip in enumerate(chips)]
        for cp in first:
            cp.start()
        passed = [copy(4 + j, (*chip, c), sibling) for j, chip in enumerate(chips)]
        for j, chip in enumerate(chips):
            copy(1 + j, (*chip, c), me).wait_recv()
            passed[j].start()
        copy(0, sibling, me).wait_recv()
        for j, chip in enumerate(chips):
            copy(4 + j, (*chip, 1 - c), me).wait_recv()
        for cp in first + passed:
            cp.wait_send()
        mine.wait()

    return pl.pallas_call(
        body,
        in_specs=[pl.BlockSpec(memory_space=pl.ANY)],
        out_specs=pl.BlockSpec(memory_space=pl.ANY),
        out_shape=jax.ShapeDtypeStruct((N_DEV, R, C), blk.dtype),
        scratch_shapes=[pltpu.SemaphoreType.DMA((7,)), pltpu.SemaphoreType.DMA((7,)), pltpu.SemaphoreType.DMA],
        name=name,
    )(blk)


def _all_gather_relayed(blk, name):
    R, C = blk.shape
    R2 = R // 2

    def body(x_ref, out_ref, send_sems, recv_sems, local_sem):
        x, y, c = _mesh_pos()
        me, sib = (x, y, c), (x, y, 1 - c)
        nx, ny, dg = (1 - x, y, c), (x, 1 - y, c), (1 - x, 1 - y, c)

        def slot(p, half=None):
            ref = out_ref.at[4 * p[0] + 2 * p[1] + p[2]]
            return ref if half is None else ref.at[pl.ds(half * R2, R2), :]

        def copy(k, block, to, half=None, own=False):
            src = slot(block, half)
            if own:
                src = x_ref if half is None else x_ref.at[pl.ds(half * R2, R2), :]
            return pltpu.make_async_remote_copy(
                src_ref=src, dst_ref=slot(block, half), send_sem=send_sems.at[k], recv_sem=recv_sems.at[k],
                device_id=to, device_id_type=pl.DeviceIdType.MESH)

        def other(p):
            return (p[0], p[1], 1 - c)

        mine = pltpu.make_async_copy(x_ref, slot(me), local_sem)
        mine.start()
        sends = [copy(1, me, nx, 0, own=True), copy(3, me, ny, 1, own=True), copy(0, me, sib, own=True),
                 copy(2, me, nx, 1, own=True), copy(4, me, ny, 0, own=True)]
        for cp in sends:
            cp.start()
        copy(1, nx, me, 0).wait_recv()
        sends.append(copy(5, nx, ny, 0))
        sends[-1].start()
        copy(3, ny, me, 1).wait_recv()
        sends.append(copy(6, ny, nx, 1))
        sends[-1].start()
        copy(2, nx, me, 1).wait_recv()
        sends.append(copy(7, nx, sib))
        sends[-1].start()
        copy(4, ny, me, 0).wait_recv()
        sends.append(copy(8, ny, sib))
        sends[-1].start()
        copy(5, dg, me, 0).wait_recv()
        copy(6, dg, me, 1).wait_recv()
        sends.append(copy(9, dg, sib))
        sends[-1].start()
        copy(0, sib, me).wait_recv()
        for k, p in ((7, nx), (8, ny), (9, dg)):
            copy(k, other(p), me).wait_recv()
        for cp in sends:
            cp.wait_send()
        mine.wait()

    return pl.pallas_call(
        body,
        in_specs=[pl.BlockSpec(memory_space=pl.ANY)],
        out_specs=pl.BlockSpec(memory_space=pl.ANY),
        out_shape=jax.ShapeDtypeStruct((N_DEV, R, C), blk.dtype),
        scratch_shapes=[pltpu.SemaphoreType.DMA((10,)), pltpu.SemaphoreType.DMA((10,)), pltpu.SemaphoreType.DMA],
        name=name,
    )(blk)


_HBM = pl.BlockSpec(memory_space=pltpu.HBM)
_SEM = pl.BlockSpec(memory_space=pltpu.SEMAPHORE)


def _scatter_start(parts, land, window, carried, name):
    hosted = _Hosted(scatters=[parts], window=window)

    def body(p_ref, land_ref, c_ref, send_sems, recv_sems, local_sems, p_thru, land_thru, c_thru):
        for cp in hosted.copies([p_ref], [land_ref], send_sems, recv_sems, local_sems):
            cp.start()

    sems = (pltpu.SemaphoreType.DMA((7,)), pltpu.SemaphoreType.DMA((7,)), pltpu.SemaphoreType.DMA((1,)))
    hbm = lambda a: pltpu.HBM(a.shape, a.dtype)
    outs = pl.pallas_call(
        body, name=name,
        out_shape=sems + (hbm(parts), hbm(land), hbm(carried)),
        in_specs=(_HBM, _HBM, _HBM), out_specs=(_SEM, _SEM, _SEM, _HBM, _HBM, _HBM),
        input_output_aliases={0: 3, 1: 4, 2: 5},
        compiler_params=pltpu.CompilerParams(has_side_effects=pltpu.SideEffectType.DATAFLOW_SIDE_EFFECTING),
    )(pltpu.with_memory_space_constraint(parts, pltpu.HBM),
      pltpu.with_memory_space_constraint(land, pltpu.HBM),
      pltpu.with_memory_space_constraint(carried, pltpu.HBM))
    return (outs[:4], window), outs[4], outs[5]


def _scatter_wait(flight, land, after, name):
    (send_sems, recv_sems, local_sems, p_thru), window = flight
    hosted = _Hosted(scatters=[p_thru], window=window)

    def body(p_ref, land_ref, send, recv, local, after_ref, p_dead, got_ref):
        for cp in hosted.copies([p_ref], [land_ref], send, recv, local):
            cp.wait()

    hbm = lambda a: pltpu.HBM(a.shape, a.dtype)
    return pl.pallas_call(
        body, name=name,
        out_shape=(hbm(p_thru), hbm(land)),
        in_specs=(_HBM, _HBM, _SEM, _SEM, _SEM, pl.BlockSpec(memory_space=pl.ANY)), out_specs=(_HBM, _HBM),
        input_output_aliases={0: 0, 1: 1},
        compiler_params=pltpu.CompilerParams(has_side_effects=pltpu.SideEffectType.DATAFLOW_SIDE_EFFECTING),
    )(p_thru, land, send_sems, recv_sems, local_sems, after)[1]


def _adamw(parts, w, m, v, name):
    R, C = w.shape
    tr = R
    for cand in (368, 352, 256, 176, 128, 64, 32, 16, 8):
        if R % cand == 0 and cand * C * 4 <= 1536 * 1024:
            tr = cand
            break
    c1 = 1.0 - ADAM_B1 ** ADAM_STEP
    c2 = 1.0 - ADAM_B2 ** ADAM_STEP

    def body(p_ref, w_ref, m_ref, v_ref, g_ref, d_ref, nm_ref, nv_ref):
        g = p_ref[0].astype(F32)
        for d in range(1, N_DEV):
            g = g + p_ref[d].astype(F32)
        nm = ADAM_B1 * m_ref[...] + (1.0 - ADAM_B1) * g
        nv = ADAM_B2 * v_ref[...] + (1.0 - ADAM_B2) * (g * g)
        g_ref[...] = g
        nm_ref[...] = nm
        nv_ref[...] = nv
        d_ref[...] = -ADAM_LR * ((nm / c1) / (jnp.sqrt(nv / c2) + ADAM_EPS) + ADAM_WD * w_ref[...])

    row = pl.BlockSpec((tr, C), lambda i: (i, 0))
    out = jax.ShapeDtypeStruct((R, C), F32)
    return pl.pallas_call(
        body,
        grid=(R // tr,),
        in_specs=[pl.BlockSpec((N_DEV, tr, C), lambda i: (0, i, 0)), row, row, row],
        out_specs=[row] * 4,
        out_shape=[out] * 4,
        compiler_params=_params(),
        name=name,
    )(parts, w, m, v)


def _col_shards(g):
    R, C8 = g.shape
    return g.reshape(R, N_DEV, C8 // N_DEV).transpose(1, 0, 2)


def _row_shards(g):
    R8, C = g.shape
    return g.reshape(N_DEV, R8 // N_DEV, C)


def _cols_full(gathered):
    n, R, C = gathered.shape
    return gathered.transpose(1, 0, 2).reshape(R, n * C)


_BIG = ("w_in", "w_mem_kv", "w_branch_swa", "w_branch_sb", "w_branch_mem", "w_out", "w_gate", "w_up", "w_down")
_COL_SHARDED = ("w_branch_swa", "w_branch_sb", "w_branch_mem")
_TRANSPOSED = ("w_in", "w_gate", "w_up")
_SMALL = ("ln_mix_pre", "ln_mix_post", "swa_sinks", "rel_bias", "ln_mem", "ln_ffn_pre", "ln_ffn_post")
_ORDER = ("ln_mix_pre", "ln_mix_post", "w_in", "swa_sinks", "rel_bias", "ln_mem", "w_mem_kv", "w_branch_swa",
          "w_branch_sb", "w_branch_mem", "w_out", "ln_ffn_pre", "ln_ffn_post", "w_gate", "w_up", "w_down")


def _pack_small(d, last=None):
    rows = [d["ln_mix_pre"], d["ln_mix_post"], d["ln_mem"], d["ln_ffn_pre"], d["ln_ffn_post"],
            jnp.pad(d["swa_sinks"].reshape(1, -1), ((0, 0), (0, D_MODEL - SWA_Q_HEADS))),
            jnp.pad(d["rel_bias"].reshape(1, -1), ((0, 0), (0, D_MODEL - N_BUCKETS * SWA_Q_HEADS))),
            jnp.zeros((1, D_MODEL), F32) if last is None else last]
    return jnp.concatenate([r.astype(F32) for r in rows], axis=0)


def _unpack_small(a):
    return dict(ln_mix_pre=a[0:1], ln_mix_post=a[1:2], ln_mem=a[2:3], ln_ffn_pre=a[3:4], ln_ffn_post=a[4:5],
                swa_sinks=a[5:6, :SWA_Q_HEADS],
                rel_bias=a[6, :N_BUCKETS * SWA_Q_HEADS].reshape(N_BUCKETS, SWA_Q_HEADS))


def kernel(x, mem, ln_mix_pre, ln_mix_post, w_in, swa_sinks, rel_bias, ln_mem, w_mem_kv, w_branch_swa, w_branch_sb, w_branch_mem, w_out, ln_ffn_pre, ln_ffn_post, w_gate, w_up, w_down, loss_target, m_ln_mix_pre, m_ln_mix_post, m_w_in, m_swa_sinks, m_rel_bias, m_ln_mem, m_w_mem_kv, m_w_branch_swa, m_w_branch_sb, m_w_branch_mem, m_w_out, m_ln_ffn_pre, m_ln_ffn_post, m_w_gate, m_w_up, m_w_down, v_ln_mix_pre, v_ln_mix_post, v_w_in, v_swa_sinks, v_rel_bias, v_ln_mem, v_w_mem_kv, v_w_branch_swa, v_w_branch_sb, v_w_branch_mem, v_w_out, v_ln_ffn_pre, v_ln_ffn_post, v_w_gate, v_w_up, v_w_down):
    w = dict(ln_mix_pre=ln_mix_pre, ln_mix_post=ln_mix_post, w_in=w_in[0], swa_sinks=swa_sinks, rel_bias=rel_bias,
             ln_mem=ln_mem, w_mem_kv=w_mem_kv[0], w_branch_swa=w_branch_swa[0], w_branch_sb=w_branch_sb[0],
             w_branch_mem=w_branch_mem[0], w_out=w_out[0], ln_ffn_pre=ln_ffn_pre, ln_ffn_post=ln_ffn_post,
             w_gate=w_gate[0], w_up=w_up[0], w_down=w_down[0])
    mom = dict(ln_mix_pre=m_ln_mix_pre, ln_mix_post=m_ln_mix_post, w_in=m_w_in[0], swa_sinks=m_swa_sinks,
               rel_bias=m_rel_bias, ln_mem=m_ln_mem, w_mem_kv=m_w_mem_kv[0], w_branch_swa=m_w_branch_swa[0],
               w_branch_sb=m_w_branch_sb[0], w_branch_mem=m_w_branch_mem[0], w_out=m_w_out[0],
               ln_ffn_pre=m_ln_ffn_pre, ln_ffn_post=m_ln_ffn_post, w_gate=m_w_gate[0], w_up=m_w_up[0],
               w_down=m_w_down[0])
    var = dict(ln_mix_pre=v_ln_mix_pre, ln_mix_post=v_ln_mix_post, w_in=v_w_in[0], swa_sinks=v_swa_sinks,
               rel_bias=v_rel_bias, ln_mem=v_ln_mem, w_mem_kv=v_w_mem_kv[0], w_branch_swa=v_w_branch_swa[0],
               w_branch_sb=v_w_branch_sb[0], w_branch_mem=v_w_branch_mem[0], w_out=v_w_out[0],
               ln_ffn_pre=v_ln_ffn_pre, ln_ffn_post=v_ln_ffn_post, w_gate=v_w_gate[0], w_up=v_w_up[0],
               w_down=v_w_down[0])
    B, S, D = x.shape
    M = mem.shape[1]
    T = B * S
    F = D_FF
    x2 = x.reshape(T, D)
    mem2 = mem.reshape(B * M, D)
    t2 = loss_target.reshape(T, D)
    buckets = jnp.asarray(_swa_buckets())
    for d in (w, mom, var):
        for n in _TRANSPOSED:
            d[n] = d[n].T
    wb = {n: w[n].astype(BF16) for n in _BIG}
    full = {}

    def landed(names, got):
        for n, g in zip(names, got):
            full[n] = _cols_full(g) if n in _COL_SHARDED else g.reshape(-1, g.shape[-1])

    def shards(n, g):
        return _col_shards(g) if n in _COL_SHARDED else _row_shards(g)

    landed(["w_in"], [_all_gather_relayed(wb["w_in"], "ag_w_in")])
    u = _rms_fwd(x2, ln_mix_pre, "rms_mix_pre")
    early = ["w_mem_kv", "w_branch_swa", "w_branch_sb", "w_branch_mem"]
    proj, got = _matmul([(u, full["w_in"])], "nt", BF16, 512, IN_W // 2, D, "proj_in",
                        hosted=_Hosted(gathers=[wb[n] for n in early]))
    landed(early, got)
    mn = _rms_fwd(mem2, ln_mem, "rms_mem")
    mkv = _matmul([(mn, full["w_mem_kv"])], "nn", BF16, 512, 1024, D, "proj_mem")
    bias_tab = _swa_bias_table(rel_bias, buckets)
    y_swa, got = _swa_fwd(proj, swa_sinks, bias_tab, B, S, hosted=_Hosted(gathers=[wb["w_out"]]))
    landed(["w_out"], got)
    late = ["w_gate", "w_up"]
    y_sb, rtot, got = _sb_fwd(proj, B, S, hosted=_Hosted(gathers=[wb[n] for n in late]))
    landed(late, got)
    y_mem = _mem_fwd(proj, mkv, B, S, M)
    wbs = (full["w_branch_swa"], full["w_branch_sb"], full["w_branch_mem"])
    merged, p_swa, p_sb, p_mem = _branch_gate(proj, (y_swa, y_sb, y_mem), wbs)
    mix, h1, u2 = _post_pre(x2, merged, full["w_out"], ln_mix_post, ln_ffn_pre)
    a, zg, zu, got = _ffn_up(u2, full["w_gate"], full["w_up"], hosted=_Hosted(gathers=[wb["w_down"]]))
    landed(["w_down"], got)
    dffn, dh2, loss_tile, d_ln_ffn_post = _loss_head(a, full["w_down"], h1, t2, ln_ffn_post)

    part = {}
    part["w_down"] = _matmul([(a, dffn)], "tn", BF16, F // 2, 1024, 1024, "dw_down")
    dzg, dzu = _ffn_down_bwd(dffn, full["w_down"], zg, zu)
    part["w_gate"] = _matmul([(dzg, u2)], "tn", BF16, F // 2, 1024, 1024, "dw_gate")
    part["w_up"] = _matmul([(dzu, u2)], "tn", BF16, F // 2, 1024, 1024, "dw_up")
    dh1, dmix, d_ln_ffn_pre, d_ln_mix_post = _mid_bwd(dzg, dzu, full["w_gate"], full["w_up"], h1, dh2, mix,
                                                      ln_ffn_pre, ln_mix_post)
    part["w_out"] = _matmul([(merged, dmix)], "tn", BF16, 1024, 1024, 1024, "dw_out")
    (dp_swa, dp_sb, dp_mem, dg0, dg1, dg2, dy_swa, dy_sb, dy_mem) = _gate_bwd(
        dmix, full["w_out"], (p_swa, p_sb, p_mem), proj, wbs)
    part["w_branch_swa"] = _matmul([(y_swa, dp_swa)], "tn", BF16, 512, 1024, 1024, "dw_branch_swa")
    part["w_branch_sb"] = _matmul([(y_sb, dp_sb)], "tn", BF16, 512, 1024, 1024, "dw_branch_sb")
    part["w_branch_mem"] = _matmul([(y_mem, dp_mem)], "tn", BF16, 512, 1024, 1024, "dw_branch_mem")
    dqm, dmk, dmv = _mem_bwd(proj, mkv, dy_mem, B, S, M)
    dmkv = jnp.concatenate([dmk, dmv], axis=1).astype(BF16)
    part["w_mem_kv"] = _matmul([(mn, dmkv)], "tn", BF16, 1024, 1024, 512, "dw_mem_kv")
    dmn = _matmul([(dmkv, full["w_mem_kv"])], "nt", F32, 512, 1024, 1024, "d_mn")
    d_ln_mem = _gain_grad(mem2, dmn)
    behind_swa = ["w_out", "w_branch_swa", "w_branch_sb"]
    dqa, dka, dva, dbias, dsink, got = _swa_bwd(
        proj, dy_swa, swa_sinks, bias_tab, B, S, hosted=_Hosted(scatters=[shards(n, part[n]) for n in behind_swa]))
    recv = dict(zip(behind_swa, got))
    behind_sb = ["w_down", "w_gate", "w_up", "w_branch_mem", "w_mem_kv"]
    dqb, dkb, dvb, got = _sb_bwd(proj, dy_sb, rtot, B, S,
                                 hosted=_Hosted(scatters=[shards(n, part[n]) for n in behind_sb]))
    recv.update(zip(behind_sb, got))
    d_rel_bias = _swa_bias_grad(dbias, buckets)[:, :SWA_Q_HEADS]
    d_sinks = dsink[:, 0].reshape(1, SWA_Q_HEADS)
    dproj = jnp.concatenate([dqa, dka.astype(BF16), dva.astype(BF16), dqb, dkb.astype(BF16), dvb.astype(BF16),
                             dqm, dg0, dg1, dg2], axis=1)
    half = D // 2
    land = lax.empty((N_DEV, IN_W // N_DEV, D), BF16)
    flights = []
    for t in range(2):
        dw_half = _matmul([(dproj, u)], "tn", BF16, IN_W // 2, half, 1024, "dw_in_%d" % t, n_cols=half, n_off=t)
        flight, land, dproj = _scatter_start(_row_shards(dw_half), land, (t * half, half), dproj,
                                             "rs_w_in_start_%d" % t)
        flights.append(flight)
    grad_x, d_ln_mix_pre = _pre_bwd(dproj, full["w_in"], x2, dh1, ln_mix_pre)

    out = {n: _adamw(recv[n], w[n], mom[n], var[n], "adamw_" + n) for n in _BIG if n != "w_in"}
    small_grads = dict(ln_mix_pre=d_ln_mix_pre, ln_mix_post=d_ln_mix_post, swa_sinks=d_sinks, rel_bias=d_rel_bias,
                       ln_mem=d_ln_mem, ln_ffn_pre=d_ln_ffn_pre, ln_ffn_post=d_ln_ffn_post)
    small_parts = _all_gather(_pack_small(small_grads, jnp.tile(loss_tile[0:1], (1, D // 128))), "ag_small")
    res = _adamw(small_parts, _pack_small(w), _pack_small(mom), _pack_small(var), "adamw_small")
    loss = res[0][7, 0]
    small = [_unpack_small(r) for r in res]
    for n in _SMALL:
        out[n] = tuple(s[n] for s in small)
    for t, flight in enumerate(flights):
        land = _scatter_wait(flight, land, res[0], "rs_w_in_wait_%d" % t)
    out["w_in"] = _adamw(land, w["w_in"], mom["w_in"], var["w_in"], "adamw_w_in")
    for n in _TRANSPOSED:
        out[n] = tuple(o.T for o in out[n])

    like = dict(ln_mix_pre=ln_mix_pre, ln_mix_post=ln_mix_post, w_in=w_in, swa_sinks=swa_sinks, rel_bias=rel_bias,
                ln_mem=ln_mem, w_mem_kv=w_mem_kv, w_branch_swa=w_branch_swa, w_branch_sb=w_branch_sb,
                w_branch_mem=w_branch_mem, w_out=w_out, ln_ffn_pre=ln_ffn_pre, ln_ffn_post=ln_ffn_post,
                w_gate=w_gate, w_up=w_up, w_down=w_down)
    result = [loss, grad_x.reshape(B, S, D)]
    for k in range(4):
        result += [out[n][k].reshape(like[n].shape) for n in _ORDER]
    return tuple(result)
```
